```python
import jax, jax.numpy as jnp
from jax import lax
import numpy as np

D_MODEL = 1024
BATCH = 4
SEQ = 8192
DEPTH = 2

GRID_W = 64
CTX_LEN = 256
N_EVEN = (DEPTH + 1) // 2
N_ODD = DEPTH // 2
HEAD_DIM = 64
N_HEADS = (D_MODEL // 2) // HEAD_DIM
N_KV_HEADS = 2
GROUP = N_HEADS // N_KV_HEADS
WINDOW = 128
ATT_BLOCK = 128
ROPE_THETA = 10000.0
FOURIER_W = D_MODEL // 2
FOURIER_GROUPS = 4
FOURIER_GROUP_W = FOURIER_W // FOURIER_GROUPS
Q_W = N_HEADS * HEAD_DIM
KV_W = N_KV_HEADS * HEAD_DIM
EVEN_MIX_W = FOURIER_W + Q_W
EVEN_IN_W = EVEN_MIX_W + 2 * KV_W
CONV_W = 31
CONV_CH = D_MODEL
N_EXPERTS = 16
N_GROUPS = 4
EXPERTS_PER_GROUP = N_EXPERTS // N_GROUPS
TOP_K = 2
GROUP_SCORE_K = 2
EXPERT_FF = 512
MOE_BLOCK = 128
EPS = 1e-6
NEG_INF = -1e30

kernel_name = 'hybrid_fourier_swa_conformer_grouped_moe'

f32 = jnp.float32


def _rmsnorm(x, g):
    xf = x.astype(f32)
    y = xf * lax.rsqrt(jnp.mean(xf * xf, axis=-1, keepdims=True) + EPS)
    return (y * g.astype(f32)).astype(x.dtype)


def _layernorm(x, g, b):
    xf = x.astype(f32)
    mu = jnp.mean(xf, axis=-1, keepdims=True)
    var = jnp.mean(jnp.square(xf - mu), axis=-1, keepdims=True)
    return ((xf - mu) * lax.rsqrt(var + EPS) * g.astype(f32) + b.astype(f32)).astype(x.dtype)


def _adaln(cond, w, b):
    m = jax.nn.silu(cond) @ w + b
    parts = jnp.split(m, 6, axis=-1)
    if m.ndim == 2:
        parts = [p_[:, None, :] for p_ in parts]
    return parts


def _modulate(xn, shift, scale):
    return xn * (1 + scale) + shift


def _rope_half(xh, pos):
    half = xh.shape[-1] // 2
    inv = ROPE_THETA ** (-jnp.arange(half, dtype=f32) / half)
    ang = pos[:, None] * inv[None, :]
    cos = jnp.cos(ang)[None, :, None, :]
    sin = jnp.sin(ang)[None, :, None, :]
    x1 = xh[..., :half].astype(f32)
    x2 = xh[..., half:].astype(f32)
    return jnp.concatenate([x1 * cos - x2 * sin, x1 * sin + x2 * cos], axis=-1).astype(xh.dtype)


def _axial_rope(x, row, col):
    h = x.shape[-1] // 2
    return jnp.concatenate([_rope_half(x[..., :h], row), _rope_half(x[..., h:], col)], axis=-1)


def _fourier_mix(u):
    B, L, _ = u.shape
    ug = u.reshape(B, L, FOURIER_GROUPS, FOURIER_GROUP_W).astype(f32)
    y = jnp.fft.fftn(ug, axes=(1, 3), norm='ortho').real
    return y.reshape(B, L, FOURIER_W).astype(u.dtype)


def _sink_softmax(logits, sink):
    sk = jnp.broadcast_to(sink, logits.shape[:-1] + (1,))
    p = jax.nn.softmax(jnp.concatenate([logits, sk], axis=-1), axis=-1)
    return p[..., :-1]


def _window_attention(q, k, v, ck, cv, sink):
    B, L, _, _ = q.shape
    C = ck.shape[1]
    nb = L // ATT_BLOCK
    scale = HEAD_DIM ** -0.5
    qg = q.reshape(B, nb, ATT_BLOCK, N_KV_HEADS, GROUP, HEAD_DIM).transpose(1, 0, 2, 3, 4, 5)
    pad = ((0, 0), (ATT_BLOCK, ATT_BLOCK), (0, 0), (0, 0))
    kp = jnp.pad(k, pad)
    vp = jnp.pad(v, pad)
    idx = jnp.arange(nb)[:, None] * ATT_BLOCK + jnp.arange(3 * ATT_BLOCK)[None, :]
    kb = jnp.moveaxis(kp[:, idx], 1, 0)
    vb = jnp.moveaxis(vp[:, idx], 1, 0)
    sink_l = sink.astype(f32).reshape(N_KV_HEADS, GROUP)[None, :, :, None, None]
    offs = jnp.arange(3 * ATT_BLOCK)[None, :] - ATT_BLOCK - jnp.arange(ATT_BLOCK)[:, None]

    def block(args):
        qb, kbb, vbb, j = args
        kpos = j * ATT_BLOCK - ATT_BLOCK + jnp.arange(3 * ATT_BLOCK)
        valid = (jnp.abs(offs) <= WINDOW) & ((kpos >= 0) & (kpos < L))[None, :]
        s_c = jnp.einsum('bqkgd,bckd->bkgqc', qb, ck, preferred_element_type=f32) * scale
        s_l = jnp.einsum('bqkgd,bpkd->bkgqp', qb, kbb, preferred_element_type=f32) * scale
        s_l = jnp.where(valid, s_l, NEG_INF)
        p = _sink_softmax(jnp.concatenate([s_c, s_l], axis=-1), sink_l)
        p_c = p[..., :C].astype(vbb.dtype)
        p_l = p[..., C:].astype(vbb.dtype)
        return (jnp.einsum('bkgqc,bckd->bqkgd', p_c, cv)
                + jnp.einsum('bkgqp,bpkd->bqkgd', p_l, vbb))

    o = lax.map(block, (qg, kb, vb, jnp.arange(nb)))
    return o.transpose(1, 0, 2, 3, 4, 5).reshape(B, L, Q_W)


def _context_attention(qc, ck, cv, sink):
    B, C, _, _ = qc.shape
    qg = qc.reshape(B, C, N_KV_HEADS, GROUP, HEAD_DIM)
    s = jnp.einsum('bqkgd,bckd->bkgqc', qg, ck, preferred_element_type=f32) * (HEAD_DIM ** -0.5)
    p = _sink_softmax(s, sink.astype(f32).reshape(N_KV_HEADS, GROUP)[None, :, :, None, None])
    o = jnp.einsum('bkgqc,bckd->bqkgd', p.astype(cv.dtype), cv)
    return o.reshape(B, C, Q_W)


def _even_mixer(h, hc, w_in, w_out, sink, ctx_out):
    B, L, _ = h.shape
    p = h @ w_in
    u_f = p[..., :FOURIER_W]
    q = p[..., FOURIER_W:EVEN_MIX_W].reshape(B, L, N_HEADS, HEAD_DIM)
    k = p[..., EVEN_MIX_W:EVEN_MIX_W + KV_W].reshape(B, L, N_KV_HEADS, HEAD_DIM)
    v = p[..., EVEN_MIX_W + KV_W:].reshape(B, L, N_KV_HEADS, HEAD_DIM)
    pc = hc @ (w_in if ctx_out else w_in[:, EVEN_MIX_W:])
    C = hc.shape[1]
    ck = pc[..., -2 * KV_W:-KV_W].reshape(B, C, N_KV_HEADS, HEAD_DIM)
    cv = pc[..., -KV_W:].reshape(B, C, N_KV_HEADS, HEAD_DIM)
    n_rows = L // GRID_W
    row = jnp.repeat(jnp.arange(n_rows, dtype=f32), GRID_W)
    col = jnp.tile(jnp.arange(GRID_W, dtype=f32), n_rows)
    q = _axial_rope(q, row, col)
    k = _axial_rope(k, row, col)
    attn = _window_attention(q, k, v, ck, cv, sink)
    out = jnp.concatenate([_fourier_mix(u_f), attn], axis=-1) @ w_out
    out_c = None
    if ctx_out:
        qc = pc[..., FOURIER_W:EVEN_MIX_W].reshape(B, C, N_HEADS, HEAD_DIM)
        out_c = jnp.concatenate([_fourier_mix(pc[..., :FOURIER_W]),
                                 _context_attention(qc, ck, cv, sink)], axis=-1) @ w_out
    return out, out_c


def _conformer_conv(h, pw1_w, pw1_b, dw_w, dw_b, ln_g, ln_b, pw2_w, pw2_b):
    a, g = jnp.split(h @ pw1_w + pw1_b, 2, axis=-1)
    u = a * jax.nn.sigmoid(g)
    u = lax.conv_general_dilated(u, dw_w[:, None, :], window_strides=(1,),
                                 padding=[(CONV_W // 2, CONV_W // 2)],
                                 dimension_numbers=('NWC', 'WIO', 'NWC'),
                                 feature_group_count=u.shape[-1]) + dw_b
    u = jax.nn.silu(_layernorm(u, ln_g, ln_b))
    return u @ pw2_w + pw2_b


def _grouped_moe(h, router_w, router_b, w_gate, w_up, w_down):
    B, L, D = h.shape
    T = B * L
    t = h.reshape(T, D)
    scores = jax.nn.sigmoid(jnp.matmul(t.astype(f32), router_w.astype(f32)))
    grouped = (scores + router_b.astype(f32)).reshape(T, N_GROUPS, EXPERTS_PER_GROUP)
    group_score = lax.top_k(grouped, GROUP_SCORE_K)[0].sum(-1)
    g_sel = jnp.argmax(group_score, axis=-1)
    in_group = jnp.take_along_axis(grouped, g_sel[:, None, None], axis=1)[:, 0]
    _, local = lax.top_k(in_group, TOP_K)
    e_idx = g_sel[:, None] * EXPERTS_PER_GROUP + local
    w = jnp.take_along_axis(scores, e_idx, axis=1)
    w = w / jnp.sum(w, axis=-1, keepdims=True)
    A = T * TOP_K
    flat_e = e_idx.reshape(A).astype(jnp.int32)
    flat_tok = jnp.repeat(jnp.arange(T, dtype=jnp.int32), TOP_K)
    flat_w = w.reshape(A).astype(h.dtype)
    order = jnp.argsort(flat_e)
    sorted_e = flat_e[order]
    counts = jnp.bincount(flat_e, length=N_EXPERTS)
    starts = jnp.cumsum(counts) - counts
    padded = (counts + MOE_BLOCK - 1) // MOE_BLOCK * MOE_BLOCK
    pends = jnp.cumsum(padded)
    pstarts = pends - padded
    dest = pstarts[sorted_e] + jnp.arange(A) - starts[sorted_e]
    n_blocks = -(-A // MOE_BLOCK) + N_EXPERTS
    n_slots = n_blocks * MOE_BLOCK
    slot_tok = jnp.full((n_slots,), T, jnp.int32).at[dest].set(flat_tok[order])
    slot_w = jnp.zeros((n_slots,), h.dtype).at[dest].set(flat_w[order])
    block_e = jnp.minimum(jnp.searchsorted(pends, jnp.arange(n_blocks) * MOE_BLOCK, side='right'),
                          N_EXPERTS - 1)
    t_pad = jnp.concatenate([t, jnp.zeros((1, D), t.dtype)], axis=0)

    def expert_block(args):
        tok, e = args
        xb = t_pad[tok]
        hid = jax.nn.silu(xb @ w_gate[e]) * (xb @ w_up[e])
        return hid @ w_down[e]

    y_slots = lax.map(expert_block, (slot_tok.reshape(n_blocks, MOE_BLOCK), block_e))
    y = jnp.zeros((T + 1, D), h.dtype).at[slot_tok].add(y_slots.reshape(n_slots, D) * slot_w[:, None])
    return y[:T].reshape(B, L, D)


def setup_inputs(seed: int = 0) -> dict:
    key = jax.random.key(seed)
    ks = jax.random.split(key, 26)
    D = D_MODEL
    nrm = lambda k, shape, s: jax.random.normal(k, shape, f32) * s
    return {
        'x': nrm(ks[0], (BATCH, SEQ, D), 1.0),
        'c': nrm(ks[1], (BATCH, D), 1.0),
        'ctx': nrm(ks[2], (BATCH, CTX_LEN, D), 1.0),
        'c_ctx': nrm(ks[3], (D,), 1.0),
        'ada_w': nrm(ks[4], (DEPTH, D, 6 * D), 0.5 * D ** -0.5),
        'ada_b': nrm(ks[5], (DEPTH, 6 * D), 0.02),
        'norm_mix_g': 1.0 + nrm(ks[6], (DEPTH, D), 0.02),
        'norm_ffn_g': 1.0 + nrm(ks[7], (DEPTH, D), 0.02),
        'even_w_in': nrm(ks[8], (N_EVEN, D, EVEN_IN_W), D ** -0.5),
        'even_w_out': nrm(ks[9], (N_EVEN, EVEN_MIX_W, D), EVEN_MIX_W ** -0.5),
        'even_sink': nrm(ks[10], (N_EVEN, N_HEADS), 0.5),
        'conv_pw1_w': nrm(ks[11], (N_ODD, D, 2 * CONV_CH), D ** -0.5),
        'conv_pw1_b': nrm(ks[12], (N_ODD, 2 * CONV_CH), 0.02),
        'conv_dw_w': nrm(ks[13], (N_ODD, CONV_W, CONV_CH), CONV_W ** -0.5),
        'conv_dw_b': nrm(ks[14], (N_ODD, CONV_CH), 0.02),
        'conv_ln_g': 1.0 + nrm(ks[15], (N_ODD, CONV_CH), 0.02),
        'conv_ln_b': nrm(ks[16], (N_ODD, CONV_CH), 0.02),
        'conv_pw2_w': nrm(ks[17], (N_ODD, CONV_CH, D), CONV_CH ** -0.5),
        'conv_pw2_b': nrm(ks[18], (N_ODD, D), 0.02),
        'router_w': nrm(ks[19], (D, N_EXPERTS), D ** -0.5),
        'router_b': nrm(ks[20], (N_EXPERTS,), 0.01),
        'moe_w_gate': nrm(ks[21], (DEPTH, N_EXPERTS, D, EXPERT_FF), D ** -0.5),
        'moe_w_up': nrm(ks[22], (DEPTH, N_EXPERTS, D, EXPERT_FF), D ** -0.5),
        'moe_w_down': nrm(ks[23], (DEPTH, N_EXPERTS, EXPERT_FF, D), EXPERT_FF ** -0.5),
        'final_norm_g': 1.0 + nrm(ks[24], (D,), 0.02),
    }


def reference(x, c, ctx, c_ctx, ada_w, ada_b, norm_mix_g, norm_ffn_g, even_w_in, even_w_out,
              even_sink, conv_pw1_w, conv_pw1_b, conv_dw_w, conv_dw_b, conv_ln_g, conv_ln_b,
              conv_pw2_w, conv_pw2_b, router_w, router_b, moe_w_gate, moe_w_up, moe_w_down,
              final_norm_g):
    last_even = DEPTH - 1 if (DEPTH - 1) % 2 == 0 else DEPTH - 2
    for i in range(DEPTH):
        j = i // 2
        is_even = i % 2 == 0
        advance_ctx = i < last_even
        sh1, sc1, g1, sh2, sc2, g2 = _adaln(c, ada_w[i], ada_b[i])
        h = _modulate(_rmsnorm(x, norm_mix_g[i]), sh1, sc1)
        if is_even or advance_ctx:
            csh1, csc1, cg1, csh2, csc2, cg2 = _adaln(c_ctx, ada_w[i], ada_b[i])
            hc = _modulate(_rmsnorm(ctx, norm_mix_g[i]), csh1, csc1)
        if is_even:
            mix, mix_c = _even_mixer(h, hc, even_w_in[j], even_w_out[j], even_sink[j], advance_ctx)
        else:
            conv_p = (conv_pw1_w[j], conv_pw1_b[j], conv_dw_w[j], conv_dw_b[j],
                      conv_ln_g[j], conv_ln_b[j], conv_pw2_w[j], conv_pw2_b[j])
            mix = _conformer_conv(h, *conv_p)
            mix_c = _conformer_conv(hc, *conv_p) if advance_ctx else None
        x = x + g1 * mix
        f = _modulate(_rmsnorm(x, norm_ffn_g[i]), sh2, sc2)
        x = x + g2 * _grouped_moe(f, router_w, router_b, moe_w_gate[i], moe_w_up[i], moe_w_down[i])
        if advance_ctx:
            ctx = ctx + cg1 * mix_c
            fc = _modulate(_rmsnorm(ctx, norm_ffn_g[i]), csh2, csc2)
            ctx = ctx + cg2 * _grouped_moe(fc, router_w, router_b, moe_w_gate[i], moe_w_up[i], moe_w_down[i])
    return _rmsnorm(x, final_norm_g)
```

```python
import functools
import math

import jax
import jax.numpy as jnp
from jax import lax
from jax.experimental import pallas as pl
from jax.experimental.pallas import tpu as pltpu

f32 = jnp.float32
bf16 = jnp.bfloat16
i32 = jnp.int32
HIGHEST = lax.Precision.HIGHEST

GRID_W = 64
HEAD_DIM = 64
N_HEADS = 8
N_KV_HEADS = 2
WINDOW = 128
ATT_BLOCK = 128
ROPE_THETA = 10000.0
FOURIER_GROUPS = 4
FOURIER_GROUP_W = 128
CONV_W = 31
N_EXPERTS = 16
N_GROUPS = 4
EXPERTS_PER_GROUP = 4
EXPERT_FF = 512
EPS = 1e-6
NEG_INF = -1e30

LANES = 128
COND_ROWS = 8
DFT_INNER = 64
TOKEN_TILE = 512
ATT_TILE = 512
EXPERT_ROWS = 256
CONV_HALO = 16
VMEM_LIMIT = 56 * 1024 * 1024


def _cparams(sem, vmem=VMEM_LIMIT):
    return pltpu.CompilerParams(dimension_semantics=sem, vmem_limit_bytes=vmem)


def _adaln_kernel(cond_ref, w_ref, b_ref, o_ref):
    s = cond_ref[...]
    s = s * jax.nn.sigmoid(s)
    o_ref[0] = jnp.dot(s, w_ref[0], precision=HIGHEST, preferred_element_type=f32) + b_ref[0]


def _adaln(cond, ada_w, ada_b):
    depth, d, n = ada_w.shape
    tn = 1536
    return pl.pallas_call(
        _adaln_kernel,
        grid=(depth, n // tn),
        in_specs=[pl.BlockSpec((COND_ROWS, d), lambda i, j: (0, 0)),
                  pl.BlockSpec((1, d, tn), lambda i, j: (i, 0, j)),
                  pl.BlockSpec((1, 1, tn), lambda i, j: (i, 0, j))],
        out_specs=pl.BlockSpec((1, COND_ROWS, tn), lambda i, j: (i, 0, j)),
        out_shape=jax.ShapeDtypeStruct((depth, COND_ROWS, n), f32),
        compiler_params=_cparams(("arbitrary", "arbitrary")),
        name="adaln",
    )(cond, ada_w, ada_b.reshape(depth, 1, n))


def _norm_mod(x, g, shift, scale):
    ms = jnp.mean(x * x, axis=-1, keepdims=True)
    return (x * lax.rsqrt(ms + EPS) * g) * (1.0 + scale) + shift


def _rope(p, cos, sin_signed, first_half):
    rot = jnp.where(first_half, pltpu.roll(p, LANES - 16, axis=1), pltpu.roll(p, 16, axis=1))
    return p * cos + rot * sin_signed


def _inproj_kernel(x_ref, mod_ref, g_ref, w_ref, cs_ref, cos_ref, sin_ref,
                   fa_ref, fb_ref, q_ref, k_ref, v_ref):
    h = _norm_mod(x_ref[0], g_ref[...], mod_ref[0, 0:1, :], mod_ref[0, 1:2, :])
    p = jnp.dot(h.astype(bf16), w_ref[...], preferred_element_type=f32)
    fw = FOURIER_GROUPS * FOURIER_GROUP_W
    for g in range(FOURIER_GROUPS):
        ug = p[:, g * LANES:(g + 1) * LANES].astype(bf16)
        ab = jnp.dot(ug, cs_ref[...], preferred_element_type=f32)
        fa_ref[0, g] = ab[:, :LANES].astype(bf16)
        fb_ref[0, g] = ab[:, LANES:].astype(bf16)
    cos = cos_ref[...]
    sin = sin_ref[...]
    lane = lax.broadcasted_iota(i32, cos.shape, 1)
    first_half = (lane % 32) < 16
    qw = N_HEADS * HEAD_DIM
    for c in range(qw // LANES):
        qc = p[:, fw + c * LANES: fw + (c + 1) * LANES]
        q_ref[0, :, c * LANES:(c + 1) * LANES] = (
            _rope(qc, cos, sin, first_half) * (HEAD_DIM ** -0.5)).astype(bf16)
    k_ref[0] = _rope(p[:, fw + qw: fw + qw + LANES], cos, sin, first_half).astype(bf16)
    v_ref[0] = p[:, fw + qw + LANES:].astype(bf16)


def _inproj(x, mod, g, w, cs, cos_t, sin_t):
    b, l, d = x.shape
    tm = TOKEN_TILE
    n = w.shape[1]
    grp = pl.BlockSpec((1, FOURIER_GROUPS, tm, LANES), lambda i, j: (i, 0, j, 0))
    return pl.pallas_call(
        _inproj_kernel,
        grid=(b, l // tm),
        in_specs=[pl.BlockSpec((1, tm, d), lambda i, j: (i, j, 0)),
                  pl.BlockSpec((1, 6, d), lambda i, j: (i, 0, 0)),
                  pl.BlockSpec((1, d), lambda i, j: (0, 0)),
                  pl.BlockSpec((d, n), lambda i, j: (0, 0)),
                  pl.BlockSpec((LANES, 2 * LANES), lambda i, j: (0, 0)),
                  pl.BlockSpec((tm, LANES), lambda i, j: (j, 0)),
                  pl.BlockSpec((tm, LANES), lambda i, j: (j, 0))],
        out_specs=[grp, grp,
                   pl.BlockSpec((1, tm, N_HEADS * HEAD_DIM), lambda i, j: (i, j, 0)),
                   pl.BlockSpec((1, tm, LANES), lambda i, j: (i, j, 0)),
                   pl.BlockSpec((1, tm, LANES), lambda i, j: (i, j, 0))],
        out_shape=[jax.ShapeDtypeStruct((b, FOURIER_GROUPS, l, LANES), bf16),
                   jax.ShapeDtypeStruct((b, FOURIER_GROUPS, l, LANES), bf16),
                   jax.ShapeDtypeStruct((b, l, N_HEADS * HEAD_DIM), bf16),
                   jax.ShapeDtypeStruct((b, l, LANES), bf16),
                   jax.ShapeDtypeStruct((b, l, LANES), bf16)],
        compiler_params=_cparams(("parallel", "parallel")),
        name="inproj",
    )(x, mod, g, w, cs, cos_t, sin_t)


def _ctxkv_kernel(x_ref, mod_ref, g_ref, w_ref, k_ref, v_ref):
    h = _norm_mod(x_ref[0], g_ref[...], mod_ref[0, 0:1, :], mod_ref[0, 1:2, :])
    p = jnp.dot(h.astype(bf16), w_ref[...], preferred_element_type=f32)
    k_ref[0] = p[:, :LANES].astype(bf16)
    v_ref[0] = p[:, LANES:].astype(bf16)


def _ctxkv(ctx, mod, g, w_kv, ctx_row):
    b, c, d = ctx.shape
    return pl.pallas_call(
        _ctxkv_kernel,
        grid=(b,),
        in_specs=[pl.BlockSpec((1, c, d), lambda i: (i, 0, 0)),
                  pl.BlockSpec((1, 6, d), lambda i: (ctx_row, 0, 0)),
                  pl.BlockSpec((1, d), lambda i: (0, 0)),
                  pl.BlockSpec((d, 2 * LANES), lambda i: (0, 0))],
        out_specs=[pl.BlockSpec((1, c, LANES), lambda i: (i, 0, 0)),
                   pl.BlockSpec((1, c, LANES), lambda i: (i, 0, 0))],
        out_shape=[jax.ShapeDtypeStruct((b, c, LANES), bf16),
                   jax.ShapeDtypeStruct((b, c, LANES), bf16)],
        compiler_params=_cparams(("parallel",)),
        name="ctxkv",
    )(ctx, mod, g, w_kv)


def _attn_kernel(seq_len, q_ref, kp_ref, km_ref, kn_ref, vp_ref, vm_ref, vn_ref,
                 ck_ref, cv_ref, sink_ref, o_ref, kext, vext):
    j = pl.program_id(1)
    tq = ATT_TILE
    blk = ATT_BLOCK
    kext[0:blk] = kp_ref[0]
    kext[blk:blk + tq] = km_ref[0]
    kext[blk + tq:] = kn_ref[0]
    vext[0:blk] = vp_ref[0]
    vext[blk:blk + tq] = vm_ref[0]
    vext[blk + tq:] = vn_ref[0]
    ck = ck_ref[0]
    cv = cv_ref[0]
    sink = sink_ref[...]
    n_chunks = (N_HEADS * HEAD_DIM) // LANES
    rows = 2 * n_chunks * blk
    lane = lax.broadcasted_iota(i32, (blk, LANES), 1)
    low = lane < HEAD_DIM
    qi = lax.broadcasted_iota(i32, (rows, 3 * blk), 0) % blk
    pk = lax.broadcasted_iota(i32, (rows, 3 * blk), 1)
    band = jnp.abs(pk - blk - qi) <= WINDOW
    nt = (((1,), (1,)), ((), ()))

    def sub(s, carry):
        r0 = pl.multiple_of(s * blk, blk)
        qs = q_ref[0, pl.ds(r0, blk), :]
        parts = []
        for c in range(n_chunks):
            qc = qs[:, c * LANES:(c + 1) * LANES]
            parts.append(jnp.where(low, qc, jnp.zeros_like(qc)))
            parts.append(jnp.where(low, jnp.zeros_like(qc), qc))
        lhs = jnp.concatenate(parts, axis=0)
        kl = kext[pl.ds(r0, 3 * blk), :]
        vl = vext[pl.ds(r0, 3 * blk), :]
        s_c = lax.dot_general(lhs, ck, nt, preferred_element_type=f32)
        s_l = lax.dot_general(lhs, kl, nt, preferred_element_type=f32)
        kpos = j * tq + r0 - blk + pk
        valid = band & (kpos >= 0) & (kpos < seq_len)
        s_l = jnp.where(valid, s_l, NEG_INF)
        m = jnp.maximum(jnp.maximum(jnp.max(s_c, axis=1, keepdims=True),
                                    jnp.max(s_l, axis=1, keepdims=True)), sink)
        e_c = jnp.exp(s_c - m)
        e_l = jnp.exp(s_l - m)
        den = (jnp.sum(e_c, axis=1, keepdims=True) + jnp.sum(e_l, axis=1, keepdims=True)
               + jnp.exp(sink - m))
        o = (jnp.dot(e_c.astype(bf16), cv, preferred_element_type=f32)
             + jnp.dot(e_l.astype(bf16), vl, preferred_element_type=f32)) / den
        for c in range(n_chunks):
            oc = jnp.where(low, o[(2 * c) * blk:(2 * c + 1) * blk],
                           o[(2 * c + 1) * blk:(2 * c + 2) * blk])
            o_ref[0, pl.ds(r0, blk), c * LANES:(c + 1) * LANES] = oc.astype(bf16)
        return carry

    lax.fori_loop(0, tq // blk, sub, 0)


def _attention(q, k, v, ck, cv, sinkcol):
    b, l, qw = q.shape
    c = ck.shape[1]
    tq = ATT_TILE
    r = tq // ATT_BLOCK
    nb = l // ATT_BLOCK
    prev = pl.BlockSpec((1, ATT_BLOCK, LANES), lambda i, j: (i, jnp.maximum(j * r - 1, 0), 0))
    main = pl.BlockSpec((1, tq, LANES), lambda i, j: (i, j, 0))
    nxt = pl.BlockSpec((1, ATT_BLOCK, LANES), lambda i, j: (i, jnp.minimum(j * r + r, nb - 1), 0))
    cspec = pl.BlockSpec((1, c, LANES), lambda i, j: (i, 0, 0))
    return pl.pallas_call(
        functools.partial(_attn_kernel, l),
        grid=(b, l // tq),
        in_specs=[pl.BlockSpec((1, tq, qw), lambda i, j: (i, j, 0)),
                  prev, main, nxt, prev, main, nxt, cspec, cspec,
                  pl.BlockSpec(sinkcol.shape, lambda i, j: (0, 0))],
        out_specs=pl.BlockSpec((1, tq, qw), lambda i, j: (i, j, 0)),
        out_shape=jax.ShapeDtypeStruct((b, l, qw), bf16),
        scratch_shapes=[pltpu.VMEM((tq + 2 * ATT_BLOCK, LANES), bf16),
                        pltpu.VMEM((tq + 2 * ATT_BLOCK, LANES), bf16)],
        compiler_params=_cparams(("parallel", "parallel")),
        name="attention",
    )(q, k, k, k, v, v, v, ck, cv, sinkcol)


def _dft1_kernel(a_ref, b_ref, m_ref, ct_ref, st_ref, o_ref):
    n1 = a_ref.shape[2]
    for t in range(a_ref.shape[3]):
        ab = jnp.concatenate([a_ref[0, 0, :, t, :], b_ref[0, 0, :, t, :]], axis=0)
        z = jnp.dot(m_ref[...], ab, preferred_element_type=f32)
        zr = z[:n1]
        zn = z[n1:]
        ct = ct_ref[t]
        st = st_ref[t]
        o_ref[0, 0, 0, t] = (ct * zr - st * zn).astype(bf16)
        o_ref[0, 0, 1, t] = (ct * zn + st * zr).astype(bf16)


def _dft2_kernel(scale, z_ref, m_ref, o_ref):
    y = jnp.dot(m_ref[...], z_ref[0, 0], preferred_element_type=f32)
    o_ref[0, 0] = (y * scale).astype(bf16)


def _fourier(fa, fb):
    b, g, l, w = fa.shape
    n2 = DFT_INNER
    n1 = l // n2
    t2 = 8
    k1 = jnp.arange(n1, dtype=i32)
    ang1 = ((k1[:, None] * k1[None, :]) % n1).astype(f32) * (2.0 * math.pi / n1)
    c1, s1 = jnp.cos(ang1), jnp.sin(ang1)
    m1 = jnp.concatenate([jnp.concatenate([c1, -s1], axis=1),
                          jnp.concatenate([s1, c1], axis=1)], axis=0).astype(bf16)
    l2 = jnp.arange(n2, dtype=i32)
    angt = ((l2[:, None] * k1[None, :]) % l).astype(f32) * (2.0 * math.pi / l)
    ct = jnp.broadcast_to(jnp.cos(angt)[:, :, None], (n2, n1, w))
    st = jnp.broadcast_to(jnp.sin(angt)[:, :, None], (n2, n1, w))
    ang2 = ((l2[:, None] * l2[None, :]) % n2).astype(f32) * (2.0 * math.pi / n2)
    m2 = jnp.concatenate([jnp.cos(ang2), -jnp.sin(ang2)], axis=1).astype(bf16)

    a5 = fa.reshape(b, g, n1, n2, w)
    b5 = fb.reshape(b, g, n1, n2, w)
    dspec = pl.BlockSpec((1, 1, n1, t2, w), lambda t, i, j: (i, j, 0, t, 0))
    tspec = pl.BlockSpec((t2, n1, w), lambda t, i, j: (t, 0, 0))
    z = pl.pallas_call(
        _dft1_kernel,
        grid=(n2 // t2, b, g),
        in_specs=[dspec, dspec, pl.BlockSpec((2 * n1, 2 * n1), lambda t, i, j: (0, 0)), tspec, tspec],
        out_specs=pl.BlockSpec((1, 1, 2, t2, n1, w), lambda t, i, j: (i, j, 0, t, 0, 0)),
        out_shape=jax.ShapeDtypeStruct((b, g, 2, n2, n1, w), bf16),
        compiler_params=_cparams(("parallel", "parallel", "parallel")),
        name="dft_outer",
    )(a5, b5, m1, ct, st)

    z2 = z.reshape(b, g, 2 * n2, n1 * w)
    tn = min(n1 * w, 4096)
    y = pl.pallas_call(
        functools.partial(_dft2_kernel, 1.0 / math.sqrt(l * w)),
        grid=(b, g, (n1 * w) // tn),
        in_specs=[pl.BlockSpec((1, 1, 2 * n2, tn), lambda i, j, t: (i, j, 0, t)),
                  pl.BlockSpec((n2, 2 * n2), lambda i, j, t: (0, 0))],
        out_specs=pl.BlockSpec((1, 1, n2, tn), lambda i, j, t: (i, j, 0, t)),
        out_shape=jax.ShapeDtypeStruct((b, g, n2, n1 * w), bf16),
        compiler_params=_cparams(("parallel", "parallel", "parallel")),
        name="dft_inner",
    )(z2, m2)
    return y.reshape(b, g, l, w)


def _first_max4(a):
    m = jnp.maximum(jnp.maximum(a[0], a[1]), jnp.maximum(a[2], a[3]))
    idx = jnp.where(a[0] == m, 0, jnp.where(a[1] == m, 1, jnp.where(a[2] == m, 2, 3)))
    return m, idx


def _pick4(vals, idx):
    return jnp.where(idx == 0, vals[0], jnp.where(idx == 1, vals[1],
                                                   jnp.where(idx == 2, vals[2], vals[3])))


def _route(f, rw_ref, rb_ref, tri_ref, base_ref, first_step, ri_ref, wc_ref, cnt_ref):
    tm = f.shape[0]
    logits = jnp.dot(f, rw_ref[...], precision=HIGHEST, preferred_element_type=f32)
    sc = jax.nn.sigmoid(logits)
    st = sc.T
    bt = (sc + rb_ref[...]).T
    neg = jnp.full((1, tm), -jnp.inf, f32)
    gs = []
    for g in range(N_GROUPS):
        a = [bt[4 * g + i: 4 * g + i + 1] for i in range(4)]
        m1, i1 = _first_max4(a)
        rest = [jnp.where(i1 == i, neg, a[i]) for i in range(4)]
        m2, _ = _first_max4(rest)
        gs.append(m1 + m2)
    _, gsel = _first_max4(gs)
    a = [_pick4([bt[4 * g + i: 4 * g + i + 1] for g in range(N_GROUPS)], gsel) for i in range(4)]
    s = [_pick4([st[4 * g + i: 4 * g + i + 1] for g in range(N_GROUPS)], gsel) for i in range(4)]
    _, i1 = _first_max4(a)
    rest = [jnp.where(i1 == i, neg, a[i]) for i in range(4)]
    _, i2 = _first_max4(rest)
    w1 = _pick4(s, i1)
    w2 = _pick4(s, i2)
    tot = w1 + w2
    w1 = w1 / tot
    w2 = w2 / tot
    e0 = gsel * EXPERTS_PER_GROUP + i1
    e1 = gsel * EXPERTS_PER_GROUP + i2

    @pl.when(first_step)
    def _():
        base_ref[...] = jnp.zeros_like(base_ref)

    eid = lax.broadcasted_iota(i32, (N_EXPERTS, tm), 0)
    oh0 = (eid == e0).astype(f32)
    oh1 = (eid == e1).astype(f32)
    oh = oh0 + oh1
    before = jnp.dot(oh.astype(bf16), tri_ref[...], preferred_element_type=f32)
    pos = before + base_ref[:, 0:1]
    r0 = jnp.sum(oh0 * pos, axis=0, keepdims=True)
    r1 = jnp.sum(oh1 * pos, axis=0, keepdims=True)
    base_ref[...] = base_ref[...] + jnp.sum(oh, axis=1, keepdims=True)
    cnt_ref[...] = base_ref[...]
    zi = jnp.zeros((1, tm), i32)
    ri_ref[...] = jnp.concatenate(
        [e0, e1, r0.astype(i32), r1.astype(i32), zi, zi, zi, zi], axis=0)
    zf = jnp.zeros((LANES - 2, tm), f32)
    wc_ref[...] = jnp.concatenate([w1, w2, zf], axis=0).T


def _outproj_kernel(yf_ref, o_ref, x_ref, mod_ref, w_ref, g_ref, rw_ref, rb_ref, tri_ref,
                    x1_ref, f_ref, ri_ref, wc_ref, cnt_ref, base_ref):
    mix = jnp.concatenate([yf_ref[0, g] for g in range(FOURIER_GROUPS)] + [o_ref[0]], axis=1)
    y = jnp.dot(mix, w_ref[...], preferred_element_type=f32)
    x1 = x_ref[0] + mod_ref[0, 2:3, :] * y
    x1_ref[0] = x1
    f = _norm_mod(x1, g_ref[...], mod_ref[0, 3:4, :], mod_ref[0, 4:5, :])
    f_ref[0] = f
    first = (pl.program_id(0) == 0) & (pl.program_id(1) == 0)
    _route(f, rw_ref, rb_ref, tri_ref, base_ref, first, ri_ref, wc_ref, cnt_ref)


def _route_specs(b, l, tm):
    nl = l // tm
    rw = lambda d: pl.BlockSpec((d, LANES), lambda i, j: (0, 0))
    rb = pl.BlockSpec((1, LANES), lambda i, j: (0, 0))
    tri = pl.BlockSpec((tm, tm), lambda i, j: (0, 0))
    out_specs = [pl.BlockSpec((8, tm), lambda i, j: (0, i * nl + j)),
                 pl.BlockSpec((tm, LANES), lambda i, j: (i * nl + j, 0)),
                 pl.BlockSpec((N_EXPERTS, LANES), lambda i, j: (0, 0))]
    out_shape = [jax.ShapeDtypeStruct((8, b * l), i32),
                 jax.ShapeDtypeStruct((b * l, LANES), f32),
                 jax.ShapeDtypeStruct((N_EXPERTS, LANES), f32)]
    return rw, rb, tri, out_specs, out_shape


def _outproj(yf, o, x, mod, w, g, rw, rb, tri):
    b, l, d = x.shape
    tm = TOKEN_TILE
    rws, rbs, tris, r_specs, r_shapes = _route_specs(b, l, tm)
    row = pl.BlockSpec((1, tm, d), lambda i, j: (i, j, 0))
    return pl.pallas_call(
        _outproj_kernel,
        grid=(b, l // tm),
        in_specs=[pl.BlockSpec((1, FOURIER_GROUPS, tm, LANES), lambda i, j: (i, 0, j, 0)),
                  pl.BlockSpec((1, tm, o.shape[2]), lambda i, j: (i, j, 0)),
                  row,
                  pl.BlockSpec((1, 6, d), lambda i, j: (i, 0, 0)),
                  pl.BlockSpec(w.shape, lambda i, j: (0, 0)),
                  pl.BlockSpec((1, d), lambda i, j: (0, 0)),
                  rws(d), rbs, tris],
        out_specs=[row, row] + r_specs,
        out_shape=[jax.ShapeDtypeStruct((b, l, d), f32),
                   jax.ShapeDtypeStruct((b, l, d), f32)] + r_shapes,
        scratch_shapes=[pltpu.VMEM((N_EXPERTS, LANES), f32)],
        compiler_params=_cparams(("arbitrary", "arbitrary")),
        name="outproj_router",
    )(yf, o, x, mod, w, g, rw, rb, tri)


def _glu_kernel(x_ref, mod_ref, g_ref, w_ref, b_ref, u_ref):
    h = _norm_mod(x_ref[0], g_ref[...], mod_ref[0, 0:1, :], mod_ref[0, 1:2, :])
    p = jnp.dot(h.astype(bf16), w_ref[...], preferred_element_type=f32) + b_ref[...]
    ch = p.shape[1] // 2
    u_ref[0] = p[:, :ch] * jax.nn.sigmoid(p[:, ch:])


def _glu(x, mod, g, w, bias):
    b, l, d = x.shape
    tm = TOKEN_TILE
    n = w.shape[1]
    return pl.pallas_call(
        _glu_kernel,
        grid=(b, l // tm),
        in_specs=[pl.BlockSpec((1, tm, d), lambda i, j: (i, j, 0)),
                  pl.BlockSpec((1, 6, d), lambda i, j: (i, 0, 0)),
                  pl.BlockSpec((1, d), lambda i, j: (0, 0)),
                  pl.BlockSpec((d, n), lambda i, j: (0, 0)),
                  pl.BlockSpec((1, n), lambda i, j: (0, 0))],
        out_specs=pl.BlockSpec((1, tm, n // 2), lambda i, j: (i, j, 0)),
        out_shape=jax.ShapeDtypeStruct((b, l, n // 2), f32),
        compiler_params=_cparams(("parallel", "parallel")),
        name="pw1_glu",
    )(x, mod, g, w, bias)


def _conv_kernel(seq_len, up_ref, um_ref, un_ref, x_ref, mod_ref, dw_ref, db_ref, lg_ref, lb_ref,
                 w_ref, pb_ref, g_ref, rw_ref, rb_ref, tri_ref,
                 x1_ref, f_ref, ri_ref, wc_ref, cnt_ref, base_ref, ext):
    j = pl.program_id(1)
    tm = um_ref.shape[1]
    hl = CONV_HALO
    half = CONV_W // 2
    ext[0:hl] = jnp.where(j > 0, up_ref[0], jnp.zeros_like(up_ref[0]))
    ext[hl:hl + tm] = um_ref[0]
    ext[hl + tm:] = jnp.where((j + 1) * tm < seq_len, un_ref[0], jnp.zeros_like(un_ref[0]))
    acc = jnp.zeros(um_ref.shape[1:], f32) + db_ref[...]
    for t in range(CONV_W):
        acc = acc + ext[hl - half + t: hl - half + t + tm, :] * dw_ref[t:t + 1, :]
    mu = jnp.mean(acc, axis=-1, keepdims=True)
    cen = acc - mu
    var = jnp.mean(cen * cen, axis=-1, keepdims=True)
    ln = cen * lax.rsqrt(var + EPS) * lg_ref[...] + lb_ref[...]
    act = ln * jax.nn.sigmoid(ln)
    y = jnp.dot(act.astype(bf16), w_ref[...], preferred_element_type=f32) + pb_ref[...]
    x1 = x_ref[0] + mod_ref[0, 2:3, :] * y
    x1_ref[0] = x1
    f = _norm_mod(x1, g_ref[...], mod_ref[0, 3:4, :], mod_ref[0, 4:5, :])
    f_ref[0] = f
    first = (pl.program_id(0) == 0) & (j == 0)
    _route(f, rw_ref, rb_ref, tri_ref, base_ref, first, ri_ref, wc_ref, cnt_ref)


def _conv(u, x, mod, dw_w, dw_b, ln_g, ln_b, pw2_w, pw2_b, g, rw, rb, tri):
    b, l, d = x.shape
    tm = TOKEN_TILE
    hl = CONV_HALO
    r = tm // hl
    nh = l // hl
    rws, rbs, tris, r_specs, r_shapes = _route_specs(b, l, tm)
    row = pl.BlockSpec((1, tm, d), lambda i, j: (i, j, 0))
    vec = pl.BlockSpec((1, d), lambda i, j: (0, 0))
    return pl.pallas_call(
        functools.partial(_conv_kernel, l),
        grid=(b, l // tm),
        in_specs=[pl.BlockSpec((1, hl, d), lambda i, j: (i, jnp.maximum(j * r - 1, 0), 0)),
                  row,
                  pl.BlockSpec((1, hl, d), lambda i, j: (i, jnp.minimum(j * r + r, nh - 1), 0)),
                  row,
                  pl.BlockSpec((1, 6, d), lambda i, j: (i, 0, 0)),
                  pl.BlockSpec(dw_w.shape, lambda i, j: (0, 0)),
                  vec, vec, vec,
                  pl.BlockSpec(pw2_w.shape, lambda i, j: (0, 0)),
                  vec, vec, rws(d), rbs, tris],
        out_specs=[row, row] + r_specs,
        out_shape=[jax.ShapeDtypeStruct((b, l, d), f32),
                   jax.ShapeDtypeStruct((b, l, d), f32)] + r_shapes,
        scratch_shapes=[pltpu.VMEM((N_EXPERTS, LANES), f32),
                        pltpu.VMEM((tm + 2 * hl, d), f32)],
        compiler_params=_cparams(("arbitrary", "arbitrary")),
        name="conv_router",
    )(u, u, u, x, mod, dw_w, dw_b, ln_g, ln_b, pw2_w, pw2_b, g, rw, rb, tri)


def _dispatch_kernel(dest_ref, f_ref, zero_ref, xs_ref, sem):
    del zero_ref
    tm = f_ref.shape[0]

    def issue(t, c):
        src = f_ref.at[pl.ds(t, 1)]
        pltpu.make_async_copy(src, xs_ref.at[pl.ds(dest_ref[t], 1)], sem).start()
        pltpu.make_async_copy(src, xs_ref.at[pl.ds(dest_ref[tm + t], 1)], sem).start()
        return c

    lax.fori_loop(0, tm, issue, 0)
    done = pltpu.make_async_copy(f_ref, xs_ref.at[pl.ds(0, tm)], sem)
    done.wait()
    done.wait()


def _dispatch(dest_flat, f2, n_slots):
    t, d = f2.shape
    tm = TOKEN_TILE
    zeros = jnp.zeros((n_slots, d), f32)
    return pl.pallas_call(
        _dispatch_kernel,
        grid=(t // tm,),
        in_specs=[pl.BlockSpec((2 * tm,), lambda i: (i,), memory_space=pltpu.SMEM),
                  pl.BlockSpec((tm, d), lambda i: (i, 0)),
                  pl.BlockSpec(memory_space=pl.ANY)],
        out_specs=pl.BlockSpec(memory_space=pl.ANY),
        out_shape=jax.ShapeDtypeStruct((n_slots, d), f32),
        scratch_shapes=[pltpu.SemaphoreType.DMA(())],
        input_output_aliases={2: 0},
        compiler_params=_cparams(("arbitrary",)),
        name="moe_dispatch",
    )(dest_flat, f2, zeros)


def _expert_kernel(be_ref, nv_ref, x_ref, wg_ref, wu_ref, wd_ref, y_ref, wgb, wub, wdb):
    i = pl.program_id(0)
    changed = jnp.logical_or(i == 0, be_ref[i] != be_ref[jnp.maximum(i - 1, 0)])

    @pl.when(changed)
    def _():
        wgb[...] = wg_ref[0].astype(bf16)
        wub[...] = wu_ref[0].astype(bf16)
        wdb[...] = wd_ref[0].astype(bf16)

    @pl.when(i < nv_ref[0])
    def _():
        xb = x_ref[...].astype(bf16)
        gate = jnp.dot(xb, wgb[...], preferred_element_type=f32)
        up = jnp.dot(xb, wub[...], preferred_element_type=f32)
        hid = (gate * jax.nn.sigmoid(gate) * up).astype(bf16)
        y_ref[...] = jnp.dot(hid, wdb[...], preferred_element_type=f32)

    @pl.when(i >= nv_ref[0])
    def _():
        y_ref[...] = jnp.zeros_like(y_ref)


def _experts(block_e, n_valid, xs, w_gate, w_up, w_down):
    ns, d = xs.shape
    tb = EXPERT_ROWS
    ff = w_gate.shape[2]
    return pl.pallas_call(
        _expert_kernel,
        grid_spec=pltpu.PrefetchScalarGridSpec(
            num_scalar_prefetch=2,
            grid=(ns // tb,),
            in_specs=[pl.BlockSpec((tb, d), lambda i, be, nv: (i, 0)),
                      pl.BlockSpec((1, d, ff), lambda i, be, nv: (be[i], 0, 0)),
                      pl.BlockSpec((1, d, ff), lambda i, be, nv: (be[i], 0, 0)),
                      pl.BlockSpec((1, ff, d), lambda i, be, nv: (be[i], 0, 0))],
            out_specs=pl.BlockSpec((tb, d), lambda i, be, nv: (i, 0)),
            scratch_shapes=[pltpu.VMEM((d, ff), bf16), pltpu.VMEM((d, ff), bf16),
                            pltpu.VMEM((ff, d), bf16)]),
        out_shape=jax.ShapeDtypeStruct((ns, d), f32),
        compiler_params=_cparams(("arbitrary",)),
        name="moe_experts",
    )(block_e, n_valid, xs, w_gate, w_up, w_down)


def _combine_kernel(final, dest_ref, ys_ref, wc_ref, x_ref, mod_ref, g_ref, o_ref, buf0, buf1, sem):
    tm = x_ref.shape[1]

    def issue(t, c):
        pltpu.make_async_copy(ys_ref.at[pl.ds(dest_ref[t], 1)], buf0.at[pl.ds(t, 1)], sem).start()
        pltpu.make_async_copy(ys_ref.at[pl.ds(dest_ref[tm + t], 1)], buf1.at[pl.ds(t, 1)], sem).start()
        return c

    lax.fori_loop(0, tm, issue, 0)
    pltpu.make_async_copy(ys_ref.at[pl.ds(0, tm)], buf0, sem).wait()
    pltpu.make_async_copy(ys_ref.at[pl.ds(0, tm)], buf1, sem).wait()
    wc = wc_ref[...]
    y = wc[:, 0:1] * buf0[...] + wc[:, 1:2] * buf1[...]
    xo = x_ref[0] + mod_ref[0, 5:6, :] * y
    if final:
        ms = jnp.mean(xo * xo, axis=-1, keepdims=True)
        xo = xo * lax.rsqrt(ms + EPS) * g_ref[...]
    o_ref[0] = xo


def _combine(dest_flat, ys, wc, x, mod, g, final):
    b, l, d = x.shape
    tm = TOKEN_TILE
    nl = l // tm
    return pl.pallas_call(
        functools.partial(_combine_kernel, final),
        grid=(b, nl),
        in_specs=[pl.BlockSpec((2 * tm,), lambda i, j: (i * nl + j,), memory_space=pltpu.SMEM),
                  pl.BlockSpec(memory_space=pl.ANY),
                  pl.BlockSpec((tm, LANES), lambda i, j: (i * nl + j, 0)),
                  pl.BlockSpec((1, tm, d), lambda i, j: (i, j, 0)),
                  pl.BlockSpec((1, 6, d), lambda i, j: (i, 0, 0)),
                  pl.BlockSpec((1, d), lambda i, j: (0, 0))],
        out_specs=pl.BlockSpec((1, tm, d), lambda i, j: (i, j, 0)),
        out_shape=jax.ShapeDtypeStruct((b, l, d), f32),
        scratch_shapes=[pltpu.VMEM((tm, d), f32), pltpu.VMEM((tm, d), f32),
                        pltpu.SemaphoreType.DMA(())],
        compiler_params=_cparams(("arbitrary", "arbitrary")),
        name="moe_combine",
    )(dest_flat, ys, wc, x, mod, g)


def _moe(f, ri, wc, cnt, x, mod, g_final, w_gate, w_up, w_down, final):
    b, l, d = x.shape
    t = b * l
    tm = TOKEN_TILE
    tb = EXPERT_ROWS
    counts = cnt[:, 0].astype(i32)
    padded = (counts + tb - 1) // tb * tb
    pends = jnp.cumsum(padded)
    pstarts = pends - padded
    n_blocks = -(-(2 * t) // tb) + N_EXPERTS
    n_slots = n_blocks * tb
    dest = jnp.stack([pstarts[ri[0]] + ri[2], pstarts[ri[1]] + ri[3]], axis=0)
    dest_flat = dest.reshape(2, t // tm, tm).transpose(1, 0, 2).reshape(-1)
    block_e = jnp.minimum(jnp.searchsorted(pends, jnp.arange(n_blocks, dtype=i32) * tb, side='right'),
                          N_EXPERTS - 1).astype(i32)
    n_valid = (pends[-1] // tb).astype(i32).reshape(1)
    xs = _dispatch(dest_flat, f.reshape(t, d), n_slots)
    ys = _experts(block_e, n_valid, xs, w_gate, w_up, w_down)
    return _combine(dest_flat, ys, wc, x, mod, g_final, final)


def _rope_tables(l):
    lane = jnp.arange(LANES)
    dh = lane % HEAD_DIM
    inv = ROPE_THETA ** (-(dh % 16).astype(f32) / 16.0)
    pos = jnp.arange(l)
    row = (pos // GRID_W).astype(f32)
    col = (pos % GRID_W).astype(f32)
    p = jnp.where((dh // 32)[None, :] == 0, row[:, None], col[:, None])
    ang = p * inv[None, :]
    sign = jnp.where((dh % 32) < 16, -1.0, 1.0).astype(f32)
    return jnp.cos(ang), jnp.sin(ang) * sign[None, :]


def kernel(x, c, ctx, c_ctx, ada_w, ada_b, norm_mix_g, norm_ffn_g, even_w_in, even_w_out, even_sink, conv_pw1_w, conv_pw1_b, conv_dw_w, conv_dw_b, conv_ln_g, conv_ln_b, conv_pw2_w, conv_pw2_b, router_w, router_b, moe_w_gate, moe_w_up, moe_w_down, final_norm_g):
    b, l, d = x.shape
    depth = ada_w.shape[0]
    assert depth == 2 and b < COND_ROWS
    ctx_row = b
    cond = jnp.zeros((COND_ROWS, d), f32).at[:b].set(c).at[ctx_row].set(c_ctx)
    mods = _adaln(cond, ada_w, ada_b).reshape(depth, COND_ROWS, 6, d)

    heads = jnp.arange(N_HEADS).reshape(N_KV_HEADS, N_HEADS // N_KV_HEADS).T.reshape(-1)
    qperm = (heads[:, None] * HEAD_DIM + jnp.arange(HEAD_DIM)[None, :]).reshape(-1)
    fw = FOURIER_GROUPS * FOURIER_GROUP_W
    qw = N_HEADS * HEAD_DIM
    w_in = even_w_in[0]
    w_in_p = jnp.concatenate([w_in[:, :fw], w_in[:, fw:fw + qw][:, qperm], w_in[:, fw + qw:]],
                             axis=1).astype(bf16)
    w_out = even_w_out[0]
    w_out_p = jnp.concatenate([w_out[:fw], w_out[fw:][qperm]], axis=0).astype(bf16)
    sinkcol = jnp.repeat(even_sink[0][heads], ATT_BLOCK).reshape(-1, 1).astype(f32)

    cidx = jnp.arange(FOURIER_GROUP_W, dtype=i32)
    angc = ((cidx[:, None] * cidx[None, :]) % FOURIER_GROUP_W).astype(f32) * (2.0 * math.pi / FOURIER_GROUP_W)
    cs = jnp.concatenate([jnp.cos(angc), jnp.sin(angc)], axis=1).astype(bf16)
    cos_t, sin_t = _rope_tables(l)

    rw = jnp.zeros((d, LANES), f32).at[:, :N_EXPERTS].set(router_w)
    rb = jnp.zeros((1, LANES), f32).at[0, :N_EXPERTS].set(router_b)
    tri = jnp.triu(jnp.ones((TOKEN_TILE, TOKEN_TILE), f32), k=1).astype(bf16)
    row = lambda v: v.reshape(1, -1)

    fa, fb, q, k, v = _inproj(x, mods[0], row(norm_mix_g[0]), w_in_p, cs, cos_t, sin_t)
    ck, cv = _ctxkv(ctx, mods[0], row(norm_mix_g[0]), w_in_p[:, fw + qw:], ctx_row)
    yf = _fourier(fa, fb)
    att = _attention(q, k, v, ck, cv, sinkcol)
    x1, f, ri, wc, cnt = _outproj(yf, att, x, mods[0], w_out_p, row(norm_ffn_g[0]), rw, rb, tri)
    x2 = _moe(f, ri, wc, cnt, x1, mods[0], row(final_norm_g), moe_w_gate[0], moe_w_up[0],
              moe_w_down[0], final=False)

    u = _glu(x2, mods[1], row(norm_mix_g[1]), conv_pw1_w[0].astype(bf16), row(conv_pw1_b[0]))
    x3, f, ri, wc, cnt = _conv(u, x2, mods[1], conv_dw_w[0], row(conv_dw_b[0]), row(conv_ln_g[0]),
                               row(conv_ln_b[0]), conv_pw2_w[0].astype(bf16), row(conv_pw2_b[0]),
                               row(norm_ffn_g[1]), rw, rb, tri)
    return _moe(f, ri, wc, cnt, x3, mods[1], row(final_norm_g), moe_w_gate[1], moe_w_up[1],
                moe_w_down[1], final=True)
```

```python
import functools
import math

import jax
import jax.numpy as jnp
from jax import lax
from jax.experimental import pallas as pl
from jax.experimental.pallas import tpu as pltpu

f32 = jnp.float32
bf16 = jnp.bfloat16
i32 = jnp.int32
u32 = jnp.uint32
HIGHEST = lax.Precision.HIGHEST

GRID_W = 64
HEAD_DIM = 64
N_HEADS = 8
N_KV_HEADS = 2
WINDOW = 128
ATT_BLOCK = 128
ROPE_THETA = 10000.0
FOURIER_GROUPS = 4
FOURIER_GROUP_W = 128
CONV_W = 31
N_EXPERTS = 16
N_GROUPS = 4
EXPERTS_PER_GROUP = 4
EXPERT_FF = 512
EPS = 1e-6
NEG_INF = -1e30

LANES = 128
COND_ROWS = 8
DFT_INNER = 64
TOKEN_TILE = 512
ATT_TILE = 512
EXPERT_ROWS = 512
DISPATCH_TILE = 256
RUN_ALIGN = 8
LOCAL_ROWS = -(-(2 * DISPATCH_TILE + N_EXPERTS * (RUN_ALIGN - 1)) // LANES) * LANES
CONV_HALO = 16
VMEM_LIMIT = 56 * 1024 * 1024


def _cparams(sem, vmem=VMEM_LIMIT):
    return pltpu.CompilerParams(dimension_semantics=sem, vmem_limit_bytes=vmem)


def _adaln_kernel(cond_ref, w_ref, b_ref, o_ref):
    s = cond_ref[...]
    s = s * jax.nn.sigmoid(s)
    o_ref[0] = jnp.dot(s, w_ref[0], precision=HIGHEST, preferred_element_type=f32) + b_ref[0]


def _adaln(cond, ada_w, ada_b):
    depth, d, n = ada_w.shape
    tn = 1536
    return pl.pallas_call(
        _adaln_kernel,
        grid=(depth, n // tn),
        in_specs=[pl.BlockSpec((COND_ROWS, d), lambda i, j: (0, 0)),
                  pl.BlockSpec((1, d, tn), lambda i, j: (i, 0, j)),
                  pl.BlockSpec((1, 1, tn), lambda i, j: (i, 0, j))],
        out_specs=pl.BlockSpec((1, COND_ROWS, tn), lambda i, j: (i, 0, j)),
        out_shape=jax.ShapeDtypeStruct((depth, COND_ROWS, n), f32),
        compiler_params=_cparams(("arbitrary", "arbitrary")),
        name="adaln",
    )(cond, ada_w, ada_b.reshape(depth, 1, n))


def _norm_mod(x, g, shift, scale):
    ms = jnp.mean(x * x, axis=-1, keepdims=True)
    return (x * lax.rsqrt(ms + EPS) * g) * (1.0 + scale) + shift


def _rope(p, cos, sin_signed, first_half):
    rot = jnp.where(first_half, pltpu.roll(p, LANES - 16, axis=1), pltpu.roll(p, 16, axis=1))
    return p * cos + rot * sin_signed


def _inproj_kernel(x_ref, mod_ref, g_ref, w_ref, cs_ref, cos_ref, sin_ref,
                   fa_ref, fb_ref, q_ref, k_ref, v_ref):
    h = _norm_mod(x_ref[0], g_ref[...], mod_ref[0, 0:1, :], mod_ref[0, 1:2, :])
    p = jnp.dot(h.astype(bf16), w_ref[...], preferred_element_type=f32)
    fw = FOURIER_GROUPS * FOURIER_GROUP_W
    for g in range(FOURIER_GROUPS):
        ug = p[:, g * LANES:(g + 1) * LANES].astype(bf16)
        ab = jnp.dot(ug, cs_ref[...], preferred_element_type=f32)
        fa_ref[0, g] = ab[:, :LANES].astype(bf16)
        fb_ref[0, g] = ab[:, LANES:].astype(bf16)
    cos = cos_ref[...]
    sin = sin_ref[...]
    lane = lax.broadcasted_iota(i32, cos.shape, 1)
    first_half = (lane % 32) < 16
    qw = N_HEADS * HEAD_DIM
    for c in range(qw // LANES):
        qc = p[:, fw + c * LANES: fw + (c + 1) * LANES]
        q_ref[0, :, c * LANES:(c + 1) * LANES] = (
            _rope(qc, cos, sin, first_half) * (HEAD_DIM ** -0.5)).astype(bf16)
    k_ref[0] = _rope(p[:, fw + qw: fw + qw + LANES], cos, sin, first_half).astype(bf16)
    v_ref[0] = p[:, fw + qw + LANES:].astype(bf16)


def _inproj(x, mod, g, w, cs, cos_t, sin_t):
    b, l, d = x.shape
    tm = TOKEN_TILE
    n = w.shape[1]
    grp = pl.BlockSpec((1, FOURIER_GROUPS, tm, LANES), lambda i, j: (i, 0, j, 0))
    return pl.pallas_call(
        _inproj_kernel,
        grid=(b, l // tm),
        in_specs=[pl.BlockSpec((1, tm, d), lambda i, j: (i, j, 0)),
                  pl.BlockSpec((1, 6, d), lambda i, j: (i, 0, 0)),
                  pl.BlockSpec((1, d), lambda i, j: (0, 0)),
                  pl.BlockSpec((d, n), lambda i, j: (0, 0)),
                  pl.BlockSpec((LANES, 2 * LANES), lambda i, j: (0, 0)),
                  pl.BlockSpec((tm, LANES), lambda i, j: (j, 0)),
                  pl.BlockSpec((tm, LANES), lambda i, j: (j, 0))],
        out_specs=[grp, grp,
                   pl.BlockSpec((1, tm, N_HEADS * HEAD_DIM), lambda i, j: (i, j, 0)),
                   pl.BlockSpec((1, tm, LANES), lambda i, j: (i, j, 0)),
                   pl.BlockSpec((1, tm, LANES), lambda i, j: (i, j, 0))],
        out_shape=[jax.ShapeDtypeStruct((b, FOURIER_GROUPS, l, LANES), bf16),
                   jax.ShapeDtypeStruct((b, FOURIER_GROUPS, l, LANES), bf16),
                   jax.ShapeDtypeStruct((b, l, N_HEADS * HEAD_DIM), bf16),
                   jax.ShapeDtypeStruct((b, l, LANES), bf16),
                   jax.ShapeDtypeStruct((b, l, LANES), bf16)],
        compiler_params=_cparams(("parallel", "parallel")),
        name="inproj",
    )(x, mod, g, w, cs, cos_t, sin_t)


def _ctxkv_kernel(x_ref, mod_ref, g_ref, w_ref, k_ref, v_ref):
    h = _norm_mod(x_ref[0], g_ref[...], mod_ref[0, 0:1, :], mod_ref[0, 1:2, :])
    p = jnp.dot(h.astype(bf16), w_ref[...], preferred_element_type=f32)
    k_ref[0] = p[:, :LANES].astype(bf16)
    v_ref[0] = p[:, LANES:].astype(bf16)


def _ctxkv(ctx, mod, g, w_kv, ctx_row):
    b, c, d = ctx.shape
    return pl.pallas_call(
        _ctxkv_kernel,
        grid=(b,),
        in_specs=[pl.BlockSpec((1, c, d), lambda i: (i, 0, 0)),
                  pl.BlockSpec((1, 6, d), lambda i: (ctx_row, 0, 0)),
                  pl.BlockSpec((1, d), lambda i: (0, 0)),
                  pl.BlockSpec((d, 2 * LANES), lambda i: (0, 0))],
        out_specs=[pl.BlockSpec((1, c, LANES), lambda i: (i, 0, 0)),
                   pl.BlockSpec((1, c, LANES), lambda i: (i, 0, 0))],
        out_shape=[jax.ShapeDtypeStruct((b, c, LANES), bf16),
                   jax.ShapeDtypeStruct((b, c, LANES), bf16)],
        compiler_params=_cparams(("parallel",)),
        name="ctxkv",
    )(ctx, mod, g, w_kv)


def _attn_kernel(seq_len, q_ref, kp_ref, km_ref, kn_ref, vp_ref, vm_ref, vn_ref,
                 ck_ref, cv_ref, sink_ref, o_ref, kext, vext):
    j = pl.program_id(1)
    tq = ATT_TILE
    blk = ATT_BLOCK
    kext[0:blk] = kp_ref[0]
    kext[blk:blk + tq] = km_ref[0]
    kext[blk + tq:] = kn_ref[0]
    vext[0:blk] = vp_ref[0]
    vext[blk:blk + tq] = vm_ref[0]
    vext[blk + tq:] = vn_ref[0]
    ck = ck_ref[0]
    cv = cv_ref[0]
    sink = sink_ref[...]
    n_chunks = (N_HEADS * HEAD_DIM) // LANES
    rows = 2 * n_chunks * blk
    lane = lax.broadcasted_iota(i32, (blk, LANES), 1)
    low = lane < HEAD_DIM
    qi = lax.broadcasted_iota(i32, (rows, 3 * blk), 0) % blk
    pk = lax.broadcasted_iota(i32, (rows, 3 * blk), 1)
    band = jnp.abs(pk - blk - qi) <= WINDOW
    nt = (((1,), (1,)), ((), ()))

    def sub(s, carry):
        r0 = pl.multiple_of(s * blk, blk)
        qs = q_ref[0, pl.ds(r0, blk), :]
        parts = []
        for c in range(n_chunks):
            qc = qs[:, c * LANES:(c + 1) * LANES]
            parts.append(jnp.where(low, qc, jnp.zeros_like(qc)))
            parts.append(jnp.where(low, jnp.zeros_like(qc), qc))
        lhs = jnp.concatenate(parts, axis=0)
        kl = kext[pl.ds(r0, 3 * blk), :]
        vl = vext[pl.ds(r0, 3 * blk), :]
        s_c = lax.dot_general(lhs, ck, nt, preferred_element_type=f32)
        s_l = lax.dot_general(lhs, kl, nt, preferred_element_type=f32)
        kpos = j * tq + r0 - blk + pk
        valid = band & (kpos >= 0) & (kpos < seq_len)
        s_l = jnp.where(valid, s_l, NEG_INF)
        m = jnp.maximum(jnp.maximum(jnp.max(s_c, axis=1, keepdims=True),
                                    jnp.max(s_l, axis=1, keepdims=True)), sink)
        e_c = jnp.exp(s_c - m)
        e_l = jnp.exp(s_l - m)
        den = (jnp.sum(e_c, axis=1, keepdims=True) + jnp.sum(e_l, axis=1, keepdims=True)
               + jnp.exp(sink - m))
        o = (jnp.dot(e_c.astype(bf16), cv, preferred_element_type=f32)
             + jnp.dot(e_l.astype(bf16), vl, preferred_element_type=f32)) / den
        for c in range(n_chunks):
            oc = jnp.where(low, o[(2 * c) * blk:(2 * c + 1) * blk],
                           o[(2 * c + 1) * blk:(2 * c + 2) * blk])
            o_ref[0, pl.ds(r0, blk), c * LANES:(c + 1) * LANES] = oc.astype(bf16)
        return carry

    lax.fori_loop(0, tq // blk, sub, 0)


def _attention(q, k, v, ck, cv, sinkcol):
    b, l, qw = q.shape
    c = ck.shape[1]
    tq = ATT_TILE
    r = tq // ATT_BLOCK
    nb = l // ATT_BLOCK
    prev = pl.BlockSpec((1, ATT_BLOCK, LANES), lambda i, j: (i, jnp.maximum(j * r - 1, 0), 0))
    main = pl.BlockSpec((1, tq, LANES), lambda i, j: (i, j, 0))
    nxt = pl.BlockSpec((1, ATT_BLOCK, LANES), lambda i, j: (i, jnp.minimum(j * r + r, nb - 1), 0))
    cspec = pl.BlockSpec((1, c, LANES), lambda i, j: (i, 0, 0))
    return pl.pallas_call(
        functools.partial(_attn_kernel, l),
        grid=(b, l // tq),
        in_specs=[pl.BlockSpec((1, tq, qw), lambda i, j: (i, j, 0)),
                  prev, main, nxt, prev, main, nxt, cspec, cspec,
                  pl.BlockSpec(sinkcol.shape, lambda i, j: (0, 0))],
        out_specs=pl.BlockSpec((1, tq, qw), lambda i, j: (i, j, 0)),
        out_shape=jax.ShapeDtypeStruct((b, l, qw), bf16),
        scratch_shapes=[pltpu.VMEM((tq + 2 * ATT_BLOCK, LANES), bf16),
                        pltpu.VMEM((tq + 2 * ATT_BLOCK, LANES), bf16)],
        compiler_params=_cparams(("parallel", "parallel")),
        name="attention",
    )(q, k, k, k, v, v, v, ck, cv, sinkcol)


def _dft1_kernel(a_ref, b_ref, m_ref, ct_ref, st_ref, o_ref):
    n1 = a_ref.shape[2]
    for t in range(a_ref.shape[3]):
        ab = jnp.concatenate([a_ref[0, 0, :, t, :], b_ref[0, 0, :, t, :]], axis=0)
        z = jnp.dot(m_ref[...], ab, preferred_element_type=f32)
        zr = z[:n1]
        zn = z[n1:]
        ct = ct_ref[t]
        st = st_ref[t]
        o_ref[0, 0, 0, t] = (ct * zr - st * zn).astype(bf16)
        o_ref[0, 0, 1, t] = (ct * zn + st * zr).astype(bf16)


def _dft2_kernel(scale, z_ref, m_ref, o_ref):
    y = jnp.dot(m_ref[...], z_ref[0, 0], preferred_element_type=f32)
    o_ref[0, 0] = (y * scale).astype(bf16)


def _fourier(fa, fb):
    b, g, l, w = fa.shape
    n2 = DFT_INNER
    n1 = l // n2
    t2 = 8
    k1 = jnp.arange(n1, dtype=i32)
    ang1 = ((k1[:, None] * k1[None, :]) % n1).astype(f32) * (2.0 * math.pi / n1)
    c1, s1 = jnp.cos(ang1), jnp.sin(ang1)
    m1 = jnp.concatenate([jnp.concatenate([c1, -s1], axis=1),
                          jnp.concatenate([s1, c1], axis=1)], axis=0).astype(bf16)
    l2 = jnp.arange(n2, dtype=i32)
    angt = ((l2[:, None] * k1[None, :]) % l).astype(f32) * (2.0 * math.pi / l)
    ct = jnp.broadcast_to(jnp.cos(angt)[:, :, None], (n2, n1, w))
    st = jnp.broadcast_to(jnp.sin(angt)[:, :, None], (n2, n1, w))
    ang2 = ((l2[:, None] * l2[None, :]) % n2).astype(f32) * (2.0 * math.pi / n2)
    m2 = jnp.concatenate([jnp.cos(ang2), -jnp.sin(ang2)], axis=1).astype(bf16)

    a5 = fa.reshape(b, g, n1, n2, w)
    b5 = fb.reshape(b, g, n1, n2, w)
    dspec = pl.BlockSpec((1, 1, n1, t2, w), lambda t, i, j: (i, j, 0, t, 0))
    tspec = pl.BlockSpec((t2, n1, w), lambda t, i, j: (t, 0, 0))
    z = pl.pallas_call(
        _dft1_kernel,
        grid=(n2 // t2, b, g),
        in_specs=[dspec, dspec, pl.BlockSpec((2 * n1, 2 * n1), lambda t, i, j: (0, 0)), tspec, tspec],
        out_specs=pl.BlockSpec((1, 1, 2, t2, n1, w), lambda t, i, j: (i, j, 0, t, 0, 0)),
        out_shape=jax.ShapeDtypeStruct((b, g, 2, n2, n1, w), bf16),
        compiler_params=_cparams(("parallel", "parallel", "parallel")),
        name="dft_outer",
    )(a5, b5, m1, ct, st)

    z2 = z.reshape(b, g, 2 * n2, n1 * w)
    tn = min(n1 * w, 4096)
    y = pl.pallas_call(
        functools.partial(_dft2_kernel, 1.0 / math.sqrt(l * w)),
        grid=(b, g, (n1 * w) // tn),
        in_specs=[pl.BlockSpec((1, 1, 2 * n2, tn), lambda i, j, t: (i, j, 0, t)),
                  pl.BlockSpec((n2, 2 * n2), lambda i, j, t: (0, 0))],
        out_specs=pl.BlockSpec((1, 1, n2, tn), lambda i, j, t: (i, j, 0, t)),
        out_shape=jax.ShapeDtypeStruct((b, g, n2, n1 * w), bf16),
        compiler_params=_cparams(("parallel", "parallel", "parallel")),
        name="dft_inner",
    )(z2, m2)
    return y.reshape(b, g, l, w)


def _first_max4(a):
    m = jnp.maximum(jnp.maximum(a[0], a[1]), jnp.maximum(a[2], a[3]))
    idx = jnp.where(a[0] == m, 0, jnp.where(a[1] == m, 1, jnp.where(a[2] == m, 2, 3)))
    return m, idx


def _pick4(vals, idx):
    return jnp.where(idx == 0, vals[0], jnp.where(idx == 1, vals[1],
                                                   jnp.where(idx == 2, vals[2], vals[3])))


def _route(f, rw_ref, rb_ref, tri_ref, base_ref, first_step, ri_ref, wc_ref, cnt_ref, meta_ref):
    tm = f.shape[0]
    logits = jnp.dot(f, rw_ref[...], precision=HIGHEST, preferred_element_type=f32)
    sc = jax.nn.sigmoid(logits)
    st = sc.T
    bt = (sc + rb_ref[...]).T
    neg = jnp.full((1, tm), -jnp.inf, f32)
    gs = []
    for g in range(N_GROUPS):
        a = [bt[4 * g + i: 4 * g + i + 1] for i in range(4)]
        m1, i1 = _first_max4(a)
        rest = [jnp.where(i1 == i, neg, a[i]) for i in range(4)]
        m2, _ = _first_max4(rest)
        gs.append(m1 + m2)
    _, gsel = _first_max4(gs)
    a = [_pick4([bt[4 * g + i: 4 * g + i + 1] for g in range(N_GROUPS)], gsel) for i in range(4)]
    s = [_pick4([st[4 * g + i: 4 * g + i + 1] for g in range(N_GROUPS)], gsel) for i in range(4)]
    _, i1 = _first_max4(a)
    rest = [jnp.where(i1 == i, neg, a[i]) for i in range(4)]
    _, i2 = _first_max4(rest)
    w1 = _pick4(s, i1)
    w2 = _pick4(s, i2)
    tot = w1 + w2
    w1 = w1 / tot
    w2 = w2 / tot
    e0 = gsel * EXPERTS_PER_GROUP + i1
    e1 = gsel * EXPERTS_PER_GROUP + i2

    @pl.when(first_step)
    def _():
        base_ref[...] = jnp.zeros_like(base_ref)

    td = DISPATCH_TILE
    eid = lax.broadcasted_iota(i32, (N_EXPERTS, tm), 0)
    oh0 = (eid == e0).astype(f32)
    oh1 = (eid == e1).astype(f32)
    oh = oh0 + oh1
    before = jnp.dot(oh.astype(bf16), tri_ref[...], preferred_element_type=f32)
    lane_tile = lax.broadcasted_iota(i32, (N_EXPERTS, tm), 1) // td
    ei = lax.broadcasted_iota(i32, (N_EXPERTS, N_EXPERTS), 0)
    ej = lax.broadcasted_iota(i32, (N_EXPERTS, N_EXPERTS), 1)
    strict_lower = (ej < ei).astype(f32)
    run_start = jnp.zeros((N_EXPERTS, tm), f32)
    goff = base_ref[...]
    for s in range(tm // td):
        cnt_s = jnp.sum(oh[:, s * td:(s + 1) * td], axis=1, keepdims=True)
        pad_s = jnp.floor((cnt_s + 7.0) * 0.125) * 8.0
        pad_b = jnp.broadcast_to(pad_s, (N_EXPERTS, LANES))
        start_b = jnp.dot(strict_lower, pad_b, precision=HIGHEST, preferred_element_type=f32)
        run_start = jnp.where(lane_tile == s, start_b[:, 0:1], run_start)
        meta_ref[s, 0] = start_b.astype(i32)
        meta_ref[s, 1] = pad_b.astype(i32)
        meta_ref[s, 2] = goff.astype(i32)
        goff = goff + pad_b
    base_ref[...] = goff
    cnt_ref[...] = goff
    pos = before + run_start
    lp0 = jnp.sum(oh0 * pos, axis=0, keepdims=True)
    lp1 = jnp.sum(oh1 * pos, axis=0, keepdims=True)
    zi = jnp.zeros((1, tm), i32)
    ri_ref[...] = jnp.concatenate(
        [lp0.astype(i32), lp1.astype(i32), e0, e1, zi, zi, zi, zi], axis=0)
    zf = jnp.zeros((LANES - 4, tm), f32)
    wc_ref[...] = jnp.concatenate([w1, w2, lp0, lp1, zf], axis=0).T


def _outproj_kernel(yf_ref, o_ref, x_ref, mod_ref, w_ref, g_ref, rw_ref, rb_ref, tri_ref,
                    x1_ref, f_ref, ri_ref, wc_ref, cnt_ref, meta_ref, base_ref):
    mix = jnp.concatenate([yf_ref[0, g] for g in range(FOURIER_GROUPS)] + [o_ref[0]], axis=1)
    y = jnp.dot(mix, w_ref[...], preferred_element_type=f32)
    x1 = x_ref[0] + mod_ref[0, 2:3, :] * y
    x1_ref[0] = x1
    f = _norm_mod(x1, g_ref[...], mod_ref[0, 3:4, :], mod_ref[0, 4:5, :])
    f_ref[0] = f.astype(bf16)
    first = (pl.program_id(0) == 0) & (pl.program_id(1) == 0)
    _route(f, rw_ref, rb_ref, tri_ref, base_ref, first, ri_ref, wc_ref, cnt_ref, meta_ref)


def _route_specs(b, l, tm):
    nl = l // tm
    rw = lambda d: pl.BlockSpec((d, LANES), lambda i, j: (0, 0))
    rb = pl.BlockSpec((1, LANES), lambda i, j: (0, 0))
    tri = pl.BlockSpec((tm, tm), lambda i, j: (0, 0))
    ns = tm // DISPATCH_TILE
    out_specs = [pl.BlockSpec((8, tm), lambda i, j: (0, i * nl + j)),
                 pl.BlockSpec((tm, LANES), lambda i, j: (i * nl + j, 0)),
                 pl.BlockSpec((N_EXPERTS, LANES), lambda i, j: (0, 0)),
                 pl.BlockSpec((ns, 3, N_EXPERTS, LANES), lambda i, j: (i * nl + j, 0, 0, 0))]
    out_shape = [jax.ShapeDtypeStruct((8, b * l), i32),
                 jax.ShapeDtypeStruct((b * l, LANES), f32),
                 jax.ShapeDtypeStruct((N_EXPERTS, LANES), f32),
                 jax.ShapeDtypeStruct((b * l // DISPATCH_TILE, 3, N_EXPERTS, LANES), i32)]
    return rw, rb, tri, out_specs, out_shape


def _outproj(yf, o, x, mod, w, g, rw, rb, tri):
    b, l, d = x.shape
    tm = TOKEN_TILE
    rws, rbs, tris, r_specs, r_shapes = _route_specs(b, l, tm)
    row = pl.BlockSpec((1, tm, d), lambda i, j: (i, j, 0))
    return pl.pallas_call(
        _outproj_kernel,
        grid=(b, l // tm),
        in_specs=[pl.BlockSpec((1, FOURIER_GROUPS, tm, LANES), lambda i, j: (i, 0, j, 0)),
                  pl.BlockSpec((1, tm, o.shape[2]), lambda i, j: (i, j, 0)),
                  row,
                  pl.BlockSpec((1, 6, d), lambda i, j: (i, 0, 0)),
                  pl.BlockSpec(w.shape, lambda i, j: (0, 0)),
                  pl.BlockSpec((1, d), lambda i, j: (0, 0)),
                  rws(d), rbs, tris],
        out_specs=[row, row] + r_specs,
        out_shape=[jax.ShapeDtypeStruct((b, l, d), f32),
                   jax.ShapeDtypeStruct((b, l, d), bf16)] + r_shapes,
        scratch_shapes=[pltpu.VMEM((N_EXPERTS, LANES), f32)],
        compiler_params=_cparams(("arbitrary", "arbitrary")),
        name="outproj_router",
    )(yf, o, x, mod, w, g, rw, rb, tri)


def _glu_kernel(x_ref, mod_ref, g_ref, w_ref, b_ref, u_ref):
    h = _norm_mod(x_ref[0], g_ref[...], mod_ref[0, 0:1, :], mod_ref[0, 1:2, :])
    p = jnp.dot(h.astype(bf16), w_ref[...], preferred_element_type=f32) + b_ref[...]
    ch = p.shape[1] // 2
    u_ref[0] = p[:, :ch] * jax.nn.sigmoid(p[:, ch:])


def _glu(x, mod, g, w, bias):
    b, l, d = x.shape
    tm = TOKEN_TILE
    n = w.shape[1]
    return pl.pallas_call(
        _glu_kernel,
        grid=(b, l // tm),
        in_specs=[pl.BlockSpec((1, tm, d), lambda i, j: (i, j, 0)),
                  pl.BlockSpec((1, 6, d), lambda i, j: (i, 0, 0)),
                  pl.BlockSpec((1, d), lambda i, j: (0, 0)),
                  pl.BlockSpec((d, n), lambda i, j: (0, 0)),
                  pl.BlockSpec((1, n), lambda i, j: (0, 0))],
        out_specs=pl.BlockSpec((1, tm, n // 2), lambda i, j: (i, j, 0)),
        out_shape=jax.ShapeDtypeStruct((b, l, n // 2), f32),
        compiler_params=_cparams(("parallel", "parallel")),
        name="pw1_glu",
    )(x, mod, g, w, bias)


def _conv_kernel(seq_len, up_ref, um_ref, un_ref, x_ref, mod_ref, dw_ref, db_ref, lg_ref, lb_ref,
                 w_ref, pb_ref, g_ref, rw_ref, rb_ref, tri_ref,
                 x1_ref, f_ref, ri_ref, wc_ref, cnt_ref, meta_ref, base_ref, ext):
    j = pl.program_id(1)
    tm = um_ref.shape[1]
    hl = CONV_HALO
    half = CONV_W // 2
    ext[0:hl] = jnp.where(j > 0, up_ref[0], jnp.zeros_like(up_ref[0]))
    ext[hl:hl + tm] = um_ref[0]
    ext[hl + tm:] = jnp.where((j + 1) * tm < seq_len, un_ref[0], jnp.zeros_like(un_ref[0]))
    acc = jnp.zeros(um_ref.shape[1:], f32) + db_ref[...]
    for t in range(CONV_W):
        acc = acc + ext[hl - half + t: hl - half + t + tm, :] * dw_ref[t:t + 1, :]
    mu = jnp.mean(acc, axis=-1, keepdims=True)
    cen = acc - mu
    var = jnp.mean(cen * cen, axis=-1, keepdims=True)
    ln = cen * lax.rsqrt(var + EPS) * lg_ref[...] + lb_ref[...]
    act = ln * jax.nn.sigmoid(ln)
    y = jnp.dot(act.astype(bf16), w_ref[...], preferred_element_type=f32) + pb_ref[...]
    x1 = x_ref[0] + mod_ref[0, 2:3, :] * y
    x1_ref[0] = x1
    f = _norm_mod(x1, g_ref[...], mod_ref[0, 3:4, :], mod_ref[0, 4:5, :])
    f_ref[0] = f.astype(bf16)
    first = (pl.program_id(0) == 0) & (j == 0)
    _route(f, rw_ref, rb_ref, tri_ref, base_ref, first, ri_ref, wc_ref, cnt_ref, meta_ref)


def _conv(u, x, mod, dw_w, dw_b, ln_g, ln_b, pw2_w, pw2_b, g, rw, rb, tri):
    b, l, d = x.shape
    tm = TOKEN_TILE
    hl = CONV_HALO
    r = tm // hl
    nh = l // hl
    rws, rbs, tris, r_specs, r_shapes = _route_specs(b, l, tm)
    row = pl.BlockSpec((1, tm, d), lambda i, j: (i, j, 0))
    vec = pl.BlockSpec((1, d), lambda i, j: (0, 0))
    return pl.pallas_call(
        functools.partial(_conv_kernel, l),
        grid=(b, l // tm),
        in_specs=[pl.BlockSpec((1, hl, d), lambda i, j: (i, jnp.maximum(j * r - 1, 0), 0)),
                  row,
                  pl.BlockSpec((1, hl, d), lambda i, j: (i, jnp.minimum(j * r + r, nh - 1), 0)),
                  row,
                  pl.BlockSpec((1, 6, d), lambda i, j: (i, 0, 0)),
                  pl.BlockSpec(dw_w.shape, lambda i, j: (0, 0)),
                  vec, vec, vec,
                  pl.BlockSpec(pw2_w.shape, lambda i, j: (0, 0)),
                  vec, vec, rws(d), rbs, tris],
        out_specs=[row, row] + r_specs,
        out_shape=[jax.ShapeDtypeStruct((b, l, d), f32),
                   jax.ShapeDtypeStruct((b, l, d), bf16)] + r_shapes,
        scratch_shapes=[pltpu.VMEM((N_EXPERTS, LANES), f32),
                        pltpu.VMEM((tm + 2 * hl, d), f32)],
        compiler_params=_cparams(("arbitrary", "arbitrary")),
        name="conv_router",
    )(u, u, u, x, mod, dw_w, dw_b, ln_g, ln_b, pw2_w, pw2_b, g, rw, rb, tri)


def _pack_bf16_pairs(x):
    h = x.shape[1] // 2
    lo = lax.bitcast_convert_type(x[:, :h], u32)
    hi = lax.bitcast_convert_type(x[:, h:], u32)
    return (lo >> 16) | (hi & jnp.uint32(0xFFFF0000))


def _unpack_bf16_pairs(u):
    lo = lax.bitcast_convert_type(u << 16, f32)
    hi = lax.bitcast_convert_type(u & jnp.uint32(0xFFFF0000), f32)
    return jnp.concatenate([lo, hi], axis=1).astype(bf16)


def _run_copies(meta, tile, local_ref, hbm_ref, sem, to_hbm):
    start_ref, size_ref, dst_ref = meta
    for e in range(N_EXPERTS):
        k = tile * N_EXPERTS + e
        size = pl.multiple_of(size_ref[k], RUN_ALIGN)

        @pl.when(size > 0)
        def _():
            loc = local_ref.at[pl.ds(pl.multiple_of(start_ref[k], RUN_ALIGN), size)]
            glob = hbm_ref.at[pl.ds(pl.multiple_of(dst_ref[k], RUN_ALIGN), size)]
            if to_hbm:
                pltpu.make_async_copy(loc, glob, sem).start()
            else:
                pltpu.make_async_copy(glob, loc, sem).start()


def _wait_rows(rows, local_ref, hbm_ref, sem):
    rows = pl.multiple_of(rows, RUN_ALIGN)

    @pl.when(rows > 0)
    def _():
        pltpu.make_async_copy(local_ref.at[pl.ds(0, rows)], hbm_ref.at[pl.ds(0, rows)], sem).wait()


def _dispatch_kernel(start_ref, size_ref, dst_ref, tot_ref, tail_start_ref, tail_size_ref, nv_ref,
                     f_ref, lp_ref, xs_ref, loc, zbuf, sem, zsem):
    i = pl.program_id(0)
    n = pl.num_programs(0)
    slot = i % 2
    meta = (start_ref, size_ref, dst_ref)

    @pl.when(i >= 2)
    def _():
        _wait_rows(tot_ref[i - 2], loc.at[slot], xs_ref, sem.at[slot])

    rows = loc.shape[1]
    td = f_ref.shape[0]
    r = lax.broadcasted_iota(i32, (rows, td), 0)
    onehot = ((r == lp_ref[0:1, :]) | (r == lp_ref[1:2, :])).astype(bf16)
    sorted_rows = jnp.dot(onehot, f_ref[...], preferred_element_type=f32)
    loc[slot] = _pack_bf16_pairs(sorted_rows)
    _run_copies(meta, i, loc.at[slot], xs_ref, sem.at[slot], to_hbm=True)

    @pl.when(i == n - 1)
    def _():
        zbuf[...] = jnp.zeros_like(zbuf)
        total = 0
        for e in range(N_EXPERTS):
            size = pl.multiple_of(tail_size_ref[e], RUN_ALIGN)
            total = total + size

            @pl.when(size > 0)
            def _():
                pltpu.make_async_copy(
                    zbuf.at[pl.ds(0, size)],
                    xs_ref.at[pl.ds(pl.multiple_of(tail_start_ref[e], RUN_ALIGN), size)], zsem).start()

        _wait_rows(total, zbuf, xs_ref, zsem)

        def zero_block(k, c):
            pltpu.make_async_copy(zbuf, xs_ref.at[pl.ds(pl.multiple_of(k * zbuf.shape[0], RUN_ALIGN),
                                                        zbuf.shape[0])], zsem).start()
            return c

        def wait_block(k, c):
            pltpu.make_async_copy(zbuf, xs_ref.at[pl.ds(0, zbuf.shape[0])], zsem).wait()
            return c

        n_blocks = xs_ref.shape[0] // zbuf.shape[0]
        lax.fori_loop(nv_ref[0], n_blocks, zero_block, 0)
        lax.fori_loop(nv_ref[0], n_blocks, wait_block, 0)
        _wait_rows(tot_ref[i], loc.at[slot], xs_ref, sem.at[slot])

        @pl.when(i >= 1)
        def _():
            _wait_rows(tot_ref[i - 1], loc.at[1 - slot], xs_ref, sem.at[1 - slot])


def _dispatch(tables, f2, ri, n_slots):
    t, d = f2.shape
    td = DISPATCH_TILE
    return pl.pallas_call(
        _dispatch_kernel,
        grid_spec=pltpu.PrefetchScalarGridSpec(
            num_scalar_prefetch=7,
            grid=(t // td,),
            in_specs=[pl.BlockSpec((td, d), lambda i, *_: (i, 0)),
                      pl.BlockSpec((8, td), lambda i, *_: (0, i))],
            out_specs=pl.BlockSpec(memory_space=pl.ANY),
            scratch_shapes=[pltpu.VMEM((2, LOCAL_ROWS, d // 2), u32),
                            pltpu.VMEM((EXPERT_ROWS, d // 2), u32),
                            pltpu.SemaphoreType.DMA((2,)), pltpu.SemaphoreType.DMA(())]),
        out_shape=jax.ShapeDtypeStruct((n_slots, d // 2), u32),
        compiler_params=_cparams(("arbitrary",)),
        name="moe_dispatch",
    )(*tables, f2, ri)


def _expert_kernel(be_ref, nv_ref, x_ref, wg_ref, wu_ref, wd_ref, y_ref, wgb, wub, wdb):
    i = pl.program_id(0)
    changed = jnp.logical_or(i == 0, be_ref[i] != be_ref[jnp.maximum(i - 1, 0)])

    @pl.when(changed)
    def _():
        wgb[...] = wg_ref[0].astype(bf16)
        wub[...] = wu_ref[0].astype(bf16)
        wdb[...] = wd_ref[0].astype(bf16)

    @pl.when(i < nv_ref[0])
    def _():
        xb = _unpack_bf16_pairs(x_ref[...])
        gate = jnp.dot(xb, wgb[...], preferred_element_type=f32)
        up = jnp.dot(xb, wub[...], preferred_element_type=f32)
        hid = (gate * jax.nn.sigmoid(gate) * up).astype(bf16)
        y = jnp.dot(hid, wdb[...], preferred_element_type=f32)
        y_ref[...] = _pack_bf16_pairs(y.astype(bf16).astype(f32))

    @pl.when(i >= nv_ref[0])
    def _():
        y_ref[...] = jnp.zeros_like(y_ref)


def _experts(block_e, n_valid, xs, w_gate, w_up, w_down):
    ns, dh = xs.shape
    tb = EXPERT_ROWS
    d, ff = w_gate.shape[1:]
    xmap = lambda i, be, nv: (jnp.minimum(i, nv[0] - 1), 0)
    return pl.pallas_call(
        _expert_kernel,
        grid_spec=pltpu.PrefetchScalarGridSpec(
            num_scalar_prefetch=2,
            grid=(ns // tb,),
            in_specs=[pl.BlockSpec((tb, dh), xmap),
                      pl.BlockSpec((1, d, ff), lambda i, be, nv: (be[i], 0, 0)),
                      pl.BlockSpec((1, d, ff), lambda i, be, nv: (be[i], 0, 0)),
                      pl.BlockSpec((1, ff, d), lambda i, be, nv: (be[i], 0, 0))],
            out_specs=pl.BlockSpec((tb, dh), lambda i, be, nv: (i, 0)),
            scratch_shapes=[pltpu.VMEM((d, ff), bf16), pltpu.VMEM((d, ff), bf16),
                            pltpu.VMEM((ff, d), bf16)]),
        out_shape=jax.ShapeDtypeStruct((ns, dh), u32),
        compiler_params=_cparams(("arbitrary",)),
        name="moe_experts",
    )(block_e, n_valid, xs, w_gate, w_up, w_down)


def _combine_kernel(final, start_ref, size_ref, dst_ref, tot_ref, ys_ref, wc_ref, x_ref, mod_ref,
                    g_ref, o_ref, loc, sem):
    i = pl.program_id(0)
    n = pl.num_programs(0)
    slot = i % 2
    meta = (start_ref, size_ref, dst_ref)

    @pl.when(i == 0)
    def _():
        loc[...] = jnp.zeros_like(loc)
        _run_copies(meta, i, loc.at[slot], ys_ref, sem.at[slot], to_hbm=False)

    @pl.when(i + 1 < n)
    def _():
        _run_copies(meta, i + 1, loc.at[1 - slot], ys_ref, sem.at[1 - slot], to_hbm=False)

    _wait_rows(tot_ref[i], loc.at[slot], ys_ref, sem.at[slot])
    rows = loc.shape[1]
    td = x_ref.shape[0]
    wc = wc_ref[...]
    c = lax.broadcasted_iota(i32, (td, rows), 1)
    sel = jnp.concatenate([(c == wc[:, 2:3].astype(i32)).astype(bf16),
                           (c == wc[:, 3:4].astype(i32)).astype(bf16)], axis=0)
    picked = jnp.dot(sel, _unpack_bf16_pairs(loc[slot]), preferred_element_type=f32)
    y = wc[:, 0:1] * picked[:td] + wc[:, 1:2] * picked[td:]
    xo = x_ref[...] + mod_ref[0, 5:6, :] * y
    if final:
        ms = jnp.mean(xo * xo, axis=-1, keepdims=True)
        xo = xo * lax.rsqrt(ms + EPS) * g_ref[...]
    o_ref[...] = xo


def _combine(tables, ys, wc, x, mod, g, final):
    b, l, d = x.shape
    td = DISPATCH_TILE
    per_batch = l // td
    out = pl.pallas_call(
        functools.partial(_combine_kernel, final),
        grid_spec=pltpu.PrefetchScalarGridSpec(
            num_scalar_prefetch=4,
            grid=(b * per_batch,),
            in_specs=[pl.BlockSpec(memory_space=pl.ANY),
                      pl.BlockSpec((td, LANES), lambda i, *_: (i, 0)),
                      pl.BlockSpec((td, d), lambda i, *_: (i, 0)),
                      pl.BlockSpec((1, 6, d), lambda i, *_: (i // per_batch, 0, 0)),
                      pl.BlockSpec((1, d), lambda i, *_: (0, 0))],
            out_specs=pl.BlockSpec((td, d), lambda i, *_: (i, 0)),
            scratch_shapes=[pltpu.VMEM((2, LOCAL_ROWS, d // 2), u32),
                            pltpu.SemaphoreType.DMA((2,))]),
        out_shape=jax.ShapeDtypeStruct((b * l, d), f32),
        compiler_params=_cparams(("arbitrary",)),
        name="moe_combine",
    )(*tables, ys, wc, x.reshape(b * l, d), mod, g)
    return out.reshape(b, l, d)


def _moe(f, routed, x, mod, g_final, w_gate, w_up, w_down, final):
    ri, wc, cnt, meta = routed
    b, l, d = x.shape
    t = b * l
    tb = EXPERT_ROWS
    n_tiles = t // DISPATCH_TILE
    used = cnt[:, 0].astype(i32)
    region = (used + tb - 1) // tb * tb
    gend = jnp.cumsum(region)
    gstart = gend - region
    max_rows = 2 * t + n_tiles * N_EXPERTS * (RUN_ALIGN - 1) + N_EXPERTS * (tb - 1)
    n_blocks = -(-max_rows // tb)
    m = meta[:, :, :, 0]
    run_start = m[:, 0].reshape(-1)
    run_size = m[:, 1].reshape(-1)
    run_dst = (m[:, 2] + gstart[None, :]).reshape(-1)
    tile_rows = jnp.sum(m[:, 1], axis=1)
    block_row = jnp.arange(n_blocks, dtype=i32) * tb
    block_e = jnp.minimum(jnp.sum((block_row[:, None] >= gend[None, :]).astype(i32), axis=1),
                          N_EXPERTS - 1)
    n_valid = (gend[-1] // tb).reshape(1)
    xs = _dispatch((run_start, run_size, run_dst, tile_rows, gstart + used, region - used, n_valid),
                   f.reshape(t, d), ri, n_blocks * tb)
    ys = _experts(block_e, n_valid, xs, w_gate, w_up, w_down)
    return _combine((run_start, run_size, run_dst, tile_rows), ys, wc, x, mod, g_final, final)


def _rope_tables(l):
    lane = jnp.arange(LANES)
    dh = lane % HEAD_DIM
    inv = ROPE_THETA ** (-(dh % 16).astype(f32) / 16.0)
    pos = jnp.arange(l)
    row = (pos // GRID_W).astype(f32)
    col = (pos % GRID_W).astype(f32)
    p = jnp.where((dh // 32)[None, :] == 0, row[:, None], col[:, None])
    ang = p * inv[None, :]
    sign = jnp.where((dh % 32) < 16, -1.0, 1.0).astype(f32)
    return jnp.cos(ang), jnp.sin(ang) * sign[None, :]


def kernel(x, c, ctx, c_ctx, ada_w, ada_b, norm_mix_g, norm_ffn_g, even_w_in, even_w_out, even_sink, conv_pw1_w, conv_pw1_b, conv_dw_w, conv_dw_b, conv_ln_g, conv_ln_b, conv_pw2_w, conv_pw2_b, router_w, router_b, moe_w_gate, moe_w_up, moe_w_down, final_norm_g):
    b, l, d = x.shape
    depth = ada_w.shape[0]
    assert depth == 2 and b < COND_ROWS
    ctx_row = b
    cond = jnp.zeros((COND_ROWS, d), f32).at[:b].set(c).at[ctx_row].set(c_ctx)
    mods = _adaln(cond, ada_w, ada_b).reshape(depth, COND_ROWS, 6, d)

    heads = jnp.arange(N_HEADS).reshape(N_KV_HEADS, N_HEADS // N_KV_HEADS).T.reshape(-1)
    qperm = (heads[:, None] * HEAD_DIM + jnp.arange(HEAD_DIM)[None, :]).reshape(-1)
    fw = FOURIER_GROUPS * FOURIER_GROUP_W
    qw = N_HEADS * HEAD_DIM
    w_in = even_w_in[0]
    w_in_p = jnp.concatenate([w_in[:, :fw], w_in[:, fw:fw + qw][:, qperm], w_in[:, fw + qw:]],
                             axis=1).astype(bf16)
    w_out = even_w_out[0]
    w_out_p = jnp.concatenate([w_out[:fw], w_out[fw:][qperm]], axis=0).astype(bf16)
    sinkcol = jnp.repeat(even_sink[0][heads], ATT_BLOCK).reshape(-1, 1).astype(f32)

    cidx = jnp.arange(FOURIER_GROUP_W, dtype=i32)
    angc = ((cidx[:, None] * cidx[None, :]) % FOURIER_GROUP_W).astype(f32) * (2.0 * math.pi / FOURIER_GROUP_W)
    cs = jnp.concatenate([jnp.cos(angc), jnp.sin(angc)], axis=1).astype(bf16)
    cos_t, sin_t = _rope_tables(l)

    rw = jnp.zeros((d, LANES), f32).at[:, :N_EXPERTS].set(router_w)
    rb = jnp.zeros((1, LANES), f32).at[0, :N_EXPERTS].set(router_b)
    tpos = jnp.arange(TOKEN_TILE)
    tri = ((tpos[:, None] < tpos[None, :])
           & (tpos[:, None] // DISPATCH_TILE == tpos[None, :] // DISPATCH_TILE)).astype(bf16)
    row = lambda v: v.reshape(1, -1)

    fa, fb, q, k, v = _inproj(x, mods[0], row(norm_mix_g[0]), w_in_p, cs, cos_t, sin_t)
    ck, cv = _ctxkv(ctx, mods[0], row(norm_mix_g[0]), w_in_p[:, fw + qw:], ctx_row)
    yf = _fourier(fa, fb)
    att = _attention(q, k, v, ck, cv, sinkcol)
    x1, f, *routed = _outproj(yf, att, x, mods[0], w_out_p, row(norm_ffn_g[0]), rw, rb, tri)
    x2 = _moe(f, routed, x1, mods[0], row(final_norm_g), moe_w_gate[0], moe_w_up[0],
              moe_w_down[0], final=False)

    u = _glu(x2, mods[1], row(norm_mix_g[1]), conv_pw1_w[0].astype(bf16), row(conv_pw1_b[0]))
    x3, f, *routed = _conv(u, x2, mods[1], conv_dw_w[0], row(conv_dw_b[0]), row(conv_ln_g[0]),
                               row(conv_ln_b[0]), conv_pw2_w[0].astype(bf16), row(conv_pw2_b[0]),
                               row(norm_ffn_g[1]), rw, rb, tri)
    return _moe(f, routed, x3, mods[1], row(final_norm_g), moe_w_gate[1], moe_w_up[1],
                moe_w_down[1], final=True)
```

```python
import functools
import math

import jax
import jax.numpy as jnp
from jax import lax
from jax.experimental import pallas as pl
from jax.experimental.pallas import tpu as pltpu

f32 = jnp.float32
bf16 = jnp.bfloat16
i32 = jnp.int32
u32 = jnp.uint32
HIGHEST = lax.Precision.HIGHEST

GRID_W = 64
HEAD_DIM = 64
N_HEADS = 8
N_KV_HEADS = 2
WINDOW = 128
ATT_BLOCK = 128
ROPE_THETA = 10000.0
FOURIER_GROUPS = 4
FOURIER_GROUP_W = 128
CONV_W = 31
N_EXPERTS = 16
N_GROUPS = 4
EXPERTS_PER_GROUP = 4
EXPERT_FF = 512
EPS = 1e-6
NEG_INF = -1e30
LOG2E = math.log2(math.e)

LANES = 128
SUBLANES = 8
COND_ROWS = 8
DFT_INNER = 64
TOKEN_TILE = 512
ATT_TILE = 512
EXPERT_ROWS = 512
DISPATCH_TILE = 256
RUN_ALIGN = 8
LOCAL_ROWS = -(-(2 * DISPATCH_TILE + N_EXPERTS * (RUN_ALIGN - 1)) // LANES) * LANES
CONV_HALO = 16
VMEM_LIMIT = 56 * 1024 * 1024


def _cparams(sem, vmem=VMEM_LIMIT):
    return pltpu.CompilerParams(dimension_semantics=sem, vmem_limit_bytes=vmem)


def _adaln_kernel(cond_ref, w_ref, b_ref, o_ref):
    s = cond_ref[...]
    s = s * jax.nn.sigmoid(s)
    o_ref[0] = jnp.dot(s, w_ref[0], precision=HIGHEST, preferred_element_type=f32) + b_ref[0]


def _adaln(cond, ada_w, ada_b):
    depth, d, n = ada_w.shape
    tn = 1536
    return pl.pallas_call(
        _adaln_kernel,
        grid=(depth, n // tn),
        in_specs=[pl.BlockSpec((COND_ROWS, d), lambda i, j: (0, 0)),
                  pl.BlockSpec((1, d, tn), lambda i, j: (i, 0, j)),
                  pl.BlockSpec((1, 1, tn), lambda i, j: (i, 0, j))],
        out_specs=pl.BlockSpec((1, COND_ROWS, tn), lambda i, j: (i, 0, j)),
        out_shape=jax.ShapeDtypeStruct((depth, COND_ROWS, n), f32),
        compiler_params=_cparams(("arbitrary", "arbitrary")),
        name="adaln",
    )(cond, ada_w, ada_b.reshape(depth, 1, n))


def _norm_mod(x, g, shift, scale):
    ms = jnp.mean(x * x, axis=-1, keepdims=True)
    return (x * lax.rsqrt(ms + EPS) * g) * (1.0 + scale) + shift


def _rope(p, cos, sin_signed, first_half):
    rot = jnp.where(first_half, pltpu.roll(p, LANES - 16, axis=1), pltpu.roll(p, 16, axis=1))
    return p * cos + rot * sin_signed


def _inproj_kernel(x_ref, mod_ref, g_ref, w_ref, cs_ref, cos_ref, sin_ref,
                   fa_ref, fb_ref, q_ref, k_ref, v_ref):
    h = _norm_mod(x_ref[0], g_ref[...], mod_ref[0, 0:1, :], mod_ref[0, 1:2, :])
    p = jnp.dot(h.astype(bf16), w_ref[...], preferred_element_type=f32)
    fw = FOURIER_GROUPS * FOURIER_GROUP_W
    for g in range(FOURIER_GROUPS):
        ug = p[:, g * LANES:(g + 1) * LANES].astype(bf16)
        ab = jnp.dot(ug, cs_ref[...], preferred_element_type=f32)
        fa_ref[0, g] = ab[:, :LANES].astype(bf16)
        fb_ref[0, g] = ab[:, LANES:].astype(bf16)
    cos = cos_ref[...]
    sin = sin_ref[...]
    lane = lax.broadcasted_iota(i32, cos.shape, 1)
    first_half = (lane % 32) < 16
    qw = N_HEADS * HEAD_DIM
    for c in range(qw // LANES):
        qc = p[:, fw + c * LANES: fw + (c + 1) * LANES]
        q_ref[0, :, c * LANES:(c + 1) * LANES] = (
            _rope(qc, cos, sin, first_half) * (LOG2E * HEAD_DIM ** -0.5)).astype(bf16)
    k_ref[0] = _rope(p[:, fw + qw: fw + qw + LANES], cos, sin, first_half).astype(bf16)
    v_ref[0] = p[:, fw + qw + LANES:].astype(bf16)


def _inproj(x, mod, g, w, cs, cos_t, sin_t):
    b, l, d = x.shape
    tm = TOKEN_TILE
    n = w.shape[1]
    grp = pl.BlockSpec((1, FOURIER_GROUPS, tm, LANES), lambda i, j: (i, 0, j, 0))
    return pl.pallas_call(
        _inproj_kernel,
        grid=(b, l // tm),
        in_specs=[pl.BlockSpec((1, tm, d), lambda i, j: (i, j, 0)),
                  pl.BlockSpec((1, 6, d), lambda i, j: (i, 0, 0)),
                  pl.BlockSpec((1, d), lambda i, j: (0, 0)),
                  pl.BlockSpec((d, n), lambda i, j: (0, 0)),
                  pl.BlockSpec((LANES, 2 * LANES), lambda i, j: (0, 0)),
                  pl.BlockSpec((tm, LANES), lambda i, j: (j, 0)),
                  pl.BlockSpec((tm, LANES), lambda i, j: (j, 0))],
        out_specs=[grp, grp,
                   pl.BlockSpec((1, tm, N_HEADS * HEAD_DIM), lambda i, j: (i, j, 0)),
                   pl.BlockSpec((1, tm, LANES), lambda i, j: (i, j, 0)),
                   pl.BlockSpec((1, tm, LANES), lambda i, j: (i, j, 0))],
        out_shape=[jax.ShapeDtypeStruct((b, FOURIER_GROUPS, l, LANES), bf16),
                   jax.ShapeDtypeStruct((b, FOURIER_GROUPS, l, LANES), bf16),
                   jax.ShapeDtypeStruct((b, l, N_HEADS * HEAD_DIM), bf16),
                   jax.ShapeDtypeStruct((b, l, LANES), bf16),
                   jax.ShapeDtypeStruct((b, l, LANES), bf16)],
        compiler_params=_cparams(("parallel", "parallel")),
        name="inproj",
    )(x, mod, g, w, cs, cos_t, sin_t)


def _ctxkv_kernel(x_ref, mod_ref, g_ref, w_ref, k_ref, v_ref):
    h = _norm_mod(x_ref[0], g_ref[...], mod_ref[0, 0:1, :], mod_ref[0, 1:2, :])
    p = jnp.dot(h.astype(bf16), w_ref[...], preferred_element_type=f32)
    k_ref[0] = p[:, :LANES].astype(bf16)
    v_ref[0] = p[:, LANES:].astype(bf16)


def _ctxkv(ctx, mod, g, w_kv, ctx_row):
    b, c, d = ctx.shape
    return pl.pallas_call(
        _ctxkv_kernel,
        grid=(b,),
        in_specs=[pl.BlockSpec((1, c, d), lambda i: (i, 0, 0)),
                  pl.BlockSpec((1, 6, d), lambda i: (ctx_row, 0, 0)),
                  pl.BlockSpec((1, d), lambda i: (0, 0)),
                  pl.BlockSpec((d, 2 * LANES), lambda i: (0, 0))],
        out_specs=[pl.BlockSpec((1, c, LANES), lambda i: (i, 0, 0)),
                   pl.BlockSpec((1, c, LANES), lambda i: (i, 0, 0))],
        out_shape=[jax.ShapeDtypeStruct((b, c, LANES), bf16),
                   jax.ShapeDtypeStruct((b, c, LANES), bf16)],
        compiler_params=_cparams(("parallel",)),
        name="ctxkv",
    )(ctx, mod, g, w_kv)


def _attn_kernel(seq_len, q_ref, kp_ref, km_ref, kn_ref, vp_ref, vm_ref, vn_ref,
                 ck_ref, cv_ref, sink_ref, o_ref, kext, vext, cvext):
    j = pl.program_id(1)
    tq = ATT_TILE
    blk = ATT_BLOCK
    kext[0:blk] = kp_ref[0]
    kext[blk:blk + tq] = km_ref[0]
    kext[blk + tq:] = kn_ref[0]
    vext[:, LANES:] = jnp.ones((tq + 2 * blk, LANES), bf16)
    vext[0:blk, :LANES] = vp_ref[0]
    vext[blk:blk + tq, :LANES] = vm_ref[0]
    vext[blk + tq:, :LANES] = vn_ref[0]
    cvext[:, LANES:] = jnp.ones((cvext.shape[0], LANES), bf16)
    cvext[:, :LANES] = cv_ref[0]
    ck = ck_ref[0]
    sink = sink_ref[...]
    n_chunks = (N_HEADS * HEAD_DIM) // LANES
    rows = 2 * n_chunks * blk
    lane = lax.broadcasted_iota(i32, (blk, LANES), 1)
    low = lane < HEAD_DIM
    qi = lax.broadcasted_iota(i32, (rows, 3 * blk), 0) % blk
    pk = lax.broadcasted_iota(i32, (rows, 3 * blk), 1)
    band_bias = jnp.where(jnp.abs(pk - blk - qi) <= WINDOW, 0.0, NEG_INF).astype(f32)
    pcol = lax.broadcasted_iota(i32, (1, 3 * blk), 1)
    nt = (((1,), (1,)), ((), ()))

    def sub(s, carry):
        r0 = pl.multiple_of(s * blk, blk)
        qs = q_ref[0, pl.ds(r0, blk), :]
        parts = []
        for c in range(n_chunks):
            qc = qs[:, c * LANES:(c + 1) * LANES]
            parts.append(jnp.where(low, qc, jnp.zeros_like(qc)))
            parts.append(jnp.where(low, jnp.zeros_like(qc), qc))
        lhs = jnp.concatenate(parts, axis=0)
        kl = kext[pl.ds(r0, 3 * blk), :]
        vl = vext[pl.ds(r0, 3 * blk), :]
        kpos = j * tq + r0 - blk + pcol
        col_bias = jnp.where((kpos >= 0) & (kpos < seq_len), 0.0, NEG_INF).astype(f32)
        s_c = lax.dot_general(lhs, ck, nt, preferred_element_type=f32)
        s_l = lax.dot_general(lhs, kl, nt, preferred_element_type=f32) + band_bias + col_bias
        m = jnp.maximum(jnp.maximum(jnp.max(s_c, axis=1, keepdims=True),
                                    jnp.max(s_l, axis=1, keepdims=True)), sink)
        e_c = jnp.exp2(s_c - m).astype(bf16)
        e_l = jnp.exp2(s_l - m).astype(bf16)
        ov = (jnp.dot(e_c, cvext[...], preferred_element_type=f32)
              + jnp.dot(e_l, vl, preferred_element_type=f32))
        den = ov[:, LANES:LANES + 1] + jnp.exp2(sink - m)
        o = ov[:, :LANES] / den
        for c in range(n_chunks):
            oc = jnp.where(low, o[(2 * c) * blk:(2 * c + 1) * blk],
                           o[(2 * c + 1) * blk:(2 * c + 2) * blk])
            o_ref[0, pl.ds(r0, blk), c * LANES:(c + 1) * LANES] = oc.astype(bf16)
        return carry

    lax.fori_loop(0, tq // blk, sub, 0, unroll=2)


def _attention(q, k, v, ck, cv, sinkcol):
    b, l, qw = q.shape
    c = ck.shape[1]
    tq = ATT_TILE
    r = tq // ATT_BLOCK
    nb = l // ATT_BLOCK
    prev = pl.BlockSpec((1, ATT_BLOCK, LANES), lambda i, j: (i, jnp.maximum(j * r - 1, 0), 0))
    main = pl.BlockSpec((1, tq, LANES), lambda i, j: (i, j, 0))
    nxt = pl.BlockSpec((1, ATT_BLOCK, LANES), lambda i, j: (i, jnp.minimum(j * r + r, nb - 1), 0))
    cspec = pl.BlockSpec((1, c, LANES), lambda i, j: (i, 0, 0))
    return pl.pallas_call(
        functools.partial(_attn_kernel, l),
        grid=(b, l // tq),
        in_specs=[pl.BlockSpec((1, tq, qw), lambda i, j: (i, j, 0)),
                  prev, main, nxt, prev, main, nxt, cspec, cspec,
                  pl.BlockSpec(sinkcol.shape, lambda i, j: (0, 0))],
        out_specs=pl.BlockSpec((1, tq, qw), lambda i, j: (i, j, 0)),
        out_shape=jax.ShapeDtypeStruct((b, l, qw), bf16),
        scratch_shapes=[pltpu.VMEM((tq + 2 * ATT_BLOCK, LANES), bf16),
                        pltpu.VMEM((tq + 2 * ATT_BLOCK, 2 * LANES), bf16),
                        pltpu.VMEM((c, 2 * LANES), bf16)],
        compiler_params=_cparams(("parallel", "parallel")),
        name="attention",
    )(q, k, k, k, v, v, v, ck, cv, sinkcol)


def _dft1_kernel(a_ref, b_ref, m_ref, ct_ref, st_ref, o_ref):
    n1 = a_ref.shape[2]
    for t in range(a_ref.shape[3]):
        ab = jnp.concatenate([a_ref[0, 0, :, t, :], b_ref[0, 0, :, t, :]], axis=0)
        z = jnp.dot(m_ref[...], ab, preferred_element_type=f32)
        zr = z[:n1]
        zn = z[n1:]
        ct = ct_ref[t]
        st = st_ref[t]
        o_ref[0, 0, 0, t] = (ct * zr - st * zn).astype(bf16)
        o_ref[0, 0, 1, t] = (ct * zn + st * zr).astype(bf16)


def _dft2_kernel(scale, z_ref, m_ref, o_ref):
    y = jnp.dot(m_ref[...], z_ref[0, 0], preferred_element_type=f32)
    o_ref[0, 0] = (y * scale).astype(bf16)


def _fourier(fa, fb):
    b, g, l, w = fa.shape
    n2 = DFT_INNER
    n1 = l // n2
    t2 = 8
    k1 = jnp.arange(n1, dtype=i32)
    ang1 = ((k1[:, None] * k1[None, :]) % n1).astype(f32) * (2.0 * math.pi / n1)
    c1, s1 = jnp.cos(ang1), jnp.sin(ang1)
    m1 = jnp.concatenate([jnp.concatenate([c1, -s1], axis=1),
                          jnp.concatenate([s1, c1], axis=1)], axis=0).astype(bf16)
    l2 = jnp.arange(n2, dtype=i32)
    angt = ((l2[:, None] * k1[None, :]) % l).astype(f32) * (2.0 * math.pi / l)
    ct = jnp.broadcast_to(jnp.cos(angt)[:, :, None], (n2, n1, w))
    st = jnp.broadcast_to(jnp.sin(angt)[:, :, None], (n2, n1, w))
    ang2 = ((l2[:, None] * l2[None, :]) % n2).astype(f32) * (2.0 * math.pi / n2)
    m2 = jnp.concatenate([jnp.cos(ang2), -jnp.sin(ang2)], axis=1).astype(bf16)

    a5 = fa.reshape(b, g, n1, n2, w)
    b5 = fb.reshape(b, g, n1, n2, w)
    dspec = pl.BlockSpec((1, 1, n1, t2, w), lambda t, i, j: (i, j, 0, t, 0))
    tspec = pl.BlockSpec((t2, n1, w), lambda t, i, j: (t, 0, 0))
    z = pl.pallas_call(
        _dft1_kernel,
        grid=(n2 // t2, b, g),
        in_specs=[dspec, dspec, pl.BlockSpec((2 * n1, 2 * n1), lambda t, i, j: (0, 0)), tspec, tspec],
        out_specs=pl.BlockSpec((1, 1, 2, t2, n1, w), lambda t, i, j: (i, j, 0, t, 0, 0)),
        out_shape=jax.ShapeDtypeStruct((b, g, 2, n2, n1, w), bf16),
        compiler_params=_cparams(("parallel", "parallel", "parallel")),
        name="dft_outer",
    )(a5, b5, m1, ct, st)

    z2 = z.reshape(b, g, 2 * n2, n1 * w)
    tn = min(n1 * w, 4096)
    y = pl.pallas_call(
        functools.partial(_dft2_kernel, 1.0 / math.sqrt(l * w)),
        grid=(b, g, (n1 * w) // tn),
        in_specs=[pl.BlockSpec((1, 1, 2 * n2, tn), lambda i, j, t: (i, j, 0, t)),
                  pl.BlockSpec((n2, 2 * n2), lambda i, j, t: (0, 0))],
        out_specs=pl.BlockSpec((1, 1, n2, tn), lambda i, j, t: (i, j, 0, t)),
        out_shape=jax.ShapeDtypeStruct((b, g, n2, n1 * w), bf16),
        compiler_params=_cparams(("parallel", "parallel", "parallel")),
        name="dft_inner",
    )(z2, m2)
    return y.reshape(b, g, l, w)


def _first_max4(a):
    m = jnp.maximum(jnp.maximum(a[0], a[1]), jnp.maximum(a[2], a[3]))
    idx = jnp.where(a[0] == m, 0, jnp.where(a[1] == m, 1, jnp.where(a[2] == m, 2, 3)))
    return m, idx


def _pick4(vals, idx):
    return jnp.where(idx == 0, vals[0], jnp.where(idx == 1, vals[1],
                                                   jnp.where(idx == 2, vals[2], vals[3])))


def _route(f, rw_ref, rb_ref, tri_ref, base_ref, first_step, ri_ref, wc_ref, cnt_ref, meta_ref):
    tm = f.shape[0]
    f_hi = f.astype(bf16)
    f_lo = (f - f_hi.astype(f32)).astype(bf16)
    rw2 = rw_ref[...]
    part = jnp.dot(f_hi, rw2, preferred_element_type=f32)
    logits = (part[:, :LANES] + part[:, LANES:]
              + jnp.dot(f_lo, rw2[:, :LANES], preferred_element_type=f32))
    sc = jax.nn.sigmoid(logits)
    st = sc.T
    bt = (sc + rb_ref[...]).T
    neg = jnp.full((1, tm), -jnp.inf, f32)
    gs = []
    for g in range(N_GROUPS):
        a = [bt[4 * g + i: 4 * g + i + 1] for i in range(4)]
        m1, i1 = _first_max4(a)
        rest = [jnp.where(i1 == i, neg, a[i]) for i in range(4)]
        m2, _ = _first_max4(rest)
        gs.append(m1 + m2)
    _, gsel = _first_max4(gs)
    a = [_pick4([bt[4 * g + i: 4 * g + i + 1] for g in range(N_GROUPS)], gsel) for i in range(4)]
    s = [_pick4([st[4 * g + i: 4 * g + i + 1] for g in range(N_GROUPS)], gsel) for i in range(4)]
    _, i1 = _first_max4(a)
    rest = [jnp.where(i1 == i, neg, a[i]) for i in range(4)]
    _, i2 = _first_max4(rest)
    w1 = _pick4(s, i1)
    w2 = _pick4(s, i2)
    tot = w1 + w2
    w1 = w1 / tot
    w2 = w2 / tot
    e0 = gsel * EXPERTS_PER_GROUP + i1
    e1 = gsel * EXPERTS_PER_GROUP + i2

    @pl.when(first_step)
    def _():
        base_ref[...] = jnp.zeros_like(base_ref)

    td = DISPATCH_TILE
    eid = lax.broadcasted_iota(i32, (N_EXPERTS, tm), 0)
    oh0 = (eid == e0).astype(f32)
    oh1 = (eid == e1).astype(f32)
    oh = oh0 + oh1
    before = jnp.dot(oh.astype(bf16), tri_ref[...], preferred_element_type=f32)
    lane_tile = lax.broadcasted_iota(i32, (N_EXPERTS, tm), 1) // td
    ei = lax.broadcasted_iota(i32, (N_EXPERTS, N_EXPERTS), 0)
    ej = lax.broadcasted_iota(i32, (N_EXPERTS, N_EXPERTS), 1)
    strict_lower = (ej < ei).astype(f32)
    run_start = jnp.zeros((N_EXPERTS, tm), f32)
    goff = base_ref[...]
    for s in range(tm // td):
        cnt_s = jnp.sum(oh[:, s * td:(s + 1) * td], axis=1, keepdims=True)
        pad_s = jnp.floor((cnt_s + 7.0) * 0.125) * 8.0
        pad_b = jnp.broadcast_to(pad_s, (N_EXPERTS, LANES))
        start_b = jnp.dot(strict_lower, pad_b, precision=HIGHEST, preferred_element_type=f32)
        run_start = jnp.where(lane_tile == s, start_b[:, 0:1], run_start)
        meta_ref[s, 0] = start_b.astype(i32)
        meta_ref[s, 1] = pad_b.astype(i32)
        meta_ref[s, 2] = goff.astype(i32)
        goff = goff + pad_b
    base_ref[...] = goff
    cnt_ref[...] = goff
    pos = before + run_start
    lp0 = jnp.sum(oh0 * pos, axis=0, keepdims=True)
    lp1 = jnp.sum(oh1 * pos, axis=0, keepdims=True)
    zi = jnp.zeros((1, tm), i32)
    ri_ref[...] = jnp.concatenate(
        [lp0.astype(i32), lp1.astype(i32), e0, e1, zi, zi, zi, zi], axis=0)
    zf = jnp.zeros((LANES - 4, tm), f32)
    wc_ref[...] = jnp.concatenate([w1, w2, lp0, lp1, zf], axis=0).T


def _outproj_kernel(yf_ref, o_ref, x_ref, mod_ref, w_ref, g_ref, rw_ref, rb_ref, tri_ref,
                    x1_ref, f_ref, ri_ref, wc_ref, cnt_ref, meta_ref, base_ref):
    mix = jnp.concatenate([yf_ref[0, g] for g in range(FOURIER_GROUPS)] + [o_ref[0]], axis=1)
    y = jnp.dot(mix, w_ref[...], preferred_element_type=f32)
    x1 = x_ref[0] + mod_ref[0, 2:3, :] * y
    x1_ref[0] = x1
    f = _norm_mod(x1, g_ref[...], mod_ref[0, 3:4, :], mod_ref[0, 4:5, :])
    f_ref[0] = f.astype(bf16)
    first = (pl.program_id(0) == 0) & (pl.program_id(1) == 0)
    _route(f, rw_ref, rb_ref, tri_ref, base_ref, first, ri_ref, wc_ref, cnt_ref, meta_ref)


def _route_specs(b, l, tm):
    nl = l // tm
    rw = lambda d: pl.BlockSpec((d, 2 * LANES), lambda i, j: (0, 0))
    rb = pl.BlockSpec((1, LANES), lambda i, j: (0, 0))
    tri = pl.BlockSpec((tm, tm), lambda i, j: (0, 0))
    ns = tm // DISPATCH_TILE
    out_specs = [pl.BlockSpec((8, tm), lambda i, j: (0, i * nl + j)),
                 pl.BlockSpec((tm, LANES), lambda i, j: (i * nl + j, 0)),
                 pl.BlockSpec((N_EXPERTS, LANES), lambda i, j: (0, 0)),
                 pl.BlockSpec((ns, 3, N_EXPERTS, LANES), lambda i, j: (i * nl + j, 0, 0, 0))]
    out_shape = [jax.ShapeDtypeStruct((8, b * l), i32),
                 jax.ShapeDtypeStruct((b * l, LANES), f32),
                 jax.ShapeDtypeStruct((N_EXPERTS, LANES), f32),
                 jax.ShapeDtypeStruct((b * l // DISPATCH_TILE, 3, N_EXPERTS, LANES), i32)]
    return rw, rb, tri, out_specs, out_shape


def _outproj(yf, o, x, mod, w, g, rw, rb, tri):
    b, l, d = x.shape
    tm = TOKEN_TILE
    rws, rbs, tris, r_specs, r_shapes = _route_specs(b, l, tm)
    row = pl.BlockSpec((1, tm, d), lambda i, j: (i, j, 0))
    return pl.pallas_call(
        _outproj_kernel,
        grid=(b, l // tm),
        in_specs=[pl.BlockSpec((1, FOURIER_GROUPS, tm, LANES), lambda i, j: (i, 0, j, 0)),
                  pl.BlockSpec((1, tm, o.shape[2]), lambda i, j: (i, j, 0)),
                  row,
                  pl.BlockSpec((1, 6, d), lambda i, j: (i, 0, 0)),
                  pl.BlockSpec(w.shape, lambda i, j: (0, 0)),
                  pl.BlockSpec((1, d), lambda i, j: (0, 0)),
                  rws(d), rbs, tris],
        out_specs=[row, row] + r_specs,
        out_shape=[jax.ShapeDtypeStruct((b, l, d), f32),
                   jax.ShapeDtypeStruct((b, l, d), bf16)] + r_shapes,
        scratch_shapes=[pltpu.VMEM((N_EXPERTS, LANES), f32)],
        compiler_params=_cparams(("arbitrary", "arbitrary")),
        name="outproj_router",
    )(yf, o, x, mod, w, g, rw, rb, tri)


def _glu_kernel(x_ref, mod_ref, g_ref, w_ref, b_ref, u_ref):
    h = _norm_mod(x_ref[0], g_ref[...], mod_ref[0, 0:1, :], mod_ref[0, 1:2, :])
    p = jnp.dot(h.astype(bf16), w_ref[...], preferred_element_type=f32) + b_ref[...]
    ch = p.shape[1] // 2
    u_ref[0] = p[:, :ch] * jax.nn.sigmoid(p[:, ch:])


def _glu(x, mod, g, w, bias):
    b, l, d = x.shape
    tm = TOKEN_TILE
    n = w.shape[1]
    return pl.pallas_call(
        _glu_kernel,
        grid=(b, l // tm),
        in_specs=[pl.BlockSpec((1, tm, d), lambda i, j: (i, j, 0)),
                  pl.BlockSpec((1, 6, d), lambda i, j: (i, 0, 0)),
                  pl.BlockSpec((1, d), lambda i, j: (0, 0)),
                  pl.BlockSpec((d, n), lambda i, j: (0, 0)),
                  pl.BlockSpec((1, n), lambda i, j: (0, 0))],
        out_specs=pl.BlockSpec((1, tm, n // 2), lambda i, j: (i, j, 0)),
        out_shape=jax.ShapeDtypeStruct((b, l, n // 2), f32),
        compiler_params=_cparams(("parallel", "parallel")),
        name="pw1_glu",
    )(x, mod, g, w, bias)


def _conv_kernel(seq_len, up_ref, um_ref, un_ref, x_ref, mod_ref, dw_ref, db_ref, lg_ref, lb_ref,
                 w_ref, pb_ref, g_ref, rw_ref, rb_ref, tri_ref,
                 x1_ref, f_ref, ri_ref, wc_ref, cnt_ref, meta_ref, base_ref, ext):
    j = pl.program_id(1)
    tm = um_ref.shape[1]
    hl = CONV_HALO
    half = CONV_W // 2
    ext[0:hl] = jnp.where(j > 0, up_ref[0], jnp.zeros_like(up_ref[0]))
    ext[hl:hl + tm] = um_ref[0]
    ext[hl + tm:] = jnp.where((j + 1) * tm < seq_len, un_ref[0], jnp.zeros_like(un_ref[0]))
    acc = jnp.zeros(um_ref.shape[1:], f32) + db_ref[...]
    base = hl - half
    span = (CONV_W - 1) // SUBLANES * SUBLANES
    for phase in range(SUBLANES):
        win = ext[base + phase: base + phase + tm + span, :]
        for t in range(phase, CONV_W, SUBLANES):
            acc = acc + win[t - phase: t - phase + tm, :] * dw_ref[t:t + 1, :]
    mu = jnp.mean(acc, axis=-1, keepdims=True)
    cen = acc - mu
    var = jnp.mean(cen * cen, axis=-1, keepdims=True)
    ln = cen * lax.rsqrt(var + EPS) * lg_ref[...] + lb_ref[...]
    act = ln * jax.nn.sigmoid(ln)
    y = jnp.dot(act.astype(bf16), w_ref[...], preferred_element_type=f32) + pb_ref[...]
    x1 = x_ref[0] + mod_ref[0, 2:3, :] * y
    x1_ref[0] = x1
    f = _norm_mod(x1, g_ref[...], mod_ref[0, 3:4, :], mod_ref[0, 4:5, :])
    f_ref[0] = f.astype(bf16)
    first = (pl.program_id(0) == 0) & (j == 0)
    _route(f, rw_ref, rb_ref, tri_ref, base_ref, first, ri_ref, wc_ref, cnt_ref, meta_ref)


def _conv(u, x, mod, dw_w, dw_b, ln_g, ln_b, pw2_w, pw2_b, g, rw, rb, tri):
    b, l, d = x.shape
    tm = TOKEN_TILE
    hl = CONV_HALO
    r = tm // hl
    nh = l // hl
    rws, rbs, tris, r_specs, r_shapes = _route_specs(b, l, tm)
    row = pl.BlockSpec((1, tm, d), lambda i, j: (i, j, 0))
    vec = pl.BlockSpec((1, d), lambda i, j: (0, 0))
    return pl.pallas_call(
        functools.partial(_conv_kernel, l),
        grid=(b, l // tm),
        in_specs=[pl.BlockSpec((1, hl, d), lambda i, j: (i, jnp.maximum(j * r - 1, 0), 0)),
                  row,
                  pl.BlockSpec((1, hl, d), lambda i, j: (i, jnp.minimum(j * r + r, nh - 1), 0)),
                  row,
                  pl.BlockSpec((1, 6, d), lambda i, j: (i, 0, 0)),
                  pl.BlockSpec(dw_w.shape, lambda i, j: (0, 0)),
                  vec, vec, vec,
                  pl.BlockSpec(pw2_w.shape, lambda i, j: (0, 0)),
                  vec, vec, rws(d), rbs, tris],
        out_specs=[row, row] + r_specs,
        out_shape=[jax.ShapeDtypeStruct((b, l, d), f32),
                   jax.ShapeDtypeStruct((b, l, d), bf16)] + r_shapes,
        scratch_shapes=[pltpu.VMEM((N_EXPERTS, LANES), f32),
                        pltpu.VMEM((tm + 2 * hl, d), f32)],
        compiler_params=_cparams(("arbitrary", "arbitrary")),
        name="conv_router",
    )(u, u, u, x, mod, dw_w, dw_b, ln_g, ln_b, pw2_w, pw2_b, g, rw, rb, tri)


def _pack_bf16_pairs(x):
    h = x.shape[1] // 2
    lo = lax.bitcast_convert_type(x[:, :h], u32)
    hi = lax.bitcast_convert_type(x[:, h:], u32)
    return (lo >> 16) | (hi & jnp.uint32(0xFFFF0000))


def _unpack_bf16_pairs(u):
    lo = lax.bitcast_convert_type(u << 16, f32)
    hi = lax.bitcast_convert_type(u & jnp.uint32(0xFFFF0000), f32)
    return jnp.concatenate([lo, hi], axis=1).astype(bf16)


def _run_copies(meta, tile, local_ref, hbm_ref, sem, to_hbm):
    start_ref, size_ref, dst_ref = meta
    for e in range(N_EXPERTS):
        k = tile * N_EXPERTS + e
        size = pl.multiple_of(size_ref[k], RUN_ALIGN)

        @pl.when(size > 0)
        def _():
            loc = local_ref.at[pl.ds(pl.multiple_of(start_ref[k], RUN_ALIGN), size)]
            glob = hbm_ref.at[pl.ds(pl.multiple_of(dst_ref[k], RUN_ALIGN), size)]
            if to_hbm:
                pltpu.make_async_copy(loc, glob, sem).start()
            else:
                pltpu.make_async_copy(glob, loc, sem).start()


def _wait_rows(rows, local_ref, hbm_ref, sem):
    rows = pl.multiple_of(rows, RUN_ALIGN)

    @pl.when(rows > 0)
    def _():
        pltpu.make_async_copy(local_ref.at[pl.ds(0, rows)], hbm_ref.at[pl.ds(0, rows)], sem).wait()


def _dispatch_kernel(start_ref, size_ref, dst_ref, tot_ref, tail_start_ref, tail_size_ref, nv_ref,
                     f_ref, lp_ref, xs_ref, loc, zbuf, sem, zsem):
    i = pl.program_id(0)
    n = pl.num_programs(0)
    slot = i % 2
    meta = (start_ref, size_ref, dst_ref)

    @pl.when(i >= 2)
    def _():
        _wait_rows(tot_ref[i - 2], loc.at[slot], xs_ref, sem.at[slot])

    rows = loc.shape[1]
    td = f_ref.shape[0]
    r = lax.broadcasted_iota(i32, (rows, td), 0)
    onehot = ((r == lp_ref[0:1, :]) | (r == lp_ref[1:2, :])).astype(bf16)
    sorted_rows = jnp.dot(onehot, f_ref[...], preferred_element_type=f32)
    loc[slot] = _pack_bf16_pairs(sorted_rows)
    _run_copies(meta, i, loc.at[slot], xs_ref, sem.at[slot], to_hbm=True)

    @pl.when(i == n - 1)
    def _():
        zbuf[...] = jnp.zeros_like(zbuf)
        total = 0
        for e in range(N_EXPERTS):
            size = pl.multiple_of(tail_size_ref[e], RUN_ALIGN)
            total = total + size

            @pl.when(size > 0)
            def _():
                pltpu.make_async_copy(
                    zbuf.at[pl.ds(0, size)],
                    xs_ref.at[pl.ds(pl.multiple_of(tail_start_ref[e], RUN_ALIGN), size)], zsem).start()

        _wait_rows(total, zbuf, xs_ref, zsem)

        def zero_block(k, c):
            pltpu.make_async_copy(zbuf, xs_ref.at[pl.ds(pl.multiple_of(k * zbuf.shape[0], RUN_ALIGN),
                                                        zbuf.shape[0])], zsem).start()
            return c

        def wait_block(k, c):
            pltpu.make_async_copy(zbuf, xs_ref.at[pl.ds(0, zbuf.shape[0])], zsem).wait()
            return c

        n_blocks = xs_ref.shape[0] // zbuf.shape[0]
        lax.fori_loop(nv_ref[0], n_blocks, zero_block, 0)
        lax.fori_loop(nv_ref[0], n_blocks, wait_block, 0)
        _wait_rows(tot_ref[i], loc.at[slot], xs_ref, sem.at[slot])

        @pl.when(i >= 1)
        def _():
            _wait_rows(tot_ref[i - 1], loc.at[1 - slot], xs_ref, sem.at[1 - slot])


def _dispatch(tables, f2, ri, n_slots):
    t, d = f2.shape
    td = DISPATCH_TILE
    return pl.pallas_call(
        _dispatch_kernel,
        grid_spec=pltpu.PrefetchScalarGridSpec(
            num_scalar_prefetch=7,
            grid=(t // td,),
            in_specs=[pl.BlockSpec((td, d), lambda i, *_: (i, 0)),
                      pl.BlockSpec((8, td), lambda i, *_: (0, i))],
            out_specs=pl.BlockSpec(memory_space=pl.ANY),
            scratch_shapes=[pltpu.VMEM((2, LOCAL_ROWS, d // 2), u32),
                            pltpu.VMEM((EXPERT_ROWS, d // 2), u32),
                            pltpu.SemaphoreType.DMA((2,)), pltpu.SemaphoreType.DMA(())]),
        out_shape=jax.ShapeDtypeStruct((n_slots, d // 2), u32),
        compiler_params=_cparams(("arbitrary",)),
        name="moe_dispatch",
    )(*tables, f2, ri)


def _expert_kernel(be_ref, nv_ref, x_ref, wg_ref, wu_ref, wd_ref, y_ref, wgb, wub, wdb):
    i = pl.program_id(0)
    changed = jnp.logical_or(i == 0, be_ref[i] != be_ref[jnp.maximum(i - 1, 0)])

    @pl.when(changed)
    def _():
        wgb[...] = wg_ref[0, 0].astype(bf16)
        wub[...] = wu_ref[0, 0].astype(bf16)
        wdb[...] = wd_ref[0, 0].astype(bf16)

    @pl.when(i < nv_ref[0])
    def _():
        xb = _unpack_bf16_pairs(x_ref[...])
        gate = jnp.dot(xb, wgb[...], preferred_element_type=f32)
        up = jnp.dot(xb, wub[...], preferred_element_type=f32)
        hid = (gate * jax.nn.sigmoid(gate) * up).astype(bf16)
        y = jnp.dot(hid, wdb[...], preferred_element_type=f32)
        y_ref[...] = _pack_bf16_pairs(y.astype(bf16).astype(f32))

    @pl.when(i >= nv_ref[0])
    def _():
        y_ref[...] = jnp.zeros_like(y_ref)


def _experts(block_e, n_valid, xs, w_gate, w_up, w_down, layer):
    ns, dh = xs.shape
    tb = EXPERT_ROWS
    d, ff = w_gate.shape[2:]
    xmap = lambda i, be, nv: (jnp.maximum(jnp.minimum(i, nv[0] - 1), 0), 0)
    wmap = lambda i, be, nv: (layer, be[i], 0, 0)
    return pl.pallas_call(
        _expert_kernel,
        grid_spec=pltpu.PrefetchScalarGridSpec(
            num_scalar_prefetch=2,
            grid=(ns // tb,),
            in_specs=[pl.BlockSpec((tb, dh), xmap),
                      pl.BlockSpec((1, 1, d, ff), wmap),
                      pl.BlockSpec((1, 1, d, ff), wmap),
                      pl.BlockSpec((1, 1, ff, d), wmap)],
            out_specs=pl.BlockSpec((tb, dh), lambda i, be, nv: (i, 0)),
            scratch_shapes=[pltpu.VMEM((d, ff), bf16), pltpu.VMEM((d, ff), bf16),
                            pltpu.VMEM((ff, d), bf16)]),
        out_shape=jax.ShapeDtypeStruct((ns, dh), u32),
        compiler_params=_cparams(("arbitrary",)),
        name="moe_experts",
    )(block_e, n_valid, xs, w_gate, w_up, w_down)


def _combine_kernel(final, start_ref, size_ref, dst_ref, tot_ref, ys_ref, wc_ref, x_ref, mod_ref,
                    g_ref, o_ref, loc, sem):
    i = pl.program_id(0)
    n = pl.num_programs(0)
    slot = i % 2
    meta = (start_ref, size_ref, dst_ref)

    @pl.when(i == 0)
    def _():
        loc[...] = jnp.zeros_like(loc)
        _run_copies(meta, i, loc.at[slot], ys_ref, sem.at[slot], to_hbm=False)

    @pl.when(i + 1 < n)
    def _():
        _run_copies(meta, i + 1, loc.at[1 - slot], ys_ref, sem.at[1 - slot], to_hbm=False)

    _wait_rows(tot_ref[i], loc.at[slot], ys_ref, sem.at[slot])
    rows = loc.shape[1]
    td = x_ref.shape[0]
    wc = wc_ref[...]
    c = lax.broadcasted_iota(i32, (td, rows), 1)
    sel = jnp.concatenate([(c == wc[:, 2:3].astype(i32)).astype(bf16),
                           (c == wc[:, 3:4].astype(i32)).astype(bf16)], axis=0)
    picked = jnp.dot(sel, _unpack_bf16_pairs(loc[slot]), preferred_element_type=f32)
    y = wc[:, 0:1] * picked[:td] + wc[:, 1:2] * picked[td:]
    xo = x_ref[...] + mod_ref[0, 5:6, :] * y
    if final:
        ms = jnp.mean(xo * xo, axis=-1, keepdims=True)
        xo = xo * lax.rsqrt(ms + EPS) * g_ref[...]
    o_ref[...] = xo


def _combine(tables, ys, wc, x, mod, g, final):
    b, l, d = x.shape
    td = DISPATCH_TILE
    per_batch = l // td
    out = pl.pallas_call(
        functools.partial(_combine_kernel, final),
        grid_spec=pltpu.PrefetchScalarGridSpec(
            num_scalar_prefetch=4,
            grid=(b * per_batch,),
            in_specs=[pl.BlockSpec(memory_space=pl.ANY),
                      pl.BlockSpec((td, LANES), lambda i, *_: (i, 0)),
                      pl.BlockSpec((td, d), lambda i, *_: (i, 0)),
                      pl.BlockSpec((1, 6, d), lambda i, *_: (i // per_batch, 0, 0)),
                      pl.BlockSpec((1, d), lambda i, *_: (0, 0))],
            out_specs=pl.BlockSpec((td, d), lambda i, *_: (i, 0)),
            scratch_shapes=[pltpu.VMEM((2, LOCAL_ROWS, d // 2), u32),
                            pltpu.SemaphoreType.DMA((2,))]),
        out_shape=jax.ShapeDtypeStruct((b * l, d), f32),
        compiler_params=_cparams(("arbitrary",)),
        name="moe_combine",
    )(*tables, ys, wc, x.reshape(b * l, d), mod, g)
    return out.reshape(b, l, d)


def _moe(f, routed, x, mod, g_final, w_gate, w_up, w_down, layer, final):
    ri, wc, cnt, meta = routed
    b, l, d = x.shape
    t = b * l
    tb = EXPERT_ROWS
    n_tiles = t // DISPATCH_TILE
    used = cnt[:, 0].astype(i32)
    region = (used + tb - 1) // tb * tb
    gend = jnp.cumsum(region)
    gstart = gend - region
    max_rows = 2 * t + n_tiles * N_EXPERTS * (RUN_ALIGN - 1) + N_EXPERTS * (tb - 1)
    n_blocks = -(-max_rows // tb)
    m = meta[:, :, :, 0]
    run_start = m[:, 0].reshape(-1)
    run_size = m[:, 1].reshape(-1)
    run_dst = (m[:, 2] + gstart[None, :]).reshape(-1)
    tile_rows = jnp.sum(m[:, 1], axis=1)
    block_row = jnp.arange(n_blocks, dtype=i32) * tb
    block_e = jnp.minimum(jnp.sum((block_row[:, None] >= gend[None, :]).astype(i32), axis=1),
                          N_EXPERTS - 1)
    n_valid = (gend[-1] // tb).reshape(1)
    xs = _dispatch((run_start, run_size, run_dst, tile_rows, gstart + used, region - used, n_valid),
                   f.reshape(t, d), ri, n_blocks * tb)
    ys = _experts(block_e, n_valid, xs, w_gate, w_up, w_down, layer)
    return _combine((run_start, run_size, run_dst, tile_rows), ys, wc, x, mod, g_final, final)


def _rope_tables(l):
    lane = jnp.arange(LANES)
    dh = lane % HEAD_DIM
    inv = ROPE_THETA ** (-(dh % 16).astype(f32) / 16.0)
    pos = jnp.arange(l)
    row = (pos // GRID_W).astype(f32)
    col = (pos % GRID_W).astype(f32)
    p = jnp.where((dh // 32)[None, :] == 0, row[:, None], col[:, None])
    ang = p * inv[None, :]
    sign = jnp.where((dh % 32) < 16, -1.0, 1.0).astype(f32)
    return jnp.cos(ang), jnp.sin(ang) * sign[None, :]


def kernel(x, c, ctx, c_ctx, ada_w, ada_b, norm_mix_g, norm_ffn_g, even_w_in, even_w_out, even_sink, conv_pw1_w, conv_pw1_b, conv_dw_w, conv_dw_b, conv_ln_g, conv_ln_b, conv_pw2_w, conv_pw2_b, router_w, router_b, moe_w_gate, moe_w_up, moe_w_down, final_norm_g):
    b, l, d = x.shape
    depth = ada_w.shape[0]
    assert depth == 2 and b < COND_ROWS
    ctx_row = b
    cond = jnp.zeros((COND_ROWS, d), f32).at[:b].set(c).at[ctx_row].set(c_ctx)
    mods = _adaln(cond, ada_w, ada_b).reshape(depth, COND_ROWS, 6, d)

    heads = jnp.arange(N_HEADS).reshape(N_KV_HEADS, N_HEADS // N_KV_HEADS).T.reshape(-1)
    qperm = (heads[:, None] * HEAD_DIM + jnp.arange(HEAD_DIM)[None, :]).reshape(-1)
    fw = FOURIER_GROUPS * FOURIER_GROUP_W
    qw = N_HEADS * HEAD_DIM
    w_in = even_w_in[0]
    w_in_p = jnp.concatenate([w_in[:, :fw], w_in[:, fw:fw + qw][:, qperm], w_in[:, fw + qw:]],
                             axis=1).astype(bf16)
    w_out = even_w_out[0]
    w_out_p = jnp.concatenate([w_out[:fw], w_out[fw:][qperm]], axis=0).astype(bf16)
    sinkcol = jnp.repeat(even_sink[0][heads].astype(f32) * LOG2E, ATT_BLOCK).reshape(-1, 1)

    cidx = jnp.arange(FOURIER_GROUP_W, dtype=i32)
    angc = ((cidx[:, None] * cidx[None, :]) % FOURIER_GROUP_W).astype(f32) * (2.0 * math.pi / FOURIER_GROUP_W)
    cs = jnp.concatenate([jnp.cos(angc), jnp.sin(angc)], axis=1).astype(bf16)
    cos_t, sin_t = _rope_tables(l)

    rw32 = jnp.zeros((d, LANES), f32).at[:, :N_EXPERTS].set(router_w.astype(f32))
    rw_hi = rw32.astype(bf16)
    rw = jnp.concatenate([rw_hi, (rw32 - rw_hi.astype(f32)).astype(bf16)], axis=1)
    rb = jnp.zeros((1, LANES), f32).at[0, :N_EXPERTS].set(router_b)
    tpos = jnp.arange(TOKEN_TILE)
    tri = ((tpos[:, None] < tpos[None, :])
           & (tpos[:, None] // DISPATCH_TILE == tpos[None, :] // DISPATCH_TILE)).astype(bf16)
    row = lambda v: v.reshape(1, -1)

    fa, fb, q, k, v = _inproj(x, mods[0], row(norm_mix_g[0]), w_in_p, cs, cos_t, sin_t)
    ck, cv = _ctxkv(ctx, mods[0], row(norm_mix_g[0]), w_in_p[:, fw + qw:], ctx_row)
    yf = _fourier(fa, fb)
    att = _attention(q, k, v, ck, cv, sinkcol)
    x1, f, *routed = _outproj(yf, att, x, mods[0], w_out_p, row(norm_ffn_g[0]), rw, rb, tri)
    x2 = _moe(f, routed, x1, mods[0], row(final_norm_g), moe_w_gate, moe_w_up, moe_w_down,
              layer=0, final=False)

    u = _glu(x2, mods[1], row(norm_mix_g[1]), conv_pw1_w[0].astype(bf16), row(conv_pw1_b[0]))
    x3, f, *routed = _conv(u, x2, mods[1], conv_dw_w[0], row(conv_dw_b[0]), row(conv_ln_g[0]),
                               row(conv_ln_b[0]), conv_pw2_w[0].astype(bf16), row(conv_pw2_b[0]),
                               row(norm_ffn_g[1]), rw, rb, tri)
    return _moe(f, routed, x3, mods[1], row(final_norm_g), moe_w_gate, moe_w_up, moe_w_down,
                layer=1, final=True)
```

```python
import functools
import math

import jax
import jax.numpy as jnp
from jax import lax
from jax.experimental import pallas as pl
from jax.experimental.pallas import tpu as pltpu

f32 = jnp.float32
bf16 = jnp.bfloat16
i32 = jnp.int32
u32 = jnp.uint32
HIGHEST = lax.Precision.HIGHEST

GRID_W = 64
HEAD_DIM = 64
N_HEADS = 8
N_KV_HEADS = 2
WINDOW = 128
ATT_BLOCK = 128
ROPE_THETA = 10000.0
FOURIER_GROUPS = 4
FOURIER_GROUP_W = 128
CONV_W = 31
N_EXPERTS = 16
N_GROUPS = 4
EXPERTS_PER_GROUP = 4
EXPERT_FF = 512
EPS = 1e-6
NEG_INF = -1e30
LOG2E = math.log2(math.e)

LANES = 128
SUBLANES = 8
COND_ROWS = 8
DFT_INNER = 64
TOKEN_TILE = 512
ATT_TILE = 512
EXPERT_ROWS = 512
DISPATCH_TILE = 256
RUN_ALIGN = 8
LOCAL_ROWS = -(-(2 * DISPATCH_TILE + N_EXPERTS * (RUN_ALIGN - 1)) // LANES) * LANES
CONV_HALO = 16
CONV_ROWS = 128
VMEM_LIMIT = 56 * 1024 * 1024


def _cparams(sem, vmem=VMEM_LIMIT):
    return pltpu.CompilerParams(dimension_semantics=sem, vmem_limit_bytes=vmem)


def _adaln_kernel(cond_ref, w_ref, b_ref, o_ref):
    s = cond_ref[...]
    s = s * jax.nn.sigmoid(s)
    o_ref[0] = jnp.dot(s, w_ref[0], precision=HIGHEST, preferred_element_type=f32) + b_ref[0]


def _adaln(cond, ada_w, ada_b):
    depth, d, n = ada_w.shape
    tn = 1536
    return pl.pallas_call(
        _adaln_kernel,
        grid=(depth, n // tn),
        in_specs=[pl.BlockSpec((COND_ROWS, d), lambda i, j: (0, 0)),
                  pl.BlockSpec((1, d, tn), lambda i, j: (i, 0, j)),
                  pl.BlockSpec((1, 1, tn), lambda i, j: (i, 0, j))],
        out_specs=pl.BlockSpec((1, COND_ROWS, tn), lambda i, j: (i, 0, j)),
        out_shape=jax.ShapeDtypeStruct((depth, COND_ROWS, n), f32),
        compiler_params=_cparams(("arbitrary", "arbitrary")),
        name="adaln",
    )(cond, ada_w, ada_b.reshape(depth, 1, n))


def _norm_mod(x, g, shift, scale):
    ms = jnp.mean(x * x, axis=-1, keepdims=True)
    return (x * lax.rsqrt(ms + EPS) * g) * (1.0 + scale) + shift


def _rope(p, cos, sin_signed, first_half):
    rot = jnp.where(first_half, pltpu.roll(p, LANES - 16, axis=1), pltpu.roll(p, 16, axis=1))
    return p * cos + rot * sin_signed


def _inproj_kernel(x_ref, mod_ref, g_ref, w_ref, cos_ref, sin_ref, q_ref, k_ref, v_ref):
    h = _norm_mod(x_ref[0], g_ref[...], mod_ref[0, 0:1, :], mod_ref[0, 1:2, :])
    p = jnp.dot(h.astype(bf16), w_ref[...], preferred_element_type=f32)
    cos = cos_ref[...]
    sin = sin_ref[...]
    lane = lax.broadcasted_iota(i32, cos.shape, 1)
    first_half = (lane % 32) < 16
    qw = N_HEADS * HEAD_DIM
    for c in range(qw // LANES):
        qc = p[:, c * LANES:(c + 1) * LANES]
        q_ref[0, :, c * LANES:(c + 1) * LANES] = (
            _rope(qc, cos, sin, first_half) * (LOG2E * HEAD_DIM ** -0.5)).astype(bf16)
    k_ref[0] = _rope(p[:, qw:qw + LANES], cos, sin, first_half).astype(bf16)
    v_ref[0] = p[:, qw + LANES:].astype(bf16)


def _inproj(x, mod, g, w, cos_t, sin_t):
    b, l, d = x.shape
    tm = TOKEN_TILE
    n = w.shape[1]
    return pl.pallas_call(
        _inproj_kernel,
        grid=(b, l // tm),
        in_specs=[pl.BlockSpec((1, tm, d), lambda i, j: (i, j, 0)),
                  pl.BlockSpec((1, 6, d), lambda i, j: (i, 0, 0)),
                  pl.BlockSpec((1, d), lambda i, j: (0, 0)),
                  pl.BlockSpec((d, n), lambda i, j: (0, 0)),
                  pl.BlockSpec((tm, LANES), lambda i, j: (j, 0)),
                  pl.BlockSpec((tm, LANES), lambda i, j: (j, 0))],
        out_specs=[pl.BlockSpec((1, tm, N_HEADS * HEAD_DIM), lambda i, j: (i, j, 0)),
                   pl.BlockSpec((1, tm, LANES), lambda i, j: (i, j, 0)),
                   pl.BlockSpec((1, tm, LANES), lambda i, j: (i, j, 0))],
        out_shape=[jax.ShapeDtypeStruct((b, l, N_HEADS * HEAD_DIM), bf16),
                   jax.ShapeDtypeStruct((b, l, LANES), bf16),
                   jax.ShapeDtypeStruct((b, l, LANES), bf16)],
        compiler_params=_cparams(("parallel", "parallel")),
        name="inproj",
    )(x, mod, g, w, cos_t, sin_t)


def _ctxkv_kernel(x_ref, mod_ref, g_ref, w_ref, k_ref, v_ref):
    h = _norm_mod(x_ref[0], g_ref[...], mod_ref[0, 0:1, :], mod_ref[0, 1:2, :])
    p = jnp.dot(h.astype(bf16), w_ref[...], preferred_element_type=f32)
    k_ref[0] = p[:, :LANES].astype(bf16)
    v_ref[0] = p[:, LANES:].astype(bf16)


def _ctxkv(ctx, mod, g, w_kv, ctx_row):
    b, c, d = ctx.shape
    return pl.pallas_call(
        _ctxkv_kernel,
        grid=(b,),
        in_specs=[pl.BlockSpec((1, c, d), lambda i: (i, 0, 0)),
                  pl.BlockSpec((1, 6, d), lambda i: (ctx_row, 0, 0)),
                  pl.BlockSpec((1, d), lambda i: (0, 0)),
                  pl.BlockSpec((d, 2 * LANES), lambda i: (0, 0))],
        out_specs=[pl.BlockSpec((1, c, LANES), lambda i: (i, 0, 0)),
                   pl.BlockSpec((1, c, LANES), lambda i: (i, 0, 0))],
        out_shape=[jax.ShapeDtypeStruct((b, c, LANES), bf16),
                   jax.ShapeDtypeStruct((b, c, LANES), bf16)],
        compiler_params=_cparams(("parallel",)),
        name="ctxkv",
    )(ctx, mod, g, w_kv)


def _attn_kernel(seq_len, q_ref, kp_ref, km_ref, kn_ref, vp_ref, vm_ref, vn_ref,
                 ck_ref, cv_ref, sink_ref, o_ref, kext, vext, cvext):
    j = pl.program_id(1)
    tq = ATT_TILE
    blk = ATT_BLOCK
    kext[0:blk] = kp_ref[0]
    kext[blk:blk + tq] = km_ref[0]
    kext[blk + tq:] = kn_ref[0]
    vext[:, LANES:] = jnp.ones((tq + 2 * blk, LANES), bf16)
    vext[0:blk, :LANES] = vp_ref[0]
    vext[blk:blk + tq, :LANES] = vm_ref[0]
    vext[blk + tq:, :LANES] = vn_ref[0]
    cvext[:, LANES:] = jnp.ones((cvext.shape[0], LANES), bf16)
    cvext[:, :LANES] = cv_ref[0]
    ck = ck_ref[0]
    sink = sink_ref[...]
    n_chunks = (N_HEADS * HEAD_DIM) // LANES
    rows = 2 * n_chunks * blk
    lane = lax.broadcasted_iota(i32, (blk, LANES), 1)
    low = lane < HEAD_DIM
    qi = lax.broadcasted_iota(i32, (rows, 3 * blk), 0) % blk
    pk = lax.broadcasted_iota(i32, (rows, 3 * blk), 1)
    band_bias = jnp.where(jnp.abs(pk - blk - qi) <= WINDOW, 0.0, NEG_INF).astype(f32)
    pcol = lax.broadcasted_iota(i32, (1, 3 * blk), 1)
    nt = (((1,), (1,)), ((), ()))

    def sub(s, carry):
        r0 = pl.multiple_of(s * blk, blk)
        qs = q_ref[0, pl.ds(r0, blk), :]
        parts = []
        for c in range(n_chunks):
            qc = qs[:, c * LANES:(c + 1) * LANES]
            parts.append(jnp.where(low, qc, jnp.zeros_like(qc)))
            parts.append(jnp.where(low, jnp.zeros_like(qc), qc))
        lhs = jnp.concatenate(parts, axis=0)
        kl = kext[pl.ds(r0, 3 * blk), :]
        vl = vext[pl.ds(r0, 3 * blk), :]
        kpos = j * tq + r0 - blk + pcol
        col_bias = jnp.where((kpos >= 0) & (kpos < seq_len), 0.0, NEG_INF).astype(f32)
        s_c = lax.dot_general(lhs, ck, nt, preferred_element_type=f32)
        s_l = lax.dot_general(lhs, kl, nt, preferred_element_type=f32) + band_bias + col_bias
        m = jnp.maximum(jnp.maximum(jnp.max(s_c, axis=1, keepdims=True),
                                    jnp.max(s_l, axis=1, keepdims=True)), sink)
        e_c = jnp.exp2(s_c - m).astype(bf16)
        e_l = jnp.exp2(s_l - m).astype(bf16)
        ov = (jnp.dot(e_c, cvext[...], preferred_element_type=f32)
              + jnp.dot(e_l, vl, preferred_element_type=f32))
        den = ov[:, LANES:LANES + 1] + jnp.exp2(sink - m)
        o = ov[:, :LANES] / den
        for c in range(n_chunks):
            oc = jnp.where(low, o[(2 * c) * blk:(2 * c + 1) * blk],
                           o[(2 * c + 1) * blk:(2 * c + 2) * blk])
            o_ref[0, pl.ds(r0, blk), c * LANES:(c + 1) * LANES] = oc.astype(bf16)
        return carry

    lax.fori_loop(0, tq // blk, sub, 0, unroll=2)


def _attention(q, k, v, ck, cv, sinkcol):
    b, l, qw = q.shape
    c = ck.shape[1]
    tq = ATT_TILE
    r = tq // ATT_BLOCK
    nb = l // ATT_BLOCK
    prev = pl.BlockSpec((1, ATT_BLOCK, LANES), lambda i, j: (i, jnp.maximum(j * r - 1, 0), 0))
    main = pl.BlockSpec((1, tq, LANES), lambda i, j: (i, j, 0))
    nxt = pl.BlockSpec((1, ATT_BLOCK, LANES), lambda i, j: (i, jnp.minimum(j * r + r, nb - 1), 0))
    cspec = pl.BlockSpec((1, c, LANES), lambda i, j: (i, 0, 0))
    return pl.pallas_call(
        functools.partial(_attn_kernel, l),
        grid=(b, l // tq),
        in_specs=[pl.BlockSpec((1, tq, qw), lambda i, j: (i, j, 0)),
                  prev, main, nxt, prev, main, nxt, cspec, cspec,
                  pl.BlockSpec(sinkcol.shape, lambda i, j: (0, 0))],
        out_specs=pl.BlockSpec((1, tq, qw), lambda i, j: (i, j, 0)),
        out_shape=jax.ShapeDtypeStruct((b, l, qw), bf16),
        scratch_shapes=[pltpu.VMEM((tq + 2 * ATT_BLOCK, LANES), bf16),
                        pltpu.VMEM((tq + 2 * ATT_BLOCK, 2 * LANES), bf16),
                        pltpu.VMEM((c, 2 * LANES), bf16)],
        compiler_params=_cparams(("parallel", "parallel")),
        name="attention",
    )(q, k, k, k, v, v, v, ck, cv, sinkcol)


def _pack_pair(lo, hi):
    lo = lax.bitcast_convert_type(lo.astype(bf16).astype(f32), u32)
    hi = lax.bitcast_convert_type(hi.astype(bf16).astype(f32), u32)
    return (lo >> 16) | (hi & jnp.uint32(0xFFFF0000))


def _fourier1_kernel(x_ref, mod_ref, g_ref, w_ref, cs_ref, m_ref, ct_ref, st_ref, z_ref, ab_ref):
    n1 = x_ref.shape[1]
    nt = x_ref.shape[2]
    x = x_ref[0].reshape(n1 * nt, x_ref.shape[3])
    h = _norm_mod(x, g_ref[...], mod_ref[0, 0:1, :], mod_ref[0, 1:2, :]).astype(bf16)
    p = jnp.dot(h, w_ref[...], preferred_element_type=f32)
    for g in range(FOURIER_GROUPS):
        ug = p[:, g * LANES:(g + 1) * LANES].astype(bf16)
        ab = jnp.dot(ug, cs_ref[...], preferred_element_type=f32)
        ab_ref[0] = ab[:, :LANES]
        ab_ref[1] = ab[:, LANES:]
        for t in range(nt):
            stack = jnp.concatenate([ab_ref[0, pl.ds(t, n1, stride=nt), :],
                                     ab_ref[1, pl.ds(t, n1, stride=nt), :]], axis=0).astype(bf16)
            z = jnp.dot(m_ref[...], stack, preferred_element_type=f32)
            zr, zn = z[:n1], z[n1:]
            ct, st = ct_ref[t], st_ref[t]
            z_ref[0, g, t] = _pack_pair(ct * zr - st * zn, ct * zn + st * zr)


def _fourier2_kernel(scale, z_ref, m_ref, o_ref, zbuf, ybuf):
    _, grp, n2, tk, w = z_ref.shape
    for g in range(grp):
        zbuf[...] = z_ref[0, g].reshape(n2 * tk, w)
        for j in range(tk):
            zp = zbuf[pl.ds(j, n2, stride=tk), :]
            zr = lax.bitcast_convert_type(zp << 16, f32).astype(bf16)
            zn = lax.bitcast_convert_type(zp & jnp.uint32(0xFFFF0000), f32).astype(bf16)
            y = jnp.dot(m_ref[...], jnp.concatenate([zr, zn], axis=0), preferred_element_type=f32)
            ybuf[pl.ds(j, n2, stride=tk), :] = y * scale
        o_ref[0, g] = ybuf[...].reshape(n2, tk, w)


def _fourier(x, mod, g, w_f, cs):
    b, l, d = x.shape
    n2 = DFT_INNER
    n1 = l // n2
    grp, w = FOURIER_GROUPS, FOURIER_GROUP_W
    t2 = SUBLANES
    k1 = jnp.arange(n1, dtype=i32)
    ang1 = ((k1[:, None] * k1[None, :]) % n1).astype(f32) * (2.0 * math.pi / n1)
    c1, s1 = jnp.cos(ang1), jnp.sin(ang1)
    m1 = jnp.concatenate([jnp.concatenate([c1, -s1], axis=1),
                          jnp.concatenate([s1, c1], axis=1)], axis=0).astype(bf16)
    l2 = jnp.arange(n2, dtype=i32)
    angt = ((l2[:, None] * k1[None, :]) % l).astype(f32) * (2.0 * math.pi / l)
    ct = jnp.broadcast_to(jnp.cos(angt)[:, :, None], (n2, n1, w))
    st = jnp.broadcast_to(jnp.sin(angt)[:, :, None], (n2, n1, w))
    ang2 = ((l2[:, None] * l2[None, :]) % n2).astype(f32) * (2.0 * math.pi / n2)
    m2 = jnp.concatenate([jnp.cos(ang2), -jnp.sin(ang2)], axis=1).astype(bf16)

    tspec = pl.BlockSpec((t2, n1, w), lambda t, i: (t, 0, 0))
    z = pl.pallas_call(
        _fourier1_kernel,
        grid=(n2 // t2, b),
        in_specs=[pl.BlockSpec((1, n1, t2, d), lambda t, i: (i, 0, t, 0)),
                  pl.BlockSpec((1, 6, d), lambda t, i: (i, 0, 0)),
                  pl.BlockSpec((1, d), lambda t, i: (0, 0)),
                  pl.BlockSpec(w_f.shape, lambda t, i: (0, 0)),
                  pl.BlockSpec(cs.shape, lambda t, i: (0, 0)),
                  pl.BlockSpec(m1.shape, lambda t, i: (0, 0)), tspec, tspec],
        out_specs=pl.BlockSpec((1, grp, t2, n1, w), lambda t, i: (i, 0, t, 0, 0)),
        out_shape=jax.ShapeDtypeStruct((b, grp, n2, n1, w), u32),
        scratch_shapes=[pltpu.VMEM((2, n1 * t2, w), f32)],
        compiler_params=_cparams(("parallel", "parallel")),
        name="fourier_outer",
    )(x.reshape(b, n1, n2, d), mod, g, w_f, cs, m1, ct, st)

    tk = SUBLANES
    y = pl.pallas_call(
        functools.partial(_fourier2_kernel, 1.0 / math.sqrt(l * w)),
        grid=(b, n1 // tk),
        in_specs=[pl.BlockSpec((1, grp, n2, tk, w), lambda i, t: (i, 0, 0, t, 0)),
                  pl.BlockSpec(m2.shape, lambda i, t: (0, 0))],
        out_specs=pl.BlockSpec((1, grp, n2, tk, w), lambda i, t: (i, 0, 0, t, 0)),
        out_shape=jax.ShapeDtypeStruct((b, grp, n2, n1, w), f32),
        scratch_shapes=[pltpu.VMEM((n2 * tk, w), u32), pltpu.VMEM((n2 * tk, w), f32)],
        compiler_params=_cparams(("parallel", "parallel")),
        name="fourier_inner",
    )(z, m2)
    return y.reshape(b, grp, l, w)


def _first_max4(a):
    m = jnp.maximum(jnp.maximum(a[0], a[1]), jnp.maximum(a[2], a[3]))
    idx = jnp.where(a[0] == m, 0, jnp.where(a[1] == m, 1, jnp.where(a[2] == m, 2, 3)))
    return m, idx


def _pick4(vals, idx):
    return jnp.where(idx == 0, vals[0], jnp.where(idx == 1, vals[1],
                                                   jnp.where(idx == 2, vals[2], vals[3])))


def _route(f, rw_ref, rb_ref, tri_ref, base_ref, first_step, ri_ref, wc_ref, cnt_ref, meta_ref):
    tm = f.shape[0]
    f_hi = f.astype(bf16)
    f_lo = (f - f_hi.astype(f32)).astype(bf16)
    rw2 = rw_ref[...]
    part = jnp.dot(f_hi, rw2, preferred_element_type=f32)
    logits = (part[:, :LANES] + part[:, LANES:]
              + jnp.dot(f_lo, rw2[:, :LANES], preferred_element_type=f32))
    sc = jax.nn.sigmoid(logits)
    st = sc.T
    bt = (sc + rb_ref[...]).T
    neg = jnp.full((1, tm), -jnp.inf, f32)
    gs = []
    for g in range(N_GROUPS):
        a = [bt[4 * g + i: 4 * g + i + 1] for i in range(4)]
        m1, i1 = _first_max4(a)
        rest = [jnp.where(i1 == i, neg, a[i]) for i in range(4)]
        m2, _ = _first_max4(rest)
        gs.append(m1 + m2)
    _, gsel = _first_max4(gs)
    a = [_pick4([bt[4 * g + i: 4 * g + i + 1] for g in range(N_GROUPS)], gsel) for i in range(4)]
    s = [_pick4([st[4 * g + i: 4 * g + i + 1] for g in range(N_GROUPS)], gsel) for i in range(4)]
    _, i1 = _first_max4(a)
    rest = [jnp.where(i1 == i, neg, a[i]) for i in range(4)]
    _, i2 = _first_max4(rest)
    w1 = _pick4(s, i1)
    w2 = _pick4(s, i2)
    tot = w1 + w2
    w1 = w1 / tot
    w2 = w2 / tot
    e0 = gsel * EXPERTS_PER_GROUP + i1
    e1 = gsel * EXPERTS_PER_GROUP + i2

    @pl.when(first_step)
    def _():
        base_ref[...] = jnp.zeros_like(base_ref)

    td = DISPATCH_TILE
    eid = lax.broadcasted_iota(i32, (N_EXPERTS, tm), 0)
    oh0 = (eid == e0).astype(f32)
    oh1 = (eid == e1).astype(f32)
    oh = oh0 + oh1
    before = jnp.dot(oh.astype(bf16), tri_ref[...], preferred_element_type=f32)
    lane_tile = lax.broadcasted_iota(i32, (N_EXPERTS, tm), 1) // td
    ei = lax.broadcasted_iota(i32, (N_EXPERTS, N_EXPERTS), 0)
    ej = lax.broadcasted_iota(i32, (N_EXPERTS, N_EXPERTS), 1)
    strict_lower = (ej < ei).astype(f32)
    run_start = jnp.zeros((N_EXPERTS, tm), f32)
    goff = base_ref[...]
    for s in range(tm // td):
        cnt_s = jnp.sum(oh[:, s * td:(s + 1) * td], axis=1, keepdims=True)
        pad_s = jnp.floor((cnt_s + 7.0) * 0.125) * 8.0
        pad_b = jnp.broadcast_to(pad_s, (N_EXPERTS, LANES))
        start_b = jnp.dot(strict_lower, pad_b, precision=HIGHEST, preferred_element_type=f32)
        run_start = jnp.where(lane_tile == s, start_b[:, 0:1], run_start)
        meta_ref[s, 0] = start_b.astype(i32)
        meta_ref[s, 1] = pad_b.astype(i32)
        meta_ref[s, 2] = goff.astype(i32)
        goff = goff + pad_b
    base_ref[...] = goff
    cnt_ref[...] = goff
    pos = before + run_start
    lp0 = jnp.sum(oh0 * pos, axis=0, keepdims=True)
    lp1 = jnp.sum(oh1 * pos, axis=0, keepdims=True)
    zi = jnp.zeros((1, tm), i32)
    ri_ref[...] = jnp.concatenate(
        [lp0.astype(i32), lp1.astype(i32), e0, e1, zi, zi, zi, zi], axis=0)
    zf = jnp.zeros((LANES - 4, tm), f32)
    wc_ref[...] = jnp.concatenate([w1, w2, lp0, lp1, zf], axis=0).T


def _outproj_kernel(yf_ref, o_ref, x_ref, mod_ref, w_ref, g_ref, rw_ref, rb_ref, tri_ref,
                    x1_ref, f_ref, ri_ref, wc_ref, cnt_ref, meta_ref, base_ref):
    mix = jnp.concatenate([yf_ref[0, g].astype(bf16) for g in range(FOURIER_GROUPS)] + [o_ref[0]],
                          axis=1)
    y = jnp.dot(mix, w_ref[...], preferred_element_type=f32)
    x1 = x_ref[0] + mod_ref[0, 2:3, :] * y
    x1_ref[0] = x1
    f = _norm_mod(x1, g_ref[...], mod_ref[0, 3:4, :], mod_ref[0, 4:5, :])
    f_ref[0] = f.astype(bf16)
    first = (pl.program_id(0) == 0) & (pl.program_id(1) == 0)
    _route(f, rw_ref, rb_ref, tri_ref, base_ref, first, ri_ref, wc_ref, cnt_ref, meta_ref)


def _route_specs(b, l, tm):
    nl = l // tm
    rw = lambda d: pl.BlockSpec((d, 2 * LANES), lambda i, j: (0, 0))
    rb = pl.BlockSpec((1, LANES), lambda i, j: (0, 0))
    tri = pl.BlockSpec((tm, tm), lambda i, j: (0, 0))
    ns = tm // DISPATCH_TILE
    out_specs = [pl.BlockSpec((8, tm), lambda i, j: (0, i * nl + j)),
                 pl.BlockSpec((tm, LANES), lambda i, j: (i * nl + j, 0)),
                 pl.BlockSpec((N_EXPERTS, LANES), lambda i, j: (0, 0)),
                 pl.BlockSpec((ns, 3, N_EXPERTS, LANES), lambda i, j: (i * nl + j, 0, 0, 0))]
    out_shape = [jax.ShapeDtypeStruct((8, b * l), i32),
                 jax.ShapeDtypeStruct((b * l, LANES), f32),
                 jax.ShapeDtypeStruct((N_EXPERTS, LANES), f32),
                 jax.ShapeDtypeStruct((b * l // DISPATCH_TILE, 3, N_EXPERTS, LANES), i32)]
    return rw, rb, tri, out_specs, out_shape


def _outproj(yf, o, x, mod, w, g, rw, rb, tri):
    b, l, d = x.shape
    tm = TOKEN_TILE
    rws, rbs, tris, r_specs, r_shapes = _route_specs(b, l, tm)
    row = pl.BlockSpec((1, tm, d), lambda i, j: (i, j, 0))
    return pl.pallas_call(
        _outproj_kernel,
        grid=(b, l // tm),
        in_specs=[pl.BlockSpec((1, FOURIER_GROUPS, tm, LANES), lambda i, j: (i, 0, j, 0)),
                  pl.BlockSpec((1, tm, o.shape[2]), lambda i, j: (i, j, 0)),
                  row,
                  pl.BlockSpec((1, 6, d), lambda i, j: (i, 0, 0)),
                  pl.BlockSpec(w.shape, lambda i, j: (0, 0)),
                  pl.BlockSpec((1, d), lambda i, j: (0, 0)),
                  rws(d), rbs, tris],
        out_specs=[row, row] + r_specs,
        out_shape=[jax.ShapeDtypeStruct((b, l, d), f32),
                   jax.ShapeDtypeStruct((b, l, d), bf16)] + r_shapes,
        scratch_shapes=[pltpu.VMEM((N_EXPERTS, LANES), f32)],
        compiler_params=_cparams(("arbitrary", "arbitrary")),
        name="outproj_router",
    )(yf, o, x, mod, w, g, rw, rb, tri)


def _glu_kernel(x_ref, mod_ref, g_ref, w_ref, b_ref, u_ref):
    h = _norm_mod(x_ref[0], g_ref[...], mod_ref[0, 0:1, :], mod_ref[0, 1:2, :])
    p = jnp.dot(h.astype(bf16), w_ref[...], preferred_element_type=f32) + b_ref[...]
    ch = p.shape[1] // 2
    u_ref[0] = p[:, :ch] * jax.nn.sigmoid(p[:, ch:])


def _glu(x, mod, g, w, bias):
    b, l, d = x.shape
    tm = TOKEN_TILE
    n = w.shape[1]
    return pl.pallas_call(
        _glu_kernel,
        grid=(b, l // tm),
        in_specs=[pl.BlockSpec((1, tm, d), lambda i, j: (i, j, 0)),
                  pl.BlockSpec((1, 6, d), lambda i, j: (i, 0, 0)),
                  pl.BlockSpec((1, d), lambda i, j: (0, 0)),
                  pl.BlockSpec((d, n), lambda i, j: (0, 0)),
                  pl.BlockSpec((1, n), lambda i, j: (0, 0))],
        out_specs=pl.BlockSpec((1, tm, n // 2), lambda i, j: (i, j, 0)),
        out_shape=jax.ShapeDtypeStruct((b, l, n // 2), f32),
        compiler_params=_cparams(("parallel", "parallel")),
        name="pw1_glu",
    )(x, mod, g, w, bias)


def _conv_kernel(seq_len, up_ref, um_ref, un_ref, x_ref, mod_ref, dw_ref, db_ref, lg_ref, lb_ref,
                 w_ref, pb_ref, g_ref, rw_ref, rb_ref, tri_ref,
                 x1_ref, f_ref, ri_ref, wc_ref, cnt_ref, meta_ref, base_ref, ext, conv_out):
    j = pl.program_id(1)
    tm = um_ref.shape[1]
    hl = CONV_HALO
    half = CONV_W // 2
    ext[0:hl] = jnp.where(j > 0, up_ref[0], jnp.zeros_like(up_ref[0]))
    ext[hl:hl + tm] = um_ref[0]
    ext[hl + tm:] = jnp.where((j + 1) * tm < seq_len, un_ref[0], jnp.zeros_like(un_ref[0]))
    base = hl - half
    span = (CONV_W - 1) // SUBLANES * SUBLANES
    rows = CONV_ROWS

    def lane_chunk(c, carry):
        lanes = pl.ds(pl.multiple_of(c * LANES, LANES), LANES)
        for r in range(0, tm, rows):
            part = jnp.broadcast_to(db_ref[:, lanes], (rows, LANES))
            for phase in range(SUBLANES):
                win = ext[base + phase + r: base + phase + r + rows + span, lanes]
                same = None
                for t in range(phase, CONV_W, SUBLANES):
                    term = win[t - phase: t - phase + rows, :] * dw_ref[t:t + 1, lanes]
                    same = term if same is None else same + term
                part = part + same
            conv_out[r:r + rows, lanes] = part
        return carry

    lax.fori_loop(0, um_ref.shape[2] // LANES, lane_chunk, 0)
    acc = conv_out[...]
    mu = jnp.mean(acc, axis=-1, keepdims=True)
    cen = acc - mu
    var = jnp.mean(cen * cen, axis=-1, keepdims=True)
    ln = cen * lax.rsqrt(var + EPS) * lg_ref[...] + lb_ref[...]
    act = ln * jax.nn.sigmoid(ln)
    y = jnp.dot(act.astype(bf16), w_ref[...], preferred_element_type=f32) + pb_ref[...]
    x1 = x_ref[0] + mod_ref[0, 2:3, :] * y
    x1_ref[0] = x1
    f = _norm_mod(x1, g_ref[...], mod_ref[0, 3:4, :], mod_ref[0, 4:5, :])
    f_ref[0] = f.astype(bf16)
    first = (pl.program_id(0) == 0) & (j == 0)
    _route(f, rw_ref, rb_ref, tri_ref, base_ref, first, ri_ref, wc_ref, cnt_ref, meta_ref)


def _conv(u, x, mod, dw_w, dw_b, ln_g, ln_b, pw2_w, pw2_b, g, rw, rb, tri):
    b, l, d = x.shape
    tm = TOKEN_TILE
    hl = CONV_HALO
    r = tm // hl
    nh = l // hl
    rws, rbs, tris, r_specs, r_shapes = _route_specs(b, l, tm)
    row = pl.BlockSpec((1, tm, d), lambda i, j: (i, j, 0))
    vec = pl.BlockSpec((1, d), lambda i, j: (0, 0))
    return pl.pallas_call(
        functools.partial(_conv_kernel, l),
        grid=(b, l // tm),
        in_specs=[pl.BlockSpec((1, hl, d), lambda i, j: (i, jnp.maximum(j * r - 1, 0), 0)),
                  row,
                  pl.BlockSpec((1, hl, d), lambda i, j: (i, jnp.minimum(j * r + r, nh - 1), 0)),
                  row,
                  pl.BlockSpec((1, 6, d), lambda i, j: (i, 0, 0)),
                  pl.BlockSpec(dw_w.shape, lambda i, j: (0, 0)),
                  vec, vec, vec,
                  pl.BlockSpec(pw2_w.shape, lambda i, j: (0, 0)),
                  vec, vec, rws(d), rbs, tris],
        out_specs=[row, row] + r_specs,
        out_shape=[jax.ShapeDtypeStruct((b, l, d), f32),
                   jax.ShapeDtypeStruct((b, l, d), bf16)] + r_shapes,
        scratch_shapes=[pltpu.VMEM((N_EXPERTS, LANES), f32),
                        pltpu.VMEM((tm + 2 * hl, d), f32),
                        pltpu.VMEM((tm, d), f32)],
        compiler_params=_cparams(("arbitrary", "arbitrary")),
        name="conv_router",
    )(u, u, u, x, mod, dw_w, dw_b, ln_g, ln_b, pw2_w, pw2_b, g, rw, rb, tri)


def _pack_bf16_pairs(x):
    h = x.shape[1] // 2
    lo = lax.bitcast_convert_type(x[:, :h], u32)
    hi = lax.bitcast_convert_type(x[:, h:], u32)
    return (lo >> 16) | (hi & jnp.uint32(0xFFFF0000))


def _unpack_bf16_pairs(u):
    lo = lax.bitcast_convert_type(u << 16, f32)
    hi = lax.bitcast_convert_type(u & jnp.uint32(0xFFFF0000), f32)
    return jnp.concatenate([lo, hi], axis=1).astype(bf16)


def _run_copies(meta, tile, local_ref, hbm_ref, sem, to_hbm):
    start_ref, size_ref, dst_ref = meta
    for e in range(N_EXPERTS):
        k = tile * N_EXPERTS + e
        size = pl.multiple_of(size_ref[k], RUN_ALIGN)

        @pl.when(size > 0)
        def _():
            loc = local_ref.at[pl.ds(pl.multiple_of(start_ref[k], RUN_ALIGN), size)]
            glob = hbm_ref.at[pl.ds(pl.multiple_of(dst_ref[k], RUN_ALIGN), size)]
            if to_hbm:
                pltpu.make_async_copy(loc, glob, sem).start()
            else:
                pltpu.make_async_copy(glob, loc, sem).start()


def _wait_rows(rows, local_ref, hbm_ref, sem):
    rows = pl.multiple_of(rows, RUN_ALIGN)

    @pl.when(rows > 0)
    def _():
        pltpu.make_async_copy(local_ref.at[pl.ds(0, rows)], hbm_ref.at[pl.ds(0, rows)], sem).wait()


def _dispatch_kernel(start_ref, size_ref, dst_ref, tot_ref, tail_start_ref, tail_size_ref, nv_ref,
                     f_ref, lp_ref, xs_ref, loc, zbuf, sem, zsem):
    i = pl.program_id(0)
    n = pl.num_programs(0)
    slot = i % 2
    meta = (start_ref, size_ref, dst_ref)

    @pl.when(i >= 2)
    def _():
        _wait_rows(tot_ref[i - 2], loc.at[slot], xs_ref, sem.at[slot])

    rows = loc.shape[1]
    td = f_ref.shape[0]
    r = lax.broadcasted_iota(i32, (rows, td), 0)
    onehot = ((r == lp_ref[0:1, :]) | (r == lp_ref[1:2, :])).astype(bf16)
    sorted_rows = jnp.dot(onehot, f_ref[...], preferred_element_type=f32)
    loc[slot] = _pack_bf16_pairs(sorted_rows)
    _run_copies(meta, i, loc.at[slot], xs_ref, sem.at[slot], to_hbm=True)

    @pl.when(i == n - 1)
    def _():
        zbuf[...] = jnp.zeros_like(zbuf)
        total = 0
        for e in range(N_EXPERTS):
            size = pl.multiple_of(tail_size_ref[e], RUN_ALIGN)
            total = total + size

            @pl.when(size > 0)
            def _():
                pltpu.make_async_copy(
                    zbuf.at[pl.ds(0, size)],
                    xs_ref.at[pl.ds(pl.multiple_of(tail_start_ref[e], RUN_ALIGN), size)], zsem).start()

        _wait_rows(total, zbuf, xs_ref, zsem)

        def zero_block(k, c):
            pltpu.make_async_copy(zbuf, xs_ref.at[pl.ds(pl.multiple_of(k * zbuf.shape[0], RUN_ALIGN),
                                                        zbuf.shape[0])], zsem).start()
            return c

        def wait_block(k, c):
            pltpu.make_async_copy(zbuf, xs_ref.at[pl.ds(0, zbuf.shape[0])], zsem).wait()
            return c

        n_blocks = xs_ref.shape[0] // zbuf.shape[0]
        lax.fori_loop(nv_ref[0], n_blocks, zero_block, 0)
        lax.fori_loop(nv_ref[0], n_blocks, wait_block, 0)
        _wait_rows(tot_ref[i], loc.at[slot], xs_ref, sem.at[slot])

        @pl.when(i >= 1)
        def _():
            _wait_rows(tot_ref[i - 1], loc.at[1 - slot], xs_ref, sem.at[1 - slot])


def _dispatch(tables, f2, ri, n_slots):
    t, d = f2.shape
    td = DISPATCH_TILE
    return pl.pallas_call(
        _dispatch_kernel,
        grid_spec=pltpu.PrefetchScalarGridSpec(
            num_scalar_prefetch=7,
            grid=(t // td,),
            in_specs=[pl.BlockSpec((td, d), lambda i, *_: (i, 0)),
                      pl.BlockSpec((8, td), lambda i, *_: (0, i))],
            out_specs=pl.BlockSpec(memory_space=pl.ANY),
            scratch_shapes=[pltpu.VMEM((2, LOCAL_ROWS, d // 2), u32),
                            pltpu.VMEM((EXPERT_ROWS, d // 2), u32),
                            pltpu.SemaphoreType.DMA((2,)), pltpu.SemaphoreType.DMA(())]),
        out_shape=jax.ShapeDtypeStruct((n_slots, d // 2), u32),
        compiler_params=_cparams(("arbitrary",)),
        name="moe_dispatch",
    )(*tables, f2, ri)


def _expert_kernel(be_ref, nv_ref, x_ref, wg_ref, wu_ref, wd_ref, y_ref, wgb, wub, wdb):
    i = pl.program_id(0)
    changed = jnp.logical_or(i == 0, be_ref[i] != be_ref[jnp.maximum(i - 1, 0)])

    @pl.when(changed)
    def _():
        wgb[...] = wg_ref[0, 0].astype(bf16)
        wub[...] = wu_ref[0, 0].astype(bf16)
        wdb[...] = wd_ref[0, 0].astype(bf16)

    @pl.when(i < nv_ref[0])
    def _():
        xb = _unpack_bf16_pairs(x_ref[...])
        gate = jnp.dot(xb, wgb[...], preferred_element_type=f32)
        up = jnp.dot(xb, wub[...], preferred_element_type=f32)
        hid = (gate * jax.nn.sigmoid(gate) * up).astype(bf16)
        y = jnp.dot(hid, wdb[...], preferred_element_type=f32)
        y_ref[...] = _pack_bf16_pairs(y.astype(bf16).astype(f32))

    @pl.when(i >= nv_ref[0])
    def _():
        y_ref[...] = jnp.zeros_like(y_ref)


def _experts(block_e, n_valid, xs, w_gate, w_up, w_down, layer):
    ns, dh = xs.shape
    tb = EXPERT_ROWS
    d, ff = w_gate.shape[2:]
    xmap = lambda i, be, nv: (jnp.maximum(jnp.minimum(i, nv[0] - 1), 0), 0)
    wmap = lambda i, be, nv: (layer, be[i], 0, 0)
    return pl.pallas_call(
        _expert_kernel,
        grid_spec=pltpu.PrefetchScalarGridSpec(
            num_scalar_prefetch=2,
            grid=(ns // tb,),
            in_specs=[pl.BlockSpec((tb, dh), xmap),
                      pl.BlockSpec((1, 1, d, ff), wmap),
                      pl.BlockSpec((1, 1, d, ff), wmap),
                      pl.BlockSpec((1, 1, ff, d), wmap)],
            out_specs=pl.BlockSpec((tb, dh), lambda i, be, nv: (i, 0)),
            scratch_shapes=[pltpu.VMEM((d, ff), bf16), pltpu.VMEM((d, ff), bf16),
                            pltpu.VMEM((ff, d), bf16)]),
        out_shape=jax.ShapeDtypeStruct((ns, dh), u32),
        compiler_params=_cparams(("arbitrary",)),
        name="moe_experts",
    )(block_e, n_valid, xs, w_gate, w_up, w_down)


def _combine_kernel(final, start_ref, size_ref, dst_ref, tot_ref, ys_ref, wc_ref, x_ref, mod_ref,
                    g_ref, o_ref, loc, sem):
    i = pl.program_id(0)
    n = pl.num_programs(0)
    slot = i % 2
    meta = (start_ref, size_ref, dst_ref)

    @pl.when(i == 0)
    def _():
        loc[...] = jnp.zeros_like(loc)
        _run_copies(meta, i, loc.at[slot], ys_ref, sem.at[slot], to_hbm=False)

    @pl.when(i + 1 < n)
    def _():
        _run_copies(meta, i + 1, loc.at[1 - slot], ys_ref, sem.at[1 - slot], to_hbm=False)

    _wait_rows(tot_ref[i], loc.at[slot], ys_ref, sem.at[slot])
    rows = loc.shape[1]
    td = x_ref.shape[0]
    wc = wc_ref[...]
    c = lax.broadcasted_iota(i32, (td, rows), 1)
    sel = jnp.concatenate([(c == wc[:, 2:3].astype(i32)).astype(bf16),
                           (c == wc[:, 3:4].astype(i32)).astype(bf16)], axis=0)
    picked = jnp.dot(sel, _unpack_bf16_pairs(loc[slot]), preferred_element_type=f32)
    y = wc[:, 0:1] * picked[:td] + wc[:, 1:2] * picked[td:]
    xo = x_ref[...] + mod_ref[0, 5:6, :] * y
    if final:
        ms = jnp.mean(xo * xo, axis=-1, keepdims=True)
        xo = xo * lax.rsqrt(ms + EPS) * g_ref[...]
    o_ref[...] = xo


def _combine(tables, ys, wc, x, mod, g, final):
    b, l, d = x.shape
    td = DISPATCH_TILE
    per_batch = l // td
    out = pl.pallas_call(
        functools.partial(_combine_kernel, final),
        grid_spec=pltpu.PrefetchScalarGridSpec(
            num_scalar_prefetch=4,
            grid=(b * per_batch,),
            in_specs=[pl.BlockSpec(memory_space=pl.ANY),
                      pl.BlockSpec((td, LANES), lambda i, *_: (i, 0)),
                      pl.BlockSpec((td, d), lambda i, *_: (i, 0)),
                      pl.BlockSpec((1, 6, d), lambda i, *_: (i // per_batch, 0, 0)),
                      pl.BlockSpec((1, d), lambda i, *_: (0, 0))],
            out_specs=pl.BlockSpec((td, d), lambda i, *_: (i, 0)),
            scratch_shapes=[pltpu.VMEM((2, LOCAL_ROWS, d // 2), u32),
                            pltpu.SemaphoreType.DMA((2,))]),
        out_shape=jax.ShapeDtypeStruct((b * l, d), f32),
        compiler_params=_cparams(("arbitrary",)),
        name="moe_combine",
    )(*tables, ys, wc, x.reshape(b * l, d), mod, g)
    return out.reshape(b, l, d)


def _moe(f, routed, x, mod, g_final, w_gate, w_up, w_down, layer, final):
    ri, wc, cnt, meta = routed
    b, l, d = x.shape
    t = b * l
    tb = EXPERT_ROWS
    n_tiles = t // DISPATCH_TILE
    used = cnt[:, 0].astype(i32)
    region = (used + tb - 1) // tb * tb
    gend = jnp.cumsum(region)
    gstart = gend - region
    max_rows = 2 * t + n_tiles * N_EXPERTS * (RUN_ALIGN - 1) + N_EXPERTS * (tb - 1)
    n_blocks = -(-max_rows // tb)
    m = meta[:, :, :, 0]
    run_start = m[:, 0].reshape(-1)
    run_size = m[:, 1].reshape(-1)
    run_dst = (m[:, 2] + gstart[None, :]).reshape(-1)
    tile_rows = jnp.sum(m[:, 1], axis=1)
    block_row = jnp.arange(n_blocks, dtype=i32) * tb
    block_e = jnp.minimum(jnp.sum((block_row[:, None] >= gend[None, :]).astype(i32), axis=1),
                          N_EXPERTS - 1)
    n_valid = (gend[-1] // tb).reshape(1)
    xs = _dispatch((run_start, run_size, run_dst, tile_rows, gstart + used, region - used, n_valid),
                   f.reshape(t, d), ri, n_blocks * tb)
    ys = _experts(block_e, n_valid, xs, w_gate, w_up, w_down, layer)
    return _combine((run_start, run_size, run_dst, tile_rows), ys, wc, x, mod, g_final, final)


def _rope_tables(l):
    lane = jnp.arange(LANES)
    dh = lane % HEAD_DIM
    inv = ROPE_THETA ** (-(dh % 16).astype(f32) / 16.0)
    pos = jnp.arange(l)
    row = (pos // GRID_W).astype(f32)
    col = (pos % GRID_W).astype(f32)
    p = jnp.where((dh // 32)[None, :] == 0, row[:, None], col[:, None])
    ang = p * inv[None, :]
    sign = jnp.where((dh % 32) < 16, -1.0, 1.0).astype(f32)
    return jnp.cos(ang), jnp.sin(ang) * sign[None, :]


def kernel(x, c, ctx, c_ctx, ada_w, ada_b, norm_mix_g, norm_ffn_g, even_w_in, even_w_out, even_sink, conv_pw1_w, conv_pw1_b, conv_dw_w, conv_dw_b, conv_ln_g, conv_ln_b, conv_pw2_w, conv_pw2_b, router_w, router_b, moe_w_gate, moe_w_up, moe_w_down, final_norm_g):
    b, l, d = x.shape
    depth = ada_w.shape[0]
    assert depth == 2 and b < COND_ROWS
    ctx_row = b
    cond = jnp.zeros((COND_ROWS, d), f32).at[:b].set(c).at[ctx_row].set(c_ctx)
    mods = _adaln(cond, ada_w, ada_b).reshape(depth, COND_ROWS, 6, d)

    heads = jnp.arange(N_HEADS).reshape(N_KV_HEADS, N_HEADS // N_KV_HEADS).T.reshape(-1)
    qperm = (heads[:, None] * HEAD_DIM + jnp.arange(HEAD_DIM)[None, :]).reshape(-1)
    fw = FOURIER_GROUPS * FOURIER_GROUP_W
    qw = N_HEADS * HEAD_DIM
    w_in = even_w_in[0]
    w_in_p = jnp.concatenate([w_in[:, :fw], w_in[:, fw:fw + qw][:, qperm], w_in[:, fw + qw:]],
                             axis=1).astype(bf16)
    w_out = even_w_out[0]
    w_out_p = jnp.concatenate([w_out[:fw], w_out[fw:][qperm]], axis=0).astype(bf16)
    sinkcol = jnp.repeat(even_sink[0][heads].astype(f32) * LOG2E, ATT_BLOCK).reshape(-1, 1)

    cidx = jnp.arange(FOURIER_GROUP_W, dtype=i32)
    angc = ((cidx[:, None] * cidx[None, :]) % FOURIER_GROUP_W).astype(f32) * (2.0 * math.pi / FOURIER_GROUP_W)
    cs = jnp.concatenate([jnp.cos(angc), jnp.sin(angc)], axis=1).astype(bf16)
    cos_t, sin_t = _rope_tables(l)

    rw32 = jnp.zeros((d, LANES), f32).at[:, :N_EXPERTS].set(router_w.astype(f32))
    rw_hi = rw32.astype(bf16)
    rw = jnp.concatenate([rw_hi, (rw32 - rw_hi.astype(f32)).astype(bf16)], axis=1)
    rb = jnp.zeros((1, LANES), f32).at[0, :N_EXPERTS].set(router_b)
    tpos = jnp.arange(TOKEN_TILE)
    tri = ((tpos[:, None] < tpos[None, :])
           & (tpos[:, None] // DISPATCH_TILE == tpos[None, :] // DISPATCH_TILE)).astype(bf16)
    row = lambda v: v.reshape(1, -1)

    q, k, v = _inproj(x, mods[0], row(norm_mix_g[0]), w_in_p[:, fw:], cos_t, sin_t)
    ck, cv = _ctxkv(ctx, mods[0], row(norm_mix_g[0]), w_in_p[:, fw + qw:], ctx_row)
    yf = _fourier(x, mods[0], row(norm_mix_g[0]), w_in_p[:, :fw], cs)
    att = _attention(q, k, v, ck, cv, sinkcol)
    x1, f, *routed = _outproj(yf, att, x, mods[0], w_out_p, row(norm_ffn_g[0]), rw, rb, tri)
    x2 = _moe(f, routed, x1, mods[0], row(final_norm_g), moe_w_gate, moe_w_up, moe_w_down,
              layer=0, final=False)

    u = _glu(x2, mods[1], row(norm_mix_g[1]), conv_pw1_w[0].astype(bf16), row(conv_pw1_b[0]))
    x3, f, *routed = _conv(u, x2, mods[1], conv_dw_w[0], row(conv_dw_b[0]), row(conv_ln_g[0]),
                               row(conv_ln_b[0]), conv_pw2_w[0].astype(bf16), row(conv_pw2_b[0]),
                               row(norm_ffn_g[1]), rw, rb, tri)
    return _moe(f, routed, x3, mods[1], row(final_norm_g), moe_w_gate, moe_w_up, moe_w_down,
                layer=1, final=True)
```

```python
import functools
import math

import jax
import jax.numpy as jnp
from jax import lax
from jax.experimental import pallas as pl
from jax.experimental.pallas import tpu as pltpu

f32 = jnp.float32
bf16 = jnp.bfloat16
i32 = jnp.int32
u32 = jnp.uint32
HIGHEST = lax.Precision.HIGHEST

GRID_W = 64
HEAD_DIM = 64
N_HEADS = 8
N_KV_HEADS = 2
WINDOW = 128
ATT_BLOCK = 128
ROPE_THETA = 10000.0
FOURIER_GROUPS = 4
FOURIER_GROUP_W = 128
CONV_W = 31
N_EXPERTS = 16
N_GROUPS = 4
EXPERTS_PER_GROUP = 4
EXPERT_FF = 512
EPS = 1e-6
NEG_INF = -1e30
LOG2E = math.log2(math.e)

LANES = 128
SUBLANES = 8
COND_ROWS = 8
DFT_INNER = 64
TOKEN_TILE = 512
ATT_TILE = 512
EXPERT_ROWS = 512
DISPATCH_TILE = 256
RUN_ALIGN = 8
LOCAL_ROWS = -(-(2 * DISPATCH_TILE + N_EXPERTS * (RUN_ALIGN - 1)) // LANES) * LANES
CONV_HALO = 16
CONV_ROWS = 128
VMEM_LIMIT = 56 * 1024 * 1024


def _cparams(sem, vmem=VMEM_LIMIT):
    return pltpu.CompilerParams(dimension_semantics=sem, vmem_limit_bytes=vmem)


def _adaln_kernel(cond_ref, w_ref, b_ref, o_ref):
    s = cond_ref[...]
    s = s * jax.nn.sigmoid(s)
    o_ref[0] = jnp.dot(s, w_ref[0], precision=HIGHEST, preferred_element_type=f32) + b_ref[0]


def _adaln(cond, ada_w, ada_b):
    depth, d, n = ada_w.shape
    tn = 1536
    return pl.pallas_call(
        _adaln_kernel,
        grid=(depth, n // tn),
        in_specs=[pl.BlockSpec((COND_ROWS, d), lambda i, j: (0, 0)),
                  pl.BlockSpec((1, d, tn), lambda i, j: (i, 0, j)),
                  pl.BlockSpec((1, 1, tn), lambda i, j: (i, 0, j))],
        out_specs=pl.BlockSpec((1, COND_ROWS, tn), lambda i, j: (i, 0, j)),
        out_shape=jax.ShapeDtypeStruct((depth, COND_ROWS, n), f32),
        compiler_params=_cparams(("arbitrary", "arbitrary")),
        name="adaln",
    )(cond, ada_w, ada_b.reshape(depth, 1, n))


def _norm_mod(x, g, shift, scale):
    ms = jnp.mean(x * x, axis=-1, keepdims=True)
    return (x * lax.rsqrt(ms + EPS) * g) * (1.0 + scale) + shift


def _rope(p, cos, sin_signed, first_half):
    rot = jnp.where(first_half, pltpu.roll(p, LANES - 16, axis=1), pltpu.roll(p, 16, axis=1))
    return p * cos + rot * sin_signed


def _inproj_kernel(x_ref, mod_ref, g_ref, w_ref, cos_ref, sin_ref, q_ref, k_ref, v_ref):
    h = _norm_mod(x_ref[0], g_ref[...], mod_ref[0, 0:1, :], mod_ref[0, 1:2, :])
    p = jnp.dot(h.astype(bf16), w_ref[...], preferred_element_type=f32)
    cos = cos_ref[...]
    sin = sin_ref[...]
    lane = lax.broadcasted_iota(i32, cos.shape, 1)
    first_half = (lane % 32) < 16
    qw = N_HEADS * HEAD_DIM
    for c in range(qw // LANES):
        qc = p[:, c * LANES:(c + 1) * LANES]
        q_ref[0, :, c * LANES:(c + 1) * LANES] = (
            _rope(qc, cos, sin, first_half) * (LOG2E * HEAD_DIM ** -0.5)).astype(bf16)
    k_ref[0] = _rope(p[:, qw:qw + LANES], cos, sin, first_half).astype(bf16)
    v_ref[0] = p[:, qw + LANES:].astype(bf16)


def _inproj(x, mod, g, w, cos_t, sin_t):
    b, l, d = x.shape
    tm = TOKEN_TILE
    n = w.shape[1]
    return pl.pallas_call(
        _inproj_kernel,
        grid=(b, l // tm),
        in_specs=[pl.BlockSpec((1, tm, d), lambda i, j: (i, j, 0)),
                  pl.BlockSpec((1, 6, d), lambda i, j: (i, 0, 0)),
                  pl.BlockSpec((1, d), lambda i, j: (0, 0)),
                  pl.BlockSpec((d, n), lambda i, j: (0, 0)),
                  pl.BlockSpec((tm, LANES), lambda i, j: (j, 0)),
                  pl.BlockSpec((tm, LANES), lambda i, j: (j, 0))],
        out_specs=[pl.BlockSpec((1, tm, N_HEADS * HEAD_DIM), lambda i, j: (i, j, 0)),
                   pl.BlockSpec((1, tm, LANES), lambda i, j: (i, j, 0)),
                   pl.BlockSpec((1, tm, LANES), lambda i, j: (i, j, 0))],
        out_shape=[jax.ShapeDtypeStruct((b, l, N_HEADS * HEAD_DIM), bf16),
                   jax.ShapeDtypeStruct((b, l, LANES), bf16),
                   jax.ShapeDtypeStruct((b, l, LANES), bf16)],
        compiler_params=_cparams(("parallel", "parallel")),
        name="inproj",
    )(x, mod, g, w, cos_t, sin_t)


def _ctxkv_kernel(x_ref, mod_ref, g_ref, w_ref, k_ref, v_ref):
    h = _norm_mod(x_ref[0], g_ref[...], mod_ref[0, 0:1, :], mod_ref[0, 1:2, :])
    p = jnp.dot(h.astype(bf16), w_ref[...], preferred_element_type=f32)
    k_ref[0] = p[:, :LANES].astype(bf16)
    v_ref[0] = p[:, LANES:].astype(bf16)


def _ctxkv(ctx, mod, g, w_kv, ctx_row):
    b, c, d = ctx.shape
    return pl.pallas_call(
        _ctxkv_kernel,
        grid=(b,),
        in_specs=[pl.BlockSpec((1, c, d), lambda i: (i, 0, 0)),
                  pl.BlockSpec((1, 6, d), lambda i: (ctx_row, 0, 0)),
                  pl.BlockSpec((1, d), lambda i: (0, 0)),
                  pl.BlockSpec((d, 2 * LANES), lambda i: (0, 0))],
        out_specs=[pl.BlockSpec((1, c, LANES), lambda i: (i, 0, 0)),
                   pl.BlockSpec((1, c, LANES), lambda i: (i, 0, 0))],
        out_shape=[jax.ShapeDtypeStruct((b, c, LANES), bf16),
                   jax.ShapeDtypeStruct((b, c, LANES), bf16)],
        compiler_params=_cparams(("parallel",)),
        name="ctxkv",
    )(ctx, mod, g, w_kv)


def _attn_kernel(seq_len, q_ref, kp_ref, km_ref, kn_ref, vp_ref, vm_ref, vn_ref,
                 ck_ref, cv_ref, sink_ref, o_ref, kext, vext, cvext):
    j = pl.program_id(1)
    tq = ATT_TILE
    blk = ATT_BLOCK
    kext[0:blk] = kp_ref[0]
    kext[blk:blk + tq] = km_ref[0]
    kext[blk + tq:] = kn_ref[0]
    vext[:, LANES:] = jnp.ones((tq + 2 * blk, LANES), bf16)
    vext[0:blk, :LANES] = vp_ref[0]
    vext[blk:blk + tq, :LANES] = vm_ref[0]
    vext[blk + tq:, :LANES] = vn_ref[0]
    cvext[:, LANES:] = jnp.ones((cvext.shape[0], LANES), bf16)
    cvext[:, :LANES] = cv_ref[0]
    ck = ck_ref[0]
    sink = sink_ref[...]
    n_chunks = (N_HEADS * HEAD_DIM) // LANES
    rows = 2 * n_chunks * blk
    lane = lax.broadcasted_iota(i32, (blk, LANES), 1)
    low = lane < HEAD_DIM
    qi = lax.broadcasted_iota(i32, (rows, 3 * blk), 0) % blk
    pk = lax.broadcasted_iota(i32, (rows, 3 * blk), 1)
    band_bias = jnp.where(jnp.abs(pk - blk - qi) <= WINDOW, 0.0, NEG_INF).astype(f32)
    pcol = lax.broadcasted_iota(i32, (1, 3 * blk), 1)
    nt = (((1,), (1,)), ((), ()))

    def sub(s, carry):
        r0 = pl.multiple_of(s * blk, blk)
        qs = q_ref[0, pl.ds(r0, blk), :]
        parts = []
        for c in range(n_chunks):
            qc = qs[:, c * LANES:(c + 1) * LANES]
            parts.append(jnp.where(low, qc, jnp.zeros_like(qc)))
            parts.append(jnp.where(low, jnp.zeros_like(qc), qc))
        lhs = jnp.concatenate(parts, axis=0)
        kl = kext[pl.ds(r0, 3 * blk), :]
        vl = vext[pl.ds(r0, 3 * blk), :]
        kpos = j * tq + r0 - blk + pcol
        col_bias = jnp.where((kpos >= 0) & (kpos < seq_len), 0.0, NEG_INF).astype(f32)
        s_c = lax.dot_general(lhs, ck, nt, preferred_element_type=f32)
        s_l = lax.dot_general(lhs, kl, nt, preferred_element_type=f32) + band_bias + col_bias
        blocks = ([s_c[:, i:i + LANES] for i in range(0, s_c.shape[1], LANES)]
                  + [s_l[:, i:i + LANES] for i in range(0, s_l.shape[1], LANES)])
        folded = functools.reduce(jnp.maximum, blocks)
        m = jnp.maximum(jnp.max(folded, axis=1, keepdims=True), sink)
        e_c = jnp.exp2(s_c - m).astype(bf16)
        e_l = jnp.exp2(s_l - m).astype(bf16)
        ov = (jnp.dot(e_c, cvext[...], preferred_element_type=f32)
              + jnp.dot(e_l, vl, preferred_element_type=f32))
        den = ov[:, LANES:] + jnp.exp2(sink - m)
        o = ov[:, :LANES] / den
        for c in range(n_chunks):
            oc = jnp.where(low, o[(2 * c) * blk:(2 * c + 1) * blk],
                           o[(2 * c + 1) * blk:(2 * c + 2) * blk])
            o_ref[0, pl.ds(r0, blk), c * LANES:(c + 1) * LANES] = oc.astype(bf16)
        return carry

    lax.fori_loop(0, tq // blk, sub, 0, unroll=2)


def _attention(q, k, v, ck, cv, sinkcol):
    b, l, qw = q.shape
    c = ck.shape[1]
    tq = ATT_TILE
    r = tq // ATT_BLOCK
    nb = l // ATT_BLOCK
    prev = pl.BlockSpec((1, ATT_BLOCK, LANES), lambda i, j: (i, jnp.maximum(j * r - 1, 0), 0))
    main = pl.BlockSpec((1, tq, LANES), lambda i, j: (i, j, 0))
    nxt = pl.BlockSpec((1, ATT_BLOCK, LANES), lambda i, j: (i, jnp.minimum(j * r + r, nb - 1), 0))
    cspec = pl.BlockSpec((1, c, LANES), lambda i, j: (i, 0, 0))
    return pl.pallas_call(
        functools.partial(_attn_kernel, l),
        grid=(b, l // tq),
        in_specs=[pl.BlockSpec((1, tq, qw), lambda i, j: (i, j, 0)),
                  prev, main, nxt, prev, main, nxt, cspec, cspec,
                  pl.BlockSpec(sinkcol.shape, lambda i, j: (0, 0))],
        out_specs=pl.BlockSpec((1, tq, qw), lambda i, j: (i, j, 0)),
        out_shape=jax.ShapeDtypeStruct((b, l, qw), bf16),
        scratch_shapes=[pltpu.VMEM((tq + 2 * ATT_BLOCK, LANES), bf16),
                        pltpu.VMEM((tq + 2 * ATT_BLOCK, 2 * LANES), bf16),
                        pltpu.VMEM((c, 2 * LANES), bf16)],
        compiler_params=_cparams(("parallel", "parallel")),
        name="attention",
    )(q, k, k, k, v, v, v, ck, cv, sinkcol)


def _pack_pair(lo, hi):
    lo = lax.bitcast_convert_type(lo.astype(bf16).astype(f32), u32)
    hi = lax.bitcast_convert_type(hi.astype(bf16).astype(f32), u32)
    return (lo >> 16) | (hi & jnp.uint32(0xFFFF0000))


def _fourier1_kernel(x_ref, mod_ref, g_ref, w_ref, cs_ref, m_ref, ct_ref, st_ref, z_ref, ab_ref):
    n1 = x_ref.shape[1]
    nt = x_ref.shape[2]
    x = x_ref[0].reshape(n1 * nt, x_ref.shape[3])
    h = _norm_mod(x, g_ref[...], mod_ref[0, 0:1, :], mod_ref[0, 1:2, :]).astype(bf16)
    p = jnp.dot(h, w_ref[...], preferred_element_type=f32)
    for g in range(FOURIER_GROUPS):
        ug = p[:, g * LANES:(g + 1) * LANES].astype(bf16)
        ab = jnp.dot(ug, cs_ref[...], preferred_element_type=f32)
        ab_ref[0] = ab[:, :LANES]
        ab_ref[1] = ab[:, LANES:]
        for t in range(nt):
            stack = jnp.concatenate([ab_ref[0, pl.ds(t, n1, stride=nt), :],
                                     ab_ref[1, pl.ds(t, n1, stride=nt), :]], axis=0).astype(bf16)
            z = jnp.dot(m_ref[...], stack, preferred_element_type=f32)
            zr, zn = z[:n1], z[n1:]
            ct, st = ct_ref[t], st_ref[t]
            z_ref[0, g, t] = _pack_pair(ct * zr - st * zn, ct * zn + st * zr)


def _fourier2_kernel(scale, z_ref, m_ref, o_ref, zbuf, ybuf):
    _, grp, n2, tk, w = z_ref.shape
    for g in range(grp):
        zbuf[...] = z_ref[0, g].reshape(n2 * tk, w)
        for j in range(tk):
            zp = zbuf[pl.ds(j, n2, stride=tk), :]
            zr = lax.bitcast_convert_type(zp << 16, f32).astype(bf16)
            zn = lax.bitcast_convert_type(zp & jnp.uint32(0xFFFF0000), f32).astype(bf16)
            y = jnp.dot(m_ref[...], jnp.concatenate([zr, zn], axis=0), preferred_element_type=f32)
            ybuf[pl.ds(j, n2, stride=tk), :] = y * scale
        o_ref[0, g] = ybuf[...].reshape(n2, tk, w)


def _fourier(x, mod, g, w_f, cs):
    b, l, d = x.shape
    n2 = DFT_INNER
    n1 = l // n2
    grp, w = FOURIER_GROUPS, FOURIER_GROUP_W
    t2 = SUBLANES
    k1 = jnp.arange(n1, dtype=i32)
    ang1 = ((k1[:, None] * k1[None, :]) % n1).astype(f32) * (2.0 * math.pi / n1)
    c1, s1 = jnp.cos(ang1), jnp.sin(ang1)
    m1 = jnp.concatenate([jnp.concatenate([c1, -s1], axis=1),
                          jnp.concatenate([s1, c1], axis=1)], axis=0).astype(bf16)
    l2 = jnp.arange(n2, dtype=i32)
    angt = ((l2[:, None] * k1[None, :]) % l).astype(f32) * (2.0 * math.pi / l)
    ct = jnp.broadcast_to(jnp.cos(angt)[:, :, None], (n2, n1, w))
    st = jnp.broadcast_to(jnp.sin(angt)[:, :, None], (n2, n1, w))
    ang2 = ((l2[:, None] * l2[None, :]) % n2).astype(f32) * (2.0 * math.pi / n2)
    m2 = jnp.concatenate([jnp.cos(ang2), -jnp.sin(ang2)], axis=1).astype(bf16)

    tspec = pl.BlockSpec((t2, n1, w), lambda t, i: (t, 0, 0))
    z = pl.pallas_call(
        _fourier1_kernel,
        grid=(n2 // t2, b),
        in_specs=[pl.BlockSpec((1, n1, t2, d), lambda t, i: (i, 0, t, 0)),
                  pl.BlockSpec((1, 6, d), lambda t, i: (i, 0, 0)),
                  pl.BlockSpec((1, d), lambda t, i: (0, 0)),
                  pl.BlockSpec(w_f.shape, lambda t, i: (0, 0)),
                  pl.BlockSpec(cs.shape, lambda t, i: (0, 0)),
                  pl.BlockSpec(m1.shape, lambda t, i: (0, 0)), tspec, tspec],
        out_specs=pl.BlockSpec((1, grp, t2, n1, w), lambda t, i: (i, 0, t, 0, 0)),
        out_shape=jax.ShapeDtypeStruct((b, grp, n2, n1, w), u32),
        scratch_shapes=[pltpu.VMEM((2, n1 * t2, w), f32)],
        compiler_params=_cparams(("parallel", "parallel")),
        name="fourier_outer",
    )(x.reshape(b, n1, n2, d), mod, g, w_f, cs, m1, ct, st)

    tk = SUBLANES
    y = pl.pallas_call(
        functools.partial(_fourier2_kernel, 1.0 / math.sqrt(l * w)),
        grid=(b, n1 // tk),
        in_specs=[pl.BlockSpec((1, grp, n2, tk, w), lambda i, t: (i, 0, 0, t, 0)),
                  pl.BlockSpec(m2.shape, lambda i, t: (0, 0))],
        out_specs=pl.BlockSpec((1, grp, n2, tk, w), lambda i, t: (i, 0, 0, t, 0)),
        out_shape=jax.ShapeDtypeStruct((b, grp, n2, n1, w), f32),
        scratch_shapes=[pltpu.VMEM((n2 * tk, w), u32), pltpu.VMEM((n2 * tk, w), f32)],
        compiler_params=_cparams(("parallel", "parallel")),
        name="fourier_inner",
    )(z, m2)
    return y.reshape(b, grp, l, w)


def _first_max4(a):
    m = jnp.maximum(jnp.maximum(a[0], a[1]), jnp.maximum(a[2], a[3]))
    idx = jnp.where(a[0] == m, 0, jnp.where(a[1] == m, 1, jnp.where(a[2] == m, 2, 3)))
    return m, idx


def _pick4(vals, idx):
    return jnp.where(idx == 0, vals[0], jnp.where(idx == 1, vals[1],
                                                   jnp.where(idx == 2, vals[2], vals[3])))


def _route(f, rw_ref, rb_ref, tri_ref, base_ref, first_step, ri_ref, wc_ref, cnt_ref, meta_ref):
    tm = f.shape[0]
    f_hi = f.astype(bf16)
    f_lo = (f - f_hi.astype(f32)).astype(bf16)
    rw2 = rw_ref[...]
    part = jnp.dot(f_hi, rw2, preferred_element_type=f32)
    logits = (part[:, :LANES] + part[:, LANES:]
              + jnp.dot(f_lo, rw2[:, :LANES], preferred_element_type=f32))
    sc = jax.nn.sigmoid(logits)
    st = sc.T
    bt = (sc + rb_ref[...]).T
    neg = jnp.full((1, tm), -jnp.inf, f32)
    gs = []
    for g in range(N_GROUPS):
        a = [bt[4 * g + i: 4 * g + i + 1] for i in range(4)]
        m1, i1 = _first_max4(a)
        rest = [jnp.where(i1 == i, neg, a[i]) for i in range(4)]
        m2, _ = _first_max4(rest)
        gs.append(m1 + m2)
    _, gsel = _first_max4(gs)
    a = [_pick4([bt[4 * g + i: 4 * g + i + 1] for g in range(N_GROUPS)], gsel) for i in range(4)]
    s = [_pick4([st[4 * g + i: 4 * g + i + 1] for g in range(N_GROUPS)], gsel) for i in range(4)]
    _, i1 = _first_max4(a)
    rest = [jnp.where(i1 == i, neg, a[i]) for i in range(4)]
    _, i2 = _first_max4(rest)
    w1 = _pick4(s, i1)
    w2 = _pick4(s, i2)
    tot = w1 + w2
    w1 = w1 / tot
    w2 = w2 / tot
    e0 = gsel * EXPERTS_PER_GROUP + i1
    e1 = gsel * EXPERTS_PER_GROUP + i2

    @pl.when(first_step)
    def _():
        base_ref[...] = jnp.zeros_like(base_ref)

    td = DISPATCH_TILE
    eid = lax.broadcasted_iota(i32, (N_EXPERTS, tm), 0)
    oh0 = (eid == e0).astype(f32)
    oh1 = (eid == e1).astype(f32)
    oh = oh0 + oh1
    before = jnp.dot(oh.astype(bf16), tri_ref[...], preferred_element_type=f32)
    lane_tile = lax.broadcasted_iota(i32, (N_EXPERTS, tm), 1) // td
    ei = lax.broadcasted_iota(i32, (N_EXPERTS, N_EXPERTS), 0)
    ej = lax.broadcasted_iota(i32, (N_EXPERTS, N_EXPERTS), 1)
    strict_lower = (ej < ei).astype(f32)
    run_start = jnp.zeros((N_EXPERTS, tm), f32)
    goff = base_ref[...]
    for s in range(tm // td):
        cnt_s = jnp.sum(oh[:, s * td:(s + 1) * td], axis=1, keepdims=True)
        pad_s = jnp.floor((cnt_s + 7.0) * 0.125) * 8.0
        pad_b = jnp.broadcast_to(pad_s, (N_EXPERTS, LANES))
        start_b = jnp.dot(strict_lower, pad_b, precision=HIGHEST, preferred_element_type=f32)
        run_start = jnp.where(lane_tile == s, start_b[:, 0:1], run_start)
        meta_ref[s, 0] = start_b.astype(i32)
        meta_ref[s, 1] = pad_b.astype(i32)
        meta_ref[s, 2] = goff.astype(i32)
        goff = goff + pad_b
    base_ref[...] = goff
    cnt_ref[...] = goff
    pos = before + run_start
    lp0 = jnp.sum(oh0 * pos, axis=0, keepdims=True)
    lp1 = jnp.sum(oh1 * pos, axis=0, keepdims=True)
    zi = jnp.zeros((1, tm), i32)
    ri_ref[...] = jnp.concatenate(
        [lp0.astype(i32), lp1.astype(i32), e0, e1, zi, zi, zi, zi], axis=0)
    zf = jnp.zeros((LANES - 4, tm), f32)
    wc_ref[...] = jnp.concatenate([w1, w2, lp0, lp1, zf], axis=0).T


def _outproj_kernel(yf_ref, o_ref, x_ref, mod_ref, w_ref, g_ref, rw_ref, rb_ref, tri_ref,
                    x1_ref, f_ref, ri_ref, wc_ref, cnt_ref, meta_ref, base_ref):
    mix = jnp.concatenate([yf_ref[0, g].astype(bf16) for g in range(FOURIER_GROUPS)] + [o_ref[0]],
                          axis=1)
    y = jnp.dot(mix, w_ref[...], preferred_element_type=f32)
    x1 = x_ref[0] + mod_ref[0, 2:3, :] * y
    x1_ref[0] = x1
    f = _norm_mod(x1, g_ref[...], mod_ref[0, 3:4, :], mod_ref[0, 4:5, :])
    f_ref[0] = f.astype(bf16)
    first = (pl.program_id(0) == 0) & (pl.program_id(1) == 0)
    _route(f, rw_ref, rb_ref, tri_ref, base_ref, first, ri_ref, wc_ref, cnt_ref, meta_ref)


def _route_specs(b, l, tm):
    nl = l // tm
    rw = lambda d: pl.BlockSpec((d, 2 * LANES), lambda i, j: (0, 0))
    rb = pl.BlockSpec((1, LANES), lambda i, j: (0, 0))
    tri = pl.BlockSpec((tm, tm), lambda i, j: (0, 0))
    ns = tm // DISPATCH_TILE
    out_specs = [pl.BlockSpec((8, tm), lambda i, j: (0, i * nl + j)),
                 pl.BlockSpec((tm, LANES), lambda i, j: (i * nl + j, 0)),
                 pl.BlockSpec((N_EXPERTS, LANES), lambda i, j: (0, 0)),
                 pl.BlockSpec((ns, 3, N_EXPERTS, LANES), lambda i, j: (i * nl + j, 0, 0, 0))]
    out_shape = [jax.ShapeDtypeStruct((8, b * l), i32),
                 jax.ShapeDtypeStruct((b * l, LANES), f32),
                 jax.ShapeDtypeStruct((N_EXPERTS, LANES), f32),
                 jax.ShapeDtypeStruct((b * l // DISPATCH_TILE, 3, N_EXPERTS, LANES), i32)]
    return rw, rb, tri, out_specs, out_shape


def _outproj(yf, o, x, mod, w, g, rw, rb, tri):
    b, l, d = x.shape
    tm = TOKEN_TILE
    rws, rbs, tris, r_specs, r_shapes = _route_specs(b, l, tm)
    row = pl.BlockSpec((1, tm, d), lambda i, j: (i, j, 0))
    return pl.pallas_call(
        _outproj_kernel,
        grid=(b, l // tm),
        in_specs=[pl.BlockSpec((1, FOURIER_GROUPS, tm, LANES), lambda i, j: (i, 0, j, 0)),
                  pl.BlockSpec((1, tm, o.shape[2]), lambda i, j: (i, j, 0)),
                  row,
                  pl.BlockSpec((1, 6, d), lambda i, j: (i, 0, 0)),
                  pl.BlockSpec(w.shape, lambda i, j: (0, 0)),
                  pl.BlockSpec((1, d), lambda i, j: (0, 0)),
                  rws(d), rbs, tris],
        out_specs=[row, row] + r_specs,
        out_shape=[jax.ShapeDtypeStruct((b, l, d), f32),
                   jax.ShapeDtypeStruct((b, l, d), bf16)] + r_shapes,
        scratch_shapes=[pltpu.VMEM((N_EXPERTS, LANES), f32)],
        compiler_params=_cparams(("arbitrary", "arbitrary")),
        name="outproj_router",
    )(yf, o, x, mod, w, g, rw, rb, tri)


def _conv_kernel(seq_len, up_ref, um_ref, un_ref, x_ref, mod_ref, dw_ref, db_ref, lg_ref, lb_ref,
                 w_ref, pb_ref, g_ref, rw_ref, rb_ref, tri_ref,
                 x1_ref, f_ref, ri_ref, wc_ref, cnt_ref, meta_ref, base_ref, ext, conv_out):
    j = pl.program_id(1)
    tm = um_ref.shape[1]
    hl = CONV_HALO
    half = CONV_W // 2
    ext[0:hl] = jnp.where(j > 0, up_ref[0], jnp.zeros_like(up_ref[0]))
    ext[hl:hl + tm] = um_ref[0]
    ext[hl + tm:] = jnp.where((j + 1) * tm < seq_len, un_ref[0], jnp.zeros_like(un_ref[0]))
    base = hl - half
    span = (CONV_W - 1) // SUBLANES * SUBLANES
    rows = CONV_ROWS

    def lane_chunk(c, carry):
        lanes = pl.ds(pl.multiple_of(c * LANES, LANES), LANES)
        for r in range(0, tm, rows):
            part = jnp.broadcast_to(db_ref[:, lanes], (rows, LANES))
            for phase in range(SUBLANES):
                win = ext[base + phase + r: base + phase + r + rows + span, lanes]
                same = None
                for t in range(phase, CONV_W, SUBLANES):
                    term = win[t - phase: t - phase + rows, :] * dw_ref[t:t + 1, lanes]
                    same = term if same is None else same + term
                part = part + same
            conv_out[r:r + rows, lanes] = part
        return carry

    lax.fori_loop(0, um_ref.shape[2] // LANES, lane_chunk, 0)
    acc = conv_out[...]
    mu = jnp.mean(acc, axis=-1, keepdims=True)
    cen = acc - mu
    var = jnp.mean(cen * cen, axis=-1, keepdims=True)
    ln = cen * lax.rsqrt(var + EPS) * lg_ref[...] + lb_ref[...]
    act = ln * jax.nn.sigmoid(ln)
    y = jnp.dot(act.astype(bf16), w_ref[...], preferred_element_type=f32) + pb_ref[...]
    x1 = x_ref[0] + mod_ref[0, 2:3, :] * y
    x1_ref[0] = x1
    f = _norm_mod(x1, g_ref[...], mod_ref[0, 3:4, :], mod_ref[0, 4:5, :])
    f_ref[0] = f.astype(bf16)
    first = (pl.program_id(0) == 0) & (j == 0)
    _route(f, rw_ref, rb_ref, tri_ref, base_ref, first, ri_ref, wc_ref, cnt_ref, meta_ref)


def _conv(u, x, mod, dw_w, dw_b, ln_g, ln_b, pw2_w, pw2_b, g, rw, rb, tri):
    b, l, d = x.shape
    tm = TOKEN_TILE
    hl = CONV_HALO
    r = tm // hl
    nh = l // hl
    rws, rbs, tris, r_specs, r_shapes = _route_specs(b, l, tm)
    row = pl.BlockSpec((1, tm, d), lambda i, j: (i, j, 0))
    vec = pl.BlockSpec((1, d), lambda i, j: (0, 0))
    return pl.pallas_call(
        functools.partial(_conv_kernel, l),
        grid=(b, l // tm),
        in_specs=[pl.BlockSpec((1, hl, d), lambda i, j: (i, jnp.maximum(j * r - 1, 0), 0)),
                  row,
                  pl.BlockSpec((1, hl, d), lambda i, j: (i, jnp.minimum(j * r + r, nh - 1), 0)),
                  row,
                  pl.BlockSpec((1, 6, d), lambda i, j: (i, 0, 0)),
                  pl.BlockSpec(dw_w.shape, lambda i, j: (0, 0)),
                  vec, vec, vec,
                  pl.BlockSpec(pw2_w.shape, lambda i, j: (0, 0)),
                  vec, vec, rws(d), rbs, tris],
        out_specs=[row, row] + r_specs,
        out_shape=[jax.ShapeDtypeStruct((b, l, d), f32),
                   jax.ShapeDtypeStruct((b, l, d), bf16)] + r_shapes,
        scratch_shapes=[pltpu.VMEM((N_EXPERTS, LANES), f32),
                        pltpu.VMEM((tm + 2 * hl, d), f32),
                        pltpu.VMEM((tm, d), f32)],
        compiler_params=_cparams(("arbitrary", "arbitrary")),
        name="conv_router",
    )(u, u, u, x, mod, dw_w, dw_b, ln_g, ln_b, pw2_w, pw2_b, g, rw, rb, tri)


def _pack_bf16_pairs(x):
    h = x.shape[1] // 2
    lo = lax.bitcast_convert_type(x[:, :h], u32)
    hi = lax.bitcast_convert_type(x[:, h:], u32)
    return (lo >> 16) | (hi & jnp.uint32(0xFFFF0000))


def _unpack_bf16_pairs(u):
    lo = lax.bitcast_convert_type(u << 16, f32)
    hi = lax.bitcast_convert_type(u & jnp.uint32(0xFFFF0000), f32)
    return jnp.concatenate([lo, hi], axis=1).astype(bf16)


def _run_copies(meta, tile, local_ref, hbm_ref, sem, to_hbm):
    start_ref, size_ref, dst_ref = meta
    for e in range(N_EXPERTS):
        k = tile * N_EXPERTS + e
        size = pl.multiple_of(size_ref[k], RUN_ALIGN)

        @pl.when(size > 0)
        def _():
            loc = local_ref.at[pl.ds(pl.multiple_of(start_ref[k], RUN_ALIGN), size)]
            glob = hbm_ref.at[pl.ds(pl.multiple_of(dst_ref[k], RUN_ALIGN), size)]
            if to_hbm:
                pltpu.make_async_copy(loc, glob, sem).start()
            else:
                pltpu.make_async_copy(glob, loc, sem).start()


def _wait_rows(rows, local_ref, hbm_ref, sem):
    rows = pl.multiple_of(rows, RUN_ALIGN)

    @pl.when(rows > 0)
    def _():
        pltpu.make_async_copy(local_ref.at[pl.ds(0, rows)], hbm_ref.at[pl.ds(0, rows)], sem).wait()


def _dispatch_kernel(start_ref, size_ref, dst_ref, tot_ref, tail_start_ref, tail_size_ref, nv_ref,
                     f_ref, lp_ref, xs_ref, loc, zbuf, sem, zsem):
    i = pl.program_id(0)
    n = pl.num_programs(0)
    slot = i % 2
    meta = (start_ref, size_ref, dst_ref)

    @pl.when(i >= 2)
    def _():
        _wait_rows(tot_ref[i - 2], loc.at[slot], xs_ref, sem.at[slot])

    rows = loc.shape[1]
    td = f_ref.shape[0]
    r = lax.broadcasted_iota(i32, (rows, td), 0)
    onehot = ((r == lp_ref[0:1, :]) | (r == lp_ref[1:2, :])).astype(bf16)
    sorted_rows = jnp.dot(onehot, f_ref[...], preferred_element_type=f32)
    loc[slot] = _pack_bf16_pairs(sorted_rows)
    _run_copies(meta, i, loc.at[slot], xs_ref, sem.at[slot], to_hbm=True)

    @pl.when(i == n - 1)
    def _():
        zbuf[...] = jnp.zeros_like(zbuf)
        total = 0
        for e in range(N_EXPERTS):
            size = pl.multiple_of(tail_size_ref[e], RUN_ALIGN)
            total = total + size

            @pl.when(size > 0)
            def _():
                pltpu.make_async_copy(
                    zbuf.at[pl.ds(0, size)],
                    xs_ref.at[pl.ds(pl.multiple_of(tail_start_ref[e], RUN_ALIGN), size)], zsem).start()

        _wait_rows(total, zbuf, xs_ref, zsem)

        def zero_block(k, c):
            pltpu.make_async_copy(zbuf, xs_ref.at[pl.ds(pl.multiple_of(k * zbuf.shape[0], RUN_ALIGN),
                                                        zbuf.shape[0])], zsem).start()
            return c

        def wait_block(k, c):
            pltpu.make_async_copy(zbuf, xs_ref.at[pl.ds(0, zbuf.shape[0])], zsem).wait()
            return c

        n_blocks = xs_ref.shape[0] // zbuf.shape[0]
        lax.fori_loop(nv_ref[0], n_blocks, zero_block, 0)
        lax.fori_loop(nv_ref[0], n_blocks, wait_block, 0)
        _wait_rows(tot_ref[i], loc.at[slot], xs_ref, sem.at[slot])

        @pl.when(i >= 1)
        def _():
            _wait_rows(tot_ref[i - 1], loc.at[1 - slot], xs_ref, sem.at[1 - slot])


def _dispatch(tables, f2, ri, n_slots):
    t, d = f2.shape
    td = DISPATCH_TILE
    return pl.pallas_call(
        _dispatch_kernel,
        grid_spec=pltpu.PrefetchScalarGridSpec(
            num_scalar_prefetch=7,
            grid=(t // td,),
            in_specs=[pl.BlockSpec((td, d), lambda i, *_: (i, 0)),
                      pl.BlockSpec((8, td), lambda i, *_: (0, i))],
            out_specs=pl.BlockSpec(memory_space=pl.ANY),
            scratch_shapes=[pltpu.VMEM((2, LOCAL_ROWS, d // 2), u32),
                            pltpu.VMEM((EXPERT_ROWS, d // 2), u32),
                            pltpu.SemaphoreType.DMA((2,)), pltpu.SemaphoreType.DMA(())]),
        out_shape=jax.ShapeDtypeStruct((n_slots, d // 2), u32),
        compiler_params=_cparams(("arbitrary",)),
        name="moe_dispatch",
    )(*tables, f2, ri)


def _expert_kernel(be_ref, nv_ref, x_ref, wg_ref, wu_ref, wd_ref, y_ref, wgb, wub, wdb):
    i = pl.program_id(0)
    changed = jnp.logical_or(i == 0, be_ref[i] != be_ref[jnp.maximum(i - 1, 0)])

    @pl.when(changed)
    def _():
        wgb[...] = wg_ref[0, 0].astype(bf16)
        wub[...] = wu_ref[0, 0].astype(bf16)
        wdb[...] = wd_ref[0, 0].astype(bf16)

    @pl.when(i < nv_ref[0])
    def _():
        xb = _unpack_bf16_pairs(x_ref[...])
        gate = jnp.dot(xb, wgb[...], preferred_element_type=f32)
        up = jnp.dot(xb, wub[...], preferred_element_type=f32)
        hid = (gate * jax.nn.sigmoid(gate) * up).astype(bf16)
        y = jnp.dot(hid, wdb[...], preferred_element_type=f32)
        y_ref[...] = _pack_bf16_pairs(y.astype(bf16).astype(f32))

    @pl.when(i >= nv_ref[0])
    def _():
        y_ref[...] = jnp.zeros_like(y_ref)


def _experts(block_e, n_valid, xs, w_gate, w_up, w_down, layer):
    ns, dh = xs.shape
    tb = EXPERT_ROWS
    d, ff = w_gate.shape[2:]
    xmap = lambda i, be, nv: (jnp.maximum(jnp.minimum(i, nv[0] - 1), 0), 0)
    wmap = lambda i, be, nv: (layer, be[i], 0, 0)
    return pl.pallas_call(
        _expert_kernel,
        grid_spec=pltpu.PrefetchScalarGridSpec(
            num_scalar_prefetch=2,
            grid=(ns // tb,),
            in_specs=[pl.BlockSpec((tb, dh), xmap),
                      pl.BlockSpec((1, 1, d, ff), wmap),
                      pl.BlockSpec((1, 1, d, ff), wmap),
                      pl.BlockSpec((1, 1, ff, d), wmap)],
            out_specs=pl.BlockSpec((tb, dh), lambda i, be, nv: (i, 0)),
            scratch_shapes=[pltpu.VMEM((d, ff), bf16), pltpu.VMEM((d, ff), bf16),
                            pltpu.VMEM((ff, d), bf16)]),
        out_shape=jax.ShapeDtypeStruct((ns, dh), u32),
        compiler_params=_cparams(("arbitrary",)),
        name="moe_experts",
    )(block_e, n_valid, xs, w_gate, w_up, w_down)


def _combine_kernel(final, start_ref, size_ref, dst_ref, tot_ref, ys_ref, wc_ref, x_ref, mod_ref,
                    g_ref, *rest):
    if final:
        o_ref, loc, sem = rest
    else:
        nmod_ref, w_ref, b_ref, o_ref, u_ref, loc, sem = rest
    i = pl.program_id(0)
    n = pl.num_programs(0)
    slot = i % 2
    meta = (start_ref, size_ref, dst_ref)

    @pl.when(i == 0)
    def _():
        loc[...] = jnp.zeros_like(loc)
        _run_copies(meta, i, loc.at[slot], ys_ref, sem.at[slot], to_hbm=False)

    @pl.when(i + 1 < n)
    def _():
        _run_copies(meta, i + 1, loc.at[1 - slot], ys_ref, sem.at[1 - slot], to_hbm=False)

    _wait_rows(tot_ref[i], loc.at[slot], ys_ref, sem.at[slot])
    rows = loc.shape[1]
    td = x_ref.shape[0]
    wc = wc_ref[...]
    c = lax.broadcasted_iota(i32, (td, rows), 1)
    sel = jnp.concatenate([(c == wc[:, 2:3].astype(i32)).astype(bf16),
                           (c == wc[:, 3:4].astype(i32)).astype(bf16)], axis=0)
    picked = jnp.dot(sel, _unpack_bf16_pairs(loc[slot]), preferred_element_type=f32)
    y = wc[:, 0:1] * picked[:td] + wc[:, 1:2] * picked[td:]
    xo = x_ref[...] + mod_ref[0, 5:6, :] * y
    if final:
        ms = jnp.mean(xo * xo, axis=-1, keepdims=True)
        o_ref[...] = xo * lax.rsqrt(ms + EPS) * g_ref[...]
    else:
        o_ref[...] = xo
        h = _norm_mod(xo, g_ref[...], nmod_ref[0, 0:1, :], nmod_ref[0, 1:2, :])
        p = jnp.dot(h.astype(bf16), w_ref[...], preferred_element_type=f32) + b_ref[...]
        ch = p.shape[1] // 2
        u_ref[...] = p[:, :ch] * jax.nn.sigmoid(p[:, ch:])


def _combine(tables, ys, wc, x, mod, g, glu=None):
    b, l, d = x.shape
    td = DISPATCH_TILE
    per_batch = l // td
    const = lambda i, *_: (0, 0)
    tile = lambda cols: pl.BlockSpec((td, cols), lambda i, *_: (i, 0))
    mod_spec = pl.BlockSpec((1, 6, d), lambda i, *_: (i // per_batch, 0, 0))
    in_specs = [pl.BlockSpec(memory_space=pl.ANY), tile(LANES), tile(d), mod_spec,
                pl.BlockSpec((1, d), const)]
    args = [ys, wc, x.reshape(b * l, d), mod, g]
    out_specs = [tile(d)]
    out_shape = [jax.ShapeDtypeStruct((b * l, d), f32)]
    if glu is not None:
        nmod, w, bias = glu
        in_specs += [mod_spec, pl.BlockSpec(w.shape, const), pl.BlockSpec(bias.shape, const)]
        args += [nmod, w, bias]
        out_specs.append(tile(w.shape[1] // 2))
        out_shape.append(jax.ShapeDtypeStruct((b * l, w.shape[1] // 2), f32))
    outs = pl.pallas_call(
        functools.partial(_combine_kernel, glu is None),
        grid_spec=pltpu.PrefetchScalarGridSpec(
            num_scalar_prefetch=4,
            grid=(b * per_batch,),
            in_specs=in_specs,
            out_specs=out_specs,
            scratch_shapes=[pltpu.VMEM((2, LOCAL_ROWS, d // 2), u32),
                            pltpu.SemaphoreType.DMA((2,))]),
        out_shape=out_shape,
        compiler_params=_cparams(("arbitrary",)),
        name="moe_combine",
    )(*tables, *args)
    return [o.reshape(b, l, -1) for o in outs]


def _moe(f, routed, x, mod, g, w_gate, w_up, w_down, layer, glu=None):
    ri, wc, cnt, meta = routed
    b, l, d = x.shape
    t = b * l
    tb = EXPERT_ROWS
    n_tiles = t // DISPATCH_TILE
    used = cnt[:, 0].astype(i32)
    region = (used + tb - 1) // tb * tb
    gend = jnp.cumsum(region)
    gstart = gend - region
    max_rows = 2 * t + n_tiles * N_EXPERTS * (RUN_ALIGN - 1) + N_EXPERTS * (tb - 1)
    n_blocks = -(-max_rows // tb)
    m = meta[:, :, :, 0]
    run_start = m[:, 0].reshape(-1)
    run_size = m[:, 1].reshape(-1)
    run_dst = (m[:, 2] + gstart[None, :]).reshape(-1)
    tile_rows = jnp.sum(m[:, 1], axis=1)
    block_row = jnp.arange(n_blocks, dtype=i32) * tb
    block_e = jnp.minimum(jnp.sum((block_row[:, None] >= gend[None, :]).astype(i32), axis=1),
                          N_EXPERTS - 1)
    n_valid = (gend[-1] // tb).reshape(1)
    xs = _dispatch((run_start, run_size, run_dst, tile_rows, gstart + used, region - used, n_valid),
                   f.reshape(t, d), ri, n_blocks * tb)
    ys = _experts(block_e, n_valid, xs, w_gate, w_up, w_down, layer)
    return _combine((run_start, run_size, run_dst, tile_rows), ys, wc, x, mod, g, glu)


def _rope_tables(l):
    lane = jnp.arange(LANES)
    dh = lane % HEAD_DIM
    inv = ROPE_THETA ** (-(dh % 16).astype(f32) / 16.0)
    pos = jnp.arange(l)
    row = (pos // GRID_W).astype(f32)
    col = (pos % GRID_W).astype(f32)
    p = jnp.where((dh // 32)[None, :] == 0, row[:, None], col[:, None])
    ang = p * inv[None, :]
    sign = jnp.where((dh % 32) < 16, -1.0, 1.0).astype(f32)
    return jnp.cos(ang), jnp.sin(ang) * sign[None, :]


def kernel(x, c, ctx, c_ctx, ada_w, ada_b, norm_mix_g, norm_ffn_g, even_w_in, even_w_out, even_sink, conv_pw1_w, conv_pw1_b, conv_dw_w, conv_dw_b, conv_ln_g, conv_ln_b, conv_pw2_w, conv_pw2_b, router_w, router_b, moe_w_gate, moe_w_up, moe_w_down, final_norm_g):
    b, l, d = x.shape
    depth = ada_w.shape[0]
    assert depth == 2 and b < COND_ROWS
    ctx_row = b
    cond = jnp.zeros((COND_ROWS, d), f32).at[:b].set(c).at[ctx_row].set(c_ctx)
    mods = _adaln(cond, ada_w, ada_b).reshape(depth, COND_ROWS, 6, d)

    heads = jnp.arange(N_HEADS).reshape(N_KV_HEADS, N_HEADS // N_KV_HEADS).T.reshape(-1)
    qperm = (heads[:, None] * HEAD_DIM + jnp.arange(HEAD_DIM)[None, :]).reshape(-1)
    fw = FOURIER_GROUPS * FOURIER_GROUP_W
    qw = N_HEADS * HEAD_DIM
    w_in = even_w_in[0]
    w_in_p = jnp.concatenate([w_in[:, :fw], w_in[:, fw:fw + qw][:, qperm], w_in[:, fw + qw:]],
                             axis=1).astype(bf16)
    w_out = even_w_out[0]
    w_out_p = jnp.concatenate([w_out[:fw], w_out[fw:][qperm]], axis=0).astype(bf16)
    sinkcol = jnp.repeat(even_sink[0][heads].astype(f32) * LOG2E, ATT_BLOCK).reshape(-1, 1)

    cidx = jnp.arange(FOURIER_GROUP_W, dtype=i32)
    angc = ((cidx[:, None] * cidx[None, :]) % FOURIER_GROUP_W).astype(f32) * (2.0 * math.pi / FOURIER_GROUP_W)
    cs = jnp.concatenate([jnp.cos(angc), jnp.sin(angc)], axis=1).astype(bf16)
    cos_t, sin_t = _rope_tables(l)

    rw32 = jnp.zeros((d, LANES), f32).at[:, :N_EXPERTS].set(router_w.astype(f32))
    rw_hi = rw32.astype(bf16)
    rw = jnp.concatenate([rw_hi, (rw32 - rw_hi.astype(f32)).astype(bf16)], axis=1)
    rb = jnp.zeros((1, LANES), f32).at[0, :N_EXPERTS].set(router_b)
    tpos = jnp.arange(TOKEN_TILE)
    tri = ((tpos[:, None] < tpos[None, :])
           & (tpos[:, None] // DISPATCH_TILE == tpos[None, :] // DISPATCH_TILE)).astype(bf16)
    row = lambda v: v.reshape(1, -1)

    q, k, v = _inproj(x, mods[0], row(norm_mix_g[0]), w_in_p[:, fw:], cos_t, sin_t)
    ck, cv = _ctxkv(ctx, mods[0], row(norm_mix_g[0]), w_in_p[:, fw + qw:], ctx_row)
    yf = _fourier(x, mods[0], row(norm_mix_g[0]), w_in_p[:, :fw], cs)
    att = _attention(q, k, v, ck, cv, sinkcol)
    x1, f, *routed = _outproj(yf, att, x, mods[0], w_out_p, row(norm_ffn_g[0]), rw, rb, tri)
    x2, u = _moe(f, routed, x1, mods[0], row(norm_mix_g[1]), moe_w_gate, moe_w_up, moe_w_down,
                 layer=0, glu=(mods[1], conv_pw1_w[0].astype(bf16), row(conv_pw1_b[0])))

    x3, f, *routed = _conv(u, x2, mods[1], conv_dw_w[0], row(conv_dw_b[0]), row(conv_ln_g[0]),
                               row(conv_ln_b[0]), conv_pw2_w[0].astype(bf16), row(conv_pw2_b[0]),
                               row(norm_ffn_g[1]), rw, rb, tri)
    (out,) = _moe(f, routed, x3, mods[1], row(final_norm_g), moe_w_gate, moe_w_up, moe_w_down,
                  layer=1)
    return out
```

```python
import functools
import math

import jax
import jax.numpy as jnp
from jax import lax
from jax.experimental import pallas as pl
from jax.experimental.pallas import tpu as pltpu

f32 = jnp.float32
bf16 = jnp.bfloat16
i32 = jnp.int32
u32 = jnp.uint32
HIGHEST = lax.Precision.HIGHEST

GRID_W = 64
HEAD_DIM = 64
N_HEADS = 8
N_KV_HEADS = 2
WINDOW = 128
ATT_BLOCK = 128
ROPE_THETA = 10000.0
FOURIER_GROUPS = 4
FOURIER_GROUP_W = 128
CONV_W = 31
N_EXPERTS = 16
N_GROUPS = 4
EXPERTS_PER_GROUP = 4
EXPERT_FF = 512
EPS = 1e-6
NEG_INF = -1e30
LOG2E = math.log2(math.e)

LANES = 128
SUBLANES = 8
COND_ROWS = 8
DFT_INNER = 64
TOKEN_TILE = 512
ATT_TILE = 512
EXPERT_ROWS = 512
DISPATCH_TILE = 256
DISPATCH_SUBTILES = 2
RUN_ALIGN = 8
LOCAL_ROWS = -(-(2 * DISPATCH_TILE + N_EXPERTS * (RUN_ALIGN - 1)) // LANES) * LANES
CONV_HALO = 16
CONV_ROWS = 128
VMEM_LIMIT = 56 * 1024 * 1024


def _cparams(sem, vmem=VMEM_LIMIT):
    return pltpu.CompilerParams(dimension_semantics=sem, vmem_limit_bytes=vmem)


def _adaln_kernel(cond_ref, w_ref, b_ref, o_ref):
    s = cond_ref[...]
    s = s * jax.nn.sigmoid(s)
    o_ref[0] = jnp.dot(s, w_ref[0], precision=HIGHEST, preferred_element_type=f32) + b_ref[0]


def _adaln(cond, ada_w, ada_b):
    depth, d, n = ada_w.shape
    tn = 1536
    return pl.pallas_call(
        _adaln_kernel,
        grid=(depth, n // tn),
        in_specs=[pl.BlockSpec((COND_ROWS, d), lambda i, j: (0, 0)),
                  pl.BlockSpec((1, d, tn), lambda i, j: (i, 0, j)),
                  pl.BlockSpec((1, 1, tn), lambda i, j: (i, 0, j))],
        out_specs=pl.BlockSpec((1, COND_ROWS, tn), lambda i, j: (i, 0, j)),
        out_shape=jax.ShapeDtypeStruct((depth, COND_ROWS, n), f32),
        compiler_params=_cparams(("arbitrary", "arbitrary")),
        name="adaln",
    )(cond, ada_w, ada_b.reshape(depth, 1, n))


def _norm_mod(x, g, shift, scale):
    ms = jnp.mean(x * x, axis=-1, keepdims=True)
    return (x * lax.rsqrt(ms + EPS) * g) * (1.0 + scale) + shift


def _rope(p, cos, sin_signed, first_half):
    rot = jnp.where(first_half, pltpu.roll(p, LANES - 16, axis=1), pltpu.roll(p, 16, axis=1))
    return p * cos + rot * sin_signed


def _inproj_kernel(x_ref, mod_ref, g_ref, w_ref, cos_ref, sin_ref, q_ref, k_ref, v_ref):
    h = _norm_mod(x_ref[0], g_ref[...], mod_ref[0, 0:1, :], mod_ref[0, 1:2, :])
    p = jnp.dot(h.astype(bf16), w_ref[...], preferred_element_type=f32)
    cos = cos_ref[...]
    sin = sin_ref[...]
    lane = lax.broadcasted_iota(i32, cos.shape, 1)
    first_half = (lane % 32) < 16
    qw = N_HEADS * HEAD_DIM
    for c in range(qw // LANES):
        qc = p[:, c * LANES:(c + 1) * LANES]
        q_ref[0, :, c * LANES:(c + 1) * LANES] = (
            _rope(qc, cos, sin, first_half) * (LOG2E * HEAD_DIM ** -0.5)).astype(bf16)
    k_ref[0] = _rope(p[:, qw:qw + LANES], cos, sin, first_half).astype(bf16)
    v_ref[0] = p[:, qw + LANES:].astype(bf16)


def _inproj(x, mod, g, w, cos_t, sin_t):
    b, l, d = x.shape
    tm = TOKEN_TILE
    n = w.shape[1]
    return pl.pallas_call(
        _inproj_kernel,
        grid=(b, l // tm),
        in_specs=[pl.BlockSpec((1, tm, d), lambda i, j: (i, j, 0)),
                  pl.BlockSpec((1, 6, d), lambda i, j: (i, 0, 0)),
                  pl.BlockSpec((1, d), lambda i, j: (0, 0)),
                  pl.BlockSpec((d, n), lambda i, j: (0, 0)),
                  pl.BlockSpec((tm, LANES), lambda i, j: (j, 0)),
                  pl.BlockSpec((tm, LANES), lambda i, j: (j, 0))],
        out_specs=[pl.BlockSpec((1, tm, N_HEADS * HEAD_DIM), lambda i, j: (i, j, 0)),
                   pl.BlockSpec((1, tm, LANES), lambda i, j: (i, j, 0)),
                   pl.BlockSpec((1, tm, LANES), lambda i, j: (i, j, 0))],
        out_shape=[jax.ShapeDtypeStruct((b, l, N_HEADS * HEAD_DIM), bf16),
                   jax.ShapeDtypeStruct((b, l, LANES), bf16),
                   jax.ShapeDtypeStruct((b, l, LANES), bf16)],
        compiler_params=_cparams(("parallel", "parallel")),
        name="inproj",
    )(x, mod, g, w, cos_t, sin_t)


def _ctxkv_kernel(x_ref, mod_ref, g_ref, w_ref, k_ref, v_ref):
    h = _norm_mod(x_ref[0], g_ref[...], mod_ref[0, 0:1, :], mod_ref[0, 1:2, :])
    p = jnp.dot(h.astype(bf16), w_ref[...], preferred_element_type=f32)
    k_ref[0] = p[:, :LANES].astype(bf16)
    v_ref[0] = p[:, LANES:].astype(bf16)


def _ctxkv(ctx, mod, g, w_kv, ctx_row):
    b, c, d = ctx.shape
    return pl.pallas_call(
        _ctxkv_kernel,
        grid=(b,),
        in_specs=[pl.BlockSpec((1, c, d), lambda i: (i, 0, 0)),
                  pl.BlockSpec((1, 6, d), lambda i: (ctx_row, 0, 0)),
                  pl.BlockSpec((1, d), lambda i: (0, 0)),
                  pl.BlockSpec((d, 2 * LANES), lambda i: (0, 0))],
        out_specs=[pl.BlockSpec((1, c, LANES), lambda i: (i, 0, 0)),
                   pl.BlockSpec((1, c, LANES), lambda i: (i, 0, 0))],
        out_shape=[jax.ShapeDtypeStruct((b, c, LANES), bf16),
                   jax.ShapeDtypeStruct((b, c, LANES), bf16)],
        compiler_params=_cparams(("parallel",)),
        name="ctxkv",
    )(ctx, mod, g, w_kv)


def _attn_kernel(seq_len, q_ref, kp_ref, km_ref, kn_ref, vp_ref, vm_ref, vn_ref,
                 ck_ref, cv_ref, sink_ref, o_ref, kext, vext, cvext):
    j = pl.program_id(1)
    tq = ATT_TILE
    blk = ATT_BLOCK
    kext[0:blk] = kp_ref[0]
    kext[blk:blk + tq] = km_ref[0]
    kext[blk + tq:] = kn_ref[0]
    vext[:, LANES:] = jnp.ones((tq + 2 * blk, LANES), bf16)
    vext[0:blk, :LANES] = vp_ref[0]
    vext[blk:blk + tq, :LANES] = vm_ref[0]
    vext[blk + tq:, :LANES] = vn_ref[0]
    cvext[:, LANES:] = jnp.ones((cvext.shape[0], LANES), bf16)
    cvext[:, :LANES] = cv_ref[0]
    ck = ck_ref[0]
    sink = sink_ref[...]
    n_chunks = (N_HEADS * HEAD_DIM) // LANES
    rows = 2 * n_chunks * blk
    lane = lax.broadcasted_iota(i32, (blk, LANES), 1)
    low = lane < HEAD_DIM
    qi = lax.broadcasted_iota(i32, (rows, 3 * blk), 0) % blk
    pk = lax.broadcasted_iota(i32, (rows, 3 * blk), 1)
    band_bias = jnp.where(jnp.abs(pk - blk - qi) <= WINDOW, 0.0, NEG_INF).astype(f32)
    pcol = lax.broadcasted_iota(i32, (1, 3 * blk), 1)
    nt = (((1,), (1,)), ((), ()))

    def sub(s, carry):
        r0 = pl.multiple_of(s * blk, blk)
        qs = q_ref[0, pl.ds(r0, blk), :]
        parts = []
        for c in range(n_chunks):
            qc = qs[:, c * LANES:(c + 1) * LANES]
            parts.append(jnp.where(low, qc, jnp.zeros_like(qc)))
            parts.append(jnp.where(low, jnp.zeros_like(qc), qc))
        lhs = jnp.concatenate(parts, axis=0)
        kl = kext[pl.ds(r0, 3 * blk), :]
        vl = vext[pl.ds(r0, 3 * blk), :]
        kpos = j * tq + r0 - blk + pcol
        col_bias = jnp.where((kpos >= 0) & (kpos < seq_len), 0.0, NEG_INF).astype(f32)
        s_c = lax.dot_general(lhs, ck, nt, preferred_element_type=f32)
        s_l = lax.dot_general(lhs, kl, nt, preferred_element_type=f32) + band_bias + col_bias
        blocks = ([s_c[:, i:i + LANES] for i in range(0, s_c.shape[1], LANES)]
                  + [s_l[:, i:i + LANES] for i in range(0, s_l.shape[1], LANES)])
        folded = functools.reduce(jnp.maximum, blocks)
        m = jnp.maximum(jnp.max(folded, axis=1, keepdims=True), sink)
        e_c = jnp.exp2(s_c - m).astype(bf16)
        e_l = jnp.exp2(s_l - m).astype(bf16)
        ov = (jnp.dot(e_c, cvext[...], preferred_element_type=f32)
              + jnp.dot(e_l, vl, preferred_element_type=f32))
        den = ov[:, LANES:] + jnp.exp2(sink - m)
        o = ov[:, :LANES] / den
        for c in range(n_chunks):
            oc = jnp.where(low, o[(2 * c) * blk:(2 * c + 1) * blk],
                           o[(2 * c + 1) * blk:(2 * c + 2) * blk])
            o_ref[0, pl.ds(r0, blk), c * LANES:(c + 1) * LANES] = oc.astype(bf16)
        return carry

    lax.fori_loop(0, tq // blk, sub, 0, unroll=2)


def _attention(q, k, v, ck, cv, sinkcol):
    b, l, qw = q.shape
    c = ck.shape[1]
    tq = ATT_TILE
    r = tq // ATT_BLOCK
    nb = l // ATT_BLOCK
    prev = pl.BlockSpec((1, ATT_BLOCK, LANES), lambda i, j: (i, jnp.maximum(j * r - 1, 0), 0))
    main = pl.BlockSpec((1, tq, LANES), lambda i, j: (i, j, 0))
    nxt = pl.BlockSpec((1, ATT_BLOCK, LANES), lambda i, j: (i, jnp.minimum(j * r + r, nb - 1), 0))
    cspec = pl.BlockSpec((1, c, LANES), lambda i, j: (i, 0, 0))
    return pl.pallas_call(
        functools.partial(_attn_kernel, l),
        grid=(b, l // tq),
        in_specs=[pl.BlockSpec((1, tq, qw), lambda i, j: (i, j, 0)),
                  prev, main, nxt, prev, main, nxt, cspec, cspec,
                  pl.BlockSpec(sinkcol.shape, lambda i, j: (0, 0))],
        out_specs=pl.BlockSpec((1, tq, qw), lambda i, j: (i, j, 0)),
        out_shape=jax.ShapeDtypeStruct((b, l, qw), bf16),
        scratch_shapes=[pltpu.VMEM((tq + 2 * ATT_BLOCK, LANES), bf16),
                        pltpu.VMEM((tq + 2 * ATT_BLOCK, 2 * LANES), bf16),
                        pltpu.VMEM((c, 2 * LANES), bf16)],
        compiler_params=_cparams(("parallel", "parallel")),
        name="attention",
    )(q, k, k, k, v, v, v, ck, cv, sinkcol)


def _pack_pair(lo, hi):
    lo = lax.bitcast_convert_type(lo.astype(bf16).astype(f32), u32)
    hi = lax.bitcast_convert_type(hi.astype(bf16).astype(f32), u32)
    return (lo >> 16) | (hi & jnp.uint32(0xFFFF0000))


def _fourier1_kernel(x_ref, mod_ref, g_ref, w_ref, cs_ref, m_ref, ct_ref, st_ref, z_ref, ab_ref):
    n1 = x_ref.shape[1]
    nt = x_ref.shape[2]
    x = x_ref[0].reshape(n1 * nt, x_ref.shape[3])
    h = _norm_mod(x, g_ref[...], mod_ref[0, 0:1, :], mod_ref[0, 1:2, :]).astype(bf16)
    p = jnp.dot(h, w_ref[...], preferred_element_type=f32)
    for g in range(FOURIER_GROUPS):
        ug = p[:, g * LANES:(g + 1) * LANES].astype(bf16)
        ab = jnp.dot(ug, cs_ref[...], preferred_element_type=f32)
        ab_ref[0] = ab[:, :LANES]
        ab_ref[1] = ab[:, LANES:]
        for t in range(nt):
            stack = jnp.concatenate([ab_ref[0, pl.ds(t, n1, stride=nt), :],
                                     ab_ref[1, pl.ds(t, n1, stride=nt), :]], axis=0).astype(bf16)
            z = jnp.dot(m_ref[...], stack, preferred_element_type=f32)
            zr, zn = z[:n1], z[n1:]
            ct, st = ct_ref[t], st_ref[t]
            z_ref[0, g, t] = _pack_pair(ct * zr - st * zn, ct * zn + st * zr)


def _fourier2_kernel(scale, z_ref, m_ref, o_ref, zbuf, ybuf):
    _, grp, n2, tk, w = z_ref.shape
    for g in range(grp):
        zbuf[...] = z_ref[0, g].reshape(n2 * tk, w)
        for j in range(tk):
            zp = zbuf[pl.ds(j, n2, stride=tk), :]
            zr = lax.bitcast_convert_type(zp << 16, f32).astype(bf16)
            zn = lax.bitcast_convert_type(zp & jnp.uint32(0xFFFF0000), f32).astype(bf16)
            y = jnp.dot(m_ref[...], jnp.concatenate([zr, zn], axis=0), preferred_element_type=f32)
            ybuf[pl.ds(j, n2, stride=tk), :] = y * scale
        o_ref[0, g] = ybuf[...].reshape(n2, tk, w)


def _fourier(x, mod, g, w_f, cs):
    b, l, d = x.shape
    n2 = DFT_INNER
    n1 = l // n2
    grp, w = FOURIER_GROUPS, FOURIER_GROUP_W
    t2 = SUBLANES
    k1 = jnp.arange(n1, dtype=i32)
    ang1 = ((k1[:, None] * k1[None, :]) % n1).astype(f32) * (2.0 * math.pi / n1)
    c1, s1 = jnp.cos(ang1), jnp.sin(ang1)
    m1 = jnp.concatenate([jnp.concatenate([c1, -s1], axis=1),
                          jnp.concatenate([s1, c1], axis=1)], axis=0).astype(bf16)
    l2 = jnp.arange(n2, dtype=i32)
    angt = ((l2[:, None] * k1[None, :]) % l).astype(f32) * (2.0 * math.pi / l)
    ct = jnp.broadcast_to(jnp.cos(angt)[:, :, None], (n2, n1, w))
    st = jnp.broadcast_to(jnp.sin(angt)[:, :, None], (n2, n1, w))
    ang2 = ((l2[:, None] * l2[None, :]) % n2).astype(f32) * (2.0 * math.pi / n2)
    m2 = jnp.concatenate([jnp.cos(ang2), -jnp.sin(ang2)], axis=1).astype(bf16)

    tspec = pl.BlockSpec((t2, n1, w), lambda t, i: (t, 0, 0))
    z = pl.pallas_call(
        _fourier1_kernel,
        grid=(n2 // t2, b),
        in_specs=[pl.BlockSpec((1, n1, t2, d), lambda t, i: (i, 0, t, 0)),
                  pl.BlockSpec((1, 6, d), lambda t, i: (i, 0, 0)),
                  pl.BlockSpec((1, d), lambda t, i: (0, 0)),
                  pl.BlockSpec(w_f.shape, lambda t, i: (0, 0)),
                  pl.BlockSpec(cs.shape, lambda t, i: (0, 0)),
                  pl.BlockSpec(m1.shape, lambda t, i: (0, 0)), tspec, tspec],
        out_specs=pl.BlockSpec((1, grp, t2, n1, w), lambda t, i: (i, 0, t, 0, 0)),
        out_shape=jax.ShapeDtypeStruct((b, grp, n2, n1, w), u32),
        scratch_shapes=[pltpu.VMEM((2, n1 * t2, w), f32)],
        compiler_params=_cparams(("parallel", "parallel")),
        name="fourier_outer",
    )(x.reshape(b, n1, n2, d), mod, g, w_f, cs, m1, ct, st)

    tk = SUBLANES
    y = pl.pallas_call(
        functools.partial(_fourier2_kernel, 1.0 / math.sqrt(l * w)),
        grid=(b, n1 // tk),
        in_specs=[pl.BlockSpec((1, grp, n2, tk, w), lambda i, t: (i, 0, 0, t, 0)),
                  pl.BlockSpec(m2.shape, lambda i, t: (0, 0))],
        out_specs=pl.BlockSpec((1, grp, n2, tk, w), lambda i, t: (i, 0, 0, t, 0)),
        out_shape=jax.ShapeDtypeStruct((b, grp, n2, n1, w), f32),
        scratch_shapes=[pltpu.VMEM((n2 * tk, w), u32), pltpu.VMEM((n2 * tk, w), f32)],
        compiler_params=_cparams(("parallel", "parallel")),
        name="fourier_inner",
    )(z, m2)
    return y.reshape(b, grp, l, w)


def _first_max4(a):
    m = jnp.maximum(jnp.maximum(a[0], a[1]), jnp.maximum(a[2], a[3]))
    idx = jnp.where(a[0] == m, 0, jnp.where(a[1] == m, 1, jnp.where(a[2] == m, 2, 3)))
    return m, idx


def _pick4(vals, idx):
    return jnp.where(idx == 0, vals[0], jnp.where(idx == 1, vals[1],
                                                   jnp.where(idx == 2, vals[2], vals[3])))


def _route(f, rw_ref, rb_ref, tri_ref, base_ref, first_step, ri_ref, wc_ref, cnt_ref, meta_ref):
    tm = f.shape[0]
    f_hi = f.astype(bf16)
    f_lo = (f - f_hi.astype(f32)).astype(bf16)
    rw2 = rw_ref[...]
    part = jnp.dot(f_hi, rw2, preferred_element_type=f32)
    logits = (part[:, :LANES] + part[:, LANES:]
              + jnp.dot(f_lo, rw2[:, :LANES], preferred_element_type=f32))
    sc = jax.nn.sigmoid(logits)
    st = sc.T
    bt = (sc + rb_ref[...]).T
    neg = jnp.full((1, tm), -jnp.inf, f32)
    gs = []
    for g in range(N_GROUPS):
        a = [bt[4 * g + i: 4 * g + i + 1] for i in range(4)]
        m1, i1 = _first_max4(a)
        rest = [jnp.where(i1 == i, neg, a[i]) for i in range(4)]
        m2, _ = _first_max4(rest)
        gs.append(m1 + m2)
    _, gsel = _first_max4(gs)
    a = [_pick4([bt[4 * g + i: 4 * g + i + 1] for g in range(N_GROUPS)], gsel) for i in range(4)]
    s = [_pick4([st[4 * g + i: 4 * g + i + 1] for g in range(N_GROUPS)], gsel) for i in range(4)]
    _, i1 = _first_max4(a)
    rest = [jnp.where(i1 == i, neg, a[i]) for i in range(4)]
    _, i2 = _first_max4(rest)
    w1 = _pick4(s, i1)
    w2 = _pick4(s, i2)
    tot = w1 + w2
    w1 = w1 / tot
    w2 = w2 / tot
    e0 = gsel * EXPERTS_PER_GROUP + i1
    e1 = gsel * EXPERTS_PER_GROUP + i2

    @pl.when(first_step)
    def _():
        base_ref[...] = jnp.zeros_like(base_ref)

    td = DISPATCH_TILE
    eid = lax.broadcasted_iota(i32, (N_EXPERTS, tm), 0)
    oh0 = (eid == e0).astype(f32)
    oh1 = (eid == e1).astype(f32)
    oh = oh0 + oh1
    before = jnp.dot(oh.astype(bf16), tri_ref[...], preferred_element_type=f32)
    lane_tile = lax.broadcasted_iota(i32, (N_EXPERTS, tm), 1) // td
    ei = lax.broadcasted_iota(i32, (N_EXPERTS, N_EXPERTS), 0)
    ej = lax.broadcasted_iota(i32, (N_EXPERTS, N_EXPERTS), 1)
    strict_lower = (ej < ei).astype(f32)
    run_start = jnp.zeros((N_EXPERTS, tm), f32)
    goff = base_ref[...]
    for s in range(tm // td):
        cnt_s = jnp.sum(oh[:, s * td:(s + 1) * td], axis=1, keepdims=True)
        pad_s = jnp.floor((cnt_s + 7.0) * 0.125) * 8.0
        pad_b = jnp.broadcast_to(pad_s, (N_EXPERTS, LANES))
        start_b = jnp.dot(strict_lower, pad_b, precision=HIGHEST, preferred_element_type=f32)
        run_start = jnp.where(lane_tile == s, start_b[:, 0:1], run_start)
        meta_ref[s, 0] = start_b.astype(i32)
        meta_ref[s, 1] = pad_b.astype(i32)
        meta_ref[s, 2] = goff.astype(i32)
        goff = goff + pad_b
    base_ref[...] = goff
    cnt_ref[...] = goff
    pos = before + run_start
    lp0 = jnp.sum(oh0 * pos, axis=0, keepdims=True)
    lp1 = jnp.sum(oh1 * pos, axis=0, keepdims=True)
    zi = jnp.zeros((1, tm), i32)
    ri_ref[...] = jnp.concatenate(
        [lp0.astype(i32), lp1.astype(i32), e0, e1, zi, zi, zi, zi], axis=0)
    zf = jnp.zeros((LANES - 4, tm), f32)
    wc_ref[...] = jnp.concatenate([w1, w2, lp0, lp1, zf], axis=0).T


def _outproj_kernel(yf_ref, o_ref, x_ref, mod_ref, w_ref, g_ref, rw_ref, rb_ref, tri_ref,
                    x1_ref, f_ref, ri_ref, wc_ref, cnt_ref, meta_ref, base_ref):
    mix = jnp.concatenate([yf_ref[0, g].astype(bf16) for g in range(FOURIER_GROUPS)] + [o_ref[0]],
                          axis=1)
    y = jnp.dot(mix, w_ref[...], preferred_element_type=f32)
    x1 = x_ref[0] + mod_ref[0, 2:3, :] * y
    x1_ref[0] = x1
    f = _norm_mod(x1, g_ref[...], mod_ref[0, 3:4, :], mod_ref[0, 4:5, :])
    f_ref[0] = f.astype(bf16)
    first = (pl.program_id(0) == 0) & (pl.program_id(1) == 0)
    _route(f, rw_ref, rb_ref, tri_ref, base_ref, first, ri_ref, wc_ref, cnt_ref, meta_ref)


def _route_specs(b, l, tm):
    nl = l // tm
    rw = lambda d: pl.BlockSpec((d, 2 * LANES), lambda i, j: (0, 0))
    rb = pl.BlockSpec((1, LANES), lambda i, j: (0, 0))
    tri = pl.BlockSpec((tm, tm), lambda i, j: (0, 0))
    ns = tm // DISPATCH_TILE
    out_specs = [pl.BlockSpec((8, tm), lambda i, j: (0, i * nl + j)),
                 pl.BlockSpec((tm, LANES), lambda i, j: (i * nl + j, 0)),
                 pl.BlockSpec((N_EXPERTS, LANES), lambda i, j: (0, 0)),
                 pl.BlockSpec((ns, 3, N_EXPERTS, LANES), lambda i, j: (i * nl + j, 0, 0, 0))]
    out_shape = [jax.ShapeDtypeStruct((8, b * l), i32),
                 jax.ShapeDtypeStruct((b * l, LANES), f32),
                 jax.ShapeDtypeStruct((N_EXPERTS, LANES), f32),
                 jax.ShapeDtypeStruct((b * l // DISPATCH_TILE, 3, N_EXPERTS, LANES), i32)]
    return rw, rb, tri, out_specs, out_shape


def _outproj(yf, o, x, mod, w, g, rw, rb, tri):
    b, l, d = x.shape
    tm = TOKEN_TILE
    rws, rbs, tris, r_specs, r_shapes = _route_specs(b, l, tm)
    row = pl.BlockSpec((1, tm, d), lambda i, j: (i, j, 0))
    return pl.pallas_call(
        _outproj_kernel,
        grid=(b, l // tm),
        in_specs=[pl.BlockSpec((1, FOURIER_GROUPS, tm, LANES), lambda i, j: (i, 0, j, 0)),
                  pl.BlockSpec((1, tm, o.shape[2]), lambda i, j: (i, j, 0)),
                  row,
                  pl.BlockSpec((1, 6, d), lambda i, j: (i, 0, 0)),
                  pl.BlockSpec(w.shape, lambda i, j: (0, 0)),
                  pl.BlockSpec((1, d), lambda i, j: (0, 0)),
                  rws(d), rbs, tris],
        out_specs=[row, row] + r_specs,
        out_shape=[jax.ShapeDtypeStruct((b, l, d), f32),
                   jax.ShapeDtypeStruct((b, l, d), bf16)] + r_shapes,
        scratch_shapes=[pltpu.VMEM((N_EXPERTS, LANES), f32)],
        compiler_params=_cparams(("arbitrary", "arbitrary")),
        name="outproj_router",
    )(yf, o, x, mod, w, g, rw, rb, tri)


def _conv_kernel(seq_len, up_ref, um_ref, un_ref, x_ref, mod_ref, dw_ref, db_ref, lg_ref, lb_ref,
                 w_ref, pb_ref, g_ref, rw_ref, rb_ref, tri_ref,
                 x1_ref, f_ref, ri_ref, wc_ref, cnt_ref, meta_ref, base_ref, ext, conv_out):
    j = pl.program_id(1)
    tm = um_ref.shape[1]
    hl = CONV_HALO
    half = CONV_W // 2
    ext[0:hl] = jnp.where(j > 0, up_ref[0], jnp.zeros_like(up_ref[0]))
    ext[hl:hl + tm] = um_ref[0]
    ext[hl + tm:] = jnp.where((j + 1) * tm < seq_len, un_ref[0], jnp.zeros_like(un_ref[0]))
    base = hl - half
    span = (CONV_W - 1) // SUBLANES * SUBLANES
    rows = CONV_ROWS

    def lane_chunk(c, carry):
        lanes = pl.ds(pl.multiple_of(c * LANES, LANES), LANES)
        for r in range(0, tm, rows):
            part = jnp.broadcast_to(db_ref[:, lanes], (rows, LANES))
            for phase in range(SUBLANES):
                win = ext[base + phase + r: base + phase + r + rows + span, lanes]
                same = None
                for t in range(phase, CONV_W, SUBLANES):
                    term = win[t - phase: t - phase + rows, :] * dw_ref[t:t + 1, lanes]
                    same = term if same is None else same + term
                part = part + same
            conv_out[r:r + rows, lanes] = part
        return carry

    lax.fori_loop(0, um_ref.shape[2] // LANES, lane_chunk, 0)
    acc = conv_out[...]
    mu = jnp.mean(acc, axis=-1, keepdims=True)
    cen = acc - mu
    var = jnp.mean(cen * cen, axis=-1, keepdims=True)
    ln = cen * lax.rsqrt(var + EPS) * lg_ref[...] + lb_ref[...]
    act = ln * jax.nn.sigmoid(ln)
    y = jnp.dot(act.astype(bf16), w_ref[...], preferred_element_type=f32) + pb_ref[...]
    x1 = x_ref[0] + mod_ref[0, 2:3, :] * y
    x1_ref[0] = x1
    f = _norm_mod(x1, g_ref[...], mod_ref[0, 3:4, :], mod_ref[0, 4:5, :])
    f_ref[0] = f.astype(bf16)
    first = (pl.program_id(0) == 0) & (j == 0)
    _route(f, rw_ref, rb_ref, tri_ref, base_ref, first, ri_ref, wc_ref, cnt_ref, meta_ref)


def _conv(u, x, mod, dw_w, dw_b, ln_g, ln_b, pw2_w, pw2_b, g, rw, rb, tri):
    b, l, d = x.shape
    tm = TOKEN_TILE
    hl = CONV_HALO
    r = tm // hl
    nh = l // hl
    rws, rbs, tris, r_specs, r_shapes = _route_specs(b, l, tm)
    row = pl.BlockSpec((1, tm, d), lambda i, j: (i, j, 0))
    vec = pl.BlockSpec((1, d), lambda i, j: (0, 0))
    return pl.pallas_call(
        functools.partial(_conv_kernel, l),
        grid=(b, l // tm),
        in_specs=[pl.BlockSpec((1, hl, d), lambda i, j: (i, jnp.maximum(j * r - 1, 0), 0)),
                  row,
                  pl.BlockSpec((1, hl, d), lambda i, j: (i, jnp.minimum(j * r + r, nh - 1), 0)),
                  row,
                  pl.BlockSpec((1, 6, d), lambda i, j: (i, 0, 0)),
                  pl.BlockSpec(dw_w.shape, lambda i, j: (0, 0)),
                  vec, vec, vec,
                  pl.BlockSpec(pw2_w.shape, lambda i, j: (0, 0)),
                  vec, vec, rws(d), rbs, tris],
        out_specs=[row, row] + r_specs,
        out_shape=[jax.ShapeDtypeStruct((b, l, d), f32),
                   jax.ShapeDtypeStruct((b, l, d), bf16)] + r_shapes,
        scratch_shapes=[pltpu.VMEM((N_EXPERTS, LANES), f32),
                        pltpu.VMEM((tm + 2 * hl, d), f32),
                        pltpu.VMEM((tm, d), f32)],
        compiler_params=_cparams(("arbitrary", "arbitrary")),
        name="conv_router",
    )(u, u, u, x, mod, dw_w, dw_b, ln_g, ln_b, pw2_w, pw2_b, g, rw, rb, tri)


def _pack_bf16_pairs(x):
    h = x.shape[1] // 2
    lo = lax.bitcast_convert_type(x[:, :h], u32)
    hi = lax.bitcast_convert_type(x[:, h:], u32)
    return (lo >> 16) | (hi & jnp.uint32(0xFFFF0000))


def _unpack_bf16_pairs(u):
    lo = lax.bitcast_convert_type(u << 16, f32)
    hi = lax.bitcast_convert_type(u & jnp.uint32(0xFFFF0000), f32)
    return jnp.concatenate([lo, hi], axis=1).astype(bf16)


def _run_copies(meta, tile, local_ref, hbm_ref, sem, to_hbm):
    start_ref, size_ref, dst_ref = meta
    for e in range(N_EXPERTS):
        k = tile * N_EXPERTS + e
        size = pl.multiple_of(size_ref[k], RUN_ALIGN)

        @pl.when(size > 0)
        def _():
            loc = local_ref.at[pl.ds(pl.multiple_of(start_ref[k], RUN_ALIGN), size)]
            glob = hbm_ref.at[pl.ds(pl.multiple_of(dst_ref[k], RUN_ALIGN), size)]
            if to_hbm:
                pltpu.make_async_copy(loc, glob, sem).start()
            else:
                pltpu.make_async_copy(glob, loc, sem).start()


def _wait_rows(rows, local_ref, hbm_ref, sem):
    rows = pl.multiple_of(rows, RUN_ALIGN)

    @pl.when(rows > 0)
    def _():
        pltpu.make_async_copy(local_ref.at[pl.ds(0, rows)], hbm_ref.at[pl.ds(0, rows)], sem).wait()


def _dispatch_kernel(start_ref, size_ref, dst_ref, tot_ref, tail_start_ref, tail_size_ref, nv_ref,
                     f_ref, lp_ref, xs_ref, loc, zbuf, sem, zsem):
    i = pl.program_id(0)
    n = pl.num_programs(0)
    slot = i % 2
    subs = loc.shape[1]
    rows = loc.shape[2]
    td = f_ref.shape[0] // subs
    meta = (start_ref, size_ref, dst_ref)

    def drain(step, which):
        for s in range(subs):
            _wait_rows(tot_ref[step * subs + s], loc.at[which, s], xs_ref, sem.at[which])

    @pl.when(i >= 2)
    def _():
        drain(i - 2, slot)

    r = lax.broadcasted_iota(i32, (rows, td), 0)
    for s in range(subs):
        cols = slice(s * td, (s + 1) * td)
        onehot = ((r == lp_ref[0:1, cols]) | (r == lp_ref[1:2, cols])).astype(bf16)
        sorted_rows = jnp.dot(onehot, f_ref[cols, :], preferred_element_type=f32)
        loc[slot, s] = _pack_bf16_pairs(sorted_rows)
        _run_copies(meta, i * subs + s, loc.at[slot, s], xs_ref, sem.at[slot], to_hbm=True)

    @pl.when(i == n - 1)
    def _():
        zbuf[...] = jnp.zeros_like(zbuf)
        total = 0
        for e in range(N_EXPERTS):
            size = pl.multiple_of(tail_size_ref[e], RUN_ALIGN)
            total = total + size

            @pl.when(size > 0)
            def _():
                pltpu.make_async_copy(
                    zbuf.at[pl.ds(0, size)],
                    xs_ref.at[pl.ds(pl.multiple_of(tail_start_ref[e], RUN_ALIGN), size)], zsem).start()

        _wait_rows(total, zbuf, xs_ref, zsem)

        def zero_block(k, c):
            pltpu.make_async_copy(zbuf, xs_ref.at[pl.ds(pl.multiple_of(k * zbuf.shape[0], RUN_ALIGN),
                                                        zbuf.shape[0])], zsem).start()
            return c

        def wait_block(k, c):
            pltpu.make_async_copy(zbuf, xs_ref.at[pl.ds(0, zbuf.shape[0])], zsem).wait()
            return c

        n_blocks = xs_ref.shape[0] // zbuf.shape[0]
        lax.fori_loop(nv_ref[0], n_blocks, zero_block, 0)
        lax.fori_loop(nv_ref[0], n_blocks, wait_block, 0)
        drain(i, slot)

        @pl.when(i >= 1)
        def _():
            drain(i - 1, 1 - slot)


def _dispatch(tables, f2, ri, n_slots):
    t, d = f2.shape
    subs = DISPATCH_SUBTILES
    tm = DISPATCH_TILE * subs
    return pl.pallas_call(
        _dispatch_kernel,
        grid_spec=pltpu.PrefetchScalarGridSpec(
            num_scalar_prefetch=7,
            grid=(t // tm,),
            in_specs=[pl.BlockSpec((tm, d), lambda i, *_: (i, 0)),
                      pl.BlockSpec((8, tm), lambda i, *_: (0, i))],
            out_specs=pl.BlockSpec(memory_space=pl.ANY),
            scratch_shapes=[pltpu.VMEM((2, subs, LOCAL_ROWS, d // 2), u32),
                            pltpu.VMEM((EXPERT_ROWS, d // 2), u32),
                            pltpu.SemaphoreType.DMA((2,)), pltpu.SemaphoreType.DMA(())]),
        out_shape=jax.ShapeDtypeStruct((n_slots, d // 2), u32),
        compiler_params=_cparams(("arbitrary",)),
        name="moe_dispatch",
    )(*tables, f2, ri)


def _expert_kernel(be_ref, nv_ref, x_ref, wg_ref, wu_ref, wd_ref, y_ref, wgb, wub, wdb):
    i = pl.program_id(0)
    changed = jnp.logical_or(i == 0, be_ref[i] != be_ref[jnp.maximum(i - 1, 0)])

    @pl.when(changed)
    def _():
        wgb[...] = wg_ref[0, 0].astype(bf16)
        wub[...] = wu_ref[0, 0].astype(bf16)
        wdb[...] = wd_ref[0, 0].astype(bf16)

    @pl.when(i < nv_ref[0])
    def _():
        xb = _unpack_bf16_pairs(x_ref[...])
        gate = jnp.dot(xb, wgb[...], preferred_element_type=f32)
        up = jnp.dot(xb, wub[...], preferred_element_type=f32)
        hid = (gate * jax.nn.sigmoid(gate) * up).astype(bf16)
        y = jnp.dot(hid, wdb[...], preferred_element_type=f32)
        y_ref[...] = _pack_bf16_pairs(y.astype(bf16).astype(f32))

    @pl.when(i >= nv_ref[0])
    def _():
        y_ref[...] = jnp.zeros_like(y_ref)


def _experts(block_e, n_valid, xs, w_gate, w_up, w_down, layer):
    ns, dh = xs.shape
    tb = EXPERT_ROWS
    d, ff = w_gate.shape[2:]
    xmap = lambda i, be, nv: (jnp.maximum(jnp.minimum(i, nv[0] - 1), 0), 0)
    wmap = lambda i, be, nv: (layer, be[i], 0, 0)
    return pl.pallas_call(
        _expert_kernel,
        grid_spec=pltpu.PrefetchScalarGridSpec(
            num_scalar_prefetch=2,
            grid=(ns // tb,),
            in_specs=[pl.BlockSpec((tb, dh), xmap),
                      pl.BlockSpec((1, 1, d, ff), wmap),
                      pl.BlockSpec((1, 1, d, ff), wmap),
                      pl.BlockSpec((1, 1, ff, d), wmap)],
            out_specs=pl.BlockSpec((tb, dh), lambda i, be, nv: (i, 0)),
            scratch_shapes=[pltpu.VMEM((d, ff), bf16), pltpu.VMEM((d, ff), bf16),
                            pltpu.VMEM((ff, d), bf16)]),
        out_shape=jax.ShapeDtypeStruct((ns, dh), u32),
        compiler_params=_cparams(("arbitrary",)),
        name="moe_experts",
    )(block_e, n_valid, xs, w_gate, w_up, w_down)


def _combine_kernel(final, start_ref, size_ref, dst_ref, tot_ref, ys_ref, wc_ref, x_ref, mod_ref,
                    g_ref, *rest):
    if final:
        o_ref, loc, sem = rest
    else:
        nmod_ref, w_ref, b_ref, o_ref, u_ref, loc, sem = rest
    i = pl.program_id(0)
    n = pl.num_programs(0)
    slot = i % 2
    subs = loc.shape[1]
    rows = loc.shape[2]
    td = x_ref.shape[0] // subs
    meta = (start_ref, size_ref, dst_ref)

    def fetch(step, which):
        for s in range(subs):
            _run_copies(meta, step * subs + s, loc.at[which, s], ys_ref, sem.at[which, s],
                        to_hbm=False)

    @pl.when(i == 0)
    def _():
        loc[...] = jnp.zeros_like(loc)
        fetch(i, slot)

    @pl.when(i + 1 < n)
    def _():
        fetch(i + 1, 1 - slot)

    c = lax.broadcasted_iota(i32, (td, rows), 1)
    parts = []
    for s in range(subs):
        _wait_rows(tot_ref[i * subs + s], loc.at[slot, s], ys_ref, sem.at[slot, s])
        wc = wc_ref[s * td:(s + 1) * td, :]
        sel = jnp.concatenate([(c == wc[:, 2:3].astype(i32)).astype(bf16),
                               (c == wc[:, 3:4].astype(i32)).astype(bf16)], axis=0)
        picked = jnp.dot(sel, _unpack_bf16_pairs(loc[slot, s]), preferred_element_type=f32)
        parts.append(wc[:, 0:1] * picked[:td] + wc[:, 1:2] * picked[td:])
    xo = x_ref[...] + mod_ref[0, 5:6, :] * jnp.concatenate(parts, axis=0)
    if final:
        ms = jnp.mean(xo * xo, axis=-1, keepdims=True)
        o_ref[...] = xo * lax.rsqrt(ms + EPS) * g_ref[...]
    else:
        o_ref[...] = xo
        h = _norm_mod(xo, g_ref[...], nmod_ref[0, 0:1, :], nmod_ref[0, 1:2, :])
        p = jnp.dot(h.astype(bf16), w_ref[...], preferred_element_type=f32) + b_ref[...]
        ch = p.shape[1] // 2
        u_ref[...] = p[:, :ch] * jax.nn.sigmoid(p[:, ch:])


def _combine(tables, ys, wc, x, mod, g, glu=None):
    b, l, d = x.shape
    subs = DISPATCH_SUBTILES
    tm = DISPATCH_TILE * subs
    per_batch = l // tm
    const = lambda i, *_: (0, 0)
    tile = lambda cols: pl.BlockSpec((tm, cols), lambda i, *_: (i, 0))
    mod_spec = pl.BlockSpec((1, 6, d), lambda i, *_: (i // per_batch, 0, 0))
    in_specs = [pl.BlockSpec(memory_space=pl.ANY), tile(LANES), tile(d), mod_spec,
                pl.BlockSpec((1, d), const)]
    args = [ys, wc, x.reshape(b * l, d), mod, g]
    out_specs = [tile(d)]
    out_shape = [jax.ShapeDtypeStruct((b * l, d), f32)]
    if glu is not None:
        nmod, w, bias = glu
        in_specs += [mod_spec, pl.BlockSpec(w.shape, const), pl.BlockSpec(bias.shape, const)]
        args += [nmod, w, bias]
        out_specs.append(tile(w.shape[1] // 2))
        out_shape.append(jax.ShapeDtypeStruct((b * l, w.shape[1] // 2), f32))
    outs = pl.pallas_call(
        functools.partial(_combine_kernel, glu is None),
        grid_spec=pltpu.PrefetchScalarGridSpec(
            num_scalar_prefetch=4,
            grid=(b * per_batch,),
            in_specs=in_specs,
            out_specs=out_specs,
            scratch_shapes=[pltpu.VMEM((2, subs, LOCAL_ROWS, d // 2), u32),
                            pltpu.SemaphoreType.DMA((2, subs))]),
        out_shape=out_shape,
        compiler_params=_cparams(("arbitrary",)),
        name="moe_combine",
    )(*tables, *args)
    return [o.reshape(b, l, -1) for o in outs]


def _moe(f, routed, x, mod, g, w_gate, w_up, w_down, layer, glu=None):
    ri, wc, cnt, meta = routed
    b, l, d = x.shape
    t = b * l
    tb = EXPERT_ROWS
    n_tiles = t // DISPATCH_TILE
    used = cnt[:, 0].astype(i32)
    region = (used + tb - 1) // tb * tb
    gend = jnp.cumsum(region)
    gstart = gend - region
    max_rows = 2 * t + n_tiles * N_EXPERTS * (RUN_ALIGN - 1) + N_EXPERTS * (tb - 1)
    n_blocks = -(-max_rows // tb)
    m = meta[:, :, :, 0]
    run_start = m[:, 0].reshape(-1)
    run_size = m[:, 1].reshape(-1)
    run_dst = (m[:, 2] + gstart[None, :]).reshape(-1)
    tile_rows = jnp.sum(m[:, 1], axis=1)
    block_row = jnp.arange(n_blocks, dtype=i32) * tb
    block_e = jnp.minimum(jnp.sum((block_row[:, None] >= gend[None, :]).astype(i32), axis=1),
                          N_EXPERTS - 1)
    n_valid = (gend[-1] // tb).reshape(1)
    xs = _dispatch((run_start, run_size, run_dst, tile_rows, gstart + used, region - used, n_valid),
                   f.reshape(t, d), ri, n_blocks * tb)
    ys = _experts(block_e, n_valid, xs, w_gate, w_up, w_down, layer)
    return _combine((run_start, run_size, run_dst, tile_rows), ys, wc, x, mod, g, glu)


def _rope_tables(l):
    lane = jnp.arange(LANES)
    dh = lane % HEAD_DIM
    inv = ROPE_THETA ** (-(dh % 16).astype(f32) / 16.0)
    pos = jnp.arange(l)
    row = (pos // GRID_W).astype(f32)
    col = (pos % GRID_W).astype(f32)
    p = jnp.where((dh // 32)[None, :] == 0, row[:, None], col[:, None])
    ang = p * inv[None, :]
    sign = jnp.where((dh % 32) < 16, -1.0, 1.0).astype(f32)
    return jnp.cos(ang), jnp.sin(ang) * sign[None, :]


def kernel(x, c, ctx, c_ctx, ada_w, ada_b, norm_mix_g, norm_ffn_g, even_w_in, even_w_out, even_sink, conv_pw1_w, conv_pw1_b, conv_dw_w, conv_dw_b, conv_ln_g, conv_ln_b, conv_pw2_w, conv_pw2_b, router_w, router_b, moe_w_gate, moe_w_up, moe_w_down, final_norm_g):
    b, l, d = x.shape
    depth = ada_w.shape[0]
    assert depth == 2 and b < COND_ROWS
    ctx_row = b
    cond = jnp.zeros((COND_ROWS, d), f32).at[:b].set(c).at[ctx_row].set(c_ctx)
    mods = _adaln(cond, ada_w, ada_b).reshape(depth, COND_ROWS, 6, d)

    heads = jnp.arange(N_HEADS).reshape(N_KV_HEADS, N_HEADS // N_KV_HEADS).T.reshape(-1)
    qperm = (heads[:, None] * HEAD_DIM + jnp.arange(HEAD_DIM)[None, :]).reshape(-1)
    fw = FOURIER_GROUPS * FOURIER_GROUP_W
    qw = N_HEADS * HEAD_DIM
    w_in = even_w_in[0]
    w_in_p = jnp.concatenate([w_in[:, :fw], w_in[:, fw:fw + qw][:, qperm], w_in[:, fw + qw:]],
                             axis=1).astype(bf16)
    w_out = even_w_out[0]
    w_out_p = jnp.concatenate([w_out[:fw], w_out[fw:][qperm]], axis=0).astype(bf16)
    sinkcol = jnp.repeat(even_sink[0][heads].astype(f32) * LOG2E, ATT_BLOCK).reshape(-1, 1)

    cidx = jnp.arange(FOURIER_GROUP_W, dtype=i32)
    angc = ((cidx[:, None] * cidx[None, :]) % FOURIER_GROUP_W).astype(f32) * (2.0 * math.pi / FOURIER_GROUP_W)
    cs = jnp.concatenate([jnp.cos(angc), jnp.sin(angc)], axis=1).astype(bf16)
    cos_t, sin_t = _rope_tables(l)

    rw32 = jnp.zeros((d, LANES), f32).at[:, :N_EXPERTS].set(router_w.astype(f32))
    rw_hi = rw32.astype(bf16)
    rw = jnp.concatenate([rw_hi, (rw32 - rw_hi.astype(f32)).astype(bf16)], axis=1)
    rb = jnp.zeros((1, LANES), f32).at[0, :N_EXPERTS].set(router_b)
    tpos = jnp.arange(TOKEN_TILE)
    tri = ((tpos[:, None] < tpos[None, :])
           & (tpos[:, None] // DISPATCH_TILE == tpos[None, :] // DISPATCH_TILE)).astype(bf16)
    row = lambda v: v.reshape(1, -1)

    q, k, v = _inproj(x, mods[0], row(norm_mix_g[0]), w_in_p[:, fw:], cos_t, sin_t)
    ck, cv = _ctxkv(ctx, mods[0], row(norm_mix_g[0]), w_in_p[:, fw + qw:], ctx_row)
    yf = _fourier(x, mods[0], row(norm_mix_g[0]), w_in_p[:, :fw], cs)
    att = _attention(q, k, v, ck, cv, sinkcol)
    x1, f, *routed = _outproj(yf, att, x, mods[0], w_out_p, row(norm_ffn_g[0]), rw, rb, tri)
    x2, u = _moe(f, routed, x1, mods[0], row(norm_mix_g[1]), moe_w_gate, moe_w_up, moe_w_down,
                 layer=0, glu=(mods[1], conv_pw1_w[0].astype(bf16), row(conv_pw1_b[0])))

    x3, f, *routed = _conv(u, x2, mods[1], conv_dw_w[0], row(conv_dw_b[0]), row(conv_ln_g[0]),
                           row(conv_ln_b[0]), conv_pw2_w[0].astype(bf16), row(conv_pw2_b[0]),
                           row(norm_ffn_g[1]), rw, rb, tri)
    (out,) = _moe(f, routed, x3, mods[1], row(final_norm_g), moe_w_gate, moe_w_up, moe_w_down,
                  layer=1)
    return out
```

```python
import functools
import math

import jax
import jax.numpy as jnp
from jax import lax
from jax.experimental import pallas as pl
from jax.experimental.pallas import tpu as pltpu

f32 = jnp.float32
bf16 = jnp.bfloat16
i32 = jnp.int32
u32 = jnp.uint32
HIGHEST = lax.Precision.HIGHEST

GRID_W = 64
HEAD_DIM = 64
N_HEADS = 8
N_KV_HEADS = 2
WINDOW = 128
ATT_BLOCK = 128
ROPE_THETA = 10000.0
FOURIER_GROUPS = 4
FOURIER_GROUP_W = 128
CONV_W = 31
N_EXPERTS = 16
N_GROUPS = 4
EXPERTS_PER_GROUP = 4
EXPERT_FF = 512
EPS = 1e-6
NEG_INF = -1e30
LOG2E = math.log2(math.e)

LANES = 128
SUBLANES = 8
COND_ROWS = 8
DFT_INNER = 64
TOKEN_TILE = 512
ATT_TILE = 1024
EXPERT_ROWS = 512
DISPATCH_TILE = 256
DISPATCH_SUBTILES = 4
GLU_SUBTILES = 2
RUN_ALIGN = 8
LOCAL_ROWS = -(-(2 * DISPATCH_TILE + N_EXPERTS * (RUN_ALIGN - 1)) // LANES) * LANES
CONV_HALO = 16
CONV_ROWS = 128
VMEM_LIMIT = 56 * 1024 * 1024


def _cparams(sem, vmem=VMEM_LIMIT):
    return pltpu.CompilerParams(dimension_semantics=sem, vmem_limit_bytes=vmem)


def _adaln_kernel(cond_ref, w_ref, b_ref, o_ref):
    s = cond_ref[...]
    s = s * jax.nn.sigmoid(s)
    o_ref[0] = jnp.dot(s, w_ref[0], precision=HIGHEST, preferred_element_type=f32) + b_ref[0]


def _adaln(cond, ada_w, ada_b):
    depth, d, n = ada_w.shape
    tn = 1536
    return pl.pallas_call(
        _adaln_kernel,
        grid=(depth, n // tn),
        in_specs=[pl.BlockSpec((COND_ROWS, d), lambda i, j: (0, 0)),
                  pl.BlockSpec((1, d, tn), lambda i, j: (i, 0, j)),
                  pl.BlockSpec((1, 1, tn), lambda i, j: (i, 0, j))],
        out_specs=pl.BlockSpec((1, COND_ROWS, tn), lambda i, j: (i, 0, j)),
        out_shape=jax.ShapeDtypeStruct((depth, COND_ROWS, n), f32),
        compiler_params=_cparams(("arbitrary", "arbitrary")),
        name="adaln",
    )(cond, ada_w, ada_b.reshape(depth, 1, n))


def _norm_mod(x, g, shift, scale):
    ms = jnp.mean(x * x, axis=-1, keepdims=True)
    return (x * lax.rsqrt(ms + EPS) * g) * (1.0 + scale) + shift


def _rope(p, cos, sin_signed, first_half):
    rot = jnp.where(first_half, pltpu.roll(p, LANES - 16, axis=1), pltpu.roll(p, 16, axis=1))
    return p * cos + rot * sin_signed


def _inproj_kernel(x_ref, mod_ref, g_ref, w_ref, cos_ref, sin_ref, q_ref, k_ref, v_ref):
    h = _norm_mod(x_ref[0], g_ref[...], mod_ref[0, 0:1, :], mod_ref[0, 1:2, :])
    p = jnp.dot(h.astype(bf16), w_ref[...], preferred_element_type=f32)
    cos = cos_ref[...]
    sin = sin_ref[...]
    lane = lax.broadcasted_iota(i32, cos.shape, 1)
    first_half = (lane % 32) < 16
    qw = N_HEADS * HEAD_DIM
    for c in range(qw // LANES):
        qc = p[:, c * LANES:(c + 1) * LANES]
        q_ref[0, :, c * LANES:(c + 1) * LANES] = (
            _rope(qc, cos, sin, first_half) * (LOG2E * HEAD_DIM ** -0.5)).astype(bf16)
    k_ref[0] = _rope(p[:, qw:qw + LANES], cos, sin, first_half).astype(bf16)
    v_ref[0] = p[:, qw + LANES:].astype(bf16)


def _inproj(x, mod, g, w, cos_t, sin_t):
    b, l, d = x.shape
    tm = TOKEN_TILE
    n = w.shape[1]
    return pl.pallas_call(
        _inproj_kernel,
        grid=(b, l // tm),
        in_specs=[pl.BlockSpec((1, tm, d), lambda i, j: (i, j, 0)),
                  pl.BlockSpec((1, 6, d), lambda i, j: (i, 0, 0)),
                  pl.BlockSpec((1, d), lambda i, j: (0, 0)),
                  pl.BlockSpec((d, n), lambda i, j: (0, 0)),
                  pl.BlockSpec((tm, LANES), lambda i, j: (j, 0)),
                  pl.BlockSpec((tm, LANES), lambda i, j: (j, 0))],
        out_specs=[pl.BlockSpec((1, tm, N_HEADS * HEAD_DIM), lambda i, j: (i, j, 0)),
                   pl.BlockSpec((1, tm, LANES), lambda i, j: (i, j, 0)),
                   pl.BlockSpec((1, tm, LANES), lambda i, j: (i, j, 0))],
        out_shape=[jax.ShapeDtypeStruct((b, l, N_HEADS * HEAD_DIM), bf16),
                   jax.ShapeDtypeStruct((b, l, LANES), bf16),
                   jax.ShapeDtypeStruct((b, l, LANES), bf16)],
        compiler_params=_cparams(("parallel", "parallel")),
        name="inproj",
    )(x, mod, g, w, cos_t, sin_t)


def _ctxkv_kernel(x_ref, mod_ref, g_ref, w_ref, k_ref, v_ref):
    h = _norm_mod(x_ref[0], g_ref[...], mod_ref[0, 0:1, :], mod_ref[0, 1:2, :])
    p = jnp.dot(h.astype(bf16), w_ref[...], preferred_element_type=f32)
    k_ref[0] = p[:, :LANES].astype(bf16)
    v_ref[0] = p[:, LANES:].astype(bf16)


def _ctxkv(ctx, mod, g, w_kv, ctx_row):
    b, c, d = ctx.shape
    return pl.pallas_call(
        _ctxkv_kernel,
        grid=(b,),
        in_specs=[pl.BlockSpec((1, c, d), lambda i: (i, 0, 0)),
                  pl.BlockSpec((1, 6, d), lambda i: (ctx_row, 0, 0)),
                  pl.BlockSpec((1, d), lambda i: (0, 0)),
                  pl.BlockSpec((d, 2 * LANES), lambda i: (0, 0))],
        out_specs=[pl.BlockSpec((1, c, LANES), lambda i: (i, 0, 0)),
                   pl.BlockSpec((1, c, LANES), lambda i: (i, 0, 0))],
        out_shape=[jax.ShapeDtypeStruct((b, c, LANES), bf16),
                   jax.ShapeDtypeStruct((b, c, LANES), bf16)],
        compiler_params=_cparams(("parallel",)),
        name="ctxkv",
    )(ctx, mod, g, w_kv)


def _attn_kernel(seq_len, q_ref, kp_ref, km_ref, kn_ref, vp_ref, vm_ref, vn_ref,
                 ck_ref, cv_ref, sink_ref, o_ref, kext, vext, cvext):
    j = pl.program_id(1)
    tq = ATT_TILE
    blk = ATT_BLOCK
    kext[0:blk] = kp_ref[0]
    kext[blk:blk + tq] = km_ref[0]
    kext[blk + tq:] = kn_ref[0]
    vext[:, LANES:] = jnp.ones((tq + 2 * blk, LANES), bf16)
    vext[0:blk, :LANES] = vp_ref[0]
    vext[blk:blk + tq, :LANES] = vm_ref[0]
    vext[blk + tq:, :LANES] = vn_ref[0]
    cvext[:, LANES:] = jnp.ones((cvext.shape[0], LANES), bf16)
    cvext[:, :LANES] = cv_ref[0]
    ck = ck_ref[0]
    sink = sink_ref[...]
    n_chunks = (N_HEADS * HEAD_DIM) // LANES
    rows = 2 * n_chunks * blk
    lane = lax.broadcasted_iota(i32, (blk, LANES), 1)
    low = lane < HEAD_DIM
    qi = lax.broadcasted_iota(i32, (rows, 3 * blk), 0) % blk
    pk = lax.broadcasted_iota(i32, (rows, 3 * blk), 1)
    band_bias = jnp.where(jnp.abs(pk - blk - qi) <= WINDOW, 0.0, NEG_INF).astype(f32)
    pcol = lax.broadcasted_iota(i32, (1, 3 * blk), 1)
    nt = (((1,), (1,)), ((), ()))

    def sub(s, carry):
        r0 = pl.multiple_of(s * blk, blk)
        qs = q_ref[0, pl.ds(r0, blk), :]
        parts = []
        for c in range(n_chunks):
            qc = qs[:, c * LANES:(c + 1) * LANES]
            parts.append(jnp.where(low, qc, jnp.zeros_like(qc)))
            parts.append(jnp.where(low, jnp.zeros_like(qc), qc))
        lhs = jnp.concatenate(parts, axis=0)
        kl = kext[pl.ds(r0, 3 * blk), :]
        vl = vext[pl.ds(r0, 3 * blk), :]
        kpos = j * tq + r0 - blk + pcol
        col_bias = jnp.where((kpos >= 0) & (kpos < seq_len), 0.0, NEG_INF).astype(f32)
        s_c = lax.dot_general(lhs, ck, nt, preferred_element_type=f32)
        s_l = lax.dot_general(lhs, kl, nt, preferred_element_type=f32) + band_bias + col_bias
        blocks = ([s_c[:, i:i + LANES] for i in range(0, s_c.shape[1], LANES)]
                  + [s_l[:, i:i + LANES] for i in range(0, s_l.shape[1], LANES)])
        folded = functools.reduce(jnp.maximum, blocks)
        m = jnp.maximum(jnp.max(folded, axis=1, keepdims=True), sink)
        e_c = jnp.exp2(s_c - m).astype(bf16)
        e_l = jnp.exp2(s_l - m).astype(bf16)
        ov = (jnp.dot(e_c, cvext[...], preferred_element_type=f32)
              + jnp.dot(e_l, vl, preferred_element_type=f32))
        den = ov[:, LANES:] + jnp.exp2(sink - m)
        o = ov[:, :LANES] / den
        for c in range(n_chunks):
            oc = jnp.where(low, o[(2 * c) * blk:(2 * c + 1) * blk],
                           o[(2 * c + 1) * blk:(2 * c + 2) * blk])
            o_ref[0, pl.ds(r0, blk), c * LANES:(c + 1) * LANES] = oc.astype(bf16)
        return carry

    lax.fori_loop(0, tq // blk, sub, 0, unroll=2)


def _attention(q, k, v, ck, cv, sinkcol):
    b, l, qw = q.shape
    c = ck.shape[1]
    tq = ATT_TILE
    r = tq // ATT_BLOCK
    nb = l // ATT_BLOCK
    prev = pl.BlockSpec((1, ATT_BLOCK, LANES), lambda i, j: (i, jnp.maximum(j * r - 1, 0), 0))
    main = pl.BlockSpec((1, tq, LANES), lambda i, j: (i, j, 0))
    nxt = pl.BlockSpec((1, ATT_BLOCK, LANES), lambda i, j: (i, jnp.minimum(j * r + r, nb - 1), 0))
    cspec = pl.BlockSpec((1, c, LANES), lambda i, j: (i, 0, 0))
    return pl.pallas_call(
        functools.partial(_attn_kernel, l),
        grid=(b, l // tq),
        in_specs=[pl.BlockSpec((1, tq, qw), lambda i, j: (i, j, 0)),
                  prev, main, nxt, prev, main, nxt, cspec, cspec,
                  pl.BlockSpec(sinkcol.shape, lambda i, j: (0, 0))],
        out_specs=pl.BlockSpec((1, tq, qw), lambda i, j: (i, j, 0)),
        out_shape=jax.ShapeDtypeStruct((b, l, qw), bf16),
        scratch_shapes=[pltpu.VMEM((tq + 2 * ATT_BLOCK, LANES), bf16),
                        pltpu.VMEM((tq + 2 * ATT_BLOCK, 2 * LANES), bf16),
                        pltpu.VMEM((c, 2 * LANES), bf16)],
        compiler_params=_cparams(("parallel", "parallel")),
        name="attention",
    )(q, k, k, k, v, v, v, ck, cv, sinkcol)


def _pack_pair(lo, hi):
    lo = lax.bitcast_convert_type(lo.astype(bf16).astype(f32), u32)
    hi = lax.bitcast_convert_type(hi.astype(bf16).astype(f32), u32)
    return (lo >> 16) | (hi & jnp.uint32(0xFFFF0000))


def _fourier1_kernel(x_ref, mod_ref, g_ref, w_ref, cs_ref, m_ref, ct_ref, st_ref, z_ref, ab_ref):
    n1 = x_ref.shape[1]
    nt = x_ref.shape[2]
    x = x_ref[0].reshape(n1 * nt, x_ref.shape[3])
    h = _norm_mod(x, g_ref[...], mod_ref[0, 0:1, :], mod_ref[0, 1:2, :]).astype(bf16)
    p = jnp.dot(h, w_ref[...], preferred_element_type=f32)
    for g in range(FOURIER_GROUPS):
        ug = p[:, g * LANES:(g + 1) * LANES].astype(bf16)
        ab = jnp.dot(ug, cs_ref[...], preferred_element_type=f32)
        ab_ref[0] = ab[:, :LANES]
        ab_ref[1] = ab[:, LANES:]
        for t in range(nt):
            stack = jnp.concatenate([ab_ref[0, pl.ds(t, n1, stride=nt), :],
                                     ab_ref[1, pl.ds(t, n1, stride=nt), :]], axis=0).astype(bf16)
            z = jnp.dot(m_ref[...], stack, preferred_element_type=f32)
            zr, zn = z[:n1], z[n1:]
            ct, st = ct_ref[t], st_ref[t]
            z_ref[0, g, t] = _pack_pair(ct * zr - st * zn, ct * zn + st * zr)


def _fourier2_kernel(scale, z_ref, m_ref, o_ref, zbuf, ybuf):
    _, grp, n2, tk, w = z_ref.shape
    for g in range(grp):
        zbuf[...] = z_ref[0, g].reshape(n2 * tk, w)
        for j in range(tk):
            zp = zbuf[pl.ds(j, n2, stride=tk), :]
            zr = lax.bitcast_convert_type(zp << 16, f32).astype(bf16)
            zn = lax.bitcast_convert_type(zp & jnp.uint32(0xFFFF0000), f32).astype(bf16)
            y = jnp.dot(m_ref[...], jnp.concatenate([zr, zn], axis=0), preferred_element_type=f32)
            ybuf[pl.ds(j, n2, stride=tk), :] = y * scale
        o_ref[0, g] = ybuf[...].reshape(n2, tk, w)


def _fourier(x, mod, g, w_f, cs):
    b, l, d = x.shape
    n2 = DFT_INNER
    n1 = l // n2
    grp, w = FOURIER_GROUPS, FOURIER_GROUP_W
    t2 = SUBLANES
    k1 = jnp.arange(n1, dtype=i32)
    ang1 = ((k1[:, None] * k1[None, :]) % n1).astype(f32) * (2.0 * math.pi / n1)
    c1, s1 = jnp.cos(ang1), jnp.sin(ang1)
    m1 = jnp.concatenate([jnp.concatenate([c1, -s1], axis=1),
                          jnp.concatenate([s1, c1], axis=1)], axis=0).astype(bf16)
    l2 = jnp.arange(n2, dtype=i32)
    angt = ((l2[:, None] * k1[None, :]) % l).astype(f32) * (2.0 * math.pi / l)
    ct = jnp.broadcast_to(jnp.cos(angt)[:, :, None], (n2, n1, w))
    st = jnp.broadcast_to(jnp.sin(angt)[:, :, None], (n2, n1, w))
    ang2 = ((l2[:, None] * l2[None, :]) % n2).astype(f32) * (2.0 * math.pi / n2)
    m2 = jnp.concatenate([jnp.cos(ang2), -jnp.sin(ang2)], axis=1).astype(bf16)

    tspec = pl.BlockSpec((t2, n1, w), lambda t, i: (t, 0, 0))
    z = pl.pallas_call(
        _fourier1_kernel,
        grid=(n2 // t2, b),
        in_specs=[pl.BlockSpec((1, n1, t2, d), lambda t, i: (i, 0, t, 0)),
                  pl.BlockSpec((1, 6, d), lambda t, i: (i, 0, 0)),
                  pl.BlockSpec((1, d), lambda t, i: (0, 0)),
                  pl.BlockSpec(w_f.shape, lambda t, i: (0, 0)),
                  pl.BlockSpec(cs.shape, lambda t, i: (0, 0)),
                  pl.BlockSpec(m1.shape, lambda t, i: (0, 0)), tspec, tspec],
        out_specs=pl.BlockSpec((1, grp, t2, n1, w), lambda t, i: (i, 0, t, 0, 0)),
        out_shape=jax.ShapeDtypeStruct((b, grp, n2, n1, w), u32),
        scratch_shapes=[pltpu.VMEM((2, n1 * t2, w), f32)],
        compiler_params=_cparams(("parallel", "parallel")),
        name="fourier_outer",
    )(x.reshape(b, n1, n2, d), mod, g, w_f, cs, m1, ct, st)

    tk = SUBLANES
    y = pl.pallas_call(
        functools.partial(_fourier2_kernel, 1.0 / math.sqrt(l * w)),
        grid=(b, n1 // tk),
        in_specs=[pl.BlockSpec((1, grp, n2, tk, w), lambda i, t: (i, 0, 0, t, 0)),
                  pl.BlockSpec(m2.shape, lambda i, t: (0, 0))],
        out_specs=pl.BlockSpec((1, grp, n2, tk, w), lambda i, t: (i, 0, 0, t, 0)),
        out_shape=jax.ShapeDtypeStruct((b, grp, n2, n1, w), f32),
        scratch_shapes=[pltpu.VMEM((n2 * tk, w), u32), pltpu.VMEM((n2 * tk, w), f32)],
        compiler_params=_cparams(("parallel", "parallel")),
        name="fourier_inner",
    )(z, m2)
    return y.reshape(b, grp, l, w)


def _first_max4(a):
    m = jnp.maximum(jnp.maximum(a[0], a[1]), jnp.maximum(a[2], a[3]))
    idx = jnp.where(a[0] == m, 0, jnp.where(a[1] == m, 1, jnp.where(a[2] == m, 2, 3)))
    return m, idx


def _pick4(vals, idx):
    return jnp.where(idx == 0, vals[0], jnp.where(idx == 1, vals[1],
                                                   jnp.where(idx == 2, vals[2], vals[3])))


def _route(f, rw_ref, rb_ref, tri_ref, base_ref, first_step, ri_ref, wc_ref, cnt_ref, meta_ref):
    tm = f.shape[0]
    f_hi = f.astype(bf16)
    f_lo = (f - f_hi.astype(f32)).astype(bf16)
    rw2 = rw_ref[...]
    part = jnp.dot(f_hi, rw2, preferred_element_type=f32)
    logits = (part[:, :LANES] + part[:, LANES:]
              + jnp.dot(f_lo, rw2[:, :LANES], preferred_element_type=f32))
    sc = jax.nn.sigmoid(logits)
    st = sc.T
    bt = (sc + rb_ref[...]).T
    neg = jnp.full((1, tm), -jnp.inf, f32)
    gs = []
    for g in range(N_GROUPS):
        a = [bt[4 * g + i: 4 * g + i + 1] for i in range(4)]
        m1, i1 = _first_max4(a)
        rest = [jnp.where(i1 == i, neg, a[i]) for i in range(4)]
        m2, _ = _first_max4(rest)
        gs.append(m1 + m2)
    _, gsel = _first_max4(gs)
    a = [_pick4([bt[4 * g + i: 4 * g + i + 1] for g in range(N_GROUPS)], gsel) for i in range(4)]
    s = [_pick4([st[4 * g + i: 4 * g + i + 1] for g in range(N_GROUPS)], gsel) for i in range(4)]
    _, i1 = _first_max4(a)
    rest = [jnp.where(i1 == i, neg, a[i]) for i in range(4)]
    _, i2 = _first_max4(rest)
    w1 = _pick4(s, i1)
    w2 = _pick4(s, i2)
    tot = w1 + w2
    w1 = w1 / tot
    w2 = w2 / tot
    e0 = gsel * EXPERTS_PER_GROUP + i1
    e1 = gsel * EXPERTS_PER_GROUP + i2

    @pl.when(first_step)
    def _():
        base_ref[...] = jnp.zeros_like(base_ref)

    td = DISPATCH_TILE
    eid = lax.broadcasted_iota(i32, (N_EXPERTS, tm), 0)
    oh0 = (eid == e0).astype(f32)
    oh1 = (eid == e1).astype(f32)
    oh = oh0 + oh1
    before = jnp.dot(oh.astype(bf16), tri_ref[...], preferred_element_type=f32)
    lane_tile = lax.broadcasted_iota(i32, (N_EXPERTS, tm), 1) // td
    ei = lax.broadcasted_iota(i32, (N_EXPERTS, N_EXPERTS), 0)
    ej = lax.broadcasted_iota(i32, (N_EXPERTS, N_EXPERTS), 1)
    strict_lower = (ej < ei).astype(f32)
    run_start = jnp.zeros((N_EXPERTS, tm), f32)
    goff = base_ref[...]
    for s in range(tm // td):
        cnt_s = jnp.sum(oh[:, s * td:(s + 1) * td], axis=1, keepdims=True)
        pad_s = jnp.floor((cnt_s + 7.0) * 0.125) * 8.0
        pad_b = jnp.broadcast_to(pad_s, (N_EXPERTS, LANES))
        start_b = jnp.dot(strict_lower, pad_b, precision=HIGHEST, preferred_element_type=f32)
        run_start = jnp.where(lane_tile == s, start_b[:, 0:1], run_start)
        meta_ref[s, 0] = start_b.astype(i32)
        meta_ref[s, 1] = pad_b.astype(i32)
        meta_ref[s, 2] = goff.astype(i32)
        goff = goff + pad_b
    base_ref[...] = goff
    cnt_ref[...] = goff
    pos = before + run_start
    lp0 = jnp.sum(oh0 * pos, axis=0, keepdims=True)
    lp1 = jnp.sum(oh1 * pos, axis=0, keepdims=True)
    zi = jnp.zeros((1, tm), i32)
    ri_ref[...] = jnp.concatenate(
        [lp0.astype(i32), lp1.astype(i32), e0, e1, zi, zi, zi, zi], axis=0)
    zf = jnp.zeros((LANES - 4, tm), f32)
    wc_ref[...] = jnp.concatenate([w1, w2, lp0, lp1, zf], axis=0).T


def _outproj_kernel(yf_ref, o_ref, x_ref, mod_ref, w_ref, g_ref, rw_ref, rb_ref, tri_ref,
                    x1_ref, f_ref, ri_ref, wc_ref, cnt_ref, meta_ref, base_ref):
    mix = jnp.concatenate([yf_ref[0, g].astype(bf16) for g in range(FOURIER_GROUPS)] + [o_ref[0]],
                          axis=1)
    y = jnp.dot(mix, w_ref[...], preferred_element_type=f32)
    x1 = x_ref[0] + mod_ref[0, 2:3, :] * y
    x1_ref[0] = x1
    f = _norm_mod(x1, g_ref[...], mod_ref[0, 3:4, :], mod_ref[0, 4:5, :])
    f_ref[0] = f.astype(bf16)
    first = (pl.program_id(0) == 0) & (pl.program_id(1) == 0)
    _route(f, rw_ref, rb_ref, tri_ref, base_ref, first, ri_ref, wc_ref, cnt_ref, meta_ref)


def _route_specs(b, l, tm):
    nl = l // tm
    rw = lambda d: pl.BlockSpec((d, 2 * LANES), lambda i, j: (0, 0))
    rb = pl.BlockSpec((1, LANES), lambda i, j: (0, 0))
    tri = pl.BlockSpec((tm, tm), lambda i, j: (0, 0))
    ns = tm // DISPATCH_TILE
    out_specs = [pl.BlockSpec((8, tm), lambda i, j: (0, i * nl + j)),
                 pl.BlockSpec((tm, LANES), lambda i, j: (i * nl + j, 0)),
                 pl.BlockSpec((N_EXPERTS, LANES), lambda i, j: (0, 0)),
                 pl.BlockSpec((ns, 3, N_EXPERTS, LANES), lambda i, j: (i * nl + j, 0, 0, 0))]
    out_shape = [jax.ShapeDtypeStruct((8, b * l), i32),
                 jax.ShapeDtypeStruct((b * l, LANES), f32),
                 jax.ShapeDtypeStruct((N_EXPERTS, LANES), f32),
                 jax.ShapeDtypeStruct((b * l // DISPATCH_TILE, 3, N_EXPERTS, LANES), i32)]
    return rw, rb, tri, out_specs, out_shape


def _outproj(yf, o, x, mod, w, g, rw, rb, tri):
    b, l, d = x.shape
    tm = TOKEN_TILE
    rws, rbs, tris, r_specs, r_shapes = _route_specs(b, l, tm)
    row = pl.BlockSpec((1, tm, d), lambda i, j: (i, j, 0))
    return pl.pallas_call(
        _outproj_kernel,
        grid=(b, l // tm),
        in_specs=[pl.BlockSpec((1, FOURIER_GROUPS, tm, LANES), lambda i, j: (i, 0, j, 0)),
                  pl.BlockSpec((1, tm, o.shape[2]), lambda i, j: (i, j, 0)),
                  row,
                  pl.BlockSpec((1, 6, d), lambda i, j: (i, 0, 0)),
                  pl.BlockSpec(w.shape, lambda i, j: (0, 0)),
                  pl.BlockSpec((1, d), lambda i, j: (0, 0)),
                  rws(d), rbs, tris],
        out_specs=[row, row] + r_specs,
        out_shape=[jax.ShapeDtypeStruct((b, l, d), f32),
                   jax.ShapeDtypeStruct((b, l, d), bf16)] + r_shapes,
        scratch_shapes=[pltpu.VMEM((N_EXPERTS, LANES), f32)],
        compiler_params=_cparams(("arbitrary", "arbitrary")),
        name="outproj_router",
    )(yf, o, x, mod, w, g, rw, rb, tri)


def _conv_kernel(seq_len, up_ref, um_ref, un_ref, x_ref, mod_ref, dw_ref, db_ref, lg_ref, lb_ref,
                 w_ref, pb_ref, g_ref, rw_ref, rb_ref, tri_ref,
                 x1_ref, f_ref, ri_ref, wc_ref, cnt_ref, meta_ref, base_ref, ext, conv_out):
    j = pl.program_id(1)
    tm = um_ref.shape[1]
    hl = CONV_HALO
    half = CONV_W // 2
    ext[0:hl] = jnp.where(j > 0, up_ref[0], jnp.zeros_like(up_ref[0]))
    ext[hl:hl + tm] = um_ref[0]
    ext[hl + tm:] = jnp.where((j + 1) * tm < seq_len, un_ref[0], jnp.zeros_like(un_ref[0]))
    base = hl - half
    span = (CONV_W - 1) // SUBLANES * SUBLANES
    rows = CONV_ROWS

    def lane_chunk(c, carry):
        lanes = pl.ds(pl.multiple_of(c * LANES, LANES), LANES)
        for r in range(0, tm, rows):
            part = jnp.broadcast_to(db_ref[:, lanes], (rows, LANES))
            for phase in range(SUBLANES):
                win = ext[base + phase + r: base + phase + r + rows + span, lanes]
                same = None
                for t in range(phase, CONV_W, SUBLANES):
                    term = win[t - phase: t - phase + rows, :] * dw_ref[t:t + 1, lanes]
                    same = term if same is None else same + term
                part = part + same
            conv_out[r:r + rows, lanes] = part
        return carry

    lax.fori_loop(0, um_ref.shape[2] // LANES, lane_chunk, 0)
    acc = conv_out[...]
    mu = jnp.mean(acc, axis=-1, keepdims=True)
    cen = acc - mu
    var = jnp.mean(cen * cen, axis=-1, keepdims=True)
    ln = cen * lax.rsqrt(var + EPS) * lg_ref[...] + lb_ref[...]
    act = ln * jax.nn.sigmoid(ln)
    y = jnp.dot(act.astype(bf16), w_ref[...], preferred_element_type=f32) + pb_ref[...]
    x1 = x_ref[0] + mod_ref[0, 2:3, :] * y
    x1_ref[0] = x1
    f = _norm_mod(x1, g_ref[...], mod_ref[0, 3:4, :], mod_ref[0, 4:5, :])
    f_ref[0] = f.astype(bf16)
    first = (pl.program_id(0) == 0) & (j == 0)
    _route(f, rw_ref, rb_ref, tri_ref, base_ref, first, ri_ref, wc_ref, cnt_ref, meta_ref)


def _conv(u, x, mod, dw_w, dw_b, ln_g, ln_b, pw2_w, pw2_b, g, rw, rb, tri):
    b, l, d = x.shape
    tm = TOKEN_TILE
    hl = CONV_HALO
    r = tm // hl
    nh = l // hl
    rws, rbs, tris, r_specs, r_shapes = _route_specs(b, l, tm)
    row = pl.BlockSpec((1, tm, d), lambda i, j: (i, j, 0))
    vec = pl.BlockSpec((1, d), lambda i, j: (0, 0))
    return pl.pallas_call(
        functools.partial(_conv_kernel, l),
        grid=(b, l // tm),
        in_specs=[pl.BlockSpec((1, hl, d), lambda i, j: (i, jnp.maximum(j * r - 1, 0), 0)),
                  row,
                  pl.BlockSpec((1, hl, d), lambda i, j: (i, jnp.minimum(j * r + r, nh - 1), 0)),
                  row,
                  pl.BlockSpec((1, 6, d), lambda i, j: (i, 0, 0)),
                  pl.BlockSpec(dw_w.shape, lambda i, j: (0, 0)),
                  vec, vec, vec,
                  pl.BlockSpec(pw2_w.shape, lambda i, j: (0, 0)),
                  vec, vec, rws(d), rbs, tris],
        out_specs=[row, row] + r_specs,
        out_shape=[jax.ShapeDtypeStruct((b, l, d), f32),
                   jax.ShapeDtypeStruct((b, l, d), bf16)] + r_shapes,
        scratch_shapes=[pltpu.VMEM((N_EXPERTS, LANES), f32),
                        pltpu.VMEM((tm + 2 * hl, d), f32),
                        pltpu.VMEM((tm, d), f32)],
        compiler_params=_cparams(("arbitrary", "arbitrary")),
        name="conv_router",
    )(u, u, u, x, mod, dw_w, dw_b, ln_g, ln_b, pw2_w, pw2_b, g, rw, rb, tri)


def _pack_bf16_pairs(x):
    h = x.shape[1] // 2
    lo = lax.bitcast_convert_type(x[:, :h], u32)
    hi = lax.bitcast_convert_type(x[:, h:], u32)
    return (lo >> 16) | (hi & jnp.uint32(0xFFFF0000))


def _unpack_bf16_pairs(u):
    lo = lax.bitcast_convert_type(u << 16, f32)
    hi = lax.bitcast_convert_type(u & jnp.uint32(0xFFFF0000), f32)
    return jnp.concatenate([lo, hi], axis=1).astype(bf16)


def _run_copies(meta, tile, local_ref, hbm_ref, sem, to_hbm):
    start_ref, size_ref, dst_ref = meta
    for e in range(N_EXPERTS):
        k = tile * N_EXPERTS + e
        size = pl.multiple_of(size_ref[k], RUN_ALIGN)

        @pl.when(size > 0)
        def _():
            loc = local_ref.at[pl.ds(pl.multiple_of(start_ref[k], RUN_ALIGN), size)]
            glob = hbm_ref.at[pl.ds(pl.multiple_of(dst_ref[k], RUN_ALIGN), size)]
            if to_hbm:
                pltpu.make_async_copy(loc, glob, sem).start()
            else:
                pltpu.make_async_copy(glob, loc, sem).start()


def _wait_rows(rows, local_ref, hbm_ref, sem):
    rows = pl.multiple_of(rows, RUN_ALIGN)

    @pl.when(rows > 0)
    def _():
        pltpu.make_async_copy(local_ref.at[pl.ds(0, rows)], hbm_ref.at[pl.ds(0, rows)], sem).wait()


def _dispatch_kernel(start_ref, size_ref, dst_ref, tot_ref, tail_start_ref, tail_size_ref, nv_ref,
                     f_ref, lp_ref, xs_ref, loc, zbuf, sem, zsem):
    i = pl.program_id(0)
    n = pl.num_programs(0)
    slot = i % 2
    subs = loc.shape[1]
    rows = loc.shape[2]
    td = f_ref.shape[0] // subs
    meta = (start_ref, size_ref, dst_ref)

    def drain(step, which):
        for s in range(subs):
            _wait_rows(tot_ref[step * subs + s], loc.at[which, s], xs_ref, sem.at[which])

    @pl.when(i >= 2)
    def _():
        drain(i - 2, slot)

    r = lax.broadcasted_iota(i32, (rows, td), 0)
    for s in range(subs):
        cols = slice(s * td, (s + 1) * td)
        onehot = ((r == lp_ref[0:1, cols]) | (r == lp_ref[1:2, cols])).astype(bf16)
        sorted_rows = jnp.dot(onehot, f_ref[cols, :], preferred_element_type=f32)
        loc[slot, s] = _pack_bf16_pairs(sorted_rows)
        _run_copies(meta, i * subs + s, loc.at[slot, s], xs_ref, sem.at[slot], to_hbm=True)

    @pl.when(i == n - 1)
    def _():
        zbuf[...] = jnp.zeros_like(zbuf)
        total = 0
        for e in range(N_EXPERTS):
            size = pl.multiple_of(tail_size_ref[e], RUN_ALIGN)
            total = total + size

            @pl.when(size > 0)
            def _():
                pltpu.make_async_copy(
                    zbuf.at[pl.ds(0, size)],
                    xs_ref.at[pl.ds(pl.multiple_of(tail_start_ref[e], RUN_ALIGN), size)], zsem).start()

        _wait_rows(total, zbuf, xs_ref, zsem)

        def zero_block(k, c):
            pltpu.make_async_copy(zbuf, xs_ref.at[pl.ds(pl.multiple_of(k * zbuf.shape[0], RUN_ALIGN),
                                                        zbuf.shape[0])], zsem).start()
            return c

        def wait_block(k, c):
            pltpu.make_async_copy(zbuf, xs_ref.at[pl.ds(0, zbuf.shape[0])], zsem).wait()
            return c

        n_blocks = xs_ref.shape[0] // zbuf.shape[0]
        lax.fori_loop(nv_ref[0], n_blocks, zero_block, 0)
        lax.fori_loop(nv_ref[0], n_blocks, wait_block, 0)
        drain(i, slot)

        @pl.when(i >= 1)
        def _():
            drain(i - 1, 1 - slot)


def _dispatch(tables, f2, ri, n_slots):
    t, d = f2.shape
    subs = DISPATCH_SUBTILES
    tm = DISPATCH_TILE * subs
    return pl.pallas_call(
        _dispatch_kernel,
        grid_spec=pltpu.PrefetchScalarGridSpec(
            num_scalar_prefetch=7,
            grid=(t // tm,),
            in_specs=[pl.BlockSpec((tm, d), lambda i, *_: (i, 0)),
                      pl.BlockSpec((8, tm), lambda i, *_: (0, i))],
            out_specs=pl.BlockSpec(memory_space=pl.ANY),
            scratch_shapes=[pltpu.VMEM((2, subs, LOCAL_ROWS, d // 2), u32),
                            pltpu.VMEM((EXPERT_ROWS, d // 2), u32),
                            pltpu.SemaphoreType.DMA((2,)), pltpu.SemaphoreType.DMA(())]),
        out_shape=jax.ShapeDtypeStruct((n_slots, d // 2), u32),
        compiler_params=_cparams(("arbitrary",)),
        name="moe_dispatch",
    )(*tables, f2, ri)


def _expert_kernel(be_ref, nv_ref, x_ref, wg_ref, wu_ref, wd_ref, y_ref, wgb, wub, wdb):
    i = pl.program_id(0)
    changed = jnp.logical_or(i == 0, be_ref[i] != be_ref[jnp.maximum(i - 1, 0)])

    @pl.when(changed)
    def _():
        wgb[...] = wg_ref[0, 0].astype(bf16)
        wub[...] = wu_ref[0, 0].astype(bf16)
        wdb[...] = wd_ref[0, 0].astype(bf16)

    @pl.when(i < nv_ref[0])
    def _():
        xb = _unpack_bf16_pairs(x_ref[...])
        gate = jnp.dot(xb, wgb[...], preferred_element_type=f32)
        up = jnp.dot(xb, wub[...], preferred_element_type=f32)
        hid = (gate * jax.nn.sigmoid(gate) * up).astype(bf16)
        y = jnp.dot(hid, wdb[...], preferred_element_type=f32)
        y_ref[...] = _pack_bf16_pairs(y.astype(bf16).astype(f32))

    @pl.when(i >= nv_ref[0])
    def _():
        y_ref[...] = jnp.zeros_like(y_ref)


def _experts(block_e, n_valid, xs, w_gate, w_up, w_down, layer):
    ns, dh = xs.shape
    tb = EXPERT_ROWS
    d, ff = w_gate.shape[2:]
    xmap = lambda i, be, nv: (jnp.maximum(jnp.minimum(i, nv[0] - 1), 0), 0)
    wmap = lambda i, be, nv: (layer, be[i], 0, 0)
    return pl.pallas_call(
        _expert_kernel,
        grid_spec=pltpu.PrefetchScalarGridSpec(
            num_scalar_prefetch=2,
            grid=(ns // tb,),
            in_specs=[pl.BlockSpec((tb, dh), xmap),
                      pl.BlockSpec((1, 1, d, ff), wmap),
                      pl.BlockSpec((1, 1, d, ff), wmap),
                      pl.BlockSpec((1, 1, ff, d), wmap)],
            out_specs=pl.BlockSpec((tb, dh), lambda i, be, nv: (i, 0)),
            scratch_shapes=[pltpu.VMEM((d, ff), bf16), pltpu.VMEM((d, ff), bf16),
                            pltpu.VMEM((ff, d), bf16)]),
        out_shape=jax.ShapeDtypeStruct((ns, dh), u32),
        compiler_params=_cparams(("arbitrary",)),
        name="moe_experts",
    )(block_e, n_valid, xs, w_gate, w_up, w_down)


def _combine_kernel(final, start_ref, size_ref, dst_ref, tot_ref, ys_ref, wc_ref, x_ref, mod_ref,
                    g_ref, *rest):
    if final:
        o_ref, loc, sem = rest
    else:
        nmod_ref, w_ref, b_ref, o_ref, u_ref, loc, sem = rest
    i = pl.program_id(0)
    n = pl.num_programs(0)
    slot = i % 2
    subs = loc.shape[1]
    rows = loc.shape[2]
    td = x_ref.shape[0] // subs
    meta = (start_ref, size_ref, dst_ref)

    def fetch(step, which):
        for s in range(subs):
            _run_copies(meta, step * subs + s, loc.at[which, s], ys_ref, sem.at[which, s],
                        to_hbm=False)

    @pl.when(i == 0)
    def _():
        loc[...] = jnp.zeros_like(loc)
        fetch(i, slot)

    @pl.when(i + 1 < n)
    def _():
        fetch(i + 1, 1 - slot)

    c = lax.broadcasted_iota(i32, (td, rows), 1)
    parts = []
    for s in range(subs):
        _wait_rows(tot_ref[i * subs + s], loc.at[slot, s], ys_ref, sem.at[slot, s])
        wc = wc_ref[s * td:(s + 1) * td, :]
        sel = jnp.concatenate([(c == wc[:, 2:3].astype(i32)).astype(bf16),
                               (c == wc[:, 3:4].astype(i32)).astype(bf16)], axis=0)
        picked = jnp.dot(sel, _unpack_bf16_pairs(loc[slot, s]), preferred_element_type=f32)
        parts.append(wc[:, 0:1] * picked[:td] + wc[:, 1:2] * picked[td:])
    xo = x_ref[...] + mod_ref[0, 5:6, :] * jnp.concatenate(parts, axis=0)
    if final:
        ms = jnp.mean(xo * xo, axis=-1, keepdims=True)
        o_ref[...] = xo * lax.rsqrt(ms + EPS) * g_ref[...]
    else:
        o_ref[...] = xo
        h = _norm_mod(xo, g_ref[...], nmod_ref[0, 0:1, :], nmod_ref[0, 1:2, :])
        p = jnp.dot(h.astype(bf16), w_ref[...], preferred_element_type=f32) + b_ref[...]
        ch = p.shape[1] // 2
        u_ref[...] = p[:, :ch] * jax.nn.sigmoid(p[:, ch:])


def _combine(tables, ys, wc, x, mod, g, glu=None):
    b, l, d = x.shape
    subs = DISPATCH_SUBTILES if glu is None else GLU_SUBTILES
    tm = DISPATCH_TILE * subs
    per_batch = l // tm
    const = lambda i, *_: (0, 0)
    tile = lambda cols: pl.BlockSpec((tm, cols), lambda i, *_: (i, 0))
    mod_spec = pl.BlockSpec((1, 6, d), lambda i, *_: (i // per_batch, 0, 0))
    in_specs = [pl.BlockSpec(memory_space=pl.ANY), tile(LANES), tile(d), mod_spec,
                pl.BlockSpec((1, d), const)]
    args = [ys, wc, x.reshape(b * l, d), mod, g]
    out_specs = [tile(d)]
    out_shape = [jax.ShapeDtypeStruct((b * l, d), f32)]
    if glu is not None:
        nmod, w, bias = glu
        in_specs += [mod_spec, pl.BlockSpec(w.shape, const), pl.BlockSpec(bias.shape, const)]
        args += [nmod, w, bias]
        out_specs.append(tile(w.shape[1] // 2))
        out_shape.append(jax.ShapeDtypeStruct((b * l, w.shape[1] // 2), f32))
    outs = pl.pallas_call(
        functools.partial(_combine_kernel, glu is None),
        grid_spec=pltpu.PrefetchScalarGridSpec(
            num_scalar_prefetch=4,
            grid=(b * per_batch,),
            in_specs=in_specs,
            out_specs=out_specs,
            scratch_shapes=[pltpu.VMEM((2, subs, LOCAL_ROWS, d // 2), u32),
                            pltpu.SemaphoreType.DMA((2, subs))]),
        out_shape=out_shape,
        compiler_params=_cparams(("arbitrary",)),
        name="moe_combine",
    )(*tables, *args)
    return [o.reshape(b, l, -1) for o in outs]


def _moe(f, routed, x, mod, g, w_gate, w_up, w_down, layer, glu=None):
    ri, wc, cnt, meta = routed
    b, l, d = x.shape
    t = b * l
    tb = EXPERT_ROWS
    n_tiles = t // DISPATCH_TILE
    used = cnt[:, 0].astype(i32)
    region = (used + tb - 1) // tb * tb
    gend = jnp.cumsum(region)
    gstart = gend - region
    max_rows = 2 * t + n_tiles * N_EXPERTS * (RUN_ALIGN - 1) + N_EXPERTS * (tb - 1)
    n_blocks = -(-max_rows // tb)
    m = meta[:, :, :, 0]
    run_start = m[:, 0].reshape(-1)
    run_size = m[:, 1].reshape(-1)
    run_dst = (m[:, 2] + gstart[None, :]).reshape(-1)
    tile_rows = jnp.sum(m[:, 1], axis=1)
    block_row = jnp.arange(n_blocks, dtype=i32) * tb
    block_e = jnp.minimum(jnp.sum((block_row[:, None] >= gend[None, :]).astype(i32), axis=1),
                          N_EXPERTS - 1)
    n_valid = (gend[-1] // tb).reshape(1)
    xs = _dispatch((run_start, run_size, run_dst, tile_rows, gstart + used, region - used, n_valid),
                   f.reshape(t, d), ri, n_blocks * tb)
    ys = _experts(block_e, n_valid, xs, w_gate, w_up, w_down, layer)
    return _combine((run_start, run_size, run_dst, tile_rows), ys, wc, x, mod, g, glu)


def _rope_tables(l):
    lane = jnp.arange(LANES)
    dh = lane % HEAD_DIM
    inv = ROPE_THETA ** (-(dh % 16).astype(f32) / 16.0)
    pos = jnp.arange(l)
    row = (pos // GRID_W).astype(f32)
    col = (pos % GRID_W).astype(f32)
    p = jnp.where((dh // 32)[None, :] == 0, row[:, None], col[:, None])
    ang = p * inv[None, :]
    sign = jnp.where((dh % 32) < 16, -1.0, 1.0).astype(f32)
    return jnp.cos(ang), jnp.sin(ang) * sign[None, :]


def kernel(x, c, ctx, c_ctx, ada_w, ada_b, norm_mix_g, norm_ffn_g, even_w_in, even_w_out, even_sink, conv_pw1_w, conv_pw1_b, conv_dw_w, conv_dw_b, conv_ln_g, conv_ln_b, conv_pw2_w, conv_pw2_b, router_w, router_b, moe_w_gate, moe_w_up, moe_w_down, final_norm_g):
    b, l, d = x.shape
    depth = ada_w.shape[0]
    assert depth == 2 and b < COND_ROWS
    ctx_row = b
    cond = jnp.zeros((COND_ROWS, d), f32).at[:b].set(c).at[ctx_row].set(c_ctx)
    mods = _adaln(cond, ada_w, ada_b).reshape(depth, COND_ROWS, 6, d)

    heads = jnp.arange(N_HEADS).reshape(N_KV_HEADS, N_HEADS // N_KV_HEADS).T.reshape(-1)
    qperm = (heads[:, None] * HEAD_DIM + jnp.arange(HEAD_DIM)[None, :]).reshape(-1)
    fw = FOURIER_GROUPS * FOURIER_GROUP_W
    qw = N_HEADS * HEAD_DIM
    w_in = even_w_in[0]
    w_in_p = jnp.concatenate([w_in[:, :fw], w_in[:, fw:fw + qw][:, qperm], w_in[:, fw + qw:]],
                             axis=1).astype(bf16)
    w_out = even_w_out[0]
    w_out_p = jnp.concatenate([w_out[:fw], w_out[fw:][qperm]], axis=0).astype(bf16)
    sinkcol = jnp.repeat(even_sink[0][heads].astype(f32) * LOG2E, ATT_BLOCK).reshape(-1, 1)

    cidx = jnp.arange(FOURIER_GROUP_W, dtype=i32)
    angc = ((cidx[:, None] * cidx[None, :]) % FOURIER_GROUP_W).astype(f32) * (2.0 * math.pi / FOURIER_GROUP_W)
    cs = jnp.concatenate([jnp.cos(angc), jnp.sin(angc)], axis=1).astype(bf16)
    cos_t, sin_t = _rope_tables(l)

    rw32 = jnp.zeros((d, LANES), f32).at[:, :N_EXPERTS].set(router_w.astype(f32))
    rw_hi = rw32.astype(bf16)
    rw = jnp.concatenate([rw_hi, (rw32 - rw_hi.astype(f32)).astype(bf16)], axis=1)
    rb = jnp.zeros((1, LANES), f32).at[0, :N_EXPERTS].set(router_b)
    tpos = jnp.arange(TOKEN_TILE)
    tri = ((tpos[:, None] < tpos[None, :])
           & (tpos[:, None] // DISPATCH_TILE == tpos[None, :] // DISPATCH_TILE)).astype(bf16)
    row = lambda v: v.reshape(1, -1)

    q, k, v = _inproj(x, mods[0], row(norm_mix_g[0]), w_in_p[:, fw:], cos_t, sin_t)
    ck, cv = _ctxkv(ctx, mods[0], row(norm_mix_g[0]), w_in_p[:, fw + qw:], ctx_row)
    yf = _fourier(x, mods[0], row(norm_mix_g[0]), w_in_p[:, :fw], cs)
    att = _attention(q, k, v, ck, cv, sinkcol)
    x1, f, *routed = _outproj(yf, att, x, mods[0], w_out_p, row(norm_ffn_g[0]), rw, rb, tri)
    x2, u = _moe(f, routed, x1, mods[0], row(norm_mix_g[1]), moe_w_gate, moe_w_up, moe_w_down,
                 layer=0, glu=(mods[1], conv_pw1_w[0].astype(bf16), row(conv_pw1_b[0])))

    x3, f, *routed = _conv(u, x2, mods[1], conv_dw_w[0], row(conv_dw_b[0]), row(conv_ln_g[0]),
                           row(conv_ln_b[0]), conv_pw2_w[0].astype(bf16), row(conv_pw2_b[0]),
                           row(norm_ffn_g[1]), rw, rb, tri)
    (out,) = _moe(f, routed, x3, mods[1], row(final_norm_g), moe_w_gate, moe_w_up, moe_w_down,
                  layer=1)
    return out
```

```python
import functools
import math

import jax
import jax.numpy as jnp
from jax import lax
from jax.experimental import pallas as pl
from jax.experimental.pallas import tpu as pltpu

f32 = jnp.float32
bf16 = jnp.bfloat16
i32 = jnp.int32
u32 = jnp.uint32
HIGHEST = lax.Precision.HIGHEST

GRID_W = 64
HEAD_DIM = 64
N_HEADS = 8
N_KV_HEADS = 2
WINDOW = 128
ATT_BLOCK = 128
ROPE_THETA = 10000.0
FOURIER_GROUPS = 4
FOURIER_GROUP_W = 128
CONV_W = 31
N_EXPERTS = 16
N_GROUPS = 4
EXPERTS_PER_GROUP = 4
EXPERT_FF = 512
EPS = 1e-6
NEG_INF = -1e30
LOG2E = math.log2(math.e)

LANES = 128
SUBLANES = 8
COND_ROWS = 8
DFT_INNER = 64
TOKEN_TILE = 512
PROJ_TILE = 1024
ATT_TILE = 1024
EXPERT_ROWS = 512
DISPATCH_TILE = 256
DISPATCH_SUBTILES = 4
GLU_SUBTILES = 2
RUN_ALIGN = 8
LOCAL_ROWS = -(-(2 * DISPATCH_TILE + N_EXPERTS * (RUN_ALIGN - 1)) // LANES) * LANES
CONV_HALO = 16
CONV_ROWS = 128
VMEM_LIMIT = 56 * 1024 * 1024


def _cparams(sem, vmem=VMEM_LIMIT):
    return pltpu.CompilerParams(dimension_semantics=sem, vmem_limit_bytes=vmem)


def _adaln_kernel(cond_ref, w_ref, b_ref, o_ref):
    s = cond_ref[...]
    s = s * jax.nn.sigmoid(s)
    o_ref[0] = jnp.dot(s, w_ref[0], precision=HIGHEST, preferred_element_type=f32) + b_ref[0]


def _adaln(cond, ada_w, ada_b):
    depth, d, n = ada_w.shape
    tn = 1536
    return pl.pallas_call(
        _adaln_kernel,
        grid=(depth, n // tn),
        in_specs=[pl.BlockSpec((COND_ROWS, d), lambda i, j: (0, 0)),
                  pl.BlockSpec((1, d, tn), lambda i, j: (i, 0, j)),
                  pl.BlockSpec((1, 1, tn), lambda i, j: (i, 0, j))],
        out_specs=pl.BlockSpec((1, COND_ROWS, tn), lambda i, j: (i, 0, j)),
        out_shape=jax.ShapeDtypeStruct((depth, COND_ROWS, n), f32),
        compiler_params=_cparams(("arbitrary", "arbitrary")),
        name="adaln",
    )(cond, ada_w, ada_b.reshape(depth, 1, n))


def _norm_mod(x, g, shift, scale):
    ms = jnp.mean(x * x, axis=-1, keepdims=True)
    return (x * lax.rsqrt(ms + EPS)) * (g * (1.0 + scale)) + shift


def _rope(p, cos, sin_signed, first_half):
    rot = jnp.where(first_half, pltpu.roll(p, LANES - 16, axis=1), pltpu.roll(p, 16, axis=1))
    return p * cos + rot * sin_signed


def _inproj_kernel(x_ref, mod_ref, g_ref, w_ref, cos_ref, sin_ref, q_ref, k_ref, v_ref):
    h = _norm_mod(x_ref[0], g_ref[...], mod_ref[0, 0:1, :], mod_ref[0, 1:2, :])
    p = jnp.dot(h.astype(bf16), w_ref[...], preferred_element_type=f32)
    cos = cos_ref[...]
    sin = sin_ref[...]
    lane = lax.broadcasted_iota(i32, cos.shape, 1)
    first_half = (lane % 32) < 16
    qw = N_HEADS * HEAD_DIM
    for c in range(qw // LANES):
        qc = p[:, c * LANES:(c + 1) * LANES]
        q_ref[0, :, c * LANES:(c + 1) * LANES] = (
            _rope(qc, cos, sin, first_half) * (LOG2E * HEAD_DIM ** -0.5)).astype(bf16)
    k_ref[0] = _rope(p[:, qw:qw + LANES], cos, sin, first_half).astype(bf16)
    v_ref[0] = p[:, qw + LANES:].astype(bf16)


def _inproj(x, mod, g, w, cos_t, sin_t):
    b, l, d = x.shape
    tm = PROJ_TILE
    n = w.shape[1]
    return pl.pallas_call(
        _inproj_kernel,
        grid=(b, l // tm),
        in_specs=[pl.BlockSpec((1, tm, d), lambda i, j: (i, j, 0)),
                  pl.BlockSpec((1, 6, d), lambda i, j: (i, 0, 0)),
                  pl.BlockSpec((1, d), lambda i, j: (0, 0)),
                  pl.BlockSpec((d, n), lambda i, j: (0, 0)),
                  pl.BlockSpec((tm, LANES), lambda i, j: (j, 0)),
                  pl.BlockSpec((tm, LANES), lambda i, j: (j, 0))],
        out_specs=[pl.BlockSpec((1, tm, N_HEADS * HEAD_DIM), lambda i, j: (i, j, 0)),
                   pl.BlockSpec((1, tm, LANES), lambda i, j: (i, j, 0)),
                   pl.BlockSpec((1, tm, LANES), lambda i, j: (i, j, 0))],
        out_shape=[jax.ShapeDtypeStruct((b, l, N_HEADS * HEAD_DIM), bf16),
                   jax.ShapeDtypeStruct((b, l, LANES), bf16),
                   jax.ShapeDtypeStruct((b, l, LANES), bf16)],
        compiler_params=_cparams(("parallel", "parallel")),
        name="inproj",
    )(x, mod, g, w, cos_t, sin_t)


def _ctxkv_kernel(x_ref, mod_ref, g_ref, w_ref, k_ref, v_ref):
    h = _norm_mod(x_ref[0], g_ref[...], mod_ref[0, 0:1, :], mod_ref[0, 1:2, :])
    p = jnp.dot(h.astype(bf16), w_ref[...], preferred_element_type=f32)
    k_ref[0] = p[:, :LANES].astype(bf16)
    v_ref[0] = p[:, LANES:].astype(bf16)


def _ctxkv(ctx, mod, g, w_kv, ctx_row):
    b, c, d = ctx.shape
    return pl.pallas_call(
        _ctxkv_kernel,
        grid=(b,),
        in_specs=[pl.BlockSpec((1, c, d), lambda i: (i, 0, 0)),
                  pl.BlockSpec((1, 6, d), lambda i: (ctx_row, 0, 0)),
                  pl.BlockSpec((1, d), lambda i: (0, 0)),
                  pl.BlockSpec((d, 2 * LANES), lambda i: (0, 0))],
        out_specs=[pl.BlockSpec((1, c, LANES), lambda i: (i, 0, 0)),
                   pl.BlockSpec((1, c, LANES), lambda i: (i, 0, 0))],
        out_shape=[jax.ShapeDtypeStruct((b, c, LANES), bf16),
                   jax.ShapeDtypeStruct((b, c, LANES), bf16)],
        compiler_params=_cparams(("parallel",)),
        name="ctxkv",
    )(ctx, mod, g, w_kv)


def _attn_kernel(seq_len, q_ref, kp_ref, km_ref, kn_ref, vp_ref, vm_ref, vn_ref,
                 ck_ref, cv_ref, sink_ref, o_ref, kext, vext, cvext):
    j = pl.program_id(1)
    tq = ATT_TILE
    blk = ATT_BLOCK
    kext[0:blk] = kp_ref[0]
    kext[blk:blk + tq] = km_ref[0]
    kext[blk + tq:] = kn_ref[0]
    vext[:, LANES:] = jnp.ones((tq + 2 * blk, LANES), bf16)
    vext[0:blk, :LANES] = vp_ref[0]
    vext[blk:blk + tq, :LANES] = vm_ref[0]
    vext[blk + tq:, :LANES] = vn_ref[0]
    cvext[:, LANES:] = jnp.ones((cvext.shape[0], LANES), bf16)
    cvext[:, :LANES] = cv_ref[0]
    ck = ck_ref[0]
    sink = sink_ref[...]
    n_chunks = (N_HEADS * HEAD_DIM) // LANES
    rows = 2 * n_chunks * blk
    lane = lax.broadcasted_iota(i32, (blk, LANES), 1)
    low = lane < HEAD_DIM
    qi = lax.broadcasted_iota(i32, (rows, 3 * blk), 0) % blk
    pk = lax.broadcasted_iota(i32, (rows, 3 * blk), 1)
    band_bias = jnp.where(jnp.abs(pk - blk - qi) <= WINDOW, 0.0, NEG_INF).astype(f32)
    pcol = lax.broadcasted_iota(i32, (1, 3 * blk), 1)
    nt = (((1,), (1,)), ((), ()))

    def sub(s, carry):
        r0 = pl.multiple_of(s * blk, blk)
        qs = q_ref[0, pl.ds(r0, blk), :]
        parts = []
        for c in range(n_chunks):
            qc = qs[:, c * LANES:(c + 1) * LANES]
            parts.append(jnp.where(low, qc, jnp.zeros_like(qc)))
            parts.append(jnp.where(low, jnp.zeros_like(qc), qc))
        lhs = jnp.concatenate(parts, axis=0)
        kl = kext[pl.ds(r0, 3 * blk), :]
        vl = vext[pl.ds(r0, 3 * blk), :]
        kpos = j * tq + r0 - blk + pcol
        col_bias = jnp.where((kpos >= 0) & (kpos < seq_len), 0.0, NEG_INF).astype(f32)
        s_c = lax.dot_general(lhs, ck, nt, preferred_element_type=f32)
        s_l = lax.dot_general(lhs, kl, nt, preferred_element_type=f32) + band_bias + col_bias
        blocks = ([s_c[:, i:i + LANES] for i in range(0, s_c.shape[1], LANES)]
                  + [s_l[:, i:i + LANES] for i in range(0, s_l.shape[1], LANES)])
        folded = functools.reduce(jnp.maximum, blocks)
        m = jnp.maximum(jnp.max(folded, axis=1, keepdims=True), sink)
        e_c = jnp.exp2(s_c - m).astype(bf16)
        e_l = jnp.exp2(s_l - m).astype(bf16)
        ov = (jnp.dot(e_c, cvext[...], preferred_element_type=f32)
              + jnp.dot(e_l, vl, preferred_element_type=f32))
        den = ov[:, LANES:] + jnp.exp2(sink - m)
        o = ov[:, :LANES] / den
        for c in range(n_chunks):
            oc = jnp.where(low, o[(2 * c) * blk:(2 * c + 1) * blk],
                           o[(2 * c + 1) * blk:(2 * c + 2) * blk])
            o_ref[0, pl.ds(r0, blk), c * LANES:(c + 1) * LANES] = oc.astype(bf16)
        return carry

    lax.fori_loop(0, tq // blk, sub, 0, unroll=2)


def _attention(q, k, v, ck, cv, sinkcol):
    b, l, qw = q.shape
    c = ck.shape[1]
    tq = ATT_TILE
    r = tq // ATT_BLOCK
    nb = l // ATT_BLOCK
    prev = pl.BlockSpec((1, ATT_BLOCK, LANES), lambda i, j: (i, jnp.maximum(j * r - 1, 0), 0))
    main = pl.BlockSpec((1, tq, LANES), lambda i, j: (i, j, 0))
    nxt = pl.BlockSpec((1, ATT_BLOCK, LANES), lambda i, j: (i, jnp.minimum(j * r + r, nb - 1), 0))
    cspec = pl.BlockSpec((1, c, LANES), lambda i, j: (i, 0, 0))
    return pl.pallas_call(
        functools.partial(_attn_kernel, l),
        grid=(b, l // tq),
        in_specs=[pl.BlockSpec((1, tq, qw), lambda i, j: (i, j, 0)),
                  prev, main, nxt, prev, main, nxt, cspec, cspec,
                  pl.BlockSpec(sinkcol.shape, lambda i, j: (0, 0))],
        out_specs=pl.BlockSpec((1, tq, qw), lambda i, j: (i, j, 0)),
        out_shape=jax.ShapeDtypeStruct((b, l, qw), bf16),
        scratch_shapes=[pltpu.VMEM((tq + 2 * ATT_BLOCK, LANES), bf16),
                        pltpu.VMEM((tq + 2 * ATT_BLOCK, 2 * LANES), bf16),
                        pltpu.VMEM((c, 2 * LANES), bf16)],
        compiler_params=_cparams(("parallel", "parallel")),
        name="attention",
    )(q, k, k, k, v, v, v, ck, cv, sinkcol)


def _pack_pair(lo, hi):
    lo = lax.bitcast_convert_type(lo.astype(bf16).astype(f32), u32)
    hi = lax.bitcast_convert_type(hi.astype(bf16).astype(f32), u32)
    return (lo >> 16) | (hi & jnp.uint32(0xFFFF0000))


def _fourier1_kernel(x_ref, mod_ref, g_ref, w_ref, cs_ref, m_ref, ct_ref, st_ref, z_ref, ab_ref):
    n1 = x_ref.shape[1]
    nt = x_ref.shape[2]
    x = x_ref[0].reshape(n1 * nt, x_ref.shape[3])
    h = _norm_mod(x, g_ref[...], mod_ref[0, 0:1, :], mod_ref[0, 1:2, :]).astype(bf16)
    p = jnp.dot(h, w_ref[...], preferred_element_type=f32)
    for g in range(FOURIER_GROUPS):
        ug = p[:, g * LANES:(g + 1) * LANES].astype(bf16)
        ab = jnp.dot(ug, cs_ref[...], preferred_element_type=f32)
        ab_ref[0] = ab[:, :LANES]
        ab_ref[1] = ab[:, LANES:]
        for t in range(nt):
            stack = jnp.concatenate([ab_ref[0, pl.ds(t, n1, stride=nt), :],
                                     ab_ref[1, pl.ds(t, n1, stride=nt), :]], axis=0).astype(bf16)
            z = jnp.dot(m_ref[...], stack, preferred_element_type=f32)
            zr, zn = z[:n1], z[n1:]
            ct, st = ct_ref[t], st_ref[t]
            z_ref[0, g, t] = _pack_pair(ct * zr - st * zn, ct * zn + st * zr)


def _fourier2_kernel(scale, z_ref, m_ref, o_ref, zbuf, ybuf):
    _, grp, n2, tk, w = z_ref.shape
    for g in range(grp):
        zbuf[...] = z_ref[0, g].reshape(n2 * tk, w)
        for j in range(tk):
            zp = zbuf[pl.ds(j, n2, stride=tk), :]
            zr = lax.bitcast_convert_type(zp << 16, f32).astype(bf16)
            zn = lax.bitcast_convert_type(zp & jnp.uint32(0xFFFF0000), f32).astype(bf16)
            y = jnp.dot(m_ref[...], jnp.concatenate([zr, zn], axis=0), preferred_element_type=f32)
            ybuf[pl.ds(j, n2, stride=tk), :] = y * scale
        o_ref[0, g] = ybuf[...].reshape(n2, tk, w)


def _fourier(x, mod, g, w_f, cs):
    b, l, d = x.shape
    n2 = DFT_INNER
    n1 = l // n2
    grp, w = FOURIER_GROUPS, FOURIER_GROUP_W
    t2 = SUBLANES
    k1 = jnp.arange(n1, dtype=i32)
    ang1 = ((k1[:, None] * k1[None, :]) % n1).astype(f32) * (2.0 * math.pi / n1)
    c1, s1 = jnp.cos(ang1), jnp.sin(ang1)
    m1 = jnp.concatenate([jnp.concatenate([c1, -s1], axis=1),
                          jnp.concatenate([s1, c1], axis=1)], axis=0).astype(bf16)
    l2 = jnp.arange(n2, dtype=i32)
    angt = ((l2[:, None] * k1[None, :]) % l).astype(f32) * (2.0 * math.pi / l)
    ct = jnp.broadcast_to(jnp.cos(angt)[:, :, None], (n2, n1, w))
    st = jnp.broadcast_to(jnp.sin(angt)[:, :, None], (n2, n1, w))
    ang2 = ((l2[:, None] * l2[None, :]) % n2).astype(f32) * (2.0 * math.pi / n2)
    m2 = jnp.concatenate([jnp.cos(ang2), -jnp.sin(ang2)], axis=1).astype(bf16)

    tspec = pl.BlockSpec((t2, n1, w), lambda t, i: (t, 0, 0))
    z = pl.pallas_call(
        _fourier1_kernel,
        grid=(n2 // t2, b),
        in_specs=[pl.BlockSpec((1, n1, t2, d), lambda t, i: (i, 0, t, 0)),
                  pl.BlockSpec((1, 6, d), lambda t, i: (i, 0, 0)),
                  pl.BlockSpec((1, d), lambda t, i: (0, 0)),
                  pl.BlockSpec(w_f.shape, lambda t, i: (0, 0)),
                  pl.BlockSpec(cs.shape, lambda t, i: (0, 0)),
                  pl.BlockSpec(m1.shape, lambda t, i: (0, 0)), tspec, tspec],
        out_specs=pl.BlockSpec((1, grp, t2, n1, w), lambda t, i: (i, 0, t, 0, 0)),
        out_shape=jax.ShapeDtypeStruct((b, grp, n2, n1, w), u32),
        scratch_shapes=[pltpu.VMEM((2, n1 * t2, w), f32)],
        compiler_params=_cparams(("parallel", "parallel")),
        name="fourier_outer",
    )(x.reshape(b, n1, n2, d), mod, g, w_f, cs, m1, ct, st)

    tk = SUBLANES
    y = pl.pallas_call(
        functools.partial(_fourier2_kernel, 1.0 / math.sqrt(l * w)),
        grid=(b, n1 // tk),
        in_specs=[pl.BlockSpec((1, grp, n2, tk, w), lambda i, t: (i, 0, 0, t, 0)),
                  pl.BlockSpec(m2.shape, lambda i, t: (0, 0))],
        out_specs=pl.BlockSpec((1, grp, n2, tk, w), lambda i, t: (i, 0, 0, t, 0)),
        out_shape=jax.ShapeDtypeStruct((b, grp, n2, n1, w), f32),
        scratch_shapes=[pltpu.VMEM((n2 * tk, w), u32), pltpu.VMEM((n2 * tk, w), f32)],
        compiler_params=_cparams(("parallel", "parallel")),
        name="fourier_inner",
    )(z, m2)
    return y.reshape(b, grp, l, w)


def _first_max4(a):
    m = jnp.maximum(jnp.maximum(a[0], a[1]), jnp.maximum(a[2], a[3]))
    idx = jnp.where(a[0] == m, 0, jnp.where(a[1] == m, 1, jnp.where(a[2] == m, 2, 3)))
    return m, idx


def _pick4(vals, idx):
    return jnp.where(idx == 0, vals[0], jnp.where(idx == 1, vals[1],
                                                   jnp.where(idx == 2, vals[2], vals[3])))


def _route(f, rw_ref, rb_ref, tri_ref, base_ref, first_step, ri_ref, wc_ref, cnt_ref, meta_ref):
    tm = f.shape[0]
    f_hi = f.astype(bf16)
    f_lo = (f - f_hi.astype(f32)).astype(bf16)
    rw2 = rw_ref[...]
    part = jnp.dot(f_hi, rw2, preferred_element_type=f32)
    logits = (part[:, :LANES] + part[:, LANES:]
              + jnp.dot(f_lo, rw2[:, :LANES], preferred_element_type=f32))
    sc = jax.nn.sigmoid(logits)
    st = sc.T
    bt = (sc + rb_ref[...]).T
    neg = jnp.full((1, tm), -jnp.inf, f32)
    gs = []
    for g in range(N_GROUPS):
        a = [bt[4 * g + i: 4 * g + i + 1] for i in range(4)]
        m1, i1 = _first_max4(a)
        rest = [jnp.where(i1 == i, neg, a[i]) for i in range(4)]
        m2, _ = _first_max4(rest)
        gs.append(m1 + m2)
    _, gsel = _first_max4(gs)
    a = [_pick4([bt[4 * g + i: 4 * g + i + 1] for g in range(N_GROUPS)], gsel) for i in range(4)]
    s = [_pick4([st[4 * g + i: 4 * g + i + 1] for g in range(N_GROUPS)], gsel) for i in range(4)]
    _, i1 = _first_max4(a)
    rest = [jnp.where(i1 == i, neg, a[i]) for i in range(4)]
    _, i2 = _first_max4(rest)
    w1 = _pick4(s, i1)
    w2 = _pick4(s, i2)
    tot = w1 + w2
    w1 = w1 / tot
    w2 = w2 / tot
    e0 = gsel * EXPERTS_PER_GROUP + i1
    e1 = gsel * EXPERTS_PER_GROUP + i2

    @pl.when(first_step)
    def _():
        base_ref[...] = jnp.zeros_like(base_ref)

    td = DISPATCH_TILE
    eid = lax.broadcasted_iota(i32, (N_EXPERTS, tm), 0)
    oh0 = (eid == e0).astype(f32)
    oh1 = (eid == e1).astype(f32)
    oh = oh0 + oh1
    before = jnp.dot(oh.astype(bf16), tri_ref[...], preferred_element_type=f32)
    lane_tile = lax.broadcasted_iota(i32, (N_EXPERTS, tm), 1) // td
    ei = lax.broadcasted_iota(i32, (N_EXPERTS, N_EXPERTS), 0)
    ej = lax.broadcasted_iota(i32, (N_EXPERTS, N_EXPERTS), 1)
    strict_lower = (ej < ei).astype(f32)
    run_start = jnp.zeros((N_EXPERTS, tm), f32)
    goff = base_ref[...]
    for s in range(tm // td):
        cnt_s = jnp.sum(oh[:, s * td:(s + 1) * td], axis=1, keepdims=True)
        pad_s = jnp.floor((cnt_s + 7.0) * 0.125) * 8.0
        pad_b = jnp.broadcast_to(pad_s, (N_EXPERTS, LANES))
        start_b = jnp.dot(strict_lower, pad_b, precision=HIGHEST, preferred_element_type=f32)
        run_start = jnp.where(lane_tile == s, start_b[:, 0:1], run_start)
        meta_ref[s, 0] = start_b.astype(i32)
        meta_ref[s, 1] = pad_b.astype(i32)
        meta_ref[s, 2] = goff.astype(i32)
        goff = goff + pad_b
    base_ref[...] = goff
    cnt_ref[...] = goff
    pos = before + run_start
    lp0 = jnp.sum(oh0 * pos, axis=0, keepdims=True)
    lp1 = jnp.sum(oh1 * pos, axis=0, keepdims=True)
    zi = jnp.zeros((1, tm), i32)
    ri_ref[...] = jnp.concatenate(
        [lp0.astype(i32), lp1.astype(i32), e0, e1, zi, zi, zi, zi], axis=0)
    zf = jnp.zeros((LANES - 4, tm), f32)
    wc_ref[...] = jnp.concatenate([w1, w2, lp0, lp1, zf], axis=0).T


def _outproj_kernel(yf_ref, o_ref, x_ref, mod_ref, w_ref, g_ref, rw_ref, rb_ref, tri_ref,
                    x1_ref, f_ref, ri_ref, wc_ref, cnt_ref, meta_ref, base_ref):
    mix = jnp.concatenate([yf_ref[0, g].astype(bf16) for g in range(FOURIER_GROUPS)] + [o_ref[0]],
                          axis=1)
    y = jnp.dot(mix, w_ref[...], preferred_element_type=f32)
    x1 = x_ref[0] + mod_ref[0, 2:3, :] * y
    x1_ref[0] = x1
    f = _norm_mod(x1, g_ref[...], mod_ref[0, 3:4, :], mod_ref[0, 4:5, :])
    f_ref[0] = f.astype(bf16)
    first = (pl.program_id(0) == 0) & (pl.program_id(1) == 0)
    _route(f, rw_ref, rb_ref, tri_ref, base_ref, first, ri_ref, wc_ref, cnt_ref, meta_ref)


def _before_in_tile(tm):
    tpos = jnp.arange(tm)
    return ((tpos[:, None] < tpos[None, :])
            & (tpos[:, None] // DISPATCH_TILE == tpos[None, :] // DISPATCH_TILE)).astype(bf16)


def _route_specs(b, l, tm):
    nl = l // tm
    rw = lambda d: pl.BlockSpec((d, 2 * LANES), lambda i, j: (0, 0))
    rb = pl.BlockSpec((1, LANES), lambda i, j: (0, 0))
    tri = pl.BlockSpec((tm, tm), lambda i, j: (0, 0))
    ns = tm // DISPATCH_TILE
    out_specs = [pl.BlockSpec((8, tm), lambda i, j: (0, i * nl + j)),
                 pl.BlockSpec((tm, LANES), lambda i, j: (i * nl + j, 0)),
                 pl.BlockSpec((N_EXPERTS, LANES), lambda i, j: (0, 0)),
                 pl.BlockSpec((ns, 3, N_EXPERTS, LANES), lambda i, j: (i * nl + j, 0, 0, 0))]
    out_shape = [jax.ShapeDtypeStruct((8, b * l), i32),
                 jax.ShapeDtypeStruct((b * l, LANES), f32),
                 jax.ShapeDtypeStruct((N_EXPERTS, LANES), f32),
                 jax.ShapeDtypeStruct((b * l // DISPATCH_TILE, 3, N_EXPERTS, LANES), i32)]
    return rw, rb, tri, out_specs, out_shape


def _outproj(yf, o, x, mod, w, g, rw, rb):
    b, l, d = x.shape
    tm = PROJ_TILE
    tri = _before_in_tile(tm)
    rws, rbs, tris, r_specs, r_shapes = _route_specs(b, l, tm)
    row = pl.BlockSpec((1, tm, d), lambda i, j: (i, j, 0))
    return pl.pallas_call(
        _outproj_kernel,
        grid=(b, l // tm),
        in_specs=[pl.BlockSpec((1, FOURIER_GROUPS, tm, LANES), lambda i, j: (i, 0, j, 0)),
                  pl.BlockSpec((1, tm, o.shape[2]), lambda i, j: (i, j, 0)),
                  row,
                  pl.BlockSpec((1, 6, d), lambda i, j: (i, 0, 0)),
                  pl.BlockSpec(w.shape, lambda i, j: (0, 0)),
                  pl.BlockSpec((1, d), lambda i, j: (0, 0)),
                  rws(d), rbs, tris],
        out_specs=[row, row] + r_specs,
        out_shape=[jax.ShapeDtypeStruct((b, l, d), f32),
                   jax.ShapeDtypeStruct((b, l, d), bf16)] + r_shapes,
        scratch_shapes=[pltpu.VMEM((N_EXPERTS, LANES), f32)],
        compiler_params=_cparams(("arbitrary", "arbitrary")),
        name="outproj_router",
    )(yf, o, x, mod, w, g, rw, rb, tri)


def _conv_kernel(seq_len, up_ref, um_ref, un_ref, x_ref, mod_ref, dw_ref, db_ref, lg_ref, lb_ref,
                 w_ref, pb_ref, g_ref, rw_ref, rb_ref, tri_ref,
                 x1_ref, f_ref, ri_ref, wc_ref, cnt_ref, meta_ref, base_ref, ext, conv_out):
    j = pl.program_id(1)
    tm = um_ref.shape[1]
    hl = CONV_HALO
    half = CONV_W // 2
    ext[0:hl] = jnp.where(j > 0, up_ref[0], jnp.zeros_like(up_ref[0]))
    ext[hl:hl + tm] = um_ref[0]
    ext[hl + tm:] = jnp.where((j + 1) * tm < seq_len, un_ref[0], jnp.zeros_like(un_ref[0]))
    base = hl - half
    span = (CONV_W - 1) // SUBLANES * SUBLANES
    rows = CONV_ROWS

    def lane_chunk(c, carry):
        lanes = pl.ds(pl.multiple_of(c * LANES, LANES), LANES)
        for r in range(0, tm, rows):
            part = jnp.broadcast_to(db_ref[:, lanes], (rows, LANES))
            for phase in range(SUBLANES):
                win = ext[base + phase + r: base + phase + r + rows + span, lanes]
                same = None
                for t in range(phase, CONV_W, SUBLANES):
                    term = win[t - phase: t - phase + rows, :] * dw_ref[t:t + 1, lanes]
                    same = term if same is None else same + term
                part = part + same
            conv_out[r:r + rows, lanes] = part
        return carry

    lax.fori_loop(0, um_ref.shape[2] // LANES, lane_chunk, 0)
    acc = conv_out[...]
    mu = jnp.mean(acc, axis=-1, keepdims=True)
    cen = acc - mu
    var = jnp.mean(cen * cen, axis=-1, keepdims=True)
    ln = cen * lax.rsqrt(var + EPS) * lg_ref[...] + lb_ref[...]
    act = ln * jax.nn.sigmoid(ln)
    y = jnp.dot(act.astype(bf16), w_ref[...], preferred_element_type=f32) + pb_ref[...]
    x1 = x_ref[0] + mod_ref[0, 2:3, :] * y
    x1_ref[0] = x1
    f = _norm_mod(x1, g_ref[...], mod_ref[0, 3:4, :], mod_ref[0, 4:5, :])
    f_ref[0] = f.astype(bf16)
    first = (pl.program_id(0) == 0) & (j == 0)
    _route(f, rw_ref, rb_ref, tri_ref, base_ref, first, ri_ref, wc_ref, cnt_ref, meta_ref)


def _conv(u, x, mod, dw_w, dw_b, ln_g, ln_b, pw2_w, pw2_b, g, rw, rb):
    b, l, d = x.shape
    tm = TOKEN_TILE
    tri = _before_in_tile(tm)
    hl = CONV_HALO
    r = tm // hl
    nh = l // hl
    rws, rbs, tris, r_specs, r_shapes = _route_specs(b, l, tm)
    row = pl.BlockSpec((1, tm, d), lambda i, j: (i, j, 0))
    vec = pl.BlockSpec((1, d), lambda i, j: (0, 0))
    return pl.pallas_call(
        functools.partial(_conv_kernel, l),
        grid=(b, l // tm),
        in_specs=[pl.BlockSpec((1, hl, d), lambda i, j: (i, jnp.maximum(j * r - 1, 0), 0)),
                  row,
                  pl.BlockSpec((1, hl, d), lambda i, j: (i, jnp.minimum(j * r + r, nh - 1), 0)),
                  row,
                  pl.BlockSpec((1, 6, d), lambda i, j: (i, 0, 0)),
                  pl.BlockSpec(dw_w.shape, lambda i, j: (0, 0)),
                  vec, vec, vec,
                  pl.BlockSpec(pw2_w.shape, lambda i, j: (0, 0)),
                  vec, vec, rws(d), rbs, tris],
        out_specs=[row, row] + r_specs,
        out_shape=[jax.ShapeDtypeStruct((b, l, d), f32),
                   jax.ShapeDtypeStruct((b, l, d), bf16)] + r_shapes,
        scratch_shapes=[pltpu.VMEM((N_EXPERTS, LANES), f32),
                        pltpu.VMEM((tm + 2 * hl, d), f32),
                        pltpu.VMEM((tm, d), f32)],
        compiler_params=_cparams(("arbitrary", "arbitrary")),
        name="conv_router",
    )(u, u, u, x, mod, dw_w, dw_b, ln_g, ln_b, pw2_w, pw2_b, g, rw, rb, tri)


def _pack_bf16_pairs(x):
    h = x.shape[1] // 2
    lo = lax.bitcast_convert_type(x[:, :h], u32)
    hi = lax.bitcast_convert_type(x[:, h:], u32)
    return (lo >> 16) | (hi & jnp.uint32(0xFFFF0000))


def _unpack_bf16_pairs(u):
    lo = lax.bitcast_convert_type(u << 16, f32)
    hi = lax.bitcast_convert_type(u & jnp.uint32(0xFFFF0000), f32)
    return jnp.concatenate([lo, hi], axis=1).astype(bf16)


def _run_copies(meta, tile, local_ref, hbm_ref, sem, to_hbm):
    start_ref, size_ref, dst_ref = meta
    for e in range(N_EXPERTS):
        k = tile * N_EXPERTS + e
        size = pl.multiple_of(size_ref[k], RUN_ALIGN)

        @pl.when(size > 0)
        def _():
            loc = local_ref.at[pl.ds(pl.multiple_of(start_ref[k], RUN_ALIGN), size)]
            glob = hbm_ref.at[pl.ds(pl.multiple_of(dst_ref[k], RUN_ALIGN), size)]
            if to_hbm:
                pltpu.make_async_copy(loc, glob, sem).start()
            else:
                pltpu.make_async_copy(glob, loc, sem).start()


def _wait_rows(rows, local_ref, hbm_ref, sem):
    rows = pl.multiple_of(rows, RUN_ALIGN)

    @pl.when(rows > 0)
    def _():
        pltpu.make_async_copy(local_ref.at[pl.ds(0, rows)], hbm_ref.at[pl.ds(0, rows)], sem).wait()


def _dispatch_kernel(start_ref, size_ref, dst_ref, tot_ref, tail_start_ref, tail_size_ref, nv_ref,
                     f_ref, lp_ref, xs_ref, loc, zbuf, sem, zsem):
    i = pl.program_id(0)
    n = pl.num_programs(0)
    slot = i % 2
    subs = loc.shape[1]
    rows = loc.shape[2]
    td = f_ref.shape[0] // subs
    meta = (start_ref, size_ref, dst_ref)

    def drain(step, which):
        for s in range(subs):
            _wait_rows(tot_ref[step * subs + s], loc.at[which, s], xs_ref, sem.at[which])

    @pl.when(i >= 2)
    def _():
        drain(i - 2, slot)

    r = lax.broadcasted_iota(i32, (rows, td), 0)
    for s in range(subs):
        cols = slice(s * td, (s + 1) * td)
        onehot = ((r == lp_ref[0:1, cols]) | (r == lp_ref[1:2, cols])).astype(bf16)
        sorted_rows = jnp.dot(onehot, f_ref[cols, :], preferred_element_type=f32)
        loc[slot, s] = _pack_bf16_pairs(sorted_rows)
        _run_copies(meta, i * subs + s, loc.at[slot, s], xs_ref, sem.at[slot], to_hbm=True)

    @pl.when(i == n - 1)
    def _():
        zbuf[...] = jnp.zeros_like(zbuf)
        total = 0
        for e in range(N_EXPERTS):
            size = pl.multiple_of(tail_size_ref[e], RUN_ALIGN)
            total = total + size

            @pl.when(size > 0)
            def _():
                pltpu.make_async_copy(
                    zbuf.at[pl.ds(0, size)],
                    xs_ref.at[pl.ds(pl.multiple_of(tail_start_ref[e], RUN_ALIGN), size)], zsem).start()

        _wait_rows(total, zbuf, xs_ref, zsem)

        def zero_block(k, c):
            pltpu.make_async_copy(zbuf, xs_ref.at[pl.ds(pl.multiple_of(k * zbuf.shape[0], RUN_ALIGN),
                                                        zbuf.shape[0])], zsem).start()
            return c

        def wait_block(k, c):
            pltpu.make_async_copy(zbuf, xs_ref.at[pl.ds(0, zbuf.shape[0])], zsem).wait()
            return c

        n_blocks = xs_ref.shape[0] // zbuf.shape[0]
        lax.fori_loop(nv_ref[0], n_blocks, zero_block, 0)
        lax.fori_loop(nv_ref[0], n_blocks, wait_block, 0)
        drain(i, slot)

        @pl.when(i >= 1)
        def _():
            drain(i - 1, 1 - slot)


def _dispatch(tables, f2, ri, n_slots):
    t, d = f2.shape
    subs = DISPATCH_SUBTILES
    tm = DISPATCH_TILE * subs
    return pl.pallas_call(
        _dispatch_kernel,
        grid_spec=pltpu.PrefetchScalarGridSpec(
            num_scalar_prefetch=7,
            grid=(t // tm,),
            in_specs=[pl.BlockSpec((tm, d), lambda i, *_: (i, 0)),
                      pl.BlockSpec((8, tm), lambda i, *_: (0, i))],
            out_specs=pl.BlockSpec(memory_space=pl.ANY),
            scratch_shapes=[pltpu.VMEM((2, subs, LOCAL_ROWS, d // 2), u32),
                            pltpu.VMEM((EXPERT_ROWS, d // 2), u32),
                            pltpu.SemaphoreType.DMA((2,)), pltpu.SemaphoreType.DMA(())]),
        out_shape=jax.ShapeDtypeStruct((n_slots, d // 2), u32),
        compiler_params=_cparams(("arbitrary",)),
        name="moe_dispatch",
    )(*tables, f2, ri)


def _expert_kernel(be_ref, nv_ref, x_ref, wg_ref, wu_ref, wd_ref, y_ref, wgb, wub, wdb):
    i = pl.program_id(0)
    changed = jnp.logical_or(i == 0, be_ref[i] != be_ref[jnp.maximum(i - 1, 0)])

    @pl.when(changed)
    def _():
        wgb[...] = wg_ref[0, 0].astype(bf16)
        wub[...] = wu_ref[0, 0].astype(bf16)
        wdb[...] = wd_ref[0, 0].astype(bf16)

    @pl.when(i < nv_ref[0])
    def _():
        xb = _unpack_bf16_pairs(x_ref[...])
        gate = jnp.dot(xb, wgb[...], preferred_element_type=f32)
        up = jnp.dot(xb, wub[...], preferred_element_type=f32)
        hid = (gate * jax.nn.sigmoid(gate) * up).astype(bf16)
        y = jnp.dot(hid, wdb[...], preferred_element_type=f32)
        y_ref[...] = _pack_bf16_pairs(y.astype(bf16).astype(f32))

    @pl.when(i >= nv_ref[0])
    def _():
        y_ref[...] = jnp.zeros_like(y_ref)


def _experts(block_e, n_valid, xs, w_gate, w_up, w_down, layer):
    ns, dh = xs.shape
    tb = EXPERT_ROWS
    d, ff = w_gate.shape[2:]
    xmap = lambda i, be, nv: (jnp.maximum(jnp.minimum(i, nv[0] - 1), 0), 0)
    wmap = lambda i, be, nv: (layer, be[i], 0, 0)
    return pl.pallas_call(
        _expert_kernel,
        grid_spec=pltpu.PrefetchScalarGridSpec(
            num_scalar_prefetch=2,
            grid=(ns // tb,),
            in_specs=[pl.BlockSpec((tb, dh), xmap),
                      pl.BlockSpec((1, 1, d, ff), wmap),
                      pl.BlockSpec((1, 1, d, ff), wmap),
                      pl.BlockSpec((1, 1, ff, d), wmap)],
            out_specs=pl.BlockSpec((tb, dh), lambda i, be, nv: (i, 0)),
            scratch_shapes=[pltpu.VMEM((d, ff), bf16), pltpu.VMEM((d, ff), bf16),
                            pltpu.VMEM((ff, d), bf16)]),
        out_shape=jax.ShapeDtypeStruct((ns, dh), u32),
        compiler_params=_cparams(("arbitrary",)),
        name="moe_experts",
    )(block_e, n_valid, xs, w_gate, w_up, w_down)


def _combine_kernel(final, start_ref, size_ref, dst_ref, tot_ref, ys_ref, wc_ref, x_ref, mod_ref,
                    g_ref, *rest):
    if final:
        o_ref, loc, sem = rest
    else:
        nmod_ref, w_ref, b_ref, o_ref, u_ref, loc, sem = rest
    i = pl.program_id(0)
    n = pl.num_programs(0)
    slot = i % 2
    subs = loc.shape[1]
    rows = loc.shape[2]
    td = x_ref.shape[0] // subs
    meta = (start_ref, size_ref, dst_ref)

    def fetch(step, which):
        for s in range(subs):
            _run_copies(meta, step * subs + s, loc.at[which, s], ys_ref, sem.at[which, s],
                        to_hbm=False)

    @pl.when(i == 0)
    def _():
        loc[...] = jnp.zeros_like(loc)
        fetch(i, slot)

    @pl.when(i + 1 < n)
    def _():
        fetch(i + 1, 1 - slot)

    c = lax.broadcasted_iota(i32, (td, rows), 1)
    parts = []
    for s in range(subs):
        _wait_rows(tot_ref[i * subs + s], loc.at[slot, s], ys_ref, sem.at[slot, s])
        wc = wc_ref[s * td:(s + 1) * td, :]
        sel = jnp.concatenate([(c == wc[:, 2:3].astype(i32)).astype(bf16),
                               (c == wc[:, 3:4].astype(i32)).astype(bf16)], axis=0)
        picked = jnp.dot(sel, _unpack_bf16_pairs(loc[slot, s]), preferred_element_type=f32)
        parts.append(wc[:, 0:1] * picked[:td] + wc[:, 1:2] * picked[td:])
    xo = x_ref[...] + mod_ref[0, 5:6, :] * jnp.concatenate(parts, axis=0)
    if final:
        ms = jnp.mean(xo * xo, axis=-1, keepdims=True)
        o_ref[...] = xo * lax.rsqrt(ms + EPS) * g_ref[...]
    else:
        o_ref[...] = xo
        h = _norm_mod(xo, g_ref[...], nmod_ref[0, 0:1, :], nmod_ref[0, 1:2, :])
        p = jnp.dot(h.astype(bf16), w_ref[...], preferred_element_type=f32) + b_ref[...]
        ch = p.shape[1] // 2
        u_ref[...] = p[:, :ch] * jax.nn.sigmoid(p[:, ch:])


def _combine(tables, ys, wc, x, mod, g, glu=None):
    b, l, d = x.shape
    subs = DISPATCH_SUBTILES if glu is None else GLU_SUBTILES
    tm = DISPATCH_TILE * subs
    per_batch = l // tm
    const = lambda i, *_: (0, 0)
    tile = lambda cols: pl.BlockSpec((tm, cols), lambda i, *_: (i, 0))
    mod_spec = pl.BlockSpec((1, 6, d), lambda i, *_: (i // per_batch, 0, 0))
    in_specs = [pl.BlockSpec(memory_space=pl.ANY), tile(LANES), tile(d), mod_spec,
                pl.BlockSpec((1, d), const)]
    args = [ys, wc, x.reshape(b * l, d), mod, g]
    out_specs = [tile(d)]
    out_shape = [jax.ShapeDtypeStruct((b * l, d), f32)]
    if glu is not None:
        nmod, w, bias = glu
        in_specs += [mod_spec, pl.BlockSpec(w.shape, const), pl.BlockSpec(bias.shape, const)]
        args += [nmod, w, bias]
        out_specs.append(tile(w.shape[1] // 2))
        out_shape.append(jax.ShapeDtypeStruct((b * l, w.shape[1] // 2), f32))
    outs = pl.pallas_call(
        functools.partial(_combine_kernel, glu is None),
        grid_spec=pltpu.PrefetchScalarGridSpec(
            num_scalar_prefetch=4,
            grid=(b * per_batch,),
            in_specs=in_specs,
            out_specs=out_specs,
            scratch_shapes=[pltpu.VMEM((2, subs, LOCAL_ROWS, d // 2), u32),
                            pltpu.SemaphoreType.DMA((2, subs))]),
        out_shape=out_shape,
        compiler_params=_cparams(("arbitrary",)),
        name="moe_combine",
    )(*tables, *args)
    return [o.reshape(b, l, -1) for o in outs]


def _moe(f, routed, x, mod, g, w_gate, w_up, w_down, layer, glu=None):
    ri, wc, cnt, meta = routed
    b, l, d = x.shape
    t = b * l
    tb = EXPERT_ROWS
    n_tiles = t // DISPATCH_TILE
    used = cnt[:, 0].astype(i32)
    region = (used + tb - 1) // tb * tb
    gend = jnp.cumsum(region)
    gstart = gend - region
    max_rows = 2 * t + n_tiles * N_EXPERTS * (RUN_ALIGN - 1) + N_EXPERTS * (tb - 1)
    n_blocks = -(-max_rows // tb)
    m = meta[:, :, :, 0]
    run_start = m[:, 0].reshape(-1)
    run_size = m[:, 1].reshape(-1)
    run_dst = (m[:, 2] + gstart[None, :]).reshape(-1)
    tile_rows = jnp.sum(m[:, 1], axis=1)
    block_row = jnp.arange(n_blocks, dtype=i32) * tb
    block_e = jnp.minimum(jnp.sum((block_row[:, None] >= gend[None, :]).astype(i32), axis=1),
                          N_EXPERTS - 1)
    n_valid = (gend[-1] // tb).reshape(1)
    xs = _dispatch((run_start, run_size, run_dst, tile_rows, gstart + used, region - used, n_valid),
                   f.reshape(t, d), ri, n_blocks * tb)
    ys = _experts(block_e, n_valid, xs, w_gate, w_up, w_down, layer)
    return _combine((run_start, run_size, run_dst, tile_rows), ys, wc, x, mod, g, glu)


def _rope_tables(l):
    lane = jnp.arange(LANES)
    dh = lane % HEAD_DIM
    inv = ROPE_THETA ** (-(dh % 16).astype(f32) / 16.0)
    sign = jnp.where((dh % 32) < 16, -1.0, 1.0).astype(f32)
    by_row = (dh // 32)[None, None, :] == 0
    ang_r = jnp.arange(l // GRID_W, dtype=f32)[:, None] * inv[None, :]
    ang_c = jnp.arange(GRID_W, dtype=f32)[:, None] * inv[None, :]
    cos = jnp.where(by_row, jnp.cos(ang_r)[:, None, :], jnp.cos(ang_c)[None, :, :])
    sin = jnp.where(by_row, jnp.sin(ang_r)[:, None, :], jnp.sin(ang_c)[None, :, :])
    return cos.reshape(l, LANES), (sin * sign[None, None, :]).reshape(l, LANES)


def kernel(x, c, ctx, c_ctx, ada_w, ada_b, norm_mix_g, norm_ffn_g, even_w_in, even_w_out, even_sink, conv_pw1_w, conv_pw1_b, conv_dw_w, conv_dw_b, conv_ln_g, conv_ln_b, conv_pw2_w, conv_pw2_b, router_w, router_b, moe_w_gate, moe_w_up, moe_w_down, final_norm_g):
    b, l, d = x.shape
    depth = ada_w.shape[0]
    assert depth == 2 and b < COND_ROWS
    ctx_row = b
    cond = jnp.zeros((COND_ROWS, d), f32).at[:b].set(c).at[ctx_row].set(c_ctx)
    mods = _adaln(cond, ada_w, ada_b).reshape(depth, COND_ROWS, 6, d)

    heads = jnp.arange(N_HEADS).reshape(N_KV_HEADS, N_HEADS // N_KV_HEADS).T.reshape(-1)
    qperm = (heads[:, None] * HEAD_DIM + jnp.arange(HEAD_DIM)[None, :]).reshape(-1)
    fw = FOURIER_GROUPS * FOURIER_GROUP_W
    qw = N_HEADS * HEAD_DIM
    w_in = even_w_in[0]
    w_in_p = jnp.concatenate([w_in[:, :fw], w_in[:, fw:fw + qw][:, qperm], w_in[:, fw + qw:]],
                             axis=1).astype(bf16)
    w_out = even_w_out[0]
    w_out_p = jnp.concatenate([w_out[:fw], w_out[fw:][qperm]], axis=0).astype(bf16)
    sinkcol = jnp.repeat(even_sink[0][heads].astype(f32) * LOG2E, ATT_BLOCK).reshape(-1, 1)

    cidx = jnp.arange(FOURIER_GROUP_W, dtype=i32)
    angc = ((cidx[:, None] * cidx[None, :]) % FOURIER_GROUP_W).astype(f32) * (2.0 * math.pi / FOURIER_GROUP_W)
    cs = jnp.concatenate([jnp.cos(angc), jnp.sin(angc)], axis=1).astype(bf16)
    cos_t, sin_t = _rope_tables(l)

    rw32 = jnp.zeros((d, LANES), f32).at[:, :N_EXPERTS].set(router_w.astype(f32))
    rw_hi = rw32.astype(bf16)
    rw = jnp.concatenate([rw_hi, (rw32 - rw_hi.astype(f32)).astype(bf16)], axis=1)
    rb = jnp.zeros((1, LANES), f32).at[0, :N_EXPERTS].set(router_b)
    row = lambda v: v.reshape(1, -1)

    q, k, v = _inproj(x, mods[0], row(norm_mix_g[0]), w_in_p[:, fw:], cos_t, sin_t)
    ck, cv = _ctxkv(ctx, mods[0], row(norm_mix_g[0]), w_in_p[:, fw + qw:], ctx_row)
    yf = _fourier(x, mods[0], row(norm_mix_g[0]), w_in_p[:, :fw], cs)
    att = _attention(q, k, v, ck, cv, sinkcol)
    x1, f, *routed = _outproj(yf, att, x, mods[0], w_out_p, row(norm_ffn_g[0]), rw, rb)
    x2, u = _moe(f, routed, x1, mods[0], row(norm_mix_g[1]), moe_w_gate, moe_w_up, moe_w_down,
                 layer=0, glu=(mods[1], conv_pw1_w[0].astype(bf16), row(conv_pw1_b[0])))

    x3, f, *routed = _conv(u, x2, mods[1], conv_dw_w[0], row(conv_dw_b[0]), row(conv_ln_g[0]),
                           row(conv_ln_b[0]), conv_pw2_w[0].astype(bf16), row(conv_pw2_b[0]),
                           row(norm_ffn_g[1]), rw, rb)
    (out,) = _moe(f, routed, x3, mods[1], row(final_norm_g), moe_w_gate, moe_w_up, moe_w_down,
                  layer=1)
    return out
```

```python
import functools
import math

import jax
import jax.numpy as jnp
from jax import lax
from jax.experimental import pallas as pl
from jax.experimental.pallas import tpu as pltpu

f32 = jnp.float32
bf16 = jnp.bfloat16
i32 = jnp.int32
u32 = jnp.uint32
HIGHEST = lax.Precision.HIGHEST

GRID_W = 64
HEAD_DIM = 64
N_HEADS = 8
N_KV_HEADS = 2
WINDOW = 128
ATT_BLOCK = 128
ROPE_THETA = 10000.0
FOURIER_GROUPS = 4
FOURIER_GROUP_W = 128
CONV_W = 31
N_EXPERTS = 16
N_GROUPS = 4
EXPERTS_PER_GROUP = 4
EXPERT_FF = 512
EPS = 1e-6
NEG_INF = -1e30
LOG2E = math.log2(math.e)

LANES = 128
SUBLANES = 8
COND_ROWS = 8
DFT_INNER = 64
TOKEN_TILE = 512
PROJ_TILE = 1024
ATT_TILE = 1024
EXPERT_ROWS = 512
DISPATCH_TILE = 256
DISPATCH_SUBTILES = 4
GLU_SUBTILES = 2
RUN_ALIGN = 8
LOCAL_ROWS = -(-(2 * DISPATCH_TILE + N_EXPERTS * (RUN_ALIGN - 1)) // LANES) * LANES
CONV_HALO = 16
CONV_ROWS = 128
VMEM_LIMIT = 56 * 1024 * 1024


def _cparams(sem, vmem=VMEM_LIMIT):
    return pltpu.CompilerParams(dimension_semantics=sem, vmem_limit_bytes=vmem)


def _adaln_kernel(cond_ref, w_ref, b_ref, o_ref):
    s = cond_ref[...]
    s = s * jax.nn.sigmoid(s)
    o_ref[0] = jnp.dot(s, w_ref[0], precision=HIGHEST, preferred_element_type=f32) + b_ref[0]


def _adaln(cond, ada_w, ada_b):
    depth, d, n = ada_w.shape
    tn = 1536
    return pl.pallas_call(
        _adaln_kernel,
        grid=(depth, n // tn),
        in_specs=[pl.BlockSpec((COND_ROWS, d), lambda i, j: (0, 0)),
                  pl.BlockSpec((1, d, tn), lambda i, j: (i, 0, j)),
                  pl.BlockSpec((1, 1, tn), lambda i, j: (i, 0, j))],
        out_specs=pl.BlockSpec((1, COND_ROWS, tn), lambda i, j: (i, 0, j)),
        out_shape=jax.ShapeDtypeStruct((depth, COND_ROWS, n), f32),
        compiler_params=_cparams(("arbitrary", "arbitrary")),
        name="adaln",
    )(cond, ada_w, ada_b.reshape(depth, 1, n))


def _norm_mod(x, g, shift, scale):
    ms = jnp.mean(x * x, axis=-1, keepdims=True)
    return (x * lax.rsqrt(ms + EPS)) * (g * (1.0 + scale)) + shift


def _rope(p, cos, sin_signed, first_half):
    rot = jnp.where(first_half, pltpu.roll(p, LANES - 16, axis=1), pltpu.roll(p, 16, axis=1))
    return p * cos + rot * sin_signed


def _inproj_kernel(x_ref, mod_ref, g_ref, w_ref, cos_ref, sin_ref, q_ref, k_ref, v_ref):
    h = _norm_mod(x_ref[0], g_ref[...], mod_ref[0, 0:1, :], mod_ref[0, 1:2, :])
    p = jnp.dot(h.astype(bf16), w_ref[...], preferred_element_type=f32)
    cos = cos_ref[...]
    sin = sin_ref[...]
    lane = lax.broadcasted_iota(i32, cos.shape, 1)
    first_half = (lane % 32) < 16
    qw = N_HEADS * HEAD_DIM
    for c in range(qw // LANES):
        qc = p[:, c * LANES:(c + 1) * LANES]
        q_ref[0, :, c * LANES:(c + 1) * LANES] = (
            _rope(qc, cos, sin, first_half) * (LOG2E * HEAD_DIM ** -0.5)).astype(bf16)
    k_ref[0] = _rope(p[:, qw:qw + LANES], cos, sin, first_half).astype(bf16)
    v_ref[0] = p[:, qw + LANES:].astype(bf16)


def _inproj(x, mod, g, w, cos_t, sin_t):
    b, l, d = x.shape
    tm = PROJ_TILE
    n = w.shape[1]
    return pl.pallas_call(
        _inproj_kernel,
        grid=(b, l // tm),
        in_specs=[pl.BlockSpec((1, tm, d), lambda i, j: (i, j, 0)),
                  pl.BlockSpec((1, 6, d), lambda i, j: (i, 0, 0)),
                  pl.BlockSpec((1, d), lambda i, j: (0, 0)),
                  pl.BlockSpec((d, n), lambda i, j: (0, 0)),
                  pl.BlockSpec((tm, LANES), lambda i, j: (j, 0)),
                  pl.BlockSpec((tm, LANES), lambda i, j: (j, 0))],
        out_specs=[pl.BlockSpec((1, tm, N_HEADS * HEAD_DIM), lambda i, j: (i, j, 0)),
                   pl.BlockSpec((1, tm, LANES), lambda i, j: (i, j, 0)),
                   pl.BlockSpec((1, tm, LANES), lambda i, j: (i, j, 0))],
        out_shape=[jax.ShapeDtypeStruct((b, l, N_HEADS * HEAD_DIM), bf16),
                   jax.ShapeDtypeStruct((b, l, LANES), bf16),
                   jax.ShapeDtypeStruct((b, l, LANES), bf16)],
        compiler_params=_cparams(("parallel", "parallel")),
        name="inproj",
    )(x, mod, g, w, cos_t, sin_t)


def _ctxkv_kernel(x_ref, mod_ref, g_ref, w_ref, k_ref, v_ref):
    h = _norm_mod(x_ref[0], g_ref[...], mod_ref[0, 0:1, :], mod_ref[0, 1:2, :])
    p = jnp.dot(h.astype(bf16), w_ref[...], preferred_element_type=f32)
    k_ref[0] = p[:, :LANES].astype(bf16)
    v_ref[0] = p[:, LANES:].astype(bf16)


def _ctxkv(ctx, mod, g, w_kv, ctx_row):
    b, c, d = ctx.shape
    return pl.pallas_call(
        _ctxkv_kernel,
        grid=(b,),
        in_specs=[pl.BlockSpec((1, c, d), lambda i: (i, 0, 0)),
                  pl.BlockSpec((1, 6, d), lambda i: (ctx_row, 0, 0)),
                  pl.BlockSpec((1, d), lambda i: (0, 0)),
                  pl.BlockSpec((d, 2 * LANES), lambda i: (0, 0))],
        out_specs=[pl.BlockSpec((1, c, LANES), lambda i: (i, 0, 0)),
                   pl.BlockSpec((1, c, LANES), lambda i: (i, 0, 0))],
        out_shape=[jax.ShapeDtypeStruct((b, c, LANES), bf16),
                   jax.ShapeDtypeStruct((b, c, LANES), bf16)],
        compiler_params=_cparams(("parallel",)),
        name="ctxkv",
    )(ctx, mod, g, w_kv)


def _attn_kernel(seq_len, q_ref, kp_ref, km_ref, kn_ref, vp_ref, vm_ref, vn_ref,
                 ck_ref, cv_ref, sink_ref, o_ref, kext, vext, cvext):
    j = pl.program_id(1)
    tq = ATT_TILE
    blk = ATT_BLOCK
    kext[0:blk] = kp_ref[0]
    kext[blk:blk + tq] = km_ref[0]
    kext[blk + tq:] = kn_ref[0]
    vext[:, LANES:] = jnp.ones((tq + 2 * blk, LANES), bf16)
    vext[0:blk, :LANES] = vp_ref[0]
    vext[blk:blk + tq, :LANES] = vm_ref[0]
    vext[blk + tq:, :LANES] = vn_ref[0]
    cvext[:, LANES:] = jnp.ones((cvext.shape[0], LANES), bf16)
    cvext[:, :LANES] = cv_ref[0]
    ck = ck_ref[0]
    sink = sink_ref[...]
    n_chunks = (N_HEADS * HEAD_DIM) // LANES
    rows = 2 * n_chunks * blk
    lane = lax.broadcasted_iota(i32, (blk, LANES), 1)
    low = lane < HEAD_DIM
    qi = lax.broadcasted_iota(i32, (rows, 3 * blk), 0) % blk
    pk = lax.broadcasted_iota(i32, (rows, 3 * blk), 1)
    band_bias = jnp.where(jnp.abs(pk - blk - qi) <= WINDOW, 0.0, NEG_INF).astype(f32)
    pcol = lax.broadcasted_iota(i32, (1, 3 * blk), 1)
    nt = (((1,), (1,)), ((), ()))

    def sub(s, carry):
        r0 = pl.multiple_of(s * blk, blk)
        qs = q_ref[0, pl.ds(r0, blk), :]
        parts = []
        for c in range(n_chunks):
            qc = qs[:, c * LANES:(c + 1) * LANES]
            parts.append(jnp.where(low, qc, jnp.zeros_like(qc)))
            parts.append(jnp.where(low, jnp.zeros_like(qc), qc))
        lhs = jnp.concatenate(parts, axis=0)
        kl = kext[pl.ds(r0, 3 * blk), :]
        vl = vext[pl.ds(r0, 3 * blk), :]
        kpos = j * tq + r0 - blk + pcol
        col_bias = jnp.where((kpos >= 0) & (kpos < seq_len), 0.0, NEG_INF).astype(f32)
        s_c = lax.dot_general(lhs, ck, nt, preferred_element_type=f32)
        s_l = lax.dot_general(lhs, kl, nt, preferred_element_type=f32) + band_bias + col_bias
        blocks = ([s_c[:, i:i + LANES] for i in range(0, s_c.shape[1], LANES)]
                  + [s_l[:, i:i + LANES] for i in range(0, s_l.shape[1], LANES)])
        folded = functools.reduce(jnp.maximum, blocks)
        m = jnp.maximum(jnp.max(folded, axis=1, keepdims=True), sink)
        e_c = jnp.exp2(s_c - m).astype(bf16)
        e_l = jnp.exp2(s_l - m).astype(bf16)
        ov = (jnp.dot(e_c, cvext[...], preferred_element_type=f32)
              + jnp.dot(e_l, vl, preferred_element_type=f32))
        den = ov[:, LANES:] + jnp.exp2(sink - m)
        o = ov[:, :LANES] / den
        for c in range(n_chunks):
            oc = jnp.where(low, o[(2 * c) * blk:(2 * c + 1) * blk],
                           o[(2 * c + 1) * blk:(2 * c + 2) * blk])
            o_ref[0, pl.ds(r0, blk), c * LANES:(c + 1) * LANES] = oc.astype(bf16)
        return carry

    lax.fori_loop(0, tq // blk, sub, 0, unroll=2)


def _attention(q, k, v, ck, cv, sinkcol):
    b, l, qw = q.shape
    c = ck.shape[1]
    tq = ATT_TILE
    r = tq // ATT_BLOCK
    nb = l // ATT_BLOCK
    prev = pl.BlockSpec((1, ATT_BLOCK, LANES), lambda i, j: (i, jnp.maximum(j * r - 1, 0), 0))
    main = pl.BlockSpec((1, tq, LANES), lambda i, j: (i, j, 0))
    nxt = pl.BlockSpec((1, ATT_BLOCK, LANES), lambda i, j: (i, jnp.minimum(j * r + r, nb - 1), 0))
    cspec = pl.BlockSpec((1, c, LANES), lambda i, j: (i, 0, 0))
    return pl.pallas_call(
        functools.partial(_attn_kernel, l),
        grid=(b, l // tq),
        in_specs=[pl.BlockSpec((1, tq, qw), lambda i, j: (i, j, 0)),
                  prev, main, nxt, prev, main, nxt, cspec, cspec,
                  pl.BlockSpec(sinkcol.shape, lambda i, j: (0, 0))],
        out_specs=pl.BlockSpec((1, tq, qw), lambda i, j: (i, j, 0)),
        out_shape=jax.ShapeDtypeStruct((b, l, qw), bf16),
        scratch_shapes=[pltpu.VMEM((tq + 2 * ATT_BLOCK, LANES), bf16),
                        pltpu.VMEM((tq + 2 * ATT_BLOCK, 2 * LANES), bf16),
                        pltpu.VMEM((c, 2 * LANES), bf16)],
        compiler_params=_cparams(("parallel", "parallel")),
        name="attention",
    )(q, k, k, k, v, v, v, ck, cv, sinkcol)


def _pack_pair(lo, hi):
    lo = lax.bitcast_convert_type(lo.astype(bf16).astype(f32), u32)
    hi = lax.bitcast_convert_type(hi.astype(bf16).astype(f32), u32)
    return (lo >> 16) | (hi & jnp.uint32(0xFFFF0000))


def _fourier1_kernel(x_ref, mod_ref, g_ref, w_ref, cs_ref, m_ref, ct_ref, st_ref, z_ref, ab_ref):
    n1 = x_ref.shape[1]
    nt = x_ref.shape[2]
    x = x_ref[0].reshape(n1 * nt, x_ref.shape[3])
    h = _norm_mod(x, g_ref[...], mod_ref[0, 0:1, :], mod_ref[0, 1:2, :]).astype(bf16)
    p = jnp.dot(h, w_ref[...], preferred_element_type=f32)
    for g in range(FOURIER_GROUPS):
        ug = p[:, g * LANES:(g + 1) * LANES].astype(bf16)
        ab = jnp.dot(ug, cs_ref[...], preferred_element_type=f32)
        ab_ref[0] = ab[:, :LANES]
        ab_ref[1] = ab[:, LANES:]
        for t in range(nt):
            stack = jnp.concatenate([ab_ref[0, pl.ds(t, n1, stride=nt), :],
                                     ab_ref[1, pl.ds(t, n1, stride=nt), :]], axis=0).astype(bf16)
            z = jnp.dot(m_ref[...], stack, preferred_element_type=f32)
            zr, zn = z[:n1], z[n1:]
            ct, st = ct_ref[t], st_ref[t]
            z_ref[0, g, t] = _pack_pair(ct * zr - st * zn, ct * zn + st * zr)


def _fourier2_kernel(scale, z_ref, m_ref, o_ref, zbuf, ybuf):
    _, grp, n2, tk, w = z_ref.shape
    for g in range(grp):
        zbuf[...] = z_ref[0, g].reshape(n2 * tk, w)
        for j in range(tk):
            zp = zbuf[pl.ds(j, n2, stride=tk), :]
            zr = lax.bitcast_convert_type(zp << 16, f32).astype(bf16)
            zn = lax.bitcast_convert_type(zp & jnp.uint32(0xFFFF0000), f32).astype(bf16)
            y = jnp.dot(m_ref[...], jnp.concatenate([zr, zn], axis=0), preferred_element_type=f32)
            ybuf[pl.ds(j, n2, stride=tk), :] = y * scale
        o_ref[0, g] = ybuf[...].reshape(n2, tk, w)


def _fourier(x, mod, g, w_f, cs):
    b, l, d = x.shape
    n2 = DFT_INNER
    n1 = l // n2
    grp, w = FOURIER_GROUPS, FOURIER_GROUP_W
    t2 = SUBLANES
    k1 = jnp.arange(n1, dtype=i32)
    ang1 = ((k1[:, None] * k1[None, :]) % n1).astype(f32) * (2.0 * math.pi / n1)
    c1, s1 = jnp.cos(ang1), jnp.sin(ang1)
    m1 = jnp.concatenate([jnp.concatenate([c1, -s1], axis=1),
                          jnp.concatenate([s1, c1], axis=1)], axis=0).astype(bf16)
    l2 = jnp.arange(n2, dtype=i32)
    angt = ((l2[:, None] * k1[None, :]) % l).astype(f32) * (2.0 * math.pi / l)
    ct = jnp.broadcast_to(jnp.cos(angt)[:, :, None], (n2, n1, w))
    st = jnp.broadcast_to(jnp.sin(angt)[:, :, None], (n2, n1, w))
    ang2 = ((l2[:, None] * l2[None, :]) % n2).astype(f32) * (2.0 * math.pi / n2)
    m2 = jnp.concatenate([jnp.cos(ang2), -jnp.sin(ang2)], axis=1).astype(bf16)

    tspec = pl.BlockSpec((t2, n1, w), lambda t, i: (t, 0, 0))
    z = pl.pallas_call(
        _fourier1_kernel,
        grid=(n2 // t2, b),
        in_specs=[pl.BlockSpec((1, n1, t2, d), lambda t, i: (i, 0, t, 0)),
                  pl.BlockSpec((1, 6, d), lambda t, i: (i, 0, 0)),
                  pl.BlockSpec((1, d), lambda t, i: (0, 0)),
                  pl.BlockSpec(w_f.shape, lambda t, i: (0, 0)),
                  pl.BlockSpec(cs.shape, lambda t, i: (0, 0)),
                  pl.BlockSpec(m1.shape, lambda t, i: (0, 0)), tspec, tspec],
        out_specs=pl.BlockSpec((1, grp, t2, n1, w), lambda t, i: (i, 0, t, 0, 0)),
        out_shape=jax.ShapeDtypeStruct((b, grp, n2, n1, w), u32),
        scratch_shapes=[pltpu.VMEM((2, n1 * t2, w), f32)],
        compiler_params=_cparams(("parallel", "parallel")),
        name="fourier_outer",
    )(x.reshape(b, n1, n2, d), mod, g, w_f, cs, m1, ct, st)

    tk = SUBLANES
    y = pl.pallas_call(
        functools.partial(_fourier2_kernel, 1.0 / math.sqrt(l * w)),
        grid=(b, n1 // tk),
        in_specs=[pl.BlockSpec((1, grp, n2, tk, w), lambda i, t: (i, 0, 0, t, 0)),
                  pl.BlockSpec(m2.shape, lambda i, t: (0, 0))],
        out_specs=pl.BlockSpec((1, grp, n2, tk, w), lambda i, t: (i, 0, 0, t, 0)),
        out_shape=jax.ShapeDtypeStruct((b, grp, n2, n1, w), f32),
        scratch_shapes=[pltpu.VMEM((n2 * tk, w), u32), pltpu.VMEM((n2 * tk, w), f32)],
        compiler_params=_cparams(("parallel", "parallel")),
        name="fourier_inner",
    )(z, m2)
    return y.reshape(b, grp, l, w)


def _first_max4(a):
    m = jnp.maximum(jnp.maximum(a[0], a[1]), jnp.maximum(a[2], a[3]))
    idx = jnp.where(a[0] == m, 0, jnp.where(a[1] == m, 1, jnp.where(a[2] == m, 2, 3)))
    return m, idx


def _pick4(vals, idx):
    return jnp.where(idx == 0, vals[0], jnp.where(idx == 1, vals[1],
                                                   jnp.where(idx == 2, vals[2], vals[3])))


def _route(f, rw_ref, rb_ref, tri_ref, base_ref, first_step, ri_ref, wc_ref, cnt_ref, meta_ref):
    tm = f.shape[0]
    f_hi = f.astype(bf16)
    f_lo = (f - f_hi.astype(f32)).astype(bf16)
    rw2 = rw_ref[...]
    part = jnp.dot(f_hi, rw2, preferred_element_type=f32)
    logits = (part[:, :LANES] + part[:, LANES:]
              + jnp.dot(f_lo, rw2[:, :LANES], preferred_element_type=f32))
    sc = jax.nn.sigmoid(logits)
    st = sc.T
    bt = (sc + rb_ref[...]).T
    neg = jnp.full((1, tm), -jnp.inf, f32)
    gs = []
    for g in range(N_GROUPS):
        a = [bt[4 * g + i: 4 * g + i + 1] for i in range(4)]
        m1, i1 = _first_max4(a)
        rest = [jnp.where(i1 == i, neg, a[i]) for i in range(4)]
        m2, _ = _first_max4(rest)
        gs.append(m1 + m2)
    _, gsel = _first_max4(gs)
    a = [_pick4([bt[4 * g + i: 4 * g + i + 1] for g in range(N_GROUPS)], gsel) for i in range(4)]
    s = [_pick4([st[4 * g + i: 4 * g + i + 1] for g in range(N_GROUPS)], gsel) for i in range(4)]
    _, i1 = _first_max4(a)
    rest = [jnp.where(i1 == i, neg, a[i]) for i in range(4)]
    _, i2 = _first_max4(rest)
    w1 = _pick4(s, i1)
    w2 = _pick4(s, i2)
    tot = w1 + w2
    w1 = w1 / tot
    w2 = w2 / tot
    e0 = gsel * EXPERTS_PER_GROUP + i1
    e1 = gsel * EXPERTS_PER_GROUP + i2

    @pl.when(first_step)
    def _():
        base_ref[...] = jnp.zeros_like(base_ref)

    td = DISPATCH_TILE
    eid = lax.broadcasted_iota(i32, (N_EXPERTS, tm), 0)
    oh0 = (eid == e0).astype(f32)
    oh1 = (eid == e1).astype(f32)
    oh = oh0 + oh1
    before = jnp.dot(oh.astype(bf16), tri_ref[...], preferred_element_type=f32)
    lane_tile = lax.broadcasted_iota(i32, (N_EXPERTS, tm), 1) // td
    ei = lax.broadcasted_iota(i32, (N_EXPERTS, N_EXPERTS), 0)
    ej = lax.broadcasted_iota(i32, (N_EXPERTS, N_EXPERTS), 1)
    strict_lower = (ej < ei).astype(f32)
    run_start = jnp.zeros((N_EXPERTS, tm), f32)
    goff = base_ref[...]
    for s in range(tm // td):
        cnt_s = jnp.sum(oh[:, s * td:(s + 1) * td], axis=1, keepdims=True)
        pad_s = jnp.floor((cnt_s + 7.0) * 0.125) * 8.0
        pad_b = jnp.broadcast_to(pad_s, (N_EXPERTS, LANES))
        start_b = jnp.dot(strict_lower, pad_b, precision=HIGHEST, preferred_element_type=f32)
        run_start = jnp.where(lane_tile == s, start_b[:, 0:1], run_start)
        meta_ref[s, 0] = start_b.astype(i32)
        meta_ref[s, 1] = pad_b.astype(i32)
        meta_ref[s, 2] = goff.astype(i32)
        goff = goff + pad_b
    base_ref[...] = goff
    cnt_ref[...] = goff
    pos = before + run_start
    lp0 = jnp.sum(oh0 * pos, axis=0, keepdims=True)
    lp1 = jnp.sum(oh1 * pos, axis=0, keepdims=True)
    zi = jnp.zeros((1, tm), i32)
    ri_ref[...] = jnp.concatenate(
        [lp0.astype(i32), lp1.astype(i32), e0, e1,
         lax.bitcast_convert_type(w1, i32), lax.bitcast_convert_type(w2, i32), zi, zi], axis=0)
    zf = jnp.zeros((LANES - 4, tm), f32)
    wc_ref[...] = jnp.concatenate([w1, w2, lp0, lp1, zf], axis=0).T


def _outproj_kernel(yf_ref, o_ref, x_ref, mod_ref, w_ref, g_ref, rw_ref, rb_ref, tri_ref,
                    x1_ref, f_ref, ri_ref, wc_ref, cnt_ref, meta_ref, base_ref):
    mix = jnp.concatenate([yf_ref[0, g].astype(bf16) for g in range(FOURIER_GROUPS)] + [o_ref[0]],
                          axis=1)
    y = jnp.dot(mix, w_ref[...], preferred_element_type=f32)
    x1 = x_ref[0] + mod_ref[0, 2:3, :] * y
    x1_ref[0] = x1
    f = _norm_mod(x1, g_ref[...], mod_ref[0, 3:4, :], mod_ref[0, 4:5, :])
    f_ref[0] = f.astype(bf16)
    first = (pl.program_id(0) == 0) & (pl.program_id(1) == 0)
    _route(f, rw_ref, rb_ref, tri_ref, base_ref, first, ri_ref, wc_ref, cnt_ref, meta_ref)


def _before_in_tile(tm):
    tpos = jnp.arange(tm)
    return ((tpos[:, None] < tpos[None, :])
            & (tpos[:, None] // DISPATCH_TILE == tpos[None, :] // DISPATCH_TILE)).astype(bf16)


def _route_specs(b, l, tm):
    nl = l // tm
    rw = lambda d: pl.BlockSpec((d, 2 * LANES), lambda i, j: (0, 0))
    rb = pl.BlockSpec((1, LANES), lambda i, j: (0, 0))
    tri = pl.BlockSpec((tm, tm), lambda i, j: (0, 0))
    ns = tm // DISPATCH_TILE
    out_specs = [pl.BlockSpec((8, tm), lambda i, j: (0, i * nl + j)),
                 pl.BlockSpec((tm, LANES), lambda i, j: (i * nl + j, 0)),
                 pl.BlockSpec((N_EXPERTS, LANES), lambda i, j: (0, 0)),
                 pl.BlockSpec((ns, 3, N_EXPERTS, LANES), lambda i, j: (i * nl + j, 0, 0, 0))]
    out_shape = [jax.ShapeDtypeStruct((8, b * l), i32),
                 jax.ShapeDtypeStruct((b * l, LANES), f32),
                 jax.ShapeDtypeStruct((N_EXPERTS, LANES), f32),
                 jax.ShapeDtypeStruct((b * l // DISPATCH_TILE, 3, N_EXPERTS, LANES), i32)]
    return rw, rb, tri, out_specs, out_shape


def _outproj(yf, o, x, mod, w, g, rw, rb):
    b, l, d = x.shape
    tm = PROJ_TILE
    tri = _before_in_tile(tm)
    rws, rbs, tris, r_specs, r_shapes = _route_specs(b, l, tm)
    row = pl.BlockSpec((1, tm, d), lambda i, j: (i, j, 0))
    return pl.pallas_call(
        _outproj_kernel,
        grid=(b, l // tm),
        in_specs=[pl.BlockSpec((1, FOURIER_GROUPS, tm, LANES), lambda i, j: (i, 0, j, 0)),
                  pl.BlockSpec((1, tm, o.shape[2]), lambda i, j: (i, j, 0)),
                  row,
                  pl.BlockSpec((1, 6, d), lambda i, j: (i, 0, 0)),
                  pl.BlockSpec(w.shape, lambda i, j: (0, 0)),
                  pl.BlockSpec((1, d), lambda i, j: (0, 0)),
                  rws(d), rbs, tris],
        out_specs=[row, row] + r_specs,
        out_shape=[jax.ShapeDtypeStruct((b, l, d), f32),
                   jax.ShapeDtypeStruct((b, l, d), bf16)] + r_shapes,
        scratch_shapes=[pltpu.VMEM((N_EXPERTS, LANES), f32)],
        compiler_params=_cparams(("arbitrary", "arbitrary")),
        name="outproj_router",
    )(yf, o, x, mod, w, g, rw, rb, tri)


def _conv_kernel(seq_len, up_ref, um_ref, un_ref, x_ref, mod_ref, dw_ref, db_ref, lg_ref, lb_ref,
                 w_ref, pb_ref, g_ref, rw_ref, rb_ref, tri_ref,
                 x1_ref, f_ref, ri_ref, wc_ref, cnt_ref, meta_ref, base_ref, ext, conv_out):
    j = pl.program_id(1)
    tm = um_ref.shape[1]
    hl = CONV_HALO
    half = CONV_W // 2
    ext[0:hl] = jnp.where(j > 0, up_ref[0], jnp.zeros_like(up_ref[0]))
    ext[hl:hl + tm] = um_ref[0]
    ext[hl + tm:] = jnp.where((j + 1) * tm < seq_len, un_ref[0], jnp.zeros_like(un_ref[0]))
    base = hl - half
    span = (CONV_W - 1) // SUBLANES * SUBLANES
    rows = CONV_ROWS

    def lane_chunk(c, carry):
        lanes = pl.ds(pl.multiple_of(c * LANES, LANES), LANES)
        for r in range(0, tm, rows):
            part = jnp.broadcast_to(db_ref[:, lanes], (rows, LANES))
            for phase in range(SUBLANES):
                win = ext[base + phase + r: base + phase + r + rows + span, lanes]
                same = None
                for t in range(phase, CONV_W, SUBLANES):
                    term = win[t - phase: t - phase + rows, :] * dw_ref[t:t + 1, lanes]
                    same = term if same is None else same + term
                part = part + same
            conv_out[r:r + rows, lanes] = part
        return carry

    lax.fori_loop(0, um_ref.shape[2] // LANES, lane_chunk, 0)
    acc = conv_out[...]
    mu = jnp.mean(acc, axis=-1, keepdims=True)
    cen = acc - mu
    var = jnp.mean(cen * cen, axis=-1, keepdims=True)
    ln = cen * lax.rsqrt(var + EPS) * lg_ref[...] + lb_ref[...]
    act = ln * jax.nn.sigmoid(ln)
    y = jnp.dot(act.astype(bf16), w_ref[...], preferred_element_type=f32) + pb_ref[...]
    x1 = x_ref[0] + mod_ref[0, 2:3, :] * y
    x1_ref[0] = x1
    f = _norm_mod(x1, g_ref[...], mod_ref[0, 3:4, :], mod_ref[0, 4:5, :])
    f_ref[0] = f.astype(bf16)
    first = (pl.program_id(0) == 0) & (j == 0)
    _route(f, rw_ref, rb_ref, tri_ref, base_ref, first, ri_ref, wc_ref, cnt_ref, meta_ref)


def _conv(u, x, mod, dw_w, dw_b, ln_g, ln_b, pw2_w, pw2_b, g, rw, rb):
    b, l, d = x.shape
    tm = TOKEN_TILE
    tri = _before_in_tile(tm)
    hl = CONV_HALO
    r = tm // hl
    nh = l // hl
    rws, rbs, tris, r_specs, r_shapes = _route_specs(b, l, tm)
    row = pl.BlockSpec((1, tm, d), lambda i, j: (i, j, 0))
    vec = pl.BlockSpec((1, d), lambda i, j: (0, 0))
    return pl.pallas_call(
        functools.partial(_conv_kernel, l),
        grid=(b, l // tm),
        in_specs=[pl.BlockSpec((1, hl, d), lambda i, j: (i, jnp.maximum(j * r - 1, 0), 0)),
                  row,
                  pl.BlockSpec((1, hl, d), lambda i, j: (i, jnp.minimum(j * r + r, nh - 1), 0)),
                  row,
                  pl.BlockSpec((1, 6, d), lambda i, j: (i, 0, 0)),
                  pl.BlockSpec(dw_w.shape, lambda i, j: (0, 0)),
                  vec, vec, vec,
                  pl.BlockSpec(pw2_w.shape, lambda i, j: (0, 0)),
                  vec, vec, rws(d), rbs, tris],
        out_specs=[row, row] + r_specs,
        out_shape=[jax.ShapeDtypeStruct((b, l, d), f32),
                   jax.ShapeDtypeStruct((b, l, d), bf16)] + r_shapes,
        scratch_shapes=[pltpu.VMEM((N_EXPERTS, LANES), f32),
                        pltpu.VMEM((tm + 2 * hl, d), f32),
                        pltpu.VMEM((tm, d), f32)],
        compiler_params=_cparams(("arbitrary", "arbitrary")),
        name="conv_router",
    )(u, u, u, x, mod, dw_w, dw_b, ln_g, ln_b, pw2_w, pw2_b, g, rw, rb, tri)


def _pack_bf16_pairs(x):
    h = x.shape[1] // 2
    lo = lax.bitcast_convert_type(x[:, :h], u32)
    hi = lax.bitcast_convert_type(x[:, h:], u32)
    return (lo >> 16) | (hi & jnp.uint32(0xFFFF0000))


def _unpack_bf16_pairs(u):
    lo = lax.bitcast_convert_type(u << 16, f32)
    hi = lax.bitcast_convert_type(u & jnp.uint32(0xFFFF0000), f32)
    return jnp.concatenate([lo, hi], axis=1).astype(bf16)


def _run_copies(meta, tile, local_ref, hbm_ref, sem, to_hbm):
    start_ref, size_ref, dst_ref = meta
    for e in range(N_EXPERTS):
        k = tile * N_EXPERTS + e
        size = pl.multiple_of(size_ref[k], RUN_ALIGN)

        @pl.when(size > 0)
        def _():
            loc = local_ref.at[pl.ds(pl.multiple_of(start_ref[k], RUN_ALIGN), size)]
            glob = hbm_ref.at[pl.ds(pl.multiple_of(dst_ref[k], RUN_ALIGN), size)]
            if to_hbm:
                pltpu.make_async_copy(loc, glob, sem).start()
            else:
                pltpu.make_async_copy(glob, loc, sem).start()


def _wait_rows(rows, local_ref, hbm_ref, sem):
    rows = pl.multiple_of(rows, RUN_ALIGN)

    @pl.when(rows > 0)
    def _():
        pltpu.make_async_copy(local_ref.at[pl.ds(0, rows)], hbm_ref.at[pl.ds(0, rows)], sem).wait()


def _dispatch_kernel(start_ref, size_ref, dst_ref, tot_ref, tail_start_ref, tail_size_ref, nv_ref,
                     f_ref, lp_ref, xs_ref, loc, zbuf, sem, zsem):
    i = pl.program_id(0)
    n = pl.num_programs(0)
    slot = i % 2
    subs = loc.shape[1]
    rows = loc.shape[2]
    td = f_ref.shape[0] // subs
    meta = (start_ref, size_ref, dst_ref)

    def drain(step, which):
        for s in range(subs):
            _wait_rows(tot_ref[step * subs + s], loc.at[which, s], xs_ref, sem.at[which])

    @pl.when(i >= 2)
    def _():
        drain(i - 2, slot)

    r = lax.broadcasted_iota(i32, (rows, td), 0)
    for s in range(subs):
        cols = slice(s * td, (s + 1) * td)
        pick0 = r == lp_ref[0:1, cols]
        pick1 = r == lp_ref[1:2, cols]
        onehot = (pick0 | pick1).astype(bf16)
        sorted_rows = jnp.dot(onehot, f_ref[cols, :], preferred_element_type=f32)
        half = sorted_rows.shape[1] // 2
        loc[slot, s, :, :half] = _pack_bf16_pairs(sorted_rows)
        w0 = lax.bitcast_convert_type(lp_ref[4:5, cols], f32)
        w1 = lax.bitcast_convert_type(lp_ref[5:6, cols], f32)
        row_w = jnp.sum(jnp.where(pick0, w0, 0.0) + jnp.where(pick1, w1, 0.0), axis=1, keepdims=True)
        loc[slot, s, :, half:] = jnp.broadcast_to(lax.bitcast_convert_type(row_w, u32), (rows, LANES))
        _run_copies(meta, i * subs + s, loc.at[slot, s], xs_ref, sem.at[slot], to_hbm=True)

    @pl.when(i == n - 1)
    def _():
        zbuf[...] = jnp.zeros_like(zbuf)
        total = 0
        for e in range(N_EXPERTS):
            size = pl.multiple_of(tail_size_ref[e], RUN_ALIGN)
            total = total + size

            @pl.when(size > 0)
            def _():
                pltpu.make_async_copy(
                    zbuf.at[pl.ds(0, size)],
                    xs_ref.at[pl.ds(pl.multiple_of(tail_start_ref[e], RUN_ALIGN), size)], zsem).start()

        _wait_rows(total, zbuf, xs_ref, zsem)

        def zero_block(k, c):
            pltpu.make_async_copy(zbuf, xs_ref.at[pl.ds(pl.multiple_of(k * zbuf.shape[0], RUN_ALIGN),
                                                        zbuf.shape[0])], zsem).start()
            return c

        def wait_block(k, c):
            pltpu.make_async_copy(zbuf, xs_ref.at[pl.ds(0, zbuf.shape[0])], zsem).wait()
            return c

        n_blocks = xs_ref.shape[0] // zbuf.shape[0]
        lax.fori_loop(nv_ref[0], n_blocks, zero_block, 0)
        lax.fori_loop(nv_ref[0], n_blocks, wait_block, 0)
        drain(i, slot)

        @pl.when(i >= 1)
        def _():
            drain(i - 1, 1 - slot)


def _dispatch(tables, f2, ri, n_slots):
    t, d = f2.shape
    subs = DISPATCH_SUBTILES
    tm = DISPATCH_TILE * subs
    return pl.pallas_call(
        _dispatch_kernel,
        grid_spec=pltpu.PrefetchScalarGridSpec(
            num_scalar_prefetch=7,
            grid=(t // tm,),
            in_specs=[pl.BlockSpec((tm, d), lambda i, *_: (i, 0)),
                      pl.BlockSpec((8, tm), lambda i, *_: (0, i))],
            out_specs=pl.BlockSpec(memory_space=pl.ANY),
            scratch_shapes=[pltpu.VMEM((2, subs, LOCAL_ROWS, d // 2 + LANES), u32),
                            pltpu.VMEM((EXPERT_ROWS, d // 2 + LANES), u32),
                            pltpu.SemaphoreType.DMA((2,)), pltpu.SemaphoreType.DMA(())]),
        out_shape=jax.ShapeDtypeStruct((n_slots, d // 2 + LANES), u32),
        compiler_params=_cparams(("arbitrary",)),
        name="moe_dispatch",
    )(*tables, f2, ri)


def _expert_kernel(be_ref, nv_ref, x_ref, wg_ref, wu_ref, wd_ref, y_ref, wgb, wub, wdb):
    i = pl.program_id(0)
    changed = jnp.logical_or(i == 0, be_ref[i] != be_ref[jnp.maximum(i - 1, 0)])

    @pl.when(changed)
    def _():
        wgb[...] = wg_ref[0, 0].astype(bf16)
        wub[...] = wu_ref[0, 0].astype(bf16)
        wdb[...] = wd_ref[0, 0].astype(bf16)

    @pl.when(i < nv_ref[0])
    def _():
        half = y_ref.shape[1]
        xb = _unpack_bf16_pairs(x_ref[:, :half])
        row_w = lax.bitcast_convert_type(x_ref[:, half:], f32)
        gate = jnp.dot(xb, wgb[...], preferred_element_type=f32)
        up = jnp.dot(xb, wub[...], preferred_element_type=f32)
        hid = (gate * jax.nn.sigmoid(gate) * up).astype(bf16)
        y = jnp.dot(hid, wdb[...], preferred_element_type=f32)
        y = jnp.concatenate([y[:, c:c + LANES] * row_w for c in range(0, y.shape[1], LANES)], axis=1)
        y_ref[...] = _pack_bf16_pairs(y.astype(bf16).astype(f32))

    @pl.when(i >= nv_ref[0])
    def _():
        y_ref[...] = jnp.zeros_like(y_ref)


def _experts(block_e, n_valid, xs, w_gate, w_up, w_down, layer):
    ns, xw = xs.shape
    tb = EXPERT_ROWS
    d, ff = w_gate.shape[2:]
    dh = d // 2
    xmap = lambda i, be, nv: (jnp.maximum(jnp.minimum(i, nv[0] - 1), 0), 0)
    wmap = lambda i, be, nv: (layer, be[i], 0, 0)
    return pl.pallas_call(
        _expert_kernel,
        grid_spec=pltpu.PrefetchScalarGridSpec(
            num_scalar_prefetch=2,
            grid=(ns // tb,),
            in_specs=[pl.BlockSpec((tb, xw), xmap),
                      pl.BlockSpec((1, 1, d, ff), wmap),
                      pl.BlockSpec((1, 1, d, ff), wmap),
                      pl.BlockSpec((1, 1, ff, d), wmap)],
            out_specs=pl.BlockSpec((tb, dh), lambda i, be, nv: (i, 0)),
            scratch_shapes=[pltpu.VMEM((d, ff), bf16), pltpu.VMEM((d, ff), bf16),
                            pltpu.VMEM((ff, d), bf16)]),
        out_shape=jax.ShapeDtypeStruct((ns, dh), u32),
        compiler_params=_cparams(("arbitrary",)),
        name="moe_experts",
    )(block_e, n_valid, xs, w_gate, w_up, w_down)


def _combine_kernel(final, start_ref, size_ref, dst_ref, tot_ref, ys_ref, wc_ref, x_ref, mod_ref,
                    g_ref, *rest):
    if final:
        o_ref, loc, sem = rest
    else:
        nmod_ref, w_ref, b_ref, o_ref, u_ref, loc, sem = rest
    i = pl.program_id(0)
    n = pl.num_programs(0)
    slot = i % 2
    subs = loc.shape[1]
    rows = loc.shape[2]
    td = x_ref.shape[0] // subs
    meta = (start_ref, size_ref, dst_ref)

    def fetch(step, which):
        for s in range(subs):
            _run_copies(meta, step * subs + s, loc.at[which, s], ys_ref, sem.at[which, s],
                        to_hbm=False)

    @pl.when(i == 0)
    def _():
        loc[...] = jnp.zeros_like(loc)
        fetch(i, slot)

    @pl.when(i + 1 < n)
    def _():
        fetch(i + 1, 1 - slot)

    c = lax.broadcasted_iota(i32, (td, rows), 1)
    parts = []
    for s in range(subs):
        _wait_rows(tot_ref[i * subs + s], loc.at[slot, s], ys_ref, sem.at[slot, s])
        wc = wc_ref[s * td:(s + 1) * td, :]
        sel = ((c == wc[:, 2:3].astype(i32)) | (c == wc[:, 3:4].astype(i32))).astype(bf16)
        parts.append(jnp.dot(sel, _unpack_bf16_pairs(loc[slot, s]), preferred_element_type=f32))
    xo = x_ref[...] + mod_ref[0, 5:6, :] * jnp.concatenate(parts, axis=0)
    if final:
        ms = jnp.mean(xo * xo, axis=-1, keepdims=True)
        o_ref[...] = xo * lax.rsqrt(ms + EPS) * g_ref[...]
    else:
        o_ref[...] = xo
        h = _norm_mod(xo, g_ref[...], nmod_ref[0, 0:1, :], nmod_ref[0, 1:2, :])
        p = jnp.dot(h.astype(bf16), w_ref[...], preferred_element_type=f32) + b_ref[...]
        ch = p.shape[1] // 2
        u_ref[...] = p[:, :ch] * jax.nn.sigmoid(p[:, ch:])


def _combine(tables, ys, wc, x, mod, g, glu=None):
    b, l, d = x.shape
    subs = DISPATCH_SUBTILES if glu is None else GLU_SUBTILES
    tm = DISPATCH_TILE * subs
    per_batch = l // tm
    const = lambda i, *_: (0, 0)
    tile = lambda cols: pl.BlockSpec((tm, cols), lambda i, *_: (i, 0))
    mod_spec = pl.BlockSpec((1, 6, d), lambda i, *_: (i // per_batch, 0, 0))
    in_specs = [pl.BlockSpec(memory_space=pl.ANY), tile(LANES), tile(d), mod_spec,
                pl.BlockSpec((1, d), const)]
    args = [ys, wc, x.reshape(b * l, d), mod, g]
    out_specs = [tile(d)]
    out_shape = [jax.ShapeDtypeStruct((b * l, d), f32)]
    if glu is not None:
        nmod, w, bias = glu
        in_specs += [mod_spec, pl.BlockSpec(w.shape, const), pl.BlockSpec(bias.shape, const)]
        args += [nmod, w, bias]
        out_specs.append(tile(w.shape[1] // 2))
        out_shape.append(jax.ShapeDtypeStruct((b * l, w.shape[1] // 2), f32))
    outs = pl.pallas_call(
        functools.partial(_combine_kernel, glu is None),
        grid_spec=pltpu.PrefetchScalarGridSpec(
            num_scalar_prefetch=4,
            grid=(b * per_batch,),
            in_specs=in_specs,
            out_specs=out_specs,
            scratch_shapes=[pltpu.VMEM((2, subs, LOCAL_ROWS, d // 2), u32),
                            pltpu.SemaphoreType.DMA((2, subs))]),
        out_shape=out_shape,
        compiler_params=_cparams(("arbitrary",)),
        name="moe_combine",
    )(*tables, *args)
    return [o.reshape(b, l, -1) for o in outs]


def _moe(f, routed, x, mod, g, w_gate, w_up, w_down, layer, glu=None):
    ri, wc, cnt, meta = routed
    b, l, d = x.shape
    t = b * l
    tb = EXPERT_ROWS
    n_tiles = t // DISPATCH_TILE
    used = cnt[:, 0].astype(i32)
    region = (used + tb - 1) // tb * tb
    gend = jnp.cumsum(region)
    gstart = gend - region
    max_rows = 2 * t + n_tiles * N_EXPERTS * (RUN_ALIGN - 1) + N_EXPERTS * (tb - 1)
    n_blocks = -(-max_rows // tb)
    m = meta[:, :, :, 0]
    run_start = m[:, 0].reshape(-1)
    run_size = m[:, 1].reshape(-1)
    run_dst = (m[:, 2] + gstart[None, :]).reshape(-1)
    tile_rows = jnp.sum(m[:, 1], axis=1)
    block_row = jnp.arange(n_blocks, dtype=i32) * tb
    block_e = jnp.minimum(jnp.sum((block_row[:, None] >= gend[None, :]).astype(i32), axis=1),
                          N_EXPERTS - 1)
    n_valid = (gend[-1] // tb).reshape(1)
    xs = _dispatch((run_start, run_size, run_dst, tile_rows, gstart + used, region - used, n_valid),
                   f.reshape(t, d), ri, n_blocks * tb)
    ys = _experts(block_e, n_valid, xs, w_gate, w_up, w_down, layer)
    return _combine((run_start, run_size, run_dst, tile_rows), ys, wc, x, mod, g, glu)


def _rope_tables(l):
    lane = jnp.arange(LANES)
    dh = lane % HEAD_DIM
    inv = ROPE_THETA ** (-(dh % 16).astype(f32) / 16.0)
    sign = jnp.where((dh % 32) < 16, -1.0, 1.0).astype(f32)
    by_row = (dh // 32)[None, None, :] == 0
    ang_r = jnp.arange(l // GRID_W, dtype=f32)[:, None] * inv[None, :]
    ang_c = jnp.arange(GRID_W, dtype=f32)[:, None] * inv[None, :]
    cos = jnp.where(by_row, jnp.cos(ang_r)[:, None, :], jnp.cos(ang_c)[None, :, :])
    sin = jnp.where(by_row, jnp.sin(ang_r)[:, None, :], jnp.sin(ang_c)[None, :, :])
    return cos.reshape(l, LANES), (sin * sign[None, None, :]).reshape(l, LANES)


def kernel(x, c, ctx, c_ctx, ada_w, ada_b, norm_mix_g, norm_ffn_g, even_w_in, even_w_out, even_sink, conv_pw1_w, conv_pw1_b, conv_dw_w, conv_dw_b, conv_ln_g, conv_ln_b, conv_pw2_w, conv_pw2_b, router_w, router_b, moe_w_gate, moe_w_up, moe_w_down, final_norm_g):
    b, l, d = x.shape
    depth = ada_w.shape[0]
    assert depth == 2 and b < COND_ROWS
    ctx_row = b
    cond = jnp.zeros((COND_ROWS, d), f32).at[:b].set(c).at[ctx_row].set(c_ctx)
    mods = _adaln(cond, ada_w, ada_b).reshape(depth, COND_ROWS, 6, d)

    heads = jnp.arange(N_HEADS).reshape(N_KV_HEADS, N_HEADS // N_KV_HEADS).T.reshape(-1)
    qperm = (heads[:, None] * HEAD_DIM + jnp.arange(HEAD_DIM)[None, :]).reshape(-1)
    fw = FOURIER_GROUPS * FOURIER_GROUP_W
    qw = N_HEADS * HEAD_DIM
    w_in = even_w_in[0]
    w_in_p = jnp.concatenate([w_in[:, :fw], w_in[:, fw:fw + qw][:, qperm], w_in[:, fw + qw:]],
                             axis=1).astype(bf16)
    w_out = even_w_out[0]
    w_out_p = jnp.concatenate([w_out[:fw], w_out[fw:][qperm]], axis=0).astype(bf16)
    sinkcol = jnp.repeat(even_sink[0][heads].astype(f32) * LOG2E, ATT_BLOCK).reshape(-1, 1)

    cidx = jnp.arange(FOURIER_GROUP_W, dtype=i32)
    angc = ((cidx[:, None] * cidx[None, :]) % FOURIER_GROUP_W).astype(f32) * (2.0 * math.pi / FOURIER_GROUP_W)
    cs = jnp.concatenate([jnp.cos(angc), jnp.sin(angc)], axis=1).astype(bf16)
    cos_t, sin_t = _rope_tables(l)

    rw32 = jnp.zeros((d, LANES), f32).at[:, :N_EXPERTS].set(router_w.astype(f32))
    rw_hi = rw32.astype(bf16)
    rw = jnp.concatenate([rw_hi, (rw32 - rw_hi.astype(f32)).astype(bf16)], axis=1)
    rb = jnp.zeros((1, LANES), f32).at[0, :N_EXPERTS].set(router_b)
    row = lambda v: v.reshape(1, -1)

    q, k, v = _inproj(x, mods[0], row(norm_mix_g[0]), w_in_p[:, fw:], cos_t, sin_t)
    ck, cv = _ctxkv(ctx, mods[0], row(norm_mix_g[0]), w_in_p[:, fw + qw:], ctx_row)
    yf = _fourier(x, mods[0], row(norm_mix_g[0]), w_in_p[:, :fw], cs)
    att = _attention(q, k, v, ck, cv, sinkcol)
    x1, f, *routed = _outproj(yf, att, x, mods[0], w_out_p, row(norm_ffn_g[0]), rw, rb)
    x2, u = _moe(f, routed, x1, mods[0], row(norm_mix_g[1]), moe_w_gate, moe_w_up, moe_w_down,
                 layer=0, glu=(mods[1], conv_pw1_w[0].astype(bf16), row(conv_pw1_b[0])))

    x3, f, *routed = _conv(u, x2, mods[1], conv_dw_w[0], row(conv_dw_b[0]), row(conv_ln_g[0]),
                           row(conv_ln_b[0]), conv_pw2_w[0].astype(bf16), row(conv_pw2_b[0]),
                           row(norm_ffn_g[1]), rw, rb)
    (out,) = _moe(f, routed, x3, mods[1], row(final_norm_g), moe_w_gate, moe_w_up, moe_w_down,
                  layer=1)
    return out
```

```python
import functools
import math

import jax
import jax.numpy as jnp
from jax import lax
from jax.experimental import pallas as pl
from jax.experimental.pallas import tpu as pltpu

f32 = jnp.float32
bf16 = jnp.bfloat16
i32 = jnp.int32
u32 = jnp.uint32
HIGHEST = lax.Precision.HIGHEST

GRID_W = 64
HEAD_DIM = 64
N_HEADS = 8
N_KV_HEADS = 2
WINDOW = 128
ATT_BLOCK = 128
ROPE_THETA = 10000.0
FOURIER_GROUPS = 4
FOURIER_GROUP_W = 128
CONV_W = 31
N_EXPERTS = 16
N_GROUPS = 4
EXPERTS_PER_GROUP = 4
EXPERT_FF = 512
EPS = 1e-6
NEG_INF = -1e30
LOG2E = math.log2(math.e)

LANES = 128
SUBLANES = 8
COND_ROWS = 8
DFT_INNER = 64
TOKEN_TILE = 512
PROJ_TILE = 1024
ATT_TILE = 1024
EXPERT_ROWS = 512
DISPATCH_TILE = 256
DISPATCH_SUBTILES = 4
GLU_SUBTILES = 2
RUN_ALIGN = 8
LOCAL_ROWS = -(-(2 * DISPATCH_TILE + N_EXPERTS * (RUN_ALIGN - 1)) // LANES) * LANES
CONV_HALO = 16
CONV_ROWS = 128
VMEM_LIMIT = 56 * 1024 * 1024


def _cparams(sem, vmem=VMEM_LIMIT):
    return pltpu.CompilerParams(dimension_semantics=sem, vmem_limit_bytes=vmem)


def _adaln_kernel(cond_ref, w_ref, b_ref, o_ref):
    s = cond_ref[...]
    s = s * jax.nn.sigmoid(s)
    o_ref[0] = jnp.dot(s, w_ref[0], precision=HIGHEST, preferred_element_type=f32) + b_ref[0]


def _adaln(cond, ada_w, ada_b):
    depth, d, n = ada_w.shape
    tn = 1536
    return pl.pallas_call(
        _adaln_kernel,
        grid=(depth, n // tn),
        in_specs=[pl.BlockSpec((COND_ROWS, d), lambda i, j: (0, 0)),
                  pl.BlockSpec((1, d, tn), lambda i, j: (i, 0, j)),
                  pl.BlockSpec((1, 1, tn), lambda i, j: (i, 0, j))],
        out_specs=pl.BlockSpec((1, COND_ROWS, tn), lambda i, j: (i, 0, j)),
        out_shape=jax.ShapeDtypeStruct((depth, COND_ROWS, n), f32),
        compiler_params=_cparams(("arbitrary", "arbitrary")),
        name="adaln",
    )(cond, ada_w, ada_b.reshape(depth, 1, n))


def _norm_mod(x, g, shift, scale):
    ms = jnp.mean(x * x, axis=-1, keepdims=True)
    return (x * lax.rsqrt(ms + EPS)) * (g * (1.0 + scale)) + shift


def _rope(p, cos, sin_signed, first_half):
    rot = jnp.where(first_half, pltpu.roll(p, LANES - 16, axis=1), pltpu.roll(p, 16, axis=1))
    return p * cos + rot * sin_signed


def _inproj_kernel(x_ref, mod_ref, g_ref, w_ref, cos_ref, sin_ref, q_ref, k_ref, v_ref):
    h = _norm_mod(x_ref[0], g_ref[...], mod_ref[0, 0:1, :], mod_ref[0, 1:2, :])
    p = jnp.dot(h.astype(bf16), w_ref[...], preferred_element_type=f32)
    cos = cos_ref[...]
    sin = sin_ref[...]
    lane = lax.broadcasted_iota(i32, cos.shape, 1)
    first_half = (lane % 32) < 16
    qw = N_HEADS * HEAD_DIM
    for c in range(qw // LANES):
        qc = p[:, c * LANES:(c + 1) * LANES]
        q_ref[0, :, c * LANES:(c + 1) * LANES] = (
            _rope(qc, cos, sin, first_half) * (LOG2E * HEAD_DIM ** -0.5)).astype(bf16)
    k_ref[0] = _rope(p[:, qw:qw + LANES], cos, sin, first_half).astype(bf16)
    v_ref[0] = p[:, qw + LANES:].astype(bf16)


def _inproj(x, mod, g, w, cos_t, sin_t):
    b, l, d = x.shape
    tm = PROJ_TILE
    n = w.shape[1]
    return pl.pallas_call(
        _inproj_kernel,
        grid=(b, l // tm),
        in_specs=[pl.BlockSpec((1, tm, d), lambda i, j: (i, j, 0)),
                  pl.BlockSpec((1, 6, d), lambda i, j: (i, 0, 0)),
                  pl.BlockSpec((1, d), lambda i, j: (0, 0)),
                  pl.BlockSpec((d, n), lambda i, j: (0, 0)),
                  pl.BlockSpec((tm, LANES), lambda i, j: (j, 0)),
                  pl.BlockSpec((tm, LANES), lambda i, j: (j, 0))],
        out_specs=[pl.BlockSpec((1, tm, N_HEADS * HEAD_DIM), lambda i, j: (i, j, 0)),
                   pl.BlockSpec((1, tm, LANES), lambda i, j: (i, j, 0)),
                   pl.BlockSpec((1, tm, LANES), lambda i, j: (i, j, 0))],
        out_shape=[jax.ShapeDtypeStruct((b, l, N_HEADS * HEAD_DIM), bf16),
                   jax.ShapeDtypeStruct((b, l, LANES), bf16),
                   jax.ShapeDtypeStruct((b, l, LANES), bf16)],
        compiler_params=_cparams(("parallel", "parallel")),
        name="inproj",
    )(x, mod, g, w, cos_t, sin_t)


def _ctxkv_kernel(x_ref, mod_ref, g_ref, w_ref, k_ref, v_ref):
    h = _norm_mod(x_ref[0], g_ref[...], mod_ref[0, 0:1, :], mod_ref[0, 1:2, :])
    p = jnp.dot(h.astype(bf16), w_ref[...], preferred_element_type=f32)
    k_ref[0] = p[:, :LANES].astype(bf16)
    v_ref[0] = p[:, LANES:].astype(bf16)


def _ctxkv(ctx, mod, g, w_kv, ctx_row):
    b, c, d = ctx.shape
    return pl.pallas_call(
        _ctxkv_kernel,
        grid=(b,),
        in_specs=[pl.BlockSpec((1, c, d), lambda i: (i, 0, 0)),
                  pl.BlockSpec((1, 6, d), lambda i: (ctx_row, 0, 0)),
                  pl.BlockSpec((1, d), lambda i: (0, 0)),
                  pl.BlockSpec((d, 2 * LANES), lambda i: (0, 0))],
        out_specs=[pl.BlockSpec((1, c, LANES), lambda i: (i, 0, 0)),
                   pl.BlockSpec((1, c, LANES), lambda i: (i, 0, 0))],
        out_shape=[jax.ShapeDtypeStruct((b, c, LANES), bf16),
                   jax.ShapeDtypeStruct((b, c, LANES), bf16)],
        compiler_params=_cparams(("parallel",)),
        name="ctxkv",
    )(ctx, mod, g, w_kv)


def _attn_kernel(seq_len, q_ref, kp_ref, km_ref, kn_ref, vp_ref, vm_ref, vn_ref,
                 ck_ref, cv_ref, sink_ref, o_ref, kext, vext):
    j = pl.program_id(1)
    tq = ATT_TILE
    blk = ATT_BLOCK
    kext[0:blk] = kp_ref[0]
    kext[blk:blk + tq] = km_ref[0]
    kext[blk + tq:] = kn_ref[0]
    vext[:, LANES:] = jnp.ones((tq + 2 * blk, LANES), bf16)
    vext[0:blk, :LANES] = vp_ref[0]
    vext[blk:blk + tq, :LANES] = vm_ref[0]
    vext[blk + tq:, :LANES] = vn_ref[0]
    n_ctx = ck_ref.shape[1]
    nk = n_ctx + 3 * blk
    n_chunks = (N_HEADS * HEAD_DIM) // LANES
    rows = n_chunks * blk
    half = HEAD_DIM
    klow = lax.broadcasted_iota(i32, (1, LANES), 1) < half
    vlane = lax.broadcasted_iota(i32, (1, 2 * LANES), 1)
    vlow = (vlane < half) | ((vlane >= LANES) & (vlane < LANES + half))
    zero = jnp.zeros((), bf16)
    ck = ck_ref[0]
    cvx = jnp.concatenate([cv_ref[0], jnp.ones((n_ctx, LANES), bf16)], axis=1)
    ck_lo, ck_hi = jnp.where(klow, ck, zero), jnp.where(klow, zero, ck)
    cv_lo, cv_hi = jnp.where(vlow, cvx, zero), jnp.where(vlow, zero, cvx)
    sink2 = sink_ref[...]
    sink_lo, sink_hi = sink2[:, 0:1], sink2[:, half:half + 1]
    low = lax.broadcasted_iota(i32, (rows, LANES), 1) < half
    qi = lax.broadcasted_iota(i32, (rows, 3 * blk), 0) % blk
    pk = lax.broadcasted_iota(i32, (rows, 3 * blk), 1)
    band_bias = jnp.where(jnp.abs(pk - blk - qi) <= WINDOW, 0.0, NEG_INF).astype(f32)
    pcol = lax.broadcasted_iota(i32, (1, 3 * blk), 1)
    nt = (((1,), (1,)), ((), ()))

    def row_max(s):
        blocks = [s[:, i:i + LANES] for i in range(0, s.shape[1], LANES)]
        return jnp.max(functools.reduce(jnp.maximum, blocks), axis=1, keepdims=True)

    def sub(s, carry):
        r0 = pl.multiple_of(s * blk, blk)
        qs = q_ref[0, pl.ds(r0, blk), :]
        lhs = jnp.concatenate([qs[:, c * LANES:(c + 1) * LANES] for c in range(n_chunks)], axis=0)
        kl = kext[pl.ds(r0, 3 * blk), :]
        vl = vext[pl.ds(r0, 3 * blk), :]
        kbd = jnp.concatenate([ck_lo, jnp.where(klow, kl, zero),
                               ck_hi, jnp.where(klow, zero, kl)], axis=0)
        vbd = jnp.concatenate([cv_lo, jnp.where(vlow, vl, zero),
                               cv_hi, jnp.where(vlow, zero, vl)], axis=0)
        kpos = j * tq + r0 - blk + pcol
        bias = band_bias + jnp.where((kpos >= 0) & (kpos < seq_len), 0.0, NEG_INF).astype(f32)
        sc = lax.dot_general(lhs, kbd, nt, preferred_element_type=f32)
        s_lo = jnp.concatenate([sc[:, :n_ctx], sc[:, n_ctx:nk] + bias], axis=1)
        s_hi = jnp.concatenate([sc[:, nk:nk + n_ctx], sc[:, nk + n_ctx:] + bias], axis=1)
        m_lo = jnp.maximum(row_max(s_lo), sink_lo)
        m_hi = jnp.maximum(row_max(s_hi), sink_hi)
        e = jnp.concatenate([jnp.exp2(s_lo - m_lo), jnp.exp2(s_hi - m_hi)], axis=1).astype(bf16)
        ov = jnp.dot(e, vbd, preferred_element_type=f32)
        den = ov[:, LANES:] + jnp.exp2(sink2 - jnp.where(low, m_lo, m_hi))
        o = (ov[:, :LANES] / den).astype(bf16)
        for c in range(n_chunks):
            o_ref[0, pl.ds(r0, blk), c * LANES:(c + 1) * LANES] = o[c * blk:(c + 1) * blk]
        return carry

    lax.fori_loop(0, tq // blk, sub, 0, unroll=8)


def _attention(q, k, v, ck, cv, sinkcol):
    b, l, qw = q.shape
    c = ck.shape[1]
    tq = ATT_TILE
    r = tq // ATT_BLOCK
    nb = l // ATT_BLOCK
    prev = pl.BlockSpec((1, ATT_BLOCK, LANES), lambda i, j: (i, jnp.maximum(j * r - 1, 0), 0))
    main = pl.BlockSpec((1, tq, LANES), lambda i, j: (i, j, 0))
    nxt = pl.BlockSpec((1, ATT_BLOCK, LANES), lambda i, j: (i, jnp.minimum(j * r + r, nb - 1), 0))
    cspec = pl.BlockSpec((1, c, LANES), lambda i, j: (i, 0, 0))
    return pl.pallas_call(
        functools.partial(_attn_kernel, l),
        grid=(b, l // tq),
        in_specs=[pl.BlockSpec((1, tq, qw), lambda i, j: (i, j, 0)),
                  prev, main, nxt, prev, main, nxt, cspec, cspec,
                  pl.BlockSpec(sinkcol.shape, lambda i, j: (0, 0))],
        out_specs=pl.BlockSpec((1, tq, qw), lambda i, j: (i, j, 0)),
        out_shape=jax.ShapeDtypeStruct((b, l, qw), bf16),
        scratch_shapes=[pltpu.VMEM((tq + 2 * ATT_BLOCK, LANES), bf16),
                        pltpu.VMEM((tq + 2 * ATT_BLOCK, 2 * LANES), bf16)],
        compiler_params=_cparams(("parallel", "parallel")),
        name="attention",
    )(q, k, k, k, v, v, v, ck, cv, sinkcol)


def _pack_pair(lo, hi):
    lo = lax.bitcast_convert_type(lo.astype(bf16).astype(f32), u32)
    hi = lax.bitcast_convert_type(hi.astype(bf16).astype(f32), u32)
    return (lo >> 16) | (hi & jnp.uint32(0xFFFF0000))


def _fourier1_kernel(x_ref, mod_ref, g_ref, w_ref, cs_ref, m_ref, ct_ref, st_ref, z_ref, ab_ref):
    n1 = x_ref.shape[1]
    nt = x_ref.shape[2]
    x = x_ref[0].reshape(n1 * nt, x_ref.shape[3])
    h = _norm_mod(x, g_ref[...], mod_ref[0, 0:1, :], mod_ref[0, 1:2, :]).astype(bf16)
    p = jnp.dot(h, w_ref[...], preferred_element_type=f32)
    for g in range(FOURIER_GROUPS):
        ug = p[:, g * LANES:(g + 1) * LANES].astype(bf16)
        ab = jnp.dot(ug, cs_ref[...], preferred_element_type=f32)
        ab_ref[0] = ab[:, :LANES]
        ab_ref[1] = ab[:, LANES:]
        for t in range(nt):
            stack = jnp.concatenate([ab_ref[0, pl.ds(t, n1, stride=nt), :],
                                     ab_ref[1, pl.ds(t, n1, stride=nt), :]], axis=0).astype(bf16)
            z = jnp.dot(m_ref[...], stack, preferred_element_type=f32)
            zr, zn = z[:n1], z[n1:]
            ct, st = ct_ref[t], st_ref[t]
            z_ref[0, g, t] = _pack_pair(ct * zr - st * zn, ct * zn + st * zr)


def _fourier2_kernel(scale, z_ref, m_ref, o_ref, zbuf, ybuf):
    _, grp, n2, tk, w = z_ref.shape
    for g in range(grp):
        zbuf[...] = z_ref[0, g].reshape(n2 * tk, w)
        for j in range(tk):
            zp = zbuf[pl.ds(j, n2, stride=tk), :]
            zr = lax.bitcast_convert_type(zp << 16, f32).astype(bf16)
            zn = lax.bitcast_convert_type(zp & jnp.uint32(0xFFFF0000), f32).astype(bf16)
            y = jnp.dot(m_ref[...], jnp.concatenate([zr, zn], axis=0), preferred_element_type=f32)
            ybuf[pl.ds(j, n2, stride=tk), :] = y * scale
        o_ref[0, g] = ybuf[...].reshape(n2, tk, w)


def _fourier(x, mod, g, w_f, cs):
    b, l, d = x.shape
    n2 = DFT_INNER
    n1 = l // n2
    grp, w = FOURIER_GROUPS, FOURIER_GROUP_W
    t2 = SUBLANES
    k1 = jnp.arange(n1, dtype=i32)
    ang1 = ((k1[:, None] * k1[None, :]) % n1).astype(f32) * (2.0 * math.pi / n1)
    c1, s1 = jnp.cos(ang1), jnp.sin(ang1)
    m1 = jnp.concatenate([jnp.concatenate([c1, -s1], axis=1),
                          jnp.concatenate([s1, c1], axis=1)], axis=0).astype(bf16)
    l2 = jnp.arange(n2, dtype=i32)
    angt = ((l2[:, None] * k1[None, :]) % l).astype(f32) * (2.0 * math.pi / l)
    ct = jnp.broadcast_to(jnp.cos(angt)[:, :, None], (n2, n1, w))
    st = jnp.broadcast_to(jnp.sin(angt)[:, :, None], (n2, n1, w))
    ang2 = ((l2[:, None] * l2[None, :]) % n2).astype(f32) * (2.0 * math.pi / n2)
    m2 = jnp.concatenate([jnp.cos(ang2), -jnp.sin(ang2)], axis=1).astype(bf16)

    tspec = pl.BlockSpec((t2, n1, w), lambda t, i: (t, 0, 0))
    z = pl.pallas_call(
        _fourier1_kernel,
        grid=(n2 // t2, b),
        in_specs=[pl.BlockSpec((1, n1, t2, d), lambda t, i: (i, 0, t, 0)),
                  pl.BlockSpec((1, 6, d), lambda t, i: (i, 0, 0)),
                  pl.BlockSpec((1, d), lambda t, i: (0, 0)),
                  pl.BlockSpec(w_f.shape, lambda t, i: (0, 0)),
                  pl.BlockSpec(cs.shape, lambda t, i: (0, 0)),
                  pl.BlockSpec(m1.shape, lambda t, i: (0, 0)), tspec, tspec],
        out_specs=pl.BlockSpec((1, grp, t2, n1, w), lambda t, i: (i, 0, t, 0, 0)),
        out_shape=jax.ShapeDtypeStruct((b, grp, n2, n1, w), u32),
        scratch_shapes=[pltpu.VMEM((2, n1 * t2, w), f32)],
        compiler_params=_cparams(("parallel", "parallel")),
        name="fourier_outer",
    )(x.reshape(b, n1, n2, d), mod, g, w_f, cs, m1, ct, st)

    tk = SUBLANES
    y = pl.pallas_call(
        functools.partial(_fourier2_kernel, 1.0 / math.sqrt(l * w)),
        grid=(b, n1 // tk),
        in_specs=[pl.BlockSpec((1, grp, n2, tk, w), lambda i, t: (i, 0, 0, t, 0)),
                  pl.BlockSpec(m2.shape, lambda i, t: (0, 0))],
        out_specs=pl.BlockSpec((1, grp, n2, tk, w), lambda i, t: (i, 0, 0, t, 0)),
        out_shape=jax.ShapeDtypeStruct((b, grp, n2, n1, w), f32),
        scratch_shapes=[pltpu.VMEM((n2 * tk, w), u32), pltpu.VMEM((n2 * tk, w), f32)],
        compiler_params=_cparams(("parallel", "parallel")),
        name="fourier_inner",
    )(z, m2)
    return y.reshape(b, grp, l, w)


def _first_max4(a):
    m = jnp.maximum(jnp.maximum(a[0], a[1]), jnp.maximum(a[2], a[3]))
    idx = jnp.where(a[0] == m, 0, jnp.where(a[1] == m, 1, jnp.where(a[2] == m, 2, 3)))
    return m, idx


def _pick4(vals, idx):
    return jnp.where(idx == 0, vals[0], jnp.where(idx == 1, vals[1],
                                                   jnp.where(idx == 2, vals[2], vals[3])))


def _route(f, rw_ref, rb_ref, tri_ref, base_ref, first_step, ri_ref, wc_ref, cnt_ref, meta_ref):
    tm = f.shape[0]
    f_hi = f.astype(bf16)
    f_lo = (f - f_hi.astype(f32)).astype(bf16)
    rw2 = rw_ref[...]
    part = jnp.dot(f_hi, rw2, preferred_element_type=f32)
    logits = (part[:, :LANES] + part[:, LANES:]
              + jnp.dot(f_lo, rw2[:, :LANES], preferred_element_type=f32))
    sc = jax.nn.sigmoid(logits)
    st = sc.T
    bt = (sc + rb_ref[...]).T
    neg = jnp.full((1, tm), -jnp.inf, f32)
    gs = []
    for g in range(N_GROUPS):
        a = [bt[4 * g + i: 4 * g + i + 1] for i in range(4)]
        m1, i1 = _first_max4(a)
        rest = [jnp.where(i1 == i, neg, a[i]) for i in range(4)]
        m2, _ = _first_max4(rest)
        gs.append(m1 + m2)
    _, gsel = _first_max4(gs)
    a = [_pick4([bt[4 * g + i: 4 * g + i + 1] for g in range(N_GROUPS)], gsel) for i in range(4)]
    s = [_pick4([st[4 * g + i: 4 * g + i + 1] for g in range(N_GROUPS)], gsel) for i in range(4)]
    _, i1 = _first_max4(a)
    rest = [jnp.where(i1 == i, neg, a[i]) for i in range(4)]
    _, i2 = _first_max4(rest)
    w1 = _pick4(s, i1)
    w2 = _pick4(s, i2)
    tot = w1 + w2
    w1 = w1 / tot
    w2 = w2 / tot
    e0 = gsel * EXPERTS_PER_GROUP + i1
    e1 = gsel * EXPERTS_PER_GROUP + i2

    @pl.when(first_step)
    def _():
        base_ref[...] = jnp.zeros_like(base_ref)

    td = DISPATCH_TILE
    eid = lax.broadcasted_iota(i32, (N_EXPERTS, tm), 0)
    oh0 = (eid == e0).astype(f32)
    oh1 = (eid == e1).astype(f32)
    oh = oh0 + oh1
    before = jnp.dot(oh.astype(bf16), tri_ref[...], preferred_element_type=f32)
    lane_tile = lax.broadcasted_iota(i32, (N_EXPERTS, tm), 1) // td
    ei = lax.broadcasted_iota(i32, (N_EXPERTS, N_EXPERTS), 0)
    ej = lax.broadcasted_iota(i32, (N_EXPERTS, N_EXPERTS), 1)
    strict_lower = (ej < ei).astype(f32)
    run_start = jnp.zeros((N_EXPERTS, tm), f32)
    goff = base_ref[...]
    for s in range(tm // td):
        cnt_s = jnp.sum(oh[:, s * td:(s + 1) * td], axis=1, keepdims=True)
        pad_s = jnp.floor((cnt_s + 7.0) * 0.125) * 8.0
        pad_b = jnp.broadcast_to(pad_s, (N_EXPERTS, LANES))
        start_b = jnp.dot(strict_lower, pad_b, precision=HIGHEST, preferred_element_type=f32)
        run_start = jnp.where(lane_tile == s, start_b[:, 0:1], run_start)
        meta_ref[s, 0] = start_b.astype(i32)
        meta_ref[s, 1] = pad_b.astype(i32)
        meta_ref[s, 2] = goff.astype(i32)
        goff = goff + pad_b
    base_ref[...] = goff
    cnt_ref[...] = goff
    pos = before + run_start
    lp0 = jnp.sum(oh0 * pos, axis=0, keepdims=True)
    lp1 = jnp.sum(oh1 * pos, axis=0, keepdims=True)
    zi = jnp.zeros((1, tm), i32)
    ri_ref[...] = jnp.concatenate(
        [lp0.astype(i32), lp1.astype(i32), e0, e1,
         lax.bitcast_convert_type(w1, i32), lax.bitcast_convert_type(w2, i32), zi, zi], axis=0)
    zf = jnp.zeros((LANES - 4, tm), f32)
    wc_ref[...] = jnp.concatenate([w1, w2, lp0, lp1, zf], axis=0).T


def _outproj_kernel(yf_ref, o_ref, x_ref, mod_ref, w_ref, g_ref, rw_ref, rb_ref, tri_ref,
                    x1_ref, f_ref, ri_ref, wc_ref, cnt_ref, meta_ref, base_ref):
    mix = jnp.concatenate([yf_ref[0, g].astype(bf16) for g in range(FOURIER_GROUPS)] + [o_ref[0]],
                          axis=1)
    y = jnp.dot(mix, w_ref[...], preferred_element_type=f32)
    x1 = x_ref[0] + mod_ref[0, 2:3, :] * y
    x1_ref[0] = x1
    f = _norm_mod(x1, g_ref[...], mod_ref[0, 3:4, :], mod_ref[0, 4:5, :])
    f_ref[0] = f.astype(bf16)
    first = (pl.program_id(0) == 0) & (pl.program_id(1) == 0)
    _route(f, rw_ref, rb_ref, tri_ref, base_ref, first, ri_ref, wc_ref, cnt_ref, meta_ref)


def _before_in_tile(tm):
    tpos = jnp.arange(tm)
    return ((tpos[:, None] < tpos[None, :])
            & (tpos[:, None] // DISPATCH_TILE == tpos[None, :] // DISPATCH_TILE)).astype(bf16)


def _route_specs(b, l, tm):
    nl = l // tm
    rw = lambda d: pl.BlockSpec((d, 2 * LANES), lambda i, j: (0, 0))
    rb = pl.BlockSpec((1, LANES), lambda i, j: (0, 0))
    tri = pl.BlockSpec((tm, tm), lambda i, j: (0, 0))
    ns = tm // DISPATCH_TILE
    out_specs = [pl.BlockSpec((8, tm), lambda i, j: (0, i * nl + j)),
                 pl.BlockSpec((tm, LANES), lambda i, j: (i * nl + j, 0)),
                 pl.BlockSpec((N_EXPERTS, LANES), lambda i, j: (0, 0)),
                 pl.BlockSpec((ns, 3, N_EXPERTS, LANES), lambda i, j: (i * nl + j, 0, 0, 0))]
    out_shape = [jax.ShapeDtypeStruct((8, b * l), i32),
                 jax.ShapeDtypeStruct((b * l, LANES), f32),
                 jax.ShapeDtypeStruct((N_EXPERTS, LANES), f32),
                 jax.ShapeDtypeStruct((b * l // DISPATCH_TILE, 3, N_EXPERTS, LANES), i32)]
    return rw, rb, tri, out_specs, out_shape


def _outproj(yf, o, x, mod, w, g, rw, rb):
    b, l, d = x.shape
    tm = PROJ_TILE
    tri = _before_in_tile(tm)
    rws, rbs, tris, r_specs, r_shapes = _route_specs(b, l, tm)
    row = pl.BlockSpec((1, tm, d), lambda i, j: (i, j, 0))
    return pl.pallas_call(
        _outproj_kernel,
        grid=(b, l // tm),
        in_specs=[pl.BlockSpec((1, FOURIER_GROUPS, tm, LANES), lambda i, j: (i, 0, j, 0)),
                  pl.BlockSpec((1, tm, o.shape[2]), lambda i, j: (i, j, 0)),
                  row,
                  pl.BlockSpec((1, 6, d), lambda i, j: (i, 0, 0)),
                  pl.BlockSpec(w.shape, lambda i, j: (0, 0)),
                  pl.BlockSpec((1, d), lambda i, j: (0, 0)),
                  rws(d), rbs, tris],
        out_specs=[row, row] + r_specs,
        out_shape=[jax.ShapeDtypeStruct((b, l, d), f32),
                   jax.ShapeDtypeStruct((b, l, d), bf16)] + r_shapes,
        scratch_shapes=[pltpu.VMEM((N_EXPERTS, LANES), f32)],
        compiler_params=_cparams(("arbitrary", "arbitrary")),
        name="outproj_router",
    )(yf, o, x, mod, w, g, rw, rb, tri)


def _conv_kernel(seq_len, up_ref, um_ref, un_ref, x_ref, mod_ref, dw_ref, db_ref, lg_ref, lb_ref,
                 w_ref, pb_ref, g_ref, rw_ref, rb_ref, tri_ref,
                 x1_ref, f_ref, ri_ref, wc_ref, cnt_ref, meta_ref, base_ref, ext, conv_out):
    j = pl.program_id(1)
    tm = um_ref.shape[1]
    hl = CONV_HALO
    half = CONV_W // 2
    ext[0:hl] = jnp.where(j > 0, up_ref[0], jnp.zeros_like(up_ref[0]))
    ext[hl:hl + tm] = um_ref[0]
    ext[hl + tm:] = jnp.where((j + 1) * tm < seq_len, un_ref[0], jnp.zeros_like(un_ref[0]))
    base = hl - half
    span = (CONV_W - 1) // SUBLANES * SUBLANES
    rows = CONV_ROWS

    def lane_chunk(c, carry):
        lanes = pl.ds(pl.multiple_of(c * LANES, LANES), LANES)
        for r in range(0, tm, rows):
            part = jnp.broadcast_to(db_ref[:, lanes], (rows, LANES))
            for phase in range(SUBLANES):
                win = ext[base + phase + r: base + phase + r + rows + span, lanes]
                same = None
                for t in range(phase, CONV_W, SUBLANES):
                    term = win[t - phase: t - phase + rows, :] * dw_ref[t:t + 1, lanes]
                    same = term if same is None else same + term
                part = part + same
            conv_out[r:r + rows, lanes] = part
        return carry

    lax.fori_loop(0, um_ref.shape[2] // LANES, lane_chunk, 0)
    acc = conv_out[...]
    mu = jnp.mean(acc, axis=-1, keepdims=True)
    cen = acc - mu
    var = jnp.mean(cen * cen, axis=-1, keepdims=True)
    ln = cen * lax.rsqrt(var + EPS) * lg_ref[...] + lb_ref[...]
    act = ln * jax.nn.sigmoid(ln)
    y = jnp.dot(act.astype(bf16), w_ref[...], preferred_element_type=f32) + pb_ref[...]
    x1 = x_ref[0] + mod_ref[0, 2:3, :] * y
    x1_ref[0] = x1
    f = _norm_mod(x1, g_ref[...], mod_ref[0, 3:4, :], mod_ref[0, 4:5, :])
    f_ref[0] = f.astype(bf16)
    first = (pl.program_id(0) == 0) & (j == 0)
    _route(f, rw_ref, rb_ref, tri_ref, base_ref, first, ri_ref, wc_ref, cnt_ref, meta_ref)


def _conv(u, x, mod, dw_w, dw_b, ln_g, ln_b, pw2_w, pw2_b, g, rw, rb):
    b, l, d = x.shape
    tm = TOKEN_TILE
    tri = _before_in_tile(tm)
    hl = CONV_HALO
    r = tm // hl
    nh = l // hl
    rws, rbs, tris, r_specs, r_shapes = _route_specs(b, l, tm)
    row = pl.BlockSpec((1, tm, d), lambda i, j: (i, j, 0))
    vec = pl.BlockSpec((1, d), lambda i, j: (0, 0))
    return pl.pallas_call(
        functools.partial(_conv_kernel, l),
        grid=(b, l // tm),
        in_specs=[pl.BlockSpec((1, hl, d), lambda i, j: (i, jnp.maximum(j * r - 1, 0), 0)),
                  row,
                  pl.BlockSpec((1, hl, d), lambda i, j: (i, jnp.minimum(j * r + r, nh - 1), 0)),
                  row,
                  pl.BlockSpec((1, 6, d), lambda i, j: (i, 0, 0)),
                  pl.BlockSpec(dw_w.shape, lambda i, j: (0, 0)),
                  vec, vec, vec,
                  pl.BlockSpec(pw2_w.shape, lambda i, j: (0, 0)),
                  vec, vec, rws(d), rbs, tris],
        out_specs=[row, row] + r_specs,
        out_shape=[jax.ShapeDtypeStruct((b, l, d), f32),
                   jax.ShapeDtypeStruct((b, l, d), bf16)] + r_shapes,
        scratch_shapes=[pltpu.VMEM((N_EXPERTS, LANES), f32),
                        pltpu.VMEM((tm + 2 * hl, d), f32),
                        pltpu.VMEM((tm, d), f32)],
        compiler_params=_cparams(("arbitrary", "arbitrary")),
        name="conv_router",
    )(u, u, u, x, mod, dw_w, dw_b, ln_g, ln_b, pw2_w, pw2_b, g, rw, rb, tri)


def _pack_bf16_pairs(x):
    h = x.shape[1] // 2
    lo = lax.bitcast_convert_type(x[:, :h], u32)
    hi = lax.bitcast_convert_type(x[:, h:], u32)
    return (lo >> 16) | (hi & jnp.uint32(0xFFFF0000))


def _unpack_bf16_pairs(u):
    lo = lax.bitcast_convert_type(u << 16, f32)
    hi = lax.bitcast_convert_type(u & jnp.uint32(0xFFFF0000), f32)
    return jnp.concatenate([lo, hi], axis=1).astype(bf16)


def _run_copies(meta, tile, local_ref, hbm_ref, sem, to_hbm):
    start_ref, size_ref, dst_ref = meta
    for e in range(N_EXPERTS):
        k = tile * N_EXPERTS + e
        size = pl.multiple_of(size_ref[k], RUN_ALIGN)

        @pl.when(size > 0)
        def _():
            loc = local_ref.at[pl.ds(pl.multiple_of(start_ref[k], RUN_ALIGN), size)]
            glob = hbm_ref.at[pl.ds(pl.multiple_of(dst_ref[k], RUN_ALIGN), size)]
            if to_hbm:
                pltpu.make_async_copy(loc, glob, sem).start()
            else:
                pltpu.make_async_copy(glob, loc, sem).start()


def _wait_rows(rows, local_ref, hbm_ref, sem):
    rows = pl.multiple_of(rows, RUN_ALIGN)

    @pl.when(rows > 0)
    def _():
        pltpu.make_async_copy(local_ref.at[pl.ds(0, rows)], hbm_ref.at[pl.ds(0, rows)], sem).wait()


def _dispatch_kernel(start_ref, size_ref, dst_ref, tot_ref, tail_start_ref, tail_size_ref, nv_ref,
                     f_ref, lp_ref, xs_ref, loc, zbuf, sem, zsem):
    i = pl.program_id(0)
    n = pl.num_programs(0)
    slot = i % 2
    subs = loc.shape[1]
    rows = loc.shape[2]
    td = f_ref.shape[0] // subs
    meta = (start_ref, size_ref, dst_ref)

    def drain(step, which):
        for s in range(subs):
            _wait_rows(tot_ref[step * subs + s], loc.at[which, s], xs_ref, sem.at[which])

    @pl.when(i >= 2)
    def _():
        drain(i - 2, slot)

    r = lax.broadcasted_iota(i32, (rows, td), 0)
    for s in range(subs):
        cols = slice(s * td, (s + 1) * td)
        pick0 = r == lp_ref[0:1, cols]
        pick1 = r == lp_ref[1:2, cols]
        onehot = (pick0 | pick1).astype(bf16)
        sorted_rows = jnp.dot(onehot, f_ref[cols, :], preferred_element_type=f32)
        half = sorted_rows.shape[1] // 2
        loc[slot, s, :, :half] = _pack_bf16_pairs(sorted_rows)
        w0 = lax.bitcast_convert_type(lp_ref[4:5, cols], f32)
        w1 = lax.bitcast_convert_type(lp_ref[5:6, cols], f32)
        row_w = jnp.sum(jnp.where(pick0, w0, 0.0) + jnp.where(pick1, w1, 0.0), axis=1, keepdims=True)
        loc[slot, s, :, half:] = jnp.broadcast_to(lax.bitcast_convert_type(row_w, u32), (rows, LANES))
        _run_copies(meta, i * subs + s, loc.at[slot, s], xs_ref, sem.at[slot], to_hbm=True)

    @pl.when(i == n - 1)
    def _():
        zbuf[...] = jnp.zeros_like(zbuf)
        total = 0
        for e in range(N_EXPERTS):
            size = pl.multiple_of(tail_size_ref[e], RUN_ALIGN)
            total = total + size

            @pl.when(size > 0)
            def _():
                pltpu.make_async_copy(
                    zbuf.at[pl.ds(0, size)],
                    xs_ref.at[pl.ds(pl.multiple_of(tail_start_ref[e], RUN_ALIGN), size)], zsem).start()

        _wait_rows(total, zbuf, xs_ref, zsem)

        def zero_block(k, c):
            pltpu.make_async_copy(zbuf, xs_ref.at[pl.ds(pl.multiple_of(k * zbuf.shape[0], RUN_ALIGN),
                                                        zbuf.shape[0])], zsem).start()
            return c

        def wait_block(k, c):
            pltpu.make_async_copy(zbuf, xs_ref.at[pl.ds(0, zbuf.shape[0])], zsem).wait()
            return c

        n_blocks = xs_ref.shape[0] // zbuf.shape[0]
        lax.fori_loop(nv_ref[0], n_blocks, zero_block, 0)
        lax.fori_loop(nv_ref[0], n_blocks, wait_block, 0)
        drain(i, slot)

        @pl.when(i >= 1)
        def _():
            drain(i - 1, 1 - slot)


def _dispatch(tables, f2, ri, n_slots):
    t, d = f2.shape
    subs = DISPATCH_SUBTILES
    tm = DISPATCH_TILE * subs
    return pl.pallas_call(
        _dispatch_kernel,
        grid_spec=pltpu.PrefetchScalarGridSpec(
            num_scalar_prefetch=7,
            grid=(t // tm,),
            in_specs=[pl.BlockSpec((tm, d), lambda i, *_: (i, 0)),
                      pl.BlockSpec((8, tm), lambda i, *_: (0, i))],
            out_specs=pl.BlockSpec(memory_space=pl.ANY),
            scratch_shapes=[pltpu.VMEM((2, subs, LOCAL_ROWS, d // 2 + LANES), u32),
                            pltpu.VMEM((EXPERT_ROWS, d // 2 + LANES), u32),
                            pltpu.SemaphoreType.DMA((2,)), pltpu.SemaphoreType.DMA(())]),
        out_shape=jax.ShapeDtypeStruct((n_slots, d // 2 + LANES), u32),
        compiler_params=_cparams(("arbitrary",)),
        name="moe_dispatch",
    )(*tables, f2, ri)


def _expert_kernel(be_ref, nv_ref, x_ref, wg_ref, wu_ref, wd_ref, y_ref, wgb, wub, wdb):
    i = pl.program_id(0)
    changed = jnp.logical_or(i == 0, be_ref[i] != be_ref[jnp.maximum(i - 1, 0)])

    @pl.when(changed)
    def _():
        wgb[...] = wg_ref[0, 0].astype(bf16)
        wub[...] = wu_ref[0, 0].astype(bf16)
        wdb[...] = wd_ref[0, 0].astype(bf16)

    @pl.when(i < nv_ref[0])
    def _():
        half = y_ref.shape[1]
        xb = _unpack_bf16_pairs(x_ref[:, :half])
        row_w = lax.bitcast_convert_type(x_ref[:, half:], f32)
        gate = jnp.dot(xb, wgb[...], preferred_element_type=f32)
        up = jnp.dot(xb, wub[...], preferred_element_type=f32)
        hid = (gate * jax.nn.sigmoid(gate) * up).astype(bf16)
        y = jnp.dot(hid, wdb[...], preferred_element_type=f32)
        y = jnp.concatenate([y[:, c:c + LANES] * row_w for c in range(0, y.shape[1], LANES)], axis=1)
        y_ref[...] = _pack_bf16_pairs(y.astype(bf16).astype(f32))

    @pl.when(i >= nv_ref[0])
    def _():
        y_ref[...] = jnp.zeros_like(y_ref)


def _experts(block_e, n_valid, xs, w_gate, w_up, w_down, layer):
    ns, xw = xs.shape
    tb = EXPERT_ROWS
    d, ff = w_gate.shape[2:]
    dh = d // 2
    xmap = lambda i, be, nv: (jnp.maximum(jnp.minimum(i, nv[0] - 1), 0), 0)
    wmap = lambda i, be, nv: (layer, be[i], 0, 0)
    return pl.pallas_call(
        _expert_kernel,
        grid_spec=pltpu.PrefetchScalarGridSpec(
            num_scalar_prefetch=2,
            grid=(ns // tb,),
            in_specs=[pl.BlockSpec((tb, xw), xmap),
                      pl.BlockSpec((1, 1, d, ff), wmap),
                      pl.BlockSpec((1, 1, d, ff), wmap),
                      pl.BlockSpec((1, 1, ff, d), wmap)],
            out_specs=pl.BlockSpec((tb, dh), lambda i, be, nv: (i, 0)),
            scratch_shapes=[pltpu.VMEM((d, ff), bf16), pltpu.VMEM((d, ff), bf16),
                            pltpu.VMEM((ff, d), bf16)]),
        out_shape=jax.ShapeDtypeStruct((ns, dh), u32),
        compiler_params=_cparams(("arbitrary",)),
        name="moe_experts",
    )(block_e, n_valid, xs, w_gate, w_up, w_down)


def _combine_kernel(final, start_ref, size_ref, dst_ref, tot_ref, ys_ref, wc_ref, x_ref, mod_ref,
                    g_ref, *rest):
    if final:
        o_ref, loc, sem = rest
    else:
        nmod_ref, w_ref, b_ref, o_ref, u_ref, loc, sem = rest
    i = pl.program_id(0)
    n = pl.num_programs(0)
    slot = i % 2
    subs = loc.shape[1]
    rows = loc.shape[2]
    td = x_ref.shape[0] // subs
    meta = (start_ref, size_ref, dst_ref)

    def fetch(step, which):
        for s in range(subs):
            _run_copies(meta, step * subs + s, loc.at[which, s], ys_ref, sem.at[which, s],
                        to_hbm=False)

    @pl.when(i == 0)
    def _():
        loc[...] = jnp.zeros_like(loc)
        fetch(i, slot)

    @pl.when(i + 1 < n)
    def _():
        fetch(i + 1, 1 - slot)

    c = lax.broadcasted_iota(i32, (td, rows), 1)
    parts = []
    for s in range(subs):
        _wait_rows(tot_ref[i * subs + s], loc.at[slot, s], ys_ref, sem.at[slot, s])
        wc = wc_ref[s * td:(s + 1) * td, :]
        sel = ((c == wc[:, 2:3].astype(i32)) | (c == wc[:, 3:4].astype(i32))).astype(bf16)
        parts.append(jnp.dot(sel, _unpack_bf16_pairs(loc[slot, s]), preferred_element_type=f32))
    xo = x_ref[...] + mod_ref[0, 5:6, :] * jnp.concatenate(parts, axis=0)
    if final:
        ms = jnp.mean(xo * xo, axis=-1, keepdims=True)
        o_ref[...] = xo * lax.rsqrt(ms + EPS) * g_ref[...]
    else:
        o_ref[...] = xo
        h = _norm_mod(xo, g_ref[...], nmod_ref[0, 0:1, :], nmod_ref[0, 1:2, :])
        p = jnp.dot(h.astype(bf16), w_ref[...], preferred_element_type=f32) + b_ref[...]
        ch = p.shape[1] // 2
        u_ref[...] = p[:, :ch] * jax.nn.sigmoid(p[:, ch:])


def _combine(tables, ys, wc, x, mod, g, glu=None):
    b, l, d = x.shape
    subs = DISPATCH_SUBTILES if glu is None else GLU_SUBTILES
    tm = DISPATCH_TILE * subs
    per_batch = l // tm
    const = lambda i, *_: (0, 0)
    tile = lambda cols: pl.BlockSpec((tm, cols), lambda i, *_: (i, 0))
    mod_spec = pl.BlockSpec((1, 6, d), lambda i, *_: (i // per_batch, 0, 0))
    in_specs = [pl.BlockSpec(memory_space=pl.ANY), tile(LANES), tile(d), mod_spec,
                pl.BlockSpec((1, d), const)]
    args = [ys, wc, x.reshape(b * l, d), mod, g]
    out_specs = [tile(d)]
    out_shape = [jax.ShapeDtypeStruct((b * l, d), f32)]
    if glu is not None:
        nmod, w, bias = glu
        in_specs += [mod_spec, pl.BlockSpec(w.shape, const), pl.BlockSpec(bias.shape, const)]
        args += [nmod, w, bias]
        out_specs.append(tile(w.shape[1] // 2))
        out_shape.append(jax.ShapeDtypeStruct((b * l, w.shape[1] // 2), f32))
    outs = pl.pallas_call(
        functools.partial(_combine_kernel, glu is None),
        grid_spec=pltpu.PrefetchScalarGridSpec(
            num_scalar_prefetch=4,
            grid=(b * per_batch,),
            in_specs=in_specs,
            out_specs=out_specs,
            scratch_shapes=[pltpu.VMEM((2, subs, LOCAL_ROWS, d // 2), u32),
                            pltpu.SemaphoreType.DMA((2, subs))]),
        out_shape=out_shape,
        compiler_params=_cparams(("arbitrary",)),
        name="moe_combine",
    )(*tables, *args)
    return [o.reshape(b, l, -1) for o in outs]


def _moe(f, routed, x, mod, g, w_gate, w_up, w_down, layer, glu=None):
    ri, wc, cnt, meta = routed
    b, l, d = x.shape
    t = b * l
    tb = EXPERT_ROWS
    n_tiles = t // DISPATCH_TILE
    used = cnt[:, 0].astype(i32)
    region = (used + tb - 1) // tb * tb
    gend = jnp.cumsum(region)
    gstart = gend - region
    max_rows = 2 * t + n_tiles * N_EXPERTS * (RUN_ALIGN - 1) + N_EXPERTS * (tb - 1)
    n_blocks = -(-max_rows // tb)
    m = meta[:, :, :, 0]
    run_start = m[:, 0].reshape(-1)
    run_size = m[:, 1].reshape(-1)
    run_dst = (m[:, 2] + gstart[None, :]).reshape(-1)
    tile_rows = jnp.sum(m[:, 1], axis=1)
    block_row = jnp.arange(n_blocks, dtype=i32) * tb
    block_e = jnp.minimum(jnp.sum((block_row[:, None] >= gend[None, :]).astype(i32), axis=1),
                          N_EXPERTS - 1)
    n_valid = (gend[-1] // tb).reshape(1)
    xs = _dispatch((run_start, run_size, run_dst, tile_rows, gstart + used, region - used, n_valid),
                   f.reshape(t, d), ri, n_blocks * tb)
    ys = _experts(block_e, n_valid, xs, w_gate, w_up, w_down, layer)
    return _combine((run_start, run_size, run_dst, tile_rows), ys, wc, x, mod, g, glu)


def _rope_tables(l):
    lane = jnp.arange(LANES)
    dh = lane % HEAD_DIM
    inv = ROPE_THETA ** (-(dh % 16).astype(f32) / 16.0)
    sign = jnp.where((dh % 32) < 16, -1.0, 1.0).astype(f32)
    by_row = (dh // 32)[None, None, :] == 0
    ang_r = jnp.arange(l // GRID_W, dtype=f32)[:, None] * inv[None, :]
    ang_c = jnp.arange(GRID_W, dtype=f32)[:, None] * inv[None, :]
    cos = jnp.where(by_row, jnp.cos(ang_r)[:, None, :], jnp.cos(ang_c)[None, :, :])
    sin = jnp.where(by_row, jnp.sin(ang_r)[:, None, :], jnp.sin(ang_c)[None, :, :])
    return cos.reshape(l, LANES), (sin * sign[None, None, :]).reshape(l, LANES)


def kernel(x, c, ctx, c_ctx, ada_w, ada_b, norm_mix_g, norm_ffn_g, even_w_in, even_w_out, even_sink, conv_pw1_w, conv_pw1_b, conv_dw_w, conv_dw_b, conv_ln_g, conv_ln_b, conv_pw2_w, conv_pw2_b, router_w, router_b, moe_w_gate, moe_w_up, moe_w_down, final_norm_g):
    b, l, d = x.shape
    depth = ada_w.shape[0]
    assert depth == 2 and b < COND_ROWS
    ctx_row = b
    cond = jnp.zeros((COND_ROWS, d), f32).at[:b].set(c).at[ctx_row].set(c_ctx)
    mods = _adaln(cond, ada_w, ada_b).reshape(depth, COND_ROWS, 6, d)

    heads = jnp.arange(N_HEADS).reshape(N_KV_HEADS, N_HEADS // N_KV_HEADS).T.reshape(-1)
    qperm = (heads[:, None] * HEAD_DIM + jnp.arange(HEAD_DIM)[None, :]).reshape(-1)
    fw = FOURIER_GROUPS * FOURIER_GROUP_W
    qw = N_HEADS * HEAD_DIM
    w_in = even_w_in[0]
    w_in_p = jnp.concatenate([w_in[:, :fw], w_in[:, fw:fw + qw][:, qperm], w_in[:, fw + qw:]],
                             axis=1).astype(bf16)
    w_out = even_w_out[0]
    w_out_p = jnp.concatenate([w_out[:fw], w_out[fw:][qperm]], axis=0).astype(bf16)
    sink_pairs = (even_sink[0].astype(f32) * LOG2E).reshape(N_KV_HEADS, N_HEADS // N_KV_HEADS).T
    sinkcol = jnp.repeat(jnp.repeat(sink_pairs, HEAD_DIM, axis=1), ATT_BLOCK, axis=0)

    cidx = jnp.arange(FOURIER_GROUP_W, dtype=i32)
    angc = ((cidx[:, None] * cidx[None, :]) % FOURIER_GROUP_W).astype(f32) * (2.0 * math.pi / FOURIER_GROUP_W)
    cs = jnp.concatenate([jnp.cos(angc), jnp.sin(angc)], axis=1).astype(bf16)
    cos_t, sin_t = _rope_tables(l)

    rw32 = jnp.zeros((d, LANES), f32).at[:, :N_EXPERTS].set(router_w.astype(f32))
    rw_hi = rw32.astype(bf16)
    rw = jnp.concatenate([rw_hi, (rw32 - rw_hi.astype(f32)).astype(bf16)], axis=1)
    rb = jnp.zeros((1, LANES), f32).at[0, :N_EXPERTS].set(router_b)
    row = lambda v: v.reshape(1, -1)

    q, k, v = _inproj(x, mods[0], row(norm_mix_g[0]), w_in_p[:, fw:], cos_t, sin_t)
    ck, cv = _ctxkv(ctx, mods[0], row(norm_mix_g[0]), w_in_p[:, fw + qw:], ctx_row)
    yf = _fourier(x, mods[0], row(norm_mix_g[0]), w_in_p[:, :fw], cs)
    att = _attention(q, k, v, ck, cv, sinkcol)
    x1, f, *routed = _outproj(yf, att, x, mods[0], w_out_p, row(norm_ffn_g[0]), rw, rb)
    x2, u = _moe(f, routed, x1, mods[0], row(norm_mix_g[1]), moe_w_gate, moe_w_up, moe_w_down,
                 layer=0, glu=(mods[1], conv_pw1_w[0].astype(bf16), row(conv_pw1_b[0])))

    x3, f, *routed = _conv(u, x2, mods[1], conv_dw_w[0], row(conv_dw_b[0]), row(conv_ln_g[0]),
                           row(conv_ln_b[0]), conv_pw2_w[0].astype(bf16), row(conv_pw2_b[0]),
                           row(norm_ffn_g[1]), rw, rb)
    (out,) = _moe(f, routed, x3, mods[1], row(final_norm_g), moe_w_gate, moe_w_up, moe_w_down,
                  layer=1)
    return out
```

```python
import functools
import math

import jax
import jax.numpy as jnp
from jax import lax
from jax.experimental import pallas as pl
from jax.experimental.pallas import tpu as pltpu

f32 = jnp.float32
bf16 = jnp.bfloat16
i32 = jnp.int32
u32 = jnp.uint32
HIGHEST = lax.Precision.HIGHEST

GRID_W = 64
HEAD_DIM = 64
N_HEADS = 8
N_KV_HEADS = 2
WINDOW = 128
ATT_BLOCK = 128
ROPE_THETA = 10000.0
FOURIER_GROUPS = 4
FOURIER_GROUP_W = 128
CONV_W = 31
N_EXPERTS = 16
N_GROUPS = 4
EXPERTS_PER_GROUP = 4
EXPERT_FF = 512
EPS = 1e-6
NEG_INF = -1e30
LOG2E = math.log2(math.e)

LANES = 128
SUBLANES = 8
COND_ROWS = 8
DFT_INNER = 64
TOKEN_TILE = 512
PROJ_TILE = 1024
ATT_TILE = 1024
EXPERT_ROWS = 512
DISPATCH_TILE = 256
DISPATCH_SUBTILES = 4
GLU_SUBTILES = 2
RUN_ALIGN = 8
LOCAL_ROWS = -(-(2 * DISPATCH_TILE + N_EXPERTS * (RUN_ALIGN - 1)) // LANES) * LANES
CONV_HALO = 16
CONV_ROWS = 128
VMEM_LIMIT = 56 * 1024 * 1024


def _cparams(sem, vmem=VMEM_LIMIT):
    return pltpu.CompilerParams(dimension_semantics=sem, vmem_limit_bytes=vmem)


def _adaln_kernel(cond_ref, w_ref, b_ref, o_ref):
    s = cond_ref[...]
    s = s * jax.nn.sigmoid(s)
    o_ref[0] = jnp.dot(s, w_ref[0], precision=HIGHEST, preferred_element_type=f32) + b_ref[0]


def _adaln(cond, ada_w, ada_b):
    depth, d, n = ada_w.shape
    tn = 1536
    return pl.pallas_call(
        _adaln_kernel,
        grid=(depth, n // tn),
        in_specs=[pl.BlockSpec((COND_ROWS, d), lambda i, j: (0, 0)),
                  pl.BlockSpec((1, d, tn), lambda i, j: (i, 0, j)),
                  pl.BlockSpec((1, 1, tn), lambda i, j: (i, 0, j))],
        out_specs=pl.BlockSpec((1, COND_ROWS, tn), lambda i, j: (i, 0, j)),
        out_shape=jax.ShapeDtypeStruct((depth, COND_ROWS, n), f32),
        compiler_params=_cparams(("arbitrary", "arbitrary")),
        name="adaln",
    )(cond, ada_w, ada_b.reshape(depth, 1, n))


def _norm_mod(x, g, shift, scale):
    ms = jnp.mean(x * x, axis=-1, keepdims=True)
    return (x * lax.rsqrt(ms + EPS)) * (g * (1.0 + scale)) + shift


def _rope(p, cos, sin_signed, first_half):
    rot = jnp.where(first_half, pltpu.roll(p, LANES - 16, axis=1), pltpu.roll(p, 16, axis=1))
    return p * cos + rot * sin_signed


def _inproj_kernel(x_ref, mod_ref, g_ref, w_ref, cos_ref, sin_ref, q_ref, k_ref, v_ref):
    h = _norm_mod(x_ref[0], g_ref[...], mod_ref[0, 0:1, :], mod_ref[0, 1:2, :])
    p = jnp.dot(h.astype(bf16), w_ref[...], preferred_element_type=f32)
    cos = cos_ref[...]
    sin = sin_ref[...]
    lane = lax.broadcasted_iota(i32, cos.shape, 1)
    first_half = (lane % 32) < 16
    qw = N_HEADS * HEAD_DIM
    for c in range(qw // LANES):
        qc = p[:, c * LANES:(c + 1) * LANES]
        q_ref[0, :, c * LANES:(c + 1) * LANES] = (
            _rope(qc, cos, sin, first_half) * (LOG2E * HEAD_DIM ** -0.5)).astype(bf16)
    k_ref[0] = _rope(p[:, qw:qw + LANES], cos, sin, first_half).astype(bf16)
    v_ref[0] = p[:, qw + LANES:].astype(bf16)


def _inproj(x, mod, g, w, cos_t, sin_t):
    b, l, d = x.shape
    tm = PROJ_TILE
    n = w.shape[1]
    return pl.pallas_call(
        _inproj_kernel,
        grid=(b, l // tm),
        in_specs=[pl.BlockSpec((1, tm, d), lambda i, j: (i, j, 0)),
                  pl.BlockSpec((1, 6, d), lambda i, j: (i, 0, 0)),
                  pl.BlockSpec((1, d), lambda i, j: (0, 0)),
                  pl.BlockSpec((d, n), lambda i, j: (0, 0)),
                  pl.BlockSpec((tm, LANES), lambda i, j: (j, 0)),
                  pl.BlockSpec((tm, LANES), lambda i, j: (j, 0))],
        out_specs=[pl.BlockSpec((1, tm, N_HEADS * HEAD_DIM), lambda i, j: (i, j, 0)),
                   pl.BlockSpec((1, tm, LANES), lambda i, j: (i, j, 0)),
                   pl.BlockSpec((1, tm, LANES), lambda i, j: (i, j, 0))],
        out_shape=[jax.ShapeDtypeStruct((b, l, N_HEADS * HEAD_DIM), bf16),
                   jax.ShapeDtypeStruct((b, l, LANES), bf16),
                   jax.ShapeDtypeStruct((b, l, LANES), bf16)],
        compiler_params=_cparams(("parallel", "parallel")),
        name="inproj",
    )(x, mod, g, w, cos_t, sin_t)


def _ctxkv_kernel(x_ref, mod_ref, g_ref, w_ref, k_ref, v_ref):
    h = _norm_mod(x_ref[0], g_ref[...], mod_ref[0, 0:1, :], mod_ref[0, 1:2, :])
    p = jnp.dot(h.astype(bf16), w_ref[...], preferred_element_type=f32)
    k_ref[0] = p[:, :LANES].astype(bf16)
    v_ref[0] = p[:, LANES:].astype(bf16)


def _ctxkv(ctx, mod, g, w_kv, ctx_row):
    b, c, d = ctx.shape
    return pl.pallas_call(
        _ctxkv_kernel,
        grid=(b,),
        in_specs=[pl.BlockSpec((1, c, d), lambda i: (i, 0, 0)),
                  pl.BlockSpec((1, 6, d), lambda i: (ctx_row, 0, 0)),
                  pl.BlockSpec((1, d), lambda i: (0, 0)),
                  pl.BlockSpec((d, 2 * LANES), lambda i: (0, 0))],
        out_specs=[pl.BlockSpec((1, c, LANES), lambda i: (i, 0, 0)),
                   pl.BlockSpec((1, c, LANES), lambda i: (i, 0, 0))],
        out_shape=[jax.ShapeDtypeStruct((b, c, LANES), bf16),
                   jax.ShapeDtypeStruct((b, c, LANES), bf16)],
        compiler_params=_cparams(("parallel",)),
        name="ctxkv",
    )(ctx, mod, g, w_kv)


def _attn_kernel(seq_len, q_ref, kp_ref, km_ref, kn_ref, vp_ref, vm_ref, vn_ref,
                 ck_ref, cv_ref, sink_ref, o_ref, kext, vext):
    j = pl.program_id(1)
    tq = ATT_TILE
    blk = ATT_BLOCK
    kext[0:blk] = kp_ref[0]
    kext[blk:blk + tq] = km_ref[0]
    kext[blk + tq:] = kn_ref[0]
    vext[:, LANES:] = jnp.ones((tq + 2 * blk, LANES), bf16)
    vext[0:blk, :LANES] = vp_ref[0]
    vext[blk:blk + tq, :LANES] = vm_ref[0]
    vext[blk + tq:, :LANES] = vn_ref[0]
    n_ctx = ck_ref.shape[1]
    nk = n_ctx + 3 * blk
    n_chunks = (N_HEADS * HEAD_DIM) // LANES
    rows = n_chunks * blk
    half = HEAD_DIM
    klow = lax.broadcasted_iota(i32, (1, LANES), 1) < half
    vlane = lax.broadcasted_iota(i32, (1, 2 * LANES), 1)
    vlow = (vlane < half) | ((vlane >= LANES) & (vlane < LANES + half))
    zero = jnp.zeros((), bf16)
    ck = ck_ref[0]
    cvx = jnp.concatenate([cv_ref[0], jnp.ones((n_ctx, LANES), bf16)], axis=1)
    ck_lo, ck_hi = jnp.where(klow, ck, zero), jnp.where(klow, zero, ck)
    cv_lo, cv_hi = jnp.where(vlow, cvx, zero), jnp.where(vlow, zero, cvx)
    sink2 = sink_ref[...]
    sink_lo, sink_hi = sink2[:, 0:1], sink2[:, half:half + 1]
    low = lax.broadcasted_iota(i32, (rows, LANES), 1) < half
    qi = lax.broadcasted_iota(i32, (rows, 3 * blk), 0) % blk
    pk = lax.broadcasted_iota(i32, (rows, 3 * blk), 1)
    band_bias = jnp.where(jnp.abs(pk - blk - qi) <= WINDOW, 0.0, NEG_INF).astype(f32)
    pcol = lax.broadcasted_iota(i32, (1, 3 * blk), 1)
    nt = (((1,), (1,)), ((), ()))

    def row_max(s):
        blocks = [s[:, i:i + LANES] for i in range(0, s.shape[1], LANES)]
        return jnp.max(functools.reduce(jnp.maximum, blocks), axis=1, keepdims=True)

    def sub(s, carry):
        r0 = pl.multiple_of(s * blk, blk)
        qs = q_ref[0, pl.ds(r0, blk), :]
        lhs = jnp.concatenate([qs[:, c * LANES:(c + 1) * LANES] for c in range(n_chunks)], axis=0)
        kl = kext[pl.ds(r0, 3 * blk), :]
        vl = vext[pl.ds(r0, 3 * blk), :]
        kbd = jnp.concatenate([ck_lo, jnp.where(klow, kl, zero),
                               ck_hi, jnp.where(klow, zero, kl)], axis=0)
        vbd = jnp.concatenate([cv_lo, jnp.where(vlow, vl, zero),
                               cv_hi, jnp.where(vlow, zero, vl)], axis=0)
        kpos = j * tq + r0 - blk + pcol
        bias = band_bias + jnp.where((kpos >= 0) & (kpos < seq_len), 0.0, NEG_INF).astype(f32)
        sc = lax.dot_general(lhs, kbd, nt, preferred_element_type=f32)
        s_lo = jnp.concatenate([sc[:, :n_ctx], sc[:, n_ctx:nk] + bias], axis=1)
        s_hi = jnp.concatenate([sc[:, nk:nk + n_ctx], sc[:, nk + n_ctx:] + bias], axis=1)
        m_lo = jnp.maximum(row_max(s_lo), sink_lo)
        m_hi = jnp.maximum(row_max(s_hi), sink_hi)
        e = jnp.concatenate([jnp.exp2(s_lo - m_lo), jnp.exp2(s_hi - m_hi)], axis=1).astype(bf16)
        ov = jnp.dot(e, vbd, preferred_element_type=f32)
        den = ov[:, LANES:] + jnp.exp2(sink2 - jnp.where(low, m_lo, m_hi))
        o = (ov[:, :LANES] / den).astype(bf16)
        for c in range(n_chunks):
            o_ref[0, pl.ds(r0, blk), c * LANES:(c + 1) * LANES] = o[c * blk:(c + 1) * blk]
        return carry

    lax.fori_loop(0, tq // blk, sub, 0, unroll=8)


def _attention(q, k, v, ck, cv, sinkcol):
    b, l, qw = q.shape
    c = ck.shape[1]
    tq = ATT_TILE
    r = tq // ATT_BLOCK
    nb = l // ATT_BLOCK
    prev = pl.BlockSpec((1, ATT_BLOCK, LANES), lambda i, j: (i, jnp.maximum(j * r - 1, 0), 0))
    main = pl.BlockSpec((1, tq, LANES), lambda i, j: (i, j, 0))
    nxt = pl.BlockSpec((1, ATT_BLOCK, LANES), lambda i, j: (i, jnp.minimum(j * r + r, nb - 1), 0))
    cspec = pl.BlockSpec((1, c, LANES), lambda i, j: (i, 0, 0))
    return pl.pallas_call(
        functools.partial(_attn_kernel, l),
        grid=(b, l // tq),
        in_specs=[pl.BlockSpec((1, tq, qw), lambda i, j: (i, j, 0)),
                  prev, main, nxt, prev, main, nxt, cspec, cspec,
                  pl.BlockSpec(sinkcol.shape, lambda i, j: (0, 0))],
        out_specs=pl.BlockSpec((1, tq, qw), lambda i, j: (i, j, 0)),
        out_shape=jax.ShapeDtypeStruct((b, l, qw), bf16),
        scratch_shapes=[pltpu.VMEM((tq + 2 * ATT_BLOCK, LANES), bf16),
                        pltpu.VMEM((tq + 2 * ATT_BLOCK, 2 * LANES), bf16)],
        compiler_params=_cparams(("parallel", "parallel")),
        name="attention",
    )(q, k, k, k, v, v, v, ck, cv, sinkcol)


def _pack_pair(lo, hi):
    lo = lax.bitcast_convert_type(lo.astype(bf16).astype(f32), u32)
    hi = lax.bitcast_convert_type(hi.astype(bf16).astype(f32), u32)
    return (lo >> 16) | (hi & jnp.uint32(0xFFFF0000))


def _fourier1_kernel(x_ref, mod_ref, g_ref, w_ref, cs_ref, m_ref, ct_ref, st_ref, z_ref, ab_ref):
    n1 = x_ref.shape[1]
    nt = x_ref.shape[2]
    x = x_ref[0].reshape(n1 * nt, x_ref.shape[3])
    h = _norm_mod(x, g_ref[...], mod_ref[0, 0:1, :], mod_ref[0, 1:2, :]).astype(bf16)
    p = jnp.dot(h, w_ref[...], preferred_element_type=f32)
    for g in range(FOURIER_GROUPS):
        ug = p[:, g * LANES:(g + 1) * LANES].astype(bf16)
        ab = jnp.dot(ug, cs_ref[...], preferred_element_type=f32)
        ab_ref[0] = ab[:, :LANES]
        ab_ref[1] = ab[:, LANES:]
        for t in range(nt):
            stack = jnp.concatenate([ab_ref[0, pl.ds(t, n1, stride=nt), :],
                                     ab_ref[1, pl.ds(t, n1, stride=nt), :]], axis=0).astype(bf16)
            z = jnp.dot(m_ref[...], stack, preferred_element_type=f32)
            zr, zn = z[:n1], z[n1:]
            ct, st = ct_ref[t], st_ref[t]
            z_ref[0, g, t] = _pack_pair(ct * zr - st * zn, ct * zn + st * zr)


def _fourier2_kernel(scale, z_ref, m_ref, o_ref, zbuf, ybuf):
    _, grp, n2, tk, w = z_ref.shape
    for g in range(grp):
        zbuf[...] = z_ref[0, g].reshape(n2 * tk, w)
        for j in range(tk):
            zp = zbuf[pl.ds(j, n2, stride=tk), :]
            zr = lax.bitcast_convert_type(zp << 16, f32).astype(bf16)
            zn = lax.bitcast_convert_type(zp & jnp.uint32(0xFFFF0000), f32).astype(bf16)
            y = jnp.dot(m_ref[...], jnp.concatenate([zr, zn], axis=0), preferred_element_type=f32)
            ybuf[pl.ds(j, n2, stride=tk), :] = y * scale
        o_ref[0, g] = ybuf[...].reshape(n2, tk, w)


def _fourier(x, mod, g, w_f, cs):
    b, l, d = x.shape
    n2 = DFT_INNER
    n1 = l // n2
    grp, w = FOURIER_GROUPS, FOURIER_GROUP_W
    t2 = SUBLANES
    k1 = jnp.arange(n1, dtype=i32)
    ang1 = ((k1[:, None] * k1[None, :]) % n1).astype(f32) * (2.0 * math.pi / n1)
    c1, s1 = jnp.cos(ang1), jnp.sin(ang1)
    m1 = jnp.concatenate([jnp.concatenate([c1, -s1], axis=1),
                          jnp.concatenate([s1, c1], axis=1)], axis=0).astype(bf16)
    l2 = jnp.arange(n2, dtype=i32)
    angt = ((l2[:, None] * k1[None, :]) % l).astype(f32) * (2.0 * math.pi / l)
    ct = jnp.broadcast_to(jnp.cos(angt)[:, :, None], (n2, n1, w))
    st = jnp.broadcast_to(jnp.sin(angt)[:, :, None], (n2, n1, w))
    ang2 = ((l2[:, None] * l2[None, :]) % n2).astype(f32) * (2.0 * math.pi / n2)
    m2 = jnp.concatenate([jnp.cos(ang2), -jnp.sin(ang2)], axis=1).astype(bf16)

    tspec = pl.BlockSpec((t2, n1, w), lambda t, i: (t, 0, 0))
    z = pl.pallas_call(
        _fourier1_kernel,
        grid=(n2 // t2, b),
        in_specs=[pl.BlockSpec((1, n1, t2, d), lambda t, i: (i, 0, t, 0)),
                  pl.BlockSpec((1, 6, d), lambda t, i: (i, 0, 0)),
                  pl.BlockSpec((1, d), lambda t, i: (0, 0)),
                  pl.BlockSpec(w_f.shape, lambda t, i: (0, 0)),
                  pl.BlockSpec(cs.shape, lambda t, i: (0, 0)),
                  pl.BlockSpec(m1.shape, lambda t, i: (0, 0)), tspec, tspec],
        out_specs=pl.BlockSpec((1, grp, t2, n1, w), lambda t, i: (i, 0, t, 0, 0)),
        out_shape=jax.ShapeDtypeStruct((b, grp, n2, n1, w), u32),
        scratch_shapes=[pltpu.VMEM((2, n1 * t2, w), f32)],
        compiler_params=_cparams(("parallel", "parallel")),
        name="fourier_outer",
    )(x.reshape(b, n1, n2, d), mod, g, w_f, cs, m1, ct, st)

    tk = SUBLANES
    y = pl.pallas_call(
        functools.partial(_fourier2_kernel, 1.0 / math.sqrt(l * w)),
        grid=(b, n1 // tk),
        in_specs=[pl.BlockSpec((1, grp, n2, tk, w), lambda i, t: (i, 0, 0, t, 0)),
                  pl.BlockSpec(m2.shape, lambda i, t: (0, 0))],
        out_specs=pl.BlockSpec((1, grp, n2, tk, w), lambda i, t: (i, 0, 0, t, 0)),
        out_shape=jax.ShapeDtypeStruct((b, grp, n2, n1, w), f32),
        scratch_shapes=[pltpu.VMEM((n2 * tk, w), u32), pltpu.VMEM((n2 * tk, w), f32)],
        compiler_params=_cparams(("parallel", "parallel")),
        name="fourier_inner",
    )(z, m2)
    return y.reshape(b, grp, l, w)


def _first_max4(a):
    m = jnp.maximum(jnp.maximum(a[0], a[1]), jnp.maximum(a[2], a[3]))
    idx = jnp.where(a[0] == m, 0, jnp.where(a[1] == m, 1, jnp.where(a[2] == m, 2, 3)))
    return m, idx


def _pick4(vals, idx):
    return jnp.where(idx == 0, vals[0], jnp.where(idx == 1, vals[1],
                                                   jnp.where(idx == 2, vals[2], vals[3])))


def _route(f, rw_ref, rb_ref, tri_ref, base_ref, first_step, ri_ref, wc_ref, cnt_ref, meta_ref):
    tm = f.shape[0]
    f_hi = f.astype(bf16)
    f_lo = (f - f_hi.astype(f32)).astype(bf16)
    rw2 = rw_ref[...]
    part = jnp.dot(f_hi, rw2, preferred_element_type=f32)
    logits = (part[:, :LANES] + part[:, LANES:]
              + jnp.dot(f_lo, rw2[:, :LANES], preferred_element_type=f32))
    sc = jax.nn.sigmoid(logits)
    st = sc.T
    bt = (sc + rb_ref[...]).T
    neg = jnp.full((1, tm), -jnp.inf, f32)
    gs = []
    for g in range(N_GROUPS):
        a = [bt[4 * g + i: 4 * g + i + 1] for i in range(4)]
        m1, i1 = _first_max4(a)
        rest = [jnp.where(i1 == i, neg, a[i]) for i in range(4)]
        m2, _ = _first_max4(rest)
        gs.append(m1 + m2)
    _, gsel = _first_max4(gs)
    a = [_pick4([bt[4 * g + i: 4 * g + i + 1] for g in range(N_GROUPS)], gsel) for i in range(4)]
    s = [_pick4([st[4 * g + i: 4 * g + i + 1] for g in range(N_GROUPS)], gsel) for i in range(4)]
    _, i1 = _first_max4(a)
    rest = [jnp.where(i1 == i, neg, a[i]) for i in range(4)]
    _, i2 = _first_max4(rest)
    w1 = _pick4(s, i1)
    w2 = _pick4(s, i2)
    tot = w1 + w2
    w1 = w1 / tot
    w2 = w2 / tot
    e0 = gsel * EXPERTS_PER_GROUP + i1
    e1 = gsel * EXPERTS_PER_GROUP + i2

    @pl.when(first_step)
    def _():
        base_ref[...] = jnp.zeros_like(base_ref)

    td = DISPATCH_TILE
    eid = lax.broadcasted_iota(i32, (N_EXPERTS, tm), 0)
    oh0 = (eid == e0).astype(f32)
    oh1 = (eid == e1).astype(f32)
    oh = oh0 + oh1
    before = jnp.dot(oh.astype(bf16), tri_ref[...], preferred_element_type=f32)
    lane_tile = lax.broadcasted_iota(i32, (N_EXPERTS, tm), 1) // td
    ei = lax.broadcasted_iota(i32, (N_EXPERTS, N_EXPERTS), 0)
    ej = lax.broadcasted_iota(i32, (N_EXPERTS, N_EXPERTS), 1)
    strict_lower = (ej < ei).astype(f32)
    run_start = jnp.zeros((N_EXPERTS, tm), f32)
    goff = base_ref[...]
    for s in range(tm // td):
        cnt_s = jnp.sum(oh[:, s * td:(s + 1) * td], axis=1, keepdims=True)
        pad_s = jnp.floor((cnt_s + 7.0) * 0.125) * 8.0
        pad_b = jnp.broadcast_to(pad_s, (N_EXPERTS, LANES))
        start_b = jnp.dot(strict_lower, pad_b, precision=HIGHEST, preferred_element_type=f32)
        run_start = jnp.where(lane_tile == s, start_b[:, 0:1], run_start)
        meta_ref[s, 0] = start_b.astype(i32)
        meta_ref[s, 1] = pad_b.astype(i32)
        meta_ref[s, 2] = goff.astype(i32)
        goff = goff + pad_b
    base_ref[...] = goff
    cnt_ref[...] = goff
    pos = before + run_start
    lp0 = jnp.sum(oh0 * pos, axis=0, keepdims=True)
    lp1 = jnp.sum(oh1 * pos, axis=0, keepdims=True)
    zi = jnp.zeros((1, tm), i32)
    ri_ref[...] = jnp.concatenate(
        [lp0.astype(i32), lp1.astype(i32), e0, e1,
         lax.bitcast_convert_type(w1, i32), lax.bitcast_convert_type(w2, i32), zi, zi], axis=0)
    zf = jnp.zeros((LANES - 4, tm), f32)
    wc_ref[...] = jnp.concatenate([w1, w2, lp0, lp1, zf], axis=0).T


def _outproj_kernel(yf_ref, o_ref, x_ref, mod_ref, w_ref, g_ref, rw_ref, rb_ref, tri_ref,
                    x1_ref, f_ref, ri_ref, wc_ref, cnt_ref, meta_ref, base_ref):
    mix = jnp.concatenate([yf_ref[0, g].astype(bf16) for g in range(FOURIER_GROUPS)] + [o_ref[0]],
                          axis=1)
    y = jnp.dot(mix, w_ref[...], preferred_element_type=f32)
    x1 = x_ref[0] + mod_ref[0, 2:3, :] * y
    x1_ref[0] = x1
    f = _norm_mod(x1, g_ref[...], mod_ref[0, 3:4, :], mod_ref[0, 4:5, :])
    f_ref[0] = f.astype(bf16)
    first = (pl.program_id(0) == 0) & (pl.program_id(1) == 0)
    _route(f, rw_ref, rb_ref, tri_ref, base_ref, first, ri_ref, wc_ref, cnt_ref, meta_ref)


def _before_in_tile(tm):
    tpos = jnp.arange(tm)
    return ((tpos[:, None] < tpos[None, :])
            & (tpos[:, None] // DISPATCH_TILE == tpos[None, :] // DISPATCH_TILE)).astype(bf16)


def _route_specs(b, l, tm):
    nl = l // tm
    rw = lambda d: pl.BlockSpec((d, 2 * LANES), lambda i, j: (0, 0))
    rb = pl.BlockSpec((1, LANES), lambda i, j: (0, 0))
    tri = pl.BlockSpec((tm, tm), lambda i, j: (0, 0))
    ns = tm // DISPATCH_TILE
    out_specs = [pl.BlockSpec((8, tm), lambda i, j: (0, i * nl + j)),
                 pl.BlockSpec((tm, LANES), lambda i, j: (i * nl + j, 0)),
                 pl.BlockSpec((N_EXPERTS, LANES), lambda i, j: (0, 0)),
                 pl.BlockSpec((ns, 3, N_EXPERTS, LANES), lambda i, j: (i * nl + j, 0, 0, 0))]
    out_shape = [jax.ShapeDtypeStruct((8, b * l), i32),
                 jax.ShapeDtypeStruct((b * l, LANES), f32),
                 jax.ShapeDtypeStruct((N_EXPERTS, LANES), f32),
                 jax.ShapeDtypeStruct((b * l // DISPATCH_TILE, 3, N_EXPERTS, LANES), i32)]
    return rw, rb, tri, out_specs, out_shape


def _outproj(yf, o, x, mod, w, g, rw, rb):
    b, l, d = x.shape
    tm = PROJ_TILE
    tri = _before_in_tile(tm)
    rws, rbs, tris, r_specs, r_shapes = _route_specs(b, l, tm)
    row = pl.BlockSpec((1, tm, d), lambda i, j: (i, j, 0))
    return pl.pallas_call(
        _outproj_kernel,
        grid=(b, l // tm),
        in_specs=[pl.BlockSpec((1, FOURIER_GROUPS, tm, LANES), lambda i, j: (i, 0, j, 0)),
                  pl.BlockSpec((1, tm, o.shape[2]), lambda i, j: (i, j, 0)),
                  row,
                  pl.BlockSpec((1, 6, d), lambda i, j: (i, 0, 0)),
                  pl.BlockSpec(w.shape, lambda i, j: (0, 0)),
                  pl.BlockSpec((1, d), lambda i, j: (0, 0)),
                  rws(d), rbs, tris],
        out_specs=[row, row] + r_specs,
        out_shape=[jax.ShapeDtypeStruct((b, l, d), f32),
                   jax.ShapeDtypeStruct((b, l, d), bf16)] + r_shapes,
        scratch_shapes=[pltpu.VMEM((N_EXPERTS, LANES), f32)],
        compiler_params=_cparams(("arbitrary", "arbitrary")),
        name="outproj_router",
    )(yf, o, x, mod, w, g, rw, rb, tri)


def _conv_kernel(seq_len, up_ref, um_ref, un_ref, x_ref, mod_ref, dw_ref, db_ref, lg_ref, lb_ref,
                 w_ref, pb_ref, g_ref, rw_ref, rb_ref, tri_ref,
                 x1_ref, f_ref, ri_ref, wc_ref, cnt_ref, meta_ref, base_ref, ext, conv_out):
    j = pl.program_id(1)
    tm = um_ref.shape[1]
    hl = CONV_HALO
    half = CONV_W // 2
    prev = jnp.where(j > 0, up_ref[0], jnp.zeros_like(up_ref[0]))
    nxt = jnp.where((j + 1) * tm < seq_len, un_ref[0], jnp.zeros_like(un_ref[0]))
    for c in range(ext.shape[0]):
        lanes_c = slice(c * LANES, (c + 1) * LANES)
        ext[c, 0:hl] = prev[:, lanes_c]
        ext[c, hl:hl + tm] = um_ref[0, :, lanes_c]
        ext[c, hl + tm:] = nxt[:, lanes_c]
    base = hl - half
    span = (CONV_W - 1) // SUBLANES * SUBLANES
    rows = CONV_ROWS

    def lane_chunk(c, carry):
        lanes = pl.ds(pl.multiple_of(c * LANES, LANES), LANES)
        for r in range(0, tm, rows):
            part = jnp.broadcast_to(db_ref[:, lanes], (rows, LANES))
            for phase in range(SUBLANES):
                win = ext[c, base + phase + r: base + phase + r + rows + span, :]
                same = None
                for t in range(phase, CONV_W, SUBLANES):
                    term = win[t - phase: t - phase + rows, :] * dw_ref[t:t + 1, lanes]
                    same = term if same is None else same + term
                part = part + same
            conv_out[r:r + rows, lanes] = part
        return carry

    lax.fori_loop(0, um_ref.shape[2] // LANES, lane_chunk, 0)
    acc = conv_out[...]
    mu = jnp.mean(acc, axis=-1, keepdims=True)
    cen = acc - mu
    var = jnp.mean(cen * cen, axis=-1, keepdims=True)
    ln = cen * lax.rsqrt(var + EPS) * lg_ref[...] + lb_ref[...]
    act = ln * jax.nn.sigmoid(ln)
    y = jnp.dot(act.astype(bf16), w_ref[...], preferred_element_type=f32) + pb_ref[...]
    x1 = x_ref[0] + mod_ref[0, 2:3, :] * y
    x1_ref[0] = x1
    f = _norm_mod(x1, g_ref[...], mod_ref[0, 3:4, :], mod_ref[0, 4:5, :])
    f_ref[0] = f.astype(bf16)
    first = (pl.program_id(0) == 0) & (j == 0)
    _route(f, rw_ref, rb_ref, tri_ref, base_ref, first, ri_ref, wc_ref, cnt_ref, meta_ref)


def _conv(u, x, mod, dw_w, dw_b, ln_g, ln_b, pw2_w, pw2_b, g, rw, rb):
    b, l, d = x.shape
    tm = TOKEN_TILE
    tri = _before_in_tile(tm)
    hl = CONV_HALO
    r = tm // hl
    nh = l // hl
    rws, rbs, tris, r_specs, r_shapes = _route_specs(b, l, tm)
    row = pl.BlockSpec((1, tm, d), lambda i, j: (i, j, 0))
    vec = pl.BlockSpec((1, d), lambda i, j: (0, 0))
    return pl.pallas_call(
        functools.partial(_conv_kernel, l),
        grid=(b, l // tm),
        in_specs=[pl.BlockSpec((1, hl, d), lambda i, j: (i, jnp.maximum(j * r - 1, 0), 0)),
                  row,
                  pl.BlockSpec((1, hl, d), lambda i, j: (i, jnp.minimum(j * r + r, nh - 1), 0)),
                  row,
                  pl.BlockSpec((1, 6, d), lambda i, j: (i, 0, 0)),
                  pl.BlockSpec(dw_w.shape, lambda i, j: (0, 0)),
                  vec, vec, vec,
                  pl.BlockSpec(pw2_w.shape, lambda i, j: (0, 0)),
                  vec, vec, rws(d), rbs, tris],
        out_specs=[row, row] + r_specs,
        out_shape=[jax.ShapeDtypeStruct((b, l, d), f32),
                   jax.ShapeDtypeStruct((b, l, d), bf16)] + r_shapes,
        scratch_shapes=[pltpu.VMEM((N_EXPERTS, LANES), f32),
                        pltpu.VMEM((d // LANES, tm + 2 * hl, LANES), f32),
                        pltpu.VMEM((tm, d), f32)],
        compiler_params=_cparams(("arbitrary", "arbitrary")),
        name="conv_router",
    )(u, u, u, x, mod, dw_w, dw_b, ln_g, ln_b, pw2_w, pw2_b, g, rw, rb, tri)


def _pack_bf16_pairs(x):
    h = x.shape[1] // 2
    lo = lax.bitcast_convert_type(x[:, :h], u32)
    hi = lax.bitcast_convert_type(x[:, h:], u32)
    return (lo >> 16) | (hi & jnp.uint32(0xFFFF0000))


def _unpack_bf16_pairs(u):
    lo = lax.bitcast_convert_type(u << 16, f32)
    hi = lax.bitcast_convert_type(u & jnp.uint32(0xFFFF0000), f32)
    return jnp.concatenate([lo, hi], axis=1).astype(bf16)


def _run_copies(meta, tile, local_ref, hbm_ref, sem, to_hbm):
    start_ref, size_ref, dst_ref = meta
    for e in range(N_EXPERTS):
        k = tile * N_EXPERTS + e
        size = pl.multiple_of(size_ref[k], RUN_ALIGN)

        @pl.when(size > 0)
        def _():
            loc = local_ref.at[pl.ds(pl.multiple_of(start_ref[k], RUN_ALIGN), size)]
            glob = hbm_ref.at[pl.ds(pl.multiple_of(dst_ref[k], RUN_ALIGN), size)]
            if to_hbm:
                pltpu.make_async_copy(loc, glob, sem).start()
            else:
                pltpu.make_async_copy(glob, loc, sem).start()


def _wait_rows(rows, local_ref, hbm_ref, sem):
    rows = pl.multiple_of(rows, RUN_ALIGN)

    @pl.when(rows > 0)
    def _():
        pltpu.make_async_copy(local_ref.at[pl.ds(0, rows)], hbm_ref.at[pl.ds(0, rows)], sem).wait()


def _dispatch_kernel(start_ref, size_ref, dst_ref, tot_ref, tail_start_ref, tail_size_ref, nv_ref,
                     f_ref, lp_ref, xs_ref, loc, zbuf, sem, zsem):
    i = pl.program_id(0)
    n = pl.num_programs(0)
    slot = i % 2
    subs = loc.shape[1]
    rows = loc.shape[2]
    td = f_ref.shape[0] // subs
    meta = (start_ref, size_ref, dst_ref)

    def drain(step, which):
        for s in range(subs):
            _wait_rows(tot_ref[step * subs + s], loc.at[which, s], xs_ref, sem.at[which])

    @pl.when(i >= 2)
    def _():
        drain(i - 2, slot)

    r = lax.broadcasted_iota(i32, (rows, td), 0)
    for s in range(subs):
        cols = slice(s * td, (s + 1) * td)
        pick0 = r == lp_ref[0:1, cols]
        pick1 = r == lp_ref[1:2, cols]
        onehot = (pick0 | pick1).astype(bf16)
        sorted_rows = jnp.dot(onehot, f_ref[cols, :], preferred_element_type=f32)
        half = sorted_rows.shape[1] // 2
        loc[slot, s, :, :half] = _pack_bf16_pairs(sorted_rows)
        w0 = lax.bitcast_convert_type(lp_ref[4:5, cols], f32)
        w1 = lax.bitcast_convert_type(lp_ref[5:6, cols], f32)
        row_w = jnp.sum(jnp.where(pick0, w0, 0.0) + jnp.where(pick1, w1, 0.0), axis=1, keepdims=True)
        loc[slot, s, :, half:] = jnp.broadcast_to(lax.bitcast_convert_type(row_w, u32), (rows, LANES))
        _run_copies(meta, i * subs + s, loc.at[slot, s], xs_ref, sem.at[slot], to_hbm=True)

    @pl.when(i == n - 1)
    def _():
        zbuf[...] = jnp.zeros_like(zbuf)
        total = 0
        for e in range(N_EXPERTS):
            size = pl.multiple_of(tail_size_ref[e], RUN_ALIGN)
            total = total + size

            @pl.when(size > 0)
            def _():
                pltpu.make_async_copy(
                    zbuf.at[pl.ds(0, size)],
                    xs_ref.at[pl.ds(pl.multiple_of(tail_start_ref[e], RUN_ALIGN), size)], zsem).start()

        _wait_rows(total, zbuf, xs_ref, zsem)

        def zero_block(k, c):
            pltpu.make_async_copy(zbuf, xs_ref.at[pl.ds(pl.multiple_of(k * zbuf.shape[0], RUN_ALIGN),
                                                        zbuf.shape[0])], zsem).start()
            return c

        def wait_block(k, c):
            pltpu.make_async_copy(zbuf, xs_ref.at[pl.ds(0, zbuf.shape[0])], zsem).wait()
            return c

        n_blocks = xs_ref.shape[0] // zbuf.shape[0]
        lax.fori_loop(nv_ref[0], n_blocks, zero_block, 0)
        lax.fori_loop(nv_ref[0], n_blocks, wait_block, 0)
        drain(i, slot)

        @pl.when(i >= 1)
        def _():
            drain(i - 1, 1 - slot)


def _dispatch(tables, f2, ri, n_slots):
    t, d = f2.shape
    subs = DISPATCH_SUBTILES
    tm = DISPATCH_TILE * subs
    return pl.pallas_call(
        _dispatch_kernel,
        grid_spec=pltpu.PrefetchScalarGridSpec(
            num_scalar_prefetch=7,
            grid=(t // tm,),
            in_specs=[pl.BlockSpec((tm, d), lambda i, *_: (i, 0)),
                      pl.BlockSpec((8, tm), lambda i, *_: (0, i))],
            out_specs=pl.BlockSpec(memory_space=pl.ANY),
            scratch_shapes=[pltpu.VMEM((2, subs, LOCAL_ROWS, d // 2 + LANES), u32),
                            pltpu.VMEM((EXPERT_ROWS, d // 2 + LANES), u32),
                            pltpu.SemaphoreType.DMA((2,)), pltpu.SemaphoreType.DMA(())]),
        out_shape=jax.ShapeDtypeStruct((n_slots, d // 2 + LANES), u32),
        compiler_params=_cparams(("arbitrary",)),
        name="moe_dispatch",
    )(*tables, f2, ri)


def _expert_kernel(be_ref, nv_ref, x_ref, wg_ref, wu_ref, wd_ref, y_ref, wgb, wub, wdb):
    i = pl.program_id(0)
    changed = jnp.logical_or(i == 0, be_ref[i] != be_ref[jnp.maximum(i - 1, 0)])

    @pl.when(changed)
    def _():
        wgb[...] = wg_ref[0, 0].astype(bf16)
        wub[...] = wu_ref[0, 0].astype(bf16)
        wdb[...] = wd_ref[0, 0].astype(bf16)

    @pl.when(i < nv_ref[0])
    def _():
        half = y_ref.shape[1]
        xb = _unpack_bf16_pairs(x_ref[:, :half])
        row_w = lax.bitcast_convert_type(x_ref[:, half:], f32)
        gate = jnp.dot(xb, wgb[...], preferred_element_type=f32)
        up = jnp.dot(xb, wub[...], preferred_element_type=f32)
        hid = (gate * jax.nn.sigmoid(gate) * up).astype(bf16)
        y = jnp.dot(hid, wdb[...], preferred_element_type=f32)
        y = jnp.concatenate([y[:, c:c + LANES] * row_w for c in range(0, y.shape[1], LANES)], axis=1)
        y_ref[...] = _pack_bf16_pairs(y.astype(bf16).astype(f32))

    @pl.when(i >= nv_ref[0])
    def _():
        y_ref[...] = jnp.zeros_like(y_ref)


def _experts(block_e, n_valid, xs, w_gate, w_up, w_down, layer):
    ns, xw = xs.shape
    tb = EXPERT_ROWS
    d, ff = w_gate.shape[2:]
    dh = d // 2
    xmap = lambda i, be, nv: (jnp.maximum(jnp.minimum(i, nv[0] - 1), 0), 0)
    wmap = lambda i, be, nv: (layer, be[i], 0, 0)
    return pl.pallas_call(
        _expert_kernel,
        grid_spec=pltpu.PrefetchScalarGridSpec(
            num_scalar_prefetch=2,
            grid=(ns // tb,),
            in_specs=[pl.BlockSpec((tb, xw), xmap),
                      pl.BlockSpec((1, 1, d, ff), wmap),
                      pl.BlockSpec((1, 1, d, ff), wmap),
                      pl.BlockSpec((1, 1, ff, d), wmap)],
            out_specs=pl.BlockSpec((tb, dh), lambda i, be, nv: (i, 0)),
            scratch_shapes=[pltpu.VMEM((d, ff), bf16), pltpu.VMEM((d, ff), bf16),
                            pltpu.VMEM((ff, d), bf16)]),
        out_shape=jax.ShapeDtypeStruct((ns, dh), u32),
        compiler_params=_cparams(("arbitrary",)),
        name="moe_experts",
    )(block_e, n_valid, xs, w_gate, w_up, w_down)


def _combine_kernel(final, start_ref, size_ref, dst_ref, tot_ref, ys_ref, wc_ref, x_ref, mod_ref,
                    g_ref, *rest):
    if final:
        o_ref, loc, sem = rest
    else:
        nmod_ref, w_ref, b_ref, o_ref, u_ref, loc, sem = rest
    i = pl.program_id(0)
    n = pl.num_programs(0)
    slot = i % 2
    subs = loc.shape[1]
    rows = loc.shape[2]
    td = x_ref.shape[0] // subs
    meta = (start_ref, size_ref, dst_ref)

    def fetch(step, which):
        for s in range(subs):
            _run_copies(meta, step * subs + s, loc.at[which, s], ys_ref, sem.at[which, s],
                        to_hbm=False)

    @pl.when(i == 0)
    def _():
        loc[...] = jnp.zeros_like(loc)
        fetch(i, slot)

    @pl.when(i + 1 < n)
    def _():
        fetch(i + 1, 1 - slot)

    c = lax.broadcasted_iota(i32, (td, rows), 1)
    parts = []
    for s in range(subs):
        _wait_rows(tot_ref[i * subs + s], loc.at[slot, s], ys_ref, sem.at[slot, s])
        wc = wc_ref[s * td:(s + 1) * td, :]
        sel = ((c == wc[:, 2:3].astype(i32)) | (c == wc[:, 3:4].astype(i32))).astype(bf16)
        parts.append(jnp.dot(sel, _unpack_bf16_pairs(loc[slot, s]), preferred_element_type=f32))
    xo = x_ref[...] + mod_ref[0, 5:6, :] * jnp.concatenate(parts, axis=0)
    if final:
        ms = jnp.mean(xo * xo, axis=-1, keepdims=True)
        o_ref[...] = xo * lax.rsqrt(ms + EPS) * g_ref[...]
    else:
        o_ref[...] = xo
        h = _norm_mod(xo, g_ref[...], nmod_ref[0, 0:1, :], nmod_ref[0, 1:2, :])
        p = jnp.dot(h.astype(bf16), w_ref[...], preferred_element_type=f32) + b_ref[...]
        ch = p.shape[1] // 2
        u_ref[...] = p[:, :ch] * jax.nn.sigmoid(p[:, ch:])


def _combine(tables, ys, wc, x, mod, g, glu=None):
    b, l, d = x.shape
    subs = DISPATCH_SUBTILES if glu is None else GLU_SUBTILES
    tm = DISPATCH_TILE * subs
    per_batch = l // tm
    const = lambda i, *_: (0, 0)
    tile = lambda cols: pl.BlockSpec((tm, cols), lambda i, *_: (i, 0))
    mod_spec = pl.BlockSpec((1, 6, d), lambda i, *_: (i // per_batch, 0, 0))
    in_specs = [pl.BlockSpec(memory_space=pl.ANY), tile(LANES), tile(d), mod_spec,
                pl.BlockSpec((1, d), const)]
    args = [ys, wc, x.reshape(b * l, d), mod, g]
    out_specs = [tile(d)]
    out_shape = [jax.ShapeDtypeStruct((b * l, d), f32)]
    if glu is not None:
        nmod, w, bias = glu
        in_specs += [mod_spec, pl.BlockSpec(w.shape, const), pl.BlockSpec(bias.shape, const)]
        args += [nmod, w, bias]
        out_specs.append(tile(w.shape[1] // 2))
        out_shape.append(jax.ShapeDtypeStruct((b * l, w.shape[1] // 2), f32))
    outs = pl.pallas_call(
        functools.partial(_combine_kernel, glu is None),
        grid_spec=pltpu.PrefetchScalarGridSpec(
            num_scalar_prefetch=4,
            grid=(b * per_batch,),
            in_specs=in_specs,
            out_specs=out_specs,
            scratch_shapes=[pltpu.VMEM((2, subs, LOCAL_ROWS, d // 2), u32),
                            pltpu.SemaphoreType.DMA((2, subs))]),
        out_shape=out_shape,
        compiler_params=_cparams(("arbitrary",)),
        name="moe_combine",
    )(*tables, *args)
    return [o.reshape(b, l, -1) for o in outs]


def _moe(f, routed, x, mod, g, w_gate, w_up, w_down, layer, glu=None):
    ri, wc, cnt, meta = routed
    b, l, d = x.shape
    t = b * l
    tb = EXPERT_ROWS
    n_tiles = t // DISPATCH_TILE
    used = cnt[:, 0].astype(i32)
    region = (used + tb - 1) // tb * tb
    gend = jnp.cumsum(region)
    gstart = gend - region
    max_rows = 2 * t + n_tiles * N_EXPERTS * (RUN_ALIGN - 1) + N_EXPERTS * (tb - 1)
    n_blocks = -(-max_rows // tb)
    m = meta[:, :, :, 0]
    run_start = m[:, 0].reshape(-1)
    run_size = m[:, 1].reshape(-1)
    run_dst = (m[:, 2] + gstart[None, :]).reshape(-1)
    tile_rows = jnp.sum(m[:, 1], axis=1)
    block_row = jnp.arange(n_blocks, dtype=i32) * tb
    block_e = jnp.minimum(jnp.sum((block_row[:, None] >= gend[None, :]).astype(i32), axis=1),
                          N_EXPERTS - 1)
    n_valid = (gend[-1] // tb).reshape(1)
    xs = _dispatch((run_start, run_size, run_dst, tile_rows, gstart + used, region - used, n_valid),
                   f.reshape(t, d), ri, n_blocks * tb)
    ys = _experts(block_e, n_valid, xs, w_gate, w_up, w_down, layer)
    return _combine((run_start, run_size, run_dst, tile_rows), ys, wc, x, mod, g, glu)


def _rope_tables(l):
    lane = jnp.arange(LANES)
    dh = lane % HEAD_DIM
    inv = ROPE_THETA ** (-(dh % 16).astype(f32) / 16.0)
    sign = jnp.where((dh % 32) < 16, -1.0, 1.0).astype(f32)
    by_row = (dh // 32)[None, None, :] == 0
    ang_r = jnp.arange(l // GRID_W, dtype=f32)[:, None] * inv[None, :]
    ang_c = jnp.arange(GRID_W, dtype=f32)[:, None] * inv[None, :]
    cos = jnp.where(by_row, jnp.cos(ang_r)[:, None, :], jnp.cos(ang_c)[None, :, :])
    sin = jnp.where(by_row, jnp.sin(ang_r)[:, None, :], jnp.sin(ang_c)[None, :, :])
    return cos.reshape(l, LANES), (sin * sign[None, None, :]).reshape(l, LANES)


def kernel(x, c, ctx, c_ctx, ada_w, ada_b, norm_mix_g, norm_ffn_g, even_w_in, even_w_out, even_sink, conv_pw1_w, conv_pw1_b, conv_dw_w, conv_dw_b, conv_ln_g, conv_ln_b, conv_pw2_w, conv_pw2_b, router_w, router_b, moe_w_gate, moe_w_up, moe_w_down, final_norm_g):
    b, l, d = x.shape
    depth = ada_w.shape[0]
    assert depth == 2 and b < COND_ROWS
    ctx_row = b
    cond = jnp.zeros((COND_ROWS, d), f32).at[:b].set(c).at[ctx_row].set(c_ctx)
    mods = _adaln(cond, ada_w, ada_b).reshape(depth, COND_ROWS, 6, d)

    heads = jnp.arange(N_HEADS).reshape(N_KV_HEADS, N_HEADS // N_KV_HEADS).T.reshape(-1)
    qperm = (heads[:, None] * HEAD_DIM + jnp.arange(HEAD_DIM)[None, :]).reshape(-1)
    fw = FOURIER_GROUPS * FOURIER_GROUP_W
    qw = N_HEADS * HEAD_DIM
    w_in = even_w_in[0]
    w_in_p = jnp.concatenate([w_in[:, :fw], w_in[:, fw:fw + qw][:, qperm], w_in[:, fw + qw:]],
                             axis=1).astype(bf16)
    w_out = even_w_out[0]
    w_out_p = jnp.concatenate([w_out[:fw], w_out[fw:][qperm]], axis=0).astype(bf16)
    sink_pairs = (even_sink[0].astype(f32) * LOG2E).reshape(N_KV_HEADS, N_HEADS // N_KV_HEADS).T
    sinkcol = jnp.repeat(jnp.repeat(sink_pairs, HEAD_DIM, axis=1), ATT_BLOCK, axis=0)

    cidx = jnp.arange(FOURIER_GROUP_W, dtype=i32)
    angc = ((cidx[:, None] * cidx[None, :]) % FOURIER_GROUP_W).astype(f32) * (2.0 * math.pi / FOURIER_GROUP_W)
    cs = jnp.concatenate([jnp.cos(angc), jnp.sin(angc)], axis=1).astype(bf16)
    cos_t, sin_t = _rope_tables(l)

    rw32 = jnp.zeros((d, LANES), f32).at[:, :N_EXPERTS].set(router_w.astype(f32))
    rw_hi = rw32.astype(bf16)
    rw = jnp.concatenate([rw_hi, (rw32 - rw_hi.astype(f32)).astype(bf16)], axis=1)
    rb = jnp.zeros((1, LANES), f32).at[0, :N_EXPERTS].set(router_b)
    row = lambda v: v.reshape(1, -1)

    q, k, v = _inproj(x, mods[0], row(norm_mix_g[0]), w_in_p[:, fw:], cos_t, sin_t)
    ck, cv = _ctxkv(ctx, mods[0], row(norm_mix_g[0]), w_in_p[:, fw + qw:], ctx_row)
    yf = _fourier(x, mods[0], row(norm_mix_g[0]), w_in_p[:, :fw], cs)
    att = _attention(q, k, v, ck, cv, sinkcol)
    x1, f, *routed = _outproj(yf, att, x, mods[0], w_out_p, row(norm_ffn_g[0]), rw, rb)
    x2, u = _moe(f, routed, x1, mods[0], row(norm_mix_g[1]), moe_w_gate, moe_w_up, moe_w_down,
                 layer=0, glu=(mods[1], conv_pw1_w[0].astype(bf16), row(conv_pw1_b[0])))

    x3, f, *routed = _conv(u, x2, mods[1], conv_dw_w[0], row(conv_dw_b[0]), row(conv_ln_g[0]),
                           row(conv_ln_b[0]), conv_pw2_w[0].astype(bf16), row(conv_pw2_b[0]),
                           row(norm_ffn_g[1]), rw, rb)
    (out,) = _moe(f, routed, x3, mods[1], row(final_norm_g), moe_w_gate, moe_w_up, moe_w_down,
                  layer=1)
    return out
```

```python
import functools
import math

import jax
import jax.numpy as jnp
from jax import lax
from jax.experimental import pallas as pl
from jax.experimental.pallas import tpu as pltpu

f32 = jnp.float32
bf16 = jnp.bfloat16
i32 = jnp.int32
u32 = jnp.uint32
HIGHEST = lax.Precision.HIGHEST

GRID_W = 64
HEAD_DIM = 64
N_HEADS = 8
N_KV_HEADS = 2
WINDOW = 128
ATT_BLOCK = 128
ROPE_THETA = 10000.0
FOURIER_GROUPS = 4
FOURIER_GROUP_W = 128
CONV_W = 31
N_EXPERTS = 16
N_GROUPS = 4
EXPERTS_PER_GROUP = 4
EXPERT_FF = 512
EPS = 1e-6
NEG_INF = -1e30
LOG2E = math.log2(math.e)

LANES = 128
SUBLANES = 8
COND_ROWS = 8
DFT_INNER = 64
TOKEN_TILE = 512
PROJ_TILE = 1024
ATT_TILE = 1024
EXPERT_ROWS = 512
DISPATCH_TILE = 256
DISPATCH_SUBTILES = 4
GLU_SUBTILES = 2
RUN_ALIGN = 8
LOCAL_ROWS = -(-(2 * DISPATCH_TILE + N_EXPERTS * (RUN_ALIGN - 1)) // LANES) * LANES
CONV_HALO = 16
CONV_ROWS = 128
VMEM_LIMIT = 56 * 1024 * 1024


def _cparams(sem, vmem=VMEM_LIMIT):
    return pltpu.CompilerParams(dimension_semantics=sem, vmem_limit_bytes=vmem)


def _adaln_kernel(cond_ref, w_ref, b_ref, o_ref):
    s = cond_ref[...]
    s = s * jax.nn.sigmoid(s)
    w = w_ref[0]
    s_hi, w_hi = s.astype(bf16), w.astype(bf16)
    s_lo = (s - s_hi.astype(f32)).astype(bf16)
    w_lo = (w - w_hi.astype(f32)).astype(bf16)
    dot = functools.partial(jnp.dot, preferred_element_type=f32)
    o_ref[0] = dot(s_hi, w_hi) + dot(s_hi, w_lo) + dot(s_lo, w_hi) + b_ref[0]


def _adaln(cond, ada_w, ada_b):
    depth, d, n = ada_w.shape
    tn = 3072
    return pl.pallas_call(
        _adaln_kernel,
        grid=(depth, n // tn),
        in_specs=[pl.BlockSpec((COND_ROWS, d), lambda i, j: (0, 0)),
                  pl.BlockSpec((1, d, tn), lambda i, j: (i, 0, j)),
                  pl.BlockSpec((1, 1, tn), lambda i, j: (i, 0, j))],
        out_specs=pl.BlockSpec((1, COND_ROWS, tn), lambda i, j: (i, 0, j)),
        out_shape=jax.ShapeDtypeStruct((depth, COND_ROWS, n), f32),
        compiler_params=_cparams(("arbitrary", "arbitrary")),
        name="adaln",
    )(cond, ada_w, ada_b.reshape(depth, 1, n))


def _norm_mod(x, g, shift, scale):
    ms = jnp.mean(x * x, axis=-1, keepdims=True)
    return (x * lax.rsqrt(ms + EPS)) * (g * (1.0 + scale)) + shift


def _rope(p, cos, sin_signed, first_half):
    rot = jnp.where(first_half, pltpu.roll(p, LANES - 16, axis=1), pltpu.roll(p, 16, axis=1))
    return p * cos + rot * sin_signed


def _inproj_kernel(x_ref, mod_ref, g_ref, w_ref, cos_ref, sin_ref, q_ref, k_ref, v_ref):
    h = _norm_mod(x_ref[0], g_ref[...], mod_ref[0, 0:1, :], mod_ref[0, 1:2, :])
    p = jnp.dot(h.astype(bf16), w_ref[...], preferred_element_type=f32)
    cos = cos_ref[...]
    sin = sin_ref[...]
    lane = lax.broadcasted_iota(i32, cos.shape, 1)
    first_half = (lane % 32) < 16
    qw = N_HEADS * HEAD_DIM
    for c in range(qw // LANES):
        qc = p[:, c * LANES:(c + 1) * LANES]
        q_ref[0, :, c * LANES:(c + 1) * LANES] = (
            _rope(qc, cos, sin, first_half) * (LOG2E * HEAD_DIM ** -0.5)).astype(bf16)
    k_ref[0] = _rope(p[:, qw:qw + LANES], cos, sin, first_half).astype(bf16)
    v_ref[0] = p[:, qw + LANES:].astype(bf16)


def _inproj(x, mod, g, w, cos_t, sin_t):
    b, l, d = x.shape
    tm = PROJ_TILE
    n = w.shape[1]
    return pl.pallas_call(
        _inproj_kernel,
        grid=(b, l // tm),
        in_specs=[pl.BlockSpec((1, tm, d), lambda i, j: (i, j, 0)),
                  pl.BlockSpec((1, 6, d), lambda i, j: (i, 0, 0)),
                  pl.BlockSpec((1, d), lambda i, j: (0, 0)),
                  pl.BlockSpec((d, n), lambda i, j: (0, 0)),
                  pl.BlockSpec((tm, LANES), lambda i, j: (j, 0)),
                  pl.BlockSpec((tm, LANES), lambda i, j: (j, 0))],
        out_specs=[pl.BlockSpec((1, tm, N_HEADS * HEAD_DIM), lambda i, j: (i, j, 0)),
                   pl.BlockSpec((1, tm, LANES), lambda i, j: (i, j, 0)),
                   pl.BlockSpec((1, tm, LANES), lambda i, j: (i, j, 0))],
        out_shape=[jax.ShapeDtypeStruct((b, l, N_HEADS * HEAD_DIM), bf16),
                   jax.ShapeDtypeStruct((b, l, LANES), bf16),
                   jax.ShapeDtypeStruct((b, l, LANES), bf16)],
        compiler_params=_cparams(("parallel", "parallel")),
        name="inproj",
    )(x, mod, g, w, cos_t, sin_t)


def _ctxkv_kernel(x_ref, mod_ref, g_ref, w_ref, k_ref, v_ref):
    h = _norm_mod(x_ref[0], g_ref[...], mod_ref[0, 0:1, :], mod_ref[0, 1:2, :])
    p = jnp.dot(h.astype(bf16), w_ref[...], preferred_element_type=f32)
    k_ref[0] = p[:, :LANES].astype(bf16)
    v_ref[0] = p[:, LANES:].astype(bf16)


def _ctxkv(ctx, mod, g, w_kv, ctx_row):
    b, c, d = ctx.shape
    return pl.pallas_call(
        _ctxkv_kernel,
        grid=(b,),
        in_specs=[pl.BlockSpec((1, c, d), lambda i: (i, 0, 0)),
                  pl.BlockSpec((1, 6, d), lambda i: (ctx_row, 0, 0)),
                  pl.BlockSpec((1, d), lambda i: (0, 0)),
                  pl.BlockSpec((d, 2 * LANES), lambda i: (0, 0))],
        out_specs=[pl.BlockSpec((1, c, LANES), lambda i: (i, 0, 0)),
                   pl.BlockSpec((1, c, LANES), lambda i: (i, 0, 0))],
        out_shape=[jax.ShapeDtypeStruct((b, c, LANES), bf16),
                   jax.ShapeDtypeStruct((b, c, LANES), bf16)],
        compiler_params=_cparams(("parallel",)),
        name="ctxkv",
    )(ctx, mod, g, w_kv)


def _attn_kernel(seq_len, q_ref, kp_ref, km_ref, kn_ref, vp_ref, vm_ref, vn_ref,
                 ck_ref, cv_ref, sink_ref, o_ref, kext, vext):
    j = pl.program_id(1)
    tq = ATT_TILE
    blk = ATT_BLOCK
    kext[0:blk] = kp_ref[0]
    kext[blk:blk + tq] = km_ref[0]
    kext[blk + tq:] = kn_ref[0]
    vext[:, LANES:] = jnp.ones((tq + 2 * blk, LANES), bf16)
    vext[0:blk, :LANES] = vp_ref[0]
    vext[blk:blk + tq, :LANES] = vm_ref[0]
    vext[blk + tq:, :LANES] = vn_ref[0]
    n_ctx = ck_ref.shape[1]
    nk = n_ctx + 3 * blk
    n_chunks = (N_HEADS * HEAD_DIM) // LANES
    rows = n_chunks * blk
    half = HEAD_DIM
    klow = lax.broadcasted_iota(i32, (1, LANES), 1) < half
    vlane = lax.broadcasted_iota(i32, (1, 2 * LANES), 1)
    vlow = (vlane < half) | ((vlane >= LANES) & (vlane < LANES + half))
    zero = jnp.zeros((), bf16)
    ck = ck_ref[0]
    cvx = jnp.concatenate([cv_ref[0], jnp.ones((n_ctx, LANES), bf16)], axis=1)
    ck_lo, ck_hi = jnp.where(klow, ck, zero), jnp.where(klow, zero, ck)
    cv_lo, cv_hi = jnp.where(vlow, cvx, zero), jnp.where(vlow, zero, cvx)
    sink2 = sink_ref[...]
    sink_lo, sink_hi = sink2[:, 0:1], sink2[:, half:half + 1]
    low = lax.broadcasted_iota(i32, (rows, LANES), 1) < half
    qi = lax.broadcasted_iota(i32, (rows, 3 * blk), 0) % blk
    pk = lax.broadcasted_iota(i32, (rows, 3 * blk), 1)
    band_bias = jnp.where(jnp.abs(pk - blk - qi) <= WINDOW, 0.0, NEG_INF).astype(f32)
    pcol = lax.broadcasted_iota(i32, (1, 3 * blk), 1)
    nt = (((1,), (1,)), ((), ()))

    def row_max(s):
        blocks = [s[:, i:i + LANES] for i in range(0, s.shape[1], LANES)]
        return jnp.max(functools.reduce(jnp.maximum, blocks), axis=1, keepdims=True)

    def sub(s, carry):
        r0 = pl.multiple_of(s * blk, blk)
        qs = q_ref[0, pl.ds(r0, blk), :]
        lhs = jnp.concatenate([qs[:, c * LANES:(c + 1) * LANES] for c in range(n_chunks)], axis=0)
        kl = kext[pl.ds(r0, 3 * blk), :]
        vl = vext[pl.ds(r0, 3 * blk), :]
        kbd = jnp.concatenate([ck_lo, jnp.where(klow, kl, zero),
                               ck_hi, jnp.where(klow, zero, kl)], axis=0)
        vbd = jnp.concatenate([cv_lo, jnp.where(vlow, vl, zero),
                               cv_hi, jnp.where(vlow, zero, vl)], axis=0)
        kpos = j * tq + r0 - blk + pcol
        bias = band_bias + jnp.where((kpos >= 0) & (kpos < seq_len), 0.0, NEG_INF).astype(f32)
        sc = lax.dot_general(lhs, kbd, nt, preferred_element_type=f32)
        s_lo = jnp.concatenate([sc[:, :n_ctx], sc[:, n_ctx:nk] + bias], axis=1)
        s_hi = jnp.concatenate([sc[:, nk:nk + n_ctx], sc[:, nk + n_ctx:] + bias], axis=1)
        m_lo = jnp.maximum(row_max(s_lo), sink_lo)
        m_hi = jnp.maximum(row_max(s_hi), sink_hi)
        e = jnp.concatenate([jnp.exp2(s_lo - m_lo), jnp.exp2(s_hi - m_hi)], axis=1).astype(bf16)
        ov = jnp.dot(e, vbd, preferred_element_type=f32)
        den = ov[:, LANES:] + jnp.exp2(sink2 - jnp.where(low, m_lo, m_hi))
        o = (ov[:, :LANES] / den).astype(bf16)
        for c in range(n_chunks):
            o_ref[0, pl.ds(r0, blk), c * LANES:(c + 1) * LANES] = o[c * blk:(c + 1) * blk]
        return carry

    lax.fori_loop(0, tq // blk, sub, 0, unroll=8)


def _attention(q, k, v, ck, cv, sinkcol):
    b, l, qw = q.shape
    c = ck.shape[1]
    tq = ATT_TILE
    r = tq // ATT_BLOCK
    nb = l // ATT_BLOCK
    prev = pl.BlockSpec((1, ATT_BLOCK, LANES), lambda i, j: (i, jnp.maximum(j * r - 1, 0), 0))
    main = pl.BlockSpec((1, tq, LANES), lambda i, j: (i, j, 0))
    nxt = pl.BlockSpec((1, ATT_BLOCK, LANES), lambda i, j: (i, jnp.minimum(j * r + r, nb - 1), 0))
    cspec = pl.BlockSpec((1, c, LANES), lambda i, j: (i, 0, 0))
    return pl.pallas_call(
        functools.partial(_attn_kernel, l),
        grid=(b, l // tq),
        in_specs=[pl.BlockSpec((1, tq, qw), lambda i, j: (i, j, 0)),
                  prev, main, nxt, prev, main, nxt, cspec, cspec,
                  pl.BlockSpec(sinkcol.shape, lambda i, j: (0, 0))],
        out_specs=pl.BlockSpec((1, tq, qw), lambda i, j: (i, j, 0)),
        out_shape=jax.ShapeDtypeStruct((b, l, qw), bf16),
        scratch_shapes=[pltpu.VMEM((tq + 2 * ATT_BLOCK, LANES), bf16),
                        pltpu.VMEM((tq + 2 * ATT_BLOCK, 2 * LANES), bf16)],
        compiler_params=_cparams(("parallel", "parallel")),
        name="attention",
    )(q, k, k, k, v, v, v, ck, cv, sinkcol)


def _pack_pair(lo, hi):
    lo = lax.bitcast_convert_type(lo.astype(bf16).astype(f32), u32)
    hi = lax.bitcast_convert_type(hi.astype(bf16).astype(f32), u32)
    return (lo >> 16) | (hi & jnp.uint32(0xFFFF0000))


def _fourier1_kernel(x_ref, mod_ref, g_ref, w_ref, cs_ref, m_ref, ct_ref, st_ref, z_ref, ab_ref):
    n1 = x_ref.shape[1]
    nt = x_ref.shape[2]
    x = x_ref[0].reshape(n1 * nt, x_ref.shape[3])
    h = _norm_mod(x, g_ref[...], mod_ref[0, 0:1, :], mod_ref[0, 1:2, :]).astype(bf16)
    p = jnp.dot(h, w_ref[...], preferred_element_type=f32)
    for g in range(FOURIER_GROUPS):
        ug = p[:, g * LANES:(g + 1) * LANES].astype(bf16)
        ab = jnp.dot(ug, cs_ref[...], preferred_element_type=f32)
        ab_ref[0] = ab[:, :LANES]
        ab_ref[1] = ab[:, LANES:]
        for t in range(nt):
            stack = jnp.concatenate([ab_ref[0, pl.ds(t, n1, stride=nt), :],
                                     ab_ref[1, pl.ds(t, n1, stride=nt), :]], axis=0).astype(bf16)
            z = jnp.dot(m_ref[...], stack, preferred_element_type=f32)
            zr, zn = z[:n1], z[n1:]
            ct, st = ct_ref[t], st_ref[t]
            z_ref[0, g, t] = _pack_pair(ct * zr - st * zn, ct * zn + st * zr)


def _fourier2_kernel(scale, z_ref, m_ref, o_ref, zbuf, ybuf):
    _, grp, n2, tk, w = z_ref.shape
    for g in range(grp):
        zbuf[...] = z_ref[0, g].reshape(n2 * tk, w)
        for j in range(tk):
            zp = zbuf[pl.ds(j, n2, stride=tk), :]
            zr = lax.bitcast_convert_type(zp << 16, f32).astype(bf16)
            zn = lax.bitcast_convert_type(zp & jnp.uint32(0xFFFF0000), f32).astype(bf16)
            y = jnp.dot(m_ref[...], jnp.concatenate([zr, zn], axis=0), preferred_element_type=f32)
            ybuf[pl.ds(j, n2, stride=tk), :] = y * scale
        o_ref[0, g] = ybuf[...].reshape(n2, tk, w)


def _fourier(x, mod, g, w_f, cs):
    b, l, d = x.shape
    n2 = DFT_INNER
    n1 = l // n2
    grp, w = FOURIER_GROUPS, FOURIER_GROUP_W
    t2 = SUBLANES
    k1 = jnp.arange(n1, dtype=i32)
    ang1 = ((k1[:, None] * k1[None, :]) % n1).astype(f32) * (2.0 * math.pi / n1)
    c1, s1 = jnp.cos(ang1), jnp.sin(ang1)
    m1 = jnp.concatenate([jnp.concatenate([c1, -s1], axis=1),
                          jnp.concatenate([s1, c1], axis=1)], axis=0).astype(bf16)
    l2 = jnp.arange(n2, dtype=i32)
    angt = ((l2[:, None] * k1[None, :]) % l).astype(f32) * (2.0 * math.pi / l)
    ct = jnp.broadcast_to(jnp.cos(angt)[:, :, None], (n2, n1, w))
    st = jnp.broadcast_to(jnp.sin(angt)[:, :, None], (n2, n1, w))
    ang2 = ((l2[:, None] * l2[None, :]) % n2).astype(f32) * (2.0 * math.pi / n2)
    m2 = jnp.concatenate([jnp.cos(ang2), -jnp.sin(ang2)], axis=1).astype(bf16)

    tspec = pl.BlockSpec((t2, n1, w), lambda t, i: (t, 0, 0))
    z = pl.pallas_call(
        _fourier1_kernel,
        grid=(n2 // t2, b),
        in_specs=[pl.BlockSpec((1, n1, t2, d), lambda t, i: (i, 0, t, 0)),
                  pl.BlockSpec((1, 6, d), lambda t, i: (i, 0, 0)),
                  pl.BlockSpec((1, d), lambda t, i: (0, 0)),
                  pl.BlockSpec(w_f.shape, lambda t, i: (0, 0)),
                  pl.BlockSpec(cs.shape, lambda t, i: (0, 0)),
                  pl.BlockSpec(m1.shape, lambda t, i: (0, 0)), tspec, tspec],
        out_specs=pl.BlockSpec((1, grp, t2, n1, w), lambda t, i: (i, 0, t, 0, 0)),
        out_shape=jax.ShapeDtypeStruct((b, grp, n2, n1, w), u32),
        scratch_shapes=[pltpu.VMEM((2, n1 * t2, w), f32)],
        compiler_params=_cparams(("parallel", "parallel")),
        name="fourier_outer",
    )(x.reshape(b, n1, n2, d), mod, g, w_f, cs, m1, ct, st)

    tk = 2 * SUBLANES
    y = pl.pallas_call(
        functools.partial(_fourier2_kernel, 1.0 / math.sqrt(l * w)),
        grid=(b, n1 // tk),
        in_specs=[pl.BlockSpec((1, grp, n2, tk, w), lambda i, t: (i, 0, 0, t, 0)),
                  pl.BlockSpec(m2.shape, lambda i, t: (0, 0))],
        out_specs=pl.BlockSpec((1, grp, n2, tk, w), lambda i, t: (i, 0, 0, t, 0)),
        out_shape=jax.ShapeDtypeStruct((b, grp, n2, n1, w), f32),
        scratch_shapes=[pltpu.VMEM((n2 * tk, w), u32), pltpu.VMEM((n2 * tk, w), f32)],
        compiler_params=_cparams(("parallel", "parallel")),
        name="fourier_inner",
    )(z, m2)
    return y.reshape(b, grp, l, w)


def _first_max4(a):
    m = jnp.maximum(jnp.maximum(a[0], a[1]), jnp.maximum(a[2], a[3]))
    idx = jnp.where(a[0] == m, 0, jnp.where(a[1] == m, 1, jnp.where(a[2] == m, 2, 3)))
    return m, idx


def _pick4(vals, idx):
    return jnp.where(idx == 0, vals[0], jnp.where(idx == 1, vals[1],
                                                   jnp.where(idx == 2, vals[2], vals[3])))


def _route(f, rw_ref, rb_ref, tri_ref, base_ref, first_step, ri_ref, wc_ref, cnt_ref, meta_ref):
    rw2 = rw_ref[...]
    pieces = []
    for fp in (f if isinstance(f, (list, tuple)) else [f]):
        f_hi = fp.astype(bf16)
        f_lo = (fp - f_hi.astype(f32)).astype(bf16)
        part = jnp.dot(f_hi, rw2, preferred_element_type=f32)
        pieces.append(part[:, :LANES] + part[:, LANES:]
                      + jnp.dot(f_lo, rw2[:, :LANES], preferred_element_type=f32))
    logits = jnp.concatenate(pieces, axis=0)
    tm = logits.shape[0]
    sc = jax.nn.sigmoid(logits)
    st = sc.T
    bt = (sc + rb_ref[...]).T
    neg = jnp.full((1, tm), -jnp.inf, f32)
    gs = []
    for g in range(N_GROUPS):
        a = [bt[4 * g + i: 4 * g + i + 1] for i in range(4)]
        m1, i1 = _first_max4(a)
        rest = [jnp.where(i1 == i, neg, a[i]) for i in range(4)]
        m2, _ = _first_max4(rest)
        gs.append(m1 + m2)
    _, gsel = _first_max4(gs)
    a = [_pick4([bt[4 * g + i: 4 * g + i + 1] for g in range(N_GROUPS)], gsel) for i in range(4)]
    s = [_pick4([st[4 * g + i: 4 * g + i + 1] for g in range(N_GROUPS)], gsel) for i in range(4)]
    _, i1 = _first_max4(a)
    rest = [jnp.where(i1 == i, neg, a[i]) for i in range(4)]
    _, i2 = _first_max4(rest)
    w1 = _pick4(s, i1)
    w2 = _pick4(s, i2)
    tot = w1 + w2
    w1 = w1 / tot
    w2 = w2 / tot
    e0 = gsel * EXPERTS_PER_GROUP + i1
    e1 = gsel * EXPERTS_PER_GROUP + i2

    @pl.when(first_step)
    def _():
        base_ref[...] = jnp.zeros_like(base_ref)

    td = DISPATCH_TILE
    eid = lax.broadcasted_iota(i32, (N_EXPERTS, tm), 0)
    oh0 = (eid == e0).astype(f32)
    oh1 = (eid == e1).astype(f32)
    oh = oh0 + oh1
    before = jnp.dot(oh.astype(bf16), tri_ref[...], preferred_element_type=f32)
    lane_tile = lax.broadcasted_iota(i32, (N_EXPERTS, tm), 1) // td
    ei = lax.broadcasted_iota(i32, (N_EXPERTS, N_EXPERTS), 0)
    ej = lax.broadcasted_iota(i32, (N_EXPERTS, N_EXPERTS), 1)
    strict_lower = (ej < ei).astype(f32)
    run_start = jnp.zeros((N_EXPERTS, tm), f32)
    goff = base_ref[...]
    for s in range(tm // td):
        cnt_s = jnp.sum(oh[:, s * td:(s + 1) * td], axis=1, keepdims=True)
        pad_s = jnp.floor((cnt_s + 7.0) * 0.125) * 8.0
        pad_b = jnp.broadcast_to(pad_s, (N_EXPERTS, LANES))
        start_b = jnp.dot(strict_lower, pad_b, precision=HIGHEST, preferred_element_type=f32)
        run_start = jnp.where(lane_tile == s, start_b[:, 0:1], run_start)
        meta_ref[s, 0] = start_b.astype(i32)
        meta_ref[s, 1] = pad_b.astype(i32)
        meta_ref[s, 2] = goff.astype(i32)
        goff = goff + pad_b
    base_ref[...] = goff
    cnt_ref[...] = goff
    pos = before + run_start
    lp0 = jnp.sum(oh0 * pos, axis=0, keepdims=True)
    lp1 = jnp.sum(oh1 * pos, axis=0, keepdims=True)
    zi = jnp.zeros((1, tm), i32)
    ri_ref[...] = jnp.concatenate(
        [lp0.astype(i32), lp1.astype(i32), e0, e1,
         lax.bitcast_convert_type(w1, i32), lax.bitcast_convert_type(w2, i32), zi, zi], axis=0)
    zf = jnp.zeros((LANES - 4, tm), f32)
    wc_ref[...] = jnp.concatenate([w1, w2, lp0, lp1, zf], axis=0).T


def _outproj_kernel(yf_ref, o_ref, x_ref, mod_ref, w_ref, g_ref, rw_ref, rb_ref, tri_ref,
                    x1_ref, f_ref, ri_ref, wc_ref, cnt_ref, meta_ref, base_ref):
    mix = jnp.concatenate([yf_ref[0, g].astype(bf16) for g in range(FOURIER_GROUPS)] + [o_ref[0]],
                          axis=1)
    y = jnp.dot(mix, w_ref[...], preferred_element_type=f32)
    x1 = x_ref[0] + mod_ref[0, 2:3, :] * y
    x1_ref[0] = x1
    f = _norm_mod(x1, g_ref[...], mod_ref[0, 3:4, :], mod_ref[0, 4:5, :])
    f_ref[0] = f.astype(bf16)
    first = (pl.program_id(0) == 0) & (pl.program_id(1) == 0)
    _route(f, rw_ref, rb_ref, tri_ref, base_ref, first, ri_ref, wc_ref, cnt_ref, meta_ref)


def _before_in_tile(tm):
    tpos = jnp.arange(tm)
    return ((tpos[:, None] < tpos[None, :])
            & (tpos[:, None] // DISPATCH_TILE == tpos[None, :] // DISPATCH_TILE)).astype(bf16)


def _route_specs(b, l, tm):
    nl = l // tm
    rw = lambda d: pl.BlockSpec((d, 2 * LANES), lambda i, j: (0, 0))
    rb = pl.BlockSpec((1, LANES), lambda i, j: (0, 0))
    tri = pl.BlockSpec((tm, tm), lambda i, j: (0, 0))
    ns = tm // DISPATCH_TILE
    out_specs = [pl.BlockSpec((8, tm), lambda i, j: (0, i * nl + j)),
                 pl.BlockSpec((tm, LANES), lambda i, j: (i * nl + j, 0)),
                 pl.BlockSpec((N_EXPERTS, LANES), lambda i, j: (0, 0)),
                 pl.BlockSpec((ns, 3, N_EXPERTS, LANES), lambda i, j: (i * nl + j, 0, 0, 0))]
    out_shape = [jax.ShapeDtypeStruct((8, b * l), i32),
                 jax.ShapeDtypeStruct((b * l, LANES), f32),
                 jax.ShapeDtypeStruct((N_EXPERTS, LANES), f32),
                 jax.ShapeDtypeStruct((b * l // DISPATCH_TILE, 3, N_EXPERTS, LANES), i32)]
    return rw, rb, tri, out_specs, out_shape


def _outproj(yf, o, x, mod, w, g, rw, rb):
    b, l, d = x.shape
    tm = PROJ_TILE
    tri = _before_in_tile(tm)
    rws, rbs, tris, r_specs, r_shapes = _route_specs(b, l, tm)
    row = pl.BlockSpec((1, tm, d), lambda i, j: (i, j, 0))
    return pl.pallas_call(
        _outproj_kernel,
        grid=(b, l // tm),
        in_specs=[pl.BlockSpec((1, FOURIER_GROUPS, tm, LANES), lambda i, j: (i, 0, j, 0)),
                  pl.BlockSpec((1, tm, o.shape[2]), lambda i, j: (i, j, 0)),
                  row,
                  pl.BlockSpec((1, 6, d), lambda i, j: (i, 0, 0)),
                  pl.BlockSpec(w.shape, lambda i, j: (0, 0)),
                  pl.BlockSpec((1, d), lambda i, j: (0, 0)),
                  rws(d), rbs, tris],
        out_specs=[row, row] + r_specs,
        out_shape=[jax.ShapeDtypeStruct((b, l, d), f32),
                   jax.ShapeDtypeStruct((b, l, d), bf16)] + r_shapes,
        scratch_shapes=[pltpu.VMEM((N_EXPERTS, LANES), f32)],
        compiler_params=_cparams(("arbitrary", "arbitrary")),
        name="outproj_router",
    )(yf, o, x, mod, w, g, rw, rb, tri)


def _conv_kernel(seq_len, up_ref, um_ref, un_ref, x_ref, mod_ref, dw_ref, db_ref, lg_ref, lb_ref,
                 w_ref, pb_ref, g_ref, rw_ref, rb_ref, tri_ref,
                 x1_ref, f_ref, ri_ref, wc_ref, cnt_ref, meta_ref, base_ref, ext, conv_out):
    j = pl.program_id(1)
    tm = um_ref.shape[1]
    hl = CONV_HALO
    half = CONV_W // 2
    prev = jnp.where(j > 0, up_ref[0], jnp.zeros_like(up_ref[0]))
    nxt = jnp.where((j + 1) * tm < seq_len, un_ref[0], jnp.zeros_like(un_ref[0]))
    for c in range(ext.shape[0]):
        lanes_c = slice(c * LANES, (c + 1) * LANES)
        ext[c, 0:hl] = prev[:, lanes_c]
        ext[c, hl:hl + tm] = um_ref[0, :, lanes_c]
        ext[c, hl + tm:] = nxt[:, lanes_c]
    base = hl - half
    span = (CONV_W - 1) // SUBLANES * SUBLANES
    rows = CONV_ROWS

    def lane_chunk(c, carry):
        lanes = pl.ds(pl.multiple_of(c * LANES, LANES), LANES)
        for r in range(0, tm, rows):
            part = jnp.broadcast_to(db_ref[:, lanes], (rows, LANES))
            for phase in range(SUBLANES):
                win = ext[c, base + phase + r: base + phase + r + rows + span, :]
                same = None
                for t in range(phase, CONV_W, SUBLANES):
                    term = win[t - phase: t - phase + rows, :] * dw_ref[t:t + 1, lanes]
                    same = term if same is None else same + term
                part = part + same
            conv_out[r:r + rows, lanes] = part
        return carry

    lax.fori_loop(0, um_ref.shape[2] // LANES, lane_chunk, 0)
    fs = []
    for r in range(0, tm, tm // 2):
        rs = slice(r, r + tm // 2)
        acc = conv_out[rs, :]
        mu = jnp.mean(acc, axis=-1, keepdims=True)
        cen = acc - mu
        var = jnp.mean(cen * cen, axis=-1, keepdims=True)
        ln = cen * lax.rsqrt(var + EPS) * lg_ref[...] + lb_ref[...]
        act = ln * jax.nn.sigmoid(ln)
        y = jnp.dot(act.astype(bf16), w_ref[...], preferred_element_type=f32) + pb_ref[...]
        x1 = x_ref[0, rs, :] + mod_ref[0, 2:3, :] * y
        x1_ref[0, rs, :] = x1
        f = _norm_mod(x1, g_ref[...], mod_ref[0, 3:4, :], mod_ref[0, 4:5, :])
        f_ref[0, rs, :] = f.astype(bf16)
        fs.append(f)
    first = (pl.program_id(0) == 0) & (j == 0)
    _route(fs, rw_ref, rb_ref, tri_ref, base_ref, first, ri_ref, wc_ref, cnt_ref, meta_ref)


def _conv(u, x, mod, dw_w, dw_b, ln_g, ln_b, pw2_w, pw2_b, g, rw, rb):
    b, l, d = x.shape
    tm = TOKEN_TILE
    tri = _before_in_tile(tm)
    hl = CONV_HALO
    r = tm // hl
    nh = l // hl
    rws, rbs, tris, r_specs, r_shapes = _route_specs(b, l, tm)
    row = pl.BlockSpec((1, tm, d), lambda i, j: (i, j, 0))
    vec = pl.BlockSpec((1, d), lambda i, j: (0, 0))
    return pl.pallas_call(
        functools.partial(_conv_kernel, l),
        grid=(b, l // tm),
        in_specs=[pl.BlockSpec((1, hl, d), lambda i, j: (i, jnp.maximum(j * r - 1, 0), 0)),
                  row,
                  pl.BlockSpec((1, hl, d), lambda i, j: (i, jnp.minimum(j * r + r, nh - 1), 0)),
                  row,
                  pl.BlockSpec((1, 6, d), lambda i, j: (i, 0, 0)),
                  pl.BlockSpec(dw_w.shape, lambda i, j: (0, 0)),
                  vec, vec, vec,
                  pl.BlockSpec(pw2_w.shape, lambda i, j: (0, 0)),
                  vec, vec, rws(d), rbs, tris],
        out_specs=[row, row] + r_specs,
        out_shape=[jax.ShapeDtypeStruct((b, l, d), f32),
                   jax.ShapeDtypeStruct((b, l, d), bf16)] + r_shapes,
        scratch_shapes=[pltpu.VMEM((N_EXPERTS, LANES), f32),
                        pltpu.VMEM((d // LANES, tm + 2 * hl, LANES), f32),
                        pltpu.VMEM((tm, d), f32)],
        compiler_params=_cparams(("arbitrary", "arbitrary")),
        name="conv_router",
    )(u, u, u, x, mod, dw_w, dw_b, ln_g, ln_b, pw2_w, pw2_b, g, rw, rb, tri)


def _pack_bf16_pairs(x):
    h = x.shape[1] // 2
    lo = lax.bitcast_convert_type(x[:, :h], u32)
    hi = lax.bitcast_convert_type(x[:, h:], u32)
    return (lo >> 16) | (hi & jnp.uint32(0xFFFF0000))


def _unpack_bf16_pairs(u):
    lo = lax.bitcast_convert_type(u << 16, f32)
    hi = lax.bitcast_convert_type(u & jnp.uint32(0xFFFF0000), f32)
    return jnp.concatenate([lo, hi], axis=1).astype(bf16)


def _run_copies(meta, tile, local_ref, hbm_ref, sem, to_hbm):
    start_ref, size_ref, dst_ref = meta
    for e in range(N_EXPERTS):
        k = tile * N_EXPERTS + e
        size = pl.multiple_of(size_ref[k], RUN_ALIGN)

        @pl.when(size > 0)
        def _():
            loc = local_ref.at[pl.ds(pl.multiple_of(start_ref[k], RUN_ALIGN), size)]
            glob = hbm_ref.at[pl.ds(pl.multiple_of(dst_ref[k], RUN_ALIGN), size)]
            if to_hbm:
                pltpu.make_async_copy(loc, glob, sem).start()
            else:
                pltpu.make_async_copy(glob, loc, sem).start()


def _wait_rows(rows, local_ref, hbm_ref, sem):
    rows = pl.multiple_of(rows, RUN_ALIGN)

    @pl.when(rows > 0)
    def _():
        pltpu.make_async_copy(local_ref.at[pl.ds(0, rows)], hbm_ref.at[pl.ds(0, rows)], sem).wait()


def _dispatch_kernel(start_ref, size_ref, dst_ref, tot_ref, tail_start_ref, tail_size_ref, nv_ref,
                     f_ref, lp_ref, xs_ref, loc, zbuf, sem, zsem):
    i = pl.program_id(0)
    n = pl.num_programs(0)
    slot = i % 2
    subs = loc.shape[1]
    rows = loc.shape[2]
    td = f_ref.shape[0] // subs
    meta = (start_ref, size_ref, dst_ref)

    def drain(step, which):
        for s in range(subs):
            _wait_rows(tot_ref[step * subs + s], loc.at[which, s], xs_ref, sem.at[which])

    @pl.when(i >= 2)
    def _():
        drain(i - 2, slot)

    r = lax.broadcasted_iota(i32, (rows, td), 0)
    for s in range(subs):
        cols = slice(s * td, (s + 1) * td)
        pick0 = r == lp_ref[0:1, cols]
        pick1 = r == lp_ref[1:2, cols]
        onehot = (pick0 | pick1).astype(bf16)
        sorted_rows = jnp.dot(onehot, f_ref[cols, :], preferred_element_type=f32)
        half = sorted_rows.shape[1] // 2
        loc[slot, s, :, :half] = _pack_bf16_pairs(sorted_rows)
        w0 = lax.bitcast_convert_type(lp_ref[4:5, cols], f32)
        w1 = lax.bitcast_convert_type(lp_ref[5:6, cols], f32)
        row_w = jnp.sum(jnp.where(pick0, w0, 0.0) + jnp.where(pick1, w1, 0.0), axis=1, keepdims=True)
        loc[slot, s, :, half:] = jnp.broadcast_to(lax.bitcast_convert_type(row_w, u32), (rows, LANES))
        _run_copies(meta, i * subs + s, loc.at[slot, s], xs_ref, sem.at[slot], to_hbm=True)

    @pl.when(i == n - 1)
    def _():
        zbuf[...] = jnp.zeros_like(zbuf)
        total = 0
        for e in range(N_EXPERTS):
            size = pl.multiple_of(tail_size_ref[e], RUN_ALIGN)
            total = total + size

            @pl.when(size > 0)
            def _():
                pltpu.make_async_copy(
                    zbuf.at[pl.ds(0, size)],
                    xs_ref.at[pl.ds(pl.multiple_of(tail_start_ref[e], RUN_ALIGN), size)], zsem).start()

        _wait_rows(total, zbuf, xs_ref, zsem)

        def zero_block(k, c):
            pltpu.make_async_copy(zbuf, xs_ref.at[pl.ds(pl.multiple_of(k * zbuf.shape[0], RUN_ALIGN),
                                                        zbuf.shape[0])], zsem).start()
            return c

        def wait_block(k, c):
            pltpu.make_async_copy(zbuf, xs_ref.at[pl.ds(0, zbuf.shape[0])], zsem).wait()
            return c

        n_blocks = xs_ref.shape[0] // zbuf.shape[0]
        lax.fori_loop(nv_ref[0], n_blocks, zero_block, 0)
        lax.fori_loop(nv_ref[0], n_blocks, wait_block, 0)
        drain(i, slot)

        @pl.when(i >= 1)
        def _():
            drain(i - 1, 1 - slot)


def _dispatch(tables, f2, ri, n_slots):
    t, d = f2.shape
    subs = DISPATCH_SUBTILES
    tm = DISPATCH_TILE * subs
    return pl.pallas_call(
        _dispatch_kernel,
        grid_spec=pltpu.PrefetchScalarGridSpec(
            num_scalar_prefetch=7,
            grid=(t // tm,),
            in_specs=[pl.BlockSpec((tm, d), lambda i, *_: (i, 0)),
                      pl.BlockSpec((8, tm), lambda i, *_: (0, i))],
            out_specs=pl.BlockSpec(memory_space=pl.ANY),
            scratch_shapes=[pltpu.VMEM((2, subs, LOCAL_ROWS, d // 2 + LANES), u32),
                            pltpu.VMEM((EXPERT_ROWS, d // 2 + LANES), u32),
                            pltpu.SemaphoreType.DMA((2,)), pltpu.SemaphoreType.DMA(())]),
        out_shape=jax.ShapeDtypeStruct((n_slots, d // 2 + LANES), u32),
        compiler_params=_cparams(("arbitrary",)),
        name="moe_dispatch",
    )(*tables, f2, ri)


def _expert_kernel(be_ref, nv_ref, x_ref, wg_ref, wu_ref, wd_ref, y_ref, wgb, wub, wdb):
    i = pl.program_id(0)
    changed = jnp.logical_or(i == 0, be_ref[i] != be_ref[jnp.maximum(i - 1, 0)])

    @pl.when(changed)
    def _():
        wgb[...] = wg_ref[0, 0].astype(bf16)
        wub[...] = wu_ref[0, 0].astype(bf16)
        wdb[...] = wd_ref[0, 0].astype(bf16)

    @pl.when(i < nv_ref[0])
    def _():
        half = y_ref.shape[1]
        xb = _unpack_bf16_pairs(x_ref[:, :half])
        row_w = lax.bitcast_convert_type(x_ref[:, half:], f32)
        gate = jnp.dot(xb, wgb[...], preferred_element_type=f32)
        up = jnp.dot(xb, wub[...], preferred_element_type=f32)
        hid = (gate * jax.nn.sigmoid(gate) * up).astype(bf16)
        y = jnp.dot(hid, wdb[...], preferred_element_type=f32)
        y = jnp.concatenate([y[:, c:c + LANES] * row_w for c in range(0, y.shape[1], LANES)], axis=1)
        y_ref[...] = _pack_bf16_pairs(y.astype(bf16).astype(f32))

    @pl.when(i >= nv_ref[0])
    def _():
        y_ref[...] = jnp.zeros_like(y_ref)


def _experts(block_e, n_valid, xs, w_gate, w_up, w_down, layer):
    ns, xw = xs.shape
    tb = EXPERT_ROWS
    d, ff = w_gate.shape[2:]
    dh = d // 2
    xmap = lambda i, be, nv: (jnp.maximum(jnp.minimum(i, nv[0] - 1), 0), 0)
    wmap = lambda i, be, nv: (layer, be[i], 0, 0)
    return pl.pallas_call(
        _expert_kernel,
        grid_spec=pltpu.PrefetchScalarGridSpec(
            num_scalar_prefetch=2,
            grid=(ns // tb,),
            in_specs=[pl.BlockSpec((tb, xw), xmap),
                      pl.BlockSpec((1, 1, d, ff), wmap),
                      pl.BlockSpec((1, 1, d, ff), wmap),
                      pl.BlockSpec((1, 1, ff, d), wmap)],
            out_specs=pl.BlockSpec((tb, dh), lambda i, be, nv: (i, 0)),
            scratch_shapes=[pltpu.VMEM((d, ff), bf16), pltpu.VMEM((d, ff), bf16),
                            pltpu.VMEM((ff, d), bf16)]),
        out_shape=jax.ShapeDtypeStruct((ns, dh), u32),
        compiler_params=_cparams(("arbitrary",)),
        name="moe_experts",
    )(block_e, n_valid, xs, w_gate, w_up, w_down)


def _combine_kernel(final, start_ref, size_ref, dst_ref, tot_ref, ys_ref, wc_ref, x_ref, mod_ref,
                    g_ref, *rest):
    if final:
        o_ref, loc, sem = rest
    else:
        nmod_ref, w_ref, b_ref, o_ref, u_ref, loc, sem = rest
    i = pl.program_id(0)
    n = pl.num_programs(0)
    slot = i % 2
    subs = loc.shape[1]
    rows = loc.shape[2]
    td = x_ref.shape[0] // subs
    meta = (start_ref, size_ref, dst_ref)

    def fetch(step, which):
        for s in range(subs):
            _run_copies(meta, step * subs + s, loc.at[which, s], ys_ref, sem.at[which, s],
                        to_hbm=False)

    @pl.when(i == 0)
    def _():
        loc[...] = jnp.zeros_like(loc)
        fetch(i, slot)

    @pl.when(i + 1 < n)
    def _():
        fetch(i + 1, 1 - slot)

    c = lax.broadcasted_iota(i32, (td, rows), 1)
    parts = []
    for s in range(subs):
        _wait_rows(tot_ref[i * subs + s], loc.at[slot, s], ys_ref, sem.at[slot, s])
        wc = wc_ref[s * td:(s + 1) * td, :]
        sel = ((c == wc[:, 2:3].astype(i32)) | (c == wc[:, 3:4].astype(i32))).astype(bf16)
        parts.append(jnp.dot(sel, _unpack_bf16_pairs(loc[slot, s]), preferred_element_type=f32))
    xo = x_ref[...] + mod_ref[0, 5:6, :] * jnp.concatenate(parts, axis=0)
    if final:
        ms = jnp.mean(xo * xo, axis=-1, keepdims=True)
        o_ref[...] = xo * lax.rsqrt(ms + EPS) * g_ref[...]
    else:
        o_ref[...] = xo
        h = _norm_mod(xo, g_ref[...], nmod_ref[0, 0:1, :], nmod_ref[0, 1:2, :])
        p = jnp.dot(h.astype(bf16), w_ref[...], preferred_element_type=f32) + b_ref[...]
        ch = p.shape[1] // 2
        u_ref[...] = p[:, :ch] * jax.nn.sigmoid(p[:, ch:])


def _combine(tables, ys, wc, x, mod, g, glu=None):
    b, l, d = x.shape
    subs = DISPATCH_SUBTILES if glu is None else GLU_SUBTILES
    tm = DISPATCH_TILE * subs
    per_batch = l // tm
    const = lambda i, *_: (0, 0)
    tile = lambda cols: pl.BlockSpec((tm, cols), lambda i, *_: (i, 0))
    mod_spec = pl.BlockSpec((1, 6, d), lambda i, *_: (i // per_batch, 0, 0))
    in_specs = [pl.BlockSpec(memory_space=pl.ANY), tile(LANES), tile(d), mod_spec,
                pl.BlockSpec((1, d), const)]
    args = [ys, wc, x.reshape(b * l, d), mod, g]
    out_specs = [tile(d)]
    out_shape = [jax.ShapeDtypeStruct((b * l, d), f32)]
    if glu is not None:
        nmod, w, bias = glu
        in_specs += [mod_spec, pl.BlockSpec(w.shape, const), pl.BlockSpec(bias.shape, const)]
        args += [nmod, w, bias]
        out_specs.append(tile(w.shape[1] // 2))
        out_shape.append(jax.ShapeDtypeStruct((b * l, w.shape[1] // 2), f32))
    outs = pl.pallas_call(
        functools.partial(_combine_kernel, glu is None),
        grid_spec=pltpu.PrefetchScalarGridSpec(
            num_scalar_prefetch=4,
            grid=(b * per_batch,),
            in_specs=in_specs,
            out_specs=out_specs,
            scratch_shapes=[pltpu.VMEM((2, subs, LOCAL_ROWS, d // 2), u32),
                            pltpu.SemaphoreType.DMA((2, subs))]),
        out_shape=out_shape,
        compiler_params=_cparams(("arbitrary",)),
        name="moe_combine",
    )(*tables, *args)
    return [o.reshape(b, l, -1) for o in outs]


def _moe(f, routed, x, mod, g, w_gate, w_up, w_down, layer, glu=None):
    ri, wc, cnt, meta = routed
    b, l, d = x.shape
    t = b * l
    tb = EXPERT_ROWS
    n_tiles = t // DISPATCH_TILE
    used = cnt[:, 0].astype(i32)
    region = (used + tb - 1) // tb * tb
    gend = jnp.cumsum(region)
    gstart = gend - region
    max_rows = 2 * t + n_tiles * N_EXPERTS * (RUN_ALIGN - 1) + N_EXPERTS * (tb - 1)
    n_blocks = -(-max_rows // tb)
    m = meta[:, :, :, 0]
    run_start = m[:, 0].reshape(-1)
    run_size = m[:, 1].reshape(-1)
    run_dst = (m[:, 2] + gstart[None, :]).reshape(-1)
    tile_rows = jnp.sum(m[:, 1], axis=1)
    block_row = jnp.arange(n_blocks, dtype=i32) * tb
    block_e = jnp.minimum(jnp.sum((block_row[:, None] >= gend[None, :]).astype(i32), axis=1),
                          N_EXPERTS - 1)
    n_valid = (gend[-1] // tb).reshape(1)
    xs = _dispatch((run_start, run_size, run_dst, tile_rows, gstart + used, region - used, n_valid),
                   f.reshape(t, d), ri, n_blocks * tb)
    ys = _experts(block_e, n_valid, xs, w_gate, w_up, w_down, layer)
    return _combine((run_start, run_size, run_dst, tile_rows), ys, wc, x, mod, g, glu)


def _rope_tables(l):
    lane = jnp.arange(LANES)
    dh = lane % HEAD_DIM
    inv = ROPE_THETA ** (-(dh % 16).astype(f32) / 16.0)
    sign = jnp.where((dh % 32) < 16, -1.0, 1.0).astype(f32)
    by_row = (dh // 32)[None, None, :] == 0
    ang_r = jnp.arange(l // GRID_W, dtype=f32)[:, None] * inv[None, :]
    ang_c = jnp.arange(GRID_W, dtype=f32)[:, None] * inv[None, :]
    cos = jnp.where(by_row, jnp.cos(ang_r)[:, None, :], jnp.cos(ang_c)[None, :, :])
    sin = jnp.where(by_row, jnp.sin(ang_r)[:, None, :], jnp.sin(ang_c)[None, :, :])
    return cos.reshape(l, LANES), (sin * sign[None, None, :]).reshape(l, LANES)


def kernel(x, c, ctx, c_ctx, ada_w, ada_b, norm_mix_g, norm_ffn_g, even_w_in, even_w_out, even_sink, conv_pw1_w, conv_pw1_b, conv_dw_w, conv_dw_b, conv_ln_g, conv_ln_b, conv_pw2_w, conv_pw2_b, router_w, router_b, moe_w_gate, moe_w_up, moe_w_down, final_norm_g):
    b, l, d = x.shape
    depth = ada_w.shape[0]
    assert depth == 2 and b < COND_ROWS
    ctx_row = b
    cond = jnp.zeros((COND_ROWS, d), f32).at[:b].set(c).at[ctx_row].set(c_ctx)
    mods = _adaln(cond, ada_w, ada_b).reshape(depth, COND_ROWS, 6, d)

    heads = jnp.arange(N_HEADS).reshape(N_KV_HEADS, N_HEADS // N_KV_HEADS).T.reshape(-1)
    qperm = (heads[:, None] * HEAD_DIM + jnp.arange(HEAD_DIM)[None, :]).reshape(-1)
    fw = FOURIER_GROUPS * FOURIER_GROUP_W
    qw = N_HEADS * HEAD_DIM
    w_in = even_w_in[0]
    w_in_p = jnp.concatenate([w_in[:, :fw], w_in[:, fw:fw + qw][:, qperm], w_in[:, fw + qw:]],
                             axis=1).astype(bf16)
    w_out = even_w_out[0]
    w_out_p = jnp.concatenate([w_out[:fw], w_out[fw:][qperm]], axis=0).astype(bf16)
    sink_pairs = (even_sink[0].astype(f32) * LOG2E).reshape(N_KV_HEADS, N_HEADS // N_KV_HEADS).T
    sinkcol = jnp.repeat(jnp.repeat(sink_pairs, HEAD_DIM, axis=1), ATT_BLOCK, axis=0)

    cidx = jnp.arange(FOURIER_GROUP_W, dtype=i32)
    angc = ((cidx[:, None] * cidx[None, :]) % FOURIER_GROUP_W).astype(f32) * (2.0 * math.pi / FOURIER_GROUP_W)
    cs = jnp.concatenate([jnp.cos(angc), jnp.sin(angc)], axis=1).astype(bf16)
    cos_t, sin_t = _rope_tables(l)

    rw32 = jnp.zeros((d, LANES), f32).at[:, :N_EXPERTS].set(router_w.astype(f32))
    rw_hi = rw32.astype(bf16)
    rw = jnp.concatenate([rw_hi, (rw32 - rw_hi.astype(f32)).astype(bf16)], axis=1)
    rb = jnp.zeros((1, LANES), f32).at[0, :N_EXPERTS].set(router_b)
    row = lambda v: v.reshape(1, -1)

    q, k, v = _inproj(x, mods[0], row(norm_mix_g[0]), w_in_p[:, fw:], cos_t, sin_t)
    ck, cv = _ctxkv(ctx, mods[0], row(norm_mix_g[0]), w_in_p[:, fw + qw:], ctx_row)
    yf = _fourier(x, mods[0], row(norm_mix_g[0]), w_in_p[:, :fw], cs)
    att = _attention(q, k, v, ck, cv, sinkcol)
    x1, f, *routed = _outproj(yf, att, x, mods[0], w_out_p, row(norm_ffn_g[0]), rw, rb)
    x2, u = _moe(f, routed, x1, mods[0], row(norm_mix_g[1]), moe_w_gate, moe_w_up, moe_w_down,
                 layer=0, glu=(mods[1], conv_pw1_w[0].astype(bf16), row(conv_pw1_b[0])))

    x3, f, *routed = _conv(u, x2, mods[1], conv_dw_w[0], row(conv_dw_b[0]), row(conv_ln_g[0]),
                           row(conv_ln_b[0]), conv_pw2_w[0].astype(bf16), row(conv_pw2_b[0]),
                           row(norm_ffn_g[1]), rw, rb)
    (out,) = _moe(f, routed, x3, mods[1], row(final_norm_g), moe_w_gate, moe_w_up, moe_w_down,
                  layer=1)
    return out
```

```python
import functools
import math

import jax
import jax.numpy as jnp
from jax import lax
from jax.experimental import pallas as pl
from jax.experimental.pallas import tpu as pltpu

f32 = jnp.float32
bf16 = jnp.bfloat16
i32 = jnp.int32
u32 = jnp.uint32
HIGHEST = lax.Precision.HIGHEST

GRID_W = 64
HEAD_DIM = 64
N_HEADS = 8
N_KV_HEADS = 2
WINDOW = 128
ATT_BLOCK = 128
ROPE_THETA = 10000.0
FOURIER_GROUPS = 4
FOURIER_GROUP_W = 128
CONV_W = 31
N_EXPERTS = 16
N_GROUPS = 4
EXPERTS_PER_GROUP = 4
EXPERT_FF = 512
EPS = 1e-6
NEG_INF = -1e30
LOG2E = math.log2(math.e)

LANES = 128
SUBLANES = 8
COND_ROWS = 8
DFT_INNER = 64
TOKEN_TILE = 512
PROJ_TILE = 1024
ATT_TILE = 1024
EXPERT_ROWS = 512
EXPERT_BLOCKS_PER_STEP = 2
DISPATCH_TILE = 256
DISPATCH_SUBTILES = 4
GLU_SUBTILES = 2
RUN_ALIGN = 8
LOCAL_ROWS = -(-(2 * DISPATCH_TILE + N_EXPERTS * (RUN_ALIGN - 1)) // LANES) * LANES
CONV_HALO = 16
CONV_ROWS = 128
VMEM_LIMIT = 56 * 1024 * 1024


def _cparams(sem, vmem=VMEM_LIMIT):
    return pltpu.CompilerParams(dimension_semantics=sem, vmem_limit_bytes=vmem)


def _adaln_kernel(cond_ref, w_ref, b_ref, o_ref):
    s = cond_ref[...]
    s = s * jax.nn.sigmoid(s)
    w = w_ref[0]
    s_hi, w_hi = s.astype(bf16), w.astype(bf16)
    s_lo = (s - s_hi.astype(f32)).astype(bf16)
    w_lo = (w - w_hi.astype(f32)).astype(bf16)
    dot = functools.partial(jnp.dot, preferred_element_type=f32)
    o_ref[0] = dot(s_hi, w_hi) + dot(s_hi, w_lo) + dot(s_lo, w_hi) + b_ref[0]


def _adaln(cond, ada_w, ada_b):
    depth, d, n = ada_w.shape
    tn = 3072
    return pl.pallas_call(
        _adaln_kernel,
        grid=(depth, n // tn),
        in_specs=[pl.BlockSpec((COND_ROWS, d), lambda i, j: (0, 0)),
                  pl.BlockSpec((1, d, tn), lambda i, j: (i, 0, j)),
                  pl.BlockSpec((1, 1, tn), lambda i, j: (i, 0, j))],
        out_specs=pl.BlockSpec((1, COND_ROWS, tn), lambda i, j: (i, 0, j)),
        out_shape=jax.ShapeDtypeStruct((depth, COND_ROWS, n), f32),
        compiler_params=_cparams(("arbitrary", "arbitrary")),
        name="adaln",
    )(cond, ada_w, ada_b.reshape(depth, 1, n))


def _norm_mod(x, g, shift, scale):
    ms = jnp.mean(x * x, axis=-1, keepdims=True)
    return (x * lax.rsqrt(ms + EPS)) * (g * (1.0 + scale)) + shift


def _rope(p, cos, sin_signed, first_half):
    rot = jnp.where(first_half, pltpu.roll(p, LANES - 16, axis=1), pltpu.roll(p, 16, axis=1))
    return p * cos + rot * sin_signed


def _inproj_kernel(x_ref, mod_ref, g_ref, w_ref, cos_ref, sin_ref, q_ref, k_ref, v_ref):
    h = _norm_mod(x_ref[0], g_ref[...], mod_ref[0, 0:1, :], mod_ref[0, 1:2, :])
    p = jnp.dot(h.astype(bf16), w_ref[...], preferred_element_type=f32)
    cos = cos_ref[...]
    sin = sin_ref[...]
    lane = lax.broadcasted_iota(i32, cos.shape, 1)
    first_half = (lane % 32) < 16
    qw = N_HEADS * HEAD_DIM
    for c in range(qw // LANES):
        qc = p[:, c * LANES:(c + 1) * LANES]
        q_ref[0, :, c * LANES:(c + 1) * LANES] = (
            _rope(qc, cos, sin, first_half) * (LOG2E * HEAD_DIM ** -0.5)).astype(bf16)
    k_ref[0] = _rope(p[:, qw:qw + LANES], cos, sin, first_half).astype(bf16)
    v_ref[0] = p[:, qw + LANES:].astype(bf16)


def _inproj(x, mod, g, w, cos_t, sin_t):
    b, l, d = x.shape
    tm = PROJ_TILE
    n = w.shape[1]
    return pl.pallas_call(
        _inproj_kernel,
        grid=(b, l // tm),
        in_specs=[pl.BlockSpec((1, tm, d), lambda i, j: (i, j, 0)),
                  pl.BlockSpec((1, 6, d), lambda i, j: (i, 0, 0)),
                  pl.BlockSpec((1, d), lambda i, j: (0, 0)),
                  pl.BlockSpec((d, n), lambda i, j: (0, 0)),
                  pl.BlockSpec((tm, LANES), lambda i, j: (j, 0)),
                  pl.BlockSpec((tm, LANES), lambda i, j: (j, 0))],
        out_specs=[pl.BlockSpec((1, tm, N_HEADS * HEAD_DIM), lambda i, j: (i, j, 0)),
                   pl.BlockSpec((1, tm, LANES), lambda i, j: (i, j, 0)),
                   pl.BlockSpec((1, tm, LANES), lambda i, j: (i, j, 0))],
        out_shape=[jax.ShapeDtypeStruct((b, l, N_HEADS * HEAD_DIM), bf16),
                   jax.ShapeDtypeStruct((b, l, LANES), bf16),
                   jax.ShapeDtypeStruct((b, l, LANES), bf16)],
        compiler_params=_cparams(("parallel", "parallel")),
        name="inproj",
    )(x, mod, g, w, cos_t, sin_t)


def _ctxkv_kernel(x_ref, mod_ref, g_ref, w_ref, k_ref, v_ref):
    h = _norm_mod(x_ref[0], g_ref[...], mod_ref[0, 0:1, :], mod_ref[0, 1:2, :])
    p = jnp.dot(h.astype(bf16), w_ref[...], preferred_element_type=f32)
    k_ref[0] = p[:, :LANES].astype(bf16)
    v_ref[0] = p[:, LANES:].astype(bf16)


def _ctxkv(ctx, mod, g, w_kv, ctx_row):
    b, c, d = ctx.shape
    return pl.pallas_call(
        _ctxkv_kernel,
        grid=(b,),
        in_specs=[pl.BlockSpec((1, c, d), lambda i: (i, 0, 0)),
                  pl.BlockSpec((1, 6, d), lambda i: (ctx_row, 0, 0)),
                  pl.BlockSpec((1, d), lambda i: (0, 0)),
                  pl.BlockSpec((d, 2 * LANES), lambda i: (0, 0))],
        out_specs=[pl.BlockSpec((1, c, LANES), lambda i: (i, 0, 0)),
                   pl.BlockSpec((1, c, LANES), lambda i: (i, 0, 0))],
        out_shape=[jax.ShapeDtypeStruct((b, c, LANES), bf16),
                   jax.ShapeDtypeStruct((b, c, LANES), bf16)],
        compiler_params=_cparams(("parallel",)),
        name="ctxkv",
    )(ctx, mod, g, w_kv)


def _attn_kernel(seq_len, q_ref, kp_ref, km_ref, kn_ref, vp_ref, vm_ref, vn_ref,
                 ck_ref, cv_ref, sink_ref, o_ref, kext, vext):
    j = pl.program_id(1)
    tq = ATT_TILE
    blk = ATT_BLOCK
    kext[0:blk] = kp_ref[0]
    kext[blk:blk + tq] = km_ref[0]
    kext[blk + tq:] = kn_ref[0]
    vext[:, LANES:] = jnp.ones((tq + 2 * blk, LANES), bf16)
    vext[0:blk, :LANES] = vp_ref[0]
    vext[blk:blk + tq, :LANES] = vm_ref[0]
    vext[blk + tq:, :LANES] = vn_ref[0]
    n_ctx = ck_ref.shape[1]
    nk = n_ctx + 3 * blk
    n_chunks = (N_HEADS * HEAD_DIM) // LANES
    rows = n_chunks * blk
    half = HEAD_DIM
    klow = lax.broadcasted_iota(i32, (1, LANES), 1) < half
    vlane = lax.broadcasted_iota(i32, (1, 2 * LANES), 1)
    vlow = (vlane < half) | ((vlane >= LANES) & (vlane < LANES + half))
    zero = jnp.zeros((), bf16)
    ck = ck_ref[0]
    cvx = jnp.concatenate([cv_ref[0], jnp.ones((n_ctx, LANES), bf16)], axis=1)
    ck_lo, ck_hi = jnp.where(klow, ck, zero), jnp.where(klow, zero, ck)
    cv_lo, cv_hi = jnp.where(vlow, cvx, zero), jnp.where(vlow, zero, cvx)
    sink2 = sink_ref[...]
    sink_lo, sink_hi = sink2[:, 0:1], sink2[:, half:half + 1]
    low = lax.broadcasted_iota(i32, (rows, LANES), 1) < half
    qi = lax.broadcasted_iota(i32, (rows, 3 * blk), 0) % blk
    pk = lax.broadcasted_iota(i32, (rows, 3 * blk), 1)
    band_bias = jnp.where(jnp.abs(pk - blk - qi) <= WINDOW, 0.0, NEG_INF).astype(f32)
    pcol = lax.broadcasted_iota(i32, (1, 3 * blk), 1)
    nt = (((1,), (1,)), ((), ()))

    def row_max(s):
        blocks = [s[:, i:i + LANES] for i in range(0, s.shape[1], LANES)]
        return jnp.max(functools.reduce(jnp.maximum, blocks), axis=1, keepdims=True)

    def sub(s, carry):
        r0 = pl.multiple_of(s * blk, blk)
        qs = q_ref[0, pl.ds(r0, blk), :]
        lhs = jnp.concatenate([qs[:, c * LANES:(c + 1) * LANES] for c in range(n_chunks)], axis=0)
        kl = kext[pl.ds(r0, 3 * blk), :]
        vl = vext[pl.ds(r0, 3 * blk), :]
        kbd = jnp.concatenate([ck_lo, jnp.where(klow, kl, zero),
                               ck_hi, jnp.where(klow, zero, kl)], axis=0)
        vbd = jnp.concatenate([cv_lo, jnp.where(vlow, vl, zero),
                               cv_hi, jnp.where(vlow, zero, vl)], axis=0)
        kpos = j * tq + r0 - blk + pcol
        bias = band_bias + jnp.where((kpos >= 0) & (kpos < seq_len), 0.0, NEG_INF).astype(f32)
        sc = lax.dot_general(lhs, kbd, nt, preferred_element_type=f32)
        s_lo = jnp.concatenate([sc[:, :n_ctx], sc[:, n_ctx:nk] + bias], axis=1)
        s_hi = jnp.concatenate([sc[:, nk:nk + n_ctx], sc[:, nk + n_ctx:] + bias], axis=1)
        m_lo = jnp.maximum(row_max(s_lo), sink_lo)
        m_hi = jnp.maximum(row_max(s_hi), sink_hi)
        e = jnp.concatenate([jnp.exp2(s_lo - m_lo), jnp.exp2(s_hi - m_hi)], axis=1).astype(bf16)
        ov = jnp.dot(e, vbd, preferred_element_type=f32)
        den = ov[:, LANES:] + jnp.exp2(sink2 - jnp.where(low, m_lo, m_hi))
        o = (ov[:, :LANES] / den).astype(bf16)
        for c in range(n_chunks):
            o_ref[0, pl.ds(r0, blk), c * LANES:(c + 1) * LANES] = o[c * blk:(c + 1) * blk]
        return carry

    lax.fori_loop(0, tq // blk, sub, 0, unroll=8)


def _attention(q, k, v, ck, cv, sinkcol):
    b, l, qw = q.shape
    c = ck.shape[1]
    tq = ATT_TILE
    r = tq // ATT_BLOCK
    nb = l // ATT_BLOCK
    prev = pl.BlockSpec((1, ATT_BLOCK, LANES), lambda i, j: (i, jnp.maximum(j * r - 1, 0), 0))
    main = pl.BlockSpec((1, tq, LANES), lambda i, j: (i, j, 0))
    nxt = pl.BlockSpec((1, ATT_BLOCK, LANES), lambda i, j: (i, jnp.minimum(j * r + r, nb - 1), 0))
    cspec = pl.BlockSpec((1, c, LANES), lambda i, j: (i, 0, 0))
    return pl.pallas_call(
        functools.partial(_attn_kernel, l),
        grid=(b, l // tq),
        in_specs=[pl.BlockSpec((1, tq, qw), lambda i, j: (i, j, 0)),
                  prev, main, nxt, prev, main, nxt, cspec, cspec,
                  pl.BlockSpec(sinkcol.shape, lambda i, j: (0, 0))],
        out_specs=pl.BlockSpec((1, tq, qw), lambda i, j: (i, j, 0)),
        out_shape=jax.ShapeDtypeStruct((b, l, qw), bf16),
        scratch_shapes=[pltpu.VMEM((tq + 2 * ATT_BLOCK, LANES), bf16),
                        pltpu.VMEM((tq + 2 * ATT_BLOCK, 2 * LANES), bf16)],
        compiler_params=_cparams(("parallel", "parallel")),
        name="attention",
    )(q, k, k, k, v, v, v, ck, cv, sinkcol)


def _pack_pair(lo, hi):
    lo = lax.bitcast_convert_type(lo.astype(bf16).astype(f32), u32)
    hi = lax.bitcast_convert_type(hi.astype(bf16).astype(f32), u32)
    return (lo >> 16) | (hi & jnp.uint32(0xFFFF0000))


def _fourier1_kernel(x_ref, mod_ref, g_ref, w_ref, cs_ref, m_ref, ct_ref, st_ref, z_ref, ab_ref):
    n1 = x_ref.shape[1]
    nt = x_ref.shape[2]
    x = x_ref[0].reshape(n1 * nt, x_ref.shape[3])
    h = _norm_mod(x, g_ref[...], mod_ref[0, 0:1, :], mod_ref[0, 1:2, :]).astype(bf16)
    p = jnp.dot(h, w_ref[...], preferred_element_type=f32)
    for g in range(FOURIER_GROUPS):
        ug = p[:, g * LANES:(g + 1) * LANES].astype(bf16)
        ab = jnp.dot(ug, cs_ref[...], preferred_element_type=f32)
        ab_ref[0] = ab[:, :LANES]
        ab_ref[1] = ab[:, LANES:]
        for t in range(nt):
            stack = jnp.concatenate([ab_ref[0, pl.ds(t, n1, stride=nt), :],
                                     ab_ref[1, pl.ds(t, n1, stride=nt), :]], axis=0).astype(bf16)
            z = jnp.dot(m_ref[...], stack, preferred_element_type=f32)
            zr, zn = z[:n1], z[n1:]
            ct, st = ct_ref[t], st_ref[t]
            z_ref[0, g, t] = _pack_pair(ct * zr - st * zn, ct * zn + st * zr)


def _fourier2_kernel(scale, z_ref, m_ref, o_ref, zbuf, ybuf):
    _, grp, n2, tk, w = z_ref.shape
    for g in range(grp):
        zbuf[...] = z_ref[0, g].reshape(n2 * tk, w)
        for j in range(tk):
            zp = zbuf[pl.ds(j, n2, stride=tk), :]
            zr = lax.bitcast_convert_type(zp << 16, f32).astype(bf16)
            zn = lax.bitcast_convert_type(zp & jnp.uint32(0xFFFF0000), f32).astype(bf16)
            y = jnp.dot(m_ref[...], jnp.concatenate([zr, zn], axis=0), preferred_element_type=f32)
            ybuf[pl.ds(j, n2, stride=tk), :] = y * scale
        o_ref[0, g] = ybuf[...].reshape(n2, tk, w)


def _fourier(x, mod, g, w_f, cs):
    b, l, d = x.shape
    n2 = DFT_INNER
    n1 = l // n2
    grp, w = FOURIER_GROUPS, FOURIER_GROUP_W
    t2 = SUBLANES
    k1 = jnp.arange(n1, dtype=i32)
    ang1 = ((k1[:, None] * k1[None, :]) % n1).astype(f32) * (2.0 * math.pi / n1)
    c1, s1 = jnp.cos(ang1), jnp.sin(ang1)
    m1 = jnp.concatenate([jnp.concatenate([c1, -s1], axis=1),
                          jnp.concatenate([s1, c1], axis=1)], axis=0).astype(bf16)
    l2 = jnp.arange(n2, dtype=i32)
    angt = ((l2[:, None] * k1[None, :]) % l).astype(f32) * (2.0 * math.pi / l)
    ct = jnp.broadcast_to(jnp.cos(angt)[:, :, None], (n2, n1, w))
    st = jnp.broadcast_to(jnp.sin(angt)[:, :, None], (n2, n1, w))
    ang2 = ((l2[:, None] * l2[None, :]) % n2).astype(f32) * (2.0 * math.pi / n2)
    m2 = jnp.concatenate([jnp.cos(ang2), -jnp.sin(ang2)], axis=1).astype(bf16)

    tspec = pl.BlockSpec((t2, n1, w), lambda t, i: (t, 0, 0))
    z = pl.pallas_call(
        _fourier1_kernel,
        grid=(n2 // t2, b),
        in_specs=[pl.BlockSpec((1, n1, t2, d), lambda t, i: (i, 0, t, 0)),
                  pl.BlockSpec((1, 6, d), lambda t, i: (i, 0, 0)),
                  pl.BlockSpec((1, d), lambda t, i: (0, 0)),
                  pl.BlockSpec(w_f.shape, lambda t, i: (0, 0)),
                  pl.BlockSpec(cs.shape, lambda t, i: (0, 0)),
                  pl.BlockSpec(m1.shape, lambda t, i: (0, 0)), tspec, tspec],
        out_specs=pl.BlockSpec((1, grp, t2, n1, w), lambda t, i: (i, 0, t, 0, 0)),
        out_shape=jax.ShapeDtypeStruct((b, grp, n2, n1, w), u32),
        scratch_shapes=[pltpu.VMEM((2, n1 * t2, w), f32)],
        compiler_params=_cparams(("parallel", "parallel")),
        name="fourier_outer",
    )(x.reshape(b, n1, n2, d), mod, g, w_f, cs, m1, ct, st)

    tk = 2 * SUBLANES
    y = pl.pallas_call(
        functools.partial(_fourier2_kernel, 1.0 / math.sqrt(l * w)),
        grid=(b, n1 // tk),
        in_specs=[pl.BlockSpec((1, grp, n2, tk, w), lambda i, t: (i, 0, 0, t, 0)),
                  pl.BlockSpec(m2.shape, lambda i, t: (0, 0))],
        out_specs=pl.BlockSpec((1, grp, n2, tk, w), lambda i, t: (i, 0, 0, t, 0)),
        out_shape=jax.ShapeDtypeStruct((b, grp, n2, n1, w), f32),
        scratch_shapes=[pltpu.VMEM((n2 * tk, w), u32), pltpu.VMEM((n2 * tk, w), f32)],
        compiler_params=_cparams(("parallel", "parallel")),
        name="fourier_inner",
    )(z, m2)
    return y.reshape(b, grp, l, w)


def _first_max4(a):
    m = jnp.maximum(jnp.maximum(a[0], a[1]), jnp.maximum(a[2], a[3]))
    idx = jnp.where(a[0] == m, 0, jnp.where(a[1] == m, 1, jnp.where(a[2] == m, 2, 3)))
    return m, idx


def _pick4(vals, idx):
    return jnp.where(idx == 0, vals[0], jnp.where(idx == 1, vals[1],
                                                   jnp.where(idx == 2, vals[2], vals[3])))


def _route(f, rw_ref, rb_ref, tri_ref, base_ref, first_step, ri_ref, wc_ref, cnt_ref, meta_ref):
    rw2 = rw_ref[...]
    pieces = []
    for fp in (f if isinstance(f, (list, tuple)) else [f]):
        f_hi = fp.astype(bf16)
        f_lo = (fp - f_hi.astype(f32)).astype(bf16)
        part = jnp.dot(f_hi, rw2, preferred_element_type=f32)
        pieces.append(part[:, :LANES] + part[:, LANES:]
                      + jnp.dot(f_lo, rw2[:, :LANES], preferred_element_type=f32))
    logits = jnp.concatenate(pieces, axis=0)
    tm = logits.shape[0]
    sc = jax.nn.sigmoid(logits)
    st = sc.T
    bt = (sc + rb_ref[...]).T
    neg = jnp.full((1, tm), -jnp.inf, f32)
    gs = []
    for g in range(N_GROUPS):
        a = [bt[4 * g + i: 4 * g + i + 1] for i in range(4)]
        m1, i1 = _first_max4(a)
        rest = [jnp.where(i1 == i, neg, a[i]) for i in range(4)]
        m2, _ = _first_max4(rest)
        gs.append(m1 + m2)
    _, gsel = _first_max4(gs)
    a = [_pick4([bt[4 * g + i: 4 * g + i + 1] for g in range(N_GROUPS)], gsel) for i in range(4)]
    s = [_pick4([st[4 * g + i: 4 * g + i + 1] for g in range(N_GROUPS)], gsel) for i in range(4)]
    _, i1 = _first_max4(a)
    rest = [jnp.where(i1 == i, neg, a[i]) for i in range(4)]
    _, i2 = _first_max4(rest)
    w1 = _pick4(s, i1)
    w2 = _pick4(s, i2)
    tot = w1 + w2
    w1 = w1 / tot
    w2 = w2 / tot
    e0 = gsel * EXPERTS_PER_GROUP + i1
    e1 = gsel * EXPERTS_PER_GROUP + i2

    @pl.when(first_step)
    def _():
        base_ref[...] = jnp.zeros_like(base_ref)

    td = DISPATCH_TILE
    eid = lax.broadcasted_iota(i32, (N_EXPERTS, tm), 0)
    oh0 = (eid == e0).astype(f32)
    oh1 = (eid == e1).astype(f32)
    oh = oh0 + oh1
    before = jnp.dot(oh.astype(bf16), tri_ref[...], preferred_element_type=f32)
    lane_tile = lax.broadcasted_iota(i32, (N_EXPERTS, tm), 1) // td
    ei = lax.broadcasted_iota(i32, (N_EXPERTS, N_EXPERTS), 0)
    ej = lax.broadcasted_iota(i32, (N_EXPERTS, N_EXPERTS), 1)
    strict_lower = (ej < ei).astype(f32)
    run_start = jnp.zeros((N_EXPERTS, tm), f32)
    goff = base_ref[...]
    for s in range(tm // td):
        cnt_s = jnp.sum(oh[:, s * td:(s + 1) * td], axis=1, keepdims=True)
        pad_s = jnp.floor((cnt_s + 7.0) * 0.125) * 8.0
        pad_b = jnp.broadcast_to(pad_s, (N_EXPERTS, LANES))
        start_b = jnp.dot(strict_lower, pad_b, precision=HIGHEST, preferred_element_type=f32)
        run_start = jnp.where(lane_tile == s, start_b[:, 0:1], run_start)
        meta_ref[s, 0] = start_b.astype(i32)
        meta_ref[s, 1] = pad_b.astype(i32)
        meta_ref[s, 2] = goff.astype(i32)
        goff = goff + pad_b
    base_ref[...] = goff
    cnt_ref[...] = goff
    pos = before + run_start
    lp0 = jnp.sum(oh0 * pos, axis=0, keepdims=True)
    lp1 = jnp.sum(oh1 * pos, axis=0, keepdims=True)
    zi = jnp.zeros((1, tm), i32)
    ri_ref[...] = jnp.concatenate(
        [lp0.astype(i32), lp1.astype(i32), e0, e1,
         lax.bitcast_convert_type(w1, i32), lax.bitcast_convert_type(w2, i32), zi, zi], axis=0)
    zf = jnp.zeros((LANES - 4, tm), f32)
    wc_ref[...] = jnp.concatenate([w1, w2, lp0, lp1, zf], axis=0).T


def _outproj_kernel(yf_ref, o_ref, x_ref, mod_ref, w_ref, g_ref, rw_ref, rb_ref, tri_ref,
                    x1_ref, f_ref, ri_ref, wc_ref, cnt_ref, meta_ref, base_ref):
    mix = jnp.concatenate([yf_ref[0, g].astype(bf16) for g in range(FOURIER_GROUPS)] + [o_ref[0]],
                          axis=1)
    y = jnp.dot(mix, w_ref[...], preferred_element_type=f32)
    x1 = x_ref[0] + mod_ref[0, 2:3, :] * y
    x1_ref[0] = x1
    f = _norm_mod(x1, g_ref[...], mod_ref[0, 3:4, :], mod_ref[0, 4:5, :])
    f_ref[0] = f.astype(bf16)
    first = (pl.program_id(0) == 0) & (pl.program_id(1) == 0)
    _route(f, rw_ref, rb_ref, tri_ref, base_ref, first, ri_ref, wc_ref, cnt_ref, meta_ref)


def _before_in_tile(tm):
    tpos = jnp.arange(tm)
    return ((tpos[:, None] < tpos[None, :])
            & (tpos[:, None] // DISPATCH_TILE == tpos[None, :] // DISPATCH_TILE)).astype(bf16)


def _route_specs(b, l, tm):
    nl = l // tm
    rw = lambda d: pl.BlockSpec((d, 2 * LANES), lambda i, j: (0, 0))
    rb = pl.BlockSpec((1, LANES), lambda i, j: (0, 0))
    tri = pl.BlockSpec((tm, tm), lambda i, j: (0, 0))
    ns = tm // DISPATCH_TILE
    out_specs = [pl.BlockSpec((8, tm), lambda i, j: (0, i * nl + j)),
                 pl.BlockSpec((tm, LANES), lambda i, j: (i * nl + j, 0)),
                 pl.BlockSpec((N_EXPERTS, LANES), lambda i, j: (0, 0)),
                 pl.BlockSpec((ns, 3, N_EXPERTS, LANES), lambda i, j: (i * nl + j, 0, 0, 0))]
    out_shape = [jax.ShapeDtypeStruct((8, b * l), i32),
                 jax.ShapeDtypeStruct((b * l, LANES), f32),
                 jax.ShapeDtypeStruct((N_EXPERTS, LANES), f32),
                 jax.ShapeDtypeStruct((b * l // DISPATCH_TILE, 3, N_EXPERTS, LANES), i32)]
    return rw, rb, tri, out_specs, out_shape


def _outproj(yf, o, x, mod, w, g, rw, rb):
    b, l, d = x.shape
    tm = PROJ_TILE
    tri = _before_in_tile(tm)
    rws, rbs, tris, r_specs, r_shapes = _route_specs(b, l, tm)
    row = pl.BlockSpec((1, tm, d), lambda i, j: (i, j, 0))
    return pl.pallas_call(
        _outproj_kernel,
        grid=(b, l // tm),
        in_specs=[pl.BlockSpec((1, FOURIER_GROUPS, tm, LANES), lambda i, j: (i, 0, j, 0)),
                  pl.BlockSpec((1, tm, o.shape[2]), lambda i, j: (i, j, 0)),
                  row,
                  pl.BlockSpec((1, 6, d), lambda i, j: (i, 0, 0)),
                  pl.BlockSpec(w.shape, lambda i, j: (0, 0)),
                  pl.BlockSpec((1, d), lambda i, j: (0, 0)),
                  rws(d), rbs, tris],
        out_specs=[row, row] + r_specs,
        out_shape=[jax.ShapeDtypeStruct((b, l, d), f32),
                   jax.ShapeDtypeStruct((b, l, d), bf16)] + r_shapes,
        scratch_shapes=[pltpu.VMEM((N_EXPERTS, LANES), f32)],
        compiler_params=_cparams(("arbitrary", "arbitrary")),
        name="outproj_router",
    )(yf, o, x, mod, w, g, rw, rb, tri)


def _conv_kernel(seq_len, up_ref, um_ref, un_ref, x_ref, mod_ref, dw_ref, db_ref, lg_ref, lb_ref,
                 w_ref, pb_ref, g_ref, rw_ref, rb_ref, tri_ref,
                 x1_ref, f_ref, ri_ref, wc_ref, cnt_ref, meta_ref, base_ref, ext, conv_out):
    j = pl.program_id(1)
    tm = um_ref.shape[1]
    hl = CONV_HALO
    half = CONV_W // 2
    prev = jnp.where(j > 0, up_ref[0], jnp.zeros_like(up_ref[0]))
    nxt = jnp.where((j + 1) * tm < seq_len, un_ref[0], jnp.zeros_like(un_ref[0]))
    for c in range(ext.shape[0]):
        lanes_c = slice(c * LANES, (c + 1) * LANES)
        ext[c, 0:hl] = prev[:, lanes_c]
        ext[c, hl:hl + tm] = um_ref[0, :, lanes_c]
        ext[c, hl + tm:] = nxt[:, lanes_c]
    base = hl - half
    span = (CONV_W - 1) // SUBLANES * SUBLANES
    rows = CONV_ROWS

    def lane_chunk(c, carry):
        lanes = pl.ds(pl.multiple_of(c * LANES, LANES), LANES)
        for r in range(0, tm, rows):
            part = jnp.broadcast_to(db_ref[:, lanes], (rows, LANES))
            for phase in range(SUBLANES):
                win = ext[c, base + phase + r: base + phase + r + rows + span, :]
                same = None
                for t in range(phase, CONV_W, SUBLANES):
                    term = win[t - phase: t - phase + rows, :] * dw_ref[t:t + 1, lanes]
                    same = term if same is None else same + term
                part = part + same
            conv_out[r:r + rows, lanes] = part
        return carry

    lax.fori_loop(0, um_ref.shape[2] // LANES, lane_chunk, 0)
    fs = []
    for r in range(0, tm, tm // 2):
        rs = slice(r, r + tm // 2)
        acc = conv_out[rs, :]
        mu = jnp.mean(acc, axis=-1, keepdims=True)
        cen = acc - mu
        var = jnp.mean(cen * cen, axis=-1, keepdims=True)
        ln = cen * lax.rsqrt(var + EPS) * lg_ref[...] + lb_ref[...]
        act = ln * jax.nn.sigmoid(ln)
        y = jnp.dot(act.astype(bf16), w_ref[...], preferred_element_type=f32) + pb_ref[...]
        x1 = x_ref[0, rs, :] + mod_ref[0, 2:3, :] * y
        x1_ref[0, rs, :] = x1
        f = _norm_mod(x1, g_ref[...], mod_ref[0, 3:4, :], mod_ref[0, 4:5, :])
        f_ref[0, rs, :] = f.astype(bf16)
        fs.append(f)
    first = (pl.program_id(0) == 0) & (j == 0)
    _route(fs, rw_ref, rb_ref, tri_ref, base_ref, first, ri_ref, wc_ref, cnt_ref, meta_ref)


def _conv(u, x, mod, dw_w, dw_b, ln_g, ln_b, pw2_w, pw2_b, g, rw, rb):
    b, l, d = x.shape
    tm = TOKEN_TILE
    tri = _before_in_tile(tm)
    hl = CONV_HALO
    r = tm // hl
    nh = l // hl
    rws, rbs, tris, r_specs, r_shapes = _route_specs(b, l, tm)
    row = pl.BlockSpec((1, tm, d), lambda i, j: (i, j, 0))
    vec = pl.BlockSpec((1, d), lambda i, j: (0, 0))
    return pl.pallas_call(
        functools.partial(_conv_kernel, l),
        grid=(b, l // tm),
        in_specs=[pl.BlockSpec((1, hl, d), lambda i, j: (i, jnp.maximum(j * r - 1, 0), 0)),
                  row,
                  pl.BlockSpec((1, hl, d), lambda i, j: (i, jnp.minimum(j * r + r, nh - 1), 0)),
                  row,
                  pl.BlockSpec((1, 6, d), lambda i, j: (i, 0, 0)),
                  pl.BlockSpec(dw_w.shape, lambda i, j: (0, 0)),
                  vec, vec, vec,
                  pl.BlockSpec(pw2_w.shape, lambda i, j: (0, 0)),
                  vec, vec, rws(d), rbs, tris],
        out_specs=[row, row] + r_specs,
        out_shape=[jax.ShapeDtypeStruct((b, l, d), f32),
                   jax.ShapeDtypeStruct((b, l, d), bf16)] + r_shapes,
        scratch_shapes=[pltpu.VMEM((N_EXPERTS, LANES), f32),
                        pltpu.VMEM((d // LANES, tm + 2 * hl, LANES), f32),
                        pltpu.VMEM((tm, d), f32)],
        compiler_params=_cparams(("arbitrary", "arbitrary")),
        name="conv_router",
    )(u, u, u, x, mod, dw_w, dw_b, ln_g, ln_b, pw2_w, pw2_b, g, rw, rb, tri)


def _pack_bf16_pairs(x):
    h = x.shape[1] // 2
    lo = lax.bitcast_convert_type(x[:, :h], u32)
    hi = lax.bitcast_convert_type(x[:, h:], u32)
    return (lo >> 16) | (hi & jnp.uint32(0xFFFF0000))


def _unpack_bf16_pairs(u):
    lo = lax.bitcast_convert_type(u << 16, f32)
    hi = lax.bitcast_convert_type(u & jnp.uint32(0xFFFF0000), f32)
    return jnp.concatenate([lo, hi], axis=1).astype(bf16)


def _run_copies(meta, tile, local_ref, hbm_ref, sem, to_hbm):
    start_ref, size_ref, dst_ref = meta
    for e in range(N_EXPERTS):
        k = tile * N_EXPERTS + e
        size = pl.multiple_of(size_ref[k], RUN_ALIGN)

        @pl.when(size > 0)
        def _():
            loc = local_ref.at[pl.ds(pl.multiple_of(start_ref[k], RUN_ALIGN), size)]
            glob = hbm_ref.at[pl.ds(pl.multiple_of(dst_ref[k], RUN_ALIGN), size)]
            if to_hbm:
                pltpu.make_async_copy(loc, glob, sem).start()
            else:
                pltpu.make_async_copy(glob, loc, sem).start()


def _wait_rows(rows, local_ref, hbm_ref, sem):
    rows = pl.multiple_of(rows, RUN_ALIGN)

    @pl.when(rows > 0)
    def _():
        pltpu.make_async_copy(local_ref.at[pl.ds(0, rows)], hbm_ref.at[pl.ds(0, rows)], sem).wait()


def _dispatch_kernel(start_ref, size_ref, dst_ref, tot_ref, tail_start_ref, tail_size_ref, nv_ref,
                     f_ref, lp_ref, xs_ref, loc, zbuf, sem, zsem):
    i = pl.program_id(0)
    n = pl.num_programs(0)
    slot = i % 2
    subs = loc.shape[1]
    rows = loc.shape[2]
    td = f_ref.shape[0] // subs
    meta = (start_ref, size_ref, dst_ref)

    def drain(step, which):
        for s in range(subs):
            _wait_rows(tot_ref[step * subs + s], loc.at[which, s], xs_ref, sem.at[which])

    @pl.when(i >= 2)
    def _():
        drain(i - 2, slot)

    r = lax.broadcasted_iota(i32, (rows, td), 0)
    for s in range(subs):
        cols = slice(s * td, (s + 1) * td)
        pick0 = r == lp_ref[0:1, cols]
        pick1 = r == lp_ref[1:2, cols]
        onehot = (pick0 | pick1).astype(bf16)
        sorted_rows = jnp.dot(onehot, f_ref[cols, :], preferred_element_type=f32)
        half = sorted_rows.shape[1] // 2
        loc[slot, s, :, :half] = _pack_bf16_pairs(sorted_rows)
        w0 = lax.bitcast_convert_type(lp_ref[4:5, cols], f32)
        w1 = lax.bitcast_convert_type(lp_ref[5:6, cols], f32)
        row_w = jnp.sum(jnp.where(pick0, w0, 0.0) + jnp.where(pick1, w1, 0.0), axis=1, keepdims=True)
        loc[slot, s, :, half:] = jnp.broadcast_to(lax.bitcast_convert_type(row_w, u32), (rows, LANES))
        _run_copies(meta, i * subs + s, loc.at[slot, s], xs_ref, sem.at[slot], to_hbm=True)

    @pl.when(i == n - 1)
    def _():
        zbuf[...] = jnp.zeros_like(zbuf)
        total = 0
        for e in range(N_EXPERTS):
            size = pl.multiple_of(tail_size_ref[e], RUN_ALIGN)
            total = total + size

            @pl.when(size > 0)
            def _():
                pltpu.make_async_copy(
                    zbuf.at[pl.ds(0, size)],
                    xs_ref.at[pl.ds(pl.multiple_of(tail_start_ref[e], RUN_ALIGN), size)], zsem).start()

        _wait_rows(total, zbuf, xs_ref, zsem)

        def zero_block(k, c):
            pltpu.make_async_copy(zbuf, xs_ref.at[pl.ds(pl.multiple_of(k * zbuf.shape[0], RUN_ALIGN),
                                                        zbuf.shape[0])], zsem).start()
            return c

        def wait_block(k, c):
            pltpu.make_async_copy(zbuf, xs_ref.at[pl.ds(0, zbuf.shape[0])], zsem).wait()
            return c

        n_blocks = xs_ref.shape[0] // zbuf.shape[0]
        lax.fori_loop(nv_ref[0], n_blocks, zero_block, 0)
        lax.fori_loop(nv_ref[0], n_blocks, wait_block, 0)
        drain(i, slot)

        @pl.when(i >= 1)
        def _():
            drain(i - 1, 1 - slot)


def _dispatch(tables, f2, ri, n_slots):
    t, d = f2.shape
    subs = DISPATCH_SUBTILES
    tm = DISPATCH_TILE * subs
    return pl.pallas_call(
        _dispatch_kernel,
        grid_spec=pltpu.PrefetchScalarGridSpec(
            num_scalar_prefetch=7,
            grid=(t // tm,),
            in_specs=[pl.BlockSpec((tm, d), lambda i, *_: (i, 0)),
                      pl.BlockSpec((8, tm), lambda i, *_: (0, i))],
            out_specs=pl.BlockSpec(memory_space=pl.ANY),
            scratch_shapes=[pltpu.VMEM((2, subs, LOCAL_ROWS, d // 2 + LANES), u32),
                            pltpu.VMEM((EXPERT_ROWS, d // 2 + LANES), u32),
                            pltpu.SemaphoreType.DMA((2,)), pltpu.SemaphoreType.DMA(())]),
        out_shape=jax.ShapeDtypeStruct((n_slots, d // 2 + LANES), u32),
        compiler_params=_cparams(("arbitrary",)),
        name="moe_dispatch",
    )(*tables, f2, ri)


def _expert_kernel(be_ref, nv_ref, x_ref, *refs):
    nb = EXPERT_BLOCKS_PER_STEP
    w_refs, y_ref, w_bf = refs[:3 * nb], refs[3 * nb], refs[3 * nb + 1:]
    i = pl.program_id(0)
    tb = y_ref.shape[0] // nb
    half = y_ref.shape[1]

    for k in range(nb):
        blk = i * nb + k
        changed = jnp.logical_or(i == 0, be_ref[blk] != be_ref[jnp.maximum(blk - nb, 0)])

        @pl.when(changed)
        def _():
            for j in range(3):
                w_bf[3 * k + j][...] = w_refs[3 * k + j][0, 0].astype(bf16)

    @pl.when(i * nb < nv_ref[0])
    def _():
        for k in range(nb):
            rows = slice(k * tb, (k + 1) * tb)
            wgb, wub, wdb = w_bf[3 * k:3 * k + 3]
            xb = _unpack_bf16_pairs(x_ref[rows, :half])
            row_w = lax.bitcast_convert_type(x_ref[rows, half:], f32)
            gate = jnp.dot(xb, wgb[...], preferred_element_type=f32)
            up = jnp.dot(xb, wub[...], preferred_element_type=f32)
            hid = (gate * jax.nn.sigmoid(gate) * up).astype(bf16)
            y = jnp.dot(hid, wdb[...], preferred_element_type=f32)
            y = jnp.concatenate([y[:, c:c + LANES] * row_w for c in range(0, y.shape[1], LANES)],
                                axis=1)
            y_ref[rows, :] = _pack_bf16_pairs(y.astype(bf16).astype(f32))

    @pl.when(i * nb >= nv_ref[0])
    def _():
        y_ref[...] = jnp.zeros_like(y_ref)


def _experts(block_e, n_valid, xs, w_gate, w_up, w_down, layer):
    ns, xw = xs.shape
    nb = EXPERT_BLOCKS_PER_STEP
    tb = EXPERT_ROWS
    d, ff = w_gate.shape[2:]
    dh = d // 2
    w_specs, w_args, w_scratch = [], [], []
    for k in range(nb):
        wmap = lambda i, be, nv, k=k: (layer, be[i * nb + k], 0, 0)
        w_specs += [pl.BlockSpec((1, 1, d, ff), wmap), pl.BlockSpec((1, 1, d, ff), wmap),
                    pl.BlockSpec((1, 1, ff, d), wmap)]
        w_args += [w_gate, w_up, w_down]
        w_scratch += [pltpu.VMEM((d, ff), bf16), pltpu.VMEM((d, ff), bf16), pltpu.VMEM((ff, d), bf16)]
    return pl.pallas_call(
        _expert_kernel,
        grid_spec=pltpu.PrefetchScalarGridSpec(
            num_scalar_prefetch=2,
            grid=(ns // (nb * tb),),
            in_specs=[pl.BlockSpec((nb * tb, xw), lambda i, be, nv: (i, 0))] + w_specs,
            out_specs=pl.BlockSpec((nb * tb, dh), lambda i, be, nv: (i, 0)),
            scratch_shapes=w_scratch),
        out_shape=jax.ShapeDtypeStruct((ns, dh), u32),
        compiler_params=_cparams(("arbitrary",)),
        name="moe_experts",
    )(block_e, n_valid, xs, *w_args)


def _combine_kernel(final, start_ref, size_ref, dst_ref, tot_ref, ys_ref, wc_ref, x_ref, mod_ref,
                    g_ref, *rest):
    if final:
        o_ref, loc, sem = rest
    else:
        nmod_ref, w_ref, b_ref, o_ref, u_ref, loc, sem = rest
    i = pl.program_id(0)
    n = pl.num_programs(0)
    slot = i % 2
    subs = loc.shape[1]
    rows = loc.shape[2]
    td = x_ref.shape[0] // subs
    meta = (start_ref, size_ref, dst_ref)

    def fetch(step, which):
        for s in range(subs):
            _run_copies(meta, step * subs + s, loc.at[which, s], ys_ref, sem.at[which, s],
                        to_hbm=False)

    @pl.when(i == 0)
    def _():
        loc[...] = jnp.zeros_like(loc)
        fetch(i, slot)

    @pl.when(i + 1 < n)
    def _():
        fetch(i + 1, 1 - slot)

    c = lax.broadcasted_iota(i32, (td, rows), 1)
    parts = []
    for s in range(subs):
        _wait_rows(tot_ref[i * subs + s], loc.at[slot, s], ys_ref, sem.at[slot, s])
        wc = wc_ref[s * td:(s + 1) * td, :]
        sel = ((c == wc[:, 2:3].astype(i32)) | (c == wc[:, 3:4].astype(i32))).astype(bf16)
        parts.append(jnp.dot(sel, _unpack_bf16_pairs(loc[slot, s]), preferred_element_type=f32))
    xo = x_ref[...] + mod_ref[0, 5:6, :] * jnp.concatenate(parts, axis=0)
    if final:
        ms = jnp.mean(xo * xo, axis=-1, keepdims=True)
        o_ref[...] = xo * lax.rsqrt(ms + EPS) * g_ref[...]
    else:
        o_ref[...] = xo
        h = _norm_mod(xo, g_ref[...], nmod_ref[0, 0:1, :], nmod_ref[0, 1:2, :])
        p = jnp.dot(h.astype(bf16), w_ref[...], preferred_element_type=f32) + b_ref[...]
        ch = p.shape[1] // 2
        u_ref[...] = p[:, :ch] * jax.nn.sigmoid(p[:, ch:])


def _combine(tables, ys, wc, x, mod, g, glu=None):
    b, l, d = x.shape
    subs = DISPATCH_SUBTILES if glu is None else GLU_SUBTILES
    tm = DISPATCH_TILE * subs
    per_batch = l // tm
    const = lambda i, *_: (0, 0)
    tile = lambda cols: pl.BlockSpec((tm, cols), lambda i, *_: (i, 0))
    mod_spec = pl.BlockSpec((1, 6, d), lambda i, *_: (i // per_batch, 0, 0))
    in_specs = [pl.BlockSpec(memory_space=pl.ANY), tile(LANES), tile(d), mod_spec,
                pl.BlockSpec((1, d), const)]
    args = [ys, wc, x.reshape(b * l, d), mod, g]
    out_specs = [tile(d)]
    out_shape = [jax.ShapeDtypeStruct((b * l, d), f32)]
    if glu is not None:
        nmod, w, bias = glu
        in_specs += [mod_spec, pl.BlockSpec(w.shape, const), pl.BlockSpec(bias.shape, const)]
        args += [nmod, w, bias]
        out_specs.append(tile(w.shape[1] // 2))
        out_shape.append(jax.ShapeDtypeStruct((b * l, w.shape[1] // 2), f32))
    outs = pl.pallas_call(
        functools.partial(_combine_kernel, glu is None),
        grid_spec=pltpu.PrefetchScalarGridSpec(
            num_scalar_prefetch=4,
            grid=(b * per_batch,),
            in_specs=in_specs,
            out_specs=out_specs,
            scratch_shapes=[pltpu.VMEM((2, subs, LOCAL_ROWS, d // 2), u32),
                            pltpu.SemaphoreType.DMA((2, subs))]),
        out_shape=out_shape,
        compiler_params=_cparams(("arbitrary",)),
        name="moe_combine",
    )(*tables, *args)
    return [o.reshape(b, l, -1) for o in outs]


def _moe(f, routed, x, mod, g, w_gate, w_up, w_down, layer, glu=None):
    ri, wc, cnt, meta = routed
    b, l, d = x.shape
    t = b * l
    tb = EXPERT_ROWS
    n_tiles = t // DISPATCH_TILE
    used = cnt[:, 0].astype(i32)
    region = (used + tb - 1) // tb * tb
    gend = jnp.cumsum(region)
    gstart = gend - region
    max_rows = 2 * t + n_tiles * N_EXPERTS * (RUN_ALIGN - 1) + N_EXPERTS * (tb - 1)
    n_blocks = -(-max_rows // (tb * EXPERT_BLOCKS_PER_STEP)) * EXPERT_BLOCKS_PER_STEP
    m = meta[:, :, :, 0]
    run_start = m[:, 0].reshape(-1)
    run_size = m[:, 1].reshape(-1)
    run_dst = (m[:, 2] + gstart[None, :]).reshape(-1)
    tile_rows = jnp.sum(m[:, 1], axis=1)
    block_row = jnp.arange(n_blocks, dtype=i32) * tb
    block_e = jnp.minimum(jnp.sum((block_row[:, None] >= gend[None, :]).astype(i32), axis=1),
                          N_EXPERTS - 1)
    n_valid = (gend[-1] // tb).reshape(1)
    xs = _dispatch((run_start, run_size, run_dst, tile_rows, gstart + used, region - used, n_valid),
                   f.reshape(t, d), ri, n_blocks * tb)
    ys = _experts(block_e, n_valid, xs, w_gate, w_up, w_down, layer)
    return _combine((run_start, run_size, run_dst, tile_rows), ys, wc, x, mod, g, glu)


def _rope_tables(l):
    lane = jnp.arange(LANES)
    dh = lane % HEAD_DIM
    inv = ROPE_THETA ** (-(dh % 16).astype(f32) / 16.0)
    sign = jnp.where((dh % 32) < 16, -1.0, 1.0).astype(f32)
    by_row = (dh // 32)[None, None, :] == 0
    ang_r = jnp.arange(l // GRID_W, dtype=f32)[:, None] * inv[None, :]
    ang_c = jnp.arange(GRID_W, dtype=f32)[:, None] * inv[None, :]
    cos = jnp.where(by_row, jnp.cos(ang_r)[:, None, :], jnp.cos(ang_c)[None, :, :])
    sin = jnp.where(by_row, jnp.sin(ang_r)[:, None, :], jnp.sin(ang_c)[None, :, :])
    return cos.reshape(l, LANES), (sin * sign[None, None, :]).reshape(l, LANES)


def kernel(x, c, ctx, c_ctx, ada_w, ada_b, norm_mix_g, norm_ffn_g, even_w_in, even_w_out, even_sink, conv_pw1_w, conv_pw1_b, conv_dw_w, conv_dw_b, conv_ln_g, conv_ln_b, conv_pw2_w, conv_pw2_b, router_w, router_b, moe_w_gate, moe_w_up, moe_w_down, final_norm_g):
    b, l, d = x.shape
    depth = ada_w.shape[0]
    assert depth == 2 and b < COND_ROWS
    ctx_row = b
    cond = jnp.zeros((COND_ROWS, d), f32).at[:b].set(c).at[ctx_row].set(c_ctx)
    mods = _adaln(cond, ada_w, ada_b).reshape(depth, COND_ROWS, 6, d)

    heads = jnp.arange(N_HEADS).reshape(N_KV_HEADS, N_HEADS // N_KV_HEADS).T.reshape(-1)
    qperm = (heads[:, None] * HEAD_DIM + jnp.arange(HEAD_DIM)[None, :]).reshape(-1)
    fw = FOURIER_GROUPS * FOURIER_GROUP_W
    qw = N_HEADS * HEAD_DIM
    w_in = even_w_in[0]
    w_in_p = jnp.concatenate([w_in[:, :fw], w_in[:, fw:fw + qw][:, qperm], w_in[:, fw + qw:]],
                             axis=1).astype(bf16)
    w_out = even_w_out[0]
    w_out_p = jnp.concatenate([w_out[:fw], w_out[fw:][qperm]], axis=0).astype(bf16)
    sink_pairs = (even_sink[0].astype(f32) * LOG2E).reshape(N_KV_HEADS, N_HEADS // N_KV_HEADS).T
    sinkcol = jnp.repeat(jnp.repeat(sink_pairs, HEAD_DIM, axis=1), ATT_BLOCK, axis=0)

    cidx = jnp.arange(FOURIER_GROUP_W, dtype=i32)
    angc = ((cidx[:, None] * cidx[None, :]) % FOURIER_GROUP_W).astype(f32) * (2.0 * math.pi / FOURIER_GROUP_W)
    cs = jnp.concatenate([jnp.cos(angc), jnp.sin(angc)], axis=1).astype(bf16)
    cos_t, sin_t = _rope_tables(l)

    rw32 = jnp.zeros((d, LANES), f32).at[:, :N_EXPERTS].set(router_w.astype(f32))
    rw_hi = rw32.astype(bf16)
    rw = jnp.concatenate([rw_hi, (rw32 - rw_hi.astype(f32)).astype(bf16)], axis=1)
    rb = jnp.zeros((1, LANES), f32).at[0, :N_EXPERTS].set(router_b)
    row = lambda v: v.reshape(1, -1)

    q, k, v = _inproj(x, mods[0], row(norm_mix_g[0]), w_in_p[:, fw:], cos_t, sin_t)
    ck, cv = _ctxkv(ctx, mods[0], row(norm_mix_g[0]), w_in_p[:, fw + qw:], ctx_row)
    yf = _fourier(x, mods[0], row(norm_mix_g[0]), w_in_p[:, :fw], cs)
    att = _attention(q, k, v, ck, cv, sinkcol)
    x1, f, *routed = _outproj(yf, att, x, mods[0], w_out_p, row(norm_ffn_g[0]), rw, rb)
    x2, u = _moe(f, routed, x1, mods[0], row(norm_mix_g[1]), moe_w_gate, moe_w_up, moe_w_down,
                 layer=0, glu=(mods[1], conv_pw1_w[0].astype(bf16), row(conv_pw1_b[0])))

    x3, f, *routed = _conv(u, x2, mods[1], conv_dw_w[0], row(conv_dw_b[0]), row(conv_ln_g[0]),
                           row(conv_ln_b[0]), conv_pw2_w[0].astype(bf16), row(conv_pw2_b[0]),
                           row(norm_ffn_g[1]), rw, rb)
    (out,) = _moe(f, routed, x3, mods[1], row(final_norm_g), moe_w_gate, moe_w_up, moe_w_down,
                  layer=1)
    return out
```

```python
import functools
import math

import jax
import jax.numpy as jnp
from jax import lax
from jax.experimental import pallas as pl
from jax.experimental.pallas import tpu as pltpu

f32 = jnp.float32
bf16 = jnp.bfloat16
i32 = jnp.int32
u32 = jnp.uint32
HIGHEST = lax.Precision.HIGHEST

GRID_W = 64
HEAD_DIM = 64
N_HEADS = 8
N_KV_HEADS = 2
WINDOW = 128
ATT_BLOCK = 128
ROPE_THETA = 10000.0
FOURIER_GROUPS = 4
FOURIER_GROUP_W = 128
CONV_W = 31
N_EXPERTS = 16
N_GROUPS = 4
EXPERTS_PER_GROUP = 4
EXPERT_FF = 512
EPS = 1e-6
NEG_INF = -1e30
LOG2E = math.log2(math.e)

LANES = 128
SUBLANES = 8
COND_ROWS = 8
DFT_INNER = 64
TOKEN_TILE = 512
PROJ_TILE = 1024
ATT_TILE = 1024
EXPERT_ROWS = 512
EXPERT_BLOCKS_PER_STEP = 2
DISPATCH_TILE = 256
DISPATCH_SUBTILES = 4
GLU_SUBTILES = 2
RUN_ALIGN = 8
LOCAL_ROWS = -(-(2 * DISPATCH_TILE + N_EXPERTS * (RUN_ALIGN - 1)) // LANES) * LANES
CONV_HALO = 16
CONV_ROWS = 128
VMEM_LIMIT = 56 * 1024 * 1024


def _cparams(sem, vmem=VMEM_LIMIT):
    return pltpu.CompilerParams(dimension_semantics=sem, vmem_limit_bytes=vmem)


def _adaln_kernel(cond_ref, w_ref, b_ref, o_ref):
    s = cond_ref[...]
    s = s * jax.nn.sigmoid(s)
    w = w_ref[0]
    s_hi, w_hi = s.astype(bf16), w.astype(bf16)
    s_lo = (s - s_hi.astype(f32)).astype(bf16)
    w_lo = (w - w_hi.astype(f32)).astype(bf16)
    dot = functools.partial(jnp.dot, preferred_element_type=f32)
    o_ref[0] = dot(s_hi, w_hi) + dot(s_hi, w_lo) + dot(s_lo, w_hi) + b_ref[0]


def _adaln(cond, ada_w, ada_b):
    depth, d, n = ada_w.shape
    tn = 3072
    return pl.pallas_call(
        _adaln_kernel,
        grid=(depth, n // tn),
        in_specs=[pl.BlockSpec((COND_ROWS, d), lambda i, j: (0, 0)),
                  pl.BlockSpec((1, d, tn), lambda i, j: (i, 0, j)),
                  pl.BlockSpec((1, 1, tn), lambda i, j: (i, 0, j))],
        out_specs=pl.BlockSpec((1, COND_ROWS, tn), lambda i, j: (i, 0, j)),
        out_shape=jax.ShapeDtypeStruct((depth, COND_ROWS, n), f32),
        compiler_params=_cparams(("arbitrary", "arbitrary")),
        name="adaln",
    )(cond, ada_w, ada_b.reshape(depth, 1, n))


def _norm_mod(x, g, shift, scale):
    ms = jnp.mean(x * x, axis=-1, keepdims=True)
    return (x * lax.rsqrt(ms + EPS)) * (g * (1.0 + scale)) + shift


def _rope(p, cos, sin_signed, first_half):
    rot = jnp.where(first_half, pltpu.roll(p, LANES - 16, axis=1), pltpu.roll(p, 16, axis=1))
    return p * cos + rot * sin_signed


def _inproj_kernel(x_ref, mod_ref, g_ref, w_ref, cos_ref, sin_ref, q_ref, k_ref, v_ref):
    h = _norm_mod(x_ref[0], g_ref[...], mod_ref[0, 0:1, :], mod_ref[0, 1:2, :])
    p = jnp.dot(h.astype(bf16), w_ref[...], preferred_element_type=f32)
    cos = cos_ref[...]
    sin = sin_ref[...]
    lane = lax.broadcasted_iota(i32, cos.shape, 1)
    first_half = (lane % 32) < 16
    qw = N_HEADS * HEAD_DIM
    for c in range(qw // LANES):
        qc = p[:, c * LANES:(c + 1) * LANES]
        q_ref[0, :, c * LANES:(c + 1) * LANES] = (
            _rope(qc, cos, sin, first_half) * (LOG2E * HEAD_DIM ** -0.5)).astype(bf16)
    k_ref[0] = _rope(p[:, qw:qw + LANES], cos, sin, first_half).astype(bf16)
    v_ref[0] = p[:, qw + LANES:].astype(bf16)


def _inproj(x, mod, g, w, cos_t, sin_t):
    b, l, d = x.shape
    tm = PROJ_TILE
    n = w.shape[1]
    return pl.pallas_call(
        _inproj_kernel,
        grid=(b, l // tm),
        in_specs=[pl.BlockSpec((1, tm, d), lambda i, j: (i, j, 0)),
                  pl.BlockSpec((1, 6, d), lambda i, j: (i, 0, 0)),
                  pl.BlockSpec((1, d), lambda i, j: (0, 0)),
                  pl.BlockSpec((d, n), lambda i, j: (0, 0)),
                  pl.BlockSpec((tm, LANES), lambda i, j: (j, 0)),
                  pl.BlockSpec((tm, LANES), lambda i, j: (j, 0))],
        out_specs=[pl.BlockSpec((1, tm, N_HEADS * HEAD_DIM), lambda i, j: (i, j, 0)),
                   pl.BlockSpec((1, tm, LANES), lambda i, j: (i, j, 0)),
                   pl.BlockSpec((1, tm, LANES), lambda i, j: (i, j, 0))],
        out_shape=[jax.ShapeDtypeStruct((b, l, N_HEADS * HEAD_DIM), bf16),
                   jax.ShapeDtypeStruct((b, l, LANES), bf16),
                   jax.ShapeDtypeStruct((b, l, LANES), bf16)],
        compiler_params=_cparams(("parallel", "parallel")),
        name="inproj",
    )(x, mod, g, w, cos_t, sin_t)


def _ctxkv_kernel(x_ref, mod_ref, g_ref, w_ref, k_ref, v_ref):
    h = _norm_mod(x_ref[0], g_ref[...], mod_ref[0, 0:1, :], mod_ref[0, 1:2, :])
    p = jnp.dot(h.astype(bf16), w_ref[...], preferred_element_type=f32)
    k_ref[0] = p[:, :LANES].astype(bf16)
    v_ref[0] = p[:, LANES:].astype(bf16)


def _ctxkv(ctx, mod, g, w_kv, ctx_row):
    b, c, d = ctx.shape
    return pl.pallas_call(
        _ctxkv_kernel,
        grid=(b,),
        in_specs=[pl.BlockSpec((1, c, d), lambda i: (i, 0, 0)),
                  pl.BlockSpec((1, 6, d), lambda i: (ctx_row, 0, 0)),
                  pl.BlockSpec((1, d), lambda i: (0, 0)),
                  pl.BlockSpec((d, 2 * LANES), lambda i: (0, 0))],
        out_specs=[pl.BlockSpec((1, c, LANES), lambda i: (i, 0, 0)),
                   pl.BlockSpec((1, c, LANES), lambda i: (i, 0, 0))],
        out_shape=[jax.ShapeDtypeStruct((b, c, LANES), bf16),
                   jax.ShapeDtypeStruct((b, c, LANES), bf16)],
        compiler_params=_cparams(("parallel",)),
        name="ctxkv",
    )(ctx, mod, g, w_kv)


def _attn_kernel(seq_len, q_ref, kp_ref, km_ref, kn_ref, vp_ref, vm_ref, vn_ref,
                 ck_ref, cv_ref, sink_ref, o_ref, kext, vext):
    j = pl.program_id(1)
    tq = ATT_TILE
    blk = ATT_BLOCK
    kext[0:blk] = kp_ref[0]
    kext[blk:blk + tq] = km_ref[0]
    kext[blk + tq:] = kn_ref[0]
    vext[:, LANES:] = jnp.ones((tq + 2 * blk, LANES), bf16)
    vext[0:blk, :LANES] = vp_ref[0]
    vext[blk:blk + tq, :LANES] = vm_ref[0]
    vext[blk + tq:, :LANES] = vn_ref[0]
    n_ctx = ck_ref.shape[1]
    nk = n_ctx + 3 * blk
    n_chunks = (N_HEADS * HEAD_DIM) // LANES
    rows = n_chunks * blk
    half = HEAD_DIM
    klow = lax.broadcasted_iota(i32, (1, LANES), 1) < half
    vlane = lax.broadcasted_iota(i32, (1, 2 * LANES), 1)
    vlow = (vlane < half) | ((vlane >= LANES) & (vlane < LANES + half))
    zero = jnp.zeros((), bf16)
    ck = ck_ref[0]
    cvx = jnp.concatenate([cv_ref[0], jnp.ones((n_ctx, LANES), bf16)], axis=1)
    ck_lo, ck_hi = jnp.where(klow, ck, zero), jnp.where(klow, zero, ck)
    cv_lo, cv_hi = jnp.where(vlow, cvx, zero), jnp.where(vlow, zero, cvx)
    sink2 = sink_ref[...]
    sink_lo, sink_hi = sink2[:, 0:1], sink2[:, half:half + 1]
    low = lax.broadcasted_iota(i32, (rows, LANES), 1) < half
    qi = lax.broadcasted_iota(i32, (rows, 3 * blk), 0) % blk
    pk = lax.broadcasted_iota(i32, (rows, 3 * blk), 1)
    band_bias = jnp.where(jnp.abs(pk - blk - qi) <= WINDOW, 0.0, NEG_INF).astype(f32)
    pcol = lax.broadcasted_iota(i32, (1, 3 * blk), 1)
    nt = (((1,), (1,)), ((), ()))

    def row_max(s):
        blocks = [s[:, i:i + LANES] for i in range(0, s.shape[1], LANES)]
        return jnp.max(functools.reduce(jnp.maximum, blocks), axis=1, keepdims=True)

    def sub(s, carry):
        r0 = pl.multiple_of(s * blk, blk)
        qs = q_ref[0, pl.ds(r0, blk), :]
        lhs = jnp.concatenate([qs[:, c * LANES:(c + 1) * LANES] for c in range(n_chunks)], axis=0)
        kl = kext[pl.ds(r0, 3 * blk), :]
        vl = vext[pl.ds(r0, 3 * blk), :]
        kbd = jnp.concatenate([ck_lo, jnp.where(klow, kl, zero),
                               ck_hi, jnp.where(klow, zero, kl)], axis=0)
        vbd = jnp.concatenate([cv_lo, jnp.where(vlow, vl, zero),
                               cv_hi, jnp.where(vlow, zero, vl)], axis=0)
        kpos = j * tq + r0 - blk + pcol
        bias = band_bias + jnp.where((kpos >= 0) & (kpos < seq_len), 0.0, NEG_INF).astype(f32)
        sc = lax.dot_general(lhs, kbd, nt, preferred_element_type=f32)
        s_lo = jnp.concatenate([sc[:, :n_ctx], sc[:, n_ctx:nk] + bias], axis=1)
        s_hi = jnp.concatenate([sc[:, nk:nk + n_ctx], sc[:, nk + n_ctx:] + bias], axis=1)
        m_lo = jnp.maximum(row_max(s_lo), sink_lo)
        m_hi = jnp.maximum(row_max(s_hi), sink_hi)
        e = jnp.concatenate([jnp.exp2(s_lo - m_lo), jnp.exp2(s_hi - m_hi)], axis=1).astype(bf16)
        ov = jnp.dot(e, vbd, preferred_element_type=f32)
        den = ov[:, LANES:] + jnp.exp2(sink2 - jnp.where(low, m_lo, m_hi))
        o = (ov[:, :LANES] / den).astype(bf16)
        for c in range(n_chunks):
            o_ref[0, pl.ds(r0, blk), c * LANES:(c + 1) * LANES] = o[c * blk:(c + 1) * blk]
        return carry

    lax.fori_loop(0, tq // blk, sub, 0, unroll=8)


def _attention(q, k, v, ck, cv, sinkcol):
    b, l, qw = q.shape
    c = ck.shape[1]
    tq = ATT_TILE
    r = tq // ATT_BLOCK
    nb = l // ATT_BLOCK
    prev = pl.BlockSpec((1, ATT_BLOCK, LANES), lambda i, j: (i, jnp.maximum(j * r - 1, 0), 0))
    main = pl.BlockSpec((1, tq, LANES), lambda i, j: (i, j, 0))
    nxt = pl.BlockSpec((1, ATT_BLOCK, LANES), lambda i, j: (i, jnp.minimum(j * r + r, nb - 1), 0))
    cspec = pl.BlockSpec((1, c, LANES), lambda i, j: (i, 0, 0))
    return pl.pallas_call(
        functools.partial(_attn_kernel, l),
        grid=(b, l // tq),
        in_specs=[pl.BlockSpec((1, tq, qw), lambda i, j: (i, j, 0)),
                  prev, main, nxt, prev, main, nxt, cspec, cspec,
                  pl.BlockSpec(sinkcol.shape, lambda i, j: (0, 0))],
        out_specs=pl.BlockSpec((1, tq, qw), lambda i, j: (i, j, 0)),
        out_shape=jax.ShapeDtypeStruct((b, l, qw), bf16),
        scratch_shapes=[pltpu.VMEM((tq + 2 * ATT_BLOCK, LANES), bf16),
                        pltpu.VMEM((tq + 2 * ATT_BLOCK, 2 * LANES), bf16)],
        compiler_params=_cparams(("parallel", "parallel")),
        name="attention",
    )(q, k, k, k, v, v, v, ck, cv, sinkcol)


def _pack_pair(lo, hi):
    lo = lax.bitcast_convert_type(lo.astype(bf16).astype(f32), u32)
    hi = lax.bitcast_convert_type(hi.astype(bf16).astype(f32), u32)
    return (lo >> 16) | (hi & jnp.uint32(0xFFFF0000))


def _fourier1_kernel(x_ref, mod_ref, g_ref, w_ref, cs_ref, m_ref, ct_ref, st_ref, z_ref, ab_ref):
    n1 = x_ref.shape[1]
    nt = x_ref.shape[2]
    x = x_ref[0].reshape(n1 * nt, x_ref.shape[3])
    h = _norm_mod(x, g_ref[...], mod_ref[0, 0:1, :], mod_ref[0, 1:2, :]).astype(bf16)
    p = jnp.dot(h, w_ref[...], preferred_element_type=f32)
    for g in range(FOURIER_GROUPS):
        ug = p[:, g * LANES:(g + 1) * LANES].astype(bf16)
        ab = jnp.dot(ug, cs_ref[...], preferred_element_type=f32)
        ab_ref[0] = ab[:, :LANES]
        ab_ref[1] = ab[:, LANES:]
        for t in range(nt):
            stack = jnp.concatenate([ab_ref[0, pl.ds(t, n1, stride=nt), :],
                                     ab_ref[1, pl.ds(t, n1, stride=nt), :]], axis=0).astype(bf16)
            z = jnp.dot(m_ref[...], stack, preferred_element_type=f32)
            zr, zn = z[:n1], z[n1:]
            ct, st = ct_ref[t], st_ref[t]
            z_ref[0, g, t] = _pack_pair(ct * zr - st * zn, ct * zn + st * zr)


def _fourier2_kernel(scale, z_ref, m_ref, o_ref, zbuf, ybuf):
    _, grp, n2, tk, w = z_ref.shape
    for g in range(grp):
        zbuf[...] = z_ref[0, g].reshape(n2 * tk, w)
        for j in range(tk):
            zp = zbuf[pl.ds(j, n2, stride=tk), :]
            zr = lax.bitcast_convert_type(zp << 16, f32).astype(bf16)
            zn = lax.bitcast_convert_type(zp & jnp.uint32(0xFFFF0000), f32).astype(bf16)
            y = jnp.dot(m_ref[...], jnp.concatenate([zr, zn], axis=0), preferred_element_type=f32)
            ybuf[pl.ds(j, n2, stride=tk), :] = y * scale
        o_ref[0, g] = ybuf[...].reshape(n2, tk, w)


def _fourier(x, mod, g, w_f, cs):
    b, l, d = x.shape
    n2 = DFT_INNER
    n1 = l // n2
    grp, w = FOURIER_GROUPS, FOURIER_GROUP_W
    t2 = SUBLANES
    k1 = jnp.arange(n1, dtype=i32)
    ang1 = ((k1[:, None] * k1[None, :]) % n1).astype(f32) * (2.0 * math.pi / n1)
    c1, s1 = jnp.cos(ang1), jnp.sin(ang1)
    m1 = jnp.concatenate([jnp.concatenate([c1, -s1], axis=1),
                          jnp.concatenate([s1, c1], axis=1)], axis=0).astype(bf16)
    l2 = jnp.arange(n2, dtype=i32)
    angt = ((l2[:, None] * k1[None, :]) % l).astype(f32) * (2.0 * math.pi / l)
    ct = jnp.broadcast_to(jnp.cos(angt)[:, :, None], (n2, n1, w))
    st = jnp.broadcast_to(jnp.sin(angt)[:, :, None], (n2, n1, w))
    ang2 = ((l2[:, None] * l2[None, :]) % n2).astype(f32) * (2.0 * math.pi / n2)
    m2 = jnp.concatenate([jnp.cos(ang2), -jnp.sin(ang2)], axis=1).astype(bf16)

    tspec = pl.BlockSpec((t2, n1, w), lambda t, i: (t, 0, 0))
    z = pl.pallas_call(
        _fourier1_kernel,
        grid=(n2 // t2, b),
        in_specs=[pl.BlockSpec((1, n1, t2, d), lambda t, i: (i, 0, t, 0)),
                  pl.BlockSpec((1, 6, d), lambda t, i: (i, 0, 0)),
                  pl.BlockSpec((1, d), lambda t, i: (0, 0)),
                  pl.BlockSpec(w_f.shape, lambda t, i: (0, 0)),
                  pl.BlockSpec(cs.shape, lambda t, i: (0, 0)),
                  pl.BlockSpec(m1.shape, lambda t, i: (0, 0)), tspec, tspec],
        out_specs=pl.BlockSpec((1, grp, t2, n1, w), lambda t, i: (i, 0, t, 0, 0)),
        out_shape=jax.ShapeDtypeStruct((b, grp, n2, n1, w), u32),
        scratch_shapes=[pltpu.VMEM((2, n1 * t2, w), f32)],
        compiler_params=_cparams(("parallel", "parallel")),
        name="fourier_outer",
    )(x.reshape(b, n1, n2, d), mod, g, w_f, cs, m1, ct, st)

    tk = 2 * SUBLANES
    y = pl.pallas_call(
        functools.partial(_fourier2_kernel, 1.0 / math.sqrt(l * w)),
        grid=(b, n1 // tk),
        in_specs=[pl.BlockSpec((1, grp, n2, tk, w), lambda i, t: (i, 0, 0, t, 0)),
                  pl.BlockSpec(m2.shape, lambda i, t: (0, 0))],
        out_specs=pl.BlockSpec((1, grp, n2, tk, w), lambda i, t: (i, 0, 0, t, 0)),
        out_shape=jax.ShapeDtypeStruct((b, grp, n2, n1, w), f32),
        scratch_shapes=[pltpu.VMEM((n2 * tk, w), u32), pltpu.VMEM((n2 * tk, w), f32)],
        compiler_params=_cparams(("parallel", "parallel")),
        name="fourier_inner",
    )(z, m2)
    return y.reshape(b, grp, l, w)


def _first_max4(a):
    m = jnp.maximum(jnp.maximum(a[0], a[1]), jnp.maximum(a[2], a[3]))
    idx = jnp.where(a[0] == m, 0, jnp.where(a[1] == m, 1, jnp.where(a[2] == m, 2, 3)))
    return m, idx


def _pick4(vals, idx):
    return jnp.where(idx == 0, vals[0], jnp.where(idx == 1, vals[1],
                                                   jnp.where(idx == 2, vals[2], vals[3])))


def _route(f, rw_ref, rb_ref, tri_ref, base_ref, first_step, ri_ref, wc_ref, cnt_ref, meta_ref):
    rw2 = rw_ref[...]
    pieces = []
    for fp in (f if isinstance(f, (list, tuple)) else [f]):
        f_hi = fp.astype(bf16)
        f_lo = (fp - f_hi.astype(f32)).astype(bf16)
        part = jnp.dot(f_hi, rw2, preferred_element_type=f32)
        pieces.append(part[:, :LANES] + part[:, LANES:]
                      + jnp.dot(f_lo, rw2[:, :LANES], preferred_element_type=f32))
    logits = jnp.concatenate(pieces, axis=0)
    tm = logits.shape[0]
    sc = jax.nn.sigmoid(logits)
    st = sc.T
    bt = (sc + rb_ref[...]).T
    neg = jnp.full((1, tm), -jnp.inf, f32)
    gs = []
    for g in range(N_GROUPS):
        a = [bt[4 * g + i: 4 * g + i + 1] for i in range(4)]
        m1, i1 = _first_max4(a)
        rest = [jnp.where(i1 == i, neg, a[i]) for i in range(4)]
        m2, _ = _first_max4(rest)
        gs.append(m1 + m2)
    _, gsel = _first_max4(gs)
    a = [_pick4([bt[4 * g + i: 4 * g + i + 1] for g in range(N_GROUPS)], gsel) for i in range(4)]
    s = [_pick4([st[4 * g + i: 4 * g + i + 1] for g in range(N_GROUPS)], gsel) for i in range(4)]
    _, i1 = _first_max4(a)
    rest = [jnp.where(i1 == i, neg, a[i]) for i in range(4)]
    _, i2 = _first_max4(rest)
    w1 = _pick4(s, i1)
    w2 = _pick4(s, i2)
    tot = w1 + w2
    w1 = w1 / tot
    w2 = w2 / tot
    e0 = gsel * EXPERTS_PER_GROUP + i1
    e1 = gsel * EXPERTS_PER_GROUP + i2

    @pl.when(first_step)
    def _():
        base_ref[...] = jnp.zeros_like(base_ref)

    td = DISPATCH_TILE
    eid = lax.broadcasted_iota(i32, (N_EXPERTS, tm), 0)
    oh0 = (eid == e0).astype(f32)
    oh1 = (eid == e1).astype(f32)
    oh = oh0 + oh1
    before = jnp.dot(oh.astype(bf16), tri_ref[...], preferred_element_type=f32)
    lane_tile = lax.broadcasted_iota(i32, (N_EXPERTS, tm), 1) // td
    ei = lax.broadcasted_iota(i32, (N_EXPERTS, N_EXPERTS), 0)
    ej = lax.broadcasted_iota(i32, (N_EXPERTS, N_EXPERTS), 1)
    strict_lower = (ej < ei).astype(f32)
    run_start = jnp.zeros((N_EXPERTS, tm), f32)
    goff = base_ref[...]
    for s in range(tm // td):
        cnt_s = jnp.sum(oh[:, s * td:(s + 1) * td], axis=1, keepdims=True)
        pad_s = jnp.floor((cnt_s + 7.0) * 0.125) * 8.0
        pad_b = jnp.broadcast_to(pad_s, (N_EXPERTS, LANES))
        start_b = jnp.dot(strict_lower, pad_b, precision=HIGHEST, preferred_element_type=f32)
        run_start = jnp.where(lane_tile == s, start_b[:, 0:1], run_start)
        meta_ref[s, 0] = start_b.astype(i32)
        meta_ref[s, 1] = pad_b.astype(i32)
        meta_ref[s, 2] = goff.astype(i32)
        goff = goff + pad_b
    base_ref[...] = goff
    cnt_ref[...] = goff
    pos = before + run_start
    lp0 = jnp.sum(oh0 * pos, axis=0, keepdims=True)
    lp1 = jnp.sum(oh1 * pos, axis=0, keepdims=True)
    zi = jnp.zeros((1, tm), i32)
    ri_ref[...] = jnp.concatenate(
        [lp0.astype(i32), lp1.astype(i32), e0, e1,
         lax.bitcast_convert_type(w1, i32), lax.bitcast_convert_type(w2, i32), zi, zi], axis=0)
    zf = jnp.zeros((LANES - 4, tm), f32)
    wc_ref[...] = jnp.concatenate([w1, w2, lp0, lp1, zf], axis=0).T


def _outproj_kernel(yf_ref, o_ref, x_ref, mod_ref, w_ref, g_ref, rw_ref, rb_ref, tri_ref,
                    x1_ref, f_ref, ri_ref, wc_ref, cnt_ref, meta_ref, base_ref):
    mix = jnp.concatenate([yf_ref[0, g].astype(bf16) for g in range(FOURIER_GROUPS)] + [o_ref[0]],
                          axis=1)
    y = jnp.dot(mix, w_ref[...], preferred_element_type=f32)
    x1 = x_ref[0] + mod_ref[0, 2:3, :] * y
    x1_ref[0] = x1
    f = _norm_mod(x1, g_ref[...], mod_ref[0, 3:4, :], mod_ref[0, 4:5, :])
    f_ref[0] = f.astype(bf16)
    first = (pl.program_id(0) == 0) & (pl.program_id(1) == 0)
    _route(f, rw_ref, rb_ref, tri_ref, base_ref, first, ri_ref, wc_ref, cnt_ref, meta_ref)


def _before_in_tile(tm):
    tpos = jnp.arange(tm)
    return ((tpos[:, None] < tpos[None, :])
            & (tpos[:, None] // DISPATCH_TILE == tpos[None, :] // DISPATCH_TILE)).astype(bf16)


def _route_specs(b, l, tm):
    nl = l // tm
    rw = lambda d: pl.BlockSpec((d, 2 * LANES), lambda i, j: (0, 0))
    rb = pl.BlockSpec((1, LANES), lambda i, j: (0, 0))
    tri = pl.BlockSpec((tm, tm), lambda i, j: (0, 0))
    ns = tm // DISPATCH_TILE
    out_specs = [pl.BlockSpec((8, tm), lambda i, j: (0, i * nl + j)),
                 pl.BlockSpec((tm, LANES), lambda i, j: (i * nl + j, 0)),
                 pl.BlockSpec((N_EXPERTS, LANES), lambda i, j: (0, 0)),
                 pl.BlockSpec((ns, 3, N_EXPERTS, LANES), lambda i, j: (i * nl + j, 0, 0, 0))]
    out_shape = [jax.ShapeDtypeStruct((8, b * l), i32),
                 jax.ShapeDtypeStruct((b * l, LANES), f32),
                 jax.ShapeDtypeStruct((N_EXPERTS, LANES), f32),
                 jax.ShapeDtypeStruct((b * l // DISPATCH_TILE, 3, N_EXPERTS, LANES), i32)]
    return rw, rb, tri, out_specs, out_shape


def _outproj(yf, o, x, mod, w, g, rw, rb):
    b, l, d = x.shape
    tm = PROJ_TILE
    tri = _before_in_tile(tm)
    rws, rbs, tris, r_specs, r_shapes = _route_specs(b, l, tm)
    row = pl.BlockSpec((1, tm, d), lambda i, j: (i, j, 0))
    return pl.pallas_call(
        _outproj_kernel,
        grid=(b, l // tm),
        in_specs=[pl.BlockSpec((1, FOURIER_GROUPS, tm, LANES), lambda i, j: (i, 0, j, 0)),
                  pl.BlockSpec((1, tm, o.shape[2]), lambda i, j: (i, j, 0)),
                  row,
                  pl.BlockSpec((1, 6, d), lambda i, j: (i, 0, 0)),
                  pl.BlockSpec(w.shape, lambda i, j: (0, 0)),
                  pl.BlockSpec((1, d), lambda i, j: (0, 0)),
                  rws(d), rbs, tris],
        out_specs=[row, row] + r_specs,
        out_shape=[jax.ShapeDtypeStruct((b, l, d), f32),
                   jax.ShapeDtypeStruct((b, l, d), bf16)] + r_shapes,
        scratch_shapes=[pltpu.VMEM((N_EXPERTS, LANES), f32)],
        compiler_params=_cparams(("arbitrary", "arbitrary")),
        name="outproj_router",
    )(yf, o, x, mod, w, g, rw, rb, tri)


def _conv_kernel(seq_len, up_ref, um_ref, un_ref, x_ref, mod_ref, dw_ref, db_ref, lg_ref, lb_ref,
                 w_ref, pb_ref, g_ref, rw_ref, rb_ref, tri_ref,
                 x1_ref, f_ref, ri_ref, wc_ref, cnt_ref, meta_ref, base_ref, ext, conv_out):
    j = pl.program_id(1)
    tm = um_ref.shape[1]
    hl = CONV_HALO
    half = CONV_W // 2
    prev = jnp.where(j > 0, up_ref[0], jnp.zeros_like(up_ref[0]))
    nxt = jnp.where((j + 1) * tm < seq_len, un_ref[0], jnp.zeros_like(un_ref[0]))
    for c in range(ext.shape[0]):
        lanes_c = slice(c * LANES, (c + 1) * LANES)
        ext[c, 0:hl] = prev[:, lanes_c]
        ext[c, hl:hl + tm] = um_ref[0, :, lanes_c]
        ext[c, hl + tm:] = nxt[:, lanes_c]
    base = hl - half
    span = (CONV_W - 1) // SUBLANES * SUBLANES
    rows = CONV_ROWS

    def lane_chunk(c, carry):
        lanes = pl.ds(pl.multiple_of(c * LANES, LANES), LANES)
        for r in range(0, tm, rows):
            part = jnp.broadcast_to(db_ref[:, lanes], (rows, LANES))
            for phase in range(SUBLANES):
                win = ext[c, base + phase + r: base + phase + r + rows + span, :]
                same = None
                for t in range(phase, CONV_W, SUBLANES):
                    term = win[t - phase: t - phase + rows, :] * dw_ref[t:t + 1, lanes]
                    same = term if same is None else same + term
                part = part + same
            conv_out[r:r + rows, lanes] = part
        return carry

    lax.fori_loop(0, um_ref.shape[2] // LANES, lane_chunk, 0)
    fs = []
    for r in range(0, tm, tm // 2):
        rs = slice(r, r + tm // 2)
        acc = conv_out[rs, :]
        mu = jnp.mean(acc, axis=-1, keepdims=True)
        cen = acc - mu
        var = jnp.mean(cen * cen, axis=-1, keepdims=True)
        ln = cen * lax.rsqrt(var + EPS) * lg_ref[...] + lb_ref[...]
        act = ln * jax.nn.sigmoid(ln)
        y = jnp.dot(act.astype(bf16), w_ref[...], preferred_element_type=f32) + pb_ref[...]
        x1 = x_ref[0, rs, :] + mod_ref[0, 2:3, :] * y
        x1_ref[0, rs, :] = x1
        f = _norm_mod(x1, g_ref[...], mod_ref[0, 3:4, :], mod_ref[0, 4:5, :])
        f_ref[0, rs, :] = f.astype(bf16)
        fs.append(f)
    first = (pl.program_id(0) == 0) & (j == 0)
    _route(fs, rw_ref, rb_ref, tri_ref, base_ref, first, ri_ref, wc_ref, cnt_ref, meta_ref)


def _conv(u, x, mod, dw_w, dw_b, ln_g, ln_b, pw2_w, pw2_b, g, rw, rb):
    b, l, d = x.shape
    tm = TOKEN_TILE
    tri = _before_in_tile(tm)
    hl = CONV_HALO
    r = tm // hl
    nh = l // hl
    rws, rbs, tris, r_specs, r_shapes = _route_specs(b, l, tm)
    row = pl.BlockSpec((1, tm, d), lambda i, j: (i, j, 0))
    vec = pl.BlockSpec((1, d), lambda i, j: (0, 0))
    return pl.pallas_call(
        functools.partial(_conv_kernel, l),
        grid=(b, l // tm),
        in_specs=[pl.BlockSpec((1, hl, d), lambda i, j: (i, jnp.maximum(j * r - 1, 0), 0)),
                  row,
                  pl.BlockSpec((1, hl, d), lambda i, j: (i, jnp.minimum(j * r + r, nh - 1), 0)),
                  row,
                  pl.BlockSpec((1, 6, d), lambda i, j: (i, 0, 0)),
                  pl.BlockSpec(dw_w.shape, lambda i, j: (0, 0)),
                  vec, vec, vec,
                  pl.BlockSpec(pw2_w.shape, lambda i, j: (0, 0)),
                  vec, vec, rws(d), rbs, tris],
        out_specs=[row, row] + r_specs,
        out_shape=[jax.ShapeDtypeStruct((b, l, d), f32),
                   jax.ShapeDtypeStruct((b, l, d), bf16)] + r_shapes,
        scratch_shapes=[pltpu.VMEM((N_EXPERTS, LANES), f32),
                        pltpu.VMEM((d // LANES, tm + 2 * hl, LANES), f32),
                        pltpu.VMEM((tm, d), f32)],
        compiler_params=_cparams(("arbitrary", "arbitrary")),
        name="conv_router",
    )(u, u, u, x, mod, dw_w, dw_b, ln_g, ln_b, pw2_w, pw2_b, g, rw, rb, tri)


def _pack_bf16_pairs(x):
    h = x.shape[1] // 2
    lo = lax.bitcast_convert_type(x[:, :h], u32)
    hi = lax.bitcast_convert_type(x[:, h:], u32)
    return (lo >> 16) | (hi & jnp.uint32(0xFFFF0000))


def _unpack_bf16_pairs(u):
    lo = lax.bitcast_convert_type(u << 16, f32)
    hi = lax.bitcast_convert_type(u & jnp.uint32(0xFFFF0000), f32)
    return jnp.concatenate([lo, hi], axis=1).astype(bf16)


def _run_copies(meta, tile, local_ref, hbm_ref, sem, to_hbm):
    start_ref, size_ref, dst_ref = meta
    for e in range(N_EXPERTS):
        k = tile * N_EXPERTS + e
        size = pl.multiple_of(size_ref[k], RUN_ALIGN)

        @pl.when(size > 0)
        def _():
            loc = local_ref.at[pl.ds(pl.multiple_of(start_ref[k], RUN_ALIGN), size)]
            glob = hbm_ref.at[pl.ds(pl.multiple_of(dst_ref[k], RUN_ALIGN), size)]
            if to_hbm:
                pltpu.make_async_copy(loc, glob, sem).start()
            else:
                pltpu.make_async_copy(glob, loc, sem).start()


def _wait_rows(rows, local_ref, hbm_ref, sem):
    rows = pl.multiple_of(rows, RUN_ALIGN)

    @pl.when(rows > 0)
    def _():
        pltpu.make_async_copy(local_ref.at[pl.ds(0, rows)], hbm_ref.at[pl.ds(0, rows)], sem).wait()


def _dispatch_kernel(start_ref, size_ref, dst_ref, tot_ref, tail_start_ref, tail_size_ref, nv_ref,
                     f_ref, lp_ref, xs_ref, loc, zbuf, sem, zsem):
    i = pl.program_id(0)
    n = pl.num_programs(0)
    slot = i % 2
    subs = loc.shape[1]
    rows = loc.shape[2]
    td = f_ref.shape[0] // subs
    meta = (start_ref, size_ref, dst_ref)

    def drain(step, which):
        for s in range(subs):
            _wait_rows(tot_ref[step * subs + s], loc.at[which, s], xs_ref, sem.at[which])

    @pl.when(i >= 2)
    def _():
        drain(i - 2, slot)

    r = lax.broadcasted_iota(i32, (rows, td), 0)
    for s in range(subs):
        cols = slice(s * td, (s + 1) * td)
        pick0 = r == lp_ref[0:1, cols]
        pick1 = r == lp_ref[1:2, cols]
        onehot = (pick0 | pick1).astype(bf16)
        sorted_rows = jnp.dot(onehot, f_ref[cols, :], preferred_element_type=f32)
        half = sorted_rows.shape[1] // 2
        loc[slot, s, :, :half] = _pack_bf16_pairs(sorted_rows)
        w0 = lax.bitcast_convert_type(lp_ref[4:5, cols], f32)
        w1 = lax.bitcast_convert_type(lp_ref[5:6, cols], f32)
        row_w = jnp.sum(jnp.where(pick0, w0, 0.0) + jnp.where(pick1, w1, 0.0), axis=1, keepdims=True)
        loc[slot, s, :, half:] = jnp.broadcast_to(lax.bitcast_convert_type(row_w, u32), (rows, LANES))
        _run_copies(meta, i * subs + s, loc.at[slot, s], xs_ref, sem.at[slot], to_hbm=True)

    @pl.when(i == n - 1)
    def _():
        zbuf[...] = jnp.zeros_like(zbuf)
        total = 0
        for e in range(N_EXPERTS):
            size = pl.multiple_of(tail_size_ref[e], RUN_ALIGN)
            total = total + size

            @pl.when(size > 0)
            def _():
                pltpu.make_async_copy(
                    zbuf.at[pl.ds(0, size)],
                    xs_ref.at[pl.ds(pl.multiple_of(tail_start_ref[e], RUN_ALIGN), size)], zsem).start()

        _wait_rows(total, zbuf, xs_ref, zsem)

        def zero_block(k, c):
            pltpu.make_async_copy(zbuf, xs_ref.at[pl.ds(pl.multiple_of(k * zbuf.shape[0], RUN_ALIGN),
                                                        zbuf.shape[0])], zsem).start()
            return c

        def wait_block(k, c):
            pltpu.make_async_copy(zbuf, xs_ref.at[pl.ds(0, zbuf.shape[0])], zsem).wait()
            return c

        n_blocks = xs_ref.shape[0] // zbuf.shape[0]
        lax.fori_loop(nv_ref[0], n_blocks, zero_block, 0)
        lax.fori_loop(nv_ref[0], n_blocks, wait_block, 0)
        drain(i, slot)

        @pl.when(i >= 1)
        def _():
            drain(i - 1, 1 - slot)


def _dispatch(tables, f2, ri, n_slots):
    t, d = f2.shape
    subs = DISPATCH_SUBTILES
    tm = DISPATCH_TILE * subs
    return pl.pallas_call(
        _dispatch_kernel,
        grid_spec=pltpu.PrefetchScalarGridSpec(
            num_scalar_prefetch=7,
            grid=(t // tm,),
            in_specs=[pl.BlockSpec((tm, d), lambda i, *_: (i, 0)),
                      pl.BlockSpec((8, tm), lambda i, *_: (0, i))],
            out_specs=pl.BlockSpec(memory_space=pl.ANY),
            scratch_shapes=[pltpu.VMEM((2, subs, LOCAL_ROWS, d // 2 + LANES), u32),
                            pltpu.VMEM((EXPERT_ROWS, d // 2 + LANES), u32),
                            pltpu.SemaphoreType.DMA((2,)), pltpu.SemaphoreType.DMA(())]),
        out_shape=jax.ShapeDtypeStruct((n_slots, d // 2 + LANES), u32),
        compiler_params=_cparams(("arbitrary",)),
        name="moe_dispatch",
    )(*tables, f2, ri)


def _expert_kernel(be_ref, second_ref, nv_ref, x_ref, wg1, wu1, wd1, wg2, wu2, wd2, y_ref, *w_bf):
    del second_ref
    i = pl.program_id(0)
    tb = y_ref.shape[0] // 2
    half = y_ref.shape[1]
    e1, e2 = be_ref[2 * i], be_ref[2 * i + 1]
    first, second = w_bf[:3], w_bf[3:]

    @pl.when(jnp.logical_or(i == 0, e1 != be_ref[jnp.maximum(2 * i - 2, 0)]))
    def _():
        for dst, src in zip(first, (wg1, wu1, wd1)):
            dst[...] = src[0, 0].astype(bf16)

    @pl.when(e2 != e1)
    def _():
        for dst, src in zip(second, (wg2, wu2, wd2)):
            dst[...] = src[0, 0].astype(bf16)

    def block(k, weights):
        wgb, wub, wdb = weights
        rows = slice(k * tb, (k + 1) * tb)
        xb = _unpack_bf16_pairs(x_ref[rows, :half])
        row_w = lax.bitcast_convert_type(x_ref[rows, half:], f32)
        gate = jnp.dot(xb, wgb[...], preferred_element_type=f32)
        up = jnp.dot(xb, wub[...], preferred_element_type=f32)
        hid = (gate * jax.nn.sigmoid(gate) * up).astype(bf16)
        y = jnp.dot(hid, wdb[...], preferred_element_type=f32)
        y = jnp.concatenate([y[:, c:c + LANES] * row_w for c in range(0, y.shape[1], LANES)], axis=1)
        y_ref[rows, :] = _pack_bf16_pairs(y.astype(bf16).astype(f32))

    used = 2 * i < nv_ref[0]

    @pl.when(used & (e2 == e1))
    def _():
        block(0, first)
        block(1, first)

    @pl.when(used & (e2 != e1))
    def _():
        block(0, first)
        block(1, second)

    @pl.when(jnp.logical_not(used))
    def _():
        y_ref[...] = jnp.zeros_like(y_ref)


def _experts(block_e, n_valid, xs, w_gate, w_up, w_down, layer):
    ns, xw = xs.shape
    nb = EXPERT_BLOCKS_PER_STEP
    tb = EXPERT_ROWS
    d, ff = w_gate.shape[2:]
    dh = d // 2
    e1, e2 = block_e[0::2], block_e[1::2]
    second = lax.cummax(jnp.where(e2 != e1, e2, 0))
    map1 = lambda i, be, sec, nv: (layer, be[2 * i], 0, 0)
    map2 = lambda i, be, sec, nv: (layer, sec[i], 0, 0)
    w_specs = [pl.BlockSpec((1, 1, d, ff), m) for m in (map1, map1)] + [pl.BlockSpec((1, 1, ff, d), map1)]
    w_specs += [pl.BlockSpec((1, 1, d, ff), m) for m in (map2, map2)] + [pl.BlockSpec((1, 1, ff, d), map2)]
    w_scratch = 2 * [pltpu.VMEM((d, ff), bf16), pltpu.VMEM((d, ff), bf16), pltpu.VMEM((ff, d), bf16)]
    return pl.pallas_call(
        _expert_kernel,
        grid_spec=pltpu.PrefetchScalarGridSpec(
            num_scalar_prefetch=3,
            grid=(ns // (nb * tb),),
            in_specs=[pl.BlockSpec((nb * tb, xw), lambda i, *_: (i, 0))] + w_specs,
            out_specs=pl.BlockSpec((nb * tb, dh), lambda i, *_: (i, 0)),
            scratch_shapes=w_scratch),
        out_shape=jax.ShapeDtypeStruct((ns, dh), u32),
        compiler_params=_cparams(("arbitrary",)),
        name="moe_experts",
    )(block_e, second, n_valid, xs, w_gate, w_up, w_down, w_gate, w_up, w_down)


def _combine_kernel(final, start_ref, size_ref, dst_ref, tot_ref, ys_ref, wc_ref, x_ref, mod_ref,
                    g_ref, *rest):
    if final:
        o_ref, loc, sem = rest
    else:
        nmod_ref, w_ref, b_ref, o_ref, u_ref, loc, sem = rest
    i = pl.program_id(0)
    n = pl.num_programs(0)
    slot = i % 2
    subs = loc.shape[1]
    rows = loc.shape[2]
    td = x_ref.shape[0] // subs
    meta = (start_ref, size_ref, dst_ref)

    def fetch(step, which):
        for s in range(subs):
            _run_copies(meta, step * subs + s, loc.at[which, s], ys_ref, sem.at[which, s],
                        to_hbm=False)

    @pl.when(i == 0)
    def _():
        loc[...] = jnp.zeros_like(loc)
        fetch(i, slot)

    @pl.when(i + 1 < n)
    def _():
        fetch(i + 1, 1 - slot)

    c = lax.broadcasted_iota(i32, (td, rows), 1)
    parts = []
    for s in range(subs):
        _wait_rows(tot_ref[i * subs + s], loc.at[slot, s], ys_ref, sem.at[slot, s])
        wc = wc_ref[s * td:(s + 1) * td, :]
        sel = ((c == wc[:, 2:3].astype(i32)) | (c == wc[:, 3:4].astype(i32))).astype(bf16)
        parts.append(jnp.dot(sel, _unpack_bf16_pairs(loc[slot, s]), preferred_element_type=f32))
    xo = x_ref[...] + mod_ref[0, 5:6, :] * jnp.concatenate(parts, axis=0)
    if final:
        ms = jnp.mean(xo * xo, axis=-1, keepdims=True)
        o_ref[...] = xo * lax.rsqrt(ms + EPS) * g_ref[...]
    else:
        o_ref[...] = xo
        h = _norm_mod(xo, g_ref[...], nmod_ref[0, 0:1, :], nmod_ref[0, 1:2, :])
        p = jnp.dot(h.astype(bf16), w_ref[...], preferred_element_type=f32) + b_ref[...]
        ch = p.shape[1] // 2
        u_ref[...] = p[:, :ch] * jax.nn.sigmoid(p[:, ch:])


def _combine(tables, ys, wc, x, mod, g, glu=None):
    b, l, d = x.shape
    subs = DISPATCH_SUBTILES if glu is None else GLU_SUBTILES
    tm = DISPATCH_TILE * subs
    per_batch = l // tm
    const = lambda i, *_: (0, 0)
    tile = lambda cols: pl.BlockSpec((tm, cols), lambda i, *_: (i, 0))
    mod_spec = pl.BlockSpec((1, 6, d), lambda i, *_: (i // per_batch, 0, 0))
    in_specs = [pl.BlockSpec(memory_space=pl.ANY), tile(LANES), tile(d), mod_spec,
                pl.BlockSpec((1, d), const)]
    args = [ys, wc, x.reshape(b * l, d), mod, g]
    out_specs = [tile(d)]
    out_shape = [jax.ShapeDtypeStruct((b * l, d), f32)]
    if glu is not None:
        nmod, w, bias = glu
        in_specs += [mod_spec, pl.BlockSpec(w.shape, const), pl.BlockSpec(bias.shape, const)]
        args += [nmod, w, bias]
        out_specs.append(tile(w.shape[1] // 2))
        out_shape.append(jax.ShapeDtypeStruct((b * l, w.shape[1] // 2), f32))
    outs = pl.pallas_call(
        functools.partial(_combine_kernel, glu is None),
        grid_spec=pltpu.PrefetchScalarGridSpec(
            num_scalar_prefetch=4,
            grid=(b * per_batch,),
            in_specs=in_specs,
            out_specs=out_specs,
            scratch_shapes=[pltpu.VMEM((2, subs, LOCAL_ROWS, d // 2), u32),
                            pltpu.SemaphoreType.DMA((2, subs))]),
        out_shape=out_shape,
        compiler_params=_cparams(("arbitrary",)),
        name="moe_combine",
    )(*tables, *args)
    return [o.reshape(b, l, -1) for o in outs]


def _moe(f, routed, x, mod, g, w_gate, w_up, w_down, layer, glu=None):
    ri, wc, cnt, meta = routed
    b, l, d = x.shape
    t = b * l
    tb = EXPERT_ROWS
    n_tiles = t // DISPATCH_TILE
    used = cnt[:, 0].astype(i32)
    region = (used + tb - 1) // tb * tb
    gend = jnp.cumsum(region)
    gstart = gend - region
    max_rows = 2 * t + n_tiles * N_EXPERTS * (RUN_ALIGN - 1) + N_EXPERTS * (tb - 1)
    n_blocks = -(-max_rows // (tb * EXPERT_BLOCKS_PER_STEP)) * EXPERT_BLOCKS_PER_STEP
    m = meta[:, :, :, 0]
    run_start = m[:, 0].reshape(-1)
    run_size = m[:, 1].reshape(-1)
    run_dst = (m[:, 2] + gstart[None, :]).reshape(-1)
    tile_rows = jnp.sum(m[:, 1], axis=1)
    block_row = jnp.arange(n_blocks, dtype=i32) * tb
    block_e = jnp.minimum(jnp.sum((block_row[:, None] >= gend[None, :]).astype(i32), axis=1),
                          N_EXPERTS - 1)
    n_valid = (gend[-1] // tb).reshape(1)
    xs = _dispatch((run_start, run_size, run_dst, tile_rows, gstart + used, region - used, n_valid),
                   f.reshape(t, d), ri, n_blocks * tb)
    ys = _experts(block_e, n_valid, xs, w_gate, w_up, w_down, layer)
    return _combine((run_start, run_size, run_dst, tile_rows), ys, wc, x, mod, g, glu)


def _rope_tables(l):
    lane = jnp.arange(LANES)
    dh = lane % HEAD_DIM
    inv = ROPE_THETA ** (-(dh % 16).astype(f32) / 16.0)
    sign = jnp.where((dh % 32) < 16, -1.0, 1.0).astype(f32)
    by_row = (dh // 32)[None, None, :] == 0
    ang_r = jnp.arange(l // GRID_W, dtype=f32)[:, None] * inv[None, :]
    ang_c = jnp.arange(GRID_W, dtype=f32)[:, None] * inv[None, :]
    cos = jnp.where(by_row, jnp.cos(ang_r)[:, None, :], jnp.cos(ang_c)[None, :, :])
    sin = jnp.where(by_row, jnp.sin(ang_r)[:, None, :], jnp.sin(ang_c)[None, :, :])
    return cos.reshape(l, LANES), (sin * sign[None, None, :]).reshape(l, LANES)


def kernel(x, c, ctx, c_ctx, ada_w, ada_b, norm_mix_g, norm_ffn_g, even_w_in, even_w_out, even_sink, conv_pw1_w, conv_pw1_b, conv_dw_w, conv_dw_b, conv_ln_g, conv_ln_b, conv_pw2_w, conv_pw2_b, router_w, router_b, moe_w_gate, moe_w_up, moe_w_down, final_norm_g):
    b, l, d = x.shape
    depth = ada_w.shape[0]
    assert depth == 2 and b < COND_ROWS
    ctx_row = b
    cond = jnp.zeros((COND_ROWS, d), f32).at[:b].set(c).at[ctx_row].set(c_ctx)
    mods = _adaln(cond, ada_w, ada_b).reshape(depth, COND_ROWS, 6, d)

    heads = jnp.arange(N_HEADS).reshape(N_KV_HEADS, N_HEADS // N_KV_HEADS).T.reshape(-1)
    qperm = (heads[:, None] * HEAD_DIM + jnp.arange(HEAD_DIM)[None, :]).reshape(-1)
    fw = FOURIER_GROUPS * FOURIER_GROUP_W
    qw = N_HEADS * HEAD_DIM
    w_in = even_w_in[0]
    w_in_p = jnp.concatenate([w_in[:, :fw], w_in[:, fw:fw + qw][:, qperm], w_in[:, fw + qw:]],
                             axis=1).astype(bf16)
    w_out = even_w_out[0]
    w_out_p = jnp.concatenate([w_out[:fw], w_out[fw:][qperm]], axis=0).astype(bf16)
    sink_pairs = (even_sink[0].astype(f32) * LOG2E).reshape(N_KV_HEADS, N_HEADS // N_KV_HEADS).T
    sinkcol = jnp.repeat(jnp.repeat(sink_pairs, HEAD_DIM, axis=1), ATT_BLOCK, axis=0)

    cidx = jnp.arange(FOURIER_GROUP_W, dtype=i32)
    angc = ((cidx[:, None] * cidx[None, :]) % FOURIER_GROUP_W).astype(f32) * (2.0 * math.pi / FOURIER_GROUP_W)
    cs = jnp.concatenate([jnp.cos(angc), jnp.sin(angc)], axis=1).astype(bf16)
    cos_t, sin_t = _rope_tables(l)

    rw32 = jnp.zeros((d, LANES), f32).at[:, :N_EXPERTS].set(router_w.astype(f32))
    rw_hi = rw32.astype(bf16)
    rw = jnp.concatenate([rw_hi, (rw32 - rw_hi.astype(f32)).astype(bf16)], axis=1)
    rb = jnp.zeros((1, LANES), f32).at[0, :N_EXPERTS].set(router_b)
    row = lambda v: v.reshape(1, -1)

    q, k, v = _inproj(x, mods[0], row(norm_mix_g[0]), w_in_p[:, fw:], cos_t, sin_t)
    ck, cv = _ctxkv(ctx, mods[0], row(norm_mix_g[0]), w_in_p[:, fw + qw:], ctx_row)
    yf = _fourier(x, mods[0], row(norm_mix_g[0]), w_in_p[:, :fw], cs)
    att = _attention(q, k, v, ck, cv, sinkcol)
    x1, f, *routed = _outproj(yf, att, x, mods[0], w_out_p, row(norm_ffn_g[0]), rw, rb)
    x2, u = _moe(f, routed, x1, mods[0], row(norm_mix_g[1]), moe_w_gate, moe_w_up, moe_w_down,
                 layer=0, glu=(mods[1], conv_pw1_w[0].astype(bf16), row(conv_pw1_b[0])))

    x3, f, *routed = _conv(u, x2, mods[1], conv_dw_w[0], row(conv_dw_b[0]), row(conv_ln_g[0]),
                           row(conv_ln_b[0]), conv_pw2_w[0].astype(bf16), row(conv_pw2_b[0]),
                           row(norm_ffn_g[1]), rw, rb)
    (out,) = _moe(f, routed, x3, mods[1], row(final_norm_g), moe_w_gate, moe_w_up, moe_w_down,
                  layer=1)
    return out
```

```python
import functools
import math

import jax
import jax.numpy as jnp
from jax import lax
from jax.experimental import pallas as pl
from jax.experimental.pallas import tpu as pltpu

f32 = jnp.float32
bf16 = jnp.bfloat16
i32 = jnp.int32
u32 = jnp.uint32
HIGHEST = lax.Precision.HIGHEST

GRID_W = 64
HEAD_DIM = 64
N_HEADS = 8
N_KV_HEADS = 2
WINDOW = 128
ATT_BLOCK = 128
ROPE_THETA = 10000.0
FOURIER_GROUPS = 4
FOURIER_GROUP_W = 128
CONV_W = 31
N_EXPERTS = 16
N_GROUPS = 4
EXPERTS_PER_GROUP = 4
EPS = 1e-6
NEG_INF = -1e30
LOG2E = math.log2(math.e)

LANES = 128
SUBLANES = 8
COND_ROWS = 8
DFT_INNER = 64
TOKEN_TILE = 512
PROJ_TILE = 1024
ATT_TILE = 1024
EXPERT_ROWS = 512
EXPERT_BLOCKS_PER_STEP = 2
DISPATCH_TILE = 256
DISPATCH_SUBTILES = 4
GLU_SUBTILES = 2
RUN_ALIGN = 8
LOCAL_ROWS = -(-(2 * DISPATCH_TILE + N_EXPERTS * (RUN_ALIGN - 1)) // LANES) * LANES
CONV_HALO = 16
CONV_ROWS = 128
VMEM_LIMIT = 56 * 1024 * 1024


def _cparams(sem, vmem=VMEM_LIMIT):
    return pltpu.CompilerParams(dimension_semantics=sem, vmem_limit_bytes=vmem)


def _adaln_kernel(cond_ref, w_ref, b_ref, o_ref):
    s = cond_ref[...]
    s = s * jax.nn.sigmoid(s)
    w = w_ref[0]
    s_hi, w_hi = s.astype(bf16), w.astype(bf16)
    s_lo = (s - s_hi.astype(f32)).astype(bf16)
    w_lo = (w - w_hi.astype(f32)).astype(bf16)
    dot = functools.partial(jnp.dot, preferred_element_type=f32)
    o_ref[0] = dot(s_hi, w_hi) + dot(s_hi, w_lo) + dot(s_lo, w_hi) + b_ref[0]


def _adaln(cond, ada_w, ada_b):
    depth, d, n = ada_w.shape
    tn = 3072
    return pl.pallas_call(
        _adaln_kernel,
        grid=(depth, n // tn),
        in_specs=[pl.BlockSpec((COND_ROWS, d), lambda i, j: (0, 0)),
                  pl.BlockSpec((1, d, tn), lambda i, j: (i, 0, j)),
                  pl.BlockSpec((1, 1, tn), lambda i, j: (i, 0, j))],
        out_specs=pl.BlockSpec((1, COND_ROWS, tn), lambda i, j: (i, 0, j)),
        out_shape=jax.ShapeDtypeStruct((depth, COND_ROWS, n), f32),
        compiler_params=_cparams(("arbitrary", "arbitrary")),
        name="adaln",
    )(cond, ada_w, ada_b.reshape(depth, 1, n))


def _norm_mod(x, g, shift, scale):
    ms = jnp.mean(x * x, axis=-1, keepdims=True)
    return (x * lax.rsqrt(ms + EPS)) * (g * (1.0 + scale)) + shift


def _rope(p, cos, sin_signed, first_half):
    rot = jnp.where(first_half, pltpu.roll(p, LANES - 16, axis=1), pltpu.roll(p, 16, axis=1))
    return p * cos + rot * sin_signed


def _inproj_kernel(x_ref, mod_ref, g_ref, w_ref, cos_ref, sin_ref, q_ref, k_ref, v_ref):
    qw = N_HEADS * HEAD_DIM
    tm = x_ref.shape[1]
    for r in range(0, tm, tm // 2):
        rs = slice(r, r + tm // 2)
        h = _norm_mod(x_ref[0, rs, :], g_ref[...], mod_ref[0, 0:1, :], mod_ref[0, 1:2, :])
        p = jnp.dot(h.astype(bf16), w_ref[...], preferred_element_type=f32)
        cos = cos_ref[rs, :]
        sin = sin_ref[rs, :]
        lane = lax.broadcasted_iota(i32, cos.shape, 1)
        first_half = (lane % 32) < 16
        for c in range(qw // LANES):
            qc = p[:, c * LANES:(c + 1) * LANES]
            q_ref[0, rs, c * LANES:(c + 1) * LANES] = (
                _rope(qc, cos, sin, first_half) * (LOG2E * HEAD_DIM ** -0.5)).astype(bf16)
        k_ref[0, rs, :] = _rope(p[:, qw:qw + LANES], cos, sin, first_half).astype(bf16)
        v_ref[0, rs, :] = p[:, qw + LANES:].astype(bf16)


def _inproj(x, mod, g, w, cos_t, sin_t):
    b, l, d = x.shape
    tm = PROJ_TILE
    n = w.shape[1]
    return pl.pallas_call(
        _inproj_kernel,
        grid=(b, l // tm),
        in_specs=[pl.BlockSpec((1, tm, d), lambda i, j: (i, j, 0)),
                  pl.BlockSpec((1, 6, d), lambda i, j: (i, 0, 0)),
                  pl.BlockSpec((1, d), lambda i, j: (0, 0)),
                  pl.BlockSpec((d, n), lambda i, j: (0, 0)),
                  pl.BlockSpec((tm, LANES), lambda i, j: (j, 0)),
                  pl.BlockSpec((tm, LANES), lambda i, j: (j, 0))],
        out_specs=[pl.BlockSpec((1, tm, N_HEADS * HEAD_DIM), lambda i, j: (i, j, 0)),
                   pl.BlockSpec((1, tm, LANES), lambda i, j: (i, j, 0)),
                   pl.BlockSpec((1, tm, LANES), lambda i, j: (i, j, 0))],
        out_shape=[jax.ShapeDtypeStruct((b, l, N_HEADS * HEAD_DIM), bf16),
                   jax.ShapeDtypeStruct((b, l, LANES), bf16),
                   jax.ShapeDtypeStruct((b, l, LANES), bf16)],
        compiler_params=_cparams(("parallel", "parallel")),
        name="inproj",
    )(x, mod, g, w, cos_t, sin_t)


def _ctxkv_kernel(x_ref, mod_ref, g_ref, w_ref, k_ref, v_ref):
    h = _norm_mod(x_ref[0], g_ref[...], mod_ref[0, 0:1, :], mod_ref[0, 1:2, :])
    p = jnp.dot(h.astype(bf16), w_ref[...], preferred_element_type=f32)
    k_ref[0] = p[:, :LANES].astype(bf16)
    v_ref[0] = p[:, LANES:].astype(bf16)


def _ctxkv(ctx, mod, g, w_kv, ctx_row):
    b, c, d = ctx.shape
    return pl.pallas_call(
        _ctxkv_kernel,
        grid=(b,),
        in_specs=[pl.BlockSpec((1, c, d), lambda i: (i, 0, 0)),
                  pl.BlockSpec((1, 6, d), lambda i: (ctx_row, 0, 0)),
                  pl.BlockSpec((1, d), lambda i: (0, 0)),
                  pl.BlockSpec((d, 2 * LANES), lambda i: (0, 0))],
        out_specs=[pl.BlockSpec((1, c, LANES), lambda i: (i, 0, 0)),
                   pl.BlockSpec((1, c, LANES), lambda i: (i, 0, 0))],
        out_shape=[jax.ShapeDtypeStruct((b, c, LANES), bf16),
                   jax.ShapeDtypeStruct((b, c, LANES), bf16)],
        compiler_params=_cparams(("parallel",)),
        name="ctxkv",
    )(ctx, mod, g, w_kv)


def _attn_kernel(seq_len, q_ref, kp_ref, km_ref, kn_ref, vp_ref, vm_ref, vn_ref,
                 ck_ref, cv_ref, sink_ref, o_ref, kext, vext):
    j = pl.program_id(1)
    tq = ATT_TILE
    blk = ATT_BLOCK
    kext[0:blk] = kp_ref[0]
    kext[blk:blk + tq] = km_ref[0]
    kext[blk + tq:] = kn_ref[0]
    vext[:, LANES:] = jnp.ones((tq + 2 * blk, LANES), bf16)
    vext[0:blk, :LANES] = vp_ref[0]
    vext[blk:blk + tq, :LANES] = vm_ref[0]
    vext[blk + tq:, :LANES] = vn_ref[0]
    n_ctx = ck_ref.shape[1]
    nk = n_ctx + 3 * blk
    n_chunks = (N_HEADS * HEAD_DIM) // LANES
    rows = n_chunks * blk
    half = HEAD_DIM
    klow = lax.broadcasted_iota(i32, (1, LANES), 1) < half
    vlane = lax.broadcasted_iota(i32, (1, 2 * LANES), 1)
    vlow = (vlane < half) | ((vlane >= LANES) & (vlane < LANES + half))
    zero = jnp.zeros((), bf16)
    ck = ck_ref[0]
    cvx = jnp.concatenate([cv_ref[0], jnp.ones((n_ctx, LANES), bf16)], axis=1)
    ck_lo, ck_hi = jnp.where(klow, ck, zero), jnp.where(klow, zero, ck)
    cv_lo, cv_hi = jnp.where(vlow, cvx, zero), jnp.where(vlow, zero, cvx)
    sink2 = sink_ref[...]
    sink_lo, sink_hi = sink2[:, 0:1], sink2[:, half:half + 1]
    low = lax.broadcasted_iota(i32, (rows, LANES), 1) < half
    qi = lax.broadcasted_iota(i32, (rows, 3 * blk), 0) % blk
    pk = lax.broadcasted_iota(i32, (rows, 3 * blk), 1)
    band_bias = jnp.where(jnp.abs(pk - blk - qi) <= WINDOW, 0.0, NEG_INF).astype(f32)
    pcol = lax.broadcasted_iota(i32, (1, 3 * blk), 1)
    nt = (((1,), (1,)), ((), ()))

    def row_max(s):
        blocks = [s[:, i:i + LANES] for i in range(0, s.shape[1], LANES)]
        return jnp.max(functools.reduce(jnp.maximum, blocks), axis=1, keepdims=True)

    def sub(s, carry):
        r0 = pl.multiple_of(s * blk, blk)
        qs = q_ref[0, pl.ds(r0, blk), :]
        lhs = jnp.concatenate([qs[:, c * LANES:(c + 1) * LANES] for c in range(n_chunks)], axis=0)
        kl = kext[pl.ds(r0, 3 * blk), :]
        vl = vext[pl.ds(r0, 3 * blk), :]
        kbd = jnp.concatenate([ck_lo, jnp.where(klow, kl, zero),
                               ck_hi, jnp.where(klow, zero, kl)], axis=0)
        vbd = jnp.concatenate([cv_lo, jnp.where(vlow, vl, zero),
                               cv_hi, jnp.where(vlow, zero, vl)], axis=0)
        kpos = j * tq + r0 - blk + pcol
        bias = band_bias + jnp.where((kpos >= 0) & (kpos < seq_len), 0.0, NEG_INF).astype(f32)
        sc = lax.dot_general(lhs, kbd, nt, preferred_element_type=f32)
        s_lo = jnp.concatenate([sc[:, :n_ctx], sc[:, n_ctx:nk] + bias], axis=1)
        s_hi = jnp.concatenate([sc[:, nk:nk + n_ctx], sc[:, nk + n_ctx:] + bias], axis=1)
        m_lo = jnp.maximum(row_max(s_lo), sink_lo)
        m_hi = jnp.maximum(row_max(s_hi), sink_hi)
        e = jnp.concatenate([jnp.exp2(s_lo - m_lo), jnp.exp2(s_hi - m_hi)], axis=1).astype(bf16)
        ov = jnp.dot(e, vbd, preferred_element_type=f32)
        den = ov[:, LANES:] + jnp.exp2(sink2 - jnp.where(low, m_lo, m_hi))
        o = (ov[:, :LANES] / den).astype(bf16)
        for c in range(n_chunks):
            o_ref[0, pl.ds(r0, blk), c * LANES:(c + 1) * LANES] = o[c * blk:(c + 1) * blk]
        return carry

    lax.fori_loop(0, tq // blk, sub, 0, unroll=8)


def _attention(q, k, v, ck, cv, sinkcol):
    b, l, qw = q.shape
    c = ck.shape[1]
    tq = ATT_TILE
    r = tq // ATT_BLOCK
    nb = l // ATT_BLOCK
    prev = pl.BlockSpec((1, ATT_BLOCK, LANES), lambda i, j: (i, jnp.maximum(j * r - 1, 0), 0))
    main = pl.BlockSpec((1, tq, LANES), lambda i, j: (i, j, 0))
    nxt = pl.BlockSpec((1, ATT_BLOCK, LANES), lambda i, j: (i, jnp.minimum(j * r + r, nb - 1), 0))
    cspec = pl.BlockSpec((1, c, LANES), lambda i, j: (i, 0, 0))
    return pl.pallas_call(
        functools.partial(_attn_kernel, l),
        grid=(b, l // tq),
        in_specs=[pl.BlockSpec((1, tq, qw), lambda i, j: (i, j, 0)),
                  prev, main, nxt, prev, main, nxt, cspec, cspec,
                  pl.BlockSpec(sinkcol.shape, lambda i, j: (0, 0))],
        out_specs=pl.BlockSpec((1, tq, qw), lambda i, j: (i, j, 0)),
        out_shape=jax.ShapeDtypeStruct((b, l, qw), bf16),
        scratch_shapes=[pltpu.VMEM((tq + 2 * ATT_BLOCK, LANES), bf16),
                        pltpu.VMEM((tq + 2 * ATT_BLOCK, 2 * LANES), bf16)],
        compiler_params=_cparams(("parallel", "parallel")),
        name="attention",
    )(q, k, k, k, v, v, v, ck, cv, sinkcol)


def _pack_pair(lo, hi):
    lo = lax.bitcast_convert_type(lo.astype(bf16).astype(f32), u32)
    hi = lax.bitcast_convert_type(hi.astype(bf16).astype(f32), u32)
    return (lo >> 16) | (hi & jnp.uint32(0xFFFF0000))


def _fourier1_kernel(x_ref, mod_ref, g_ref, w_ref, cs_ref, m_ref, ct_ref, st_ref, z_ref, ab_ref):
    n1 = x_ref.shape[1]
    nt = x_ref.shape[2]
    x = x_ref[0].reshape(n1 * nt, x_ref.shape[3])
    h = _norm_mod(x, g_ref[...], mod_ref[0, 0:1, :], mod_ref[0, 1:2, :]).astype(bf16)
    p = jnp.dot(h, w_ref[...], preferred_element_type=f32)
    for g in range(FOURIER_GROUPS):
        ug = p[:, g * LANES:(g + 1) * LANES].astype(bf16)
        ab = jnp.dot(ug, cs_ref[...], preferred_element_type=f32)
        ab_ref[0] = ab[:, :LANES]
        ab_ref[1] = ab[:, LANES:]
        for t in range(nt):
            stack = jnp.concatenate([ab_ref[0, pl.ds(t, n1, stride=nt), :],
                                     ab_ref[1, pl.ds(t, n1, stride=nt), :]], axis=0).astype(bf16)
            z = jnp.dot(m_ref[...], stack, preferred_element_type=f32)
            zr, zn = z[:n1], z[n1:]
            ct, st = ct_ref[t], st_ref[t]
            z_ref[0, g, t] = _pack_pair(ct * zr - st * zn, ct * zn + st * zr)


def _fourier2_kernel(scale, z_ref, m_ref, o_ref, zbuf, ybuf):
    _, grp, n2, tk, w = z_ref.shape
    for g in range(grp):
        zbuf[...] = z_ref[0, g].reshape(n2 * tk, w)
        for j in range(tk):
            zp = zbuf[pl.ds(j, n2, stride=tk), :]
            zr = lax.bitcast_convert_type(zp << 16, f32).astype(bf16)
            zn = lax.bitcast_convert_type(zp & jnp.uint32(0xFFFF0000), f32).astype(bf16)
            y = jnp.dot(m_ref[...], jnp.concatenate([zr, zn], axis=0), preferred_element_type=f32)
            ybuf[pl.ds(j, n2, stride=tk), :] = y * scale
        o_ref[0, g] = ybuf[...].reshape(n2, tk, w)


def _fourier(x, mod, g, w_f, cs):
    b, l, d = x.shape
    n2 = DFT_INNER
    n1 = l // n2
    grp, w = FOURIER_GROUPS, FOURIER_GROUP_W
    t2 = SUBLANES
    k1 = jnp.arange(n1, dtype=i32)
    ang1 = ((k1[:, None] * k1[None, :]) % n1).astype(f32) * (2.0 * math.pi / n1)
    c1, s1 = jnp.cos(ang1), jnp.sin(ang1)
    m1 = jnp.concatenate([jnp.concatenate([c1, -s1], axis=1),
                          jnp.concatenate([s1, c1], axis=1)], axis=0).astype(bf16)
    l2 = jnp.arange(n2, dtype=i32)
    angt = ((l2[:, None] * k1[None, :]) % l).astype(f32) * (2.0 * math.pi / l)
    ct = jnp.broadcast_to(jnp.cos(angt)[:, :, None], (n2, n1, w))
    st = jnp.broadcast_to(jnp.sin(angt)[:, :, None], (n2, n1, w))
    ang2 = ((l2[:, None] * l2[None, :]) % n2).astype(f32) * (2.0 * math.pi / n2)
    m2 = jnp.concatenate([jnp.cos(ang2), -jnp.sin(ang2)], axis=1).astype(bf16)

    tspec = pl.BlockSpec((t2, n1, w), lambda t, i: (t, 0, 0))
    z = pl.pallas_call(
        _fourier1_kernel,
        grid=(n2 // t2, b),
        in_specs=[pl.BlockSpec((1, n1, t2, d), lambda t, i: (i, 0, t, 0)),
                  pl.BlockSpec((1, 6, d), lambda t, i: (i, 0, 0)),
                  pl.BlockSpec((1, d), lambda t, i: (0, 0)),
                  pl.BlockSpec(w_f.shape, lambda t, i: (0, 0)),
                  pl.BlockSpec(cs.shape, lambda t, i: (0, 0)),
                  pl.BlockSpec(m1.shape, lambda t, i: (0, 0)), tspec, tspec],
        out_specs=pl.BlockSpec((1, grp, t2, n1, w), lambda t, i: (i, 0, t, 0, 0)),
        out_shape=jax.ShapeDtypeStruct((b, grp, n2, n1, w), u32),
        scratch_shapes=[pltpu.VMEM((2, n1 * t2, w), f32)],
        compiler_params=_cparams(("parallel", "parallel")),
        name="fourier_outer",
    )(x.reshape(b, n1, n2, d), mod, g, w_f, cs, m1, ct, st)

    tk = 2 * SUBLANES
    y = pl.pallas_call(
        functools.partial(_fourier2_kernel, 1.0 / math.sqrt(l * w)),
        grid=(b, n1 // tk),
        in_specs=[pl.BlockSpec((1, grp, n2, tk, w), lambda i, t: (i, 0, 0, t, 0)),
                  pl.BlockSpec(m2.shape, lambda i, t: (0, 0))],
        out_specs=pl.BlockSpec((1, grp, n2, tk, w), lambda i, t: (i, 0, 0, t, 0)),
        out_shape=jax.ShapeDtypeStruct((b, grp, n2, n1, w), f32),
        scratch_shapes=[pltpu.VMEM((n2 * tk, w), u32), pltpu.VMEM((n2 * tk, w), f32)],
        compiler_params=_cparams(("parallel", "parallel")),
        name="fourier_inner",
    )(z, m2)
    return y.reshape(b, grp, l, w)


def _first_max4(a):
    m = jnp.maximum(jnp.maximum(a[0], a[1]), jnp.maximum(a[2], a[3]))
    idx = jnp.where(a[0] == m, 0, jnp.where(a[1] == m, 1, jnp.where(a[2] == m, 2, 3)))
    return m, idx


def _pick4(vals, idx):
    return jnp.where(idx == 0, vals[0], jnp.where(idx == 1, vals[1],
                                                   jnp.where(idx == 2, vals[2], vals[3])))


def _route(f, rw_ref, rb_ref, tri_ref, base_ref, first_step, ri_ref, wc_ref, cnt_ref, meta_ref):
    rw2 = rw_ref[...]
    pieces = []
    for fp in (f if isinstance(f, (list, tuple)) else [f]):
        f_hi = fp.astype(bf16)
        f_lo = (fp - f_hi.astype(f32)).astype(bf16)
        part = jnp.dot(f_hi, rw2, preferred_element_type=f32)
        pieces.append(part[:, :LANES] + part[:, LANES:]
                      + jnp.dot(f_lo, rw2[:, :LANES], preferred_element_type=f32))
    logits = jnp.concatenate(pieces, axis=0)
    tm = logits.shape[0]
    sc = jax.nn.sigmoid(logits)
    st = sc.T
    bt = (sc + rb_ref[...]).T
    neg = jnp.full((1, tm), -jnp.inf, f32)
    gs = []
    for g in range(N_GROUPS):
        a = [bt[4 * g + i: 4 * g + i + 1] for i in range(4)]
        m1, i1 = _first_max4(a)
        rest = [jnp.where(i1 == i, neg, a[i]) for i in range(4)]
        m2, _ = _first_max4(rest)
        gs.append(m1 + m2)
    _, gsel = _first_max4(gs)
    a = [_pick4([bt[4 * g + i: 4 * g + i + 1] for g in range(N_GROUPS)], gsel) for i in range(4)]
    s = [_pick4([st[4 * g + i: 4 * g + i + 1] for g in range(N_GROUPS)], gsel) for i in range(4)]
    _, i1 = _first_max4(a)
    rest = [jnp.where(i1 == i, neg, a[i]) for i in range(4)]
    _, i2 = _first_max4(rest)
    w1 = _pick4(s, i1)
    w2 = _pick4(s, i2)
    tot = w1 + w2
    w1 = w1 / tot
    w2 = w2 / tot
    e0 = gsel * EXPERTS_PER_GROUP + i1
    e1 = gsel * EXPERTS_PER_GROUP + i2

    @pl.when(first_step)
    def _():
        base_ref[...] = jnp.zeros_like(base_ref)

    td = DISPATCH_TILE
    eid = lax.broadcasted_iota(i32, (N_EXPERTS, tm), 0)
    oh0 = (eid == e0).astype(f32)
    oh1 = (eid == e1).astype(f32)
    oh = oh0 + oh1
    before = jnp.dot(oh.astype(bf16), tri_ref[...], preferred_element_type=f32)
    lane_tile = lax.broadcasted_iota(i32, (N_EXPERTS, tm), 1) // td
    ei = lax.broadcasted_iota(i32, (N_EXPERTS, N_EXPERTS), 0)
    ej = lax.broadcasted_iota(i32, (N_EXPERTS, N_EXPERTS), 1)
    strict_lower = (ej < ei).astype(f32)
    run_start = jnp.zeros((N_EXPERTS, tm), f32)
    goff = base_ref[...]
    for s in range(tm // td):
        cnt_s = jnp.sum(oh[:, s * td:(s + 1) * td], axis=1, keepdims=True)
        pad_s = jnp.floor((cnt_s + 7.0) * 0.125) * 8.0
        pad_b = jnp.broadcast_to(pad_s, (N_EXPERTS, LANES))
        start_b = jnp.dot(strict_lower, pad_b, precision=HIGHEST, preferred_element_type=f32)
        run_start = jnp.where(lane_tile == s, start_b[:, 0:1], run_start)
        meta_ref[s, 0] = start_b.astype(i32)
        meta_ref[s, 1] = pad_b.astype(i32)
        meta_ref[s, 2] = goff.astype(i32)
        goff = goff + pad_b
    base_ref[...] = goff
    cnt_ref[...] = goff
    pos = before + run_start
    lp0 = jnp.sum(oh0 * pos, axis=0, keepdims=True)
    lp1 = jnp.sum(oh1 * pos, axis=0, keepdims=True)
    zi = jnp.zeros((1, tm), i32)
    ri_ref[...] = jnp.concatenate(
        [lp0.astype(i32), lp1.astype(i32), e0, e1,
         lax.bitcast_convert_type(w1, i32), lax.bitcast_convert_type(w2, i32), zi, zi], axis=0)
    zf = jnp.zeros((LANES - 4, tm), f32)
    wc_ref[...] = jnp.concatenate([w1, w2, lp0, lp1, zf], axis=0).T


def _outproj_kernel(yf_ref, o_ref, x_ref, mod_ref, w_ref, g_ref, rw_ref, rb_ref, tri_ref,
                    x1_ref, f_ref, ri_ref, wc_ref, cnt_ref, meta_ref, base_ref):
    tm = x_ref.shape[1]
    fs = []
    for r in range(0, tm, tm // 2):
        rs = slice(r, r + tm // 2)
        mix = jnp.concatenate([yf_ref[0, g, rs, :].astype(bf16) for g in range(FOURIER_GROUPS)]
                              + [o_ref[0, rs, :]], axis=1)
        y = jnp.dot(mix, w_ref[...], preferred_element_type=f32)
        x1 = x_ref[0, rs, :] + mod_ref[0, 2:3, :] * y
        x1_ref[0, rs, :] = x1
        f = _norm_mod(x1, g_ref[...], mod_ref[0, 3:4, :], mod_ref[0, 4:5, :])
        f_ref[0, rs, :] = f.astype(bf16)
        fs.append(f)
    first = (pl.program_id(0) == 0) & (pl.program_id(1) == 0)
    _route(fs, rw_ref, rb_ref, tri_ref, base_ref, first, ri_ref, wc_ref, cnt_ref, meta_ref)


def _before_in_tile(tm):
    tpos = jnp.arange(tm)
    return ((tpos[:, None] < tpos[None, :])
            & (tpos[:, None] // DISPATCH_TILE == tpos[None, :] // DISPATCH_TILE)).astype(bf16)


def _route_specs(b, l, tm):
    nl = l // tm
    rw = lambda d: pl.BlockSpec((d, 2 * LANES), lambda i, j: (0, 0))
    rb = pl.BlockSpec((1, LANES), lambda i, j: (0, 0))
    tri = pl.BlockSpec((tm, tm), lambda i, j: (0, 0))
    ns = tm // DISPATCH_TILE
    out_specs = [pl.BlockSpec((8, tm), lambda i, j: (0, i * nl + j)),
                 pl.BlockSpec((tm, LANES), lambda i, j: (i * nl + j, 0)),
                 pl.BlockSpec((N_EXPERTS, LANES), lambda i, j: (0, 0)),
                 pl.BlockSpec((ns, 3, N_EXPERTS, LANES), lambda i, j: (i * nl + j, 0, 0, 0))]
    out_shape = [jax.ShapeDtypeStruct((8, b * l), i32),
                 jax.ShapeDtypeStruct((b * l, LANES), f32),
                 jax.ShapeDtypeStruct((N_EXPERTS, LANES), f32),
                 jax.ShapeDtypeStruct((b * l // DISPATCH_TILE, 3, N_EXPERTS, LANES), i32)]
    return rw, rb, tri, out_specs, out_shape


def _outproj(yf, o, x, mod, w, g, rw, rb):
    b, l, d = x.shape
    tm = PROJ_TILE
    tri = _before_in_tile(tm)
    rws, rbs, tris, r_specs, r_shapes = _route_specs(b, l, tm)
    row = pl.BlockSpec((1, tm, d), lambda i, j: (i, j, 0))
    return pl.pallas_call(
        _outproj_kernel,
        grid=(b, l // tm),
        in_specs=[pl.BlockSpec((1, FOURIER_GROUPS, tm, LANES), lambda i, j: (i, 0, j, 0)),
                  pl.BlockSpec((1, tm, o.shape[2]), lambda i, j: (i, j, 0)),
                  row,
                  pl.BlockSpec((1, 6, d), lambda i, j: (i, 0, 0)),
                  pl.BlockSpec(w.shape, lambda i, j: (0, 0)),
                  pl.BlockSpec((1, d), lambda i, j: (0, 0)),
                  rws(d), rbs, tris],
        out_specs=[row, row] + r_specs,
        out_shape=[jax.ShapeDtypeStruct((b, l, d), f32),
                   jax.ShapeDtypeStruct((b, l, d), bf16)] + r_shapes,
        scratch_shapes=[pltpu.VMEM((N_EXPERTS, LANES), f32)],
        compiler_params=_cparams(("arbitrary", "arbitrary")),
        name="outproj_router",
    )(yf, o, x, mod, w, g, rw, rb, tri)


def _conv_kernel(seq_len, up_ref, um_ref, un_ref, x_ref, mod_ref, dw_ref, db_ref, lg_ref, lb_ref,
                 w_ref, pb_ref, g_ref, rw_ref, rb_ref, tri_ref,
                 x1_ref, f_ref, ri_ref, wc_ref, cnt_ref, meta_ref, base_ref, ext, conv_out):
    j = pl.program_id(1)
    tm = um_ref.shape[1]
    hl = CONV_HALO
    half = CONV_W // 2
    prev = jnp.where(j > 0, up_ref[0], jnp.zeros_like(up_ref[0]))
    nxt = jnp.where((j + 1) * tm < seq_len, un_ref[0], jnp.zeros_like(un_ref[0]))
    for c in range(ext.shape[0]):
        lanes_c = slice(c * LANES, (c + 1) * LANES)
        ext[c, 0:hl] = prev[:, lanes_c]
        ext[c, hl:hl + tm] = um_ref[0, :, lanes_c]
        ext[c, hl + tm:] = nxt[:, lanes_c]
    base = hl - half
    span = (CONV_W - 1) // SUBLANES * SUBLANES
    rows = CONV_ROWS

    def lane_chunk(c, carry):
        lanes = pl.ds(pl.multiple_of(c * LANES, LANES), LANES)
        for r in range(0, tm, rows):
            part = jnp.broadcast_to(db_ref[:, lanes], (rows, LANES))
            for phase in range(SUBLANES):
                win = ext[c, base + phase + r: base + phase + r + rows + span, :]
                same = None
                for t in range(phase, CONV_W, SUBLANES):
                    term = win[t - phase: t - phase + rows, :] * dw_ref[t:t + 1, lanes]
                    same = term if same is None else same + term
                part = part + same
            conv_out[r:r + rows, lanes] = part
        return carry

    lax.fori_loop(0, um_ref.shape[2] // LANES, lane_chunk, 0)
    fs = []
    for r in range(0, tm, tm // 2):
        rs = slice(r, r + tm // 2)
        acc = conv_out[rs, :]
        mu = jnp.mean(acc, axis=-1, keepdims=True)
        cen = acc - mu
        var = jnp.mean(cen * cen, axis=-1, keepdims=True)
        ln = cen * lax.rsqrt(var + EPS) * lg_ref[...] + lb_ref[...]
        act = ln * jax.nn.sigmoid(ln)
        y = jnp.dot(act.astype(bf16), w_ref[...], preferred_element_type=f32) + pb_ref[...]
        x1 = x_ref[0, rs, :] + mod_ref[0, 2:3, :] * y
        x1_ref[0, rs, :] = x1
        f = _norm_mod(x1, g_ref[...], mod_ref[0, 3:4, :], mod_ref[0, 4:5, :])
        f_ref[0, rs, :] = f.astype(bf16)
        fs.append(f)
    first = (pl.program_id(0) == 0) & (j == 0)
    _route(fs, rw_ref, rb_ref, tri_ref, base_ref, first, ri_ref, wc_ref, cnt_ref, meta_ref)


def _conv(u, x, mod, dw_w, dw_b, ln_g, ln_b, pw2_w, pw2_b, g, rw, rb):
    b, l, d = x.shape
    tm = TOKEN_TILE
    tri = _before_in_tile(tm)
    hl = CONV_HALO
    r = tm // hl
    nh = l // hl
    rws, rbs, tris, r_specs, r_shapes = _route_specs(b, l, tm)
    row = pl.BlockSpec((1, tm, d), lambda i, j: (i, j, 0))
    vec = pl.BlockSpec((1, d), lambda i, j: (0, 0))
    return pl.pallas_call(
        functools.partial(_conv_kernel, l),
        grid=(b, l // tm),
        in_specs=[pl.BlockSpec((1, hl, d), lambda i, j: (i, jnp.maximum(j * r - 1, 0), 0)),
                  row,
                  pl.BlockSpec((1, hl, d), lambda i, j: (i, jnp.minimum(j * r + r, nh - 1), 0)),
                  row,
                  pl.BlockSpec((1, 6, d), lambda i, j: (i, 0, 0)),
                  pl.BlockSpec(dw_w.shape, lambda i, j: (0, 0)),
                  vec, vec, vec,
                  pl.BlockSpec(pw2_w.shape, lambda i, j: (0, 0)),
                  vec, vec, rws(d), rbs, tris],
        out_specs=[row, row] + r_specs,
        out_shape=[jax.ShapeDtypeStruct((b, l, d), f32),
                   jax.ShapeDtypeStruct((b, l, d), bf16)] + r_shapes,
        scratch_shapes=[pltpu.VMEM((N_EXPERTS, LANES), f32),
                        pltpu.VMEM((d // LANES, tm + 2 * hl, LANES), f32),
                        pltpu.VMEM((tm, d), f32)],
        compiler_params=_cparams(("arbitrary", "arbitrary")),
        name="conv_router",
    )(u, u, u, x, mod, dw_w, dw_b, ln_g, ln_b, pw2_w, pw2_b, g, rw, rb, tri)


def _pack_bf16_pairs(x):
    h = x.shape[1] // 2
    lo = lax.bitcast_convert_type(x[:, :h], u32)
    hi = lax.bitcast_convert_type(x[:, h:], u32)
    return (lo >> 16) | (hi & jnp.uint32(0xFFFF0000))


def _unpack_bf16_pairs(u):
    lo = lax.bitcast_convert_type(u << 16, f32)
    hi = lax.bitcast_convert_type(u & jnp.uint32(0xFFFF0000), f32)
    return jnp.concatenate([lo, hi], axis=1).astype(bf16)


def _run_copies(meta, tile, local_ref, hbm_ref, sem, to_hbm):
    start_ref, size_ref, dst_ref = meta
    for e in range(N_EXPERTS):
        k = tile * N_EXPERTS + e
        size = pl.multiple_of(size_ref[k], RUN_ALIGN)

        @pl.when(size > 0)
        def _():
            loc = local_ref.at[pl.ds(pl.multiple_of(start_ref[k], RUN_ALIGN), size)]
            glob = hbm_ref.at[pl.ds(pl.multiple_of(dst_ref[k], RUN_ALIGN), size)]
            if to_hbm:
                pltpu.make_async_copy(loc, glob, sem).start()
            else:
                pltpu.make_async_copy(glob, loc, sem).start()


def _wait_rows(rows, local_ref, hbm_ref, sem):
    rows = pl.multiple_of(rows, RUN_ALIGN)

    @pl.when(rows > 0)
    def _():
        pltpu.make_async_copy(local_ref.at[pl.ds(0, rows)], hbm_ref.at[pl.ds(0, rows)], sem).wait()


def _dispatch_kernel(start_ref, size_ref, dst_ref, tot_ref, tail_start_ref, tail_size_ref, nv_ref,
                     f_ref, lp_ref, xs_ref, loc, zbuf, sem, zsem):
    i = pl.program_id(0)
    n = pl.num_programs(0)
    slot = i % 2
    subs = loc.shape[1]
    rows = loc.shape[2]
    td = f_ref.shape[0] // subs
    meta = (start_ref, size_ref, dst_ref)

    def drain(step, which):
        for s in range(subs):
            _wait_rows(tot_ref[step * subs + s], loc.at[which, s], xs_ref, sem.at[which])

    @pl.when(i >= 2)
    def _():
        drain(i - 2, slot)

    r = lax.broadcasted_iota(i32, (rows, td), 0)
    for s in range(subs):
        cols = slice(s * td, (s + 1) * td)
        pick0 = r == lp_ref[0:1, cols]
        pick1 = r == lp_ref[1:2, cols]
        onehot = (pick0 | pick1).astype(bf16)
        sorted_rows = jnp.dot(onehot, f_ref[cols, :], preferred_element_type=f32)
        half = sorted_rows.shape[1] // 2
        loc[slot, s, :, :half] = _pack_bf16_pairs(sorted_rows)
        w0 = lax.bitcast_convert_type(lp_ref[4:5, cols], f32)
        w1 = lax.bitcast_convert_type(lp_ref[5:6, cols], f32)
        row_w = jnp.sum(jnp.where(pick0, w0, 0.0) + jnp.where(pick1, w1, 0.0), axis=1, keepdims=True)
        loc[slot, s, :, half:] = jnp.broadcast_to(lax.bitcast_convert_type(row_w, u32), (rows, LANES))
        _run_copies(meta, i * subs + s, loc.at[slot, s], xs_ref, sem.at[slot], to_hbm=True)

    @pl.when(i == n - 1)
    def _():
        zbuf[...] = jnp.zeros_like(zbuf)
        total = 0
        for e in range(N_EXPERTS):
            size = pl.multiple_of(tail_size_ref[e], RUN_ALIGN)
            total = total + size

            @pl.when(size > 0)
            def _():
                pltpu.make_async_copy(
                    zbuf.at[pl.ds(0, size)],
                    xs_ref.at[pl.ds(pl.multiple_of(tail_start_ref[e], RUN_ALIGN), size)], zsem).start()

        _wait_rows(total, zbuf, xs_ref, zsem)

        def zero_block(k, c):
            pltpu.make_async_copy(zbuf, xs_ref.at[pl.ds(pl.multiple_of(k * zbuf.shape[0], RUN_ALIGN),
                                                        zbuf.shape[0])], zsem).start()
            return c

        def wait_block(k, c):
            pltpu.make_async_copy(zbuf, xs_ref.at[pl.ds(0, zbuf.shape[0])], zsem).wait()
            return c

        n_blocks = xs_ref.shape[0] // zbuf.shape[0]
        lax.fori_loop(nv_ref[0], n_blocks, zero_block, 0)
        lax.fori_loop(nv_ref[0], n_blocks, wait_block, 0)
        drain(i, slot)

        @pl.when(i >= 1)
        def _():
            drain(i - 1, 1 - slot)


def _dispatch(tables, f2, ri, n_slots):
    t, d = f2.shape
    subs = DISPATCH_SUBTILES
    tm = DISPATCH_TILE * subs
    return pl.pallas_call(
        _dispatch_kernel,
        grid_spec=pltpu.PrefetchScalarGridSpec(
            num_scalar_prefetch=7,
            grid=(t // tm,),
            in_specs=[pl.BlockSpec((tm, d), lambda i, *_: (i, 0)),
                      pl.BlockSpec((8, tm), lambda i, *_: (0, i))],
            out_specs=pl.BlockSpec(memory_space=pl.ANY),
            scratch_shapes=[pltpu.VMEM((2, subs, LOCAL_ROWS, d // 2 + LANES), u32),
                            pltpu.VMEM((EXPERT_ROWS, d // 2 + LANES), u32),
                            pltpu.SemaphoreType.DMA((2,)), pltpu.SemaphoreType.DMA(())]),
        out_shape=jax.ShapeDtypeStruct((n_slots, d // 2 + LANES), u32),
        compiler_params=_cparams(("arbitrary",)),
        name="moe_dispatch",
    )(*tables, f2, ri)


def _expert_kernel(be_ref, second_ref, nv_ref, x_ref, wg1, wu1, wd1, wg2, wu2, wd2, y_ref, *w_bf):
    del second_ref
    i = pl.program_id(0)
    tb = y_ref.shape[0] // 2
    half = y_ref.shape[1]
    e1, e2 = be_ref[2 * i], be_ref[2 * i + 1]
    first, second = w_bf[:3], w_bf[3:]

    @pl.when(jnp.logical_or(i == 0, e1 != be_ref[jnp.maximum(2 * i - 2, 0)]))
    def _():
        for dst, src in zip(first, (wg1, wu1, wd1)):
            dst[...] = src[0, 0].astype(bf16)

    @pl.when(e2 != e1)
    def _():
        for dst, src in zip(second, (wg2, wu2, wd2)):
            dst[...] = src[0, 0].astype(bf16)

    def block(k, weights):
        wgb, wub, wdb = weights
        rows = slice(k * tb, (k + 1) * tb)
        xb = _unpack_bf16_pairs(x_ref[rows, :half])
        row_w = lax.bitcast_convert_type(x_ref[rows, half:], f32)
        gate = jnp.dot(xb, wgb[...], preferred_element_type=f32)
        up = jnp.dot(xb, wub[...], preferred_element_type=f32)
        hid = (gate * jax.nn.sigmoid(gate) * up).astype(bf16)
        y = jnp.dot(hid, wdb[...], preferred_element_type=f32)
        y = jnp.concatenate([y[:, c:c + LANES] * row_w for c in range(0, y.shape[1], LANES)], axis=1)
        y_ref[rows, :] = _pack_bf16_pairs(y.astype(bf16).astype(f32))

    used = 2 * i < nv_ref[0]

    @pl.when(used & (e2 == e1))
    def _():
        block(0, first)
        block(1, first)

    @pl.when(used & (e2 != e1))
    def _():
        block(0, first)
        block(1, second)

    @pl.when(jnp.logical_not(used))
    def _():
        y_ref[...] = jnp.zeros_like(y_ref)


def _experts(block_e, n_valid, xs, w_gate, w_up, w_down, layer):
    ns, xw = xs.shape
    nb = EXPERT_BLOCKS_PER_STEP
    tb = EXPERT_ROWS
    d, ff = w_gate.shape[2:]
    dh = d // 2
    e1, e2 = block_e[0::2], block_e[1::2]
    second = lax.cummax(jnp.where(e2 != e1, e2, 0))
    map1 = lambda i, be, sec, nv: (layer, be[2 * i], 0, 0)
    map2 = lambda i, be, sec, nv: (layer, sec[i], 0, 0)
    w_specs = [pl.BlockSpec((1, 1, d, ff), m) for m in (map1, map1)] + [pl.BlockSpec((1, 1, ff, d), map1)]
    w_specs += [pl.BlockSpec((1, 1, d, ff), m) for m in (map2, map2)] + [pl.BlockSpec((1, 1, ff, d), map2)]
    w_scratch = 2 * [pltpu.VMEM((d, ff), bf16), pltpu.VMEM((d, ff), bf16), pltpu.VMEM((ff, d), bf16)]
    return pl.pallas_call(
        _expert_kernel,
        grid_spec=pltpu.PrefetchScalarGridSpec(
            num_scalar_prefetch=3,
            grid=(ns // (nb * tb),),
            in_specs=[pl.BlockSpec((nb * tb, xw), lambda i, *_: (i, 0))] + w_specs,
            out_specs=pl.BlockSpec((nb * tb, dh), lambda i, *_: (i, 0)),
            scratch_shapes=w_scratch),
        out_shape=jax.ShapeDtypeStruct((ns, dh), u32),
        compiler_params=_cparams(("arbitrary",)),
        name="moe_experts",
    )(block_e, second, n_valid, xs, w_gate, w_up, w_down, w_gate, w_up, w_down)


def _combine_kernel(final, start_ref, size_ref, dst_ref, tot_ref, ys_ref, wc_ref, x_ref, mod_ref,
                    g_ref, *rest):
    if final:
        o_ref, loc, sem = rest
    else:
        nmod_ref, w_ref, b_ref, o_ref, u_ref, loc, sem = rest
    i = pl.program_id(0)
    n = pl.num_programs(0)
    slot = i % 2
    subs = loc.shape[1]
    rows = loc.shape[2]
    td = x_ref.shape[0] // subs
    meta = (start_ref, size_ref, dst_ref)

    def fetch(step, which):
        for s in range(subs):
            _run_copies(meta, step * subs + s, loc.at[which, s], ys_ref, sem.at[which, s],
                        to_hbm=False)

    @pl.when(i == 0)
    def _():
        loc[...] = jnp.zeros_like(loc)
        fetch(i, slot)

    @pl.when(i + 1 < n)
    def _():
        fetch(i + 1, 1 - slot)

    c = lax.broadcasted_iota(i32, (td, rows), 1)
    parts = []
    for s in range(subs):
        _wait_rows(tot_ref[i * subs + s], loc.at[slot, s], ys_ref, sem.at[slot, s])
        wc = wc_ref[s * td:(s + 1) * td, :]
        sel = ((c == wc[:, 2:3].astype(i32)) | (c == wc[:, 3:4].astype(i32))).astype(bf16)
        parts.append(jnp.dot(sel, _unpack_bf16_pairs(loc[slot, s]), preferred_element_type=f32))
    xo = x_ref[...] + mod_ref[0, 5:6, :] * jnp.concatenate(parts, axis=0)
    if final:
        ms = jnp.mean(xo * xo, axis=-1, keepdims=True)
        o_ref[...] = xo * lax.rsqrt(ms + EPS) * g_ref[...]
    else:
        o_ref[...] = xo
        h = _norm_mod(xo, g_ref[...], nmod_ref[0, 0:1, :], nmod_ref[0, 1:2, :])
        p = jnp.dot(h.astype(bf16), w_ref[...], preferred_element_type=f32) + b_ref[...]
        ch = p.shape[1] // 2
        u_ref[...] = p[:, :ch] * jax.nn.sigmoid(p[:, ch:])


def _combine(tables, ys, wc, x, mod, g, glu=None):
    b, l, d = x.shape
    subs = DISPATCH_SUBTILES if glu is None else GLU_SUBTILES
    tm = DISPATCH_TILE * subs
    per_batch = l // tm
    const = lambda i, *_: (0, 0)
    tile = lambda cols: pl.BlockSpec((tm, cols), lambda i, *_: (i, 0))
    mod_spec = pl.BlockSpec((1, 6, d), lambda i, *_: (i // per_batch, 0, 0))
    in_specs = [pl.BlockSpec(memory_space=pl.ANY), tile(LANES), tile(d), mod_spec,
                pl.BlockSpec((1, d), const)]
    args = [ys, wc, x.reshape(b * l, d), mod, g]
    out_specs = [tile(d)]
    out_shape = [jax.ShapeDtypeStruct((b * l, d), f32)]
    if glu is not None:
        nmod, w, bias = glu
        in_specs += [mod_spec, pl.BlockSpec(w.shape, const), pl.BlockSpec(bias.shape, const)]
        args += [nmod, w, bias]
        out_specs.append(tile(w.shape[1] // 2))
        out_shape.append(jax.ShapeDtypeStruct((b * l, w.shape[1] // 2), f32))
    outs = pl.pallas_call(
        functools.partial(_combine_kernel, glu is None),
        grid_spec=pltpu.PrefetchScalarGridSpec(
            num_scalar_prefetch=4,
            grid=(b * per_batch,),
            in_specs=in_specs,
            out_specs=out_specs,
            scratch_shapes=[pltpu.VMEM((2, subs, LOCAL_ROWS, d // 2), u32),
                            pltpu.SemaphoreType.DMA((2, subs))]),
        out_shape=out_shape,
        compiler_params=_cparams(("arbitrary",)),
        name="moe_combine",
    )(*tables, *args)
    return [o.reshape(b, l, -1) for o in outs]


def _moe(f, routed, x, mod, g, w_gate, w_up, w_down, layer, glu=None):
    ri, wc, cnt, meta = routed
    b, l, d = x.shape
    t = b * l
    tb = EXPERT_ROWS
    n_tiles = t // DISPATCH_TILE
    used = cnt[:, 0].astype(i32)
    region = (used + tb - 1) // tb * tb
    gend = jnp.cumsum(region)
    gstart = gend - region
    max_rows = 2 * t + n_tiles * N_EXPERTS * (RUN_ALIGN - 1) + N_EXPERTS * (tb - 1)
    n_blocks = -(-max_rows // (tb * EXPERT_BLOCKS_PER_STEP)) * EXPERT_BLOCKS_PER_STEP
    m = meta[:, :, :, 0]
    run_start = m[:, 0].reshape(-1)
    run_size = m[:, 1].reshape(-1)
    run_dst = (m[:, 2] + gstart[None, :]).reshape(-1)
    tile_rows = jnp.sum(m[:, 1], axis=1)
    block_row = jnp.arange(n_blocks, dtype=i32) * tb
    block_e = jnp.minimum(jnp.sum((block_row[:, None] >= gend[None, :]).astype(i32), axis=1),
                          N_EXPERTS - 1)
    n_valid = (gend[-1] // tb).reshape(1)
    xs = _dispatch((run_start, run_size, run_dst, tile_rows, gstart + used, region - used, n_valid),
                   f.reshape(t, d), ri, n_blocks * tb)
    ys = _experts(block_e, n_valid, xs, w_gate, w_up, w_down, layer)
    return _combine((run_start, run_size, run_dst, tile_rows), ys, wc, x, mod, g, glu)


def _rope_tables(l):
    lane = jnp.arange(LANES)
    dh = lane % HEAD_DIM
    inv = ROPE_THETA ** (-(dh % 16).astype(f32) / 16.0)
    sign = jnp.where((dh % 32) < 16, -1.0, 1.0).astype(f32)
    by_row = (dh // 32)[None, None, :] == 0
    ang_r = jnp.arange(l // GRID_W, dtype=f32)[:, None] * inv[None, :]
    ang_c = jnp.arange(GRID_W, dtype=f32)[:, None] * inv[None, :]
    cos = jnp.where(by_row, jnp.cos(ang_r)[:, None, :], jnp.cos(ang_c)[None, :, :])
    sin = jnp.where(by_row, jnp.sin(ang_r)[:, None, :], jnp.sin(ang_c)[None, :, :])
    return cos.reshape(l, LANES), (sin * sign[None, None, :]).reshape(l, LANES)


def kernel(x, c, ctx, c_ctx, ada_w, ada_b, norm_mix_g, norm_ffn_g, even_w_in, even_w_out, even_sink, conv_pw1_w, conv_pw1_b, conv_dw_w, conv_dw_b, conv_ln_g, conv_ln_b, conv_pw2_w, conv_pw2_b, router_w, router_b, moe_w_gate, moe_w_up, moe_w_down, final_norm_g):
    b, l, d = x.shape
    depth = ada_w.shape[0]
    assert depth == 2 and b < COND_ROWS
    ctx_row = b
    cond = jnp.zeros((COND_ROWS, d), f32).at[:b].set(c).at[ctx_row].set(c_ctx)
    mods = _adaln(cond, ada_w, ada_b).reshape(depth, COND_ROWS, 6, d)

    heads = jnp.arange(N_HEADS).reshape(N_KV_HEADS, N_HEADS // N_KV_HEADS).T.reshape(-1)
    qperm = (heads[:, None] * HEAD_DIM + jnp.arange(HEAD_DIM)[None, :]).reshape(-1)
    fw = FOURIER_GROUPS * FOURIER_GROUP_W
    qw = N_HEADS * HEAD_DIM
    w_in = even_w_in[0]
    w_in_p = jnp.concatenate([w_in[:, :fw], w_in[:, fw:fw + qw][:, qperm], w_in[:, fw + qw:]],
                             axis=1).astype(bf16)
    w_out = even_w_out[0]
    w_out_p = jnp.concatenate([w_out[:fw], w_out[fw:][qperm]], axis=0).astype(bf16)
    sink_pairs = (even_sink[0].astype(f32) * LOG2E).reshape(N_KV_HEADS, N_HEADS // N_KV_HEADS).T
    sinkcol = jnp.repeat(jnp.repeat(sink_pairs, HEAD_DIM, axis=1), ATT_BLOCK, axis=0)

    cidx = jnp.arange(FOURIER_GROUP_W, dtype=i32)
    angc = ((cidx[:, None] * cidx[None, :]) % FOURIER_GROUP_W).astype(f32) * (2.0 * math.pi / FOURIER_GROUP_W)
    cs = jnp.concatenate([jnp.cos(angc), jnp.sin(angc)], axis=1).astype(bf16)
    cos_t, sin_t = _rope_tables(l)

    rw32 = jnp.zeros((d, LANES), f32).at[:, :N_EXPERTS].set(router_w.astype(f32))
    rw_hi = rw32.astype(bf16)
    rw = jnp.concatenate([rw_hi, (rw32 - rw_hi.astype(f32)).astype(bf16)], axis=1)
    rb = jnp.zeros((1, LANES), f32).at[0, :N_EXPERTS].set(router_b)
    row = lambda v: v.reshape(1, -1)

    q, k, v = _inproj(x, mods[0], row(norm_mix_g[0]), w_in_p[:, fw:], cos_t, sin_t)
    ck, cv = _ctxkv(ctx, mods[0], row(norm_mix_g[0]), w_in_p[:, fw + qw:], ctx_row)
    yf = _fourier(x, mods[0], row(norm_mix_g[0]), w_in_p[:, :fw], cs)
    att = _attention(q, k, v, ck, cv, sinkcol)
    x1, f, *routed = _outproj(yf, att, x, mods[0], w_out_p, row(norm_ffn_g[0]), rw, rb)
    x2, u = _moe(f, routed, x1, mods[0], row(norm_mix_g[1]), moe_w_gate, moe_w_up, moe_w_down,
                 layer=0, glu=(mods[1], conv_pw1_w[0].astype(bf16), row(conv_pw1_b[0])))

    x3, f, *routed = _conv(u, x2, mods[1], conv_dw_w[0], row(conv_dw_b[0]), row(conv_ln_g[0]),
                           row(conv_ln_b[0]), conv_pw2_w[0].astype(bf16), row(conv_pw2_b[0]),
                           row(norm_ffn_g[1]), rw, rb)
    (out,) = _moe(f, routed, x3, mods[1], row(final_norm_g), moe_w_gate, moe_w_up, moe_w_down,
                  layer=1)
    return out
```

```python
import functools
import math

import jax
import jax.numpy as jnp
from jax import lax
from jax.experimental import pallas as pl
from jax.experimental.pallas import tpu as pltpu

f32 = jnp.float32
bf16 = jnp.bfloat16
i32 = jnp.int32
u32 = jnp.uint32
HIGHEST = lax.Precision.HIGHEST

GRID_W = 64
HEAD_DIM = 64
N_HEADS = 8
N_KV_HEADS = 2
WINDOW = 128
ATT_BLOCK = 128
ROPE_THETA = 10000.0
FOURIER_GROUPS = 4
FOURIER_GROUP_W = 128
CONV_W = 31
N_EXPERTS = 16
N_GROUPS = 4
EXPERTS_PER_GROUP = 4
EPS = 1e-6
NEG_INF = -1e30
LOG2E = math.log2(math.e)

LANES = 128
SUBLANES = 8
COND_ROWS = 8
DFT_INNER = 64
TOKEN_TILE = 512
PROJ_TILE = 1024
ATT_TILE = 1024
EXPERT_ROWS = 512
EXPERT_BLOCKS_PER_STEP = 2
DISPATCH_TILE = 256
DISPATCH_SUBTILES = 4
GLU_SUBTILES = 2
RUN_ALIGN = 8
LOCAL_ROWS = -(-(2 * DISPATCH_TILE + N_EXPERTS * (RUN_ALIGN - 1)) // LANES) * LANES
CONV_HALO = 16
CONV_ROWS = 128
VMEM_LIMIT = 56 * 1024 * 1024


def _cparams(sem, vmem=VMEM_LIMIT):
    return pltpu.CompilerParams(dimension_semantics=sem, vmem_limit_bytes=vmem)


def _adaln_kernel(cond_ref, w_ref, b_ref, o_ref):
    s = cond_ref[...]
    s = s * jax.nn.sigmoid(s)
    w = w_ref[0]
    s_hi, w_hi = s.astype(bf16), w.astype(bf16)
    s_lo = (s - s_hi.astype(f32)).astype(bf16)
    w_lo = (w - w_hi.astype(f32)).astype(bf16)
    dot = functools.partial(jnp.dot, preferred_element_type=f32)
    o_ref[0] = dot(s_hi, w_hi) + dot(s_hi, w_lo) + dot(s_lo, w_hi) + b_ref[0]


def _adaln(cond, ada_w, ada_b):
    depth, d, n = ada_w.shape
    tn = 3072
    return pl.pallas_call(
        _adaln_kernel,
        grid=(depth, n // tn),
        in_specs=[pl.BlockSpec((COND_ROWS, d), lambda i, j: (0, 0)),
                  pl.BlockSpec((1, d, tn), lambda i, j: (i, 0, j)),
                  pl.BlockSpec((1, 1, tn), lambda i, j: (i, 0, j))],
        out_specs=pl.BlockSpec((1, COND_ROWS, tn), lambda i, j: (i, 0, j)),
        out_shape=jax.ShapeDtypeStruct((depth, COND_ROWS, n), f32),
        compiler_params=_cparams(("arbitrary", "arbitrary")),
        name="adaln",
    )(cond, ada_w, ada_b.reshape(depth, 1, n))


def _norm_mod(x, g, shift, scale):
    ms = jnp.mean(x * x, axis=-1, keepdims=True)
    return (x * lax.rsqrt(ms + EPS)) * (g * (1.0 + scale)) + shift


def _rope(p, cos, sin_signed, first_half):
    rot = jnp.where(first_half, pltpu.roll(p, LANES - 16, axis=1), pltpu.roll(p, 16, axis=1))
    return p * cos + rot * sin_signed


def _inproj_kernel(x_ref, mod_ref, g_ref, w_ref, cos_ref, sin_ref, q_ref, k_ref, v_ref):
    qw = N_HEADS * HEAD_DIM
    tm = x_ref.shape[1]
    for r in range(0, tm, tm // 2):
        rs = slice(r, r + tm // 2)
        h = _norm_mod(x_ref[0, rs, :], g_ref[...], mod_ref[0, 0:1, :], mod_ref[0, 1:2, :])
        p = jnp.dot(h.astype(bf16), w_ref[...], preferred_element_type=f32)
        cos = cos_ref[rs, :]
        sin = sin_ref[rs, :]
        lane = lax.broadcasted_iota(i32, cos.shape, 1)
        first_half = (lane % 32) < 16
        for c in range(qw // LANES):
            qc = p[:, c * LANES:(c + 1) * LANES]
            q_ref[0, rs, c * LANES:(c + 1) * LANES] = (
                _rope(qc, cos, sin, first_half) * (LOG2E * HEAD_DIM ** -0.5)).astype(bf16)
        k_ref[0, rs, :] = _rope(p[:, qw:qw + LANES], cos, sin, first_half).astype(bf16)
        v_ref[0, rs, :] = p[:, qw + LANES:].astype(bf16)


def _inproj(x, mod, g, w, cos_t, sin_t):
    b, l, d = x.shape
    tm = PROJ_TILE
    n = w.shape[1]
    return pl.pallas_call(
        _inproj_kernel,
        grid=(b, l // tm),
        in_specs=[pl.BlockSpec((1, tm, d), lambda i, j: (i, j, 0)),
                  pl.BlockSpec((1, 6, d), lambda i, j: (i, 0, 0)),
                  pl.BlockSpec((1, d), lambda i, j: (0, 0)),
                  pl.BlockSpec((d, n), lambda i, j: (0, 0)),
                  pl.BlockSpec((tm, LANES), lambda i, j: (j, 0)),
                  pl.BlockSpec((tm, LANES), lambda i, j: (j, 0))],
        out_specs=[pl.BlockSpec((1, tm, N_HEADS * HEAD_DIM), lambda i, j: (i, j, 0)),
                   pl.BlockSpec((1, tm, LANES), lambda i, j: (i, j, 0)),
                   pl.BlockSpec((1, tm, LANES), lambda i, j: (i, j, 0))],
        out_shape=[jax.ShapeDtypeStruct((b, l, N_HEADS * HEAD_DIM), bf16),
                   jax.ShapeDtypeStruct((b, l, LANES), bf16),
                   jax.ShapeDtypeStruct((b, l, LANES), bf16)],
        compiler_params=_cparams(("parallel", "parallel")),
        name="inproj",
    )(x, mod, g, w, cos_t, sin_t)


def _ctxkv_kernel(x_ref, mod_ref, g_ref, w_ref, k_ref, v_ref):
    h = _norm_mod(x_ref[0], g_ref[...], mod_ref[0, 0:1, :], mod_ref[0, 1:2, :])
    p = jnp.dot(h.astype(bf16), w_ref[...], preferred_element_type=f32)
    k_ref[0] = p[:, :LANES].astype(bf16)
    v_ref[0] = p[:, LANES:].astype(bf16)


def _ctxkv(ctx, mod, g, w_kv, ctx_row):
    b, c, d = ctx.shape
    return pl.pallas_call(
        _ctxkv_kernel,
        grid=(b,),
        in_specs=[pl.BlockSpec((1, c, d), lambda i: (i, 0, 0)),
                  pl.BlockSpec((1, 6, d), lambda i: (ctx_row, 0, 0)),
                  pl.BlockSpec((1, d), lambda i: (0, 0)),
                  pl.BlockSpec((d, 2 * LANES), lambda i: (0, 0))],
        out_specs=[pl.BlockSpec((1, c, LANES), lambda i: (i, 0, 0)),
                   pl.BlockSpec((1, c, LANES), lambda i: (i, 0, 0))],
        out_shape=[jax.ShapeDtypeStruct((b, c, LANES), bf16),
                   jax.ShapeDtypeStruct((b, c, LANES), bf16)],
        compiler_params=_cparams(("parallel",)),
        name="ctxkv",
    )(ctx, mod, g, w_kv)


def _attn_kernel(seq_len, q_ref, kp_ref, km_ref, kn_ref, vp_ref, vm_ref, vn_ref,
                 ck_ref, cv_ref, sink_ref, o_ref, kext, vext):
    j = pl.program_id(1)
    tq = ATT_TILE
    blk = ATT_BLOCK
    kext[0:blk] = kp_ref[0]
    kext[blk:blk + tq] = km_ref[0]
    kext[blk + tq:] = kn_ref[0]
    vext[:, LANES:] = jnp.ones((tq + 2 * blk, LANES), bf16)
    vext[0:blk, :LANES] = vp_ref[0]
    vext[blk:blk + tq, :LANES] = vm_ref[0]
    vext[blk + tq:, :LANES] = vn_ref[0]
    n_ctx = ck_ref.shape[1]
    nk = n_ctx + 3 * blk
    n_chunks = (N_HEADS * HEAD_DIM) // LANES
    rows = n_chunks * blk
    half = HEAD_DIM
    klow = lax.broadcasted_iota(i32, (1, LANES), 1) < half
    vlane = lax.broadcasted_iota(i32, (1, 2 * LANES), 1)
    vlow = (vlane < half) | ((vlane >= LANES) & (vlane < LANES + half))
    zero = jnp.zeros((), bf16)
    ck = ck_ref[0]
    cvx = jnp.concatenate([cv_ref[0], jnp.ones((n_ctx, LANES), bf16)], axis=1)
    ck_lo, ck_hi = jnp.where(klow, ck, zero), jnp.where(klow, zero, ck)
    cv_lo, cv_hi = jnp.where(vlow, cvx, zero), jnp.where(vlow, zero, cvx)
    sink2 = sink_ref[...]
    sink_lo, sink_hi = sink2[:, 0:1], sink2[:, half:half + 1]
    low = lax.broadcasted_iota(i32, (rows, LANES), 1) < half
    qi = lax.broadcasted_iota(i32, (rows, 3 * blk), 0) % blk
    pk = lax.broadcasted_iota(i32, (rows, 3 * blk), 1)
    band_bias = jnp.where(jnp.abs(pk - blk - qi) <= WINDOW, 0.0, NEG_INF).astype(f32)
    pcol = lax.broadcasted_iota(i32, (1, 3 * blk), 1)
    nt = (((1,), (1,)), ((), ()))

    def row_max(s):
        blocks = [s[:, i:i + LANES] for i in range(0, s.shape[1], LANES)]
        return jnp.max(functools.reduce(jnp.maximum, blocks), axis=1, keepdims=True)

    def sub(s, carry):
        r0 = pl.multiple_of(s * blk, blk)
        qs = q_ref[0, pl.ds(r0, blk), :]
        lhs = jnp.concatenate([qs[:, c * LANES:(c + 1) * LANES] for c in range(n_chunks)], axis=0)
        kl = kext[pl.ds(r0, 3 * blk), :]
        vl = vext[pl.ds(r0, 3 * blk), :]
        kbd = jnp.concatenate([ck_lo, jnp.where(klow, kl, zero),
                               ck_hi, jnp.where(klow, zero, kl)], axis=0)
        vbd = jnp.concatenate([cv_lo, jnp.where(vlow, vl, zero),
                               cv_hi, jnp.where(vlow, zero, vl)], axis=0)
        kpos = j * tq + r0 - blk + pcol
        bias = band_bias + jnp.where((kpos >= 0) & (kpos < seq_len), 0.0, NEG_INF).astype(f32)
        sc = lax.dot_general(lhs, kbd, nt, preferred_element_type=f32)
        s_lo = jnp.concatenate([sc[:, :n_ctx], sc[:, n_ctx:nk] + bias], axis=1)
        s_hi = jnp.concatenate([sc[:, nk:nk + n_ctx], sc[:, nk + n_ctx:] + bias], axis=1)
        m_lo = jnp.maximum(row_max(s_lo), sink_lo)
        m_hi = jnp.maximum(row_max(s_hi), sink_hi)
        e = jnp.concatenate([jnp.exp2(s_lo - m_lo), jnp.exp2(s_hi - m_hi)], axis=1).astype(bf16)
        ov = jnp.dot(e, vbd, preferred_element_type=f32)
        den = ov[:, LANES:] + jnp.exp2(sink2 - jnp.where(low, m_lo, m_hi))
        o = (ov[:, :LANES] / den).astype(bf16)
        for c in range(n_chunks):
            o_ref[0, pl.ds(r0, blk), c * LANES:(c + 1) * LANES] = o[c * blk:(c + 1) * blk]
        return carry

    lax.fori_loop(0, tq // blk, sub, 0, unroll=8)


def _attention(q, k, v, ck, cv, sinkcol):
    b, l, qw = q.shape
    c = ck.shape[1]
    tq = ATT_TILE
    r = tq // ATT_BLOCK
    nb = l // ATT_BLOCK
    prev = pl.BlockSpec((1, ATT_BLOCK, LANES), lambda i, j: (i, jnp.maximum(j * r - 1, 0), 0))
    main = pl.BlockSpec((1, tq, LANES), lambda i, j: (i, j, 0))
    nxt = pl.BlockSpec((1, ATT_BLOCK, LANES), lambda i, j: (i, jnp.minimum(j * r + r, nb - 1), 0))
    cspec = pl.BlockSpec((1, c, LANES), lambda i, j: (i, 0, 0))
    return pl.pallas_call(
        functools.partial(_attn_kernel, l),
        grid=(b, l // tq),
        in_specs=[pl.BlockSpec((1, tq, qw), lambda i, j: (i, j, 0)),
                  prev, main, nxt, prev, main, nxt, cspec, cspec,
                  pl.BlockSpec(sinkcol.shape, lambda i, j: (0, 0))],
        out_specs=pl.BlockSpec((1, tq, qw), lambda i, j: (i, j, 0)),
        out_shape=jax.ShapeDtypeStruct((b, l, qw), bf16),
        scratch_shapes=[pltpu.VMEM((tq + 2 * ATT_BLOCK, LANES), bf16),
                        pltpu.VMEM((tq + 2 * ATT_BLOCK, 2 * LANES), bf16)],
        compiler_params=_cparams(("parallel", "parallel")),
        name="attention",
    )(q, k, k, k, v, v, v, ck, cv, sinkcol)


def _pack_pair(lo, hi):
    lo = lax.bitcast_convert_type(lo.astype(bf16).astype(f32), u32)
    hi = lax.bitcast_convert_type(hi.astype(bf16).astype(f32), u32)
    return (lo >> 16) | (hi & jnp.uint32(0xFFFF0000))


def _fourier1_kernel(x_ref, mod_ref, g_ref, w_ref, cs_ref, m_ref, ct_ref, st_ref, z_ref, ab_ref):
    n1 = x_ref.shape[1]
    nt = x_ref.shape[2]
    x = x_ref[0].reshape(n1 * nt, x_ref.shape[3])
    h = _norm_mod(x, g_ref[...], mod_ref[0, 0:1, :], mod_ref[0, 1:2, :]).astype(bf16)
    p = jnp.dot(h, w_ref[...], preferred_element_type=f32)
    for g in range(FOURIER_GROUPS):
        ug = p[:, g * LANES:(g + 1) * LANES].astype(bf16)
        ab = jnp.dot(ug, cs_ref[...], preferred_element_type=f32)
        ab_ref[0] = ab[:, :LANES]
        ab_ref[1] = ab[:, LANES:]
        for t0 in range(0, nt, 2):
            stack = jnp.concatenate(
                [jnp.concatenate([ab_ref[0, pl.ds(t, n1, stride=nt), :],
                                  ab_ref[1, pl.ds(t, n1, stride=nt), :]], axis=0)
                 for t in (t0, t0 + 1)], axis=1).astype(bf16)
            z2 = jnp.dot(m_ref[...], stack, preferred_element_type=f32)
            for t in (t0, t0 + 1):
                z = z2[:, (t - t0) * LANES:(t - t0 + 1) * LANES]
                zr, zn = z[:n1], z[n1:]
                ct, st = ct_ref[t], st_ref[t]
                z_ref[0, g, t] = _pack_pair(ct * zr - st * zn, ct * zn + st * zr)


def _fourier2_kernel(scale, z_ref, m_ref, o_ref, zbuf, ybuf):
    _, grp, n2, tk, w = z_ref.shape
    for g in range(grp):
        zbuf[...] = z_ref[0, g].reshape(n2 * tk, w)
        for j in range(tk):
            zp = zbuf[pl.ds(j, n2, stride=tk), :]
            zr = lax.bitcast_convert_type(zp << 16, f32).astype(bf16)
            zn = lax.bitcast_convert_type(zp & jnp.uint32(0xFFFF0000), f32).astype(bf16)
            y = jnp.dot(m_ref[...], jnp.concatenate([zr, zn], axis=0), preferred_element_type=f32)
            ybuf[pl.ds(j, n2, stride=tk), :] = y * scale
        o_ref[0, g] = ybuf[...].reshape(n2, tk, w)


def _fourier(x, mod, g, w_f, cs):
    b, l, d = x.shape
    n2 = DFT_INNER
    n1 = l // n2
    grp, w = FOURIER_GROUPS, FOURIER_GROUP_W
    t2 = SUBLANES
    k1 = jnp.arange(n1, dtype=i32)
    ang1 = ((k1[:, None] * k1[None, :]) % n1).astype(f32) * (2.0 * math.pi / n1)
    c1, s1 = jnp.cos(ang1), jnp.sin(ang1)
    m1 = jnp.concatenate([jnp.concatenate([c1, -s1], axis=1),
                          jnp.concatenate([s1, c1], axis=1)], axis=0).astype(bf16)
    l2 = jnp.arange(n2, dtype=i32)
    angt = ((l2[:, None] * k1[None, :]) % l).astype(f32) * (2.0 * math.pi / l)
    ct = jnp.broadcast_to(jnp.cos(angt)[:, :, None], (n2, n1, w))
    st = jnp.broadcast_to(jnp.sin(angt)[:, :, None], (n2, n1, w))
    ang2 = ((l2[:, None] * l2[None, :]) % n2).astype(f32) * (2.0 * math.pi / n2)
    m2 = jnp.concatenate([jnp.cos(ang2), -jnp.sin(ang2)], axis=1).astype(bf16)

    tspec = pl.BlockSpec((t2, n1, w), lambda t, i: (t, 0, 0))
    z = pl.pallas_call(
        _fourier1_kernel,
        grid=(n2 // t2, b),
        in_specs=[pl.BlockSpec((1, n1, t2, d), lambda t, i: (i, 0, t, 0)),
                  pl.BlockSpec((1, 6, d), lambda t, i: (i, 0, 0)),
                  pl.BlockSpec((1, d), lambda t, i: (0, 0)),
                  pl.BlockSpec(w_f.shape, lambda t, i: (0, 0)),
                  pl.BlockSpec(cs.shape, lambda t, i: (0, 0)),
                  pl.BlockSpec(m1.shape, lambda t, i: (0, 0)), tspec, tspec],
        out_specs=pl.BlockSpec((1, grp, t2, n1, w), lambda t, i: (i, 0, t, 0, 0)),
        out_shape=jax.ShapeDtypeStruct((b, grp, n2, n1, w), u32),
        scratch_shapes=[pltpu.VMEM((2, n1 * t2, w), f32)],
        compiler_params=_cparams(("parallel", "parallel")),
        name="fourier_outer",
    )(x.reshape(b, n1, n2, d), mod, g, w_f, cs, m1, ct, st)

    tk = 2 * SUBLANES
    y = pl.pallas_call(
        functools.partial(_fourier2_kernel, 1.0 / math.sqrt(l * w)),
        grid=(b, n1 // tk),
        in_specs=[pl.BlockSpec((1, grp, n2, tk, w), lambda i, t: (i, 0, 0, t, 0)),
                  pl.BlockSpec(m2.shape, lambda i, t: (0, 0))],
        out_specs=pl.BlockSpec((1, grp, n2, tk, w), lambda i, t: (i, 0, 0, t, 0)),
        out_shape=jax.ShapeDtypeStruct((b, grp, n2, n1, w), f32),
        scratch_shapes=[pltpu.VMEM((n2 * tk, w), u32), pltpu.VMEM((n2 * tk, w), f32)],
        compiler_params=_cparams(("parallel", "parallel")),
        name="fourier_inner",
    )(z, m2)
    return y.reshape(b, grp, l, w)


def _first_max4(a):
    m = jnp.maximum(jnp.maximum(a[0], a[1]), jnp.maximum(a[2], a[3]))
    idx = jnp.where(a[0] == m, 0, jnp.where(a[1] == m, 1, jnp.where(a[2] == m, 2, 3)))
    return m, idx


def _pick4(vals, idx):
    return jnp.where(idx == 0, vals[0], jnp.where(idx == 1, vals[1],
                                                   jnp.where(idx == 2, vals[2], vals[3])))


def _route(f, rw_ref, rb_ref, tri_ref, base_ref, first_step, ri_ref, wc_ref, cnt_ref, meta_ref):
    rw2 = rw_ref[...]
    pieces = []
    for fp in (f if isinstance(f, (list, tuple)) else [f]):
        f_hi = fp.astype(bf16)
        f_lo = (fp - f_hi.astype(f32)).astype(bf16)
        part = jnp.dot(f_hi, rw2, preferred_element_type=f32)
        pieces.append(part[:, :LANES] + part[:, LANES:]
                      + jnp.dot(f_lo, rw2[:, :LANES], preferred_element_type=f32))
    logits = jnp.concatenate(pieces, axis=0)
    tm = logits.shape[0]
    sc = jax.nn.sigmoid(logits)
    st = sc.T
    bt = (sc + rb_ref[...]).T
    neg = jnp.full((1, tm), -jnp.inf, f32)
    gs = []
    for g in range(N_GROUPS):
        a = [bt[4 * g + i: 4 * g + i + 1] for i in range(4)]
        m1, i1 = _first_max4(a)
        rest = [jnp.where(i1 == i, neg, a[i]) for i in range(4)]
        m2, _ = _first_max4(rest)
        gs.append(m1 + m2)
    _, gsel = _first_max4(gs)
    a = [_pick4([bt[4 * g + i: 4 * g + i + 1] for g in range(N_GROUPS)], gsel) for i in range(4)]
    s = [_pick4([st[4 * g + i: 4 * g + i + 1] for g in range(N_GROUPS)], gsel) for i in range(4)]
    _, i1 = _first_max4(a)
    rest = [jnp.where(i1 == i, neg, a[i]) for i in range(4)]
    _, i2 = _first_max4(rest)
    w1 = _pick4(s, i1)
    w2 = _pick4(s, i2)
    tot = w1 + w2
    w1 = w1 / tot
    w2 = w2 / tot
    e0 = gsel * EXPERTS_PER_GROUP + i1
    e1 = gsel * EXPERTS_PER_GROUP + i2

    @pl.when(first_step)
    def _():
        base_ref[...] = jnp.zeros_like(base_ref)

    td = DISPATCH_TILE
    eid = lax.broadcasted_iota(i32, (N_EXPERTS, tm), 0)
    oh0 = (eid == e0).astype(f32)
    oh1 = (eid == e1).astype(f32)
    oh = oh0 + oh1
    before = jnp.dot(oh.astype(bf16), tri_ref[...], preferred_element_type=f32)
    lane_tile = lax.broadcasted_iota(i32, (N_EXPERTS, tm), 1) // td
    ei = lax.broadcasted_iota(i32, (N_EXPERTS, N_EXPERTS), 0)
    ej = lax.broadcasted_iota(i32, (N_EXPERTS, N_EXPERTS), 1)
    strict_lower = (ej < ei).astype(f32)
    run_start = jnp.zeros((N_EXPERTS, tm), f32)
    goff = base_ref[...]
    for s in range(tm // td):
        cnt_s = jnp.sum(oh[:, s * td:(s + 1) * td], axis=1, keepdims=True)
        pad_s = jnp.floor((cnt_s + 7.0) * 0.125) * 8.0
        pad_b = jnp.broadcast_to(pad_s, (N_EXPERTS, LANES))
        start_b = jnp.dot(strict_lower, pad_b, precision=HIGHEST, preferred_element_type=f32)
        run_start = jnp.where(lane_tile == s, start_b[:, 0:1], run_start)
        meta_ref[s, 0] = start_b.astype(i32)
        meta_ref[s, 1] = pad_b.astype(i32)
        meta_ref[s, 2] = goff.astype(i32)
        goff = goff + pad_b
    base_ref[...] = goff
    cnt_ref[...] = goff
    pos = before + run_start
    lp0 = jnp.sum(oh0 * pos, axis=0, keepdims=True)
    lp1 = jnp.sum(oh1 * pos, axis=0, keepdims=True)
    zi = jnp.zeros((1, tm), i32)
    ri_ref[...] = jnp.concatenate(
        [lp0.astype(i32), lp1.astype(i32), e0, e1,
         lax.bitcast_convert_type(w1, i32), lax.bitcast_convert_type(w2, i32), zi, zi], axis=0)
    zf = jnp.zeros((LANES - 4, tm), f32)
    wc_ref[...] = jnp.concatenate([w1, w2, lp0, lp1, zf], axis=0).T


def _outproj_kernel(yf_ref, o_ref, x_ref, mod_ref, w_ref, g_ref, rw_ref, rb_ref, tri_ref,
                    x1_ref, f_ref, ri_ref, wc_ref, cnt_ref, meta_ref, base_ref):
    tm = x_ref.shape[1]
    fs = []
    for r in range(0, tm, tm // 2):
        rs = slice(r, r + tm // 2)
        mix = jnp.concatenate([yf_ref[0, g, rs, :].astype(bf16) for g in range(FOURIER_GROUPS)]
                              + [o_ref[0, rs, :]], axis=1)
        y = jnp.dot(mix, w_ref[...], preferred_element_type=f32)
        x1 = x_ref[0, rs, :] + mod_ref[0, 2:3, :] * y
        x1_ref[0, rs, :] = x1
        f = _norm_mod(x1, g_ref[...], mod_ref[0, 3:4, :], mod_ref[0, 4:5, :])
        f_ref[0, rs, :] = f.astype(bf16)
        fs.append(f)
    first = (pl.program_id(0) == 0) & (pl.program_id(1) == 0)
    _route(fs, rw_ref, rb_ref, tri_ref, base_ref, first, ri_ref, wc_ref, cnt_ref, meta_ref)


def _before_in_tile(tm):
    tpos = jnp.arange(tm)
    return ((tpos[:, None] < tpos[None, :])
            & (tpos[:, None] // DISPATCH_TILE == tpos[None, :] // DISPATCH_TILE)).astype(bf16)


def _route_specs(b, l, tm):
    nl = l // tm
    rw = lambda d: pl.BlockSpec((d, 2 * LANES), lambda i, j: (0, 0))
    rb = pl.BlockSpec((1, LANES), lambda i, j: (0, 0))
    tri = pl.BlockSpec((tm, tm), lambda i, j: (0, 0))
    ns = tm // DISPATCH_TILE
    out_specs = [pl.BlockSpec((8, tm), lambda i, j: (0, i * nl + j)),
                 pl.BlockSpec((tm, LANES), lambda i, j: (i * nl + j, 0)),
                 pl.BlockSpec((N_EXPERTS, LANES), lambda i, j: (0, 0)),
                 pl.BlockSpec((ns, 3, N_EXPERTS, LANES), lambda i, j: (i * nl + j, 0, 0, 0))]
    out_shape = [jax.ShapeDtypeStruct((8, b * l), i32),
                 jax.ShapeDtypeStruct((b * l, LANES), f32),
                 jax.ShapeDtypeStruct((N_EXPERTS, LANES), f32),
                 jax.ShapeDtypeStruct((b * l // DISPATCH_TILE, 3, N_EXPERTS, LANES), i32)]
    return rw, rb, tri, out_specs, out_shape


def _outproj(yf, o, x, mod, w, g, rw, rb):
    b, l, d = x.shape
    tm = PROJ_TILE
    tri = _before_in_tile(tm)
    rws, rbs, tris, r_specs, r_shapes = _route_specs(b, l, tm)
    row = pl.BlockSpec((1, tm, d), lambda i, j: (i, j, 0))
    return pl.pallas_call(
        _outproj_kernel,
        grid=(b, l // tm),
        in_specs=[pl.BlockSpec((1, FOURIER_GROUPS, tm, LANES), lambda i, j: (i, 0, j, 0)),
                  pl.BlockSpec((1, tm, o.shape[2]), lambda i, j: (i, j, 0)),
                  row,
                  pl.BlockSpec((1, 6, d), lambda i, j: (i, 0, 0)),
                  pl.BlockSpec(w.shape, lambda i, j: (0, 0)),
                  pl.BlockSpec((1, d), lambda i, j: (0, 0)),
                  rws(d), rbs, tris],
        out_specs=[row, row] + r_specs,
        out_shape=[jax.ShapeDtypeStruct((b, l, d), f32),
                   jax.ShapeDtypeStruct((b, l, d), bf16)] + r_shapes,
        scratch_shapes=[pltpu.VMEM((N_EXPERTS, LANES), f32)],
        compiler_params=_cparams(("arbitrary", "arbitrary")),
        name="outproj_router",
    )(yf, o, x, mod, w, g, rw, rb, tri)


def _conv_kernel(seq_len, up_ref, um_ref, un_ref, x_ref, mod_ref, dw_ref, db_ref, lg_ref, lb_ref,
                 w_ref, pb_ref, g_ref, rw_ref, rb_ref, tri_ref,
                 x1_ref, f_ref, ri_ref, wc_ref, cnt_ref, meta_ref, base_ref, ext, conv_out):
    j = pl.program_id(1)
    tm = um_ref.shape[1]
    hl = CONV_HALO
    half = CONV_W // 2
    prev = jnp.where(j > 0, up_ref[0], jnp.zeros_like(up_ref[0]))
    nxt = jnp.where((j + 1) * tm < seq_len, un_ref[0], jnp.zeros_like(un_ref[0]))
    for c in range(ext.shape[0]):
        lanes_c = slice(c * LANES, (c + 1) * LANES)
        ext[c, 0:hl] = prev[:, lanes_c]
        ext[c, hl:hl + tm] = um_ref[0, :, lanes_c]
        ext[c, hl + tm:] = nxt[:, lanes_c]
    base = hl - half
    span = (CONV_W - 1) // SUBLANES * SUBLANES
    rows = CONV_ROWS

    def lane_chunk(c, carry):
        lanes = pl.ds(pl.multiple_of(c * LANES, LANES), LANES)
        for r in range(0, tm, rows):
            part = jnp.broadcast_to(db_ref[:, lanes], (rows, LANES))
            for phase in range(SUBLANES):
                win = ext[c, base + phase + r: base + phase + r + rows + span, :]
                same = None
                for t in range(phase, CONV_W, SUBLANES):
                    term = win[t - phase: t - phase + rows, :] * dw_ref[t:t + 1, lanes]
                    same = term if same is None else same + term
                part = part + same
            conv_out[r:r + rows, lanes] = part
        return carry

    lax.fori_loop(0, um_ref.shape[2] // LANES, lane_chunk, 0)
    fs = []
    for r in range(0, tm, tm // 2):
        rs = slice(r, r + tm // 2)
        acc = conv_out[rs, :]
        mu = jnp.mean(acc, axis=-1, keepdims=True)
        cen = acc - mu
        var = jnp.mean(cen * cen, axis=-1, keepdims=True)
        ln = cen * lax.rsqrt(var + EPS) * lg_ref[...] + lb_ref[...]
        act = ln * jax.nn.sigmoid(ln)
        y = jnp.dot(act.astype(bf16), w_ref[...], preferred_element_type=f32) + pb_ref[...]
        x1 = x_ref[0, rs, :] + mod_ref[0, 2:3, :] * y
        x1_ref[0, rs, :] = x1
        f = _norm_mod(x1, g_ref[...], mod_ref[0, 3:4, :], mod_ref[0, 4:5, :])
        f_ref[0, rs, :] = f.astype(bf16)
        fs.append(f)
    first = (pl.program_id(0) == 0) & (j == 0)
    _route(fs, rw_ref, rb_ref, tri_ref, base_ref, first, ri_ref, wc_ref, cnt_ref, meta_ref)


def _conv(u, x, mod, dw_w, dw_b, ln_g, ln_b, pw2_w, pw2_b, g, rw, rb):
    b, l, d = x.shape
    tm = TOKEN_TILE
    tri = _before_in_tile(tm)
    hl = CONV_HALO
    r = tm // hl
    nh = l // hl
    rws, rbs, tris, r_specs, r_shapes = _route_specs(b, l, tm)
    row = pl.BlockSpec((1, tm, d), lambda i, j: (i, j, 0))
    vec = pl.BlockSpec((1, d), lambda i, j: (0, 0))
    return pl.pallas_call(
        functools.partial(_conv_kernel, l),
        grid=(b, l // tm),
        in_specs=[pl.BlockSpec((1, hl, d), lambda i, j: (i, jnp.maximum(j * r - 1, 0), 0)),
                  row,
                  pl.BlockSpec((1, hl, d), lambda i, j: (i, jnp.minimum(j * r + r, nh - 1), 0)),
                  row,
                  pl.BlockSpec((1, 6, d), lambda i, j: (i, 0, 0)),
                  pl.BlockSpec(dw_w.shape, lambda i, j: (0, 0)),
                  vec, vec, vec,
                  pl.BlockSpec(pw2_w.shape, lambda i, j: (0, 0)),
                  vec, vec, rws(d), rbs, tris],
        out_specs=[row, row] + r_specs,
        out_shape=[jax.ShapeDtypeStruct((b, l, d), f32),
                   jax.ShapeDtypeStruct((b, l, d), bf16)] + r_shapes,
        scratch_shapes=[pltpu.VMEM((N_EXPERTS, LANES), f32),
                        pltpu.VMEM((d // LANES, tm + 2 * hl, LANES), f32),
                        pltpu.VMEM((tm, d), f32)],
        compiler_params=_cparams(("arbitrary", "arbitrary")),
        name="conv_router",
    )(u, u, u, x, mod, dw_w, dw_b, ln_g, ln_b, pw2_w, pw2_b, g, rw, rb, tri)


def _pack_bf16_pairs(x):
    h = x.shape[1] // 2
    lo = lax.bitcast_convert_type(x[:, :h], u32)
    hi = lax.bitcast_convert_type(x[:, h:], u32)
    return (lo >> 16) | (hi & jnp.uint32(0xFFFF0000))


def _unpack_bf16_pairs(u):
    lo = lax.bitcast_convert_type(u << 16, f32)
    hi = lax.bitcast_convert_type(u & jnp.uint32(0xFFFF0000), f32)
    return jnp.concatenate([lo, hi], axis=1).astype(bf16)


def _run_copies(meta, tile, local_ref, hbm_ref, sem, to_hbm):
    start_ref, size_ref, dst_ref = meta
    for e in range(N_EXPERTS):
        k = tile * N_EXPERTS + e
        size = pl.multiple_of(size_ref[k], RUN_ALIGN)

        @pl.when(size > 0)
        def _():
            loc = local_ref.at[pl.ds(pl.multiple_of(start_ref[k], RUN_ALIGN), size)]
            glob = hbm_ref.at[pl.ds(pl.multiple_of(dst_ref[k], RUN_ALIGN), size)]
            if to_hbm:
                pltpu.make_async_copy(loc, glob, sem).start()
            else:
                pltpu.make_async_copy(glob, loc, sem).start()


def _wait_rows(rows, local_ref, hbm_ref, sem):
    rows = pl.multiple_of(rows, RUN_ALIGN)

    @pl.when(rows > 0)
    def _():
        pltpu.make_async_copy(local_ref.at[pl.ds(0, rows)], hbm_ref.at[pl.ds(0, rows)], sem).wait()


def _dispatch_kernel(start_ref, size_ref, dst_ref, tot_ref, tail_start_ref, tail_size_ref, nv_ref,
                     f_ref, lp_ref, xs_ref, loc, zbuf, sem, zsem):
    i = pl.program_id(0)
    n = pl.num_programs(0)
    slot = i % 2
    subs = loc.shape[1]
    rows = loc.shape[2]
    td = f_ref.shape[0] // subs
    meta = (start_ref, size_ref, dst_ref)

    def drain(step, which):
        for s in range(subs):
            _wait_rows(tot_ref[step * subs + s], loc.at[which, s], xs_ref, sem.at[which])

    @pl.when(i >= 2)
    def _():
        drain(i - 2, slot)

    r = lax.broadcasted_iota(i32, (rows, td), 0)
    for s in range(subs):
        cols = slice(s * td, (s + 1) * td)
        pick0 = r == lp_ref[0:1, cols]
        pick1 = r == lp_ref[1:2, cols]
        onehot = (pick0 | pick1).astype(bf16)
        sorted_rows = jnp.dot(onehot, f_ref[cols, :], preferred_element_type=f32)
        half = sorted_rows.shape[1] // 2
        loc[slot, s, :, :half] = _pack_bf16_pairs(sorted_rows)
        w0 = lax.bitcast_convert_type(lp_ref[4:5, cols], f32)
        w1 = lax.bitcast_convert_type(lp_ref[5:6, cols], f32)
        row_w = jnp.sum(jnp.where(pick0, w0, 0.0) + jnp.where(pick1, w1, 0.0), axis=1, keepdims=True)
        loc[slot, s, :, half:] = jnp.broadcast_to(lax.bitcast_convert_type(row_w, u32), (rows, LANES))
        _run_copies(meta, i * subs + s, loc.at[slot, s], xs_ref, sem.at[slot], to_hbm=True)

    @pl.when(i == n - 1)
    def _():
        zbuf[...] = jnp.zeros_like(zbuf)
        total = 0
        for e in range(N_EXPERTS):
            size = pl.multiple_of(tail_size_ref[e], RUN_ALIGN)
            total = total + size

            @pl.when(size > 0)
            def _():
                pltpu.make_async_copy(
                    zbuf.at[pl.ds(0, size)],
                    xs_ref.at[pl.ds(pl.multiple_of(tail_start_ref[e], RUN_ALIGN), size)], zsem).start()

        _wait_rows(total, zbuf, xs_ref, zsem)

        def zero_block(k, c):
            pltpu.make_async_copy(zbuf, xs_ref.at[pl.ds(pl.multiple_of(k * zbuf.shape[0], RUN_ALIGN),
                                                        zbuf.shape[0])], zsem).start()
            return c

        def wait_block(k, c):
            pltpu.make_async_copy(zbuf, xs_ref.at[pl.ds(0, zbuf.shape[0])], zsem).wait()
            return c

        n_blocks = xs_ref.shape[0] // zbuf.shape[0]
        lax.fori_loop(nv_ref[0], n_blocks, zero_block, 0)
        lax.fori_loop(nv_ref[0], n_blocks, wait_block, 0)
        drain(i, slot)

        @pl.when(i >= 1)
        def _():
            drain(i - 1, 1 - slot)


def _dispatch(tables, f2, ri, n_slots):
    t, d = f2.shape
    subs = DISPATCH_SUBTILES
    tm = DISPATCH_TILE * subs
    return pl.pallas_call(
        _dispatch_kernel,
        grid_spec=pltpu.PrefetchScalarGridSpec(
            num_scalar_prefetch=7,
            grid=(t // tm,),
            in_specs=[pl.BlockSpec((tm, d), lambda i, *_: (i, 0)),
                      pl.BlockSpec((8, tm), lambda i, *_: (0, i))],
            out_specs=pl.BlockSpec(memory_space=pl.ANY),
            scratch_shapes=[pltpu.VMEM((2, subs, LOCAL_ROWS, d // 2 + LANES), u32),
                            pltpu.VMEM((EXPERT_ROWS, d // 2 + LANES), u32),
                            pltpu.SemaphoreType.DMA((2,)), pltpu.SemaphoreType.DMA(())]),
        out_shape=jax.ShapeDtypeStruct((n_slots, d // 2 + LANES), u32),
        compiler_params=_cparams(("arbitrary",)),
        name="moe_dispatch",
    )(*tables, f2, ri)


def _expert_kernel(be_ref, second_ref, nv_ref, x_ref, wg1, wu1, wd1, wg2, wu2, wd2, y_ref, *w_bf):
    del second_ref
    i = pl.program_id(0)
    tb = y_ref.shape[0] // 2
    half = y_ref.shape[1]
    e1, e2 = be_ref[2 * i], be_ref[2 * i + 1]
    first, second = w_bf[:3], w_bf[3:]

    @pl.when(jnp.logical_or(i == 0, e1 != be_ref[jnp.maximum(2 * i - 2, 0)]))
    def _():
        for dst, src in zip(first, (wg1, wu1, wd1)):
            dst[...] = src[0, 0].astype(bf16)

    @pl.when(e2 != e1)
    def _():
        for dst, src in zip(second, (wg2, wu2, wd2)):
            dst[...] = src[0, 0].astype(bf16)

    def block(k, weights):
        wgb, wub, wdb = weights
        rows = slice(k * tb, (k + 1) * tb)
        xb = _unpack_bf16_pairs(x_ref[rows, :half])
        row_w = lax.bitcast_convert_type(x_ref[rows, half:], f32)
        gate = jnp.dot(xb, wgb[...], preferred_element_type=f32)
        up = jnp.dot(xb, wub[...], preferred_element_type=f32)
        hid = (gate * jax.nn.sigmoid(gate) * up).astype(bf16)
        y = jnp.dot(hid, wdb[...], preferred_element_type=f32)
        y = jnp.concatenate([y[:, c:c + LANES] * row_w for c in range(0, y.shape[1], LANES)], axis=1)
        y_ref[rows, :] = _pack_bf16_pairs(y.astype(bf16).astype(f32))

    used = 2 * i < nv_ref[0]

    @pl.when(used & (e2 == e1))
    def _():
        block(0, first)
        block(1, first)

    @pl.when(used & (e2 != e1))
    def _():
        block(0, first)
        block(1, second)

    @pl.when(jnp.logical_not(used))
    def _():
        y_ref[...] = jnp.zeros_like(y_ref)


def _experts(block_e, n_valid, xs, w_gate, w_up, w_down, layer):
    ns, xw = xs.shape
    nb = EXPERT_BLOCKS_PER_STEP
    tb = EXPERT_ROWS
    d, ff = w_gate.shape[2:]
    dh = d // 2
    e1, e2 = block_e[0::2], block_e[1::2]
    second = lax.cummax(jnp.where(e2 != e1, e2, 0))
    map1 = lambda i, be, sec, nv: (layer, be[2 * i], 0, 0)
    map2 = lambda i, be, sec, nv: (layer, sec[i], 0, 0)
    w_specs = [pl.BlockSpec((1, 1, d, ff), m) for m in (map1, map1)] + [pl.BlockSpec((1, 1, ff, d), map1)]
    w_specs += [pl.BlockSpec((1, 1, d, ff), m) for m in (map2, map2)] + [pl.BlockSpec((1, 1, ff, d), map2)]
    w_scratch = 2 * [pltpu.VMEM((d, ff), bf16), pltpu.VMEM((d, ff), bf16), pltpu.VMEM((ff, d), bf16)]
    return pl.pallas_call(
        _expert_kernel,
        grid_spec=pltpu.PrefetchScalarGridSpec(
            num_scalar_prefetch=3,
            grid=(ns // (nb * tb),),
            in_specs=[pl.BlockSpec((nb * tb, xw), lambda i, *_: (i, 0))] + w_specs,
            out_specs=pl.BlockSpec((nb * tb, dh), lambda i, *_: (i, 0)),
            scratch_shapes=w_scratch),
        out_shape=jax.ShapeDtypeStruct((ns, dh), u32),
        compiler_params=_cparams(("arbitrary",)),
        name="moe_experts",
    )(block_e, second, n_valid, xs, w_gate, w_up, w_down, w_gate, w_up, w_down)


def _combine_kernel(final, start_ref, size_ref, dst_ref, tot_ref, ys_ref, wc_ref, x_ref, mod_ref,
                    g_ref, *rest):
    if final:
        o_ref, loc, sem = rest
    else:
        nmod_ref, w_ref, b_ref, o_ref, u_ref, loc, sem = rest
    i = pl.program_id(0)
    n = pl.num_programs(0)
    slot = i % 2
    subs = loc.shape[1]
    rows = loc.shape[2]
    td = x_ref.shape[0] // subs
    meta = (start_ref, size_ref, dst_ref)

    def fetch(step, which):
        for s in range(subs):
            _run_copies(meta, step * subs + s, loc.at[which, s], ys_ref, sem.at[which, s],
                        to_hbm=False)

    @pl.when(i == 0)
    def _():
        loc[...] = jnp.zeros_like(loc)
        fetch(i, slot)

    @pl.when(i + 1 < n)
    def _():
        fetch(i + 1, 1 - slot)

    c = lax.broadcasted_iota(i32, (td, rows), 1)
    parts = []
    for s in range(subs):
        _wait_rows(tot_ref[i * subs + s], loc.at[slot, s], ys_ref, sem.at[slot, s])
        wc = wc_ref[s * td:(s + 1) * td, :]
        sel = ((c == wc[:, 2:3].astype(i32)) | (c == wc[:, 3:4].astype(i32))).astype(bf16)
        parts.append(jnp.dot(sel, _unpack_bf16_pairs(loc[slot, s]), preferred_element_type=f32))
    xo = x_ref[...] + mod_ref[0, 5:6, :] * jnp.concatenate(parts, axis=0)
    if final:
        ms = jnp.mean(xo * xo, axis=-1, keepdims=True)
        o_ref[...] = xo * lax.rsqrt(ms + EPS) * g_ref[...]
    else:
        o_ref[...] = xo
        h = _norm_mod(xo, g_ref[...], nmod_ref[0, 0:1, :], nmod_ref[0, 1:2, :])
        p = jnp.dot(h.astype(bf16), w_ref[...], preferred_element_type=f32) + b_ref[...]
        ch = p.shape[1] // 2
        u_ref[...] = p[:, :ch] * jax.nn.sigmoid(p[:, ch:])


def _combine(tables, ys, wc, x, mod, g, glu=None):
    b, l, d = x.shape
    subs = DISPATCH_SUBTILES if glu is None else GLU_SUBTILES
    tm = DISPATCH_TILE * subs
    per_batch = l // tm
    const = lambda i, *_: (0, 0)
    tile = lambda cols: pl.BlockSpec((tm, cols), lambda i, *_: (i, 0))
    mod_spec = pl.BlockSpec((1, 6, d), lambda i, *_: (i // per_batch, 0, 0))
    in_specs = [pl.BlockSpec(memory_space=pl.ANY), tile(LANES), tile(d), mod_spec,
                pl.BlockSpec((1, d), const)]
    args = [ys, wc, x.reshape(b * l, d), mod, g]
    out_specs = [tile(d)]
    out_shape = [jax.ShapeDtypeStruct((b * l, d), f32)]
    if glu is not None:
        nmod, w, bias = glu
        in_specs += [mod_spec, pl.BlockSpec(w.shape, const), pl.BlockSpec(bias.shape, const)]
        args += [nmod, w, bias]
        out_specs.append(tile(w.shape[1] // 2))
        out_shape.append(jax.ShapeDtypeStruct((b * l, w.shape[1] // 2), f32))
    outs = pl.pallas_call(
        functools.partial(_combine_kernel, glu is None),
        grid_spec=pltpu.PrefetchScalarGridSpec(
            num_scalar_prefetch=4,
            grid=(b * per_batch,),
            in_specs=in_specs,
            out_specs=out_specs,
            scratch_shapes=[pltpu.VMEM((2, subs, LOCAL_ROWS, d // 2), u32),
                            pltpu.SemaphoreType.DMA((2, subs))]),
        out_shape=out_shape,
        compiler_params=_cparams(("arbitrary",)),
        name="moe_combine",
    )(*tables, *args)
    return [o.reshape(b, l, -1) for o in outs]


def _moe(f, routed, x, mod, g, w_gate, w_up, w_down, layer, glu=None):
    ri, wc, cnt, meta = routed
    b, l, d = x.shape
    t = b * l
    tb = EXPERT_ROWS
    n_tiles = t // DISPATCH_TILE
    used = cnt[:, 0].astype(i32)
    region = (used + tb - 1) // tb * tb
    gend = jnp.cumsum(region)
    gstart = gend - region
    max_rows = 2 * t + n_tiles * N_EXPERTS * (RUN_ALIGN - 1) + N_EXPERTS * (tb - 1)
    n_blocks = -(-max_rows // (tb * EXPERT_BLOCKS_PER_STEP)) * EXPERT_BLOCKS_PER_STEP
    m = meta[:, :, :, 0]
    run_start = m[:, 0].reshape(-1)
    run_size = m[:, 1].reshape(-1)
    run_dst = (m[:, 2] + gstart[None, :]).reshape(-1)
    tile_rows = jnp.sum(m[:, 1], axis=1)
    block_row = jnp.arange(n_blocks, dtype=i32) * tb
    block_e = jnp.minimum(jnp.sum((block_row[:, None] >= gend[None, :]).astype(i32), axis=1),
                          N_EXPERTS - 1)
    n_valid = (gend[-1] // tb).reshape(1)
    xs = _dispatch((run_start, run_size, run_dst, tile_rows, gstart + used, region - used, n_valid),
                   f.reshape(t, d), ri, n_blocks * tb)
    ys = _experts(block_e, n_valid, xs, w_gate, w_up, w_down, layer)
    return _combine((run_start, run_size, run_dst, tile_rows), ys, wc, x, mod, g, glu)


def _rope_tables(l):
    lane = jnp.arange(LANES)
    dh = lane % HEAD_DIM
    inv = ROPE_THETA ** (-(dh % 16).astype(f32) / 16.0)
    sign = jnp.where((dh % 32) < 16, -1.0, 1.0).astype(f32)
    by_row = (dh // 32)[None, None, :] == 0
    ang_r = jnp.arange(l // GRID_W, dtype=f32)[:, None] * inv[None, :]
    ang_c = jnp.arange(GRID_W, dtype=f32)[:, None] * inv[None, :]
    cos = jnp.where(by_row, jnp.cos(ang_r)[:, None, :], jnp.cos(ang_c)[None, :, :])
    sin = jnp.where(by_row, jnp.sin(ang_r)[:, None, :], jnp.sin(ang_c)[None, :, :])
    return cos.reshape(l, LANES), (sin * sign[None, None, :]).reshape(l, LANES)


def kernel(x, c, ctx, c_ctx, ada_w, ada_b, norm_mix_g, norm_ffn_g, even_w_in, even_w_out, even_sink, conv_pw1_w, conv_pw1_b, conv_dw_w, conv_dw_b, conv_ln_g, conv_ln_b, conv_pw2_w, conv_pw2_b, router_w, router_b, moe_w_gate, moe_w_up, moe_w_down, final_norm_g):
    b, l, d = x.shape
    depth = ada_w.shape[0]
    assert depth == 2 and b < COND_ROWS
    ctx_row = b
    cond = jnp.zeros((COND_ROWS, d), f32).at[:b].set(c).at[ctx_row].set(c_ctx)
    mods = _adaln(cond, ada_w, ada_b).reshape(depth, COND_ROWS, 6, d)

    heads = jnp.arange(N_HEADS).reshape(N_KV_HEADS, N_HEADS // N_KV_HEADS).T.reshape(-1)
    qperm = (heads[:, None] * HEAD_DIM + jnp.arange(HEAD_DIM)[None, :]).reshape(-1)
    fw = FOURIER_GROUPS * FOURIER_GROUP_W
    qw = N_HEADS * HEAD_DIM
    w_in = even_w_in[0]
    w_in_p = jnp.concatenate([w_in[:, :fw], w_in[:, fw:fw + qw][:, qperm], w_in[:, fw + qw:]],
                             axis=1).astype(bf16)
    w_out = even_w_out[0]
    w_out_p = jnp.concatenate([w_out[:fw], w_out[fw:][qperm]], axis=0).astype(bf16)
    sink_pairs = (even_sink[0].astype(f32) * LOG2E).reshape(N_KV_HEADS, N_HEADS // N_KV_HEADS).T
    sinkcol = jnp.repeat(jnp.repeat(sink_pairs, HEAD_DIM, axis=1), ATT_BLOCK, axis=0)

    cidx = jnp.arange(FOURIER_GROUP_W, dtype=i32)
    angc = ((cidx[:, None] * cidx[None, :]) % FOURIER_GROUP_W).astype(f32) * (2.0 * math.pi / FOURIER_GROUP_W)
    cs = jnp.concatenate([jnp.cos(angc), jnp.sin(angc)], axis=1).astype(bf16)
    cos_t, sin_t = _rope_tables(l)

    rw32 = jnp.zeros((d, LANES), f32).at[:, :N_EXPERTS].set(router_w.astype(f32))
    rw_hi = rw32.astype(bf16)
    rw = jnp.concatenate([rw_hi, (rw32 - rw_hi.astype(f32)).astype(bf16)], axis=1)
    rb = jnp.zeros((1, LANES), f32).at[0, :N_EXPERTS].set(router_b)
    row = lambda v: v.reshape(1, -1)

    q, k, v = _inproj(x, mods[0], row(norm_mix_g[0]), w_in_p[:, fw:], cos_t, sin_t)
    ck, cv = _ctxkv(ctx, mods[0], row(norm_mix_g[0]), w_in_p[:, fw + qw:], ctx_row)
    yf = _fourier(x, mods[0], row(norm_mix_g[0]), w_in_p[:, :fw], cs)
    att = _attention(q, k, v, ck, cv, sinkcol)
    x1, f, *routed = _outproj(yf, att, x, mods[0], w_out_p, row(norm_ffn_g[0]), rw, rb)
    x2, u = _moe(f, routed, x1, mods[0], row(norm_mix_g[1]), moe_w_gate, moe_w_up, moe_w_down,
                 layer=0, glu=(mods[1], conv_pw1_w[0].astype(bf16), row(conv_pw1_b[0])))

    x3, f, *routed = _conv(u, x2, mods[1], conv_dw_w[0], row(conv_dw_b[0]), row(conv_ln_g[0]),
                           row(conv_ln_b[0]), conv_pw2_w[0].astype(bf16), row(conv_pw2_b[0]),
                           row(norm_ffn_g[1]), rw, rb)
    (out,) = _moe(f, routed, x3, mods[1], row(final_norm_g), moe_w_gate, moe_w_up, moe_w_down,
                  layer=1)
    return out
```

```python
import functools
import math

import jax
import jax.numpy as jnp
from jax import lax
from jax.experimental import pallas as pl
from jax.experimental.pallas import tpu as pltpu

f32 = jnp.float32
bf16 = jnp.bfloat16
i32 = jnp.int32
u32 = jnp.uint32
HIGHEST = lax.Precision.HIGHEST

GRID_W = 64
HEAD_DIM = 64
N_HEADS = 8
N_KV_HEADS = 2
WINDOW = 128
ATT_BLOCK = 128
ROPE_THETA = 10000.0
FOURIER_GROUPS = 4
FOURIER_GROUP_W = 128
CONV_W = 31
N_EXPERTS = 16
N_GROUPS = 4
EXPERTS_PER_GROUP = 4
EPS = 1e-6
NEG_INF = -1e30
LOG2E = math.log2(math.e)

LANES = 128
SUBLANES = 8
COND_ROWS = 8
DFT_INNER = 64
TOKEN_TILE = 512
PROJ_TILE = 1024
ATT_TILE = 1024
EXPERT_ROWS = 512
EXPERT_BLOCKS_PER_STEP = 2
DISPATCH_TILE = 256
DISPATCH_SUBTILES = 4
GLU_SUBTILES = 2
RUN_ALIGN = 8
LOCAL_ROWS = -(-(2 * DISPATCH_TILE + N_EXPERTS * (RUN_ALIGN - 1)) // LANES) * LANES
CONV_HALO = 16
CONV_ROWS = 128
VMEM_LIMIT = 56 * 1024 * 1024


def _cparams(sem, vmem=VMEM_LIMIT):
    return pltpu.CompilerParams(dimension_semantics=sem, vmem_limit_bytes=vmem)


def _adaln_kernel(cond_ref, w_ref, b_ref, o_ref):
    s = cond_ref[...]
    s = s * jax.nn.sigmoid(s)
    w = w_ref[0]
    s_hi, w_hi = s.astype(bf16), w.astype(bf16)
    s_lo = (s - s_hi.astype(f32)).astype(bf16)
    w_lo = (w - w_hi.astype(f32)).astype(bf16)
    dot = functools.partial(jnp.dot, preferred_element_type=f32)
    o_ref[0] = dot(s_hi, w_hi) + dot(s_hi, w_lo) + dot(s_lo, w_hi) + b_ref[0]


def _adaln(cond, ada_w, ada_b):
    depth, d, n = ada_w.shape
    tn = 3072
    return pl.pallas_call(
        _adaln_kernel,
        grid=(depth, n // tn),
        in_specs=[pl.BlockSpec((COND_ROWS, d), lambda i, j: (0, 0)),
                  pl.BlockSpec((1, d, tn), lambda i, j: (i, 0, j)),
                  pl.BlockSpec((1, 1, tn), lambda i, j: (i, 0, j))],
        out_specs=pl.BlockSpec((1, COND_ROWS, tn), lambda i, j: (i, 0, j)),
        out_shape=jax.ShapeDtypeStruct((depth, COND_ROWS, n), f32),
        compiler_params=_cparams(("arbitrary", "arbitrary")),
        name="adaln",
    )(cond, ada_w, ada_b.reshape(depth, 1, n))


def _norm_mod(x, g, shift, scale):
    ms = jnp.mean(x * x, axis=-1, keepdims=True)
    return (x * lax.rsqrt(ms + EPS)) * (g * (1.0 + scale)) + shift


def _rope(p, cos, sin_signed, first_half):
    rot = jnp.where(first_half, pltpu.roll(p, LANES - 16, axis=1), pltpu.roll(p, 16, axis=1))
    return p * cos + rot * sin_signed


def _inproj_kernel(x_ref, mod_ref, g_ref, w_ref, cos_ref, sin_ref, q_ref, k_ref, v_ref):
    qw = N_HEADS * HEAD_DIM
    tm = x_ref.shape[1]
    for r in range(0, tm, tm // 2):
        rs = slice(r, r + tm // 2)
        h = _norm_mod(x_ref[0, rs, :], g_ref[...], mod_ref[0, 0:1, :], mod_ref[0, 1:2, :])
        p = jnp.dot(h.astype(bf16), w_ref[...], preferred_element_type=f32)
        cos = cos_ref[rs, :]
        sin = sin_ref[rs, :]
        lane = lax.broadcasted_iota(i32, cos.shape, 1)
        first_half = (lane % 32) < 16
        for c in range(qw // LANES):
            qc = p[:, c * LANES:(c + 1) * LANES]
            q_ref[0, rs, c * LANES:(c + 1) * LANES] = (
                _rope(qc, cos, sin, first_half) * (LOG2E * HEAD_DIM ** -0.5)).astype(bf16)
        k_ref[0, rs, :] = _rope(p[:, qw:qw + LANES], cos, sin, first_half).astype(bf16)
        v_ref[0, rs, :] = p[:, qw + LANES:].astype(bf16)


def _inproj(x, mod, g, w, cos_t, sin_t):
    b, l, d = x.shape
    tm = PROJ_TILE
    n = w.shape[1]
    return pl.pallas_call(
        _inproj_kernel,
        grid=(b, l // tm),
        in_specs=[pl.BlockSpec((1, tm, d), lambda i, j: (i, j, 0)),
                  pl.BlockSpec((1, 6, d), lambda i, j: (i, 0, 0)),
                  pl.BlockSpec((1, d), lambda i, j: (0, 0)),
                  pl.BlockSpec((d, n), lambda i, j: (0, 0)),
                  pl.BlockSpec((tm, LANES), lambda i, j: (j, 0)),
                  pl.BlockSpec((tm, LANES), lambda i, j: (j, 0))],
        out_specs=[pl.BlockSpec((1, tm, N_HEADS * HEAD_DIM), lambda i, j: (i, j, 0)),
                   pl.BlockSpec((1, tm, LANES), lambda i, j: (i, j, 0)),
                   pl.BlockSpec((1, tm, LANES), lambda i, j: (i, j, 0))],
        out_shape=[jax.ShapeDtypeStruct((b, l, N_HEADS * HEAD_DIM), bf16),
                   jax.ShapeDtypeStruct((b, l, LANES), bf16),
                   jax.ShapeDtypeStruct((b, l, LANES), bf16)],
        compiler_params=_cparams(("parallel", "parallel")),
        name="inproj",
    )(x, mod, g, w, cos_t, sin_t)


def _ctxkv_kernel(x_ref, mod_ref, g_ref, w_ref, k_ref, v_ref):
    h = _norm_mod(x_ref[0], g_ref[...], mod_ref[0, 0:1, :], mod_ref[0, 1:2, :])
    p = jnp.dot(h.astype(bf16), w_ref[...], preferred_element_type=f32)
    k_ref[0] = p[:, :LANES].astype(bf16)
    v_ref[0] = p[:, LANES:].astype(bf16)


def _ctxkv(ctx, mod, g, w_kv, ctx_row):
    b, c, d = ctx.shape
    return pl.pallas_call(
        _ctxkv_kernel,
        grid=(b,),
        in_specs=[pl.BlockSpec((1, c, d), lambda i: (i, 0, 0)),
                  pl.BlockSpec((1, 6, d), lambda i: (ctx_row, 0, 0)),
                  pl.BlockSpec((1, d), lambda i: (0, 0)),
                  pl.BlockSpec((d, 2 * LANES), lambda i: (0, 0))],
        out_specs=[pl.BlockSpec((1, c, LANES), lambda i: (i, 0, 0)),
                   pl.BlockSpec((1, c, LANES), lambda i: (i, 0, 0))],
        out_shape=[jax.ShapeDtypeStruct((b, c, LANES), bf16),
                   jax.ShapeDtypeStruct((b, c, LANES), bf16)],
        compiler_params=_cparams(("parallel",)),
        name="ctxkv",
    )(ctx, mod, g, w_kv)


def _attn_kernel(seq_len, q_ref, kp_ref, km_ref, kn_ref, vp_ref, vm_ref, vn_ref,
                 ck_ref, cv_ref, sink_ref, o_ref, kext, vext):
    j = pl.program_id(1)
    tq = ATT_TILE
    blk = ATT_BLOCK
    kext[0:blk] = kp_ref[0]
    kext[blk:blk + tq] = km_ref[0]
    kext[blk + tq:] = kn_ref[0]
    vext[:, LANES:] = jnp.ones((tq + 2 * blk, LANES), bf16)
    vext[0:blk, :LANES] = vp_ref[0]
    vext[blk:blk + tq, :LANES] = vm_ref[0]
    vext[blk + tq:, :LANES] = vn_ref[0]
    n_ctx = ck_ref.shape[1]
    nk = n_ctx + 3 * blk
    n_chunks = (N_HEADS * HEAD_DIM) // LANES
    rows = n_chunks * blk
    half = HEAD_DIM
    klow = lax.broadcasted_iota(i32, (1, LANES), 1) < half
    vlane = lax.broadcasted_iota(i32, (1, 2 * LANES), 1)
    vlow = (vlane < half) | ((vlane >= LANES) & (vlane < LANES + half))
    zero = jnp.zeros((), bf16)
    ck = ck_ref[0]
    cvx = jnp.concatenate([cv_ref[0], jnp.ones((n_ctx, LANES), bf16)], axis=1)
    ck_lo, ck_hi = jnp.where(klow, ck, zero), jnp.where(klow, zero, ck)
    cv_lo, cv_hi = jnp.where(vlow, cvx, zero), jnp.where(vlow, zero, cvx)
    sink2 = sink_ref[...]
    sink_lo, sink_hi = sink2[:, 0:1], sink2[:, half:half + 1]
    low = lax.broadcasted_iota(i32, (rows, LANES), 1) < half
    qi = lax.broadcasted_iota(i32, (rows, 3 * blk), 0) % blk
    pk = lax.broadcasted_iota(i32, (rows, 3 * blk), 1)
    band_bias = jnp.where(jnp.abs(pk - blk - qi) <= WINDOW, 0.0, NEG_INF).astype(f32)
    pcol = lax.broadcasted_iota(i32, (1, 3 * blk), 1)
    nt = (((1,), (1,)), ((), ()))

    def row_max(s):
        blocks = [s[:, i:i + LANES] for i in range(0, s.shape[1], LANES)]
        return jnp.max(functools.reduce(jnp.maximum, blocks), axis=1, keepdims=True)

    def sub(s, carry):
        r0 = pl.multiple_of(s * blk, blk)
        qs = q_ref[0, pl.ds(r0, blk), :]
        lhs = jnp.concatenate([qs[:, c * LANES:(c + 1) * LANES] for c in range(n_chunks)], axis=0)
        kl = kext[pl.ds(r0, 3 * blk), :]
        vl = vext[pl.ds(r0, 3 * blk), :]
        kbd = jnp.concatenate([ck_lo, jnp.where(klow, kl, zero),
                               ck_hi, jnp.where(klow, zero, kl)], axis=0)
        vbd = jnp.concatenate([cv_lo, jnp.where(vlow, vl, zero),
                               cv_hi, jnp.where(vlow, zero, vl)], axis=0)
        kpos = j * tq + r0 - blk + pcol
        bias = band_bias + jnp.where((kpos >= 0) & (kpos < seq_len), 0.0, NEG_INF).astype(f32)
        sc = lax.dot_general(lhs, kbd, nt, preferred_element_type=f32)
        s_lo = jnp.concatenate([sc[:, :n_ctx], sc[:, n_ctx:nk] + bias], axis=1)
        s_hi = jnp.concatenate([sc[:, nk:nk + n_ctx], sc[:, nk + n_ctx:] + bias], axis=1)
        m_lo = jnp.maximum(row_max(s_lo), sink_lo)
        m_hi = jnp.maximum(row_max(s_hi), sink_hi)
        e = jnp.concatenate([jnp.exp2(s_lo - m_lo), jnp.exp2(s_hi - m_hi)], axis=1).astype(bf16)
        ov = jnp.dot(e, vbd, preferred_element_type=f32)
        den = ov[:, LANES:] + jnp.exp2(sink2 - jnp.where(low, m_lo, m_hi))
        o = (ov[:, :LANES] / den).astype(bf16)
        for c in range(n_chunks):
            o_ref[0, pl.ds(r0, blk), c * LANES:(c + 1) * LANES] = o[c * blk:(c + 1) * blk]
        return carry

    lax.fori_loop(0, tq // blk, sub, 0, unroll=8)


def _attention(q, k, v, ck, cv, sinkcol):
    b, l, qw = q.shape
    c = ck.shape[1]
    tq = ATT_TILE
    r = tq // ATT_BLOCK
    nb = l // ATT_BLOCK
    prev = pl.BlockSpec((1, ATT_BLOCK, LANES), lambda i, j: (i, jnp.maximum(j * r - 1, 0), 0))
    main = pl.BlockSpec((1, tq, LANES), lambda i, j: (i, j, 0))
    nxt = pl.BlockSpec((1, ATT_BLOCK, LANES), lambda i, j: (i, jnp.minimum(j * r + r, nb - 1), 0))
    cspec = pl.BlockSpec((1, c, LANES), lambda i, j: (i, 0, 0))
    return pl.pallas_call(
        functools.partial(_attn_kernel, l),
        grid=(b, l // tq),
        in_specs=[pl.BlockSpec((1, tq, qw), lambda i, j: (i, j, 0)),
                  prev, main, nxt, prev, main, nxt, cspec, cspec,
                  pl.BlockSpec(sinkcol.shape, lambda i, j: (0, 0))],
        out_specs=pl.BlockSpec((1, tq, qw), lambda i, j: (i, j, 0)),
        out_shape=jax.ShapeDtypeStruct((b, l, qw), bf16),
        scratch_shapes=[pltpu.VMEM((tq + 2 * ATT_BLOCK, LANES), bf16),
                        pltpu.VMEM((tq + 2 * ATT_BLOCK, 2 * LANES), bf16)],
        compiler_params=_cparams(("parallel", "parallel")),
        name="attention",
    )(q, k, k, k, v, v, v, ck, cv, sinkcol)


def _pack_pair(lo, hi):
    lo = lax.bitcast_convert_type(lo.astype(bf16).astype(f32), u32)
    hi = lax.bitcast_convert_type(hi.astype(bf16).astype(f32), u32)
    return (lo >> 16) | (hi & jnp.uint32(0xFFFF0000))


def _fourier1_kernel(x_ref, mod_ref, g_ref, w_ref, cs_ref, m_ref, ct_ref, st_ref, z_ref, ab_ref):
    n1 = x_ref.shape[1]
    nt = x_ref.shape[2]
    x = x_ref[0].reshape(n1 * nt, x_ref.shape[3])
    h = _norm_mod(x, g_ref[...], mod_ref[0, 0:1, :], mod_ref[0, 1:2, :]).astype(bf16)
    p = jnp.dot(h, w_ref[...], preferred_element_type=f32)
    for g in range(FOURIER_GROUPS):
        ug = p[:, g * LANES:(g + 1) * LANES].astype(bf16)
        ab = jnp.dot(ug, cs_ref[...], preferred_element_type=f32)
        ab_ref[0] = ab[:, :LANES]
        ab_ref[1] = ab[:, LANES:]
        for t0 in range(0, nt, 2):
            stack = jnp.concatenate(
                [jnp.concatenate([ab_ref[0, pl.ds(t, n1, stride=nt), :],
                                  ab_ref[1, pl.ds(t, n1, stride=nt), :]], axis=0)
                 for t in (t0, t0 + 1)], axis=1).astype(bf16)
            z2 = jnp.dot(m_ref[...], stack, preferred_element_type=f32)
            for t in (t0, t0 + 1):
                z = z2[:, (t - t0) * LANES:(t - t0 + 1) * LANES]
                zr, zn = z[:n1], z[n1:]
                ct, st = ct_ref[t], st_ref[t]
                z_ref[0, g, t] = _pack_pair(ct * zr - st * zn, ct * zn + st * zr)


def _fourier2_kernel(scale, z_ref, m_ref, o_ref, zbuf, ybuf):
    _, grp, n2, tk, w = z_ref.shape
    for gp in range(grp // 2):
        zbuf[0] = z_ref[0, 2 * gp].reshape(n2 * tk, w)
        zbuf[1] = z_ref[0, 2 * gp + 1].reshape(n2 * tk, w)
        for j in range(tk):
            ys = []
            for h in range(2):
                zp = zbuf[h, pl.ds(j, n2, stride=tk), :]
                zr = lax.bitcast_convert_type(zp << 16, f32).astype(bf16)
                zn = lax.bitcast_convert_type(zp & jnp.uint32(0xFFFF0000), f32).astype(bf16)
                ys.append(jnp.dot(m_ref[...], jnp.concatenate([zr, zn], axis=0),
                                  preferred_element_type=f32) * scale)
            ybuf[pl.ds(j, n2, stride=tk), :] = _pack_pair(ys[0], ys[1])
        o_ref[0, gp] = ybuf[...].reshape(n2, tk, w)


def _fourier(x, mod, g, w_f, cs):
    b, l, d = x.shape
    n2 = DFT_INNER
    n1 = l // n2
    grp, w = FOURIER_GROUPS, FOURIER_GROUP_W
    t2 = SUBLANES
    k1 = jnp.arange(n1, dtype=i32)
    ang1 = ((k1[:, None] * k1[None, :]) % n1).astype(f32) * (2.0 * math.pi / n1)
    c1, s1 = jnp.cos(ang1), jnp.sin(ang1)
    m1 = jnp.concatenate([jnp.concatenate([c1, -s1], axis=1),
                          jnp.concatenate([s1, c1], axis=1)], axis=0).astype(bf16)
    l2 = jnp.arange(n2, dtype=i32)
    angt = ((l2[:, None] * k1[None, :]) % l).astype(f32) * (2.0 * math.pi / l)
    ct = jnp.broadcast_to(jnp.cos(angt)[:, :, None], (n2, n1, w))
    st = jnp.broadcast_to(jnp.sin(angt)[:, :, None], (n2, n1, w))
    ang2 = ((l2[:, None] * l2[None, :]) % n2).astype(f32) * (2.0 * math.pi / n2)
    m2 = jnp.concatenate([jnp.cos(ang2), -jnp.sin(ang2)], axis=1).astype(bf16)

    tspec = pl.BlockSpec((t2, n1, w), lambda t, i: (t, 0, 0))
    z = pl.pallas_call(
        _fourier1_kernel,
        grid=(n2 // t2, b),
        in_specs=[pl.BlockSpec((1, n1, t2, d), lambda t, i: (i, 0, t, 0)),
                  pl.BlockSpec((1, 6, d), lambda t, i: (i, 0, 0)),
                  pl.BlockSpec((1, d), lambda t, i: (0, 0)),
                  pl.BlockSpec(w_f.shape, lambda t, i: (0, 0)),
                  pl.BlockSpec(cs.shape, lambda t, i: (0, 0)),
                  pl.BlockSpec(m1.shape, lambda t, i: (0, 0)), tspec, tspec],
        out_specs=pl.BlockSpec((1, grp, t2, n1, w), lambda t, i: (i, 0, t, 0, 0)),
        out_shape=jax.ShapeDtypeStruct((b, grp, n2, n1, w), u32),
        scratch_shapes=[pltpu.VMEM((2, n1 * t2, w), f32)],
        compiler_params=_cparams(("parallel", "parallel")),
        name="fourier_outer",
    )(x.reshape(b, n1, n2, d), mod, g, w_f, cs, m1, ct, st)

    tk = 2 * SUBLANES
    y = pl.pallas_call(
        functools.partial(_fourier2_kernel, 1.0 / math.sqrt(l * w)),
        grid=(b, n1 // tk),
        in_specs=[pl.BlockSpec((1, grp, n2, tk, w), lambda i, t: (i, 0, 0, t, 0)),
                  pl.BlockSpec(m2.shape, lambda i, t: (0, 0))],
        out_specs=pl.BlockSpec((1, grp // 2, n2, tk, w), lambda i, t: (i, 0, 0, t, 0)),
        out_shape=jax.ShapeDtypeStruct((b, grp // 2, n2, n1, w), u32),
        scratch_shapes=[pltpu.VMEM((2, n2 * tk, w), u32), pltpu.VMEM((n2 * tk, w), u32)],
        compiler_params=_cparams(("parallel", "parallel")),
        name="fourier_inner",
    )(z, m2)
    return y.reshape(b, grp // 2, l, w)


def _first_max4(a):
    m = jnp.maximum(jnp.maximum(a[0], a[1]), jnp.maximum(a[2], a[3]))
    idx = jnp.where(a[0] == m, 0, jnp.where(a[1] == m, 1, jnp.where(a[2] == m, 2, 3)))
    return m, idx


def _pick4(vals, idx):
    return jnp.where(idx == 0, vals[0], jnp.where(idx == 1, vals[1],
                                                   jnp.where(idx == 2, vals[2], vals[3])))


def _route(f, rw_ref, rb_ref, tri_ref, base_ref, first_step, ri_ref, wc_ref, cnt_ref, meta_ref):
    rw2 = rw_ref[...]
    pieces = []
    for fp in (f if isinstance(f, (list, tuple)) else [f]):
        f_hi = fp.astype(bf16)
        f_lo = (fp - f_hi.astype(f32)).astype(bf16)
        part = jnp.dot(f_hi, rw2, preferred_element_type=f32)
        pieces.append(part[:, :LANES] + part[:, LANES:]
                      + jnp.dot(f_lo, rw2[:, :LANES], preferred_element_type=f32))
    logits = jnp.concatenate(pieces, axis=0)
    tm = logits.shape[0]
    sc = jax.nn.sigmoid(logits)
    st = sc.T
    bt = (sc + rb_ref[...]).T
    neg = jnp.full((1, tm), -jnp.inf, f32)
    gs = []
    for g in range(N_GROUPS):
        a = [bt[4 * g + i: 4 * g + i + 1] for i in range(4)]
        m1, i1 = _first_max4(a)
        rest = [jnp.where(i1 == i, neg, a[i]) for i in range(4)]
        m2, _ = _first_max4(rest)
        gs.append(m1 + m2)
    _, gsel = _first_max4(gs)
    a = [_pick4([bt[4 * g + i: 4 * g + i + 1] for g in range(N_GROUPS)], gsel) for i in range(4)]
    s = [_pick4([st[4 * g + i: 4 * g + i + 1] for g in range(N_GROUPS)], gsel) for i in range(4)]
    _, i1 = _first_max4(a)
    rest = [jnp.where(i1 == i, neg, a[i]) for i in range(4)]
    _, i2 = _first_max4(rest)
    w1 = _pick4(s, i1)
    w2 = _pick4(s, i2)
    tot = w1 + w2
    w1 = w1 / tot
    w2 = w2 / tot
    e0 = gsel * EXPERTS_PER_GROUP + i1
    e1 = gsel * EXPERTS_PER_GROUP + i2

    @pl.when(first_step)
    def _():
        base_ref[...] = jnp.zeros_like(base_ref)

    td = DISPATCH_TILE
    eid = lax.broadcasted_iota(i32, (N_EXPERTS, tm), 0)
    oh0 = (eid == e0).astype(f32)
    oh1 = (eid == e1).astype(f32)
    oh = oh0 + oh1
    before = jnp.dot(oh.astype(bf16), tri_ref[...], preferred_element_type=f32)
    lane_tile = lax.broadcasted_iota(i32, (N_EXPERTS, tm), 1) // td
    ei = lax.broadcasted_iota(i32, (N_EXPERTS, N_EXPERTS), 0)
    ej = lax.broadcasted_iota(i32, (N_EXPERTS, N_EXPERTS), 1)
    strict_lower = (ej < ei).astype(f32)
    run_start = jnp.zeros((N_EXPERTS, tm), f32)
    goff = base_ref[...]
    for s in range(tm // td):
        cnt_s = jnp.sum(oh[:, s * td:(s + 1) * td], axis=1, keepdims=True)
        pad_s = jnp.floor((cnt_s + 7.0) * 0.125) * 8.0
        pad_b = jnp.broadcast_to(pad_s, (N_EXPERTS, LANES))
        start_b = jnp.dot(strict_lower, pad_b, precision=HIGHEST, preferred_element_type=f32)
        run_start = jnp.where(lane_tile == s, start_b[:, 0:1], run_start)
        meta_ref[s, 0] = start_b.astype(i32)
        meta_ref[s, 1] = pad_b.astype(i32)
        meta_ref[s, 2] = goff.astype(i32)
        goff = goff + pad_b
    base_ref[...] = goff
    cnt_ref[...] = goff
    pos = before + run_start
    lp0 = jnp.sum(oh0 * pos, axis=0, keepdims=True)
    lp1 = jnp.sum(oh1 * pos, axis=0, keepdims=True)
    zi = jnp.zeros((1, tm), i32)
    ri_ref[...] = jnp.concatenate(
        [lp0.astype(i32), lp1.astype(i32), e0, e1,
         lax.bitcast_convert_type(w1, i32), lax.bitcast_convert_type(w2, i32), zi, zi], axis=0)
    zf = jnp.zeros((LANES - 4, tm), f32)
    wc_ref[...] = jnp.concatenate([w1, w2, lp0, lp1, zf], axis=0).T


def _outproj_kernel(yf_ref, o_ref, x_ref, mod_ref, w_ref, g_ref, rw_ref, rb_ref, tri_ref,
                    x1_ref, f_ref, ri_ref, wc_ref, cnt_ref, meta_ref, base_ref):
    tm = x_ref.shape[1]
    fs = []
    for r in range(0, tm, tm // 2):
        rs = slice(r, r + tm // 2)
        groups = []
        for gp in range(FOURIER_GROUPS // 2):
            pair = yf_ref[0, gp, rs, :]
            groups.append(lax.bitcast_convert_type(pair << 16, f32).astype(bf16))
            groups.append(lax.bitcast_convert_type(pair & jnp.uint32(0xFFFF0000), f32).astype(bf16))
        mix = jnp.concatenate(groups + [o_ref[0, rs, :]], axis=1)
        y = jnp.dot(mix, w_ref[...], preferred_element_type=f32)
        x1 = x_ref[0, rs, :] + mod_ref[0, 2:3, :] * y
        x1_ref[0, rs, :] = x1
        f = _norm_mod(x1, g_ref[...], mod_ref[0, 3:4, :], mod_ref[0, 4:5, :])
        f_ref[0, rs, :] = f.astype(bf16)
        fs.append(f)
    first = (pl.program_id(0) == 0) & (pl.program_id(1) == 0)
    _route(fs, rw_ref, rb_ref, tri_ref, base_ref, first, ri_ref, wc_ref, cnt_ref, meta_ref)


def _before_in_tile(tm):
    tpos = jnp.arange(tm)
    return ((tpos[:, None] < tpos[None, :])
            & (tpos[:, None] // DISPATCH_TILE == tpos[None, :] // DISPATCH_TILE)).astype(bf16)


def _route_specs(b, l, tm):
    nl = l // tm
    rw = lambda d: pl.BlockSpec((d, 2 * LANES), lambda i, j: (0, 0))
    rb = pl.BlockSpec((1, LANES), lambda i, j: (0, 0))
    tri = pl.BlockSpec((tm, tm), lambda i, j: (0, 0))
    ns = tm // DISPATCH_TILE
    out_specs = [pl.BlockSpec((8, tm), lambda i, j: (0, i * nl + j)),
                 pl.BlockSpec((tm, LANES), lambda i, j: (i * nl + j, 0)),
                 pl.BlockSpec((N_EXPERTS, LANES), lambda i, j: (0, 0)),
                 pl.BlockSpec((ns, 3, N_EXPERTS, LANES), lambda i, j: (i * nl + j, 0, 0, 0))]
    out_shape = [jax.ShapeDtypeStruct((8, b * l), i32),
                 jax.ShapeDtypeStruct((b * l, LANES), f32),
                 jax.ShapeDtypeStruct((N_EXPERTS, LANES), f32),
                 jax.ShapeDtypeStruct((b * l // DISPATCH_TILE, 3, N_EXPERTS, LANES), i32)]
    return rw, rb, tri, out_specs, out_shape


def _outproj(yf, o, x, mod, w, g, rw, rb):
    b, l, d = x.shape
    tm = PROJ_TILE
    tri = _before_in_tile(tm)
    rws, rbs, tris, r_specs, r_shapes = _route_specs(b, l, tm)
    row = pl.BlockSpec((1, tm, d), lambda i, j: (i, j, 0))
    return pl.pallas_call(
        _outproj_kernel,
        grid=(b, l // tm),
        in_specs=[pl.BlockSpec((1, yf.shape[1], tm, LANES), lambda i, j: (i, 0, j, 0)),
                  pl.BlockSpec((1, tm, o.shape[2]), lambda i, j: (i, j, 0)),
                  row,
                  pl.BlockSpec((1, 6, d), lambda i, j: (i, 0, 0)),
                  pl.BlockSpec(w.shape, lambda i, j: (0, 0)),
                  pl.BlockSpec((1, d), lambda i, j: (0, 0)),
                  rws(d), rbs, tris],
        out_specs=[row, row] + r_specs,
        out_shape=[jax.ShapeDtypeStruct((b, l, d), f32),
                   jax.ShapeDtypeStruct((b, l, d), bf16)] + r_shapes,
        scratch_shapes=[pltpu.VMEM((N_EXPERTS, LANES), f32)],
        compiler_params=_cparams(("arbitrary", "arbitrary")),
        name="outproj_router",
    )(yf, o, x, mod, w, g, rw, rb, tri)


def _conv_kernel(seq_len, up_ref, um_ref, un_ref, x_ref, mod_ref, dw_ref, db_ref, lg_ref, lb_ref,
                 w_ref, pb_ref, g_ref, rw_ref, rb_ref, tri_ref,
                 x1_ref, f_ref, ri_ref, wc_ref, cnt_ref, meta_ref, base_ref, ext, conv_out):
    j = pl.program_id(1)
    tm = um_ref.shape[1]
    hl = CONV_HALO
    half = CONV_W // 2
    prev = jnp.where(j > 0, up_ref[0], jnp.zeros_like(up_ref[0]))
    nxt = jnp.where((j + 1) * tm < seq_len, un_ref[0], jnp.zeros_like(un_ref[0]))
    for c in range(ext.shape[0]):
        lanes_c = slice(c * LANES, (c + 1) * LANES)
        ext[c, 0:hl] = prev[:, lanes_c]
        ext[c, hl:hl + tm] = um_ref[0, :, lanes_c]
        ext[c, hl + tm:] = nxt[:, lanes_c]
    base = hl - half
    span = (CONV_W - 1) // SUBLANES * SUBLANES
    rows = CONV_ROWS

    def lane_chunk(c, carry):
        lanes = pl.ds(pl.multiple_of(c * LANES, LANES), LANES)
        for r in range(0, tm, rows):
            part = jnp.broadcast_to(db_ref[:, lanes], (rows, LANES))
            for phase in range(SUBLANES):
                win = ext[c, base + phase + r: base + phase + r + rows + span, :]
                same = None
                for t in range(phase, CONV_W, SUBLANES):
                    term = win[t - phase: t - phase + rows, :] * dw_ref[t:t + 1, lanes]
                    same = term if same is None else same + term
                part = part + same
            conv_out[r:r + rows, lanes] = part
        return carry

    lax.fori_loop(0, um_ref.shape[2] // LANES, lane_chunk, 0)
    fs = []
    for r in range(0, tm, tm // 2):
        rs = slice(r, r + tm // 2)
        acc = conv_out[rs, :]
        mu = jnp.mean(acc, axis=-1, keepdims=True)
        cen = acc - mu
        var = jnp.mean(cen * cen, axis=-1, keepdims=True)
        ln = cen * lax.rsqrt(var + EPS) * lg_ref[...] + lb_ref[...]
        act = ln * jax.nn.sigmoid(ln)
        y = jnp.dot(act.astype(bf16), w_ref[...], preferred_element_type=f32) + pb_ref[...]
        x1 = x_ref[0, rs, :] + mod_ref[0, 2:3, :] * y
        x1_ref[0, rs, :] = x1
        f = _norm_mod(x1, g_ref[...], mod_ref[0, 3:4, :], mod_ref[0, 4:5, :])
        f_ref[0, rs, :] = f.astype(bf16)
        fs.append(f)
    first = (pl.program_id(0) == 0) & (j == 0)
    _route(fs, rw_ref, rb_ref, tri_ref, base_ref, first, ri_ref, wc_ref, cnt_ref, meta_ref)


def _conv(u, x, mod, dw_w, dw_b, ln_g, ln_b, pw2_w, pw2_b, g, rw, rb):
    b, l, d = x.shape
    tm = TOKEN_TILE
    tri = _before_in_tile(tm)
    hl = CONV_HALO
    r = tm // hl
    nh = l // hl
    rws, rbs, tris, r_specs, r_shapes = _route_specs(b, l, tm)
    row = pl.BlockSpec((1, tm, d), lambda i, j: (i, j, 0))
    vec = pl.BlockSpec((1, d), lambda i, j: (0, 0))
    return pl.pallas_call(
        functools.partial(_conv_kernel, l),
        grid=(b, l // tm),
        in_specs=[pl.BlockSpec((1, hl, d), lambda i, j: (i, jnp.maximum(j * r - 1, 0), 0)),
                  row,
                  pl.BlockSpec((1, hl, d), lambda i, j: (i, jnp.minimum(j * r + r, nh - 1), 0)),
                  row,
                  pl.BlockSpec((1, 6, d), lambda i, j: (i, 0, 0)),
                  pl.BlockSpec(dw_w.shape, lambda i, j: (0, 0)),
                  vec, vec, vec,
                  pl.BlockSpec(pw2_w.shape, lambda i, j: (0, 0)),
                  vec, vec, rws(d), rbs, tris],
        out_specs=[row, row] + r_specs,
        out_shape=[jax.ShapeDtypeStruct((b, l, d), f32),
                   jax.ShapeDtypeStruct((b, l, d), bf16)] + r_shapes,
        scratch_shapes=[pltpu.VMEM((N_EXPERTS, LANES), f32),
                        pltpu.VMEM((d // LANES, tm + 2 * hl, LANES), f32),
                        pltpu.VMEM((tm, d), f32)],
        compiler_params=_cparams(("arbitrary", "arbitrary")),
        name="conv_router",
    )(u, u, u, x, mod, dw_w, dw_b, ln_g, ln_b, pw2_w, pw2_b, g, rw, rb, tri)


def _pack_bf16_pairs(x):
    h = x.shape[1] // 2
    lo = lax.bitcast_convert_type(x[:, :h], u32)
    hi = lax.bitcast_convert_type(x[:, h:], u32)
    return (lo >> 16) | (hi & jnp.uint32(0xFFFF0000))


def _unpack_bf16_pairs(u):
    lo = lax.bitcast_convert_type(u << 16, f32)
    hi = lax.bitcast_convert_type(u & jnp.uint32(0xFFFF0000), f32)
    return jnp.concatenate([lo, hi], axis=1).astype(bf16)


def _run_copies(meta, tile, local_ref, hbm_ref, sem, to_hbm):
    start_ref, size_ref, dst_ref = meta
    for e in range(N_EXPERTS):
        k = tile * N_EXPERTS + e
        size = pl.multiple_of(size_ref[k], RUN_ALIGN)

        @pl.when(size > 0)
        def _():
            loc = local_ref.at[pl.ds(pl.multiple_of(start_ref[k], RUN_ALIGN), size)]
            glob = hbm_ref.at[pl.ds(pl.multiple_of(dst_ref[k], RUN_ALIGN), size)]
            if to_hbm:
                pltpu.make_async_copy(loc, glob, sem).start()
            else:
                pltpu.make_async_copy(glob, loc, sem).start()


def _wait_rows(rows, local_ref, hbm_ref, sem):
    rows = pl.multiple_of(rows, RUN_ALIGN)

    @pl.when(rows > 0)
    def _():
        pltpu.make_async_copy(local_ref.at[pl.ds(0, rows)], hbm_ref.at[pl.ds(0, rows)], sem).wait()


def _dispatch_kernel(start_ref, size_ref, dst_ref, tot_ref, tail_start_ref, tail_size_ref, nv_ref,
                     f_ref, lp_ref, xs_ref, loc, zbuf, sem, zsem):
    i = pl.program_id(0)
    n = pl.num_programs(0)
    slot = i % 2
    subs = loc.shape[1]
    rows = loc.shape[2]
    td = f_ref.shape[0] // subs
    meta = (start_ref, size_ref, dst_ref)

    def drain(step, which):
        for s in range(subs):
            _wait_rows(tot_ref[step * subs + s], loc.at[which, s], xs_ref, sem.at[which])

    @pl.when(i >= 2)
    def _():
        drain(i - 2, slot)

    r = lax.broadcasted_iota(i32, (rows, td), 0)
    for s in range(subs):
        cols = slice(s * td, (s + 1) * td)
        pick0 = r == lp_ref[0:1, cols]
        pick1 = r == lp_ref[1:2, cols]
        onehot = (pick0 | pick1).astype(bf16)
        sorted_rows = jnp.dot(onehot, f_ref[cols, :], preferred_element_type=f32)
        half = sorted_rows.shape[1] // 2
        loc[slot, s, :, :half] = _pack_bf16_pairs(sorted_rows)
        w0 = lax.bitcast_convert_type(lp_ref[4:5, cols], f32)
        w1 = lax.bitcast_convert_type(lp_ref[5:6, cols], f32)
        row_w = jnp.sum(jnp.where(pick0, w0, 0.0) + jnp.where(pick1, w1, 0.0), axis=1, keepdims=True)
        loc[slot, s, :, half:] = jnp.broadcast_to(lax.bitcast_convert_type(row_w, u32), (rows, LANES))
        _run_copies(meta, i * subs + s, loc.at[slot, s], xs_ref, sem.at[slot], to_hbm=True)

    @pl.when(i == n - 1)
    def _():
        zbuf[...] = jnp.zeros_like(zbuf)
        total = 0
        for e in range(N_EXPERTS):
            size = pl.multiple_of(tail_size_ref[e], RUN_ALIGN)
            total = total + size

            @pl.when(size > 0)
            def _():
                pltpu.make_async_copy(
                    zbuf.at[pl.ds(0, size)],
                    xs_ref.at[pl.ds(pl.multiple_of(tail_start_ref[e], RUN_ALIGN), size)], zsem).start()

        _wait_rows(total, zbuf, xs_ref, zsem)

        def zero_block(k, c):
            pltpu.make_async_copy(zbuf, xs_ref.at[pl.ds(pl.multiple_of(k * zbuf.shape[0], RUN_ALIGN),
                                                        zbuf.shape[0])], zsem).start()
            return c

        def wait_block(k, c):
            pltpu.make_async_copy(zbuf, xs_ref.at[pl.ds(0, zbuf.shape[0])], zsem).wait()
            return c

        n_blocks = xs_ref.shape[0] // zbuf.shape[0]
        lax.fori_loop(nv_ref[0], n_blocks, zero_block, 0)
        lax.fori_loop(nv_ref[0], n_blocks, wait_block, 0)
        drain(i, slot)

        @pl.when(i >= 1)
        def _():
            drain(i - 1, 1 - slot)


def _dispatch(tables, f2, ri, n_slots):
    t, d = f2.shape
    subs = DISPATCH_SUBTILES
    tm = DISPATCH_TILE * subs
    return pl.pallas_call(
        _dispatch_kernel,
        grid_spec=pltpu.PrefetchScalarGridSpec(
            num_scalar_prefetch=7,
            grid=(t // tm,),
            in_specs=[pl.BlockSpec((tm, d), lambda i, *_: (i, 0)),
                      pl.BlockSpec((8, tm), lambda i, *_: (0, i))],
            out_specs=pl.BlockSpec(memory_space=pl.ANY),
            scratch_shapes=[pltpu.VMEM((2, subs, LOCAL_ROWS, d // 2 + LANES), u32),
                            pltpu.VMEM((EXPERT_ROWS, d // 2 + LANES), u32),
                            pltpu.SemaphoreType.DMA((2,)), pltpu.SemaphoreType.DMA(())]),
        out_shape=jax.ShapeDtypeStruct((n_slots, d // 2 + LANES), u32),
        compiler_params=_cparams(("arbitrary",)),
        name="moe_dispatch",
    )(*tables, f2, ri)


def _expert_kernel(be_ref, second_ref, nv_ref, x_ref, wg1, wu1, wd1, wg2, wu2, wd2, y_ref, *w_bf):
    del second_ref
    i = pl.program_id(0)
    tb = y_ref.shape[0] // 2
    half = y_ref.shape[1]
    e1, e2 = be_ref[2 * i], be_ref[2 * i + 1]
    first, second = w_bf[:3], w_bf[3:]

    @pl.when(jnp.logical_or(i == 0, e1 != be_ref[jnp.maximum(2 * i - 2, 0)]))
    def _():
        for dst, src in zip(first, (wg1, wu1, wd1)):
            dst[...] = src[0, 0].astype(bf16)

    @pl.when(e2 != e1)
    def _():
        for dst, src in zip(second, (wg2, wu2, wd2)):
            dst[...] = src[0, 0].astype(bf16)

    def block(k, weights):
        wgb, wub, wdb = weights
        rows = slice(k * tb, (k + 1) * tb)
        xb = _unpack_bf16_pairs(x_ref[rows, :half])
        row_w = lax.bitcast_convert_type(x_ref[rows, half:], f32)
        gate = jnp.dot(xb, wgb[...], preferred_element_type=f32)
        up = jnp.dot(xb, wub[...], preferred_element_type=f32)
        hid = (gate * jax.nn.sigmoid(gate) * up).astype(bf16)
        y = jnp.dot(hid, wdb[...], preferred_element_type=f32)
        y = jnp.concatenate([y[:, c:c + LANES] * row_w for c in range(0, y.shape[1], LANES)], axis=1)
        y_ref[rows, :] = _pack_bf16_pairs(y.astype(bf16).astype(f32))

    used = 2 * i < nv_ref[0]

    @pl.when(used & (e2 == e1))
    def _():
        block(0, first)
        block(1, first)

    @pl.when(used & (e2 != e1))
    def _():
        block(0, first)
        block(1, second)

    @pl.when(jnp.logical_not(used))
    def _():
        y_ref[...] = jnp.zeros_like(y_ref)


def _experts(block_e, n_valid, xs, w_gate, w_up, w_down, layer):
    ns, xw = xs.shape
    nb = EXPERT_BLOCKS_PER_STEP
    tb = EXPERT_ROWS
    d, ff = w_gate.shape[2:]
    dh = d // 2
    e1, e2 = block_e[0::2], block_e[1::2]
    second = lax.cummax(jnp.where(e2 != e1, e2, 0))
    map1 = lambda i, be, sec, nv: (layer, be[2 * i], 0, 0)
    map2 = lambda i, be, sec, nv: (layer, sec[i], 0, 0)
    w_specs = [pl.BlockSpec((1, 1, d, ff), m) for m in (map1, map1)] + [pl.BlockSpec((1, 1, ff, d), map1)]
    w_specs += [pl.BlockSpec((1, 1, d, ff), m) for m in (map2, map2)] + [pl.BlockSpec((1, 1, ff, d), map2)]
    w_scratch = 2 * [pltpu.VMEM((d, ff), bf16), pltpu.VMEM((d, ff), bf16), pltpu.VMEM((ff, d), bf16)]
    return pl.pallas_call(
        _expert_kernel,
        grid_spec=pltpu.PrefetchScalarGridSpec(
            num_scalar_prefetch=3,
            grid=(ns // (nb * tb),),
            in_specs=[pl.BlockSpec((nb * tb, xw), lambda i, *_: (i, 0))] + w_specs,
            out_specs=pl.BlockSpec((nb * tb, dh), lambda i, *_: (i, 0)),
            scratch_shapes=w_scratch),
        out_shape=jax.ShapeDtypeStruct((ns, dh), u32),
        compiler_params=_cparams(("arbitrary",)),
        name="moe_experts",
    )(block_e, second, n_valid, xs, w_gate, w_up, w_down, w_gate, w_up, w_down)


def _combine_kernel(final, start_ref, size_ref, dst_ref, tot_ref, ys_ref, wc_ref, x_ref, mod_ref,
                    g_ref, *rest):
    if final:
        o_ref, loc, sem = rest
    else:
        nmod_ref, w_ref, b_ref, o_ref, u_ref, loc, sem = rest
    i = pl.program_id(0)
    n = pl.num_programs(0)
    slot = i % 2
    subs = loc.shape[1]
    rows = loc.shape[2]
    td = x_ref.shape[0] // subs
    meta = (start_ref, size_ref, dst_ref)

    def fetch(step, which):
        for s in range(subs):
            _run_copies(meta, step * subs + s, loc.at[which, s], ys_ref, sem.at[which, s],
                        to_hbm=False)

    @pl.when(i == 0)
    def _():
        loc[...] = jnp.zeros_like(loc)
        fetch(i, slot)

    @pl.when(i + 1 < n)
    def _():
        fetch(i + 1, 1 - slot)

    c = lax.broadcasted_iota(i32, (td, rows), 1)
    parts = []
    for s in range(subs):
        _wait_rows(tot_ref[i * subs + s], loc.at[slot, s], ys_ref, sem.at[slot, s])
        wc = wc_ref[s * td:(s + 1) * td, :]
        sel = ((c == wc[:, 2:3].astype(i32)) | (c == wc[:, 3:4].astype(i32))).astype(bf16)
        parts.append(jnp.dot(sel, _unpack_bf16_pairs(loc[slot, s]), preferred_element_type=f32))
    xo = x_ref[...] + mod_ref[0, 5:6, :] * jnp.concatenate(parts, axis=0)
    if final:
        ms = jnp.mean(xo * xo, axis=-1, keepdims=True)
        o_ref[...] = xo * lax.rsqrt(ms + EPS) * g_ref[...]
    else:
        o_ref[...] = xo
        h = _norm_mod(xo, g_ref[...], nmod_ref[0, 0:1, :], nmod_ref[0, 1:2, :])
        p = jnp.dot(h.astype(bf16), w_ref[...], preferred_element_type=f32) + b_ref[...]
        ch = p.shape[1] // 2
        u_ref[...] = p[:, :ch] * jax.nn.sigmoid(p[:, ch:])


def _combine(tables, ys, wc, x, mod, g, glu=None):
    b, l, d = x.shape
    subs = DISPATCH_SUBTILES if glu is None else GLU_SUBTILES
    tm = DISPATCH_TILE * subs
    per_batch = l // tm
    const = lambda i, *_: (0, 0)
    tile = lambda cols: pl.BlockSpec((tm, cols), lambda i, *_: (i, 0))
    mod_spec = pl.BlockSpec((1, 6, d), lambda i, *_: (i // per_batch, 0, 0))
    in_specs = [pl.BlockSpec(memory_space=pl.ANY), tile(LANES), tile(d), mod_spec,
                pl.BlockSpec((1, d), const)]
    args = [ys, wc, x.reshape(b * l, d), mod, g]
    out_specs = [tile(d)]
    out_shape = [jax.ShapeDtypeStruct((b * l, d), f32)]
    if glu is not None:
        nmod, w, bias = glu
        in_specs += [mod_spec, pl.BlockSpec(w.shape, const), pl.BlockSpec(bias.shape, const)]
        args += [nmod, w, bias]
        out_specs.append(tile(w.shape[1] // 2))
        out_shape.append(jax.ShapeDtypeStruct((b * l, w.shape[1] // 2), f32))
    outs = pl.pallas_call(
        functools.partial(_combine_kernel, glu is None),
        grid_spec=pltpu.PrefetchScalarGridSpec(
            num_scalar_prefetch=4,
            grid=(b * per_batch,),
            in_specs=in_specs,
            out_specs=out_specs,
            scratch_shapes=[pltpu.VMEM((2, subs, LOCAL_ROWS, d // 2), u32),
                            pltpu.SemaphoreType.DMA((2, subs))]),
        out_shape=out_shape,
        compiler_params=_cparams(("arbitrary",)),
        name="moe_combine",
    )(*tables, *args)
    return [o.reshape(b, l, -1) for o in outs]


def _moe(f, routed, x, mod, g, w_gate, w_up, w_down, layer, glu=None):
    ri, wc, cnt, meta = routed
    b, l, d = x.shape
    t = b * l
    tb = EXPERT_ROWS
    n_tiles = t // DISPATCH_TILE
    used = cnt[:, 0].astype(i32)
    region = (used + tb - 1) // tb * tb
    gend = jnp.cumsum(region)
    gstart = gend - region
    max_rows = 2 * t + n_tiles * N_EXPERTS * (RUN_ALIGN - 1) + N_EXPERTS * (tb - 1)
    n_blocks = -(-max_rows // (tb * EXPERT_BLOCKS_PER_STEP)) * EXPERT_BLOCKS_PER_STEP
    m = meta[:, :, :, 0]
    run_start = m[:, 0].reshape(-1)
    run_size = m[:, 1].reshape(-1)
    run_dst = (m[:, 2] + gstart[None, :]).reshape(-1)
    tile_rows = jnp.sum(m[:, 1], axis=1)
    block_row = jnp.arange(n_blocks, dtype=i32) * tb
    block_e = jnp.minimum(jnp.sum((block_row[:, None] >= gend[None, :]).astype(i32), axis=1),
                          N_EXPERTS - 1)
    n_valid = (gend[-1] // tb).reshape(1)
    xs = _dispatch((run_start, run_size, run_dst, tile_rows, gstart + used, region - used, n_valid),
                   f.reshape(t, d), ri, n_blocks * tb)
    ys = _experts(block_e, n_valid, xs, w_gate, w_up, w_down, layer)
    return _combine((run_start, run_size, run_dst, tile_rows), ys, wc, x, mod, g, glu)


def _rope_tables(l):
    lane = jnp.arange(LANES)
    dh = lane % HEAD_DIM
    inv = ROPE_THETA ** (-(dh % 16).astype(f32) / 16.0)
    sign = jnp.where((dh % 32) < 16, -1.0, 1.0).astype(f32)
    by_row = (dh // 32)[None, None, :] == 0
    ang_r = jnp.arange(l // GRID_W, dtype=f32)[:, None] * inv[None, :]
    ang_c = jnp.arange(GRID_W, dtype=f32)[:, None] * inv[None, :]
    cos = jnp.where(by_row, jnp.cos(ang_r)[:, None, :], jnp.cos(ang_c)[None, :, :])
    sin = jnp.where(by_row, jnp.sin(ang_r)[:, None, :], jnp.sin(ang_c)[None, :, :])
    return cos.reshape(l, LANES), (sin * sign[None, None, :]).reshape(l, LANES)


def kernel(x, c, ctx, c_ctx, ada_w, ada_b, norm_mix_g, norm_ffn_g, even_w_in, even_w_out, even_sink, conv_pw1_w, conv_pw1_b, conv_dw_w, conv_dw_b, conv_ln_g, conv_ln_b, conv_pw2_w, conv_pw2_b, router_w, router_b, moe_w_gate, moe_w_up, moe_w_down, final_norm_g):
    b, l, d = x.shape
    depth = ada_w.shape[0]
    assert depth == 2 and b < COND_ROWS
    ctx_row = b
    cond = jnp.zeros((COND_ROWS, d), f32).at[:b].set(c).at[ctx_row].set(c_ctx)
    mods = _adaln(cond, ada_w, ada_b).reshape(depth, COND_ROWS, 6, d)

    heads = jnp.arange(N_HEADS).reshape(N_KV_HEADS, N_HEADS // N_KV_HEADS).T.reshape(-1)
    qperm = (heads[:, None] * HEAD_DIM + jnp.arange(HEAD_DIM)[None, :]).reshape(-1)
    fw = FOURIER_GROUPS * FOURIER_GROUP_W
    qw = N_HEADS * HEAD_DIM
    w_in = even_w_in[0]
    w_in_p = jnp.concatenate([w_in[:, :fw], w_in[:, fw:fw + qw][:, qperm], w_in[:, fw + qw:]],
                             axis=1).astype(bf16)
    w_out = even_w_out[0]
    w_out_p = jnp.concatenate([w_out[:fw], w_out[fw:][qperm]], axis=0).astype(bf16)
    sink_pairs = (even_sink[0].astype(f32) * LOG2E).reshape(N_KV_HEADS, N_HEADS // N_KV_HEADS).T
    sinkcol = jnp.repeat(jnp.repeat(sink_pairs, HEAD_DIM, axis=1), ATT_BLOCK, axis=0)

    cidx = jnp.arange(FOURIER_GROUP_W, dtype=i32)
    angc = ((cidx[:, None] * cidx[None, :]) % FOURIER_GROUP_W).astype(f32) * (2.0 * math.pi / FOURIER_GROUP_W)
    cs = jnp.concatenate([jnp.cos(angc), jnp.sin(angc)], axis=1).astype(bf16)
    cos_t, sin_t = _rope_tables(l)

    rw32 = jnp.zeros((d, LANES), f32).at[:, :N_EXPERTS].set(router_w.astype(f32))
    rw_hi = rw32.astype(bf16)
    rw = jnp.concatenate([rw_hi, (rw32 - rw_hi.astype(f32)).astype(bf16)], axis=1)
    rb = jnp.zeros((1, LANES), f32).at[0, :N_EXPERTS].set(router_b)
    row = lambda v: v.reshape(1, -1)

    q, k, v = _inproj(x, mods[0], row(norm_mix_g[0]), w_in_p[:, fw:], cos_t, sin_t)
    ck, cv = _ctxkv(ctx, mods[0], row(norm_mix_g[0]), w_in_p[:, fw + qw:], ctx_row)
    yf = _fourier(x, mods[0], row(norm_mix_g[0]), w_in_p[:, :fw], cs)
    att = _attention(q, k, v, ck, cv, sinkcol)
    x1, f, *routed = _outproj(yf, att, x, mods[0], w_out_p, row(norm_ffn_g[0]), rw, rb)
    x2, u = _moe(f, routed, x1, mods[0], row(norm_mix_g[1]), moe_w_gate, moe_w_up, moe_w_down,
                 layer=0, glu=(mods[1], conv_pw1_w[0].astype(bf16), row(conv_pw1_b[0])))

    x3, f, *routed = _conv(u, x2, mods[1], conv_dw_w[0], row(conv_dw_b[0]), row(conv_ln_g[0]),
                           row(conv_ln_b[0]), conv_pw2_w[0].astype(bf16), row(conv_pw2_b[0]),
                           row(norm_ffn_g[1]), rw, rb)
    (out,) = _moe(f, routed, x3, mods[1], row(final_norm_g), moe_w_gate, moe_w_up, moe_w_down,
                  layer=1)
    return out
```

```python
import functools
import math

import jax
import jax.numpy as jnp
from jax import lax
from jax.experimental import pallas as pl
from jax.experimental.pallas import tpu as pltpu

f32 = jnp.float32
bf16 = jnp.bfloat16
i32 = jnp.int32
u32 = jnp.uint32
HIGHEST = lax.Precision.HIGHEST

GRID_W = 64
HEAD_DIM = 64
N_HEADS = 8
N_KV_HEADS = 2
WINDOW = 128
ATT_BLOCK = 128
ROPE_THETA = 10000.0
FOURIER_GROUPS = 4
FOURIER_GROUP_W = 128
CONV_W = 31
N_EXPERTS = 16
N_GROUPS = 4
EXPERTS_PER_GROUP = 4
EPS = 1e-6
NEG_INF = -1e30
LOG2E = math.log2(math.e)

LANES = 128
SUBLANES = 8
COND_ROWS = 8
DFT_INNER = 64
TOKEN_TILE = 512
PROJ_TILE = 1024
ATT_TILE = 1024
EXPERT_ROWS = 512
EXPERT_BLOCKS_PER_STEP = 2
DISPATCH_TILE = 256
DISPATCH_SUBTILES = 4
GLU_SUBTILES = 2
RUN_ALIGN = 8
LOCAL_ROWS = -(-(2 * DISPATCH_TILE + N_EXPERTS * (RUN_ALIGN - 1)) // LANES) * LANES
CONV_HALO = 16
CONV_ROWS = 128
VMEM_LIMIT = 56 * 1024 * 1024


def _cparams(sem, vmem=VMEM_LIMIT):
    return pltpu.CompilerParams(dimension_semantics=sem, vmem_limit_bytes=vmem)


def _adaln_kernel(cond_ref, w_ref, b_ref, o_ref):
    s = cond_ref[...]
    s = s * jax.nn.sigmoid(s)
    w = w_ref[0]
    s_hi, w_hi = s.astype(bf16), w.astype(bf16)
    s_lo = (s - s_hi.astype(f32)).astype(bf16)
    w_lo = (w - w_hi.astype(f32)).astype(bf16)
    dot = functools.partial(jnp.dot, preferred_element_type=f32)
    o_ref[0] = dot(s_hi, w_hi) + dot(s_hi, w_lo) + dot(s_lo, w_hi) + b_ref[0]


def _adaln(cond, ada_w, ada_b):
    depth, d, n = ada_w.shape
    tn = 3072
    return pl.pallas_call(
        _adaln_kernel,
        grid=(depth, n // tn),
        in_specs=[pl.BlockSpec((COND_ROWS, d), lambda i, j: (0, 0)),
                  pl.BlockSpec((1, d, tn), lambda i, j: (i, 0, j)),
                  pl.BlockSpec((1, 1, tn), lambda i, j: (i, 0, j))],
        out_specs=pl.BlockSpec((1, COND_ROWS, tn), lambda i, j: (i, 0, j)),
        out_shape=jax.ShapeDtypeStruct((depth, COND_ROWS, n), f32),
        compiler_params=_cparams(("arbitrary", "arbitrary")),
        name="adaln",
    )(cond, ada_w, ada_b.reshape(depth, 1, n))


def _norm_mod(x, g, shift, scale):
    ms = jnp.mean(x * x, axis=-1, keepdims=True)
    return (x * lax.rsqrt(ms + EPS)) * (g * (1.0 + scale)) + shift


def _rope(p, cos, sin_signed, first_half):
    rot = jnp.where(first_half, pltpu.roll(p, LANES - 16, axis=1), pltpu.roll(p, 16, axis=1))
    return p * cos + rot * sin_signed


def _inproj_kernel(x_ref, mod_ref, g_ref, w_ref, cos_ref, sin_ref, q_ref, k_ref, v_ref):
    qw = N_HEADS * HEAD_DIM
    tm = x_ref.shape[1]
    for r in range(0, tm, tm // 2):
        rs = slice(r, r + tm // 2)
        h = _norm_mod(x_ref[0, rs, :], g_ref[...], mod_ref[0, 0:1, :], mod_ref[0, 1:2, :])
        p = jnp.dot(h.astype(bf16), w_ref[...], preferred_element_type=f32)
        cos = cos_ref[rs, :]
        sin = sin_ref[rs, :]
        lane = lax.broadcasted_iota(i32, cos.shape, 1)
        first_half = (lane % 32) < 16
        for c in range(qw // LANES):
            qc = p[:, c * LANES:(c + 1) * LANES]
            q_ref[0, rs, c * LANES:(c + 1) * LANES] = (
                _rope(qc, cos, sin, first_half) * (LOG2E * HEAD_DIM ** -0.5)).astype(bf16)
        k_ref[0, rs, :] = _rope(p[:, qw:qw + LANES], cos, sin, first_half).astype(bf16)
        v_ref[0, rs, :] = p[:, qw + LANES:].astype(bf16)


def _inproj(x, mod, g, w, cos_t, sin_t):
    b, l, d = x.shape
    tm = PROJ_TILE
    n = w.shape[1]
    return pl.pallas_call(
        _inproj_kernel,
        grid=(b, l // tm),
        in_specs=[pl.BlockSpec((1, tm, d), lambda i, j: (i, j, 0)),
                  pl.BlockSpec((1, 6, d), lambda i, j: (i, 0, 0)),
                  pl.BlockSpec((1, d), lambda i, j: (0, 0)),
                  pl.BlockSpec((d, n), lambda i, j: (0, 0)),
                  pl.BlockSpec((tm, LANES), lambda i, j: (j, 0)),
                  pl.BlockSpec((tm, LANES), lambda i, j: (j, 0))],
        out_specs=[pl.BlockSpec((1, tm, N_HEADS * HEAD_DIM), lambda i, j: (i, j, 0)),
                   pl.BlockSpec((1, tm, LANES), lambda i, j: (i, j, 0)),
                   pl.BlockSpec((1, tm, LANES), lambda i, j: (i, j, 0))],
        out_shape=[jax.ShapeDtypeStruct((b, l, N_HEADS * HEAD_DIM), bf16),
                   jax.ShapeDtypeStruct((b, l, LANES), bf16),
                   jax.ShapeDtypeStruct((b, l, LANES), bf16)],
        compiler_params=_cparams(("parallel", "parallel")),
        name="inproj",
    )(x, mod, g, w, cos_t, sin_t)


def _ctxkv_kernel(x_ref, mod_ref, g_ref, w_ref, k_ref, v_ref):
    h = _norm_mod(x_ref[0], g_ref[...], mod_ref[0, 0:1, :], mod_ref[0, 1:2, :])
    p = jnp.dot(h.astype(bf16), w_ref[...], preferred_element_type=f32)
    k_ref[0] = p[:, :LANES].astype(bf16)
    v_ref[0] = p[:, LANES:].astype(bf16)


def _ctxkv(ctx, mod, g, w_kv, ctx_row):
    b, c, d = ctx.shape
    return pl.pallas_call(
        _ctxkv_kernel,
        grid=(b,),
        in_specs=[pl.BlockSpec((1, c, d), lambda i: (i, 0, 0)),
                  pl.BlockSpec((1, 6, d), lambda i: (ctx_row, 0, 0)),
                  pl.BlockSpec((1, d), lambda i: (0, 0)),
                  pl.BlockSpec((d, 2 * LANES), lambda i: (0, 0))],
        out_specs=[pl.BlockSpec((1, c, LANES), lambda i: (i, 0, 0)),
                   pl.BlockSpec((1, c, LANES), lambda i: (i, 0, 0))],
        out_shape=[jax.ShapeDtypeStruct((b, c, LANES), bf16),
                   jax.ShapeDtypeStruct((b, c, LANES), bf16)],
        compiler_params=_cparams(("parallel",)),
        name="ctxkv",
    )(ctx, mod, g, w_kv)


def _attn_kernel(seq_len, q_ref, kp_ref, km_ref, kn_ref, vp_ref, vm_ref, vn_ref,
                 ck_ref, cv_ref, sink_ref, o_ref, kext, vext):
    j = pl.program_id(1)
    tq = ATT_TILE
    blk = ATT_BLOCK
    kext[0:blk] = kp_ref[0]
    kext[blk:blk + tq] = km_ref[0]
    kext[blk + tq:] = kn_ref[0]
    vext[:, LANES:] = jnp.ones((tq + 2 * blk, LANES), bf16)
    vext[0:blk, :LANES] = vp_ref[0]
    vext[blk:blk + tq, :LANES] = vm_ref[0]
    vext[blk + tq:, :LANES] = vn_ref[0]
    n_ctx = ck_ref.shape[1]
    nk = n_ctx + 3 * blk
    n_chunks = (N_HEADS * HEAD_DIM) // LANES
    rows = n_chunks * blk
    half = HEAD_DIM
    klow = lax.broadcasted_iota(i32, (1, LANES), 1) < half
    vlane = lax.broadcasted_iota(i32, (1, 2 * LANES), 1)
    vlow = (vlane < half) | ((vlane >= LANES) & (vlane < LANES + half))
    zero = jnp.zeros((), bf16)
    ck = ck_ref[0]
    cvx = jnp.concatenate([cv_ref[0], jnp.ones((n_ctx, LANES), bf16)], axis=1)
    ck_lo, ck_hi = jnp.where(klow, ck, zero), jnp.where(klow, zero, ck)
    cv_lo, cv_hi = jnp.where(vlow, cvx, zero), jnp.where(vlow, zero, cvx)
    sink2 = sink_ref[...]
    sink_lo, sink_hi = sink2[:, 0:1], sink2[:, half:half + 1]
    low = lax.broadcasted_iota(i32, (rows, LANES), 1) < half
    qi = lax.broadcasted_iota(i32, (rows, 3 * blk), 0) % blk
    pk = lax.broadcasted_iota(i32, (rows, 3 * blk), 1)
    band_bias = jnp.where(jnp.abs(pk - blk - qi) <= WINDOW, 0.0, NEG_INF).astype(f32)
    pcol = lax.broadcasted_iota(i32, (1, 3 * blk), 1)
    nt = (((1,), (1,)), ((), ()))

    def row_max(s):
        blocks = [s[:, i:i + LANES] for i in range(0, s.shape[1], LANES)]
        return jnp.max(functools.reduce(jnp.maximum, blocks), axis=1, keepdims=True)

    def sub(s, carry):
        r0 = pl.multiple_of(s * blk, blk)
        qs = q_ref[0, pl.ds(r0, blk), :]
        lhs = jnp.concatenate([qs[:, c * LANES:(c + 1) * LANES] for c in range(n_chunks)], axis=0)
        kl = kext[pl.ds(r0, 3 * blk), :]
        vl = vext[pl.ds(r0, 3 * blk), :]
        kbd = jnp.concatenate([ck_lo, jnp.where(klow, kl, zero),
                               ck_hi, jnp.where(klow, zero, kl)], axis=0)
        vbd = jnp.concatenate([cv_lo, jnp.where(vlow, vl, zero),
                               cv_hi, jnp.where(vlow, zero, vl)], axis=0)
        kpos = j * tq + r0 - blk + pcol
        bias = band_bias + jnp.where((kpos >= 0) & (kpos < seq_len), 0.0, NEG_INF).astype(f32)
        sc = lax.dot_general(lhs, kbd, nt, preferred_element_type=f32)
        s_lo = jnp.concatenate([sc[:, :n_ctx], sc[:, n_ctx:nk] + bias], axis=1)
        s_hi = jnp.concatenate([sc[:, nk:nk + n_ctx], sc[:, nk + n_ctx:] + bias], axis=1)
        m_lo = jnp.maximum(row_max(s_lo), sink_lo)
        m_hi = jnp.maximum(row_max(s_hi), sink_hi)
        e = jnp.concatenate([jnp.exp2(s_lo - m_lo), jnp.exp2(s_hi - m_hi)], axis=1).astype(bf16)
        ov = jnp.dot(e, vbd, preferred_element_type=f32)
        den = ov[:, LANES:] + jnp.exp2(sink2 - jnp.where(low, m_lo, m_hi))
        o = (ov[:, :LANES] / den).astype(bf16)
        for c in range(n_chunks):
            o_ref[0, pl.ds(r0, blk), c * LANES:(c + 1) * LANES] = o[c * blk:(c + 1) * blk]
        return carry

    lax.fori_loop(0, tq // blk, sub, 0, unroll=8)


def _attention(q, k, v, ck, cv, sinkcol):
    b, l, qw = q.shape
    c = ck.shape[1]
    tq = ATT_TILE
    r = tq // ATT_BLOCK
    nb = l // ATT_BLOCK
    prev = pl.BlockSpec((1, ATT_BLOCK, LANES), lambda i, j: (i, jnp.maximum(j * r - 1, 0), 0))
    main = pl.BlockSpec((1, tq, LANES), lambda i, j: (i, j, 0))
    nxt = pl.BlockSpec((1, ATT_BLOCK, LANES), lambda i, j: (i, jnp.minimum(j * r + r, nb - 1), 0))
    cspec = pl.BlockSpec((1, c, LANES), lambda i, j: (i, 0, 0))
    return pl.pallas_call(
        functools.partial(_attn_kernel, l),
        grid=(b, l // tq),
        in_specs=[pl.BlockSpec((1, tq, qw), lambda i, j: (i, j, 0)),
                  prev, main, nxt, prev, main, nxt, cspec, cspec,
                  pl.BlockSpec(sinkcol.shape, lambda i, j: (0, 0))],
        out_specs=pl.BlockSpec((1, tq, qw), lambda i, j: (i, j, 0)),
        out_shape=jax.ShapeDtypeStruct((b, l, qw), bf16),
        scratch_shapes=[pltpu.VMEM((tq + 2 * ATT_BLOCK, LANES), bf16),
                        pltpu.VMEM((tq + 2 * ATT_BLOCK, 2 * LANES), bf16)],
        compiler_params=_cparams(("parallel", "parallel")),
        name="attention",
    )(q, k, k, k, v, v, v, ck, cv, sinkcol)


def _pack_pair(lo, hi):
    lo = lax.bitcast_convert_type(lo.astype(bf16).astype(f32), u32)
    hi = lax.bitcast_convert_type(hi.astype(bf16).astype(f32), u32)
    return (lo >> 16) | (hi & jnp.uint32(0xFFFF0000))


def _fourier1_kernel(x_ref, mod_ref, g_ref, w_ref, cs_ref, m_ref, ct_ref, st_ref, z_ref, ab_ref):
    n1 = x_ref.shape[1]
    nt = x_ref.shape[2]
    x = x_ref[0].reshape(n1 * nt, x_ref.shape[3])
    h = _norm_mod(x, g_ref[...], mod_ref[0, 0:1, :], mod_ref[0, 1:2, :]).astype(bf16)
    p = jnp.dot(h, w_ref[...], preferred_element_type=f32)
    for g in range(FOURIER_GROUPS):
        ug = p[:, g * LANES:(g + 1) * LANES].astype(bf16)
        ab = jnp.dot(ug, cs_ref[...], preferred_element_type=f32)
        ab_ref[0] = ab[:, :LANES]
        ab_ref[1] = ab[:, LANES:]
        for t0 in range(0, nt, 2):
            stack = jnp.concatenate(
                [jnp.concatenate([ab_ref[0, pl.ds(t, n1, stride=nt), :],
                                  ab_ref[1, pl.ds(t, n1, stride=nt), :]], axis=0)
                 for t in (t0, t0 + 1)], axis=1).astype(bf16)
            z2 = jnp.dot(m_ref[...], stack, preferred_element_type=f32)
            for t in (t0, t0 + 1):
                z = z2[:, (t - t0) * LANES:(t - t0 + 1) * LANES]
                zr, zn = z[:n1], z[n1:]
                ct, st = ct_ref[t], st_ref[t]
                z_ref[0, g, t] = _pack_pair(ct * zr - st * zn, ct * zn + st * zr)


def _fourier2_kernel(scale, z_ref, m_ref, o_ref, zbuf, ybuf):
    _, grp, n2, tk, w = z_ref.shape
    for gp in range(grp // 2):
        zbuf[0] = z_ref[0, 2 * gp].reshape(n2 * tk, w)
        zbuf[1] = z_ref[0, 2 * gp + 1].reshape(n2 * tk, w)
        for j in range(tk):
            ys = []
            for h in range(2):
                zp = zbuf[h, pl.ds(j, n2, stride=tk), :]
                zr = lax.bitcast_convert_type(zp << 16, f32).astype(bf16)
                zn = lax.bitcast_convert_type(zp & jnp.uint32(0xFFFF0000), f32).astype(bf16)
                ys.append(jnp.dot(m_ref[...], jnp.concatenate([zr, zn], axis=0),
                                  preferred_element_type=f32) * scale)
            ybuf[pl.ds(j, n2, stride=tk), :] = _pack_pair(ys[0], ys[1])
        o_ref[0, gp] = ybuf[...].reshape(n2, tk, w)


def _fourier(x, mod, g, w_f, cs):
    b, l, d = x.shape
    n2 = DFT_INNER
    n1 = l // n2
    grp, w = FOURIER_GROUPS, FOURIER_GROUP_W
    t2 = SUBLANES
    k1 = jnp.arange(n1, dtype=i32)
    ang1 = ((k1[:, None] * k1[None, :]) % n1).astype(f32) * (2.0 * math.pi / n1)
    c1, s1 = jnp.cos(ang1), jnp.sin(ang1)
    m1 = jnp.concatenate([jnp.concatenate([c1, -s1], axis=1),
                          jnp.concatenate([s1, c1], axis=1)], axis=0).astype(bf16)
    l2 = jnp.arange(n2, dtype=i32)
    angt = ((l2[:, None] * k1[None, :]) % l).astype(f32) * (2.0 * math.pi / l)
    ct = jnp.broadcast_to(jnp.cos(angt)[:, :, None], (n2, n1, w))
    st = jnp.broadcast_to(jnp.sin(angt)[:, :, None], (n2, n1, w))
    ang2 = ((l2[:, None] * l2[None, :]) % n2).astype(f32) * (2.0 * math.pi / n2)
    m2 = jnp.concatenate([jnp.cos(ang2), -jnp.sin(ang2)], axis=1).astype(bf16)

    tspec = pl.BlockSpec((t2, n1, w), lambda t, i: (t, 0, 0))
    z = pl.pallas_call(
        _fourier1_kernel,
        grid=(n2 // t2, b),
        in_specs=[pl.BlockSpec((1, n1, t2, d), lambda t, i: (i, 0, t, 0)),
                  pl.BlockSpec((1, 6, d), lambda t, i: (i, 0, 0)),
                  pl.BlockSpec((1, d), lambda t, i: (0, 0)),
                  pl.BlockSpec(w_f.shape, lambda t, i: (0, 0)),
                  pl.BlockSpec(cs.shape, lambda t, i: (0, 0)),
                  pl.BlockSpec(m1.shape, lambda t, i: (0, 0)), tspec, tspec],
        out_specs=pl.BlockSpec((1, grp, t2, n1, w), lambda t, i: (i, 0, t, 0, 0)),
        out_shape=jax.ShapeDtypeStruct((b, grp, n2, n1, w), u32),
        scratch_shapes=[pltpu.VMEM((2, n1 * t2, w), f32)],
        compiler_params=_cparams(("parallel", "parallel")),
        name="fourier_outer",
    )(x.reshape(b, n1, n2, d), mod, g, w_f, cs, m1, ct, st)

    tk = 2 * SUBLANES
    y = pl.pallas_call(
        functools.partial(_fourier2_kernel, 1.0 / math.sqrt(l * w)),
        grid=(b, n1 // tk),
        in_specs=[pl.BlockSpec((1, grp, n2, tk, w), lambda i, t: (i, 0, 0, t, 0)),
                  pl.BlockSpec(m2.shape, lambda i, t: (0, 0))],
        out_specs=pl.BlockSpec((1, grp // 2, n2, tk, w), lambda i, t: (i, 0, 0, t, 0)),
        out_shape=jax.ShapeDtypeStruct((b, grp // 2, n2, n1, w), u32),
        scratch_shapes=[pltpu.VMEM((2, n2 * tk, w), u32), pltpu.VMEM((n2 * tk, w), u32)],
        compiler_params=_cparams(("parallel", "parallel")),
        name="fourier_inner",
    )(z, m2)
    return y.reshape(b, grp // 2, l, w)


def _first_max4(a):
    m = jnp.maximum(jnp.maximum(a[0], a[1]), jnp.maximum(a[2], a[3]))
    idx = jnp.where(a[0] == m, 0, jnp.where(a[1] == m, 1, jnp.where(a[2] == m, 2, 3)))
    return m, idx


def _pick4(vals, idx):
    return jnp.where(idx == 0, vals[0], jnp.where(idx == 1, vals[1],
                                                   jnp.where(idx == 2, vals[2], vals[3])))


def _route(f, rw_ref, rb_ref, tri_ref, base_ref, first_step, ri_ref, wc_ref, cnt_ref, meta_ref):
    rw2 = rw_ref[...]
    pieces = []
    for fp in (f if isinstance(f, (list, tuple)) else [f]):
        f_hi = fp.astype(bf16)
        f_lo = (fp - f_hi.astype(f32)).astype(bf16)
        part = jnp.dot(f_hi, rw2, preferred_element_type=f32)
        pieces.append(part[:, :LANES] + part[:, LANES:]
                      + jnp.dot(f_lo, rw2[:, :LANES], preferred_element_type=f32))
    logits = jnp.concatenate(pieces, axis=0)
    tm = logits.shape[0]
    sc = jax.nn.sigmoid(logits)
    st = sc.T
    bt = (sc + rb_ref[...]).T
    neg = jnp.full((1, tm), -jnp.inf, f32)
    gs = []
    for g in range(N_GROUPS):
        a = [bt[4 * g + i: 4 * g + i + 1] for i in range(4)]
        m1, i1 = _first_max4(a)
        rest = [jnp.where(i1 == i, neg, a[i]) for i in range(4)]
        m2, _ = _first_max4(rest)
        gs.append(m1 + m2)
    _, gsel = _first_max4(gs)
    a = [_pick4([bt[4 * g + i: 4 * g + i + 1] for g in range(N_GROUPS)], gsel) for i in range(4)]
    s = [_pick4([st[4 * g + i: 4 * g + i + 1] for g in range(N_GROUPS)], gsel) for i in range(4)]
    _, i1 = _first_max4(a)
    rest = [jnp.where(i1 == i, neg, a[i]) for i in range(4)]
    _, i2 = _first_max4(rest)
    w1 = _pick4(s, i1)
    w2 = _pick4(s, i2)
    tot = w1 + w2
    w1 = w1 / tot
    w2 = w2 / tot
    e0 = gsel * EXPERTS_PER_GROUP + i1
    e1 = gsel * EXPERTS_PER_GROUP + i2

    @pl.when(first_step)
    def _():
        base_ref[...] = jnp.zeros_like(base_ref)

    td = DISPATCH_TILE
    eid = lax.broadcasted_iota(i32, (N_EXPERTS, tm), 0)
    oh0 = (eid == e0).astype(f32)
    oh1 = (eid == e1).astype(f32)
    oh = oh0 + oh1
    ohb = oh.astype(bf16)
    before = jnp.concatenate([jnp.dot(ohb[:, s:s + td], tri_ref[...], preferred_element_type=f32)
                              for s in range(0, tm, td)], axis=1)
    lane_tile = lax.broadcasted_iota(i32, (N_EXPERTS, tm), 1) // td
    ei = lax.broadcasted_iota(i32, (N_EXPERTS, N_EXPERTS), 0)
    ej = lax.broadcasted_iota(i32, (N_EXPERTS, N_EXPERTS), 1)
    strict_lower = (ej < ei).astype(f32)
    run_start = jnp.zeros((N_EXPERTS, tm), f32)
    goff = base_ref[...]
    for s in range(tm // td):
        cnt_s = jnp.sum(oh[:, s * td:(s + 1) * td], axis=1, keepdims=True)
        pad_s = jnp.floor((cnt_s + 7.0) * 0.125) * 8.0
        pad_b = jnp.broadcast_to(pad_s, (N_EXPERTS, LANES))
        start_b = jnp.dot(strict_lower, pad_b, precision=HIGHEST, preferred_element_type=f32)
        run_start = jnp.where(lane_tile == s, start_b[:, 0:1], run_start)
        meta_ref[s, 0] = start_b.astype(i32)
        meta_ref[s, 1] = pad_b.astype(i32)
        meta_ref[s, 2] = goff.astype(i32)
        goff = goff + pad_b
    base_ref[...] = goff
    cnt_ref[...] = goff
    pos = before + run_start
    lp0 = jnp.sum(oh0 * pos, axis=0, keepdims=True)
    lp1 = jnp.sum(oh1 * pos, axis=0, keepdims=True)
    zi = jnp.zeros((1, tm), i32)
    ri_ref[...] = jnp.concatenate(
        [lp0.astype(i32), lp1.astype(i32), e0, e1,
         lax.bitcast_convert_type(w1, i32), lax.bitcast_convert_type(w2, i32), zi, zi], axis=0)
    zf = jnp.zeros((LANES - 4, tm), f32)
    wc_ref[...] = jnp.concatenate([w1, w2, lp0, lp1, zf], axis=0).T


def _outproj_kernel(yf_ref, o_ref, x_ref, mod_ref, w_ref, g_ref, rw_ref, rb_ref, tri_ref,
                    x1_ref, f_ref, ri_ref, wc_ref, cnt_ref, meta_ref, base_ref):
    tm = x_ref.shape[1]
    fs = []
    for r in range(0, tm, tm // 2):
        rs = slice(r, r + tm // 2)
        groups = []
        for gp in range(FOURIER_GROUPS // 2):
            pair = yf_ref[0, gp, rs, :]
            groups.append(lax.bitcast_convert_type(pair << 16, f32).astype(bf16))
            groups.append(lax.bitcast_convert_type(pair & jnp.uint32(0xFFFF0000), f32).astype(bf16))
        mix = jnp.concatenate(groups + [o_ref[0, rs, :]], axis=1)
        y = jnp.dot(mix, w_ref[...], preferred_element_type=f32)
        x1 = x_ref[0, rs, :] + mod_ref[0, 2:3, :] * y
        x1_ref[0, rs, :] = x1
        f = _norm_mod(x1, g_ref[...], mod_ref[0, 3:4, :], mod_ref[0, 4:5, :])
        f_ref[0, rs, :] = f.astype(bf16)
        fs.append(f)
    first = (pl.program_id(0) == 0) & (pl.program_id(1) == 0)
    _route(fs, rw_ref, rb_ref, tri_ref, base_ref, first, ri_ref, wc_ref, cnt_ref, meta_ref)


def _before_in_tile():
    tpos = jnp.arange(DISPATCH_TILE)
    return (tpos[:, None] < tpos[None, :]).astype(bf16)


def _route_specs(b, l, tm):
    nl = l // tm
    rw = lambda d: pl.BlockSpec((d, 2 * LANES), lambda i, j: (0, 0))
    rb = pl.BlockSpec((1, LANES), lambda i, j: (0, 0))
    tri = pl.BlockSpec((DISPATCH_TILE, DISPATCH_TILE), lambda i, j: (0, 0))
    ns = tm // DISPATCH_TILE
    out_specs = [pl.BlockSpec((8, tm), lambda i, j: (0, i * nl + j)),
                 pl.BlockSpec((tm, LANES), lambda i, j: (i * nl + j, 0)),
                 pl.BlockSpec((N_EXPERTS, LANES), lambda i, j: (0, 0)),
                 pl.BlockSpec((ns, 3, N_EXPERTS, LANES), lambda i, j: (i * nl + j, 0, 0, 0))]
    out_shape = [jax.ShapeDtypeStruct((8, b * l), i32),
                 jax.ShapeDtypeStruct((b * l, LANES), f32),
                 jax.ShapeDtypeStruct((N_EXPERTS, LANES), f32),
                 jax.ShapeDtypeStruct((b * l // DISPATCH_TILE, 3, N_EXPERTS, LANES), i32)]
    return rw, rb, tri, out_specs, out_shape


def _outproj(yf, o, x, mod, w, g, rw, rb):
    b, l, d = x.shape
    tm = PROJ_TILE
    tri = _before_in_tile()
    rws, rbs, tris, r_specs, r_shapes = _route_specs(b, l, tm)
    row = pl.BlockSpec((1, tm, d), lambda i, j: (i, j, 0))
    return pl.pallas_call(
        _outproj_kernel,
        grid=(b, l // tm),
        in_specs=[pl.BlockSpec((1, yf.shape[1], tm, LANES), lambda i, j: (i, 0, j, 0)),
                  pl.BlockSpec((1, tm, o.shape[2]), lambda i, j: (i, j, 0)),
                  row,
                  pl.BlockSpec((1, 6, d), lambda i, j: (i, 0, 0)),
                  pl.BlockSpec(w.shape, lambda i, j: (0, 0)),
                  pl.BlockSpec((1, d), lambda i, j: (0, 0)),
                  rws(d), rbs, tris],
        out_specs=[row, row] + r_specs,
        out_shape=[jax.ShapeDtypeStruct((b, l, d), f32),
                   jax.ShapeDtypeStruct((b, l, d), bf16)] + r_shapes,
        scratch_shapes=[pltpu.VMEM((N_EXPERTS, LANES), f32)],
        compiler_params=_cparams(("arbitrary", "arbitrary")),
        name="outproj_router",
    )(yf, o, x, mod, w, g, rw, rb, tri)


def _conv_kernel(seq_len, up_ref, um_ref, un_ref, x_ref, mod_ref, dw_ref, db_ref, lg_ref, lb_ref,
                 w_ref, pb_ref, g_ref, rw_ref, rb_ref, tri_ref,
                 x1_ref, f_ref, ri_ref, wc_ref, cnt_ref, meta_ref, base_ref, ext, conv_out):
    j = pl.program_id(1)
    tm = um_ref.shape[1]
    hl = CONV_HALO
    half = CONV_W // 2
    prev = jnp.where(j > 0, up_ref[0], jnp.zeros_like(up_ref[0]))
    nxt = jnp.where((j + 1) * tm < seq_len, un_ref[0], jnp.zeros_like(un_ref[0]))
    for c in range(ext.shape[0]):
        lanes_c = slice(c * LANES, (c + 1) * LANES)
        ext[c, 0:hl] = prev[:, lanes_c]
        ext[c, hl:hl + tm] = um_ref[0, :, lanes_c]
        ext[c, hl + tm:] = nxt[:, lanes_c]
    base = hl - half
    span = (CONV_W - 1) // SUBLANES * SUBLANES
    rows = CONV_ROWS

    def lane_chunk(c, carry):
        lanes = pl.ds(pl.multiple_of(c * LANES, LANES), LANES)
        for r in range(0, tm, rows):
            part = jnp.broadcast_to(db_ref[:, lanes], (rows, LANES))
            for phase in range(SUBLANES):
                win = ext[c, base + phase + r: base + phase + r + rows + span, :]
                same = None
                for t in range(phase, CONV_W, SUBLANES):
                    term = win[t - phase: t - phase + rows, :] * dw_ref[t:t + 1, lanes]
                    same = term if same is None else same + term
                part = part + same
            conv_out[r:r + rows, lanes] = part
        return carry

    lax.fori_loop(0, um_ref.shape[2] // LANES, lane_chunk, 0)
    fs = []
    for r in range(0, tm, tm // 2):
        rs = slice(r, r + tm // 2)
        acc = conv_out[rs, :]
        mu = jnp.mean(acc, axis=-1, keepdims=True)
        cen = acc - mu
        var = jnp.mean(cen * cen, axis=-1, keepdims=True)
        ln = cen * lax.rsqrt(var + EPS) * lg_ref[...] + lb_ref[...]
        act = ln * jax.nn.sigmoid(ln)
        y = jnp.dot(act.astype(bf16), w_ref[...], preferred_element_type=f32) + pb_ref[...]
        x1 = x_ref[0, rs, :] + mod_ref[0, 2:3, :] * y
        x1_ref[0, rs, :] = x1
        f = _norm_mod(x1, g_ref[...], mod_ref[0, 3:4, :], mod_ref[0, 4:5, :])
        f_ref[0, rs, :] = f.astype(bf16)
        fs.append(f)
    first = (pl.program_id(0) == 0) & (j == 0)
    _route(fs, rw_ref, rb_ref, tri_ref, base_ref, first, ri_ref, wc_ref, cnt_ref, meta_ref)


def _conv(u, x, mod, dw_w, dw_b, ln_g, ln_b, pw2_w, pw2_b, g, rw, rb):
    b, l, d = x.shape
    tm = TOKEN_TILE
    tri = _before_in_tile()
    hl = CONV_HALO
    r = tm // hl
    nh = l // hl
    rws, rbs, tris, r_specs, r_shapes = _route_specs(b, l, tm)
    row = pl.BlockSpec((1, tm, d), lambda i, j: (i, j, 0))
    vec = pl.BlockSpec((1, d), lambda i, j: (0, 0))
    return pl.pallas_call(
        functools.partial(_conv_kernel, l),
        grid=(b, l // tm),
        in_specs=[pl.BlockSpec((1, hl, d), lambda i, j: (i, jnp.maximum(j * r - 1, 0), 0)),
                  row,
                  pl.BlockSpec((1, hl, d), lambda i, j: (i, jnp.minimum(j * r + r, nh - 1), 0)),
                  row,
                  pl.BlockSpec((1, 6, d), lambda i, j: (i, 0, 0)),
                  pl.BlockSpec(dw_w.shape, lambda i, j: (0, 0)),
                  vec, vec, vec,
                  pl.BlockSpec(pw2_w.shape, lambda i, j: (0, 0)),
                  vec, vec, rws(d), rbs, tris],
        out_specs=[row, row] + r_specs,
        out_shape=[jax.ShapeDtypeStruct((b, l, d), f32),
                   jax.ShapeDtypeStruct((b, l, d), bf16)] + r_shapes,
        scratch_shapes=[pltpu.VMEM((N_EXPERTS, LANES), f32),
                        pltpu.VMEM((d // LANES, tm + 2 * hl, LANES), f32),
                        pltpu.VMEM((tm, d), f32)],
        compiler_params=_cparams(("arbitrary", "arbitrary")),
        name="conv_router",
    )(u, u, u, x, mod, dw_w, dw_b, ln_g, ln_b, pw2_w, pw2_b, g, rw, rb, tri)


def _pack_bf16_pairs(x):
    h = x.shape[1] // 2
    lo = lax.bitcast_convert_type(x[:, :h], u32)
    hi = lax.bitcast_convert_type(x[:, h:], u32)
    return (lo >> 16) | (hi & jnp.uint32(0xFFFF0000))


def _unpack_bf16_pairs(u):
    lo = lax.bitcast_convert_type(u << 16, f32)
    hi = lax.bitcast_convert_type(u & jnp.uint32(0xFFFF0000), f32)
    return jnp.concatenate([lo, hi], axis=1).astype(bf16)


def _run_copies(meta, tile, local_ref, hbm_ref, sem, to_hbm):
    start_ref, size_ref, dst_ref = meta
    for e in range(N_EXPERTS):
        k = tile * N_EXPERTS + e
        size = pl.multiple_of(size_ref[k], RUN_ALIGN)

        @pl.when(size > 0)
        def _():
            loc = local_ref.at[pl.ds(pl.multiple_of(start_ref[k], RUN_ALIGN), size)]
            glob = hbm_ref.at[pl.ds(pl.multiple_of(dst_ref[k], RUN_ALIGN), size)]
            if to_hbm:
                pltpu.make_async_copy(loc, glob, sem).start()
            else:
                pltpu.make_async_copy(glob, loc, sem).start()


def _wait_rows(rows, local_ref, hbm_ref, sem):
    rows = pl.multiple_of(rows, RUN_ALIGN)

    @pl.when(rows > 0)
    def _():
        pltpu.make_async_copy(local_ref.at[pl.ds(0, rows)], hbm_ref.at[pl.ds(0, rows)], sem).wait()


def _dispatch_kernel(start_ref, size_ref, dst_ref, tot_ref, tail_start_ref, tail_size_ref, nv_ref,
                     f_ref, lp_ref, xs_ref, loc, zbuf, sem, zsem):
    i = pl.program_id(0)
    n = pl.num_programs(0)
    slot = i % 2
    subs = loc.shape[1]
    rows = loc.shape[2]
    td = f_ref.shape[0] // subs
    meta = (start_ref, size_ref, dst_ref)

    def drain(step, which):
        for s in range(subs):
            _wait_rows(tot_ref[step * subs + s], loc.at[which, s], xs_ref, sem.at[which])

    @pl.when(i >= 2)
    def _():
        drain(i - 2, slot)

    r = lax.broadcasted_iota(i32, (rows, td), 0)
    for s in range(subs):
        cols = slice(s * td, (s + 1) * td)
        pick0 = r == lp_ref[0:1, cols]
        pick1 = r == lp_ref[1:2, cols]
        onehot = (pick0 | pick1).astype(bf16)
        sorted_rows = jnp.dot(onehot, f_ref[cols, :], preferred_element_type=f32)
        half = sorted_rows.shape[1] // 2
        loc[slot, s, :, :half] = _pack_bf16_pairs(sorted_rows)
        w0 = lax.bitcast_convert_type(lp_ref[4:5, cols], f32)
        w1 = lax.bitcast_convert_type(lp_ref[5:6, cols], f32)
        row_w = jnp.sum(jnp.where(pick0, w0, 0.0) + jnp.where(pick1, w1, 0.0), axis=1, keepdims=True)
        loc[slot, s, :, half:] = jnp.broadcast_to(lax.bitcast_convert_type(row_w, u32), (rows, LANES))
        _run_copies(meta, i * subs + s, loc.at[slot, s], xs_ref, sem.at[slot], to_hbm=True)

    @pl.when(i == n - 1)
    def _():
        zbuf[...] = jnp.zeros_like(zbuf)
        total = 0
        for e in range(N_EXPERTS):
            size = pl.multiple_of(tail_size_ref[e], RUN_ALIGN)
            total = total + size

            @pl.when(size > 0)
            def _():
                pltpu.make_async_copy(
                    zbuf.at[pl.ds(0, size)],
                    xs_ref.at[pl.ds(pl.multiple_of(tail_start_ref[e], RUN_ALIGN), size)], zsem).start()

        _wait_rows(total, zbuf, xs_ref, zsem)

        def zero_block(k, c):
            pltpu.make_async_copy(zbuf, xs_ref.at[pl.ds(pl.multiple_of(k * zbuf.shape[0], RUN_ALIGN),
                                                        zbuf.shape[0])], zsem).start()
            return c

        def wait_block(k, c):
            pltpu.make_async_copy(zbuf, xs_ref.at[pl.ds(0, zbuf.shape[0])], zsem).wait()
            return c

        n_blocks = xs_ref.shape[0] // zbuf.shape[0]
        lax.fori_loop(nv_ref[0], n_blocks, zero_block, 0)
        lax.fori_loop(nv_ref[0], n_blocks, wait_block, 0)
        drain(i, slot)

        @pl.when(i >= 1)
        def _():
            drain(i - 1, 1 - slot)


def _dispatch(tables, f2, ri, n_slots):
    t, d = f2.shape
    subs = DISPATCH_SUBTILES
    tm = DISPATCH_TILE * subs
    return pl.pallas_call(
        _dispatch_kernel,
        grid_spec=pltpu.PrefetchScalarGridSpec(
            num_scalar_prefetch=7,
            grid=(t // tm,),
            in_specs=[pl.BlockSpec((tm, d), lambda i, *_: (i, 0)),
                      pl.BlockSpec((8, tm), lambda i, *_: (0, i))],
            out_specs=pl.BlockSpec(memory_space=pl.ANY),
            scratch_shapes=[pltpu.VMEM((2, subs, LOCAL_ROWS, d // 2 + LANES), u32),
                            pltpu.VMEM((EXPERT_ROWS, d // 2 + LANES), u32),
                            pltpu.SemaphoreType.DMA((2,)), pltpu.SemaphoreType.DMA(())]),
        out_shape=jax.ShapeDtypeStruct((n_slots, d // 2 + LANES), u32),
        compiler_params=_cparams(("arbitrary",)),
        name="moe_dispatch",
    )(*tables, f2, ri)


def _expert_kernel(be_ref, second_ref, nv_ref, x_ref, wg1, wu1, wd1, wg2, wu2, wd2, y_ref, *w_bf):
    del second_ref
    i = pl.program_id(0)
    tb = y_ref.shape[0] // 2
    half = y_ref.shape[1]
    e1, e2 = be_ref[2 * i], be_ref[2 * i + 1]
    first, second = w_bf[:3], w_bf[3:]

    @pl.when(jnp.logical_or(i == 0, e1 != be_ref[jnp.maximum(2 * i - 2, 0)]))
    def _():
        for dst, src in zip(first, (wg1, wu1, wd1)):
            dst[...] = src[0, 0].astype(bf16)

    @pl.when(e2 != e1)
    def _():
        for dst, src in zip(second, (wg2, wu2, wd2)):
            dst[...] = src[0, 0].astype(bf16)

    def block(k, weights):
        wgb, wub, wdb = weights
        rows = slice(k * tb, (k + 1) * tb)
        xb = _unpack_bf16_pairs(x_ref[rows, :half])
        row_w = lax.bitcast_convert_type(x_ref[rows, half:], f32)
        gate = jnp.dot(xb, wgb[...], preferred_element_type=f32)
        up = jnp.dot(xb, wub[...], preferred_element_type=f32)
        hid = (gate * jax.nn.sigmoid(gate) * up).astype(bf16)
        y = jnp.dot(hid, wdb[...], preferred_element_type=f32)
        y = jnp.concatenate([y[:, c:c + LANES] * row_w for c in range(0, y.shape[1], LANES)], axis=1)
        y_ref[rows, :] = _pack_bf16_pairs(y.astype(bf16).astype(f32))

    used = 2 * i < nv_ref[0]

    @pl.when(used & (e2 == e1))
    def _():
        block(0, first)
        block(1, first)

    @pl.when(used & (e2 != e1))
    def _():
        block(0, first)
        block(1, second)

    @pl.when(jnp.logical_not(used))
    def _():
        y_ref[...] = jnp.zeros_like(y_ref)


def _experts(block_e, n_valid, xs, w_gate, w_up, w_down, layer):
    ns, xw = xs.shape
    nb = EXPERT_BLOCKS_PER_STEP
    tb = EXPERT_ROWS
    d, ff = w_gate.shape[2:]
    dh = d // 2
    e1, e2 = block_e[0::2], block_e[1::2]
    second = lax.cummax(jnp.where(e2 != e1, e2, 0))
    map1 = lambda i, be, sec, nv: (layer, be[2 * i], 0, 0)
    map2 = lambda i, be, sec, nv: (layer, sec[i], 0, 0)
    w_specs = [pl.BlockSpec((1, 1, d, ff), m) for m in (map1, map1)] + [pl.BlockSpec((1, 1, ff, d), map1)]
    w_specs += [pl.BlockSpec((1, 1, d, ff), m) for m in (map2, map2)] + [pl.BlockSpec((1, 1, ff, d), map2)]
    w_scratch = 2 * [pltpu.VMEM((d, ff), bf16), pltpu.VMEM((d, ff), bf16), pltpu.VMEM((ff, d), bf16)]
    return pl.pallas_call(
        _expert_kernel,
        grid_spec=pltpu.PrefetchScalarGridSpec(
            num_scalar_prefetch=3,
            grid=(ns // (nb * tb),),
            in_specs=[pl.BlockSpec((nb * tb, xw), lambda i, *_: (i, 0))] + w_specs,
            out_specs=pl.BlockSpec((nb * tb, dh), lambda i, *_: (i, 0)),
            scratch_shapes=w_scratch),
        out_shape=jax.ShapeDtypeStruct((ns, dh), u32),
        compiler_params=_cparams(("arbitrary",)),
        name="moe_experts",
    )(block_e, second, n_valid, xs, w_gate, w_up, w_down, w_gate, w_up, w_down)


def _combine_kernel(final, start_ref, size_ref, dst_ref, tot_ref, ys_ref, wc_ref, x_ref, mod_ref,
                    g_ref, *rest):
    if final:
        o_ref, loc, sem = rest
    else:
        nmod_ref, w_ref, b_ref, o_ref, u_ref, loc, sem = rest
    i = pl.program_id(0)
    n = pl.num_programs(0)
    slot = i % 2
    subs = loc.shape[1]
    rows = loc.shape[2]
    td = x_ref.shape[0] // subs
    meta = (start_ref, size_ref, dst_ref)

    def fetch(step, which):
        for s in range(subs):
            _run_copies(meta, step * subs + s, loc.at[which, s], ys_ref, sem.at[which, s],
                        to_hbm=False)

    @pl.when(i == 0)
    def _():
        loc[...] = jnp.zeros_like(loc)
        fetch(i, slot)

    @pl.when(i + 1 < n)
    def _():
        fetch(i + 1, 1 - slot)

    c = lax.broadcasted_iota(i32, (td, rows), 1)
    parts = []
    for s in range(subs):
        _wait_rows(tot_ref[i * subs + s], loc.at[slot, s], ys_ref, sem.at[slot, s])
        wc = wc_ref[s * td:(s + 1) * td, :]
        sel = ((c == wc[:, 2:3].astype(i32)) | (c == wc[:, 3:4].astype(i32))).astype(bf16)
        parts.append(jnp.dot(sel, _unpack_bf16_pairs(loc[slot, s]), preferred_element_type=f32))
    xo = x_ref[...] + mod_ref[0, 5:6, :] * jnp.concatenate(parts, axis=0)
    if final:
        ms = jnp.mean(xo * xo, axis=-1, keepdims=True)
        o_ref[...] = xo * lax.rsqrt(ms + EPS) * g_ref[...]
    else:
        o_ref[...] = xo
        h = _norm_mod(xo, g_ref[...], nmod_ref[0, 0:1, :], nmod_ref[0, 1:2, :])
        p = jnp.dot(h.astype(bf16), w_ref[...], preferred_element_type=f32) + b_ref[...]
        ch = p.shape[1] // 2
        u_ref[...] = p[:, :ch] * jax.nn.sigmoid(p[:, ch:])


def _combine(tables, ys, wc, x, mod, g, glu=None):
    b, l, d = x.shape
    subs = DISPATCH_SUBTILES if glu is None else GLU_SUBTILES
    tm = DISPATCH_TILE * subs
    per_batch = l // tm
    const = lambda i, *_: (0, 0)
    tile = lambda cols: pl.BlockSpec((tm, cols), lambda i, *_: (i, 0))
    mod_spec = pl.BlockSpec((1, 6, d), lambda i, *_: (i // per_batch, 0, 0))
    in_specs = [pl.BlockSpec(memory_space=pl.ANY), tile(LANES), tile(d), mod_spec,
                pl.BlockSpec((1, d), const)]
    args = [ys, wc, x.reshape(b * l, d), mod, g]
    out_specs = [tile(d)]
    out_shape = [jax.ShapeDtypeStruct((b * l, d), f32)]
    if glu is not None:
        nmod, w, bias = glu
        in_specs += [mod_spec, pl.BlockSpec(w.shape, const), pl.BlockSpec(bias.shape, const)]
        args += [nmod, w, bias]
        out_specs.append(tile(w.shape[1] // 2))
        out_shape.append(jax.ShapeDtypeStruct((b * l, w.shape[1] // 2), f32))
    outs = pl.pallas_call(
        functools.partial(_combine_kernel, glu is None),
        grid_spec=pltpu.PrefetchScalarGridSpec(
            num_scalar_prefetch=4,
            grid=(b * per_batch,),
            in_specs=in_specs,
            out_specs=out_specs,
            scratch_shapes=[pltpu.VMEM((2, subs, LOCAL_ROWS, d // 2), u32),
                            pltpu.SemaphoreType.DMA((2, subs))]),
        out_shape=out_shape,
        compiler_params=_cparams(("arbitrary",)),
        name="moe_combine",
    )(*tables, *args)
    return [o.reshape(b, l, -1) for o in outs]


def _moe(f, routed, x, mod, g, w_gate, w_up, w_down, layer, glu=None):
    ri, wc, cnt, meta = routed
    b, l, d = x.shape
    t = b * l
    tb = EXPERT_ROWS
    n_tiles = t // DISPATCH_TILE
    used = cnt[:, 0].astype(i32)
    region = (used + tb - 1) // tb * tb
    gend = jnp.cumsum(region)
    gstart = gend - region
    max_rows = 2 * t + n_tiles * N_EXPERTS * (RUN_ALIGN - 1) + N_EXPERTS * (tb - 1)
    n_blocks = -(-max_rows // (tb * EXPERT_BLOCKS_PER_STEP)) * EXPERT_BLOCKS_PER_STEP
    m = meta[:, :, :, 0]
    run_start = m[:, 0].reshape(-1)
    run_size = m[:, 1].reshape(-1)
    run_dst = (m[:, 2] + gstart[None, :]).reshape(-1)
    tile_rows = jnp.sum(m[:, 1], axis=1)
    block_row = jnp.arange(n_blocks, dtype=i32) * tb
    block_e = jnp.minimum(jnp.sum((block_row[:, None] >= gend[None, :]).astype(i32), axis=1),
                          N_EXPERTS - 1)
    n_valid = (gend[-1] // tb).reshape(1)
    xs = _dispatch((run_start, run_size, run_dst, tile_rows, gstart + used, region - used, n_valid),
                   f.reshape(t, d), ri, n_blocks * tb)
    ys = _experts(block_e, n_valid, xs, w_gate, w_up, w_down, layer)
    return _combine((run_start, run_size, run_dst, tile_rows), ys, wc, x, mod, g, glu)


def _rope_tables(l):
    lane = jnp.arange(LANES)
    dh = lane % HEAD_DIM
    inv = ROPE_THETA ** (-(dh % 16).astype(f32) / 16.0)
    sign = jnp.where((dh % 32) < 16, -1.0, 1.0).astype(f32)
    by_row = (dh // 32)[None, None, :] == 0
    ang_r = jnp.arange(l // GRID_W, dtype=f32)[:, None] * inv[None, :]
    ang_c = jnp.arange(GRID_W, dtype=f32)[:, None] * inv[None, :]
    cos = jnp.where(by_row, jnp.cos(ang_r)[:, None, :], jnp.cos(ang_c)[None, :, :])
    sin = jnp.where(by_row, jnp.sin(ang_r)[:, None, :], jnp.sin(ang_c)[None, :, :])
    return cos.reshape(l, LANES), (sin * sign[None, None, :]).reshape(l, LANES)


def kernel(x, c, ctx, c_ctx, ada_w, ada_b, norm_mix_g, norm_ffn_g, even_w_in, even_w_out, even_sink, conv_pw1_w, conv_pw1_b, conv_dw_w, conv_dw_b, conv_ln_g, conv_ln_b, conv_pw2_w, conv_pw2_b, router_w, router_b, moe_w_gate, moe_w_up, moe_w_down, final_norm_g):
    b, l, d = x.shape
    depth = ada_w.shape[0]
    assert depth == 2 and b < COND_ROWS
    ctx_row = b
    cond = jnp.zeros((COND_ROWS, d), f32).at[:b].set(c).at[ctx_row].set(c_ctx)
    mods = _adaln(cond, ada_w, ada_b).reshape(depth, COND_ROWS, 6, d)

    heads = jnp.arange(N_HEADS).reshape(N_KV_HEADS, N_HEADS // N_KV_HEADS).T.reshape(-1)
    qperm = (heads[:, None] * HEAD_DIM + jnp.arange(HEAD_DIM)[None, :]).reshape(-1)
    fw = FOURIER_GROUPS * FOURIER_GROUP_W
    qw = N_HEADS * HEAD_DIM
    w_in = even_w_in[0]
    w_in_p = jnp.concatenate([w_in[:, :fw], w_in[:, fw:fw + qw][:, qperm], w_in[:, fw + qw:]],
                             axis=1).astype(bf16)
    w_out = even_w_out[0]
    w_out_p = jnp.concatenate([w_out[:fw], w_out[fw:][qperm]], axis=0).astype(bf16)
    sink_pairs = (even_sink[0].astype(f32) * LOG2E).reshape(N_KV_HEADS, N_HEADS // N_KV_HEADS).T
    sinkcol = jnp.repeat(jnp.repeat(sink_pairs, HEAD_DIM, axis=1), ATT_BLOCK, axis=0)

    cidx = jnp.arange(FOURIER_GROUP_W, dtype=i32)
    angc = ((cidx[:, None] * cidx[None, :]) % FOURIER_GROUP_W).astype(f32) * (2.0 * math.pi / FOURIER_GROUP_W)
    cs = jnp.concatenate([jnp.cos(angc), jnp.sin(angc)], axis=1).astype(bf16)
    cos_t, sin_t = _rope_tables(l)

    rw32 = jnp.zeros((d, LANES), f32).at[:, :N_EXPERTS].set(router_w.astype(f32))
    rw_hi = rw32.astype(bf16)
    rw = jnp.concatenate([rw_hi, (rw32 - rw_hi.astype(f32)).astype(bf16)], axis=1)
    rb = jnp.zeros((1, LANES), f32).at[0, :N_EXPERTS].set(router_b)
    row = lambda v: v.reshape(1, -1)

    q, k, v = _inproj(x, mods[0], row(norm_mix_g[0]), w_in_p[:, fw:], cos_t, sin_t)
    ck, cv = _ctxkv(ctx, mods[0], row(norm_mix_g[0]), w_in_p[:, fw + qw:], ctx_row)
    yf = _fourier(x, mods[0], row(norm_mix_g[0]), w_in_p[:, :fw], cs)
    att = _attention(q, k, v, ck, cv, sinkcol)
    x1, f, *routed = _outproj(yf, att, x, mods[0], w_out_p, row(norm_ffn_g[0]), rw, rb)
    x2, u = _moe(f, routed, x1, mods[0], row(norm_mix_g[1]), moe_w_gate, moe_w_up, moe_w_down,
                 layer=0, glu=(mods[1], conv_pw1_w[0].astype(bf16), row(conv_pw1_b[0])))

    x3, f, *routed = _conv(u, x2, mods[1], conv_dw_w[0], row(conv_dw_b[0]), row(conv_ln_g[0]),
                           row(conv_ln_b[0]), conv_pw2_w[0].astype(bf16), row(conv_pw2_b[0]),
                           row(norm_ffn_g[1]), rw, rb)
    (out,) = _moe(f, routed, x3, mods[1], row(final_norm_g), moe_w_gate, moe_w_up, moe_w_down,
                  layer=1)
    return out
```

```python
import functools
import math

import jax
import jax.numpy as jnp
from jax import lax
from jax.experimental import pallas as pl
from jax.experimental.pallas import tpu as pltpu

f32 = jnp.float32
bf16 = jnp.bfloat16
i32 = jnp.int32
u32 = jnp.uint32
HIGHEST = lax.Precision.HIGHEST

GRID_W = 64
HEAD_DIM = 64
N_HEADS = 8
N_KV_HEADS = 2
WINDOW = 128
ATT_BLOCK = 128
ROPE_THETA = 10000.0
FOURIER_GROUPS = 4
FOURIER_GROUP_W = 128
CONV_W = 31
N_EXPERTS = 16
N_GROUPS = 4
EXPERTS_PER_GROUP = 4
EPS = 1e-6
NEG_INF = -1e30
LOG2E = math.log2(math.e)

LANES = 128
SUBLANES = 8
COND_ROWS = 8
DFT_INNER = 64
TOKEN_TILE = 512
PROJ_TILE = 1024
ATT_TILE = 1024
EXPERT_ROWS = 512
EXPERT_BLOCKS_PER_STEP = 2
DISPATCH_TILE = 256
DISPATCH_SUBTILES = 4
GLU_SUBTILES = 2
RUN_ALIGN = 8
LOCAL_ROWS = -(-(2 * DISPATCH_TILE + N_EXPERTS * (RUN_ALIGN - 1)) // LANES) * LANES
CONV_HALO = 16
CONV_ROWS = 128
VMEM_LIMIT = 56 * 1024 * 1024


def _cparams(sem, vmem=VMEM_LIMIT):
    return pltpu.CompilerParams(dimension_semantics=sem, vmem_limit_bytes=vmem)


def _adaln_kernel(cond_ref, w_ref, b_ref, o_ref):
    s = cond_ref[...]
    s = s * jax.nn.sigmoid(s)
    w = w_ref[0]
    s_hi, w_hi = s.astype(bf16), w.astype(bf16)
    s_lo = (s - s_hi.astype(f32)).astype(bf16)
    w_lo = (w - w_hi.astype(f32)).astype(bf16)
    dot = functools.partial(jnp.dot, preferred_element_type=f32)
    o_ref[0] = dot(s_hi, w_hi) + dot(s_hi, w_lo) + dot(s_lo, w_hi) + b_ref[0]


def _adaln(cond, ada_w, ada_b):
    depth, d, n = ada_w.shape
    tn = 3072
    return pl.pallas_call(
        _adaln_kernel,
        grid=(depth, n // tn),
        in_specs=[pl.BlockSpec((COND_ROWS, d), lambda i, j: (0, 0)),
                  pl.BlockSpec((1, d, tn), lambda i, j: (i, 0, j)),
                  pl.BlockSpec((1, 1, tn), lambda i, j: (i, 0, j))],
        out_specs=pl.BlockSpec((1, COND_ROWS, tn), lambda i, j: (i, 0, j)),
        out_shape=jax.ShapeDtypeStruct((depth, COND_ROWS, n), f32),
        compiler_params=_cparams(("arbitrary", "arbitrary")),
        name="adaln",
    )(cond, ada_w, ada_b.reshape(depth, 1, n))


def _norm_mod(x, g, shift, scale):
    ms = jnp.mean(x * x, axis=-1, keepdims=True)
    return (x * lax.rsqrt(ms + EPS)) * (g * (1.0 + scale)) + shift


def _rope(p, cos, sin_signed, first_half):
    rot = jnp.where(first_half, pltpu.roll(p, LANES - 16, axis=1), pltpu.roll(p, 16, axis=1))
    return p * cos + rot * sin_signed


def _inproj_kernel(x_ref, mod_ref, g_ref, w_ref, cos_ref, sin_ref, q_ref, k_ref, v_ref):
    qw = N_HEADS * HEAD_DIM
    tm = x_ref.shape[1]
    for r in range(0, tm, tm // 2):
        rs = slice(r, r + tm // 2)
        h = _norm_mod(x_ref[0, rs, :], g_ref[...], mod_ref[0, 0:1, :], mod_ref[0, 1:2, :])
        p = jnp.dot(h.astype(bf16), w_ref[...], preferred_element_type=f32)
        cos = cos_ref[rs, :]
        sin = sin_ref[rs, :]
        lane = lax.broadcasted_iota(i32, cos.shape, 1)
        first_half = (lane % 32) < 16
        for c in range(qw // LANES):
            qc = p[:, c * LANES:(c + 1) * LANES]
            q_ref[0, rs, c * LANES:(c + 1) * LANES] = (
                _rope(qc, cos, sin, first_half) * (LOG2E * HEAD_DIM ** -0.5)).astype(bf16)
        k_ref[0, rs, :] = _rope(p[:, qw:qw + LANES], cos, sin, first_half).astype(bf16)
        v_ref[0, rs, :] = p[:, qw + LANES:].astype(bf16)


def _inproj(x, mod, g, w, cos_t, sin_t):
    b, l, d = x.shape
    tm = PROJ_TILE
    n = w.shape[1]
    return pl.pallas_call(
        _inproj_kernel,
        grid=(b, l // tm),
        in_specs=[pl.BlockSpec((1, tm, d), lambda i, j: (i, j, 0)),
                  pl.BlockSpec((1, 6, d), lambda i, j: (i, 0, 0)),
                  pl.BlockSpec((1, d), lambda i, j: (0, 0)),
                  pl.BlockSpec((d, n), lambda i, j: (0, 0)),
                  pl.BlockSpec((tm, LANES), lambda i, j: (j, 0)),
                  pl.BlockSpec((tm, LANES), lambda i, j: (j, 0))],
        out_specs=[pl.BlockSpec((1, tm, N_HEADS * HEAD_DIM), lambda i, j: (i, j, 0)),
                   pl.BlockSpec((1, tm, LANES), lambda i, j: (i, j, 0)),
                   pl.BlockSpec((1, tm, LANES), lambda i, j: (i, j, 0))],
        out_shape=[jax.ShapeDtypeStruct((b, l, N_HEADS * HEAD_DIM), bf16),
                   jax.ShapeDtypeStruct((b, l, LANES), bf16),
                   jax.ShapeDtypeStruct((b, l, LANES), bf16)],
        compiler_params=_cparams(("parallel", "parallel")),
        name="inproj",
    )(x, mod, g, w, cos_t, sin_t)


def _ctxkv_kernel(x_ref, mod_ref, g_ref, w_ref, k_ref, v_ref):
    h = _norm_mod(x_ref[0], g_ref[...], mod_ref[0, 0:1, :], mod_ref[0, 1:2, :])
    p = jnp.dot(h.astype(bf16), w_ref[...], preferred_element_type=f32)
    k_ref[0] = p[:, :LANES].astype(bf16)
    v_ref[0] = p[:, LANES:].astype(bf16)


def _ctxkv(ctx, mod, g, w_kv, ctx_row):
    b, c, d = ctx.shape
    return pl.pallas_call(
        _ctxkv_kernel,
        grid=(b,),
        in_specs=[pl.BlockSpec((1, c, d), lambda i: (i, 0, 0)),
                  pl.BlockSpec((1, 6, d), lambda i: (ctx_row, 0, 0)),
                  pl.BlockSpec((1, d), lambda i: (0, 0)),
                  pl.BlockSpec((d, 2 * LANES), lambda i: (0, 0))],
        out_specs=[pl.BlockSpec((1, c, LANES), lambda i: (i, 0, 0)),
                   pl.BlockSpec((1, c, LANES), lambda i: (i, 0, 0))],
        out_shape=[jax.ShapeDtypeStruct((b, c, LANES), bf16),
                   jax.ShapeDtypeStruct((b, c, LANES), bf16)],
        compiler_params=_cparams(("parallel",)),
        name="ctxkv",
    )(ctx, mod, g, w_kv)


def _attn_kernel(seq_len, q_ref, kp_ref, km_ref, kn_ref, vp_ref, vm_ref, vn_ref,
                 ck_ref, cv_ref, sink_ref, o_ref, kext, vext):
    j = pl.program_id(1)
    tq = ATT_TILE
    blk = ATT_BLOCK
    kext[0:blk] = kp_ref[0]
    kext[blk:blk + tq] = km_ref[0]
    kext[blk + tq:] = kn_ref[0]
    vext[:, LANES:] = jnp.ones((tq + 2 * blk, LANES), bf16)
    vext[0:blk, :LANES] = vp_ref[0]
    vext[blk:blk + tq, :LANES] = vm_ref[0]
    vext[blk + tq:, :LANES] = vn_ref[0]
    n_ctx = ck_ref.shape[1]
    nk = n_ctx + 3 * blk
    n_chunks = (N_HEADS * HEAD_DIM) // LANES
    rows = n_chunks * blk
    half = HEAD_DIM
    klow = lax.broadcasted_iota(i32, (1, LANES), 1) < half
    vlane = lax.broadcasted_iota(i32, (1, 2 * LANES), 1)
    vlow = (vlane < half) | ((vlane >= LANES) & (vlane < LANES + half))
    zero = jnp.zeros((), bf16)
    ck = ck_ref[0]
    cvx = jnp.concatenate([cv_ref[0], jnp.ones((n_ctx, LANES), bf16)], axis=1)
    ck_lo, ck_hi = jnp.where(klow, ck, zero), jnp.where(klow, zero, ck)
    cv_lo, cv_hi = jnp.where(vlow, cvx, zero), jnp.where(vlow, zero, cvx)
    sink2 = sink_ref[...]
    sink_lo, sink_hi = sink2[:, 0:1], sink2[:, half:half + 1]
    low = lax.broadcasted_iota(i32, (rows, LANES), 1) < half
    qi = lax.broadcasted_iota(i32, (rows, 3 * blk), 0) % blk
    pk = lax.broadcasted_iota(i32, (rows, 3 * blk), 1)
    band_bias = jnp.where(jnp.abs(pk - blk - qi) <= WINDOW, 0.0, NEG_INF).astype(f32)
    pcol = lax.broadcasted_iota(i32, (1, 3 * blk), 1)
    nt = (((1,), (1,)), ((), ()))

    def row_max(s):
        blocks = [s[:, i:i + LANES] for i in range(0, s.shape[1], LANES)]
        return jnp.max(functools.reduce(jnp.maximum, blocks), axis=1, keepdims=True)

    def sub(s, carry):
        r0 = pl.multiple_of(s * blk, blk)
        qs = q_ref[0, pl.ds(r0, blk), :]
        lhs = jnp.concatenate([qs[:, c * LANES:(c + 1) * LANES] for c in range(n_chunks)], axis=0)
        kl = kext[pl.ds(r0, 3 * blk), :]
        vl = vext[pl.ds(r0, 3 * blk), :]
        kbd = jnp.concatenate([ck_lo, jnp.where(klow, kl, zero),
                               ck_hi, jnp.where(klow, zero, kl)], axis=0)
        vbd = jnp.concatenate([cv_lo, jnp.where(vlow, vl, zero),
                               cv_hi, jnp.where(vlow, zero, vl)], axis=0)
        kpos = j * tq + r0 - blk + pcol
        bias = band_bias + jnp.where((kpos >= 0) & (kpos < seq_len), 0.0, NEG_INF).astype(f32)
        sc = lax.dot_general(lhs, kbd, nt, preferred_element_type=f32)
        s_lo = jnp.concatenate([sc[:, :n_ctx], sc[:, n_ctx:nk] + bias], axis=1)
        s_hi = jnp.concatenate([sc[:, nk:nk + n_ctx], sc[:, nk + n_ctx:] + bias], axis=1)
        m_lo = jnp.maximum(row_max(s_lo), sink_lo)
        m_hi = jnp.maximum(row_max(s_hi), sink_hi)
        e = jnp.concatenate([jnp.exp2(s_lo - m_lo), jnp.exp2(s_hi - m_hi)], axis=1).astype(bf16)
        ov = jnp.dot(e, vbd, preferred_element_type=f32)
        den = ov[:, LANES:] + jnp.exp2(sink2 - jnp.where(low, m_lo, m_hi))
        o = (ov[:, :LANES] / den).astype(bf16)
        for c in range(n_chunks):
            o_ref[0, pl.ds(r0, blk), c * LANES:(c + 1) * LANES] = o[c * blk:(c + 1) * blk]
        return carry

    lax.fori_loop(0, tq // blk, sub, 0, unroll=8)


def _attention(q, k, v, ck, cv, sinkcol):
    b, l, qw = q.shape
    c = ck.shape[1]
    tq = ATT_TILE
    r = tq // ATT_BLOCK
    nb = l // ATT_BLOCK
    prev = pl.BlockSpec((1, ATT_BLOCK, LANES), lambda i, j: (i, jnp.maximum(j * r - 1, 0), 0))
    main = pl.BlockSpec((1, tq, LANES), lambda i, j: (i, j, 0))
    nxt = pl.BlockSpec((1, ATT_BLOCK, LANES), lambda i, j: (i, jnp.minimum(j * r + r, nb - 1), 0))
    cspec = pl.BlockSpec((1, c, LANES), lambda i, j: (i, 0, 0))
    return pl.pallas_call(
        functools.partial(_attn_kernel, l),
        grid=(b, l // tq),
        in_specs=[pl.BlockSpec((1, tq, qw), lambda i, j: (i, j, 0)),
                  prev, main, nxt, prev, main, nxt, cspec, cspec,
                  pl.BlockSpec(sinkcol.shape, lambda i, j: (0, 0))],
        out_specs=pl.BlockSpec((1, tq, qw), lambda i, j: (i, j, 0)),
        out_shape=jax.ShapeDtypeStruct((b, l, qw), bf16),
        scratch_shapes=[pltpu.VMEM((tq + 2 * ATT_BLOCK, LANES), bf16),
                        pltpu.VMEM((tq + 2 * ATT_BLOCK, 2 * LANES), bf16)],
        compiler_params=_cparams(("parallel", "parallel")),
        name="attention",
    )(q, k, k, k, v, v, v, ck, cv, sinkcol)


def _pack_pair(lo, hi):
    lo = lax.bitcast_convert_type(lo.astype(bf16).astype(f32), u32)
    hi = lax.bitcast_convert_type(hi.astype(bf16).astype(f32), u32)
    return (lo >> 16) | (hi & jnp.uint32(0xFFFF0000))


def _fourier1_kernel(x_ref, mod_ref, g_ref, w_ref, cs_ref, m_ref, ct_ref, st_ref, z_ref, ab_ref):
    n1 = x_ref.shape[1]
    nt = x_ref.shape[2]
    x = x_ref[0].reshape(n1 * nt, x_ref.shape[3])
    h = _norm_mod(x, g_ref[...], mod_ref[0, 0:1, :], mod_ref[0, 1:2, :]).astype(bf16)
    p = jnp.dot(h, w_ref[...], preferred_element_type=f32)
    for g in range(FOURIER_GROUPS):
        ug = p[:, g * LANES:(g + 1) * LANES].astype(bf16)
        ab = jnp.dot(ug, cs_ref[...], preferred_element_type=f32)
        ab_ref[0] = ab[:, :LANES]
        ab_ref[1] = ab[:, LANES:]
        for t0 in range(0, nt, 2):
            stack = jnp.concatenate(
                [jnp.concatenate([ab_ref[0, pl.ds(t, n1, stride=nt), :],
                                  ab_ref[1, pl.ds(t, n1, stride=nt), :]], axis=0)
                 for t in (t0, t0 + 1)], axis=1).astype(bf16)
            z2 = jnp.dot(m_ref[...], stack, preferred_element_type=f32)
            for t in (t0, t0 + 1):
                z = z2[:, (t - t0) * LANES:(t - t0 + 1) * LANES]
                zr, zn = z[:n1], z[n1:]
                ct, st = ct_ref[t], st_ref[t]
                z_ref[0, g, t] = _pack_pair(ct * zr - st * zn, ct * zn + st * zr)


def _fourier2_kernel(scale, z_ref, m_ref, o_ref, zbuf, ybuf):
    _, grp, n2, tk, w = z_ref.shape
    for gp in range(grp // 2):
        zbuf[0] = z_ref[0, 2 * gp].reshape(n2 * tk, w)
        zbuf[1] = z_ref[0, 2 * gp + 1].reshape(n2 * tk, w)
        for j in range(tk):
            ys = []
            for h in range(2):
                zp = zbuf[h, pl.ds(j, n2, stride=tk), :]
                zr = lax.bitcast_convert_type(zp << 16, f32).astype(bf16)
                zn = lax.bitcast_convert_type(zp & jnp.uint32(0xFFFF0000), f32).astype(bf16)
                ys.append(jnp.dot(m_ref[...], jnp.concatenate([zr, zn], axis=0),
                                  preferred_element_type=f32) * scale)
            ybuf[pl.ds(j, n2, stride=tk), :] = _pack_pair(ys[0], ys[1])
        o_ref[0, gp] = ybuf[...].reshape(n2, tk, w)


def _fourier(x, mod, g, w_f, cs):
    b, l, d = x.shape
    n2 = DFT_INNER
    n1 = l // n2
    grp, w = FOURIER_GROUPS, FOURIER_GROUP_W
    t2 = SUBLANES
    k1 = jnp.arange(n1, dtype=i32)
    ang1 = ((k1[:, None] * k1[None, :]) % n1).astype(f32) * (2.0 * math.pi / n1)
    c1, s1 = jnp.cos(ang1), jnp.sin(ang1)
    m1 = jnp.concatenate([jnp.concatenate([c1, -s1], axis=1),
                          jnp.concatenate([s1, c1], axis=1)], axis=0).astype(bf16)
    l2 = jnp.arange(n2, dtype=i32)
    angt = ((l2[:, None] * k1[None, :]) % l).astype(f32) * (2.0 * math.pi / l)
    ct = jnp.broadcast_to(jnp.cos(angt)[:, :, None], (n2, n1, w))
    st = jnp.broadcast_to(jnp.sin(angt)[:, :, None], (n2, n1, w))
    ang2 = ((l2[:, None] * l2[None, :]) % n2).astype(f32) * (2.0 * math.pi / n2)
    m2 = jnp.concatenate([jnp.cos(ang2), -jnp.sin(ang2)], axis=1).astype(bf16)

    tspec = pl.BlockSpec((t2, n1, w), lambda t, i: (t, 0, 0))
    z = pl.pallas_call(
        _fourier1_kernel,
        grid=(n2 // t2, b),
        in_specs=[pl.BlockSpec((1, n1, t2, d), lambda t, i: (i, 0, t, 0)),
                  pl.BlockSpec((1, 6, d), lambda t, i: (i, 0, 0)),
                  pl.BlockSpec((1, d), lambda t, i: (0, 0)),
                  pl.BlockSpec(w_f.shape, lambda t, i: (0, 0)),
                  pl.BlockSpec(cs.shape, lambda t, i: (0, 0)),
                  pl.BlockSpec(m1.shape, lambda t, i: (0, 0)), tspec, tspec],
        out_specs=pl.BlockSpec((1, grp, t2, n1, w), lambda t, i: (i, 0, t, 0, 0)),
        out_shape=jax.ShapeDtypeStruct((b, grp, n2, n1, w), u32),
        scratch_shapes=[pltpu.VMEM((2, n1 * t2, w), f32)],
        compiler_params=_cparams(("parallel", "parallel")),
        name="fourier_outer",
    )(x.reshape(b, n1, n2, d), mod, g, w_f, cs, m1, ct, st)

    tk = 2 * SUBLANES
    y = pl.pallas_call(
        functools.partial(_fourier2_kernel, 1.0 / math.sqrt(l * w)),
        grid=(b, n1 // tk),
        in_specs=[pl.BlockSpec((1, grp, n2, tk, w), lambda i, t: (i, 0, 0, t, 0)),
                  pl.BlockSpec(m2.shape, lambda i, t: (0, 0))],
        out_specs=pl.BlockSpec((1, grp // 2, n2, tk, w), lambda i, t: (i, 0, 0, t, 0)),
        out_shape=jax.ShapeDtypeStruct((b, grp // 2, n2, n1, w), u32),
        scratch_shapes=[pltpu.VMEM((2, n2 * tk, w), u32), pltpu.VMEM((n2 * tk, w), u32)],
        compiler_params=_cparams(("parallel", "parallel")),
        name="fourier_inner",
    )(z, m2)
    return y.reshape(b, grp // 2, l, w)


def _first_max4(a):
    m = jnp.maximum(jnp.maximum(a[0], a[1]), jnp.maximum(a[2], a[3]))
    idx = jnp.where(a[0] == m, 0, jnp.where(a[1] == m, 1, jnp.where(a[2] == m, 2, 3)))
    return m, idx


def _pick4(vals, idx):
    return jnp.where(idx == 0, vals[0], jnp.where(idx == 1, vals[1],
                                                   jnp.where(idx == 2, vals[2], vals[3])))


def _route(f, rw_ref, rb_ref, tri_ref, base_ref, first_step, ri_ref, wc_ref, cnt_ref, meta_ref):
    rw2 = rw_ref[...]
    pieces = []
    for fp in (f if isinstance(f, (list, tuple)) else [f]):
        f_hi = fp.astype(bf16)
        f_lo = (fp - f_hi.astype(f32)).astype(bf16)
        part = jnp.dot(f_hi, rw2, preferred_element_type=f32)
        pieces.append(part[:, :LANES] + part[:, LANES:]
                      + jnp.dot(f_lo, rw2[:, :LANES], preferred_element_type=f32))
    logits = jnp.concatenate(pieces, axis=0)
    tm = logits.shape[0]
    sc = jax.nn.sigmoid(logits)
    st = sc.T
    bt = (sc + rb_ref[...]).T
    neg = jnp.full((1, tm), -jnp.inf, f32)
    gs = []
    for g in range(N_GROUPS):
        a = [bt[4 * g + i: 4 * g + i + 1] for i in range(4)]
        m1, i1 = _first_max4(a)
        rest = [jnp.where(i1 == i, neg, a[i]) for i in range(4)]
        m2, _ = _first_max4(rest)
        gs.append(m1 + m2)
    _, gsel = _first_max4(gs)
    a = [_pick4([bt[4 * g + i: 4 * g + i + 1] for g in range(N_GROUPS)], gsel) for i in range(4)]
    s = [_pick4([st[4 * g + i: 4 * g + i + 1] for g in range(N_GROUPS)], gsel) for i in range(4)]
    _, i1 = _first_max4(a)
    rest = [jnp.where(i1 == i, neg, a[i]) for i in range(4)]
    _, i2 = _first_max4(rest)
    w1 = _pick4(s, i1)
    w2 = _pick4(s, i2)
    tot = w1 + w2
    w1 = w1 / tot
    w2 = w2 / tot
    e0 = gsel * EXPERTS_PER_GROUP + i1
    e1 = gsel * EXPERTS_PER_GROUP + i2

    @pl.when(first_step)
    def _():
        base_ref[...] = jnp.zeros_like(base_ref)

    td = DISPATCH_TILE
    eid = lax.broadcasted_iota(i32, (N_EXPERTS, tm), 0)
    oh0 = (eid == e0).astype(f32)
    oh1 = (eid == e1).astype(f32)
    oh = oh0 + oh1
    ohb = oh.astype(bf16)
    before = jnp.concatenate([jnp.dot(ohb[:, s:s + td], tri_ref[...], preferred_element_type=f32)
                              for s in range(0, tm, td)], axis=1)
    lane_tile = lax.broadcasted_iota(i32, (N_EXPERTS, tm), 1) // td
    ei = lax.broadcasted_iota(i32, (N_EXPERTS, N_EXPERTS), 0)
    ej = lax.broadcasted_iota(i32, (N_EXPERTS, N_EXPERTS), 1)
    strict_lower = (ej < ei).astype(f32)
    run_start = jnp.zeros((N_EXPERTS, tm), f32)
    goff = base_ref[...]
    for s in range(tm // td):
        cnt_s = jnp.sum(oh[:, s * td:(s + 1) * td], axis=1, keepdims=True)
        pad_s = jnp.floor((cnt_s + 7.0) * 0.125) * 8.0
        pad_b = jnp.broadcast_to(pad_s, (N_EXPERTS, LANES))
        start_b = jnp.dot(strict_lower, pad_b, precision=HIGHEST, preferred_element_type=f32)
        run_start = jnp.where(lane_tile == s, start_b[:, 0:1], run_start)
        meta_ref[s, 0] = start_b.astype(i32)
        meta_ref[s, 1] = pad_b.astype(i32)
        meta_ref[s, 2] = goff.astype(i32)
        goff = goff + pad_b
    base_ref[...] = goff
    cnt_ref[...] = goff
    pos = before + run_start
    lp0 = jnp.sum(oh0 * pos, axis=0, keepdims=True)
    lp1 = jnp.sum(oh1 * pos, axis=0, keepdims=True)
    zi = jnp.zeros((1, tm), i32)
    ri_ref[...] = jnp.concatenate(
        [lp0.astype(i32), lp1.astype(i32), e0, e1,
         lax.bitcast_convert_type(w1, i32), lax.bitcast_convert_type(w2, i32), zi, zi], axis=0)
    zf = jnp.zeros((LANES - 4, tm), f32)
    wc_ref[...] = jnp.concatenate([w1, w2, lp0, lp1, zf], axis=0).T


def _outproj_kernel(yf_ref, o_ref, x_ref, mod_ref, w_ref, g_ref, rw_ref, rb_ref, tri_ref,
                    x1_ref, f_ref, ri_ref, wc_ref, cnt_ref, meta_ref, base_ref):
    tm = x_ref.shape[1]
    fs = []
    for r in range(0, tm, tm // 2):
        rs = slice(r, r + tm // 2)
        groups = []
        for gp in range(FOURIER_GROUPS // 2):
            pair = yf_ref[0, gp, rs, :]
            groups.append(lax.bitcast_convert_type(pair << 16, f32).astype(bf16))
            groups.append(lax.bitcast_convert_type(pair & jnp.uint32(0xFFFF0000), f32).astype(bf16))
        mix = jnp.concatenate(groups + [o_ref[0, rs, :]], axis=1)
        y = jnp.dot(mix, w_ref[...], preferred_element_type=f32)
        x1 = x_ref[0, rs, :] + mod_ref[0, 2:3, :] * y
        x1_ref[0, rs, :] = x1
        f = _norm_mod(x1, g_ref[...], mod_ref[0, 3:4, :], mod_ref[0, 4:5, :])
        f_ref[0, rs, :] = f.astype(bf16)
        fs.append(f)
    first = (pl.program_id(0) == 0) & (pl.program_id(1) == 0)
    _route(fs, rw_ref, rb_ref, tri_ref, base_ref, first, ri_ref, wc_ref, cnt_ref, meta_ref)


def _before_in_tile():
    tpos = jnp.arange(DISPATCH_TILE)
    return (tpos[:, None] < tpos[None, :]).astype(bf16)


def _route_specs(b, l, tm):
    nl = l // tm
    rw = lambda d: pl.BlockSpec((d, 2 * LANES), lambda i, j: (0, 0))
    rb = pl.BlockSpec((1, LANES), lambda i, j: (0, 0))
    tri = pl.BlockSpec((DISPATCH_TILE, DISPATCH_TILE), lambda i, j: (0, 0))
    ns = tm // DISPATCH_TILE
    out_specs = [pl.BlockSpec((8, tm), lambda i, j: (0, i * nl + j)),
                 pl.BlockSpec((tm, LANES), lambda i, j: (i * nl + j, 0)),
                 pl.BlockSpec((N_EXPERTS, LANES), lambda i, j: (0, 0)),
                 pl.BlockSpec((ns, 3, N_EXPERTS, LANES), lambda i, j: (i * nl + j, 0, 0, 0))]
    out_shape = [jax.ShapeDtypeStruct((8, b * l), i32),
                 jax.ShapeDtypeStruct((b * l, LANES), f32),
                 jax.ShapeDtypeStruct((N_EXPERTS, LANES), f32),
                 jax.ShapeDtypeStruct((b * l // DISPATCH_TILE, 3, N_EXPERTS, LANES), i32)]
    return rw, rb, tri, out_specs, out_shape


def _outproj(yf, o, x, mod, w, g, rw, rb):
    b, l, d = x.shape
    tm = PROJ_TILE
    tri = _before_in_tile()
    rws, rbs, tris, r_specs, r_shapes = _route_specs(b, l, tm)
    row = pl.BlockSpec((1, tm, d), lambda i, j: (i, j, 0))
    return pl.pallas_call(
        _outproj_kernel,
        grid=(b, l // tm),
        in_specs=[pl.BlockSpec((1, yf.shape[1], tm, LANES), lambda i, j: (i, 0, j, 0)),
                  pl.BlockSpec((1, tm, o.shape[2]), lambda i, j: (i, j, 0)),
                  row,
                  pl.BlockSpec((1, 6, d), lambda i, j: (i, 0, 0)),
                  pl.BlockSpec(w.shape, lambda i, j: (0, 0)),
                  pl.BlockSpec((1, d), lambda i, j: (0, 0)),
                  rws(d), rbs, tris],
        out_specs=[row, row] + r_specs,
        out_shape=[jax.ShapeDtypeStruct((b, l, d), f32),
                   jax.ShapeDtypeStruct((b, l, d), bf16)] + r_shapes,
        scratch_shapes=[pltpu.VMEM((N_EXPERTS, LANES), f32)],
        compiler_params=_cparams(("arbitrary", "arbitrary")),
        name="outproj_router",
    )(yf, o, x, mod, w, g, rw, rb, tri)


def _conv_kernel(seq_len, up_ref, um_ref, un_ref, x_ref, mod_ref, dw_ref, db_ref, lg_ref, lb_ref,
                 w_ref, pb_ref, g_ref, rw_ref, rb_ref, tri_ref,
                 x1_ref, f_ref, ri_ref, wc_ref, cnt_ref, meta_ref, base_ref, ext, conv_out):
    j = pl.program_id(1)
    tm = um_ref.shape[1]
    hl = CONV_HALO
    half = CONV_W // 2
    prev = jnp.where(j > 0, up_ref[0], jnp.zeros_like(up_ref[0]))
    nxt = jnp.where((j + 1) * tm < seq_len, un_ref[0], jnp.zeros_like(un_ref[0]))
    for c in range(ext.shape[0]):
        lanes_c = slice(c * LANES, (c + 1) * LANES)
        ext[c, 0:hl] = prev[:, lanes_c]
        ext[c, hl:hl + tm] = um_ref[0, :, lanes_c]
        ext[c, hl + tm:] = nxt[:, lanes_c]
    base = hl - half
    span = (CONV_W - 1) // SUBLANES * SUBLANES
    rows = CONV_ROWS

    def lane_chunk(c, carry):
        lanes = pl.ds(pl.multiple_of(c * LANES, LANES), LANES)
        for r in range(0, tm, rows):
            part = jnp.broadcast_to(db_ref[:, lanes], (rows, LANES))
            for phase in range(SUBLANES):
                win = ext[c, base + phase + r: base + phase + r + rows + span, :]
                same = None
                for t in range(phase, CONV_W, SUBLANES):
                    term = win[t - phase: t - phase + rows, :] * dw_ref[t:t + 1, lanes]
                    same = term if same is None else same + term
                part = part + same
            conv_out[r:r + rows, lanes] = part
        return carry

    lax.fori_loop(0, um_ref.shape[2] // LANES, lane_chunk, 0)
    fs = []
    for r in range(0, tm, tm // 2):
        rs = slice(r, r + tm // 2)
        acc = conv_out[rs, :]
        mu = jnp.mean(acc, axis=-1, keepdims=True)
        cen = acc - mu
        var = jnp.mean(cen * cen, axis=-1, keepdims=True)
        ln = cen * lax.rsqrt(var + EPS) * lg_ref[...] + lb_ref[...]
        act = ln * jax.nn.sigmoid(ln)
        y = jnp.dot(act.astype(bf16), w_ref[...], preferred_element_type=f32) + pb_ref[...]
        x1 = x_ref[0, rs, :] + mod_ref[0, 2:3, :] * y
        x1_ref[0, rs, :] = x1
        f = _norm_mod(x1, g_ref[...], mod_ref[0, 3:4, :], mod_ref[0, 4:5, :])
        f_ref[0, rs, :] = f.astype(bf16)
        fs.append(f)
    first = (pl.program_id(0) == 0) & (j == 0)
    _route(fs, rw_ref, rb_ref, tri_ref, base_ref, first, ri_ref, wc_ref, cnt_ref, meta_ref)


def _conv(u, x, mod, dw_w, dw_b, ln_g, ln_b, pw2_w, pw2_b, g, rw, rb):
    b, l, d = x.shape
    tm = TOKEN_TILE
    tri = _before_in_tile()
    hl = CONV_HALO
    r = tm // hl
    nh = l // hl
    rws, rbs, tris, r_specs, r_shapes = _route_specs(b, l, tm)
    row = pl.BlockSpec((1, tm, d), lambda i, j: (i, j, 0))
    vec = pl.BlockSpec((1, d), lambda i, j: (0, 0))
    return pl.pallas_call(
        functools.partial(_conv_kernel, l),
        grid=(b, l // tm),
        in_specs=[pl.BlockSpec((1, hl, d), lambda i, j: (i, jnp.maximum(j * r - 1, 0), 0)),
                  row,
                  pl.BlockSpec((1, hl, d), lambda i, j: (i, jnp.minimum(j * r + r, nh - 1), 0)),
                  row,
                  pl.BlockSpec((1, 6, d), lambda i, j: (i, 0, 0)),
                  pl.BlockSpec(dw_w.shape, lambda i, j: (0, 0)),
                  vec, vec, vec,
                  pl.BlockSpec(pw2_w.shape, lambda i, j: (0, 0)),
                  vec, vec, rws(d), rbs, tris],
        out_specs=[row, row] + r_specs,
        out_shape=[jax.ShapeDtypeStruct((b, l, d), f32),
                   jax.ShapeDtypeStruct((b, l, d), bf16)] + r_shapes,
        scratch_shapes=[pltpu.VMEM((N_EXPERTS, LANES), f32),
                        pltpu.VMEM((d // LANES, tm + 2 * hl, LANES), f32),
                        pltpu.VMEM((tm, d), f32)],
        compiler_params=_cparams(("arbitrary", "arbitrary")),
        name="conv_router",
    )(u, u, u, x, mod, dw_w, dw_b, ln_g, ln_b, pw2_w, pw2_b, g, rw, rb, tri)


def _pack_bf16_pairs(x):
    h = x.shape[1] // 2
    lo = lax.bitcast_convert_type(x[:, :h], u32)
    hi = lax.bitcast_convert_type(x[:, h:], u32)
    return (lo >> 16) | (hi & jnp.uint32(0xFFFF0000))


def _unpack_bf16_pairs(u):
    lo = lax.bitcast_convert_type(u << 16, f32)
    hi = lax.bitcast_convert_type(u & jnp.uint32(0xFFFF0000), f32)
    return jnp.concatenate([lo, hi], axis=1).astype(bf16)


def _run_copies(meta, tile, local_ref, hbm_ref, sem, to_hbm):
    start_ref, size_ref, dst_ref = meta
    for e in range(N_EXPERTS):
        k = tile * N_EXPERTS + e
        size = pl.multiple_of(size_ref[k], RUN_ALIGN)

        @pl.when(size > 0)
        def _():
            loc = local_ref.at[pl.ds(pl.multiple_of(start_ref[k], RUN_ALIGN), size)]
            glob = hbm_ref.at[pl.ds(pl.multiple_of(dst_ref[k], RUN_ALIGN), size)]
            if to_hbm:
                pltpu.make_async_copy(loc, glob, sem).start()
            else:
                pltpu.make_async_copy(glob, loc, sem).start()


def _wait_rows(rows, local_ref, hbm_ref, sem):
    rows = pl.multiple_of(rows, RUN_ALIGN)

    @pl.when(rows > 0)
    def _():
        pltpu.make_async_copy(local_ref.at[pl.ds(0, rows)], hbm_ref.at[pl.ds(0, rows)], sem).wait()


def _dispatch_kernel(start_ref, size_ref, dst_ref, tot_ref, tail_start_ref, tail_size_ref, nv_ref,
                     f_ref, lp_ref, xs_ref, loc, zbuf, sem, zsem):
    i = pl.program_id(0)
    n = pl.num_programs(0)
    slot = i % 2
    subs = loc.shape[1]
    rows = loc.shape[2]
    td = f_ref.shape[0] // subs
    meta = (start_ref, size_ref, dst_ref)

    def drain(step, which):
        for s in range(subs):
            _wait_rows(tot_ref[step * subs + s], loc.at[which, s], xs_ref, sem.at[which])

    @pl.when(i >= 2)
    def _():
        drain(i - 2, slot)

    r = lax.broadcasted_iota(i32, (rows, td), 0)
    for s in range(subs):
        cols = slice(s * td, (s + 1) * td)
        pick0 = r == lp_ref[0:1, cols]
        pick1 = r == lp_ref[1:2, cols]
        onehot = (pick0 | pick1).astype(bf16)
        sorted_rows = jnp.dot(onehot, f_ref[cols, :], preferred_element_type=f32)
        half = sorted_rows.shape[1] // 2
        loc[slot, s, :, :half] = _pack_bf16_pairs(sorted_rows)
        w0 = lax.bitcast_convert_type(lp_ref[4:5, cols], f32)
        w1 = lax.bitcast_convert_type(lp_ref[5:6, cols], f32)
        row_w = jnp.sum(jnp.where(pick0, w0, 0.0) + jnp.where(pick1, w1, 0.0), axis=1, keepdims=True)
        loc[slot, s, :, half:] = jnp.broadcast_to(lax.bitcast_convert_type(row_w, u32), (rows, LANES))
        _run_copies(meta, i * subs + s, loc.at[slot, s], xs_ref, sem.at[slot], to_hbm=True)

    @pl.when(i == n - 1)
    def _():
        zbuf[...] = jnp.zeros_like(zbuf)
        total = 0
        for e in range(N_EXPERTS):
            size = pl.multiple_of(tail_size_ref[e], RUN_ALIGN)
            total = total + size

            @pl.when(size > 0)
            def _():
                pltpu.make_async_copy(
                    zbuf.at[pl.ds(0, size)],
                    xs_ref.at[pl.ds(pl.multiple_of(tail_start_ref[e], RUN_ALIGN), size)], zsem).start()

        _wait_rows(total, zbuf, xs_ref, zsem)

        def zero_block(k, c):
            pltpu.make_async_copy(zbuf, xs_ref.at[pl.ds(pl.multiple_of(k * zbuf.shape[0], RUN_ALIGN),
                                                        zbuf.shape[0])], zsem).start()
            return c

        def wait_block(k, c):
            pltpu.make_async_copy(zbuf, xs_ref.at[pl.ds(0, zbuf.shape[0])], zsem).wait()
            return c

        n_blocks = xs_ref.shape[0] // zbuf.shape[0]
        lax.fori_loop(nv_ref[0], n_blocks, zero_block, 0)
        lax.fori_loop(nv_ref[0], n_blocks, wait_block, 0)
        drain(i, slot)

        @pl.when(i >= 1)
        def _():
            drain(i - 1, 1 - slot)


def _dispatch(tables, f2, ri, n_slots):
    t, d = f2.shape
    subs = DISPATCH_SUBTILES
    tm = DISPATCH_TILE * subs
    return pl.pallas_call(
        _dispatch_kernel,
        grid_spec=pltpu.PrefetchScalarGridSpec(
            num_scalar_prefetch=7,
            grid=(t // tm,),
            in_specs=[pl.BlockSpec((tm, d), lambda i, *_: (i, 0)),
                      pl.BlockSpec((8, tm), lambda i, *_: (0, i))],
            out_specs=pl.BlockSpec(memory_space=pl.ANY),
            scratch_shapes=[pltpu.VMEM((2, subs, LOCAL_ROWS, d // 2 + LANES), u32),
                            pltpu.VMEM((EXPERT_ROWS, d // 2 + LANES), u32),
                            pltpu.SemaphoreType.DMA((2,)), pltpu.SemaphoreType.DMA(())]),
        out_shape=jax.ShapeDtypeStruct((n_slots, d // 2 + LANES), u32),
        compiler_params=_cparams(("arbitrary",)),
        name="moe_dispatch",
    )(*tables, f2, ri)


def _expert_kernel(be_ref, second_ref, nv_ref, x_ref, wg1, wu1, wd1, wg2, wu2, wd2, y_ref, *w_bf):
    del second_ref
    i = pl.program_id(0)
    tb = y_ref.shape[0] // 2
    half = y_ref.shape[1]
    e1, e2 = be_ref[2 * i], be_ref[2 * i + 1]
    first, second = w_bf[:3], w_bf[3:]

    @pl.when(jnp.logical_or(i == 0, e1 != be_ref[jnp.maximum(2 * i - 2, 0)]))
    def _():
        for dst, src in zip(first, (wg1, wu1, wd1)):
            dst[...] = src[0, 0].astype(bf16)

    @pl.when(e2 != e1)
    def _():
        for dst, src in zip(second, (wg2, wu2, wd2)):
            dst[...] = src[0, 0].astype(bf16)

    def block(k, weights):
        wgb, wub, wdb = weights
        rows = slice(k * tb, (k + 1) * tb)
        xb = _unpack_bf16_pairs(x_ref[rows, :half])
        row_w = lax.bitcast_convert_type(x_ref[rows, half:], f32)
        gate = jnp.dot(xb, wgb[...], preferred_element_type=f32)
        up = jnp.dot(xb, wub[...], preferred_element_type=f32)
        hid = (gate * jax.nn.sigmoid(gate) * up).astype(bf16)
        y = jnp.dot(hid, wdb[...], preferred_element_type=f32)
        y = jnp.concatenate([y[:, c:c + LANES] * row_w for c in range(0, y.shape[1], LANES)], axis=1)
        y_ref[rows, :] = _pack_bf16_pairs(y.astype(bf16).astype(f32))

    used = 2 * i < nv_ref[0]

    @pl.when(used & (e2 == e1))
    def _():
        block(0, first)
        block(1, first)

    @pl.when(used & (e2 != e1))
    def _():
        block(0, first)
        block(1, second)

    @pl.when(jnp.logical_not(used))
    def _():
        y_ref[...] = jnp.zeros_like(y_ref)


def _experts(block_e, n_valid, xs, w_gate, w_up, w_down, layer):
    ns, xw = xs.shape
    nb = EXPERT_BLOCKS_PER_STEP
    tb = EXPERT_ROWS
    d, ff = w_gate.shape[2:]
    dh = d // 2
    e1, e2 = block_e[0::2], block_e[1::2]
    second = lax.cummax(jnp.where(e2 != e1, e2, 0))
    map1 = lambda i, be, sec, nv: (layer, be[2 * i], 0, 0)
    map2 = lambda i, be, sec, nv: (layer, sec[i], 0, 0)
    w_specs = [pl.BlockSpec((1, 1, d, ff), m) for m in (map1, map1)] + [pl.BlockSpec((1, 1, ff, d), map1)]
    w_specs += [pl.BlockSpec((1, 1, d, ff), m) for m in (map2, map2)] + [pl.BlockSpec((1, 1, ff, d), map2)]
    w_scratch = 2 * [pltpu.VMEM((d, ff), bf16), pltpu.VMEM((d, ff), bf16), pltpu.VMEM((ff, d), bf16)]
    return pl.pallas_call(
        _expert_kernel,
        grid_spec=pltpu.PrefetchScalarGridSpec(
            num_scalar_prefetch=3,
            grid=(ns // (nb * tb),),
            in_specs=[pl.BlockSpec((nb * tb, xw), lambda i, *_: (i, 0))] + w_specs,
            out_specs=pl.BlockSpec((nb * tb, dh), lambda i, *_: (i, 0)),
            scratch_shapes=w_scratch),
        out_shape=jax.ShapeDtypeStruct((ns, dh), u32),
        compiler_params=_cparams(("arbitrary",)),
        name="moe_experts",
    )(block_e, second, n_valid, xs, w_gate, w_up, w_down, w_gate, w_up, w_down)


def _combine_kernel(final, start_ref, size_ref, dst_ref, tot_ref, ys_ref, wc_ref, x_ref, mod_ref,
                    g_ref, *rest):
    if final:
        o_ref, loc, sem, xbuf, xsem = rest
    else:
        nmod_ref, w_ref, b_ref, o_ref, u_ref, loc, sem, xbuf, xsem = rest
    i = pl.program_id(0)
    n = pl.num_programs(0)
    slot = i % 2
    subs = loc.shape[1]
    rows = loc.shape[2]
    tm = o_ref.shape[0]
    td = tm // subs

    def x_copy(step):
        return pltpu.make_async_copy(x_ref.at[pl.ds(pl.multiple_of(step * tm, tm), tm)],
                                     xbuf.at[step % 3], xsem.at[step % 3])

    @pl.when(i == 0)
    def _():
        x_copy(0).start()

        @pl.when(n > 1)
        def _():
            x_copy(1).start()

    @pl.when(i + 2 < n)
    def _():
        x_copy(i + 2).start()
    meta = (start_ref, size_ref, dst_ref)

    def fetch(step, which):
        for s in range(subs):
            _run_copies(meta, step * subs + s, loc.at[which, s], ys_ref, sem.at[which, s],
                        to_hbm=False)

    @pl.when(i == 0)
    def _():
        loc[...] = jnp.zeros_like(loc)
        fetch(i, slot)

    @pl.when(i + 1 < n)
    def _():
        fetch(i + 1, 1 - slot)

    c = lax.broadcasted_iota(i32, (td, rows), 1)
    parts = []
    for s in range(subs):
        _wait_rows(tot_ref[i * subs + s], loc.at[slot, s], ys_ref, sem.at[slot, s])
        wc = wc_ref[s * td:(s + 1) * td, :]
        sel = ((c == wc[:, 2:3].astype(i32)) | (c == wc[:, 3:4].astype(i32))).astype(bf16)
        parts.append(jnp.dot(sel, _unpack_bf16_pairs(loc[slot, s]), preferred_element_type=f32))
    x_copy(i).wait()
    xo = xbuf[i % 3] + mod_ref[0, 5:6, :] * jnp.concatenate(parts, axis=0)
    if final:
        ms = jnp.mean(xo * xo, axis=-1, keepdims=True)
        o_ref[...] = xo * lax.rsqrt(ms + EPS) * g_ref[...]
    else:
        o_ref[...] = xo
        h = _norm_mod(xo, g_ref[...], nmod_ref[0, 0:1, :], nmod_ref[0, 1:2, :])
        p = jnp.dot(h.astype(bf16), w_ref[...], preferred_element_type=f32) + b_ref[...]
        ch = p.shape[1] // 2
        u_ref[...] = p[:, :ch] * jax.nn.sigmoid(p[:, ch:])


def _combine(tables, ys, wc, x, mod, g, glu=None):
    b, l, d = x.shape
    subs = DISPATCH_SUBTILES if glu is None else GLU_SUBTILES
    tm = DISPATCH_TILE * subs
    per_batch = l // tm
    const = lambda i, *_: (0, 0)
    tile = lambda cols: pl.BlockSpec((tm, cols), lambda i, *_: (i, 0))
    mod_spec = pl.BlockSpec((1, 6, d), lambda i, *_: (i // per_batch, 0, 0))
    in_specs = [pl.BlockSpec(memory_space=pl.ANY), tile(LANES), pl.BlockSpec(memory_space=pl.ANY), mod_spec,
                pl.BlockSpec((1, d), const)]
    args = [ys, wc, x.reshape(b * l, d), mod, g]
    out_specs = [tile(d)]
    out_shape = [jax.ShapeDtypeStruct((b * l, d), f32)]
    if glu is not None:
        nmod, w, bias = glu
        in_specs += [mod_spec, pl.BlockSpec(w.shape, const), pl.BlockSpec(bias.shape, const)]
        args += [nmod, w, bias]
        out_specs.append(tile(w.shape[1] // 2))
        out_shape.append(jax.ShapeDtypeStruct((b * l, w.shape[1] // 2), f32))
    outs = pl.pallas_call(
        functools.partial(_combine_kernel, glu is None),
        grid_spec=pltpu.PrefetchScalarGridSpec(
            num_scalar_prefetch=4,
            grid=(b * per_batch,),
            in_specs=in_specs,
            out_specs=out_specs,
            scratch_shapes=[pltpu.VMEM((2, subs, LOCAL_ROWS, d // 2), u32),
                            pltpu.SemaphoreType.DMA((2, subs)),
                            pltpu.VMEM((3, tm, d), f32), pltpu.SemaphoreType.DMA((3,))]),
        out_shape=out_shape,
        compiler_params=_cparams(("arbitrary",)),
        name="moe_combine",
    )(*tables, *args)
    return [o.reshape(b, l, -1) for o in outs]


def _moe(f, routed, x, mod, g, w_gate, w_up, w_down, layer, glu=None):
    ri, wc, cnt, meta = routed
    b, l, d = x.shape
    t = b * l
    tb = EXPERT_ROWS
    n_tiles = t // DISPATCH_TILE
    used = cnt[:, 0].astype(i32)
    region = (used + tb - 1) // tb * tb
    gend = jnp.cumsum(region)
    gstart = gend - region
    max_rows = 2 * t + n_tiles * N_EXPERTS * (RUN_ALIGN - 1) + N_EXPERTS * (tb - 1)
    n_blocks = -(-max_rows // (tb * EXPERT_BLOCKS_PER_STEP)) * EXPERT_BLOCKS_PER_STEP
    m = meta[:, :, :, 0]
    run_start = m[:, 0].reshape(-1)
    run_size = m[:, 1].reshape(-1)
    run_dst = (m[:, 2] + gstart[None, :]).reshape(-1)
    tile_rows = jnp.sum(m[:, 1], axis=1)
    block_row = jnp.arange(n_blocks, dtype=i32) * tb
    block_e = jnp.minimum(jnp.sum((block_row[:, None] >= gend[None, :]).astype(i32), axis=1),
                          N_EXPERTS - 1)
    n_valid = (gend[-1] // tb).reshape(1)
    xs = _dispatch((run_start, run_size, run_dst, tile_rows, gstart + used, region - used, n_valid),
                   f.reshape(t, d), ri, n_blocks * tb)
    ys = _experts(block_e, n_valid, xs, w_gate, w_up, w_down, layer)
    return _combine((run_start, run_size, run_dst, tile_rows), ys, wc, x, mod, g, glu)


def _rope_tables(l):
    lane = jnp.arange(LANES)
    dh = lane % HEAD_DIM
    inv = ROPE_THETA ** (-(dh % 16).astype(f32) / 16.0)
    sign = jnp.where((dh % 32) < 16, -1.0, 1.0).astype(f32)
    by_row = (dh // 32)[None, None, :] == 0
    ang_r = jnp.arange(l // GRID_W, dtype=f32)[:, None] * inv[None, :]
    ang_c = jnp.arange(GRID_W, dtype=f32)[:, None] * inv[None, :]
    cos = jnp.where(by_row, jnp.cos(ang_r)[:, None, :], jnp.cos(ang_c)[None, :, :])
    sin = jnp.where(by_row, jnp.sin(ang_r)[:, None, :], jnp.sin(ang_c)[None, :, :])
    return cos.reshape(l, LANES), (sin * sign[None, None, :]).reshape(l, LANES)


def kernel(x, c, ctx, c_ctx, ada_w, ada_b, norm_mix_g, norm_ffn_g, even_w_in, even_w_out, even_sink, conv_pw1_w, conv_pw1_b, conv_dw_w, conv_dw_b, conv_ln_g, conv_ln_b, conv_pw2_w, conv_pw2_b, router_w, router_b, moe_w_gate, moe_w_up, moe_w_down, final_norm_g):
    b, l, d = x.shape
    depth = ada_w.shape[0]
    assert depth == 2 and b < COND_ROWS
    ctx_row = b
    cond = jnp.zeros((COND_ROWS, d), f32).at[:b].set(c).at[ctx_row].set(c_ctx)
    mods = _adaln(cond, ada_w, ada_b).reshape(depth, COND_ROWS, 6, d)

    heads = jnp.arange(N_HEADS).reshape(N_KV_HEADS, N_HEADS // N_KV_HEADS).T.reshape(-1)
    qperm = (heads[:, None] * HEAD_DIM + jnp.arange(HEAD_DIM)[None, :]).reshape(-1)
    fw = FOURIER_GROUPS * FOURIER_GROUP_W
    qw = N_HEADS * HEAD_DIM
    w_in = even_w_in[0]
    w_in_p = jnp.concatenate([w_in[:, :fw], w_in[:, fw:fw + qw][:, qperm], w_in[:, fw + qw:]],
                             axis=1).astype(bf16)
    w_out = even_w_out[0]
    w_out_p = jnp.concatenate([w_out[:fw], w_out[fw:][qperm]], axis=0).astype(bf16)
    sink_pairs = (even_sink[0].astype(f32) * LOG2E).reshape(N_KV_HEADS, N_HEADS // N_KV_HEADS).T
    sinkcol = jnp.repeat(jnp.repeat(sink_pairs, HEAD_DIM, axis=1), ATT_BLOCK, axis=0)

    cidx = jnp.arange(FOURIER_GROUP_W, dtype=i32)
    angc = ((cidx[:, None] * cidx[None, :]) % FOURIER_GROUP_W).astype(f32) * (2.0 * math.pi / FOURIER_GROUP_W)
    cs = jnp.concatenate([jnp.cos(angc), jnp.sin(angc)], axis=1).astype(bf16)
    cos_t, sin_t = _rope_tables(l)

    rw32 = jnp.zeros((d, LANES), f32).at[:, :N_EXPERTS].set(router_w.astype(f32))
    rw_hi = rw32.astype(bf16)
    rw = jnp.concatenate([rw_hi, (rw32 - rw_hi.astype(f32)).astype(bf16)], axis=1)
    rb = jnp.zeros((1, LANES), f32).at[0, :N_EXPERTS].set(router_b)
    row = lambda v: v.reshape(1, -1)

    q, k, v = _inproj(x, mods[0], row(norm_mix_g[0]), w_in_p[:, fw:], cos_t, sin_t)
    ck, cv = _ctxkv(ctx, mods[0], row(norm_mix_g[0]), w_in_p[:, fw + qw:], ctx_row)
    yf = _fourier(x, mods[0], row(norm_mix_g[0]), w_in_p[:, :fw], cs)
    att = _attention(q, k, v, ck, cv, sinkcol)
    x1, f, *routed = _outproj(yf, att, x, mods[0], w_out_p, row(norm_ffn_g[0]), rw, rb)
    x2, u = _moe(f, routed, x1, mods[0], row(norm_mix_g[1]), moe_w_gate, moe_w_up, moe_w_down,
                 layer=0, glu=(mods[1], conv_pw1_w[0].astype(bf16), row(conv_pw1_b[0])))

    x3, f, *routed = _conv(u, x2, mods[1], conv_dw_w[0], row(conv_dw_b[0]), row(conv_ln_g[0]),
                           row(conv_ln_b[0]), conv_pw2_w[0].astype(bf16), row(conv_pw2_b[0]),
                           row(norm_ffn_g[1]), rw, rb)
    (out,) = _moe(f, routed, x3, mods[1], row(final_norm_g), moe_w_gate, moe_w_up, moe_w_down,
                  layer=1)
    return out
```
